```python
import jax, jax.numpy as jnp
from jax import lax
import numpy as np

D_MODEL = 1024
BATCH = 8
SEQ = 4096
DEPTH = 1

HEAD_DIM = 64
A_HEADS = D_MODEL // (2 * HEAD_DIM)
A_KV_HEADS = max(1, A_HEADS // 4)
A_WINDOW = 128
B_HEADS = D_MODEL // (2 * HEAD_DIM)
B_PATTERNS = ((128, 1), (512, 4), (2048, 16))
BLOCK = 128
ROPE_THETA = 10000.0
EPS = 1e-6
NEG = -1e30

A_WIDTH = A_HEADS * HEAD_DIM
A_KV_WIDTH = A_KV_HEADS * HEAD_DIM
B_WIDTH = B_HEADS * HEAD_DIM
MIX_WIDTH = A_WIDTH + B_WIDTH
IN_SPLITS = (A_WIDTH, A_KV_WIDTH, A_KV_WIDTH, A_WIDTH, B_WIDTH, B_WIDTH, B_WIDTH, B_WIDTH)
IN_WIDTH = sum(IN_SPLITS)

kernel_name = "hybrid_swa_sink_dilated_gated"


def rmsnorm(t, gain):
    tf = t.astype(jnp.float32)
    tf = tf * lax.rsqrt(jnp.mean(tf * tf, axis=-1, keepdims=True) + EPS)
    return (tf * gain.astype(jnp.float32)).astype(t.dtype)


def rope(t, pos):
    half = HEAD_DIM // 2
    inv = ROPE_THETA ** (-jnp.arange(half, dtype=jnp.float32) / half)
    ang = pos.astype(jnp.float32)[:, None] * inv[None, :]
    cos = jnp.cos(ang)[:, None, :]
    sin = jnp.sin(ang)[:, None, :]
    tf = t.astype(jnp.float32)
    t1, t2 = tf[..., :half], tf[..., half:]
    return jnp.concatenate([t1 * cos - t2 * sin, t2 * cos + t1 * sin], axis=-1).astype(t.dtype)


def banded_attention(q, k, v, max_dist, sinks=None):
    n, L, h, d = q.shape
    hkv = k.shape[2]
    g = h // hkv
    nb = -(-L // BLOCK)
    lp = nb * BLOCK
    pad = lp - L
    q = jnp.pad(q, ((0, 0), (0, pad), (0, 0), (0, 0)))
    k = jnp.pad(k, ((0, 0), (BLOCK, pad), (0, 0), (0, 0)))
    v = jnp.pad(v, ((0, 0), (BLOCK, pad), (0, 0), (0, 0)))
    qb = q.reshape(n, nb, BLOCK, hkv, g, d)
    kb = k.reshape(n, nb + 1, BLOCK, hkv, d)
    vb = v.reshape(n, nb + 1, BLOCK, hkv, d)
    kw = jnp.concatenate([kb[:, :-1], kb[:, 1:]], axis=2)
    vw = jnp.concatenate([vb[:, :-1], vb[:, 1:]], axis=2)
    s = jnp.einsum("nbqkgd,nbskd->nbkgqs", qb, kw,
                   preferred_element_type=jnp.float32) * (d ** -0.5)
    qi = jnp.arange(BLOCK)[:, None]
    sj = jnp.arange(2 * BLOCK)[None, :]
    dist = qi - sj + BLOCK
    key_pos = jnp.arange(nb)[:, None] * BLOCK - BLOCK + sj
    valid = ((dist >= 0) & (dist <= max_dist))[None] & (key_pos >= 0)[:, None, :]
    s = jnp.where(valid[None, :, None, None], s, NEG)
    m = s.max(axis=-1)
    if sinks is not None:
        sk = sinks.astype(jnp.float32).reshape(hkv, g)[None, None, :, :, None]
        m = jnp.maximum(m, sk)
    p = jnp.exp(s - m[..., None])
    l = p.sum(axis=-1)
    if sinks is not None:
        l = l + jnp.exp(sk - m)
    o = jnp.einsum("nbkgqs,nbskd->nbqkgd", p, vw.astype(jnp.float32))
    m = m.transpose(0, 1, 4, 2, 3)
    l = l.transpose(0, 1, 4, 2, 3)
    o = o / l[..., None]
    o = o.reshape(n, lp, h, d)[:, :L]
    m = m.reshape(n, lp, h)[:, :L]
    l = l.reshape(n, lp, h)[:, :L]
    return o, m, l


def dilated_attention(q, k, v):
    b, S, h, d = q.shape
    outs, ms, ls = [], [], []
    for window, dil in B_PATTERNS:
        L = S // dil

        def fold(t):
            return t.reshape(b, L, dil, h, d).transpose(0, 2, 1, 3, 4).reshape(b * dil, L, h, d)

        o, m, l = banded_attention(fold(q), fold(k), fold(v), window // dil)
        outs.append(o.reshape(b, dil, L, h, d).transpose(0, 2, 1, 3, 4).reshape(b, S, h, d))
        ms.append(m.reshape(b, dil, L, h).transpose(0, 2, 1, 3).reshape(b, S, h))
        ls.append(l.reshape(b, dil, L, h).transpose(0, 2, 1, 3).reshape(b, S, h))
    o = jnp.stack(outs)
    m = jnp.stack(ms)
    l = jnp.stack(ls)
    w = l * jnp.exp(m - m.max(axis=0, keepdims=True))
    return (w[..., None] * o).sum(axis=0) / w.sum(axis=0)[..., None]


def _fwd_setup_inputs(seed: int = 0) -> dict:
    key = jax.random.key(seed)
    ks = jax.random.split(key, 10)
    f32 = jnp.float32
    x = jax.random.normal(ks[0], (BATCH, SEQ, D_MODEL), f32)
    norm_gain = 1.0 + 0.1 * jax.random.normal(ks[1], (DEPTH, D_MODEL), f32)
    w_in = jax.random.normal(ks[2], (DEPTH, D_MODEL, IN_WIDTH), f32) * D_MODEL ** -0.5
    q_norm_a = 1.0 + 0.1 * jax.random.normal(ks[3], (DEPTH, HEAD_DIM), f32)
    k_norm_a = 1.0 + 0.1 * jax.random.normal(ks[4], (DEPTH, HEAD_DIM), f32)
    sinks_a = 0.5 * jax.random.normal(ks[5], (DEPTH, A_HEADS), f32)
    q_norm_b = 1.0 + 0.1 * jax.random.normal(ks[6], (DEPTH, HEAD_DIM), f32)
    k_norm_b = 1.0 + 0.1 * jax.random.normal(ks[7], (DEPTH, HEAD_DIM), f32)
    w_out = jax.random.normal(ks[8], (DEPTH, MIX_WIDTH, D_MODEL), f32) * MIX_WIDTH ** -0.5
    return {"x": x, "norm_gain": norm_gain, "w_in": w_in,
            "q_norm_a": q_norm_a, "k_norm_a": k_norm_a, "sinks_a": sinks_a,
            "q_norm_b": q_norm_b, "k_norm_b": k_norm_b, "w_out": w_out}


def _fwd_reference(x, norm_gain, w_in, q_norm_a, k_norm_a, sinks_a, q_norm_b, k_norm_b, w_out):
    b, S, _ = x.shape
    pos = jnp.arange(S)
    split_at = [int(c) for c in np.cumsum(IN_SPLITS)[:-1]]
    for i in range(DEPTH):
        hdn = rmsnorm(x, norm_gain[i])
        proj = jnp.einsum("bsd,de->bse", hdn, w_in[i])
        q_a, k_a, v_a, g_a, q_b, k_b, v_b, g_b = jnp.split(proj, split_at, axis=-1)

        q_a = rope(rmsnorm(q_a.reshape(b, S, A_HEADS, HEAD_DIM), q_norm_a[i]), pos)
        k_a = rope(rmsnorm(k_a.reshape(b, S, A_KV_HEADS, HEAD_DIM), k_norm_a[i]), pos)
        v_a = v_a.reshape(b, S, A_KV_HEADS, HEAD_DIM)
        o_a, _, _ = banded_attention(q_a, k_a, v_a, A_WINDOW - 1, sinks=sinks_a[i])
        o_a = o_a.reshape(b, S, A_WIDTH).astype(x.dtype) * jax.nn.silu(g_a)

        q_b = rope(rmsnorm(q_b.reshape(b, S, B_HEADS, HEAD_DIM), q_norm_b[i]), pos)
        k_b = rope(rmsnorm(k_b.reshape(b, S, B_HEADS, HEAD_DIM), k_norm_b[i]), pos)
        v_b = v_b.reshape(b, S, B_HEADS, HEAD_DIM)
        o_b = dilated_attention(q_b, k_b, v_b)
        o_b = o_b.reshape(b, S, B_WIDTH).astype(x.dtype) * jax.nn.silu(g_b)

        mixed = jnp.concatenate([o_a, o_b], axis=-1)
        x = x + jnp.einsum("bse,ed->bsd", mixed, w_out[i])
    return x


import jax as _jax
import jax.numpy as _jnp

TWIN_FORMAT = 'train_step'
FWD_PARAMS = ['x', 'norm_gain', 'w_in', 'q_norm_a', 'k_norm_a', 'sinks_a', 'q_norm_b', 'k_norm_b', 'w_out']
TWIN_WEIGHTS = ['norm_gain', 'w_in', 'q_norm_a', 'k_norm_a', 'sinks_a', 'q_norm_b', 'k_norm_b', 'w_out']
TWIN_DIFF_INPUT = 'x'
TWIN_INPUTS = ['x', 'norm_gain', 'w_in', 'q_norm_a', 'k_norm_a', 'sinks_a', 'q_norm_b', 'k_norm_b', 'w_out', 'loss_target', 'm_norm_gain', 'm_w_in', 'm_q_norm_a', 'm_k_norm_a', 'm_sinks_a', 'm_q_norm_b', 'm_k_norm_b', 'm_w_out', 'v_norm_gain', 'v_w_in', 'v_q_norm_a', 'v_k_norm_a', 'v_sinks_a', 'v_q_norm_b', 'v_k_norm_b', 'v_w_out']
TWIN_OUTPUTS = ['loss', 'grad_x', 'grad_norm_gain', 'grad_w_in', 'grad_q_norm_a', 'grad_k_norm_a', 'grad_sinks_a', 'grad_q_norm_b', 'grad_k_norm_b', 'grad_w_out', 'delta_norm_gain', 'delta_w_in', 'delta_q_norm_a', 'delta_k_norm_a', 'delta_sinks_a', 'delta_q_norm_b', 'delta_k_norm_b', 'delta_w_out', 'new_m_norm_gain', 'new_m_w_in', 'new_m_q_norm_a', 'new_m_k_norm_a', 'new_m_sinks_a', 'new_m_q_norm_b', 'new_m_k_norm_b', 'new_m_w_out', 'new_v_norm_gain', 'new_v_w_in', 'new_v_q_norm_a', 'new_v_k_norm_a', 'new_v_sinks_a', 'new_v_q_norm_b', 'new_v_k_norm_b', 'new_v_w_out']
TWIN_LEAF_KINDS = {'loss': 'loss', 'grad_x': 'grad_x', 'grad_norm_gain': 'grad_w', 'grad_w_in': 'grad_w', 'grad_q_norm_a': 'grad_w', 'grad_k_norm_a': 'grad_w', 'grad_sinks_a': 'grad_w', 'grad_q_norm_b': 'grad_w', 'grad_k_norm_b': 'grad_w', 'grad_w_out': 'grad_w', 'delta_norm_gain': 'delta_w', 'delta_w_in': 'delta_w', 'delta_q_norm_a': 'delta_w', 'delta_k_norm_a': 'delta_w', 'delta_sinks_a': 'delta_w', 'delta_q_norm_b': 'delta_w', 'delta_k_norm_b': 'delta_w', 'delta_w_out': 'delta_w', 'new_m_norm_gain': 'new_m', 'new_m_w_in': 'new_m', 'new_m_q_norm_a': 'new_m', 'new_m_k_norm_a': 'new_m', 'new_m_sinks_a': 'new_m', 'new_m_q_norm_b': 'new_m', 'new_m_k_norm_b': 'new_m', 'new_m_w_out': 'new_m', 'new_v_norm_gain': 'new_v', 'new_v_w_in': 'new_v', 'new_v_q_norm_a': 'new_v', 'new_v_k_norm_a': 'new_v', 'new_v_sinks_a': 'new_v', 'new_v_q_norm_b': 'new_v', 'new_v_k_norm_b': 'new_v', 'new_v_w_out': 'new_v'}


def _forward(args):
    return _fwd_reference(*[args[k] for k in FWD_PARAMS])


def _output_shape():
    out = _jax.eval_shape(lambda: _forward(_fwd_setup_inputs(0)))
    return out.shape, out.dtype

N_MICROBATCH = 1
ADAM_LR = 0.001
ADAM_B1 = 0.9
ADAM_B2 = 0.999
ADAM_EPS = 1e-08
ADAM_WD = 0.01
ADAM_STEP = 10
PER_EXAMPLE_BATCH_AXIS = {'x': 0, 'loss_target': 0}
SHARED_INPUTS = []
_WEIGHT_DTYPES = {'norm_gain': _jnp.float32, 'w_in': _jnp.float32, 'q_norm_a': _jnp.float32, 'k_norm_a': _jnp.float32, 'sinks_a': _jnp.float32, 'q_norm_b': _jnp.float32, 'k_norm_b': _jnp.float32, 'w_out': _jnp.float32}
MOMENT_SCALE = {'norm_gain': 2.217925e-01, 'w_in': 3.963448e-02, 'q_norm_a': 1.010157e+00, 'k_norm_a': 1.018051e+00, 'sinks_a': 2.888526e-01, 'q_norm_b': 6.414703e-01, 'k_norm_b': 6.463596e-01, 'w_out': 3.335834e-02}


def _to_microbatches(a, axis):
    t = _jnp.moveaxis(a, axis, 0)
    t = t.reshape((N_MICROBATCH, t.shape[0] // N_MICROBATCH) + t.shape[1:])
    return _jnp.moveaxis(t, 1, axis + 1)


def setup_inputs(seed: int = 0) -> dict:
    inp = _fwd_setup_inputs(seed)
    key = _jax.random.fold_in(_jax.random.key(seed), 7919)
    shape, _ = _output_shape()
    out = dict(inp)
    out["loss_target"] = _jax.random.normal(_jax.random.fold_in(key, 0), shape, _jnp.float32)
    for i, name in enumerate(TWIN_WEIGHTS):
        w = inp[name].astype(_jnp.float32)
        if MOMENT_SCALE is None:
            s = _jnp.sqrt(_jnp.mean(_jnp.square(w)) + 1e-30)
        else:
            s = MOMENT_SCALE[name]
        km, kv = _jax.random.split(_jax.random.fold_in(key, i + 1))
        out[name] = w
        out["m_" + name] = s * _jax.random.normal(km, w.shape, _jnp.float32)
        out["v_" + name] = (s * s) * _jax.random.uniform(kv, w.shape, _jnp.float32, 0.5, 1.5)
    if N_MICROBATCH > 1:
        for name, axis in PER_EXAMPLE_BATCH_AXIS.items():
            out[name] = _to_microbatches(out[name], axis)
    return {'x': out['x'], 'norm_gain': out['norm_gain'], 'w_in': out['w_in'], 'q_norm_a': out['q_norm_a'], 'k_norm_a': out['k_norm_a'], 'sinks_a': out['sinks_a'], 'q_norm_b': out['q_norm_b'], 'k_norm_b': out['k_norm_b'], 'w_out': out['w_out'], 'loss_target': out['loss_target'], 'm_norm_gain': out['m_norm_gain'], 'm_w_in': out['m_w_in'], 'm_q_norm_a': out['m_q_norm_a'], 'm_k_norm_a': out['m_k_norm_a'], 'm_sinks_a': out['m_sinks_a'], 'm_q_norm_b': out['m_q_norm_b'], 'm_k_norm_b': out['m_k_norm_b'], 'm_w_out': out['m_w_out'], 'v_norm_gain': out['v_norm_gain'], 'v_w_in': out['v_w_in'], 'v_q_norm_a': out['v_q_norm_a'], 'v_k_norm_a': out['v_k_norm_a'], 'v_sinks_a': out['v_sinks_a'], 'v_q_norm_b': out['v_q_norm_b'], 'v_k_norm_b': out['v_k_norm_b'], 'v_w_out': out['v_w_out']}


def _loss(weights, diff, rest, loss_target):
    with _jax.named_scope("forward"):
        args = {**rest, TWIN_DIFF_INPUT: diff, **{k: w.astype(_WEIGHT_DTYPES[k]) for k, w in weights.items()}}
        y = _forward(args)
    with _jax.named_scope("loss_head"):
        err = _jnp.square(y.astype(_jnp.float32) - loss_target)
        return 0.5 * _jnp.sum(_jnp.mean(err, axis=-1)) if err.ndim else 0.5 * err


def _adamw(w, g, m, v):
    m = ADAM_B1 * m + (1.0 - ADAM_B1) * g
    v = ADAM_B2 * v + (1.0 - ADAM_B2) * _jnp.square(g)
    m_hat = m / (1.0 - ADAM_B1 ** ADAM_STEP)
    v_hat = v / (1.0 - ADAM_B2 ** ADAM_STEP)
    delta = -ADAM_LR * (m_hat / (_jnp.sqrt(v_hat) + ADAM_EPS) + ADAM_WD * w)
    return delta, m, v


def reference(x, norm_gain, w_in, q_norm_a, k_norm_a, sinks_a, q_norm_b, k_norm_b, w_out, loss_target, m_norm_gain, m_w_in, m_q_norm_a, m_k_norm_a, m_sinks_a, m_q_norm_b, m_k_norm_b, m_w_out, v_norm_gain, v_w_in, v_q_norm_a, v_k_norm_a, v_sinks_a, v_q_norm_b, v_k_norm_b, v_w_out):
    given = dict(x=x, norm_gain=norm_gain, w_in=w_in, q_norm_a=q_norm_a, k_norm_a=k_norm_a, sinks_a=sinks_a, q_norm_b=q_norm_b, k_norm_b=k_norm_b, w_out=w_out, loss_target=loss_target, m_norm_gain=m_norm_gain, m_w_in=m_w_in, m_q_norm_a=m_q_norm_a, m_k_norm_a=m_k_norm_a, m_sinks_a=m_sinks_a, m_q_norm_b=m_q_norm_b, m_k_norm_b=m_k_norm_b, m_w_out=m_w_out, v_norm_gain=v_norm_gain, v_w_in=v_w_in, v_q_norm_a=v_q_norm_a, v_k_norm_a=v_k_norm_a, v_sinks_a=v_sinks_a, v_q_norm_b=v_q_norm_b, v_k_norm_b=v_k_norm_b, v_w_out=v_w_out)
    weights = {n: given[n] for n in TWIN_WEIGHTS}
    shared = {n: given[n] for n in SHARED_INPUTS}
    per_example = {n: given[n] for n in ['x']}
    grad_fn = _jax.value_and_grad(_loss, argnums=(0, 1))

    def one_microbatch(ex, loss_target):
        ex = dict(ex)
        diff = ex.pop(TWIN_DIFF_INPUT)
        return grad_fn(weights, diff, {**shared, **ex}, loss_target)

    if N_MICROBATCH == 1:
        loss, (grad_w, grad_x) = one_microbatch(per_example, given["loss_target"])
    else:
        def body(carry, xs):
            loss_sum, grad_sum = carry
            l_k, (gw_k, gx_k) = one_microbatch(xs[0], xs[1])
            with _jax.named_scope("update"):
                return (loss_sum + l_k, _jax.tree.map(_jnp.add, grad_sum, gw_k)), gx_k

        init = (_jnp.zeros((), _jnp.float32), _jax.tree.map(_jnp.zeros_like, weights))
        (loss, grad_w), grad_x = _jax.lax.scan(body, init, (per_example, given["loss_target"]))
    with _jax.named_scope("update"):
        delta_w, new_m, new_v = {}, {}, {}
        for n in TWIN_WEIGHTS:
            delta_w[n], new_m[n], new_v[n] = _adamw(weights[n], grad_w[n], given["m_" + n], given["v_" + n])
    return (loss, grad_x, *[grad_w[n] for n in TWIN_WEIGHTS], *[delta_w[n] for n in TWIN_WEIGHTS],
            *[new_m[n] for n in TWIN_WEIGHTS], *[new_v[n] for n in TWIN_WEIGHTS])
```

```python
import functools

import numpy as np
import jax
import jax.numpy as jnp
from jax import lax
from jax.experimental import pallas as pl
from jax.experimental.pallas import tpu as pltpu

F32 = jnp.float32
BF16 = jnp.bfloat16

SEQ = 4096
D_MODEL = 1024
HEAD_DIM = 64
PAIR = 2 * HEAD_DIM
N_PAIRS = 4
HALF_WIDTH = N_PAIRS * PAIR
KV_A_WIDTH = 128
IN_WIDTH = 3328
BLOCK = 128
EPS = 1e-6
NEG = -1e30
ROPE_THETA = 10000.0
N_DEV = 8
SHARD_IN = IN_WIDTH // N_DEV
SHARD_OUT = D_MODEL // N_DEV
PAYLOAD = SHARD_IN + SHARD_OUT
SMALL_ROWS, SMALL_COLS = 8, 256

C_QA, C_GA, C_QB, C_KB, C_VB, C_GB, C_KA, C_VA = 0, 512, 1024, 1536, 2048, 2560, 3072, 3200

ADAM_LR = 0.001
ADAM_B1 = 0.9
ADAM_B2 = 0.999
ADAM_EPS = 1e-08
ADAM_WD = 0.01
ADAM_STEP = 10

ROW_TILE = 256
VMEM_LIMIT = 56 * 1024 * 1024

MESH = pl.DeviceIdType.MESH


def _params(sem, vmem=VMEM_LIMIT):
    return pltpu.CompilerParams(dimension_semantics=sem, vmem_limit_bytes=vmem)


def _head_sum(v, bm):
    hi = v.astype(BF16)
    lo = (v - hi.astype(F32)).astype(BF16)
    return (jnp.dot(hi, bm, preferred_element_type=F32) + jnp.dot(lo, bm, preferred_element_type=F32))


def _swap_halves(y):
    lane = lax.broadcasted_iota(jnp.int32, y.shape, 1)
    first = (lane & 32) == 0
    return jnp.where(first, pltpu.roll(y, 96, 1), pltpu.roll(y, 32, 1))


def _sigmoid(g):
    return 1.0 / (1.0 + jnp.exp(-g))


def _all_gather_weights(payload):
    rows, cols = payload.shape

    def body(x_ref, out_ref, mine_ref, send_sems, recv_sems):
        x, y, c = lax.axis_index("x"), lax.axis_index("y"), lax.axis_index("c")
        me, sibling = (x, y, c), (x, y, 1 - c)
        chips = [(1 - x, y), (x, 1 - y), (1 - x, 1 - y)]

        def slot(px, py, pc):
            return out_ref.at[4 * px + 2 * py + pc]

        def copy(k, block, to, src=None):
            return pltpu.make_async_remote_copy(
                src_ref=slot(*block) if src is None else src, dst_ref=slot(*block),
                send_sem=send_sems.at[k], recv_sem=recv_sems.at[k], device_id=to, device_id_type=MESH)

        mine_ref[...] = x_ref[...].astype(BF16)
        first = [copy(0, me, sibling, src=mine_ref)]
        first += [copy(1 + j, me, (*chip, c), src=mine_ref) for j, chip in enumerate(chips)]
        for cp in first:
            cp.start()
        slot(*me)[...] = mine_ref[...]
        passed = [copy(4 + j, (*chip, c), sibling) for j, chip in enumerate(chips)]
        for j, chip in enumerate(chips):
            copy(1 + j, (*chip, c), me).wait_recv()
            passed[j].start()
        copy(0, sibling, me).wait_recv()
        for j, chip in enumerate(chips):
            copy(4 + j, (*chip, 1 - c), me).wait_recv()
        for cp in first + passed:
            cp.wait_send()

    return pl.pallas_call(
        body, name="ag_weights",
        out_shape=jax.ShapeDtypeStruct((N_DEV, rows, cols), BF16),
        in_specs=[pl.BlockSpec(memory_space=pltpu.VMEM)],
        out_specs=pl.BlockSpec(memory_space=pltpu.VMEM),
        scratch_shapes=[pltpu.VMEM((rows, cols), BF16),
                        pltpu.SemaphoreType.DMA((7,)), pltpu.SemaphoreType.DMA((7,))],
        compiler_params=pltpu.CompilerParams(vmem_limit_bytes=VMEM_LIMIT),
    )(payload)


def _reduce_scatter_grads(blocks, small):
    _, rows, cols = blocks.shape

    def body(g_hbm, small_ref, out_ref, small_out_ref, part, from_sib, from_chips, small_all,
             load_sems, sib_send, sib_recv, chip_send, chip_recv, small_send, small_recv):
        x, y, c = lax.axis_index("x"), lax.axis_index("y"), lax.axis_index("c")
        sibling = (x, y, 1 - c)
        chips = [(x, y), (1 - x, y), (x, 1 - y), (1 - x, 1 - y)]
        my_id = 4 * x + 2 * y + c

        def blk(chip, core):
            return g_hbm.at[4 * chip[0] + 2 * chip[1] + core]

        small_all[my_id] = small_ref[...]
        small_copies = []
        for rel in range(1, N_DEV):
            dx, dy, dc = (rel >> 2) & 1, (rel >> 1) & 1, rel & 1
            to = (1 - x if dx else x, 1 - y if dy else y, 1 - c if dc else c)
            small_copies.append(pltpu.make_async_remote_copy(
                src_ref=small_ref, dst_ref=small_all.at[my_id],
                send_sem=small_send.at[rel - 1], recv_sem=small_recv.at[rel - 1], device_id=to, device_id_type=MESH))
        for cp in small_copies:
            cp.start()

        loads = [pltpu.make_async_copy(blk(chips[k], c), part.at[k], load_sems.at[k]) for k in range(4)]
        for cp in loads:
            cp.start()
        to_sib = [pltpu.make_async_remote_copy(
            src_ref=blk(chips[k], 1 - c), dst_ref=from_sib.at[k], send_sem=sib_send.at[k], recv_sem=sib_recv.at[k],
            device_id=sibling, device_id_type=MESH) for k in range(4)]
        for cp in to_sib:
            cp.start()

        to_chips = [pltpu.make_async_remote_copy(
            src_ref=part.at[k], dst_ref=from_chips.at[k - 1], send_sem=chip_send.at[k - 1], recv_sem=chip_recv.at[k - 1],
            device_id=(*chips[k], c), device_id_type=MESH) for k in range(1, 4)]
        for k in (1, 2, 3, 0):
            loads[k].wait()
            to_sib[k].wait_recv()
            part[k] = part[k] + from_sib[k]
            if k > 0:
                to_chips[k - 1].start()
        acc = part[0]
        for k in range(3):
            to_chips[k].wait_recv()
            acc = acc + from_chips[k]
        out_ref[...] = acc

        for cp in small_copies:
            cp.wait_recv()
        tot = small_all[0]
        for d in range(1, N_DEV):
            tot = tot + small_all[d]
        small_out_ref[...] = tot
        for cp in to_sib + to_chips + small_copies:
            cp.wait_send()

    vmem = pl.BlockSpec(memory_space=pltpu.VMEM)
    return pl.pallas_call(
        body, name="rs_grads",
        out_shape=(jax.ShapeDtypeStruct((rows, cols), F32), jax.ShapeDtypeStruct((SMALL_ROWS, SMALL_COLS), F32)),
        in_specs=[pl.BlockSpec(memory_space=pl.ANY), vmem],
        out_specs=(vmem, vmem),
        scratch_shapes=[pltpu.VMEM((4, rows, cols), F32), pltpu.VMEM((4, rows, cols), F32),
                        pltpu.VMEM((3, rows, cols), F32), pltpu.VMEM((N_DEV, SMALL_ROWS, SMALL_COLS), F32),
                        pltpu.SemaphoreType.DMA((4,)), pltpu.SemaphoreType.DMA((4,)), pltpu.SemaphoreType.DMA((4,)),
                        pltpu.SemaphoreType.DMA((3,)), pltpu.SemaphoreType.DMA((3,)),
                        pltpu.SemaphoreType.DMA((7,)), pltpu.SemaphoreType.DMA((7,))],
        compiler_params=pltpu.CompilerParams(vmem_limit_bytes=VMEM_LIMIT),
    )(blocks, small)


def _fold_scratch(tm):
    return pltpu.VMEM((N_PAIRS, tm, PAIR), F32)


def _fold_store(val, scr, out4, out16, tm):
    for j in range(N_PAIRS):
        scr[j] = val[:, j * PAIR:(j + 1) * PAIR]
    for dil, out in ((4, out4), (16, out16)):
        for r in range(dil):
            for j in range(N_PAIRS):
                out[r, :, j * PAIR:(j + 1) * PAIR] = scr[j, pl.ds(r, tm // dil, stride=dil), :].astype(out.dtype)


def _unfold_load(src, scr, dil, tm):
    for r in range(dil):
        for j in range(N_PAIRS):
            scr[j, pl.ds(r, tm // dil, stride=dil), :] = src[r, :, j * PAIR:(j + 1) * PAIR]
    return jnp.concatenate([scr[j] for j in range(N_PAIRS)], axis=1)


def _fold_specs(tm, dtype):
    shapes = (jax.ShapeDtypeStruct((4, SEQ // 4, HALF_WIDTH), dtype), jax.ShapeDtypeStruct((16, SEQ // 16, HALF_WIDTH), dtype))
    specs = (pl.BlockSpec((4, tm // 4, HALF_WIDTH), lambda i: (0, i, 0)),
             pl.BlockSpec((16, tm // 16, HALF_WIDTH), lambda i: (0, i, 0)))
    return shapes, specs


def _proj_fwd(x, gain, w, qkg, cos4, sin4, bmean):
    tm = ROW_TILE

    def norm_rope(t, g, cos, sin, bm, scale):
        rr = lax.rsqrt(_head_sum(t * t, bm) + EPS)
        yv = t * rr * g
        return (yv * cos + _swap_halves(yv) * sin) * scale

    def body(x_ref, g_ref, w_ref, qkg_ref, cos_ref, sin_ref, bm_ref,
             proj_ref, ht_ref, qa_ref, ka_ref, va_ref, qb_ref, kb_ref, vb_ref,
             qb4_ref, qb16_ref, kb4_ref, kb16_ref, vb4_ref, vb16_ref, scr):
        xf = x_ref[...]
        r = lax.rsqrt(jnp.mean(xf * xf, axis=-1, keepdims=True) + EPS)
        hf = xf * r * g_ref[...]
        h = hf.astype(BF16)
        ht_ref[...] = hf.T.astype(BF16)
        cos, sin, bm = cos_ref[...], sin_ref[...], bm_ref[...]

        def section(c0, width):
            p = jnp.dot(h, w_ref[:, c0:c0 + width], preferred_element_type=F32)
            proj_ref[:, c0:c0 + width] = p
            return p

        def roped(p, row, scale):
            g = qkg_ref[row:row + 1, :]
            return jnp.concatenate(
                [norm_rope(p[:, j * PAIR:(j + 1) * PAIR], g, cos, sin, bm, scale) for j in range(p.shape[1] // PAIR)], axis=1)

        qa_ref[...] = roped(section(C_QA, HALF_WIDTH), 0, HEAD_DIM ** -0.5).astype(BF16)
        section(C_GA, HALF_WIDTH)
        qb = roped(section(C_QB, HALF_WIDTH), 2, HEAD_DIM ** -0.5)
        qb_ref[...] = qb.astype(BF16)
        _fold_store(qb, scr, qb4_ref, qb16_ref, tm)
        kb = roped(section(C_KB, HALF_WIDTH), 3, 1.0)
        kb_ref[...] = kb.astype(BF16)
        _fold_store(kb, scr, kb4_ref, kb16_ref, tm)
        vb = section(C_VB, HALF_WIDTH)
        vb_ref[...] = vb.astype(BF16)
        _fold_store(vb, scr, vb4_ref, vb16_ref, tm)
        section(C_GB, HALF_WIDTH)
        kva = section(C_KA, 2 * KV_A_WIDTH)
        ka_ref[...] = roped(kva[:, :KV_A_WIDTH], 1, 1.0).astype(BF16)
        va_ref[...] = kva[:, KV_A_WIDTH:].astype(BF16)

    row = lambda width: pl.BlockSpec((tm, width), lambda i: (i, 0))
    full = lambda a: pl.BlockSpec(a.shape, lambda i: (0,) * a.ndim)
    nat = lambda width: jax.ShapeDtypeStruct((SEQ, width), BF16)
    f_shapes, f_specs = _fold_specs(tm, BF16)
    return pl.pallas_call(
        body, name="proj_fwd", grid=(SEQ // tm,),
        in_specs=[row(D_MODEL), full(gain), full(w), full(qkg), row(PAIR), row(PAIR), full(bmean)],
        out_specs=(row(IN_WIDTH), pl.BlockSpec((D_MODEL, tm), lambda i: (0, i)),
                   row(HALF_WIDTH), row(KV_A_WIDTH), row(KV_A_WIDTH), row(HALF_WIDTH), row(HALF_WIDTH), row(HALF_WIDTH),
                   *f_specs, *f_specs, *f_specs),
        out_shape=(jax.ShapeDtypeStruct((SEQ, IN_WIDTH), F32), jax.ShapeDtypeStruct((D_MODEL, SEQ), BF16),
                   nat(HALF_WIDTH), nat(KV_A_WIDTH), nat(KV_A_WIDTH), nat(HALF_WIDTH), nat(HALF_WIDTH), nat(HALF_WIDTH),
                   *f_shapes, *f_shapes, *f_shapes),
        scratch_shapes=[_fold_scratch(tm)],
        compiler_params=_params(("arbitrary",)),
    )(x, gain, w, qkg, cos4, sin4, bmean)


def _band_mask(i, max_dist):
    j = lax.broadcasted_iota(jnp.int32, (2 * BLOCK, 2 * BLOCK), 0)
    c = lax.broadcasted_iota(jnp.int32, (2 * BLOCK, 2 * BLOCK), 1)
    dist = (c & (BLOCK - 1)) + BLOCK - j
    return (dist >= 0) & (dist <= max_dist) & ((j >= BLOCK) | (i > 0))


def _stack_heads(t):
    lane = lax.broadcasted_iota(jnp.int32, t.shape, 1)
    low = lane < HEAD_DIM
    zero = jnp.zeros_like(t)
    return jnp.concatenate([jnp.where(low, t, zero), jnp.where(low, zero, t)], axis=0)


def _unstack_t(t):
    return jnp.concatenate([t[:HEAD_DIM, :BLOCK], t[HEAD_DIM:, BLOCK:]], axis=0).T


def _rows_to_pair(row):
    return jnp.concatenate([jnp.broadcast_to(row[:, :BLOCK], (HEAD_DIM, BLOCK)),
                            jnp.broadcast_to(row[:, BLOCK:], (HEAD_DIM, BLOCK))], axis=0).T


def _pair_to_rows(t):
    tt = t.T
    return jnp.concatenate([tt[0:1, :], tt[HEAD_DIM:HEAD_DIM + 1, :]], axis=1)


def _attn_fwd(name, q, k, v, sink_rows, max_dist):
    n_seq, length, _ = q.shape
    ck = k.shape[2]
    nb = length // BLOCK
    shared = ck == PAIR
    has_sinks = sink_rows is not None

    def body(*refs):
        if has_sinks:
            q_ref, kp_ref, kc_ref, vp_ref, vc_ref, sink_ref, o_ref, lse_ref = refs
        else:
            q_ref, kp_ref, kc_ref, vp_ref, vc_ref, o_ref, lse_ref = refs
        valid = _band_mask(pl.program_id(1), max_dist)
        for p in range(N_PAIRS):
            cols = slice(p * PAIR, (p + 1) * PAIR)
            kcols = slice(0, PAIR) if shared else cols
            q_st = _stack_heads(q_ref[:, cols])
            k2 = jnp.concatenate([kp_ref[:, kcols], kc_ref[:, kcols]], axis=0)
            v2 = jnp.concatenate([vp_ref[:, kcols], vc_ref[:, kcols]], axis=0)
            st = lax.dot_general(k2, q_st, (((1,), (1,)), ((), ())), preferred_element_type=F32)
            st = jnp.where(valid, st, NEG)
            m = jnp.max(st, axis=0, keepdims=True)
            if has_sinks:
                sk = sink_ref[p:p + 1, :]
                m = jnp.maximum(m, sk)
            pt = jnp.exp(st - m)
            l = jnp.sum(pt, axis=0, keepdims=True)
            if has_sinks:
                l = l + jnp.exp(sk - m)
            v2t = v2.astype(F32).T.astype(BF16)
            ot = jnp.dot(v2t, pt.astype(BF16), preferred_element_type=F32) / l
            o_ref[:, cols] = _unstack_t(ot)
            lse_ref[:, cols] = _rows_to_pair(m + jnp.log(l))

    cur = lambda width: pl.BlockSpec((None, BLOCK, width), lambda r, i: (r, i, 0))
    prev = lambda width: pl.BlockSpec((None, BLOCK, width), lambda r, i: (r, jnp.maximum(i - 1, 0), 0))
    in_specs = [cur(HALF_WIDTH), prev(ck), cur(ck), prev(ck), cur(ck)]
    args = [q, k, k, v, v]
    if has_sinks:
        in_specs.append(pl.BlockSpec(sink_rows.shape, lambda r, i: (0, 0)))
        args.append(sink_rows)
    out = jax.ShapeDtypeStruct((n_seq, length, HALF_WIDTH), F32)
    return pl.pallas_call(
        body, name=name, grid=(n_seq, nb), in_specs=in_specs,
        out_specs=(cur(HALF_WIDTH), cur(HALF_WIDTH)), out_shape=(out, out),
        compiler_params=_params(("arbitrary", "arbitrary")),
    )(*args)


def _attn_bwd(name, q, k, v, d_o, lse, delta, sink_rows, max_dist):
    n_seq, length, _ = q.shape
    ck = k.shape[2]
    nb = length // BLOCK
    shared = ck == PAIR
    has_sinks = sink_rows is not None

    def body(*refs):
        if has_sinks:
            (q_ref, kp_ref, kc_ref, vp_ref, vc_ref, do_ref, lse_ref, dl_ref, sink_ref,
             dq_ref, dk_ref, dv_ref, dsink_ref, ck_scr, cv_scr) = refs
        else:
            (q_ref, kp_ref, kc_ref, vp_ref, vc_ref, do_ref, lse_ref, dl_ref,
             dq_ref, dk_ref, dv_ref, ck_scr, cv_scr) = refs
        r_id, i = pl.program_id(0), pl.program_id(1)

        @pl.when(i == 0)
        def _():
            ck_scr[...] = jnp.zeros_like(ck_scr)
            cv_scr[...] = jnp.zeros_like(cv_scr)

        if has_sinks:
            @pl.when((i == 0) & (r_id == 0))
            def _():
                dsink_ref[...] = jnp.zeros_like(dsink_ref)

        @pl.when(i < nb)
        def _():
            valid = _band_mask(i, max_dist)
            dk_acc = jnp.zeros((2 * BLOCK, PAIR), F32)
            dv_acc = jnp.zeros((2 * BLOCK, PAIR), F32)
            for p in range(N_PAIRS):
                cols = slice(p * PAIR, (p + 1) * PAIR)
                kcols = slice(0, PAIR) if shared else cols
                q_st = _stack_heads(q_ref[:, cols])
                do_st = _stack_heads(do_ref[:, cols])
                k2 = jnp.concatenate([kp_ref[:, kcols], kc_ref[:, kcols]], axis=0)
                v2 = jnp.concatenate([vp_ref[:, kcols], vc_ref[:, kcols]], axis=0)
                lse_row = _pair_to_rows(lse_ref[:, cols])
                dl_row = _pair_to_rows(dl_ref[:, cols])
                st = lax.dot_general(k2, q_st, (((1,), (1,)), ((), ())), preferred_element_type=F32)
                pt = jnp.exp(jnp.where(valid, st, NEG) - lse_row)
                dpt = lax.dot_general(v2, do_st, (((1,), (1,)), ((), ())), preferred_element_type=F32)
                dst = (pt * (dpt - dl_row)).astype(BF16)
                dv2 = jnp.dot(pt.astype(BF16), do_st, preferred_element_type=F32)
                dk2 = jnp.dot(dst, q_st, preferred_element_type=F32)
                k2t = k2.astype(F32).T.astype(BF16)
                dq_ref[:, cols] = _unstack_t(jnp.dot(k2t, dst, preferred_element_type=F32))
                if has_sinks:
                    p_sink = jnp.exp(sink_ref[p:p + 1, :] - lse_row)
                    dsink_ref[p:p + 1, :] = dsink_ref[p:p + 1, :] - p_sink * dl_row
                if shared:
                    dk_acc, dv_acc = dk_acc + dk2, dv_acc + dv2
                else:
                    dk_ref[:, cols] = ck_scr[:, cols] + dk2[:BLOCK]
                    dv_ref[:, cols] = cv_scr[:, cols] + dv2[:BLOCK]
                    ck_scr[:, cols] = dk2[BLOCK:]
                    cv_scr[:, cols] = dv2[BLOCK:]
            if shared:
                dk_ref[...] = ck_scr[...] + dk_acc[:BLOCK]
                dv_ref[...] = cv_scr[...] + dv_acc[:BLOCK]
                ck_scr[...] = dk_acc[BLOCK:]
                cv_scr[...] = dv_acc[BLOCK:]

        @pl.when(i == nb)
        def _():
            dk_ref[...] = ck_scr[...]
            dv_ref[...] = cv_scr[...]

    last = nb - 1
    cur = lambda width: pl.BlockSpec((None, BLOCK, width), lambda r, i: (r, jnp.minimum(i, last), 0))
    prev = lambda width: pl.BlockSpec((None, BLOCK, width), lambda r, i: (r, jnp.clip(i - 1, 0, last), 0))
    in_specs = [cur(HALF_WIDTH), prev(ck), cur(ck), prev(ck), cur(ck), cur(HALF_WIDTH), cur(HALF_WIDTH), cur(HALF_WIDTH)]
    args = [q, k, k, v, v, d_o, lse, delta]
    out_specs = [cur(HALF_WIDTH), prev(ck), prev(ck)]
    out_shape = [jax.ShapeDtypeStruct((n_seq, length, HALF_WIDTH), F32),
                 jax.ShapeDtypeStruct((n_seq, length, ck), F32), jax.ShapeDtypeStruct((n_seq, length, ck), F32)]
    if has_sinks:
        in_specs.append(pl.BlockSpec(sink_rows.shape, lambda r, i: (0, 0)))
        args.append(sink_rows)
        out_specs.append(pl.BlockSpec(sink_rows.shape, lambda r, i: (0, 0)))
        out_shape.append(jax.ShapeDtypeStruct(sink_rows.shape, F32))
    return pl.pallas_call(
        body, name=name, grid=(n_seq, nb + 1), in_specs=in_specs,
        out_specs=tuple(out_specs), out_shape=tuple(out_shape),
        scratch_shapes=[pltpu.VMEM((BLOCK, ck), F32), pltpu.VMEM((BLOCK, ck), F32)],
        compiler_params=_params(("arbitrary", "arbitrary")),
    )(*args)


def _tail(oa, ob1, lb1, ob4, lb4, ob16, lb16, proj, x, target, w_out, bones):
    tm = ROW_TILE

    def body(oa_ref, ob1_ref, lb1_ref, ob4_ref, lb4_ref, ob16_ref, lb16_ref, ga_ref, gb_ref, x_ref, t_ref, w_ref, bo_ref,
             loss_ref, dy_ref, gwo_ref, doa_ref, dla_ref, dga_ref, dgb_ref,
             dob_ref, dob4_ref, dob16_ref, dlb_ref, dlb4_ref, dlb16_ref, lse_ref, lse4_ref, lse16_ref,
             s_f):
        i = pl.program_id(0)
        o4, o16 = _unfold_load(ob4_ref, s_f, 4, tm), _unfold_load(ob16_ref, s_f, 16, tm)
        l4, l16 = _unfold_load(lb4_ref, s_f, 4, tm), _unfold_load(lb16_ref, s_f, 16, tm)
        o1, l1 = ob1_ref[...], lb1_ref[...]
        mx = jnp.maximum(jnp.maximum(l1, l4), l16)
        e1, e4, e16 = jnp.exp(l1 - mx), jnp.exp(l4 - mx), jnp.exp(l16 - mx)
        den = e1 + e4 + e16
        ob = (e1 * o1 + e4 * o4 + e16 * o16) / den
        lse_b = mx + jnp.log(den)

        oa, ga, gb = oa_ref[...], ga_ref[...], gb_ref[...]
        sa, sb = _sigmoid(ga), _sigmoid(gb)
        mixed = jnp.concatenate([oa * (ga * sa), ob * (gb * sb)], axis=1)
        mixed_bf = mixed.astype(BF16)
        w = w_ref[...]
        yv = x_ref[...] + jnp.dot(mixed_bf, w, preferred_element_type=F32)
        err = yv - t_ref[...]
        sq = jnp.sum(err * err, axis=0, keepdims=True)
        dy = err * (1.0 / D_MODEL)
        dy_ref[...] = dy
        dy_bf = dy.astype(BF16)
        gw = jnp.dot(mixed.T.astype(BF16), dy_bf, preferred_element_type=F32)

        @pl.when(i == 0)
        def _():
            loss_ref[...] = sq
            gwo_ref[...] = gw

        @pl.when(i > 0)
        def _():
            loss_ref[...] += sq
            gwo_ref[...] += gw

        dmix = lax.dot_general(dy_bf, w, (((1,), (1,)), ((), ())), preferred_element_type=F32)
        dma, dmb = dmix[:, :HALF_WIDTH], dmix[:, HALF_WIDTH:]
        bo = bo_ref[...]

        def head_delta(d_o, o):
            prod = d_o * o
            return jnp.concatenate([_head_sum(prod[:, j * PAIR:(j + 1) * PAIR], bo) for j in range(N_PAIRS)], axis=1)

        doa = dma * (ga * sa)
        doa_ref[...] = doa.astype(BF16)
        dla_ref[...] = head_delta(doa, oa)
        dga_ref[...] = dma * oa * (sa * (1.0 + ga * (1.0 - sa)))
        dob = dmb * (gb * sb)
        dgb_ref[...] = dmb * ob * (sb * (1.0 + gb * (1.0 - sb)))
        dlb = head_delta(dob, ob)
        dob_ref[...] = dob.astype(BF16)
        _fold_store(dob, s_f, dob4_ref, dob16_ref, tm)
        dlb_ref[...] = dlb
        _fold_store(dlb, s_f, dlb4_ref, dlb16_ref, tm)
        lse_ref[...] = lse_b
        _fold_store(lse_b, s_f, lse4_ref, lse16_ref, tm)

    row = lambda width: pl.BlockSpec((tm, width), lambda i: (i, 0))
    col_block = lambda c0: pl.BlockSpec((tm, HALF_WIDTH), lambda i: (i, c0 // HALF_WIDTH))
    full = lambda a: pl.BlockSpec(a.shape, lambda i: (0,) * a.ndim)
    fb_shapes, fb_specs = _fold_specs(tm, BF16)
    ff_shapes, ff_specs = _fold_specs(tm, F32)
    nat = lambda dtype: jax.ShapeDtypeStruct((SEQ, HALF_WIDTH), dtype)
    return pl.pallas_call(
        body, name="tail", grid=(SEQ // tm,),
        in_specs=[row(HALF_WIDTH), row(HALF_WIDTH), row(HALF_WIDTH), ff_specs[0], ff_specs[0], ff_specs[1], ff_specs[1],
                  col_block(C_GA), col_block(C_GB), row(D_MODEL), row(D_MODEL), full(w_out), full(bones)],
        out_specs=(pl.BlockSpec((1, D_MODEL), lambda i: (0, 0)), row(D_MODEL),
                   pl.BlockSpec((D_MODEL, D_MODEL), lambda i: (0, 0)),
                   row(HALF_WIDTH), row(HALF_WIDTH), row(HALF_WIDTH), row(HALF_WIDTH),
                   row(HALF_WIDTH), *fb_specs, row(HALF_WIDTH), *ff_specs, row(HALF_WIDTH), *ff_specs),
        out_shape=(jax.ShapeDtypeStruct((1, D_MODEL), F32), jax.ShapeDtypeStruct((SEQ, D_MODEL), F32),
                   jax.ShapeDtypeStruct((D_MODEL, D_MODEL), F32),
                   nat(BF16), nat(F32), nat(F32), nat(F32),
                   nat(BF16), *fb_shapes, nat(F32), *ff_shapes, nat(F32), *ff_shapes),
        scratch_shapes=[_fold_scratch(tm)],
        compiler_params=_params(("arbitrary",)),
    )(oa, ob1, lb1, ob4, lb4, ob16, lb16, proj, proj, x, target, w_out, bones)


def _dproj_assemble(dqa, dka, dva, dga, dgb, dq1, dk1, dv1, dq4, dk4, dv4, dq16, dk16, dv16, proj, qkg, cos4, sin4, bmean):
    tm = ROW_TILE

    def norm_rope_bwd(d_out, t, g, cos, sin, bm, scale):
        d_r = d_out * scale
        dyv = d_r * cos + _swap_halves(d_r * sin)
        rr = lax.rsqrt(_head_sum(t * t, bm) + EPS)
        that = t * rr
        dgain = jnp.sum(dyv * that, axis=0, keepdims=True)
        gdy = dyv * g
        dt = rr * (gdy - that * _head_sum(that * gdy, bm))
        return dt, dgain

    def body(dqa_ref, dka_ref, dva_ref, dga_ref, dgb_ref, dq1_ref, dk1_ref, dv1_ref, dq4_ref, dk4_ref, dv4_ref,
             dq16_ref, dk16_ref, dv16_ref, tqa_ref, tqb_ref, tkb_ref, tka_ref, qkg_ref, cos_ref, sin_ref, bm_ref,
             dproj_ref, dqkg_ref, s_f):
        i = pl.program_id(0)
        cos, sin, bm = cos_ref[...], sin_ref[...], bm_ref[...]

        def merged(nat_ref, f4_ref, f16_ref):
            return nat_ref[...] + _unfold_load(f4_ref, s_f, 4, tm) + _unfold_load(f16_ref, s_f, 16, tm)

        @pl.when(i == 0)
        def _():
            dqkg_ref[...] = jnp.zeros_like(dqkg_ref)

        def through(d_out, t, row, scale, c0):
            g = qkg_ref[row:row + 1, :]
            tot = jnp.zeros((1, PAIR), F32)
            for j in range(d_out.shape[1] // PAIR):
                cols = slice(j * PAIR, (j + 1) * PAIR)
                dt, dg = norm_rope_bwd(d_out[:, cols], t[:, cols], g, cos, sin, bm, scale)
                dproj_ref[:, c0 + j * PAIR:c0 + (j + 1) * PAIR] = dt.astype(BF16)
                tot = tot + dg
            dqkg_ref[row:row + 1, :] += tot

        through(dqa_ref[...], tqa_ref[...], 0, HEAD_DIM ** -0.5, C_QA)
        through(dka_ref[...], tka_ref[...], 1, 1.0, C_KA)
        through(merged(dq1_ref, dq4_ref, dq16_ref), tqb_ref[...], 2, HEAD_DIM ** -0.5, C_QB)
        through(merged(dk1_ref, dk4_ref, dk16_ref), tkb_ref[...], 3, 1.0, C_KB)
        dproj_ref[:, C_VB:C_VB + HALF_WIDTH] = merged(dv1_ref, dv4_ref, dv16_ref).astype(BF16)
        dproj_ref[:, C_GA:C_GA + HALF_WIDTH] = dga_ref[...].astype(BF16)
        dproj_ref[:, C_GB:C_GB + HALF_WIDTH] = dgb_ref[...].astype(BF16)
        dproj_ref[:, C_VA:C_VA + KV_A_WIDTH] = dva_ref[...].astype(BF16)

    row = lambda width: pl.BlockSpec((tm, width), lambda i: (i, 0))
    col_block = lambda c0, width: pl.BlockSpec((tm, width), lambda i: (i, c0 // width))
    full = lambda a: pl.BlockSpec(a.shape, lambda i: (0,) * a.ndim)
    _, ff_specs = _fold_specs(tm, F32)
    return pl.pallas_call(
        body, name="dproj_assemble", grid=(SEQ // tm,),
        in_specs=[row(HALF_WIDTH), row(KV_A_WIDTH), row(KV_A_WIDTH), row(HALF_WIDTH), row(HALF_WIDTH),
                  row(HALF_WIDTH), row(HALF_WIDTH), row(HALF_WIDTH), ff_specs[0], ff_specs[0], ff_specs[0],
                  ff_specs[1], ff_specs[1], ff_specs[1],
                  col_block(C_QA, HALF_WIDTH), col_block(C_QB, HALF_WIDTH), col_block(C_KB, HALF_WIDTH),
                  col_block(C_KA, KV_A_WIDTH), full(qkg), row(PAIR), row(PAIR), full(bmean)],
        out_specs=(row(IN_WIDTH), pl.BlockSpec((SMALL_ROWS, PAIR), lambda i: (0, 0))),
        out_shape=(jax.ShapeDtypeStruct((SEQ, IN_WIDTH), BF16), jax.ShapeDtypeStruct((SMALL_ROWS, PAIR), F32)),
        scratch_shapes=[_fold_scratch(tm)],
        compiler_params=_params(("arbitrary",)),
    )(dqa, dka, dva, dga, dgb, dq1, dk1, dv1, dq4, dk4, dv4, dq16, dk16, dv16, proj, proj, proj, proj, qkg, cos4, sin4, bmean)


def _input_grad(dproj, w, x, gain, dy):
    tm = ROW_TILE

    def body(dp_ref, w_ref, x_ref, g_ref, dy_ref, gx_ref, dgain_ref):
        i = pl.program_id(0)
        dh = lax.dot_general(dp_ref[...], w_ref[...], (((1,), (1,)), ((), ())), preferred_element_type=F32)
        xf = x_ref[...]
        r = lax.rsqrt(jnp.mean(xf * xf, axis=-1, keepdims=True) + EPS)
        xhat = xf * r
        dg = jnp.sum(dh * xhat, axis=0, keepdims=True)
        dxh = dh * g_ref[...]
        dx = r * (dxh - xhat * jnp.mean(dxh * xhat, axis=-1, keepdims=True))
        gx_ref[...] = dy_ref[...] + dx

        @pl.when(i == 0)
        def _():
            dgain_ref[...] = dg

        @pl.when(i > 0)
        def _():
            dgain_ref[...] += dg

    row = lambda width: pl.BlockSpec((tm, width), lambda i: (i, 0))
    full = lambda a: pl.BlockSpec(a.shape, lambda i: (0,) * a.ndim)
    return pl.pallas_call(
        body, name="input_grad", grid=(SEQ // tm,),
        in_specs=[row(IN_WIDTH), full(w), row(D_MODEL), full(gain), row(D_MODEL)],
        out_specs=(row(D_MODEL), pl.BlockSpec((1, D_MODEL), lambda i: (0, 0))),
        out_shape=(jax.ShapeDtypeStruct((SEQ, D_MODEL), F32), jax.ShapeDtypeStruct((1, D_MODEL), F32)),
        compiler_params=_params(("arbitrary",)),
    )(dproj, w, x, gain, dy)


def _weight_grad(h_t, dproj):
    tk = 512
    cb = IN_WIDTH // 2
    n_k = SEQ // tk

    def body(ht_ref, dp_ref, out_ref):
        k = pl.program_id(1)
        upd = jnp.dot(ht_ref[...], dp_ref[...], preferred_element_type=F32)

        @pl.when(k == 0)
        def _():
            out_ref[...] = upd

        @pl.when(k > 0)
        def _():
            out_ref[...] += upd

    return pl.pallas_call(
        body, name="weight_grad", grid=(2, n_k),
        in_specs=[pl.BlockSpec((D_MODEL, tk), lambda j, k: (0, k)), pl.BlockSpec((tk, cb), lambda j, k: (k, j))],
        out_specs=pl.BlockSpec((D_MODEL, cb), lambda j, k: (0, j)),
        out_shape=jax.ShapeDtypeStruct((D_MODEL, IN_WIDTH), F32),
        compiler_params=_params(("arbitrary", "arbitrary")),
    )(h_t, dproj)


def _adamw(name, w, g, m, v):
    def body(w_ref, g_ref, m_ref, v_ref, d_ref, nm_ref, nv_ref):
        gv = g_ref[...]
        nm = ADAM_B1 * m_ref[...] + (1.0 - ADAM_B1) * gv
        nv = ADAM_B2 * v_ref[...] + (1.0 - ADAM_B2) * jnp.square(gv)
        m_hat = nm / (1.0 - ADAM_B1 ** ADAM_STEP)
        v_hat = nv / (1.0 - ADAM_B2 ** ADAM_STEP)
        d_ref[...] = -ADAM_LR * (m_hat / (jnp.sqrt(v_hat) + ADAM_EPS) + ADAM_WD * w_ref[...])
        nm_ref[...] = nm
        nv_ref[...] = nv

    vmem = pl.BlockSpec(memory_space=pltpu.VMEM)
    out = jax.ShapeDtypeStruct(w.shape, F32)
    return pl.pallas_call(
        body, name=name, in_specs=[vmem] * 4, out_specs=(vmem,) * 3, out_shape=(out,) * 3,
        compiler_params=pltpu.CompilerParams(vmem_limit_bytes=VMEM_LIMIT),
    )(w, g, m, v)


def _pair_heads(a, axis):
    a = jnp.moveaxis(a, axis, 0)
    rest = a.shape[1:]
    a = a.reshape((2, 4, HEAD_DIM) + rest).transpose((1, 0, 2) + tuple(range(3, 3 + len(rest))))
    return jnp.moveaxis(a.reshape((HALF_WIDTH,) + rest), 0, axis)


def _unpair_heads(a, axis):
    a = jnp.moveaxis(a, axis, 0)
    rest = a.shape[1:]
    a = a.reshape((4, 2, HEAD_DIM) + rest).transpose((1, 0, 2) + tuple(range(3, 3 + len(rest))))
    return jnp.moveaxis(a.reshape((HALF_WIDTH,) + rest), 0, axis)


def _to_kernel_order(w):
    qa, ka, va, ga = w[:, 0:512], w[:, 512:640], w[:, 640:768], w[:, 768:1280]
    qb, kb, vb, gb = w[:, 1280:1792], w[:, 1792:2304], w[:, 2304:2816], w[:, 2816:3328]
    return jnp.concatenate([_pair_heads(qa, 1), _pair_heads(ga, 1), qb, kb, vb, gb, ka, va], axis=1)


def _from_kernel_order(g):
    qa, ga = _unpair_heads(g[:, C_QA:C_QA + 512], 1), _unpair_heads(g[:, C_GA:C_GA + 512], 1)
    qb, kb, vb, gb = g[:, C_QB:C_QB + 512], g[:, C_KB:C_KB + 512], g[:, C_VB:C_VB + 512], g[:, C_GB:C_GB + 512]
    ka, va = g[:, C_KA:C_KA + 128], g[:, C_VA:C_VA + 128]
    return jnp.concatenate([qa, ka, va, ga, qb, kb, vb, gb], axis=1)


def _pack_small(norm_gain, qa, ka, sinks, qb, kb):
    flat = jnp.concatenate([norm_gain.reshape(-1), qa.reshape(-1), ka.reshape(-1), sinks.reshape(-1),
                            qb.reshape(-1), kb.reshape(-1)])
    flat = jnp.pad(flat, (0, SMALL_ROWS * SMALL_COLS - flat.shape[0]))
    return flat.reshape(SMALL_ROWS, SMALL_COLS)


def _unpack_small(a):
    flat = a.reshape(-1)
    sizes = (D_MODEL, HEAD_DIM, HEAD_DIM, 8, HEAD_DIM, HEAD_DIM)
    out, off = [], 0
    for s in sizes:
        out.append(flat[off:off + s].reshape(1, s))
        off += s
    return out


def _fold_heads(row):
    return row[0, :HEAD_DIM] + row[0, HEAD_DIM:]


def kernel(x, norm_gain, w_in, q_norm_a, k_norm_a, sinks_a, q_norm_b, k_norm_b, w_out, loss_target, m_norm_gain, m_w_in, m_q_norm_a, m_k_norm_a, m_sinks_a, m_q_norm_b, m_k_norm_b, m_w_out, v_norm_gain, v_w_in, v_q_norm_a, v_k_norm_a, v_sinks_a, v_q_norm_b, v_k_norm_b, v_w_out):
    x2, tgt = x[0], loss_target[0]
    w_in_sh, w_out_sh = w_in[0], w_out[0]

    payload = jnp.concatenate([w_in_sh, w_out_sh.reshape(D_MODEL, SHARD_OUT)], axis=1)
    gathered = _all_gather_weights(payload)
    w_full = _to_kernel_order(gathered[:, :, :SHARD_IN].transpose(1, 0, 2).reshape(D_MODEL, IN_WIDTH))
    wo_full = gathered[:, :, SHARD_IN:].reshape(D_MODEL, D_MODEL)
    wo_full = jnp.concatenate([_pair_heads(wo_full[:HALF_WIDTH], 0), wo_full[HALF_WIDTH:]], axis=0)

    pos = jnp.arange(SEQ, dtype=F32)
    inv = ROPE_THETA ** (-jnp.arange(HEAD_DIM // 2, dtype=F32) / (HEAD_DIM // 2))
    ang = pos[:, None] * inv[None, :]
    cos, sin = jnp.cos(ang), jnp.sin(ang)
    cos4 = jnp.concatenate([cos, cos, cos, cos], axis=1)
    sin4 = jnp.concatenate([-sin, sin, -sin, sin], axis=1)
    blockdiag = np.kron(np.eye(2, dtype=np.float32), np.ones((HEAD_DIM, HEAD_DIM), np.float32))
    bmean = jnp.asarray(blockdiag / HEAD_DIM, dtype=BF16)
    bones = jnp.asarray(blockdiag, dtype=BF16)
    two = lambda g: jnp.concatenate([g, g], axis=1)
    qkg = jnp.concatenate([two(q_norm_a), two(k_norm_a), two(q_norm_b), two(k_norm_b),
                           jnp.zeros((SMALL_ROWS - 4, PAIR), F32)], axis=0)
    sinks_paired = jnp.stack([sinks_a[0, :N_PAIRS], sinks_a[0, N_PAIRS:]], axis=1)
    sink_rows = jnp.concatenate([jnp.repeat(sinks_paired, BLOCK, axis=1),
                                 jnp.zeros((SMALL_ROWS - N_PAIRS, 2 * BLOCK), F32)], axis=0)

    (proj, h_t, qa, ka, va, qb, kb, vb, qb4, qb16, kb4, kb16, vb4, vb16) = _proj_fwd(
        x2, norm_gain, w_full, qkg, cos4, sin4, bmean)
    oa, la = _attn_fwd("attn_a_fwd", qa[None], ka[None], va[None], sink_rows, BLOCK - 1)
    ob1, lb1 = _attn_fwd("attn_b1_fwd", qb[None], kb[None], vb[None], None, BLOCK)
    ob4, lb4 = _attn_fwd("attn_b4_fwd", qb4, kb4, vb4, None, BLOCK)
    ob16, lb16 = _attn_fwd("attn_b16_fwd", qb16, kb16, vb16, None, BLOCK)
    (loss_cols, dy, gwo, doa, dla, dga, dgb, dob, dob4, dob16, dlb, dlb4, dlb16, lse_b, lse4, lse16) = _tail(
        oa[0], ob1[0], lb1[0], ob4, lb4, ob16, lb16, proj, x2, tgt, wo_full, bones)

    dqa, dka, dva, dsink = _attn_bwd("attn_a_bwd", qa[None], ka[None], va[None], doa[None], la, dla[None], sink_rows, BLOCK - 1)
    dq1, dk1, dv1 = _attn_bwd("attn_b1_bwd", qb[None], kb[None], vb[None], dob[None], lse_b[None], dlb[None], None, BLOCK)
    dq4, dk4, dv4 = _attn_bwd("attn_b4_bwd", qb4, kb4, vb4, dob4, lse4, dlb4, None, BLOCK)
    dq16, dk16, dv16 = _attn_bwd("attn_b16_bwd", qb16, kb16, vb16, dob16, lse16, dlb16, None, BLOCK)
    dproj, dqkg = _dproj_assemble(dqa[0], dka[0], dva[0], dga, dgb, dq1[0], dk1[0], dv1[0], dq4, dk4, dv4,
                                  dq16, dk16, dv16, proj, qkg, cos4, sin4, bmean)
    grad_x, dgain = _input_grad(dproj, w_full, x2, norm_gain, dy)
    gw_in = _from_kernel_order(_weight_grad(h_t, dproj))
    gw_out = jnp.concatenate([_unpair_heads(gwo[:HALF_WIDTH], 0), gwo[HALF_WIDTH:]], axis=0)

    blocks = jnp.concatenate([gw_in.reshape(D_MODEL, N_DEV, SHARD_IN).transpose(1, 0, 2),
                              gw_out.reshape(N_DEV, SHARD_OUT, D_MODEL).reshape(N_DEV, D_MODEL, SHARD_OUT)], axis=2)
    g_sinks = jnp.concatenate([jnp.sum(dsink[:N_PAIRS, :BLOCK], axis=1), jnp.sum(dsink[:N_PAIRS, BLOCK:], axis=1)])
    small = _pack_small(dgain, _fold_heads(dqkg[0:1]), _fold_heads(dqkg[1:2]), g_sinks,
                        _fold_heads(dqkg[2:3]), _fold_heads(dqkg[3:4]))
    reduced, small_red = _reduce_scatter_grads(blocks, small)
    g_w_in = reduced[:, :SHARD_IN]
    g_w_out = reduced[:, SHARD_IN:].reshape(SHARD_OUT, D_MODEL)
    g_small = _unpack_small(small_red)

    d_in, nm_in, nv_in = _adamw("adamw_w_in", w_in_sh, g_w_in, m_w_in[0], v_w_in[0])
    d_out, nm_out, nv_out = _adamw("adamw_w_out", w_out_sh, g_w_out, m_w_out[0], v_w_out[0])
    d_s, nm_s, nv_s = _adamw(
        "adamw_small",
        _pack_small(norm_gain, q_norm_a, k_norm_a, sinks_a, q_norm_b, k_norm_b), small_red,
        _pack_small(m_norm_gain, m_q_norm_a, m_k_norm_a, m_sinks_a, m_q_norm_b, m_k_norm_b),
        _pack_small(v_norm_gain, v_q_norm_a, v_k_norm_a, v_sinks_a, v_q_norm_b, v_k_norm_b))
    d_small, nm_small, nv_small = _unpack_small(d_s), _unpack_small(nm_s), _unpack_small(nv_s)

    loss = lax.psum(0.5 * jnp.sum(loss_cols) / D_MODEL, ("x", "y", "c"))

    def assemble(small_list, big_in, big_out):
        ng, qa_, ka_, sk_, qb_, kb_ = small_list
        return [ng, big_in[None], qa_, ka_, sk_, qb_, kb_, big_out[None]]

    return (loss, grad_x[None], *assemble(g_small, g_w_in, g_w_out), *assemble(d_small, d_in, d_out),
            *assemble(nm_small, nm_in, nm_out), *assemble(nv_small, nv_in, nv_out))
```

```python
import functools

import numpy as np
import jax
import jax.numpy as jnp
from jax import lax
from jax.experimental import pallas as pl
from jax.experimental.pallas import tpu as pltpu

F32 = jnp.float32
BF16 = jnp.bfloat16

SEQ = 4096
D_MODEL = 1024
HEAD_DIM = 64
PAIR = 2 * HEAD_DIM
N_PAIRS = 4
HALF_WIDTH = N_PAIRS * PAIR
KV_A_WIDTH = 128
IN_WIDTH = 3328
BLOCK = 128
EPS = 1e-6
NEG = -1e30
ROPE_THETA = 10000.0
N_DEV = 8
SHARD_IN = IN_WIDTH // N_DEV
SHARD_OUT = D_MODEL // N_DEV
PAYLOAD = SHARD_IN + SHARD_OUT
SMALL_ROWS, SMALL_COLS = 8, 256

C_QA, C_GA, C_QB, C_KB, C_VB, C_GB, C_KA, C_VA = 0, 512, 1024, 1536, 2048, 2560, 3072, 3200

ADAM_LR = 0.001
ADAM_B1 = 0.9
ADAM_B2 = 0.999
ADAM_EPS = 1e-08
ADAM_WD = 0.01
ADAM_STEP = 10

ROW_TILE = 256
VMEM_LIMIT = 56 * 1024 * 1024

MESH = pl.DeviceIdType.MESH


def _params(sem, vmem=VMEM_LIMIT):
    return pltpu.CompilerParams(dimension_semantics=sem, vmem_limit_bytes=vmem)


def _head_sum(v, bm):
    hi = v.astype(BF16)
    lo = (v - hi.astype(F32)).astype(BF16)
    return (jnp.dot(hi, bm, preferred_element_type=F32) + jnp.dot(lo, bm, preferred_element_type=F32))


def _swap_halves(y):
    lane = lax.broadcasted_iota(jnp.int32, y.shape, 1)
    first = (lane & 32) == 0
    return jnp.where(first, pltpu.roll(y, 96, 1), pltpu.roll(y, 32, 1))


def _sigmoid(g):
    return 1.0 / (1.0 + jnp.exp(-g))


def _all_gather_weights(payload):
    rows, cols = payload.shape

    def body(x_ref, out_ref, mine_ref, send_sems, recv_sems):
        x, y, c = lax.axis_index("x"), lax.axis_index("y"), lax.axis_index("c")
        me, sibling = (x, y, c), (x, y, 1 - c)
        chips = [(1 - x, y), (x, 1 - y), (1 - x, 1 - y)]

        def slot(px, py, pc):
            return out_ref.at[4 * px + 2 * py + pc]

        def copy(k, block, to, src=None):
            return pltpu.make_async_remote_copy(
                src_ref=slot(*block) if src is None else src, dst_ref=slot(*block),
                send_sem=send_sems.at[k], recv_sem=recv_sems.at[k], device_id=to, device_id_type=MESH)

        mine_ref[...] = x_ref[...].astype(BF16)
        first = [copy(0, me, sibling, src=mine_ref)]
        first += [copy(1 + j, me, (*chip, c), src=mine_ref) for j, chip in enumerate(chips)]
        for cp in first:
            cp.start()
        slot(*me)[...] = mine_ref[...]
        passed = [copy(4 + j, (*chip, c), sibling) for j, chip in enumerate(chips)]
        for j, chip in enumerate(chips):
            copy(1 + j, (*chip, c), me).wait_recv()
            passed[j].start()
        copy(0, sibling, me).wait_recv()
        for j, chip in enumerate(chips):
            copy(4 + j, (*chip, 1 - c), me).wait_recv()
        for cp in first + passed:
            cp.wait_send()

    return pl.pallas_call(
        body, name="ag_weights",
        out_shape=jax.ShapeDtypeStruct((N_DEV, rows, cols), BF16),
        in_specs=[pl.BlockSpec(memory_space=pltpu.VMEM)],
        out_specs=pl.BlockSpec(memory_space=pltpu.VMEM),
        scratch_shapes=[pltpu.VMEM((rows, cols), BF16),
                        pltpu.SemaphoreType.DMA((7,)), pltpu.SemaphoreType.DMA((7,))],
        compiler_params=pltpu.CompilerParams(vmem_limit_bytes=VMEM_LIMIT),
    )(payload)


def _reduce_scatter_grads(blocks, small):
    _, rows, cols = blocks.shape

    def body(g_hbm, small_ref, out_ref, small_out_ref, part, from_sib, to_wire, from_chips, small_all,
             load_sems, sib_send, sib_recv, chip_send, chip_recv, small_send, small_recv):
        x, y, c = lax.axis_index("x"), lax.axis_index("y"), lax.axis_index("c")
        sibling = (x, y, 1 - c)
        chips = [(x, y), (1 - x, y), (x, 1 - y), (1 - x, 1 - y)]
        my_id = 4 * x + 2 * y + c

        def blk(chip, core):
            return g_hbm.at[4 * chip[0] + 2 * chip[1] + core]

        small_all[my_id] = small_ref[...]
        small_copies = []
        for rel in range(1, N_DEV):
            dx, dy, dc = (rel >> 2) & 1, (rel >> 1) & 1, rel & 1
            to = (1 - x if dx else x, 1 - y if dy else y, 1 - c if dc else c)
            small_copies.append(pltpu.make_async_remote_copy(
                src_ref=small_ref, dst_ref=small_all.at[my_id],
                send_sem=small_send.at[rel - 1], recv_sem=small_recv.at[rel - 1], device_id=to, device_id_type=MESH))
        for cp in small_copies:
            cp.start()

        loads = [pltpu.make_async_copy(blk(chips[k], c), part.at[k], load_sems.at[k]) for k in range(4)]
        for cp in loads:
            cp.start()
        to_sib = [pltpu.make_async_remote_copy(
            src_ref=blk(chips[k], 1 - c), dst_ref=from_sib.at[k], send_sem=sib_send.at[k], recv_sem=sib_recv.at[k],
            device_id=sibling, device_id_type=MESH) for k in range(4)]
        for cp in to_sib:
            cp.start()

        to_chips = [pltpu.make_async_remote_copy(
            src_ref=to_wire.at[k - 1], dst_ref=from_chips.at[k - 1], send_sem=chip_send.at[k - 1], recv_sem=chip_recv.at[k - 1],
            device_id=(*chips[k], c), device_id_type=MESH) for k in range(1, 4)]
        for k in (1, 2, 3):
            loads[k].wait()
            to_sib[k].wait_recv()
            to_wire[k - 1] = (part[k] + from_sib[k]).astype(BF16)
            to_chips[k - 1].start()
        loads[0].wait()
        to_sib[0].wait_recv()
        acc = part[0] + from_sib[0]
        for k in range(3):
            to_chips[k].wait_recv()
            acc = acc + from_chips[k].astype(F32)
        out_ref[...] = acc

        for cp in small_copies:
            cp.wait_recv()
        tot = small_all[0]
        for d in range(1, N_DEV):
            tot = tot + small_all[d]
        small_out_ref[...] = tot
        for cp in to_sib + to_chips + small_copies:
            cp.wait_send()

    vmem = pl.BlockSpec(memory_space=pltpu.VMEM)
    return pl.pallas_call(
        body, name="rs_grads",
        out_shape=(jax.ShapeDtypeStruct((rows, cols), F32), jax.ShapeDtypeStruct((SMALL_ROWS, SMALL_COLS), F32)),
        in_specs=[pl.BlockSpec(memory_space=pl.ANY), vmem],
        out_specs=(vmem, vmem),
        scratch_shapes=[pltpu.VMEM((4, rows, cols), F32), pltpu.VMEM((4, rows, cols), F32),
                        pltpu.VMEM((3, rows, cols), BF16), pltpu.VMEM((3, rows, cols), BF16),
                        pltpu.VMEM((N_DEV, SMALL_ROWS, SMALL_COLS), F32),
                        pltpu.SemaphoreType.DMA((4,)), pltpu.SemaphoreType.DMA((4,)), pltpu.SemaphoreType.DMA((4,)),
                        pltpu.SemaphoreType.DMA((3,)), pltpu.SemaphoreType.DMA((3,)),
                        pltpu.SemaphoreType.DMA((7,)), pltpu.SemaphoreType.DMA((7,))],
        compiler_params=pltpu.CompilerParams(vmem_limit_bytes=VMEM_LIMIT),
    )(blocks, small)


def _fold_scratch(tm):
    return pltpu.VMEM((N_PAIRS, tm, PAIR), F32)


def _fold_store(val, scr, out4, out16, tm):
    for j in range(N_PAIRS):
        scr[j] = val[:, j * PAIR:(j + 1) * PAIR]
    for dil, out in ((4, out4), (16, out16)):
        for r in range(dil):
            for j in range(N_PAIRS):
                out[r, :, j * PAIR:(j + 1) * PAIR] = scr[j, pl.ds(r, tm // dil, stride=dil), :].astype(out.dtype)


def _unfold_load(src, scr, dil, tm):
    for r in range(dil):
        for j in range(N_PAIRS):
            scr[j, pl.ds(r, tm // dil, stride=dil), :] = src[r, :, j * PAIR:(j + 1) * PAIR]
    return jnp.concatenate([scr[j] for j in range(N_PAIRS)], axis=1)


def _fold_specs(tm, dtype):
    shapes = (jax.ShapeDtypeStruct((4, SEQ // 4, HALF_WIDTH), dtype), jax.ShapeDtypeStruct((16, SEQ // 16, HALF_WIDTH), dtype))
    specs = (pl.BlockSpec((4, tm // 4, HALF_WIDTH), lambda i: (0, i, 0)),
             pl.BlockSpec((16, tm // 16, HALF_WIDTH), lambda i: (0, i, 0)))
    return shapes, specs


def _proj_fwd(x, gain, w, qkg, cos4, sin4, bmean):
    tm = ROW_TILE

    def norm_rope(t, g, cos, sin, bm, scale):
        rr = lax.rsqrt(_head_sum(t * t, bm) + EPS)
        yv = t * rr * g
        return (yv * cos + _swap_halves(yv) * sin) * scale

    def body(x_ref, g_ref, w_ref, qkg_ref, cos_ref, sin_ref, bm_ref,
             proj_ref, ht_ref, qa_ref, ka_ref, va_ref, qb_ref, kb_ref, vb_ref,
             qb4_ref, qb16_ref, kb4_ref, kb16_ref, vb4_ref, vb16_ref, scr):
        xf = x_ref[...]
        r = lax.rsqrt(jnp.mean(xf * xf, axis=-1, keepdims=True) + EPS)
        hf = xf * r * g_ref[...]
        h = hf.astype(BF16)
        ht_ref[...] = hf.T.astype(BF16)
        cos, sin, bm = cos_ref[...], sin_ref[...], bm_ref[...]

        def section(c0, width):
            p = jnp.dot(h, w_ref[:, c0:c0 + width], preferred_element_type=F32)
            proj_ref[:, c0:c0 + width] = p
            return p

        def roped(p, row, scale):
            g = qkg_ref[row:row + 1, :]
            return jnp.concatenate(
                [norm_rope(p[:, j * PAIR:(j + 1) * PAIR], g, cos, sin, bm, scale) for j in range(p.shape[1] // PAIR)], axis=1)

        qa_ref[...] = roped(section(C_QA, HALF_WIDTH), 0, HEAD_DIM ** -0.5).astype(BF16)
        section(C_GA, HALF_WIDTH)
        qb = roped(section(C_QB, HALF_WIDTH), 2, HEAD_DIM ** -0.5)
        qb_ref[...] = qb.astype(BF16)
        _fold_store(qb, scr, qb4_ref, qb16_ref, tm)
        kb = roped(section(C_KB, HALF_WIDTH), 3, 1.0)
        kb_ref[...] = kb.astype(BF16)
        _fold_store(kb, scr, kb4_ref, kb16_ref, tm)
        vb = section(C_VB, HALF_WIDTH)
        vb_ref[...] = vb.astype(BF16)
        _fold_store(vb, scr, vb4_ref, vb16_ref, tm)
        section(C_GB, HALF_WIDTH)
        kva = section(C_KA, 2 * KV_A_WIDTH)
        ka_ref[...] = roped(kva[:, :KV_A_WIDTH], 1, 1.0).astype(BF16)
        va_ref[...] = kva[:, KV_A_WIDTH:].astype(BF16)

    row = lambda width: pl.BlockSpec((tm, width), lambda i: (i, 0))
    full = lambda a: pl.BlockSpec(a.shape, lambda i: (0,) * a.ndim)
    nat = lambda width: jax.ShapeDtypeStruct((SEQ, width), BF16)
    f_shapes, f_specs = _fold_specs(tm, BF16)
    return pl.pallas_call(
        body, name="proj_fwd", grid=(SEQ // tm,),
        in_specs=[row(D_MODEL), full(gain), full(w), full(qkg), row(PAIR), row(PAIR), full(bmean)],
        out_specs=(row(IN_WIDTH), pl.BlockSpec((D_MODEL, tm), lambda i: (0, i)),
                   row(HALF_WIDTH), row(KV_A_WIDTH), row(KV_A_WIDTH), row(HALF_WIDTH), row(HALF_WIDTH), row(HALF_WIDTH),
                   *f_specs, *f_specs, *f_specs),
        out_shape=(jax.ShapeDtypeStruct((SEQ, IN_WIDTH), F32), jax.ShapeDtypeStruct((D_MODEL, SEQ), BF16),
                   nat(HALF_WIDTH), nat(KV_A_WIDTH), nat(KV_A_WIDTH), nat(HALF_WIDTH), nat(HALF_WIDTH), nat(HALF_WIDTH),
                   *f_shapes, *f_shapes, *f_shapes),
        scratch_shapes=[_fold_scratch(tm)],
        compiler_params=_params(("arbitrary",)),
    )(x, gain, w, qkg, cos4, sin4, bmean)


def _band_mask(i, max_dist):
    j = lax.broadcasted_iota(jnp.int32, (2 * BLOCK, 2 * BLOCK), 0)
    c = lax.broadcasted_iota(jnp.int32, (2 * BLOCK, 2 * BLOCK), 1)
    dist = (c & (BLOCK - 1)) + BLOCK - j
    return (dist >= 0) & (dist <= max_dist) & ((j >= BLOCK) | (i > 0))


def _stack_heads(t):
    lane = lax.broadcasted_iota(jnp.int32, t.shape, 1)
    low = lane < HEAD_DIM
    zero = jnp.zeros_like(t)
    return jnp.concatenate([jnp.where(low, t, zero), jnp.where(low, zero, t)], axis=0)


def _unstack_t(t):
    return jnp.concatenate([t[:HEAD_DIM, :BLOCK], t[HEAD_DIM:, BLOCK:]], axis=0).T


def _rows_to_pair(row):
    return jnp.concatenate([jnp.broadcast_to(row[:, :BLOCK], (HEAD_DIM, BLOCK)),
                            jnp.broadcast_to(row[:, BLOCK:], (HEAD_DIM, BLOCK))], axis=0).T


def _pair_to_rows(t):
    tt = t.T
    return jnp.concatenate([tt[0:1, :], tt[HEAD_DIM:HEAD_DIM + 1, :]], axis=1)


def _attn_fwd(name, q, k, v, sink_rows, max_dist):
    n_seq, length, _ = q.shape
    ck = k.shape[2]
    nb = length // BLOCK
    shared = ck == PAIR
    has_sinks = sink_rows is not None

    def body(*refs):
        if has_sinks:
            q_ref, kc_ref, vc_ref, sink_ref, o_ref, lse_ref, kp_ref, vp_ref = refs
        else:
            q_ref, kc_ref, vc_ref, o_ref, lse_ref, kp_ref, vp_ref = refs
        i = pl.program_id(1)

        @pl.when(i == 0)
        def _():
            kp_ref[...] = jnp.zeros_like(kp_ref)
            vp_ref[...] = jnp.zeros_like(vp_ref)

        valid = _band_mask(i, max_dist)
        pairs = range(N_PAIRS)
        cols = [slice(p * PAIR, (p + 1) * PAIR) for p in pairs]
        kcols = [slice(0, PAIR) if shared else cols[p] for p in pairs]
        st = [lax.dot_general(jnp.concatenate([kp_ref[:, kcols[p]], kc_ref[:, kcols[p]]], axis=0), _stack_heads(q_ref[:, cols[p]]),
                              (((1,), (1,)), ((), ())), preferred_element_type=F32) for p in pairs]
        st = [jnp.where(valid, s, NEG) for s in st]
        m = [jnp.max(s, axis=0, keepdims=True) for s in st]
        if has_sinks:
            sk = [sink_ref[p:p + 1, :] for p in pairs]
            m = [jnp.maximum(m[p], sk[p]) for p in pairs]
        pt = [jnp.exp(st[p] - m[p]) for p in pairs]
        l = [jnp.sum(t, axis=0, keepdims=True) for t in pt]
        if has_sinks:
            l = [l[p] + jnp.exp(sk[p] - m[p]) for p in pairs]
        v2t = [jnp.concatenate([vp_ref[:, kcols[p]], vc_ref[:, kcols[p]]], axis=0).astype(F32).T.astype(BF16) for p in pairs]
        ot = [jnp.dot(v2t[p], pt[p].astype(BF16), preferred_element_type=F32) / l[p] for p in pairs]
        for p in pairs:
            o_ref[:, cols[p]] = _unstack_t(ot[p])
            lse_ref[:, cols[p]] = _rows_to_pair(m[p] + jnp.log(l[p]))
        kp_ref[...] = kc_ref[...]
        vp_ref[...] = vc_ref[...]

    cur = lambda width: pl.BlockSpec((None, BLOCK, width), lambda r, i: (r, i, 0))
    in_specs = [cur(HALF_WIDTH), cur(ck), cur(ck)]
    args = [q, k, v]
    if has_sinks:
        in_specs.append(pl.BlockSpec(sink_rows.shape, lambda r, i: (0, 0)))
        args.append(sink_rows)
    out = jax.ShapeDtypeStruct((n_seq, length, HALF_WIDTH), F32)
    return pl.pallas_call(
        body, name=name, grid=(n_seq, nb), in_specs=in_specs,
        out_specs=(cur(HALF_WIDTH), cur(HALF_WIDTH)), out_shape=(out, out),
        scratch_shapes=[pltpu.VMEM((BLOCK, ck), BF16), pltpu.VMEM((BLOCK, ck), BF16)],
        compiler_params=_params(("arbitrary", "arbitrary")),
    )(*args)


def _attn_bwd(name, q, k, v, d_o, lse, delta, sink_rows, max_dist):
    n_seq, length, _ = q.shape
    ck = k.shape[2]
    nb = length // BLOCK
    shared = ck == PAIR
    has_sinks = sink_rows is not None

    def body(*refs):
        if has_sinks:
            (q_ref, kc_ref, vc_ref, do_ref, lse_ref, dl_ref, sink_ref,
             dq_ref, dk_ref, dv_ref, dsink_ref, ck_scr, cv_scr, kp_ref, vp_ref) = refs
        else:
            (q_ref, kc_ref, vc_ref, do_ref, lse_ref, dl_ref,
             dq_ref, dk_ref, dv_ref, ck_scr, cv_scr, kp_ref, vp_ref) = refs
        r_id, i = pl.program_id(0), pl.program_id(1)

        @pl.when(i == 0)
        def _():
            ck_scr[...] = jnp.zeros_like(ck_scr)
            cv_scr[...] = jnp.zeros_like(cv_scr)
            kp_ref[...] = jnp.zeros_like(kp_ref)
            vp_ref[...] = jnp.zeros_like(vp_ref)

        if has_sinks:
            @pl.when((i == 0) & (r_id == 0))
            def _():
                dsink_ref[...] = jnp.zeros_like(dsink_ref)

        @pl.when(i < nb)
        def _():
            valid = _band_mask(i, max_dist)
            pairs = range(N_PAIRS)
            cols = [slice(p * PAIR, (p + 1) * PAIR) for p in pairs]
            kcols = [slice(0, PAIR) if shared else cols[p] for p in pairs]
            nt = (((1,), (1,)), ((), ()))
            q_st = [_stack_heads(q_ref[:, cols[p]]) for p in pairs]
            do_st = [_stack_heads(do_ref[:, cols[p]]) for p in pairs]
            k2 = [jnp.concatenate([kp_ref[:, kcols[p]], kc_ref[:, kcols[p]]], axis=0) for p in pairs]
            v2 = [jnp.concatenate([vp_ref[:, kcols[p]], vc_ref[:, kcols[p]]], axis=0) for p in pairs]
            st = [lax.dot_general(k2[p], q_st[p], nt, preferred_element_type=F32) for p in pairs]
            dpt = [lax.dot_general(v2[p], do_st[p], nt, preferred_element_type=F32) for p in pairs]
            lse_row = [_pair_to_rows(lse_ref[:, cols[p]]) for p in pairs]
            dl_row = [_pair_to_rows(dl_ref[:, cols[p]]) for p in pairs]
            pt = [jnp.exp(jnp.where(valid, st[p], NEG) - lse_row[p]) for p in pairs]
            dst = [(pt[p] * (dpt[p] - dl_row[p])).astype(BF16) for p in pairs]
            ptb = [t.astype(BF16) for t in pt]
            dv2 = [jnp.dot(ptb[p], do_st[p], preferred_element_type=F32) for p in pairs]
            dk2 = [jnp.dot(dst[p], q_st[p], preferred_element_type=F32) for p in pairs]
            k2t = [k2[p].astype(F32).T.astype(BF16) for p in pairs]
            dqt = [jnp.dot(k2t[p], dst[p], preferred_element_type=F32) for p in pairs]
            for p in pairs:
                dq_ref[:, cols[p]] = _unstack_t(dqt[p])
            if has_sinks:
                for p in pairs:
                    p_sink = jnp.exp(sink_ref[p:p + 1, :] - lse_row[p])
                    dsink_ref[p:p + 1, :] = dsink_ref[p:p + 1, :] - p_sink * dl_row[p]
            if shared:
                dk_acc = (dk2[0] + dk2[1]) + (dk2[2] + dk2[3])
                dv_acc = (dv2[0] + dv2[1]) + (dv2[2] + dv2[3])
                dk_ref[...] = ck_scr[...] + dk_acc[:BLOCK]
                dv_ref[...] = cv_scr[...] + dv_acc[:BLOCK]
                ck_scr[...] = dk_acc[BLOCK:]
                cv_scr[...] = dv_acc[BLOCK:]
            else:
                for p in pairs:
                    dk_ref[:, cols[p]] = ck_scr[:, cols[p]] + dk2[p][:BLOCK]
                    dv_ref[:, cols[p]] = cv_scr[:, cols[p]] + dv2[p][:BLOCK]
                    ck_scr[:, cols[p]] = dk2[p][BLOCK:]
                    cv_scr[:, cols[p]] = dv2[p][BLOCK:]
            kp_ref[...] = kc_ref[...]
            vp_ref[...] = vc_ref[...]

        @pl.when(i == nb)
        def _():
            dk_ref[...] = ck_scr[...]
            dv_ref[...] = cv_scr[...]

    last = nb - 1
    cur = lambda width: pl.BlockSpec((None, BLOCK, width), lambda r, i: (r, jnp.minimum(i, last), 0))
    prev = lambda width: pl.BlockSpec((None, BLOCK, width), lambda r, i: (r, jnp.clip(i - 1, 0, last), 0))
    in_specs = [cur(HALF_WIDTH), cur(ck), cur(ck), cur(HALF_WIDTH), cur(HALF_WIDTH), cur(HALF_WIDTH)]
    args = [q, k, v, d_o, lse, delta]
    out_specs = [cur(HALF_WIDTH), prev(ck), prev(ck)]
    out_shape = [jax.ShapeDtypeStruct((n_seq, length, HALF_WIDTH), F32),
                 jax.ShapeDtypeStruct((n_seq, length, ck), F32), jax.ShapeDtypeStruct((n_seq, length, ck), F32)]
    if has_sinks:
        in_specs.append(pl.BlockSpec(sink_rows.shape, lambda r, i: (0, 0)))
        args.append(sink_rows)
        out_specs.append(pl.BlockSpec(sink_rows.shape, lambda r, i: (0, 0)))
        out_shape.append(jax.ShapeDtypeStruct(sink_rows.shape, F32))
    return pl.pallas_call(
        body, name=name, grid=(n_seq, nb + 1), in_specs=in_specs,
        out_specs=tuple(out_specs), out_shape=tuple(out_shape),
        scratch_shapes=[pltpu.VMEM((BLOCK, ck), F32), pltpu.VMEM((BLOCK, ck), F32),
                        pltpu.VMEM((BLOCK, ck), BF16), pltpu.VMEM((BLOCK, ck), BF16)],
        compiler_params=_params(("arbitrary", "arbitrary")),
    )(*args)


def _tail(oa, ob1, lb1, ob4, lb4, ob16, lb16, proj, x, target, w_out, bones):
    tm = ROW_TILE

    def body(oa_ref, ob1_ref, lb1_ref, ob4_ref, lb4_ref, ob16_ref, lb16_ref, ga_ref, gb_ref, x_ref, t_ref, w_ref, bo_ref,
             loss_ref, dy_ref, gwo_ref, doa_ref, dla_ref, dga_ref, dgb_ref,
             dob_ref, dob4_ref, dob16_ref, dlb_ref, dlb4_ref, dlb16_ref, lse_ref, lse4_ref, lse16_ref,
             s_f):
        i = pl.program_id(0)
        o4, o16 = _unfold_load(ob4_ref, s_f, 4, tm), _unfold_load(ob16_ref, s_f, 16, tm)
        l4, l16 = _unfold_load(lb4_ref, s_f, 4, tm), _unfold_load(lb16_ref, s_f, 16, tm)
        o1, l1 = ob1_ref[...], lb1_ref[...]
        mx = jnp.maximum(jnp.maximum(l1, l4), l16)
        e1, e4, e16 = jnp.exp(l1 - mx), jnp.exp(l4 - mx), jnp.exp(l16 - mx)
        den = e1 + e4 + e16
        ob = (e1 * o1 + e4 * o4 + e16 * o16) / den
        lse_b = mx + jnp.log(den)

        oa, ga, gb = oa_ref[...], ga_ref[...], gb_ref[...]
        sa, sb = _sigmoid(ga), _sigmoid(gb)
        mixed = jnp.concatenate([oa * (ga * sa), ob * (gb * sb)], axis=1)
        mixed_bf = mixed.astype(BF16)
        w = w_ref[...]
        yv = x_ref[...] + jnp.dot(mixed_bf, w, preferred_element_type=F32)
        err = yv - t_ref[...]
        sq = jnp.sum(err * err, axis=0, keepdims=True)
        dy = err * (1.0 / D_MODEL)
        dy_ref[...] = dy
        dy_bf = dy.astype(BF16)
        gw = jnp.dot(mixed.T.astype(BF16), dy_bf, preferred_element_type=F32)

        @pl.when(i == 0)
        def _():
            loss_ref[...] = sq
            gwo_ref[...] = gw

        @pl.when(i > 0)
        def _():
            loss_ref[...] += sq
            gwo_ref[...] += gw

        dmix = lax.dot_general(dy_bf, w, (((1,), (1,)), ((), ())), preferred_element_type=F32)
        dma, dmb = dmix[:, :HALF_WIDTH], dmix[:, HALF_WIDTH:]
        bo = bo_ref[...]

        def head_delta(d_o, o):
            prod = d_o * o
            return jnp.concatenate([_head_sum(prod[:, j * PAIR:(j + 1) * PAIR], bo) for j in range(N_PAIRS)], axis=1)

        doa = dma * (ga * sa)
        doa_ref[...] = doa.astype(BF16)
        dla_ref[...] = head_delta(doa, oa)
        dga_ref[...] = dma * oa * (sa * (1.0 + ga * (1.0 - sa)))
        dob = dmb * (gb * sb)
        dgb_ref[...] = dmb * ob * (sb * (1.0 + gb * (1.0 - sb)))
        dlb = head_delta(dob, ob)
        dob_ref[...] = dob.astype(BF16)
        _fold_store(dob, s_f, dob4_ref, dob16_ref, tm)
        dlb_ref[...] = dlb
        _fold_store(dlb, s_f, dlb4_ref, dlb16_ref, tm)
        lse_ref[...] = lse_b
        _fold_store(lse_b, s_f, lse4_ref, lse16_ref, tm)

    row = lambda width: pl.BlockSpec((tm, width), lambda i: (i, 0))
    col_block = lambda c0: pl.BlockSpec((tm, HALF_WIDTH), lambda i: (i, c0 // HALF_WIDTH))
    full = lambda a: pl.BlockSpec(a.shape, lambda i: (0,) * a.ndim)
    fb_shapes, fb_specs = _fold_specs(tm, BF16)
    ff_shapes, ff_specs = _fold_specs(tm, F32)
    nat = lambda dtype: jax.ShapeDtypeStruct((SEQ, HALF_WIDTH), dtype)
    return pl.pallas_call(
        body, name="tail", grid=(SEQ // tm,),
        in_specs=[row(HALF_WIDTH), row(HALF_WIDTH), row(HALF_WIDTH), ff_specs[0], ff_specs[0], ff_specs[1], ff_specs[1],
                  col_block(C_GA), col_block(C_GB), row(D_MODEL), row(D_MODEL), full(w_out), full(bones)],
        out_specs=(pl.BlockSpec((1, D_MODEL), lambda i: (0, 0)), row(D_MODEL),
                   pl.BlockSpec((D_MODEL, D_MODEL), lambda i: (0, 0)),
                   row(HALF_WIDTH), row(HALF_WIDTH), row(HALF_WIDTH), row(HALF_WIDTH),
                   row(HALF_WIDTH), *fb_specs, row(HALF_WIDTH), *ff_specs, row(HALF_WIDTH), *ff_specs),
        out_shape=(jax.ShapeDtypeStruct((1, D_MODEL), F32), jax.ShapeDtypeStruct((SEQ, D_MODEL), F32),
                   jax.ShapeDtypeStruct((D_MODEL, D_MODEL), F32),
                   nat(BF16), nat(F32), nat(F32), nat(F32),
                   nat(BF16), *fb_shapes, nat(F32), *ff_shapes, nat(F32), *ff_shapes),
        scratch_shapes=[_fold_scratch(tm)],
        compiler_params=_params(("arbitrary",)),
    )(oa, ob1, lb1, ob4, lb4, ob16, lb16, proj, proj, x, target, w_out, bones)


def _dproj_assemble(dqa, dka, dva, dga, dgb, dq1, dk1, dv1, dq4, dk4, dv4, dq16, dk16, dv16, proj, qkg, cos4, sin4, bmean):
    tm = ROW_TILE

    def norm_rope_bwd(d_out, t, g, cos, sin, bm, scale):
        d_r = d_out * scale
        dyv = d_r * cos + _swap_halves(d_r * sin)
        rr = lax.rsqrt(_head_sum(t * t, bm) + EPS)
        that = t * rr
        dgain = jnp.sum(dyv * that, axis=0, keepdims=True)
        gdy = dyv * g
        dt = rr * (gdy - that * _head_sum(that * gdy, bm))
        return dt, dgain

    def body(dqa_ref, dka_ref, dva_ref, dga_ref, dgb_ref, dq1_ref, dk1_ref, dv1_ref, dq4_ref, dk4_ref, dv4_ref,
             dq16_ref, dk16_ref, dv16_ref, tqa_ref, tqb_ref, tkb_ref, tka_ref, qkg_ref, cos_ref, sin_ref, bm_ref,
             dproj_ref, dqkg_ref, s_f):
        i = pl.program_id(0)
        cos, sin, bm = cos_ref[...], sin_ref[...], bm_ref[...]

        def merged(nat_ref, f4_ref, f16_ref):
            return nat_ref[...] + _unfold_load(f4_ref, s_f, 4, tm) + _unfold_load(f16_ref, s_f, 16, tm)

        @pl.when(i == 0)
        def _():
            dqkg_ref[...] = jnp.zeros_like(dqkg_ref)

        def through(d_out, t, row, scale, c0):
            g = qkg_ref[row:row + 1, :]
            tot = jnp.zeros((1, PAIR), F32)
            for j in range(d_out.shape[1] // PAIR):
                cols = slice(j * PAIR, (j + 1) * PAIR)
                dt, dg = norm_rope_bwd(d_out[:, cols], t[:, cols], g, cos, sin, bm, scale)
                dproj_ref[:, c0 + j * PAIR:c0 + (j + 1) * PAIR] = dt.astype(BF16)
                tot = tot + dg
            dqkg_ref[row:row + 1, :] += tot

        through(dqa_ref[...], tqa_ref[...], 0, HEAD_DIM ** -0.5, C_QA)
        through(dka_ref[...], tka_ref[...], 1, 1.0, C_KA)
        through(merged(dq1_ref, dq4_ref, dq16_ref), tqb_ref[...], 2, HEAD_DIM ** -0.5, C_QB)
        through(merged(dk1_ref, dk4_ref, dk16_ref), tkb_ref[...], 3, 1.0, C_KB)
        dproj_ref[:, C_VB:C_VB + HALF_WIDTH] = merged(dv1_ref, dv4_ref, dv16_ref).astype(BF16)
        dproj_ref[:, C_GA:C_GA + HALF_WIDTH] = dga_ref[...].astype(BF16)
        dproj_ref[:, C_GB:C_GB + HALF_WIDTH] = dgb_ref[...].astype(BF16)
        dproj_ref[:, C_VA:C_VA + KV_A_WIDTH] = dva_ref[...].astype(BF16)

    row = lambda width: pl.BlockSpec((tm, width), lambda i: (i, 0))
    col_block = lambda c0, width: pl.BlockSpec((tm, width), lambda i: (i, c0 // width))
    full = lambda a: pl.BlockSpec(a.shape, lambda i: (0,) * a.ndim)
    _, ff_specs = _fold_specs(tm, F32)
    return pl.pallas_call(
        body, name="dproj_assemble", grid=(SEQ // tm,),
        in_specs=[row(HALF_WIDTH), row(KV_A_WIDTH), row(KV_A_WIDTH), row(HALF_WIDTH), row(HALF_WIDTH),
                  row(HALF_WIDTH), row(HALF_WIDTH), row(HALF_WIDTH), ff_specs[0], ff_specs[0], ff_specs[0],
                  ff_specs[1], ff_specs[1], ff_specs[1],
                  col_block(C_QA, HALF_WIDTH), col_block(C_QB, HALF_WIDTH), col_block(C_KB, HALF_WIDTH),
                  col_block(C_KA, KV_A_WIDTH), full(qkg), row(PAIR), row(PAIR), full(bmean)],
        out_specs=(row(IN_WIDTH), pl.BlockSpec((SMALL_ROWS, PAIR), lambda i: (0, 0))),
        out_shape=(jax.ShapeDtypeStruct((SEQ, IN_WIDTH), BF16), jax.ShapeDtypeStruct((SMALL_ROWS, PAIR), F32)),
        scratch_shapes=[_fold_scratch(tm)],
        compiler_params=_params(("arbitrary",)),
    )(dqa, dka, dva, dga, dgb, dq1, dk1, dv1, dq4, dk4, dv4, dq16, dk16, dv16, proj, proj, proj, proj, qkg, cos4, sin4, bmean)


def _input_grad(dproj, w, x, gain, dy):
    tm = ROW_TILE

    def body(dp_ref, w_ref, x_ref, g_ref, dy_ref, gx_ref, dgain_ref):
        i = pl.program_id(0)
        dh = lax.dot_general(dp_ref[...], w_ref[...], (((1,), (1,)), ((), ())), preferred_element_type=F32)
        xf = x_ref[...]
        r = lax.rsqrt(jnp.mean(xf * xf, axis=-1, keepdims=True) + EPS)
        xhat = xf * r
        dg = jnp.sum(dh * xhat, axis=0, keepdims=True)
        dxh = dh * g_ref[...]
        dx = r * (dxh - xhat * jnp.mean(dxh * xhat, axis=-1, keepdims=True))
        gx_ref[...] = dy_ref[...] + dx

        @pl.when(i == 0)
        def _():
            dgain_ref[...] = dg

        @pl.when(i > 0)
        def _():
            dgain_ref[...] += dg

    row = lambda width: pl.BlockSpec((tm, width), lambda i: (i, 0))
    full = lambda a: pl.BlockSpec(a.shape, lambda i: (0,) * a.ndim)
    return pl.pallas_call(
        body, name="input_grad", grid=(SEQ // tm,),
        in_specs=[row(IN_WIDTH), full(w), row(D_MODEL), full(gain), row(D_MODEL)],
        out_specs=(row(D_MODEL), pl.BlockSpec((1, D_MODEL), lambda i: (0, 0))),
        out_shape=(jax.ShapeDtypeStruct((SEQ, D_MODEL), F32), jax.ShapeDtypeStruct((1, D_MODEL), F32)),
        compiler_params=_params(("arbitrary",)),
    )(dproj, w, x, gain, dy)


def _weight_grad(h_t, dproj):
    tk = 512
    cb = IN_WIDTH // 2
    n_k = SEQ // tk

    def body(ht_ref, dp_ref, out_ref):
        k = pl.program_id(1)
        upd = jnp.dot(ht_ref[...], dp_ref[...], preferred_element_type=F32)

        @pl.when(k == 0)
        def _():
            out_ref[...] = upd

        @pl.when(k > 0)
        def _():
            out_ref[...] += upd

    return pl.pallas_call(
        body, name="weight_grad", grid=(2, n_k),
        in_specs=[pl.BlockSpec((D_MODEL, tk), lambda j, k: (0, k)), pl.BlockSpec((tk, cb), lambda j, k: (k, j))],
        out_specs=pl.BlockSpec((D_MODEL, cb), lambda j, k: (0, j)),
        out_shape=jax.ShapeDtypeStruct((D_MODEL, IN_WIDTH), F32),
        compiler_params=_params(("arbitrary", "arbitrary")),
    )(h_t, dproj)


def _adamw(name, w, g, m, v):
    def body(w_ref, g_ref, m_ref, v_ref, d_ref, nm_ref, nv_ref):
        gv = g_ref[...]
        nm = ADAM_B1 * m_ref[...] + (1.0 - ADAM_B1) * gv
        nv = ADAM_B2 * v_ref[...] + (1.0 - ADAM_B2) * jnp.square(gv)
        m_hat = nm / (1.0 - ADAM_B1 ** ADAM_STEP)
        v_hat = nv / (1.0 - ADAM_B2 ** ADAM_STEP)
        d_ref[...] = -ADAM_LR * (m_hat / (jnp.sqrt(v_hat) + ADAM_EPS) + ADAM_WD * w_ref[...])
        nm_ref[...] = nm
        nv_ref[...] = nv

    vmem = pl.BlockSpec(memory_space=pltpu.VMEM)
    out = jax.ShapeDtypeStruct(w.shape, F32)
    return pl.pallas_call(
        body, name=name, in_specs=[vmem] * 4, out_specs=(vmem,) * 3, out_shape=(out,) * 3,
        compiler_params=pltpu.CompilerParams(vmem_limit_bytes=VMEM_LIMIT),
    )(w, g, m, v)


def _pair_heads(a, axis):
    a = jnp.moveaxis(a, axis, 0)
    rest = a.shape[1:]
    a = a.reshape((2, 4, HEAD_DIM) + rest).transpose((1, 0, 2) + tuple(range(3, 3 + len(rest))))
    return jnp.moveaxis(a.reshape((HALF_WIDTH,) + rest), 0, axis)


def _unpair_heads(a, axis):
    a = jnp.moveaxis(a, axis, 0)
    rest = a.shape[1:]
    a = a.reshape((4, 2, HEAD_DIM) + rest).transpose((1, 0, 2) + tuple(range(3, 3 + len(rest))))
    return jnp.moveaxis(a.reshape((HALF_WIDTH,) + rest), 0, axis)


def _to_kernel_order(w):
    qa, ka, va, ga = w[:, 0:512], w[:, 512:640], w[:, 640:768], w[:, 768:1280]
    qb, kb, vb, gb = w[:, 1280:1792], w[:, 1792:2304], w[:, 2304:2816], w[:, 2816:3328]
    return jnp.concatenate([_pair_heads(qa, 1), _pair_heads(ga, 1), qb, kb, vb, gb, ka, va], axis=1)


def _from_kernel_order(g):
    qa, ga = _unpair_heads(g[:, C_QA:C_QA + 512], 1), _unpair_heads(g[:, C_GA:C_GA + 512], 1)
    qb, kb, vb, gb = g[:, C_QB:C_QB + 512], g[:, C_KB:C_KB + 512], g[:, C_VB:C_VB + 512], g[:, C_GB:C_GB + 512]
    ka, va = g[:, C_KA:C_KA + 128], g[:, C_VA:C_VA + 128]
    return jnp.concatenate([qa, ka, va, ga, qb, kb, vb, gb], axis=1)


SMALL_USED = D_MODEL + 4 * HEAD_DIM + 8


def _pack_small(norm_gain, qa, ka, sinks, qb, kb, extra=None):
    parts = [norm_gain.reshape(-1), qa.reshape(-1), ka.reshape(-1), sinks.reshape(-1), qb.reshape(-1), kb.reshape(-1)]
    if extra is not None:
        parts.append(extra.reshape(-1))
    flat = jnp.concatenate(parts)
    flat = jnp.pad(flat, (0, SMALL_ROWS * SMALL_COLS - flat.shape[0]))
    return flat.reshape(SMALL_ROWS, SMALL_COLS)


def _unpack_small(a):
    flat = a.reshape(-1)
    sizes = (D_MODEL, HEAD_DIM, HEAD_DIM, 8, HEAD_DIM, HEAD_DIM)
    out, off = [], 0
    for s in sizes:
        out.append(flat[off:off + s].reshape(1, s))
        off += s
    return out


def _fold_heads(row):
    return row[0, :HEAD_DIM] + row[0, HEAD_DIM:]


def kernel(x, norm_gain, w_in, q_norm_a, k_norm_a, sinks_a, q_norm_b, k_norm_b, w_out, loss_target, m_norm_gain, m_w_in, m_q_norm_a, m_k_norm_a, m_sinks_a, m_q_norm_b, m_k_norm_b, m_w_out, v_norm_gain, v_w_in, v_q_norm_a, v_k_norm_a, v_sinks_a, v_q_norm_b, v_k_norm_b, v_w_out):
    x2, tgt = x[0], loss_target[0]
    w_in_sh, w_out_sh = w_in[0], w_out[0]

    payload = jnp.concatenate([w_in_sh, w_out_sh.reshape(D_MODEL, SHARD_OUT)], axis=1)
    gathered = _all_gather_weights(payload)
    w_full = _to_kernel_order(gathered[:, :, :SHARD_IN].transpose(1, 0, 2).reshape(D_MODEL, IN_WIDTH))
    wo_full = gathered[:, :, SHARD_IN:].reshape(D_MODEL, D_MODEL)
    wo_full = jnp.concatenate([_pair_heads(wo_full[:HALF_WIDTH], 0), wo_full[HALF_WIDTH:]], axis=0)

    pos = jnp.arange(SEQ, dtype=F32)
    inv = ROPE_THETA ** (-jnp.arange(HEAD_DIM // 2, dtype=F32) / (HEAD_DIM // 2))
    ang = pos[:, None] * inv[None, :]
    cos, sin = jnp.cos(ang), jnp.sin(ang)
    cos4 = jnp.concatenate([cos, cos, cos, cos], axis=1)
    sin4 = jnp.concatenate([-sin, sin, -sin, sin], axis=1)
    blockdiag = np.kron(np.eye(2, dtype=np.float32), np.ones((HEAD_DIM, HEAD_DIM), np.float32))
    bmean = jnp.asarray(blockdiag / HEAD_DIM, dtype=BF16)
    bones = jnp.asarray(blockdiag, dtype=BF16)
    two = lambda g: jnp.concatenate([g, g], axis=1)
    qkg = jnp.concatenate([two(q_norm_a), two(k_norm_a), two(q_norm_b), two(k_norm_b),
                           jnp.zeros((SMALL_ROWS - 4, PAIR), F32)], axis=0)
    sinks_paired = jnp.stack([sinks_a[0, :N_PAIRS], sinks_a[0, N_PAIRS:]], axis=1)
    sink_rows = jnp.concatenate([jnp.repeat(sinks_paired, BLOCK, axis=1),
                                 jnp.zeros((SMALL_ROWS - N_PAIRS, 2 * BLOCK), F32)], axis=0)

    (proj, h_t, qa, ka, va, qb, kb, vb, qb4, qb16, kb4, kb16, vb4, vb16) = _proj_fwd(
        x2, norm_gain, w_full, qkg, cos4, sin4, bmean)
    oa, la = _attn_fwd("attn_a_fwd", qa[None], ka[None], va[None], sink_rows, BLOCK - 1)
    ob1, lb1 = _attn_fwd("attn_b1_fwd", qb[None], kb[None], vb[None], None, BLOCK)
    ob4, lb4 = _attn_fwd("attn_b4_fwd", qb4, kb4, vb4, None, BLOCK)
    ob16, lb16 = _attn_fwd("attn_b16_fwd", qb16, kb16, vb16, None, BLOCK)
    (loss_cols, dy, gwo, doa, dla, dga, dgb, dob, dob4, dob16, dlb, dlb4, dlb16, lse_b, lse4, lse16) = _tail(
        oa[0], ob1[0], lb1[0], ob4, lb4, ob16, lb16, proj, x2, tgt, wo_full, bones)

    dqa, dka, dva, dsink = _attn_bwd("attn_a_bwd", qa[None], ka[None], va[None], doa[None], la, dla[None], sink_rows, BLOCK - 1)
    dq1, dk1, dv1 = _attn_bwd("attn_b1_bwd", qb[None], kb[None], vb[None], dob[None], lse_b[None], dlb[None], None, BLOCK)
    dq4, dk4, dv4 = _attn_bwd("attn_b4_bwd", qb4, kb4, vb4, dob4, lse4, dlb4, None, BLOCK)
    dq16, dk16, dv16 = _attn_bwd("attn_b16_bwd", qb16, kb16, vb16, dob16, lse16, dlb16, None, BLOCK)
    dproj, dqkg = _dproj_assemble(dqa[0], dka[0], dva[0], dga, dgb, dq1[0], dk1[0], dv1[0], dq4, dk4, dv4,
                                  dq16, dk16, dv16, proj, qkg, cos4, sin4, bmean)
    grad_x, dgain = _input_grad(dproj, w_full, x2, norm_gain, dy)
    gw_in = _from_kernel_order(_weight_grad(h_t, dproj))
    gw_out = jnp.concatenate([_unpair_heads(gwo[:HALF_WIDTH], 0), gwo[HALF_WIDTH:]], axis=0)

    blocks = jnp.concatenate([gw_in.reshape(D_MODEL, N_DEV, SHARD_IN).transpose(1, 0, 2),
                              gw_out.reshape(N_DEV, SHARD_OUT, D_MODEL).reshape(N_DEV, D_MODEL, SHARD_OUT)], axis=2)
    g_sinks = jnp.concatenate([jnp.sum(dsink[:N_PAIRS, :BLOCK], axis=1), jnp.sum(dsink[:N_PAIRS, BLOCK:], axis=1)])
    small = _pack_small(dgain, _fold_heads(dqkg[0:1]), _fold_heads(dqkg[1:2]), g_sinks,
                        _fold_heads(dqkg[2:3]), _fold_heads(dqkg[3:4]), extra=0.5 * jnp.sum(loss_cols) / D_MODEL)
    reduced, small_red = _reduce_scatter_grads(blocks, small)
    g_w_in = reduced[:, :SHARD_IN]
    g_w_out = reduced[:, SHARD_IN:].reshape(SHARD_OUT, D_MODEL)
    g_small = _unpack_small(small_red)

    d_in, nm_in, nv_in = _adamw("adamw_w_in", w_in_sh, g_w_in, m_w_in[0], v_w_in[0])
    d_out, nm_out, nv_out = _adamw("adamw_w_out", w_out_sh, g_w_out, m_w_out[0], v_w_out[0])
    d_s, nm_s, nv_s = _adamw(
        "adamw_small",
        _pack_small(norm_gain, q_norm_a, k_norm_a, sinks_a, q_norm_b, k_norm_b), small_red,
        _pack_small(m_norm_gain, m_q_norm_a, m_k_norm_a, m_sinks_a, m_q_norm_b, m_k_norm_b),
        _pack_small(v_norm_gain, v_q_norm_a, v_k_norm_a, v_sinks_a, v_q_norm_b, v_k_norm_b))
    d_small, nm_small, nv_small = _unpack_small(d_s), _unpack_small(nm_s), _unpack_small(nv_s)

    loss = small_red.reshape(-1)[SMALL_USED]

    def assemble(small_list, big_in, big_out):
        ng, qa_, ka_, sk_, qb_, kb_ = small_list
        return [ng, big_in[None], qa_, ka_, sk_, qb_, kb_, big_out[None]]

    return (loss, grad_x[None], *assemble(g_small, g_w_in, g_w_out), *assemble(d_small, d_in, d_out),
            *assemble(nm_small, nm_in, nm_out), *assemble(nv_small, nv_in, nv_out))
```

```python
import functools

import numpy as np
import jax
import jax.numpy as jnp
from jax import lax
from jax.experimental import pallas as pl
from jax.experimental.pallas import tpu as pltpu

F32 = jnp.float32
BF16 = jnp.bfloat16

SEQ = 4096
D_MODEL = 1024
HEAD_DIM = 64
PAIR = 2 * HEAD_DIM
N_PAIRS = 4
HALF_WIDTH = N_PAIRS * PAIR
KV_A_WIDTH = 128
IN_WIDTH = 3328
BLOCK = 128
EPS = 1e-6
NEG = -1e30
ROPE_THETA = 10000.0
N_DEV = 8
SHARD_IN = IN_WIDTH // N_DEV
SHARD_OUT = D_MODEL // N_DEV
PAYLOAD = SHARD_IN + SHARD_OUT
SMALL_ROWS, SMALL_COLS = 8, 256

C_QA, C_GA, C_QB, C_KB, C_VB, C_GB, C_KA, C_VA = 0, 512, 1024, 1536, 2048, 2560, 3072, 3200

ADAM_LR = 0.001
ADAM_B1 = 0.9
ADAM_B2 = 0.999
ADAM_EPS = 1e-08
ADAM_WD = 0.01
ADAM_STEP = 10

ROW_TILE = 256
VMEM_LIMIT = 56 * 1024 * 1024

MESH = pl.DeviceIdType.MESH


def _params(sem, vmem=VMEM_LIMIT):
    return pltpu.CompilerParams(dimension_semantics=sem, vmem_limit_bytes=vmem)


def _head_sum(v, bm):
    hi = v.astype(BF16)
    lo = (v - hi.astype(F32)).astype(BF16)
    return (jnp.dot(hi, bm, preferred_element_type=F32) + jnp.dot(lo, bm, preferred_element_type=F32))


def _swap_halves(y):
    lane = lax.broadcasted_iota(jnp.int32, y.shape, 1)
    first = (lane & 32) == 0
    return jnp.where(first, pltpu.roll(y, 96, 1), pltpu.roll(y, 32, 1))


def _sigmoid(g):
    return 1.0 / (1.0 + jnp.exp(-g))


def _all_gather_weights(payload):
    rows, cols = payload.shape

    def body(x_ref, out_ref, mine_ref, send_sems, recv_sems):
        x, y, c = lax.axis_index("x"), lax.axis_index("y"), lax.axis_index("c")
        me, sibling = (x, y, c), (x, y, 1 - c)
        chips = [(1 - x, y), (x, 1 - y), (1 - x, 1 - y)]

        def slot(px, py, pc):
            return out_ref.at[4 * px + 2 * py + pc]

        def copy(k, block, to, src=None):
            return pltpu.make_async_remote_copy(
                src_ref=slot(*block) if src is None else src, dst_ref=slot(*block),
                send_sem=send_sems.at[k], recv_sem=recv_sems.at[k], device_id=to, device_id_type=MESH)

        mine_ref[...] = x_ref[...].astype(BF16)
        first = [copy(0, me, sibling, src=mine_ref)]
        first += [copy(1 + j, me, (*chip, c), src=mine_ref) for j, chip in enumerate(chips)]
        for cp in first:
            cp.start()
        slot(*me)[...] = mine_ref[...]
        passed = [copy(4 + j, (*chip, c), sibling) for j, chip in enumerate(chips)]
        for j, chip in enumerate(chips):
            copy(1 + j, (*chip, c), me).wait_recv()
            passed[j].start()
        copy(0, sibling, me).wait_recv()
        for j, chip in enumerate(chips):
            copy(4 + j, (*chip, 1 - c), me).wait_recv()
        for cp in first + passed:
            cp.wait_send()

    return pl.pallas_call(
        body, name="ag_weights",
        out_shape=jax.ShapeDtypeStruct((N_DEV, rows, cols), BF16),
        in_specs=[pl.BlockSpec(memory_space=pltpu.VMEM)],
        out_specs=pl.BlockSpec(memory_space=pltpu.VMEM),
        scratch_shapes=[pltpu.VMEM((rows, cols), BF16),
                        pltpu.SemaphoreType.DMA((7,)), pltpu.SemaphoreType.DMA((7,))],
        compiler_params=pltpu.CompilerParams(vmem_limit_bytes=VMEM_LIMIT),
    )(payload)


def _reduce_scatter_grads(blocks, small):
    _, rows, cols = blocks.shape

    def body(g_hbm, small_ref, out_ref, small_out_ref, part, from_sib, to_wire, from_chips, small_all,
             load_sems, sib_send, sib_recv, chip_send, chip_recv, small_send, small_recv):
        x, y, c = lax.axis_index("x"), lax.axis_index("y"), lax.axis_index("c")
        sibling = (x, y, 1 - c)
        chips = [(x, y), (1 - x, y), (x, 1 - y), (1 - x, 1 - y)]
        my_id = 4 * x + 2 * y + c

        def blk(chip, core):
            return g_hbm.at[4 * chip[0] + 2 * chip[1] + core]

        small_all[my_id] = small_ref[...]
        small_copies = []
        for rel in range(1, N_DEV):
            dx, dy, dc = (rel >> 2) & 1, (rel >> 1) & 1, rel & 1
            to = (1 - x if dx else x, 1 - y if dy else y, 1 - c if dc else c)
            small_copies.append(pltpu.make_async_remote_copy(
                src_ref=small_ref, dst_ref=small_all.at[my_id],
                send_sem=small_send.at[rel - 1], recv_sem=small_recv.at[rel - 1], device_id=to, device_id_type=MESH))
        for cp in small_copies:
            cp.start()

        loads = [pltpu.make_async_copy(blk(chips[k], c), part.at[k], load_sems.at[k]) for k in range(4)]
        for cp in loads:
            cp.start()
        to_sib = [pltpu.make_async_remote_copy(
            src_ref=blk(chips[k], 1 - c), dst_ref=from_sib.at[k], send_sem=sib_send.at[k], recv_sem=sib_recv.at[k],
            device_id=sibling, device_id_type=MESH) for k in range(4)]
        for cp in to_sib:
            cp.start()

        to_chips = [pltpu.make_async_remote_copy(
            src_ref=to_wire.at[k - 1], dst_ref=from_chips.at[k - 1], send_sem=chip_send.at[k - 1], recv_sem=chip_recv.at[k - 1],
            device_id=(*chips[k], c), device_id_type=MESH) for k in range(1, 4)]
        for k in (1, 2, 3):
            loads[k].wait()
            to_sib[k].wait_recv()
            to_wire[k - 1] = (part[k] + from_sib[k]).astype(BF16)
            to_chips[k - 1].start()
        loads[0].wait()
        to_sib[0].wait_recv()
        acc = part[0] + from_sib[0]
        for k in range(3):
            to_chips[k].wait_recv()
            acc = acc + from_chips[k].astype(F32)
        out_ref[...] = acc

        for cp in small_copies:
            cp.wait_recv()
        tot = small_all[0]
        for d in range(1, N_DEV):
            tot = tot + small_all[d]
        small_out_ref[...] = tot
        for cp in to_sib + to_chips + small_copies:
            cp.wait_send()

    vmem = pl.BlockSpec(memory_space=pltpu.VMEM)
    return pl.pallas_call(
        body, name="rs_grads",
        out_shape=(jax.ShapeDtypeStruct((rows, cols), F32), jax.ShapeDtypeStruct((SMALL_ROWS, SMALL_COLS), F32)),
        in_specs=[pl.BlockSpec(memory_space=pl.ANY), vmem],
        out_specs=(vmem, vmem),
        scratch_shapes=[pltpu.VMEM((4, rows, cols), F32), pltpu.VMEM((4, rows, cols), F32),
                        pltpu.VMEM((3, rows, cols), BF16), pltpu.VMEM((3, rows, cols), BF16),
                        pltpu.VMEM((N_DEV, SMALL_ROWS, SMALL_COLS), F32),
                        pltpu.SemaphoreType.DMA((4,)), pltpu.SemaphoreType.DMA((4,)), pltpu.SemaphoreType.DMA((4,)),
                        pltpu.SemaphoreType.DMA((3,)), pltpu.SemaphoreType.DMA((3,)),
                        pltpu.SemaphoreType.DMA((7,)), pltpu.SemaphoreType.DMA((7,))],
        compiler_params=pltpu.CompilerParams(vmem_limit_bytes=VMEM_LIMIT),
    )(blocks, small)


def _fold_scratch(tm):
    return pltpu.VMEM((N_PAIRS, tm, PAIR), F32)


def _fold_store(val, scr, out4, out16, tm):
    for j in range(N_PAIRS):
        scr[j] = val[:, j * PAIR:(j + 1) * PAIR]
    for dil, out in ((4, out4), (16, out16)):
        for r in range(dil):
            for j in range(N_PAIRS):
                out[r, :, j * PAIR:(j + 1) * PAIR] = scr[j, pl.ds(r, tm // dil, stride=dil), :].astype(out.dtype)


def _unfold_load(src, scr, dil, tm):
    for r in range(dil):
        for j in range(N_PAIRS):
            scr[j, pl.ds(r, tm // dil, stride=dil), :] = src[r, :, j * PAIR:(j + 1) * PAIR].astype(F32)
    return jnp.concatenate([scr[j] for j in range(N_PAIRS)], axis=1)


def _fold_specs(tm, dtype):
    shapes = (jax.ShapeDtypeStruct((4, SEQ // 4, HALF_WIDTH), dtype), jax.ShapeDtypeStruct((16, SEQ // 16, HALF_WIDTH), dtype))
    specs = (pl.BlockSpec((4, tm // 4, HALF_WIDTH), lambda i: (0, i, 0)),
             pl.BlockSpec((16, tm // 16, HALF_WIDTH), lambda i: (0, i, 0)))
    return shapes, specs


def _proj_fwd(x, gain, w, qkg, cos4, sin4, bmean):
    tm = ROW_TILE

    def norm_rope(t, g, cos, sin, bm, scale):
        rr = lax.rsqrt(_head_sum(t * t, bm) + EPS)
        yv = t * rr * g
        return (yv * cos + _swap_halves(yv) * sin) * scale

    def body(x_ref, g_ref, w_ref, qkg_ref, cos_ref, sin_ref, bm_ref,
             proj_ref, ht_ref, qa_ref, ka_ref, va_ref, qb_ref, kb_ref, vb_ref,
             qb4_ref, qb16_ref, kb4_ref, kb16_ref, vb4_ref, vb16_ref, scr):
        xf = x_ref[...]
        r = lax.rsqrt(jnp.mean(xf * xf, axis=-1, keepdims=True) + EPS)
        hf = xf * r * g_ref[...]
        h = hf.astype(BF16)
        ht_ref[...] = hf.T.astype(BF16)
        cos, sin, bm = cos_ref[...], sin_ref[...], bm_ref[...]

        def section(c0, width):
            p = jnp.dot(h, w_ref[:, c0:c0 + width], preferred_element_type=F32)
            proj_ref[:, c0:c0 + width] = p
            return p

        def roped(p, row, scale):
            g = qkg_ref[row:row + 1, :]
            return jnp.concatenate(
                [norm_rope(p[:, j * PAIR:(j + 1) * PAIR], g, cos, sin, bm, scale) for j in range(p.shape[1] // PAIR)], axis=1)

        qa_ref[...] = roped(section(C_QA, HALF_WIDTH), 0, HEAD_DIM ** -0.5).astype(BF16)
        section(C_GA, HALF_WIDTH)
        qb = roped(section(C_QB, HALF_WIDTH), 2, HEAD_DIM ** -0.5)
        qb_ref[...] = qb.astype(BF16)
        _fold_store(qb, scr, qb4_ref, qb16_ref, tm)
        kb = roped(section(C_KB, HALF_WIDTH), 3, 1.0)
        kb_ref[...] = kb.astype(BF16)
        _fold_store(kb, scr, kb4_ref, kb16_ref, tm)
        vb = section(C_VB, HALF_WIDTH)
        vb_ref[...] = vb.astype(BF16)
        _fold_store(vb, scr, vb4_ref, vb16_ref, tm)
        section(C_GB, HALF_WIDTH)
        kva = section(C_KA, 2 * KV_A_WIDTH)
        ka_ref[...] = roped(kva[:, :KV_A_WIDTH], 1, 1.0).astype(BF16)
        va_ref[...] = kva[:, KV_A_WIDTH:].astype(BF16)

    row = lambda width: pl.BlockSpec((tm, width), lambda i: (i, 0))
    full = lambda a: pl.BlockSpec(a.shape, lambda i: (0,) * a.ndim)
    nat = lambda width: jax.ShapeDtypeStruct((SEQ, width), BF16)
    f_shapes, f_specs = _fold_specs(tm, BF16)
    return pl.pallas_call(
        body, name="proj_fwd", grid=(SEQ // tm,),
        in_specs=[row(D_MODEL), full(gain), full(w), full(qkg), row(PAIR), row(PAIR), full(bmean)],
        out_specs=(row(IN_WIDTH), pl.BlockSpec((D_MODEL, tm), lambda i: (0, i)),
                   row(HALF_WIDTH), row(KV_A_WIDTH), row(KV_A_WIDTH), row(HALF_WIDTH), row(HALF_WIDTH), row(HALF_WIDTH),
                   *f_specs, *f_specs, *f_specs),
        out_shape=(jax.ShapeDtypeStruct((SEQ, IN_WIDTH), F32), jax.ShapeDtypeStruct((D_MODEL, SEQ), BF16),
                   nat(HALF_WIDTH), nat(KV_A_WIDTH), nat(KV_A_WIDTH), nat(HALF_WIDTH), nat(HALF_WIDTH), nat(HALF_WIDTH),
                   *f_shapes, *f_shapes, *f_shapes),
        scratch_shapes=[_fold_scratch(tm)],
        compiler_params=_params(("arbitrary",)),
    )(x, gain, w, qkg, cos4, sin4, bmean)


def _band_mask(i, max_dist):
    j = lax.broadcasted_iota(jnp.int32, (2 * BLOCK, 2 * BLOCK), 0)
    c = lax.broadcasted_iota(jnp.int32, (2 * BLOCK, 2 * BLOCK), 1)
    dist = (c & (BLOCK - 1)) + BLOCK - j
    return (dist >= 0) & (dist <= max_dist) & ((j >= BLOCK) | (i > 0))


def _stack_heads(t):
    lane = lax.broadcasted_iota(jnp.int32, t.shape, 1)
    low = lane < HEAD_DIM
    zero = jnp.zeros_like(t)
    return jnp.concatenate([jnp.where(low, t, zero), jnp.where(low, zero, t)], axis=0)


def _unstack_t(t):
    return jnp.concatenate([t[:HEAD_DIM, :BLOCK], t[HEAD_DIM:, BLOCK:]], axis=0).T


def _rows_to_pair(row):
    return jnp.concatenate([jnp.broadcast_to(row[:, :BLOCK], (HEAD_DIM, BLOCK)),
                            jnp.broadcast_to(row[:, BLOCK:], (HEAD_DIM, BLOCK))], axis=0).T


def _pair_to_rows(t):
    tt = t.T
    return jnp.concatenate([tt[0:1, :], tt[HEAD_DIM:HEAD_DIM + 1, :]], axis=1)


def _attn_fwd(name, q, k, v, sink_rows, max_dist):
    n_seq, length, _ = q.shape
    ck = k.shape[2]
    nb = length // BLOCK
    shared = ck == PAIR
    has_sinks = sink_rows is not None

    def body(*refs):
        if has_sinks:
            q_ref, kc_ref, vc_ref, sink_ref, o_ref, lse_ref, kp_ref, vp_ref = refs
        else:
            q_ref, kc_ref, vc_ref, o_ref, lse_ref, kp_ref, vp_ref = refs
        i = pl.program_id(1)

        @pl.when(i == 0)
        def _():
            kp_ref[...] = jnp.zeros_like(kp_ref)
            vp_ref[...] = jnp.zeros_like(vp_ref)

        valid = _band_mask(i, max_dist)
        pairs = range(N_PAIRS)
        cols = [slice(p * PAIR, (p + 1) * PAIR) for p in pairs]
        kcols = [slice(0, PAIR) if shared else cols[p] for p in pairs]
        st = [lax.dot_general(jnp.concatenate([kp_ref[:, kcols[p]], kc_ref[:, kcols[p]]], axis=0), _stack_heads(q_ref[:, cols[p]]),
                              (((1,), (1,)), ((), ())), preferred_element_type=F32) for p in pairs]
        st = [jnp.where(valid, s, NEG) for s in st]
        m = [jnp.max(s, axis=0, keepdims=True) for s in st]
        if has_sinks:
            sk = [sink_ref[p:p + 1, :] for p in pairs]
            m = [jnp.maximum(m[p], sk[p]) for p in pairs]
        pt = [jnp.exp(st[p] - m[p]) for p in pairs]
        l = [jnp.sum(t, axis=0, keepdims=True) for t in pt]
        if has_sinks:
            l = [l[p] + jnp.exp(sk[p] - m[p]) for p in pairs]
        v2t = [jnp.concatenate([vp_ref[:, kcols[p]], vc_ref[:, kcols[p]]], axis=0).astype(F32).T.astype(BF16) for p in pairs]
        ot = [jnp.dot(v2t[p], pt[p].astype(BF16), preferred_element_type=F32) / l[p] for p in pairs]
        for p in pairs:
            o_ref[:, cols[p]] = _unstack_t(ot[p]).astype(BF16)
            lse_ref[:, cols[p]] = _rows_to_pair(m[p] + jnp.log(l[p]))
        kp_ref[...] = kc_ref[...]
        vp_ref[...] = vc_ref[...]

    cur = lambda width: pl.BlockSpec((None, BLOCK, width), lambda r, i: (r, i, 0))
    in_specs = [cur(HALF_WIDTH), cur(ck), cur(ck)]
    args = [q, k, v]
    if has_sinks:
        in_specs.append(pl.BlockSpec(sink_rows.shape, lambda r, i: (0, 0)))
        args.append(sink_rows)
    out = lambda dtype: jax.ShapeDtypeStruct((n_seq, length, HALF_WIDTH), dtype)
    return pl.pallas_call(
        body, name=name, grid=(n_seq, nb), in_specs=in_specs,
        out_specs=(cur(HALF_WIDTH), cur(HALF_WIDTH)), out_shape=(out(BF16), out(F32)),
        scratch_shapes=[pltpu.VMEM((BLOCK, ck), BF16), pltpu.VMEM((BLOCK, ck), BF16)],
        compiler_params=_params(("arbitrary", "arbitrary")),
    )(*args)


def _attn_bwd(name, q, k, v, d_o, lse, delta, sink_rows, max_dist):
    n_seq, length, _ = q.shape
    ck = k.shape[2]
    nb = length // BLOCK
    shared = ck == PAIR
    has_sinks = sink_rows is not None

    def body(*refs):
        if has_sinks:
            (q_ref, kc_ref, vc_ref, do_ref, lse_ref, dl_ref, sink_ref,
             dq_ref, dk_ref, dv_ref, dsink_ref, ck_scr, cv_scr, kp_ref, vp_ref) = refs
        else:
            (q_ref, kc_ref, vc_ref, do_ref, lse_ref, dl_ref,
             dq_ref, dk_ref, dv_ref, ck_scr, cv_scr, kp_ref, vp_ref) = refs
        r_id, i = pl.program_id(0), pl.program_id(1)

        @pl.when(i == 0)
        def _():
            ck_scr[...] = jnp.zeros_like(ck_scr)
            cv_scr[...] = jnp.zeros_like(cv_scr)
            kp_ref[...] = jnp.zeros_like(kp_ref)
            vp_ref[...] = jnp.zeros_like(vp_ref)

        if has_sinks:
            @pl.when((i == 0) & (r_id == 0))
            def _():
                dsink_ref[...] = jnp.zeros_like(dsink_ref)

        @pl.when(i < nb)
        def _():
            valid = _band_mask(i, max_dist)
            pairs = range(N_PAIRS)
            cols = [slice(p * PAIR, (p + 1) * PAIR) for p in pairs]
            kcols = [slice(0, PAIR) if shared else cols[p] for p in pairs]
            nt = (((1,), (1,)), ((), ()))
            q_st = [_stack_heads(q_ref[:, cols[p]]) for p in pairs]
            do_st = [_stack_heads(do_ref[:, cols[p]]) for p in pairs]
            k2 = [jnp.concatenate([kp_ref[:, kcols[p]], kc_ref[:, kcols[p]]], axis=0) for p in pairs]
            v2 = [jnp.concatenate([vp_ref[:, kcols[p]], vc_ref[:, kcols[p]]], axis=0) for p in pairs]
            st = [lax.dot_general(k2[p], q_st[p], nt, preferred_element_type=F32) for p in pairs]
            dpt = [lax.dot_general(v2[p], do_st[p], nt, preferred_element_type=F32) for p in pairs]
            lse_row = [_pair_to_rows(lse_ref[:, cols[p]]) for p in pairs]
            dl_row = [_pair_to_rows(dl_ref[:, cols[p]]) for p in pairs]
            pt = [jnp.exp(jnp.where(valid, st[p], NEG) - lse_row[p]) for p in pairs]
            dst = [(pt[p] * (dpt[p] - dl_row[p])).astype(BF16) for p in pairs]
            ptb = [t.astype(BF16) for t in pt]
            dv2 = [jnp.dot(ptb[p], do_st[p], preferred_element_type=F32) for p in pairs]
            dk2 = [jnp.dot(dst[p], q_st[p], preferred_element_type=F32) for p in pairs]
            k2t = [k2[p].astype(F32).T.astype(BF16) for p in pairs]
            dqt = [jnp.dot(k2t[p], dst[p], preferred_element_type=F32) for p in pairs]
            for p in pairs:
                dq_ref[:, cols[p]] = _unstack_t(dqt[p]).astype(BF16)
            if has_sinks:
                for p in pairs:
                    p_sink = jnp.exp(sink_ref[p:p + 1, :] - lse_row[p])
                    dsink_ref[p:p + 1, :] = dsink_ref[p:p + 1, :] - p_sink * dl_row[p]
            if shared:
                dk_acc = (dk2[0] + dk2[1]) + (dk2[2] + dk2[3])
                dv_acc = (dv2[0] + dv2[1]) + (dv2[2] + dv2[3])
                dk_ref[...] = (ck_scr[...] + dk_acc[:BLOCK]).astype(BF16)
                dv_ref[...] = (cv_scr[...] + dv_acc[:BLOCK]).astype(BF16)
                ck_scr[...] = dk_acc[BLOCK:]
                cv_scr[...] = dv_acc[BLOCK:]
            else:
                for p in pairs:
                    dk_ref[:, cols[p]] = (ck_scr[:, cols[p]] + dk2[p][:BLOCK]).astype(BF16)
                    dv_ref[:, cols[p]] = (cv_scr[:, cols[p]] + dv2[p][:BLOCK]).astype(BF16)
                    ck_scr[:, cols[p]] = dk2[p][BLOCK:]
                    cv_scr[:, cols[p]] = dv2[p][BLOCK:]
            kp_ref[...] = kc_ref[...]
            vp_ref[...] = vc_ref[...]

        @pl.when(i == nb)
        def _():
            dk_ref[...] = ck_scr[...].astype(BF16)
            dv_ref[...] = cv_scr[...].astype(BF16)

    last = nb - 1
    cur = lambda width: pl.BlockSpec((None, BLOCK, width), lambda r, i: (r, jnp.minimum(i, last), 0))
    prev = lambda width: pl.BlockSpec((None, BLOCK, width), lambda r, i: (r, jnp.clip(i - 1, 0, last), 0))
    in_specs = [cur(HALF_WIDTH), cur(ck), cur(ck), cur(HALF_WIDTH), cur(HALF_WIDTH), cur(HALF_WIDTH)]
    args = [q, k, v, d_o, lse, delta]
    out_specs = [cur(HALF_WIDTH), prev(ck), prev(ck)]
    out_shape = [jax.ShapeDtypeStruct((n_seq, length, HALF_WIDTH), BF16),
                 jax.ShapeDtypeStruct((n_seq, length, ck), BF16), jax.ShapeDtypeStruct((n_seq, length, ck), BF16)]
    if has_sinks:
        in_specs.append(pl.BlockSpec(sink_rows.shape, lambda r, i: (0, 0)))
        args.append(sink_rows)
        out_specs.append(pl.BlockSpec(sink_rows.shape, lambda r, i: (0, 0)))
        out_shape.append(jax.ShapeDtypeStruct(sink_rows.shape, F32))
    return pl.pallas_call(
        body, name=name, grid=(n_seq, nb + 1), in_specs=in_specs,
        out_specs=tuple(out_specs), out_shape=tuple(out_shape),
        scratch_shapes=[pltpu.VMEM((BLOCK, ck), F32), pltpu.VMEM((BLOCK, ck), F32),
                        pltpu.VMEM((BLOCK, ck), BF16), pltpu.VMEM((BLOCK, ck), BF16)],
        compiler_params=_params(("arbitrary", "arbitrary")),
    )(*args)


def _tail(oa, ob1, lb1, ob4, lb4, ob16, lb16, proj, x, target, w_out, bones):
    tm = ROW_TILE

    def body(oa_ref, ob1_ref, lb1_ref, ob4_ref, lb4_ref, ob16_ref, lb16_ref, ga_ref, gb_ref, x_ref, t_ref, w_ref, bo_ref,
             loss_ref, dy_ref, gwo_ref, doa_ref, dla_ref, dga_ref, dgb_ref,
             dob_ref, dob4_ref, dob16_ref, dlb_ref, dlb4_ref, dlb16_ref, lse_ref, lse4_ref, lse16_ref,
             s_f):
        i = pl.program_id(0)
        o4, o16 = _unfold_load(ob4_ref, s_f, 4, tm), _unfold_load(ob16_ref, s_f, 16, tm)
        l4, l16 = _unfold_load(lb4_ref, s_f, 4, tm), _unfold_load(lb16_ref, s_f, 16, tm)
        o1, l1 = ob1_ref[...].astype(F32), lb1_ref[...]
        mx = jnp.maximum(jnp.maximum(l1, l4), l16)
        e1, e4, e16 = jnp.exp(l1 - mx), jnp.exp(l4 - mx), jnp.exp(l16 - mx)
        den = e1 + e4 + e16
        ob = (e1 * o1 + e4 * o4 + e16 * o16) / den
        lse_b = mx + jnp.log(den)

        oa, ga, gb = oa_ref[...].astype(F32), ga_ref[...], gb_ref[...]
        sa, sb = _sigmoid(ga), _sigmoid(gb)
        mixed = jnp.concatenate([oa * (ga * sa), ob * (gb * sb)], axis=1)
        mixed_bf = mixed.astype(BF16)
        w = w_ref[...]
        yv = x_ref[...] + jnp.dot(mixed_bf, w, preferred_element_type=F32)
        err = yv - t_ref[...]
        sq = jnp.sum(err * err, axis=0, keepdims=True)
        dy = err * (1.0 / D_MODEL)
        dy_ref[...] = dy
        dy_bf = dy.astype(BF16)
        gw = jnp.dot(mixed.T.astype(BF16), dy_bf, preferred_element_type=F32)

        @pl.when(i == 0)
        def _():
            loss_ref[...] = sq
            gwo_ref[...] = gw

        @pl.when(i > 0)
        def _():
            loss_ref[...] += sq
            gwo_ref[...] += gw

        dmix = lax.dot_general(dy_bf, w, (((1,), (1,)), ((), ())), preferred_element_type=F32)
        dma, dmb = dmix[:, :HALF_WIDTH], dmix[:, HALF_WIDTH:]
        bo = bo_ref[...]

        def head_delta(d_o, o):
            prod = d_o * o
            return jnp.concatenate([_head_sum(prod[:, j * PAIR:(j + 1) * PAIR], bo) for j in range(N_PAIRS)], axis=1)

        doa = dma * (ga * sa)
        doa_ref[...] = doa.astype(BF16)
        dla_ref[...] = head_delta(doa, oa)
        dga_ref[...] = dma * oa * (sa * (1.0 + ga * (1.0 - sa)))
        dob = dmb * (gb * sb)
        dgb_ref[...] = dmb * ob * (sb * (1.0 + gb * (1.0 - sb)))
        dlb = head_delta(dob, ob)
        dob_ref[...] = dob.astype(BF16)
        _fold_store(dob, s_f, dob4_ref, dob16_ref, tm)
        dlb_ref[...] = dlb
        _fold_store(dlb, s_f, dlb4_ref, dlb16_ref, tm)
        lse_ref[...] = lse_b
        _fold_store(lse_b, s_f, lse4_ref, lse16_ref, tm)

    row = lambda width: pl.BlockSpec((tm, width), lambda i: (i, 0))
    col_block = lambda c0: pl.BlockSpec((tm, HALF_WIDTH), lambda i: (i, c0 // HALF_WIDTH))
    full = lambda a: pl.BlockSpec(a.shape, lambda i: (0,) * a.ndim)
    fb_shapes, fb_specs = _fold_specs(tm, BF16)
    ff_shapes, ff_specs = _fold_specs(tm, F32)
    nat = lambda dtype: jax.ShapeDtypeStruct((SEQ, HALF_WIDTH), dtype)
    return pl.pallas_call(
        body, name="tail", grid=(SEQ // tm,),
        in_specs=[row(HALF_WIDTH), row(HALF_WIDTH), row(HALF_WIDTH), ff_specs[0], ff_specs[0], ff_specs[1], ff_specs[1],
                  col_block(C_GA), col_block(C_GB), row(D_MODEL), row(D_MODEL), full(w_out), full(bones)],
        out_specs=(pl.BlockSpec((1, D_MODEL), lambda i: (0, 0)), row(D_MODEL),
                   pl.BlockSpec((D_MODEL, D_MODEL), lambda i: (0, 0)),
                   row(HALF_WIDTH), row(HALF_WIDTH), row(HALF_WIDTH), row(HALF_WIDTH),
                   row(HALF_WIDTH), *fb_specs, row(HALF_WIDTH), *ff_specs, row(HALF_WIDTH), *ff_specs),
        out_shape=(jax.ShapeDtypeStruct((1, D_MODEL), F32), jax.ShapeDtypeStruct((SEQ, D_MODEL), F32),
                   jax.ShapeDtypeStruct((D_MODEL, D_MODEL), F32),
                   nat(BF16), nat(F32), nat(F32), nat(F32),
                   nat(BF16), *fb_shapes, nat(F32), *ff_shapes, nat(F32), *ff_shapes),
        scratch_shapes=[_fold_scratch(tm)],
        compiler_params=_params(("arbitrary",)),
    )(oa, ob1, lb1, ob4, lb4, ob16, lb16, proj, proj, x, target, w_out, bones)


def _dproj_assemble(dqa, dka, dva, dga, dgb, dq1, dk1, dv1, dq4, dk4, dv4, dq16, dk16, dv16, proj, qkg, cos4, sin4, bmean):
    tm = ROW_TILE

    def norm_rope_bwd(d_out, t, g, cos, sin, bm, scale):
        d_r = d_out * scale
        dyv = d_r * cos + _swap_halves(d_r * sin)
        rr = lax.rsqrt(_head_sum(t * t, bm) + EPS)
        that = t * rr
        dgain = jnp.sum(dyv * that, axis=0, keepdims=True)
        gdy = dyv * g
        dt = rr * (gdy - that * _head_sum(that * gdy, bm))
        return dt, dgain

    def body(dqa_ref, dka_ref, dva_ref, dga_ref, dgb_ref, dq1_ref, dk1_ref, dv1_ref, dq4_ref, dk4_ref, dv4_ref,
             dq16_ref, dk16_ref, dv16_ref, tqa_ref, tqb_ref, tkb_ref, tka_ref, qkg_ref, cos_ref, sin_ref, bm_ref,
             dproj_ref, dqkg_ref, s_f):
        i = pl.program_id(0)
        cos, sin, bm = cos_ref[...], sin_ref[...], bm_ref[...]

        def merged(nat_ref, f4_ref, f16_ref):
            return nat_ref[...].astype(F32) + _unfold_load(f4_ref, s_f, 4, tm) + _unfold_load(f16_ref, s_f, 16, tm)

        @pl.when(i == 0)
        def _():
            dqkg_ref[...] = jnp.zeros_like(dqkg_ref)

        def through(d_out, t, row, scale, c0):
            g = qkg_ref[row:row + 1, :]
            tot = jnp.zeros((1, PAIR), F32)
            for j in range(d_out.shape[1] // PAIR):
                cols = slice(j * PAIR, (j + 1) * PAIR)
                dt, dg = norm_rope_bwd(d_out[:, cols], t[:, cols], g, cos, sin, bm, scale)
                dproj_ref[:, c0 + j * PAIR:c0 + (j + 1) * PAIR] = dt.astype(BF16)
                tot = tot + dg
            dqkg_ref[row:row + 1, :] += tot

        through(dqa_ref[...].astype(F32), tqa_ref[...], 0, HEAD_DIM ** -0.5, C_QA)
        through(dka_ref[...].astype(F32), tka_ref[...], 1, 1.0, C_KA)
        through(merged(dq1_ref, dq4_ref, dq16_ref), tqb_ref[...], 2, HEAD_DIM ** -0.5, C_QB)
        through(merged(dk1_ref, dk4_ref, dk16_ref), tkb_ref[...], 3, 1.0, C_KB)
        dproj_ref[:, C_VB:C_VB + HALF_WIDTH] = merged(dv1_ref, dv4_ref, dv16_ref).astype(BF16)
        dproj_ref[:, C_GA:C_GA + HALF_WIDTH] = dga_ref[...].astype(BF16)
        dproj_ref[:, C_GB:C_GB + HALF_WIDTH] = dgb_ref[...].astype(BF16)
        dproj_ref[:, C_VA:C_VA + KV_A_WIDTH] = dva_ref[...].astype(BF16)

    row = lambda width: pl.BlockSpec((tm, width), lambda i: (i, 0))
    col_block = lambda c0, width: pl.BlockSpec((tm, width), lambda i: (i, c0 // width))
    full = lambda a: pl.BlockSpec(a.shape, lambda i: (0,) * a.ndim)
    _, ff_specs = _fold_specs(tm, F32)
    return pl.pallas_call(
        body, name="dproj_assemble", grid=(SEQ // tm,),
        in_specs=[row(HALF_WIDTH), row(KV_A_WIDTH), row(KV_A_WIDTH), row(HALF_WIDTH), row(HALF_WIDTH),
                  row(HALF_WIDTH), row(HALF_WIDTH), row(HALF_WIDTH), ff_specs[0], ff_specs[0], ff_specs[0],
                  ff_specs[1], ff_specs[1], ff_specs[1],
                  col_block(C_QA, HALF_WIDTH), col_block(C_QB, HALF_WIDTH), col_block(C_KB, HALF_WIDTH),
                  col_block(C_KA, KV_A_WIDTH), full(qkg), row(PAIR), row(PAIR), full(bmean)],
        out_specs=(row(IN_WIDTH), pl.BlockSpec((SMALL_ROWS, PAIR), lambda i: (0, 0))),
        out_shape=(jax.ShapeDtypeStruct((SEQ, IN_WIDTH), BF16), jax.ShapeDtypeStruct((SMALL_ROWS, PAIR), F32)),
        scratch_shapes=[_fold_scratch(tm)],
        compiler_params=_params(("arbitrary",)),
    )(dqa, dka, dva, dga, dgb, dq1, dk1, dv1, dq4, dk4, dv4, dq16, dk16, dv16, proj, proj, proj, proj, qkg, cos4, sin4, bmean)


def _input_grad(dproj, w, x, gain, dy):
    tm = ROW_TILE

    def body(dp_ref, w_ref, x_ref, g_ref, dy_ref, gx_ref, dgain_ref):
        i = pl.program_id(0)
        dh = lax.dot_general(dp_ref[...], w_ref[...], (((1,), (1,)), ((), ())), preferred_element_type=F32)
        xf = x_ref[...]
        r = lax.rsqrt(jnp.mean(xf * xf, axis=-1, keepdims=True) + EPS)
        xhat = xf * r
        dg = jnp.sum(dh * xhat, axis=0, keepdims=True)
        dxh = dh * g_ref[...]
        dx = r * (dxh - xhat * jnp.mean(dxh * xhat, axis=-1, keepdims=True))
        gx_ref[...] = dy_ref[...] + dx

        @pl.when(i == 0)
        def _():
            dgain_ref[...] = dg

        @pl.when(i > 0)
        def _():
            dgain_ref[...] += dg

    row = lambda width: pl.BlockSpec((tm, width), lambda i: (i, 0))
    full = lambda a: pl.BlockSpec(a.shape, lambda i: (0,) * a.ndim)
    return pl.pallas_call(
        body, name="input_grad", grid=(SEQ // tm,),
        in_specs=[row(IN_WIDTH), full(w), row(D_MODEL), full(gain), row(D_MODEL)],
        out_specs=(row(D_MODEL), pl.BlockSpec((1, D_MODEL), lambda i: (0, 0))),
        out_shape=(jax.ShapeDtypeStruct((SEQ, D_MODEL), F32), jax.ShapeDtypeStruct((1, D_MODEL), F32)),
        compiler_params=_params(("arbitrary",)),
    )(dproj, w, x, gain, dy)


def _weight_grad(h_t, dproj):
    tk = 512
    cb = IN_WIDTH // 2
    n_k = SEQ // tk

    def body(ht_ref, dp_ref, out_ref):
        k = pl.program_id(1)
        upd = jnp.dot(ht_ref[...], dp_ref[...], preferred_element_type=F32)

        @pl.when(k == 0)
        def _():
            out_ref[...] = upd

        @pl.when(k > 0)
        def _():
            out_ref[...] += upd

    return pl.pallas_call(
        body, name="weight_grad", grid=(2, n_k),
        in_specs=[pl.BlockSpec((D_MODEL, tk), lambda j, k: (0, k)), pl.BlockSpec((tk, cb), lambda j, k: (k, j))],
        out_specs=pl.BlockSpec((D_MODEL, cb), lambda j, k: (0, j)),
        out_shape=jax.ShapeDtypeStruct((D_MODEL, IN_WIDTH), F32),
        compiler_params=_params(("arbitrary", "arbitrary")),
    )(h_t, dproj)


def _adamw(name, w, g, m, v):
    def body(w_ref, g_ref, m_ref, v_ref, d_ref, nm_ref, nv_ref):
        gv = g_ref[...]
        nm = ADAM_B1 * m_ref[...] + (1.0 - ADAM_B1) * gv
        nv = ADAM_B2 * v_ref[...] + (1.0 - ADAM_B2) * jnp.square(gv)
        m_hat = nm / (1.0 - ADAM_B1 ** ADAM_STEP)
        v_hat = nv / (1.0 - ADAM_B2 ** ADAM_STEP)
        d_ref[...] = -ADAM_LR * (m_hat / (jnp.sqrt(v_hat) + ADAM_EPS) + ADAM_WD * w_ref[...])
        nm_ref[...] = nm
        nv_ref[...] = nv

    vmem = pl.BlockSpec(memory_space=pltpu.VMEM)
    out = jax.ShapeDtypeStruct(w.shape, F32)
    return pl.pallas_call(
        body, name=name, in_specs=[vmem] * 4, out_specs=(vmem,) * 3, out_shape=(out,) * 3,
        compiler_params=pltpu.CompilerParams(vmem_limit_bytes=VMEM_LIMIT),
    )(w, g, m, v)


def _pair_heads(a, axis):
    a = jnp.moveaxis(a, axis, 0)
    rest = a.shape[1:]
    a = a.reshape((2, 4, HEAD_DIM) + rest).transpose((1, 0, 2) + tuple(range(3, 3 + len(rest))))
    return jnp.moveaxis(a.reshape((HALF_WIDTH,) + rest), 0, axis)


def _unpair_heads(a, axis):
    a = jnp.moveaxis(a, axis, 0)
    rest = a.shape[1:]
    a = a.reshape((4, 2, HEAD_DIM) + rest).transpose((1, 0, 2) + tuple(range(3, 3 + len(rest))))
    return jnp.moveaxis(a.reshape((HALF_WIDTH,) + rest), 0, axis)


def _to_kernel_order(w):
    qa, ka, va, ga = w[:, 0:512], w[:, 512:640], w[:, 640:768], w[:, 768:1280]
    qb, kb, vb, gb = w[:, 1280:1792], w[:, 1792:2304], w[:, 2304:2816], w[:, 2816:3328]
    return jnp.concatenate([_pair_heads(qa, 1), _pair_heads(ga, 1), qb, kb, vb, gb, ka, va], axis=1)


def _from_kernel_order(g):
    qa, ga = _unpair_heads(g[:, C_QA:C_QA + 512], 1), _unpair_heads(g[:, C_GA:C_GA + 512], 1)
    qb, kb, vb, gb = g[:, C_QB:C_QB + 512], g[:, C_KB:C_KB + 512], g[:, C_VB:C_VB + 512], g[:, C_GB:C_GB + 512]
    ka, va = g[:, C_KA:C_KA + 128], g[:, C_VA:C_VA + 128]
    return jnp.concatenate([qa, ka, va, ga, qb, kb, vb, gb], axis=1)


SMALL_USED = D_MODEL + 4 * HEAD_DIM + 8


def _pack_small(norm_gain, qa, ka, sinks, qb, kb, extra=None):
    parts = [norm_gain.reshape(-1), qa.reshape(-1), ka.reshape(-1), sinks.reshape(-1), qb.reshape(-1), kb.reshape(-1)]
    if extra is not None:
        parts.append(extra.reshape(-1))
    flat = jnp.concatenate(parts)
    flat = jnp.pad(flat, (0, SMALL_ROWS * SMALL_COLS - flat.shape[0]))
    return flat.reshape(SMALL_ROWS, SMALL_COLS)


def _unpack_small(a):
    flat = a.reshape(-1)
    sizes = (D_MODEL, HEAD_DIM, HEAD_DIM, 8, HEAD_DIM, HEAD_DIM)
    out, off = [], 0
    for s in sizes:
        out.append(flat[off:off + s].reshape(1, s))
        off += s
    return out


def _fold_heads(row):
    return row[0, :HEAD_DIM] + row[0, HEAD_DIM:]


def kernel(x, norm_gain, w_in, q_norm_a, k_norm_a, sinks_a, q_norm_b, k_norm_b, w_out, loss_target, m_norm_gain, m_w_in, m_q_norm_a, m_k_norm_a, m_sinks_a, m_q_norm_b, m_k_norm_b, m_w_out, v_norm_gain, v_w_in, v_q_norm_a, v_k_norm_a, v_sinks_a, v_q_norm_b, v_k_norm_b, v_w_out):
    x2, tgt = x[0], loss_target[0]
    w_in_sh, w_out_sh = w_in[0], w_out[0]

    payload = jnp.concatenate([w_in_sh, w_out_sh.reshape(D_MODEL, SHARD_OUT)], axis=1)
    gathered = _all_gather_weights(payload)
    w_full = _to_kernel_order(gathered[:, :, :SHARD_IN].transpose(1, 0, 2).reshape(D_MODEL, IN_WIDTH))
    wo_full = gathered[:, :, SHARD_IN:].reshape(D_MODEL, D_MODEL)
    wo_full = jnp.concatenate([_pair_heads(wo_full[:HALF_WIDTH], 0), wo_full[HALF_WIDTH:]], axis=0)

    pos = jnp.arange(SEQ, dtype=F32)
    inv = ROPE_THETA ** (-jnp.arange(HEAD_DIM // 2, dtype=F32) / (HEAD_DIM // 2))
    ang = pos[:, None] * inv[None, :]
    cos, sin = jnp.cos(ang), jnp.sin(ang)
    cos4 = jnp.concatenate([cos, cos, cos, cos], axis=1)
    sin4 = jnp.concatenate([-sin, sin, -sin, sin], axis=1)
    blockdiag = np.kron(np.eye(2, dtype=np.float32), np.ones((HEAD_DIM, HEAD_DIM), np.float32))
    bmean = jnp.asarray(blockdiag / HEAD_DIM, dtype=BF16)
    bones = jnp.asarray(blockdiag, dtype=BF16)
    two = lambda g: jnp.concatenate([g, g], axis=1)
    qkg = jnp.concatenate([two(q_norm_a), two(k_norm_a), two(q_norm_b), two(k_norm_b),
                           jnp.zeros((SMALL_ROWS - 4, PAIR), F32)], axis=0)
    sinks_paired = jnp.stack([sinks_a[0, :N_PAIRS], sinks_a[0, N_PAIRS:]], axis=1)
    sink_rows = jnp.concatenate([jnp.repeat(sinks_paired, BLOCK, axis=1),
                                 jnp.zeros((SMALL_ROWS - N_PAIRS, 2 * BLOCK), F32)], axis=0)

    (proj, h_t, qa, ka, va, qb, kb, vb, qb4, qb16, kb4, kb16, vb4, vb16) = _proj_fwd(
        x2, norm_gain, w_full, qkg, cos4, sin4, bmean)
    oa, la = _attn_fwd("attn_a_fwd", qa[None], ka[None], va[None], sink_rows, BLOCK - 1)
    ob1, lb1 = _attn_fwd("attn_b1_fwd", qb[None], kb[None], vb[None], None, BLOCK)
    ob4, lb4 = _attn_fwd("attn_b4_fwd", qb4, kb4, vb4, None, BLOCK)
    ob16, lb16 = _attn_fwd("attn_b16_fwd", qb16, kb16, vb16, None, BLOCK)
    (loss_cols, dy, gwo, doa, dla, dga, dgb, dob, dob4, dob16, dlb, dlb4, dlb16, lse_b, lse4, lse16) = _tail(
        oa[0], ob1[0], lb1[0], ob4, lb4, ob16, lb16, proj, x2, tgt, wo_full, bones)

    dqa, dka, dva, dsink = _attn_bwd("attn_a_bwd", qa[None], ka[None], va[None], doa[None], la, dla[None], sink_rows, BLOCK - 1)
    dq1, dk1, dv1 = _attn_bwd("attn_b1_bwd", qb[None], kb[None], vb[None], dob[None], lse_b[None], dlb[None], None, BLOCK)
    dq4, dk4, dv4 = _attn_bwd("attn_b4_bwd", qb4, kb4, vb4, dob4, lse4, dlb4, None, BLOCK)
    dq16, dk16, dv16 = _attn_bwd("attn_b16_bwd", qb16, kb16, vb16, dob16, lse16, dlb16, None, BLOCK)
    dproj, dqkg = _dproj_assemble(dqa[0], dka[0], dva[0], dga, dgb, dq1[0], dk1[0], dv1[0], dq4, dk4, dv4,
                                  dq16, dk16, dv16, proj, qkg, cos4, sin4, bmean)
    grad_x, dgain = _input_grad(dproj, w_full, x2, norm_gain, dy)
    gw_in = _from_kernel_order(_weight_grad(h_t, dproj))
    gw_out = jnp.concatenate([_unpair_heads(gwo[:HALF_WIDTH], 0), gwo[HALF_WIDTH:]], axis=0)

    blocks = jnp.concatenate([gw_in.reshape(D_MODEL, N_DEV, SHARD_IN).transpose(1, 0, 2),
                              gw_out.reshape(N_DEV, SHARD_OUT, D_MODEL).reshape(N_DEV, D_MODEL, SHARD_OUT)], axis=2)
    g_sinks = jnp.concatenate([jnp.sum(dsink[:N_PAIRS, :BLOCK], axis=1), jnp.sum(dsink[:N_PAIRS, BLOCK:], axis=1)])
    small = _pack_small(dgain, _fold_heads(dqkg[0:1]), _fold_heads(dqkg[1:2]), g_sinks,
                        _fold_heads(dqkg[2:3]), _fold_heads(dqkg[3:4]), extra=0.5 * jnp.sum(loss_cols) / D_MODEL)
    reduced, small_red = _reduce_scatter_grads(blocks, small)
    g_w_in = reduced[:, :SHARD_IN]
    g_w_out = reduced[:, SHARD_IN:].reshape(SHARD_OUT, D_MODEL)
    g_small = _unpack_small(small_red)

    d_in, nm_in, nv_in = _adamw("adamw_w_in", w_in_sh, g_w_in, m_w_in[0], v_w_in[0])
    d_out, nm_out, nv_out = _adamw("adamw_w_out", w_out_sh, g_w_out, m_w_out[0], v_w_out[0])
    d_s, nm_s, nv_s = _adamw(
        "adamw_small",
        _pack_small(norm_gain, q_norm_a, k_norm_a, sinks_a, q_norm_b, k_norm_b), small_red,
        _pack_small(m_norm_gain, m_q_norm_a, m_k_norm_a, m_sinks_a, m_q_norm_b, m_k_norm_b),
        _pack_small(v_norm_gain, v_q_norm_a, v_k_norm_a, v_sinks_a, v_q_norm_b, v_k_norm_b))
    d_small, nm_small, nv_small = _unpack_small(d_s), _unpack_small(nm_s), _unpack_small(nv_s)

    loss = small_red.reshape(-1)[SMALL_USED]

    def assemble(small_list, big_in, big_out):
        ng, qa_, ka_, sk_, qb_, kb_ = small_list
        return [ng, big_in[None], qa_, ka_, sk_, qb_, kb_, big_out[None]]

    return (loss, grad_x[None], *assemble(g_small, g_w_in, g_w_out), *assemble(d_small, d_in, d_out),
            *assemble(nm_small, nm_in, nm_out), *assemble(nv_small, nv_in, nv_out))
```

```python
import functools

import numpy as np
import jax
import jax.numpy as jnp
from jax import lax
from jax.experimental import pallas as pl
from jax.experimental.pallas import tpu as pltpu

F32 = jnp.float32
BF16 = jnp.bfloat16

SEQ = 4096
D_MODEL = 1024
HEAD_DIM = 64
PAIR = 2 * HEAD_DIM
N_PAIRS = 4
HALF_WIDTH = N_PAIRS * PAIR
KV_A_WIDTH = 128
IN_WIDTH = 3328
BLOCK = 128
EPS = 1e-6
NEG = -1e30
ROPE_THETA = 10000.0
N_DEV = 8
SHARD_IN = IN_WIDTH // N_DEV
SHARD_OUT = D_MODEL // N_DEV
PAYLOAD = SHARD_IN + SHARD_OUT
SMALL_ROWS, SMALL_COLS = 8, 256

C_QA, C_KA, C_VA, C_GA, C_QB, C_KB, C_VB, C_GB = 0, 512, 640, 768, 1280, 1792, 2304, 2816

ADAM_LR = 0.001
ADAM_B1 = 0.9
ADAM_B2 = 0.999
ADAM_EPS = 1e-08
ADAM_WD = 0.01
ADAM_STEP = 10

ROW_TILE = 256
VMEM_LIMIT = 56 * 1024 * 1024

MESH = pl.DeviceIdType.MESH


def _params(sem, vmem=VMEM_LIMIT):
    return pltpu.CompilerParams(dimension_semantics=sem, vmem_limit_bytes=vmem)


def _head_sum(v, bm):
    hi = v.astype(BF16)
    lo = (v - hi.astype(F32)).astype(BF16)
    return (jnp.dot(hi, bm, preferred_element_type=F32) + jnp.dot(lo, bm, preferred_element_type=F32))


def _swap_halves(y):
    lane = lax.broadcasted_iota(jnp.int32, y.shape, 1)
    first = (lane & 32) == 0
    return jnp.where(first, pltpu.roll(y, 96, 1), pltpu.roll(y, 32, 1))


def _sigmoid(g):
    return 1.0 / (1.0 + jnp.exp(-g))


def _tiles(a):
    return [a[:, j * PAIR:(j + 1) * PAIR] for j in range(N_PAIRS)]


def _pair_tiles(t):
    low = lax.broadcasted_iota(jnp.int32, t[0].shape, 1) < HEAD_DIM
    r = [pltpu.roll(a, HEAD_DIM, 1) for a in t]
    return [jnp.where(low, t[0], r[2]), jnp.where(low, r[0], t[2]), jnp.where(low, t[1], r[3]), jnp.where(low, r[1], t[3])]


def _unpair_tiles(p):
    low = lax.broadcasted_iota(jnp.int32, p[0].shape, 1) < HEAD_DIM
    r = [pltpu.roll(a, HEAD_DIM, 1) for a in p]
    return [jnp.where(low, p[0], r[1]), jnp.where(low, p[2], r[3]), jnp.where(low, r[0], p[1]), jnp.where(low, r[2], p[3])]


def _all_gather_weights(w_in_sh, w_out_sh):
    shapes = (w_in_sh.shape, w_out_sh.shape)

    def body(a_ref, b_ref, out_a, out_b, mine_a, mine_b, send_sems, recv_sems):
        x, y, c = lax.axis_index("x"), lax.axis_index("y"), lax.axis_index("c")
        me, sibling = (x, y, c), (x, y, 1 - c)
        chips = [(1 - x, y), (x, 1 - y), (1 - x, 1 - y)]
        parts = ((mine_a, out_a), (mine_b, out_b))

        def slot(a, px, py, pc):
            return parts[a][1].at[4 * px + 2 * py + pc]

        def copy(a, k, block, to, from_mine=False):
            return pltpu.make_async_remote_copy(
                src_ref=parts[a][0] if from_mine else slot(a, *block), dst_ref=slot(a, *block),
                send_sem=send_sems.at[a, k], recv_sem=recv_sems.at[a, k], device_id=to, device_id_type=MESH)

        mine_a[...] = a_ref[...].astype(BF16)
        mine_b[...] = b_ref[...].astype(BF16)
        both = (0, 1)
        first = [copy(a, 0, me, sibling, True) for a in both]
        first += [copy(a, 1 + j, me, (*chip, c), True) for j, chip in enumerate(chips) for a in both]
        for cp in first:
            cp.start()
        slot(0, *me)[...] = mine_a[...]
        slot(1, *me)[...] = mine_b[...]
        passed = []
        for j, chip in enumerate(chips):
            for a in both:
                copy(a, 1 + j, (*chip, c), me).wait_recv()
                passed.append(copy(a, 4 + j, (*chip, c), sibling))
                passed[-1].start()
        for a in both:
            copy(a, 0, sibling, me).wait_recv()
            for j, chip in enumerate(chips):
                copy(a, 4 + j, (*chip, 1 - c), me).wait_recv()
        for cp in first + passed:
            cp.wait_send()

    vmem = pl.BlockSpec(memory_space=pltpu.VMEM)
    return pl.pallas_call(
        body, name="ag_weights",
        out_shape=tuple(jax.ShapeDtypeStruct((N_DEV,) + s, BF16) for s in shapes),
        in_specs=[vmem, vmem], out_specs=(vmem, vmem),
        scratch_shapes=[pltpu.VMEM(shapes[0], BF16), pltpu.VMEM(shapes[1], BF16),
                        pltpu.SemaphoreType.DMA((2, 7)), pltpu.SemaphoreType.DMA((2, 7))],
        compiler_params=pltpu.CompilerParams(vmem_limit_bytes=VMEM_LIMIT),
    )(w_in_sh, w_out_sh)


def _reduce_scatter_grads(blocks_in, blocks_out, small):
    shapes = (blocks_in.shape[1:], blocks_out.shape[1:])

    def body(ga_hbm, gb_hbm, small_ref, out_a, out_b, small_out_ref,
             part_a, part_b, sib_a, sib_b, wire_a, wire_b, chips_a, chips_b, small_all,
             load_sems, sib_send, sib_recv, chip_send, chip_recv, small_send, small_recv):
        x, y, c = lax.axis_index("x"), lax.axis_index("y"), lax.axis_index("c")
        sibling = (x, y, 1 - c)
        chips = [(x, y), (1 - x, y), (x, 1 - y), (1 - x, 1 - y)]
        my_id = 4 * x + 2 * y + c
        g_hbm, part, from_sib = (ga_hbm, gb_hbm), (part_a, part_b), (sib_a, sib_b)
        to_wire, from_chips, out = (wire_a, wire_b), (chips_a, chips_b), (out_a, out_b)
        both = (0, 1)

        def blk(a, chip, core):
            return g_hbm[a].at[4 * chip[0] + 2 * chip[1] + core]

        small_all[my_id] = small_ref[...]
        small_copies = []
        for rel in range(1, N_DEV):
            dx, dy, dc = (rel >> 2) & 1, (rel >> 1) & 1, rel & 1
            to = (1 - x if dx else x, 1 - y if dy else y, 1 - c if dc else c)
            small_copies.append(pltpu.make_async_remote_copy(
                src_ref=small_ref, dst_ref=small_all.at[my_id],
                send_sem=small_send.at[rel - 1], recv_sem=small_recv.at[rel - 1], device_id=to, device_id_type=MESH))
        for cp in small_copies:
            cp.start()

        loads = [[pltpu.make_async_copy(blk(a, chips[k], c), part[a].at[k], load_sems.at[a, k]) for k in range(4)] for a in both]
        to_sib = [[pltpu.make_async_remote_copy(
            src_ref=blk(a, chips[k], 1 - c), dst_ref=from_sib[a].at[k], send_sem=sib_send.at[a, k], recv_sem=sib_recv.at[a, k],
            device_id=sibling, device_id_type=MESH) for k in range(4)] for a in both]
        for k in (1, 2, 3, 0):
            for a in both:
                loads[a][k].start()
                to_sib[a][k].start()

        to_chips = [[pltpu.make_async_remote_copy(
            src_ref=to_wire[a].at[k - 1], dst_ref=from_chips[a].at[k - 1],
            send_sem=chip_send.at[a, k - 1], recv_sem=chip_recv.at[a, k - 1],
            device_id=(*chips[k], c), device_id_type=MESH) for k in range(1, 4)] for a in both]
        for k in (1, 2, 3):
            for a in both:
                loads[a][k].wait()
                to_sib[a][k].wait_recv()
                to_wire[a][k - 1] = (part[a][k] + from_sib[a][k]).astype(BF16)
                to_chips[a][k - 1].start()
        for a in both:
            loads[a][0].wait()
            to_sib[a][0].wait_recv()
            acc = part[a][0] + from_sib[a][0]
            for k in range(3):
                to_chips[a][k].wait_recv()
                acc = acc + from_chips[a][k].astype(F32)
            out[a][...] = acc

        for cp in small_copies:
            cp.wait_recv()
        tot = small_all[0]
        for d in range(1, N_DEV):
            tot = tot + small_all[d]
        small_out_ref[...] = tot
        for cp in to_sib[0] + to_sib[1] + to_chips[0] + to_chips[1] + small_copies:
            cp.wait_send()

    vmem = pl.BlockSpec(memory_space=pltpu.VMEM)
    hbm = pl.BlockSpec(memory_space=pl.ANY)
    buf = lambda n, dtype: [pltpu.VMEM((n,) + s, dtype) for s in shapes]
    return pl.pallas_call(
        body, name="rs_grads",
        out_shape=(jax.ShapeDtypeStruct(shapes[0], F32), jax.ShapeDtypeStruct(shapes[1], F32),
                   jax.ShapeDtypeStruct((SMALL_ROWS, SMALL_COLS), F32)),
        in_specs=[hbm, hbm, vmem], out_specs=(vmem, vmem, vmem),
        scratch_shapes=[*buf(4, F32), *buf(4, F32), *buf(3, BF16), *buf(3, BF16),
                        pltpu.VMEM((N_DEV, SMALL_ROWS, SMALL_COLS), F32),
                        pltpu.SemaphoreType.DMA((2, 4)), pltpu.SemaphoreType.DMA((2, 4)), pltpu.SemaphoreType.DMA((2, 4)),
                        pltpu.SemaphoreType.DMA((2, 3)), pltpu.SemaphoreType.DMA((2, 3)),
                        pltpu.SemaphoreType.DMA((7,)), pltpu.SemaphoreType.DMA((7,))],
        compiler_params=pltpu.CompilerParams(vmem_limit_bytes=VMEM_LIMIT),
    )(blocks_in, blocks_out, small)


def _fold_scratch(tm):
    return pltpu.VMEM((N_PAIRS, tm, PAIR), F32)


def _fold_store(val, scr, out4, out16, tm):
    for j in range(N_PAIRS):
        scr[j] = val[:, j * PAIR:(j + 1) * PAIR]
    for dil, out in ((4, out4), (16, out16)):
        for r in range(dil):
            for j in range(N_PAIRS):
                out[r, :, j * PAIR:(j + 1) * PAIR] = scr[j, pl.ds(r, tm // dil, stride=dil), :].astype(out.dtype)


def _unfold_load(src, scr, dil, tm):
    for r in range(dil):
        for j in range(N_PAIRS):
            scr[j, pl.ds(r, tm // dil, stride=dil), :] = src[r, :, j * PAIR:(j + 1) * PAIR].astype(F32)
    return jnp.concatenate([scr[j] for j in range(N_PAIRS)], axis=1)


def _fold_specs(tm, dtype):
    shapes = (jax.ShapeDtypeStruct((4, SEQ // 4, HALF_WIDTH), dtype), jax.ShapeDtypeStruct((16, SEQ // 16, HALF_WIDTH), dtype))
    specs = (pl.BlockSpec((4, tm // 4, HALF_WIDTH), lambda i: (0, i, 0)),
             pl.BlockSpec((16, tm // 16, HALF_WIDTH), lambda i: (0, i, 0)))
    return shapes, specs


def _proj_fwd(x, gain, w, qkg, cos4, sin4, bmean):
    tm = ROW_TILE

    def norm_rope(t, g, cos, sin, bm, scale):
        rr = lax.rsqrt(_head_sum(t * t, bm) + EPS)
        yv = t * rr * g
        return (yv * cos + _swap_halves(yv) * sin) * scale

    def body(x_ref, g_ref, w_ref, qkg_ref, cos_ref, sin_ref, bm_ref,
             tqa_ref, tka_ref, tqb_ref, tkb_ref, ga_ref, gb_ref, ht_ref, qa_ref, ka_ref, va_ref, qb_ref, kb_ref, vb_ref,
             qb4_ref, qb16_ref, kb4_ref, kb16_ref, vb4_ref, vb16_ref, proj, scr):
        xf = x_ref[...]
        r = lax.rsqrt(jnp.mean(xf * xf, axis=-1, keepdims=True) + EPS)
        hf = xf * r * g_ref[...]
        ht_ref[...] = hf.T.astype(BF16)
        cos, sin, bm = cos_ref[...], sin_ref[...], bm_ref[...]
        proj[...] = jnp.dot(hf.astype(BF16), w_ref[...], preferred_element_type=F32)

        def roped(tiles, row, scale):
            g = qkg_ref[row:row + 1, :]
            return jnp.concatenate([norm_rope(t, g, cos, sin, bm, scale) for t in tiles], axis=1)

        tqa = _pair_tiles(_tiles(proj[:, C_QA:C_QA + HALF_WIDTH]))
        tqa_ref[...] = jnp.concatenate(tqa, axis=1)
        qa_ref[...] = roped(tqa, 0, HEAD_DIM ** -0.5).astype(BF16)
        ga_ref[...] = jnp.concatenate(_pair_tiles(_tiles(proj[:, C_GA:C_GA + HALF_WIDTH])), axis=1)
        gb_ref[...] = proj[:, C_GB:C_GB + HALF_WIDTH]
        tqb = proj[:, C_QB:C_QB + HALF_WIDTH]
        tqb_ref[...] = tqb
        qb = roped(_tiles(tqb), 2, HEAD_DIM ** -0.5)
        qb_ref[...] = qb.astype(BF16)
        _fold_store(qb, scr, qb4_ref, qb16_ref, tm)
        tkb = proj[:, C_KB:C_KB + HALF_WIDTH]
        tkb_ref[...] = tkb
        kb = roped(_tiles(tkb), 3, 1.0)
        kb_ref[...] = kb.astype(BF16)
        _fold_store(kb, scr, kb4_ref, kb16_ref, tm)
        vb = proj[:, C_VB:C_VB + HALF_WIDTH]
        vb_ref[...] = vb.astype(BF16)
        _fold_store(vb, scr, vb4_ref, vb16_ref, tm)
        tka = proj[:, C_KA:C_KA + KV_A_WIDTH]
        tka_ref[...] = tka
        ka_ref[...] = roped([tka], 1, 1.0).astype(BF16)
        va_ref[...] = proj[:, C_VA:C_VA + KV_A_WIDTH].astype(BF16)

    row = lambda width: pl.BlockSpec((tm, width), lambda i: (i, 0))
    full = lambda a: pl.BlockSpec(a.shape, lambda i: (0,) * a.ndim)
    nat = lambda width, dtype=BF16: jax.ShapeDtypeStruct((SEQ, width), dtype)
    f_shapes, f_specs = _fold_specs(tm, BF16)
    return pl.pallas_call(
        body, name="proj_fwd", grid=(SEQ // tm,),
        in_specs=[row(D_MODEL), full(gain), full(w), full(qkg), row(PAIR), row(PAIR), full(bmean)],
        out_specs=(row(HALF_WIDTH), row(KV_A_WIDTH), row(HALF_WIDTH), row(HALF_WIDTH), row(HALF_WIDTH), row(HALF_WIDTH),
                   pl.BlockSpec((D_MODEL, tm), lambda i: (0, i)),
                   row(HALF_WIDTH), row(KV_A_WIDTH), row(KV_A_WIDTH), row(HALF_WIDTH), row(HALF_WIDTH), row(HALF_WIDTH),
                   *f_specs, *f_specs, *f_specs),
        out_shape=(nat(HALF_WIDTH, F32), nat(KV_A_WIDTH, F32), nat(HALF_WIDTH, F32), nat(HALF_WIDTH, F32),
                   nat(HALF_WIDTH, F32), nat(HALF_WIDTH, F32),
                   jax.ShapeDtypeStruct((D_MODEL, SEQ), BF16),
                   nat(HALF_WIDTH), nat(KV_A_WIDTH), nat(KV_A_WIDTH), nat(HALF_WIDTH), nat(HALF_WIDTH), nat(HALF_WIDTH),
                   *f_shapes, *f_shapes, *f_shapes),
        scratch_shapes=[pltpu.VMEM((tm, IN_WIDTH), F32), _fold_scratch(tm)],
        compiler_params=_params(("arbitrary",)),
    )(x, gain, w, qkg, cos4, sin4, bmean)


def _band_mask(i, max_dist):
    j = lax.broadcasted_iota(jnp.int32, (2 * BLOCK, 2 * BLOCK), 0)
    c = lax.broadcasted_iota(jnp.int32, (2 * BLOCK, 2 * BLOCK), 1)
    dist = (c & (BLOCK - 1)) + BLOCK - j
    return (dist >= 0) & (dist <= max_dist) & ((j >= BLOCK) | (i > 0))


def _stack_heads(t):
    lane = lax.broadcasted_iota(jnp.int32, t.shape, 1)
    low = lane < HEAD_DIM
    zero = jnp.zeros_like(t)
    return jnp.concatenate([jnp.where(low, t, zero), jnp.where(low, zero, t)], axis=0)


def _unstack_t(t):
    return jnp.concatenate([t[:HEAD_DIM, :BLOCK], t[HEAD_DIM:, BLOCK:]], axis=0).T


def _rows_to_pair(row):
    return jnp.concatenate([jnp.broadcast_to(row[:, :BLOCK], (HEAD_DIM, BLOCK)),
                            jnp.broadcast_to(row[:, BLOCK:], (HEAD_DIM, BLOCK))], axis=0).T


def _pair_to_rows(t):
    tt = t.T
    return jnp.concatenate([tt[0:1, :], tt[HEAD_DIM:HEAD_DIM + 1, :]], axis=1)


def _attn_fwd(name, q, k, v, sink_rows, max_dist):
    n_seq, length, _ = q.shape
    ck = k.shape[2]
    nb = length // BLOCK
    shared = ck == PAIR
    has_sinks = sink_rows is not None

    def body(*refs):
        if has_sinks:
            q_ref, kc_ref, vc_ref, sink_ref, o_ref, lse_ref, kp_ref, vp_ref = refs
        else:
            q_ref, kc_ref, vc_ref, o_ref, lse_ref, kp_ref, vp_ref = refs
        i = pl.program_id(1)

        @pl.when(i == 0)
        def _():
            kp_ref[...] = jnp.zeros_like(kp_ref)
            vp_ref[...] = jnp.zeros_like(vp_ref)

        valid = _band_mask(i, max_dist)
        pairs = range(N_PAIRS)
        cols = [slice(p * PAIR, (p + 1) * PAIR) for p in pairs]
        kcols = [slice(0, PAIR) if shared else cols[p] for p in pairs]
        st = [lax.dot_general(jnp.concatenate([kp_ref[:, kcols[p]], kc_ref[:, kcols[p]]], axis=0), _stack_heads(q_ref[:, cols[p]]),
                              (((1,), (1,)), ((), ())), preferred_element_type=F32) for p in pairs]
        st = [jnp.where(valid, s, NEG) for s in st]
        m = [jnp.max(s, axis=0, keepdims=True) for s in st]
        if has_sinks:
            sk = [sink_ref[p:p + 1, :] for p in pairs]
            m = [jnp.maximum(m[p], sk[p]) for p in pairs]
        pt = [jnp.exp(st[p] - m[p]) for p in pairs]
        l = [jnp.sum(t, axis=0, keepdims=True) for t in pt]
        if has_sinks:
            l = [l[p] + jnp.exp(sk[p] - m[p]) for p in pairs]
        v2t = [jnp.concatenate([vp_ref[:, kcols[p]], vc_ref[:, kcols[p]]], axis=0).astype(F32).T.astype(BF16) for p in pairs]
        ot = [jnp.dot(v2t[p], pt[p].astype(BF16), preferred_element_type=F32) / l[p] for p in pairs]
        for p in pairs:
            o_ref[:, cols[p]] = _unstack_t(ot[p]).astype(BF16)
            lse_ref[:, cols[p]] = _rows_to_pair(m[p] + jnp.log(l[p]))
        kp_ref[...] = kc_ref[...]
        vp_ref[...] = vc_ref[...]

    cur = lambda width: pl.BlockSpec((None, BLOCK, width), lambda r, i: (r, i, 0))
    in_specs = [cur(HALF_WIDTH), cur(ck), cur(ck)]
    args = [q, k, v]
    if has_sinks:
        in_specs.append(pl.BlockSpec(sink_rows.shape, lambda r, i: (0, 0)))
        args.append(sink_rows)
    out = lambda dtype: jax.ShapeDtypeStruct((n_seq, length, HALF_WIDTH), dtype)
    return pl.pallas_call(
        body, name=name, grid=(n_seq, nb), in_specs=in_specs,
        out_specs=(cur(HALF_WIDTH), cur(HALF_WIDTH)), out_shape=(out(BF16), out(F32)),
        scratch_shapes=[pltpu.VMEM((BLOCK, ck), BF16), pltpu.VMEM((BLOCK, ck), BF16)],
        compiler_params=_params(("arbitrary", "arbitrary")),
    )(*args)


def _attn_bwd(name, q, k, v, d_o, lse, delta, sink_rows, max_dist):
    n_seq, length, _ = q.shape
    ck = k.shape[2]
    nb = length // BLOCK
    shared = ck == PAIR
    has_sinks = sink_rows is not None

    def body(*refs):
        if has_sinks:
            (q_ref, kc_ref, vc_ref, do_ref, lse_ref, dl_ref, sink_ref,
             dq_ref, dk_ref, dv_ref, dsink_ref, ck_scr, cv_scr, kp_ref, vp_ref) = refs
        else:
            (q_ref, kc_ref, vc_ref, do_ref, lse_ref, dl_ref,
             dq_ref, dk_ref, dv_ref, ck_scr, cv_scr, kp_ref, vp_ref) = refs
        r_id, i = pl.program_id(0), pl.program_id(1)

        @pl.when(i == 0)
        def _():
            ck_scr[...] = jnp.zeros_like(ck_scr)
            cv_scr[...] = jnp.zeros_like(cv_scr)
            kp_ref[...] = jnp.zeros_like(kp_ref)
            vp_ref[...] = jnp.zeros_like(vp_ref)

        if has_sinks:
            @pl.when((i == 0) & (r_id == 0))
            def _():
                dsink_ref[...] = jnp.zeros_like(dsink_ref)

        @pl.when(i < nb)
        def _():
            valid = _band_mask(i, max_dist)
            pairs = range(N_PAIRS)
            cols = [slice(p * PAIR, (p + 1) * PAIR) for p in pairs]
            kcols = [slice(0, PAIR) if shared else cols[p] for p in pairs]
            nt = (((1,), (1,)), ((), ()))
            q_st = [_stack_heads(q_ref[:, cols[p]]) for p in pairs]
            do_st = [_stack_heads(do_ref[:, cols[p]]) for p in pairs]
            k2 = [jnp.concatenate([kp_ref[:, kcols[p]], kc_ref[:, kcols[p]]], axis=0) for p in pairs]
            v2 = [jnp.concatenate([vp_ref[:, kcols[p]], vc_ref[:, kcols[p]]], axis=0) for p in pairs]
            st = [lax.dot_general(k2[p], q_st[p], nt, preferred_element_type=F32) for p in pairs]
            dpt = [lax.dot_general(v2[p], do_st[p], nt, preferred_element_type=F32) for p in pairs]
            lse_row = [_pair_to_rows(lse_ref[:, cols[p]]) for p in pairs]
            dl_row = [_pair_to_rows(dl_ref[:, cols[p]]) for p in pairs]
            pt = [jnp.exp(jnp.where(valid, st[p], NEG) - lse_row[p]) for p in pairs]
            dst = [(pt[p] * (dpt[p] - dl_row[p])).astype(BF16) for p in pairs]
            ptb = [t.astype(BF16) for t in pt]
            dv2 = [jnp.dot(ptb[p], do_st[p], preferred_element_type=F32) for p in pairs]
            dk2 = [jnp.dot(dst[p], q_st[p], preferred_element_type=F32) for p in pairs]
            k2t = [k2[p].astype(F32).T.astype(BF16) for p in pairs]
            dqt = [jnp.dot(k2t[p], dst[p], preferred_element_type=F32) for p in pairs]
            for p in pairs:
                dq_ref[:, cols[p]] = _unstack_t(dqt[p]).astype(BF16)
            if has_sinks:
                for p in pairs:
                    p_sink = jnp.exp(sink_ref[p:p + 1, :] - lse_row[p])
                    dsink_ref[p:p + 1, :] = dsink_ref[p:p + 1, :] - p_sink * dl_row[p]
            if shared:
                dk_acc = (dk2[0] + dk2[1]) + (dk2[2] + dk2[3])
                dv_acc = (dv2[0] + dv2[1]) + (dv2[2] + dv2[3])
                dk_ref[...] = (ck_scr[...] + dk_acc[:BLOCK]).astype(BF16)
                dv_ref[...] = (cv_scr[...] + dv_acc[:BLOCK]).astype(BF16)
                ck_scr[...] = dk_acc[BLOCK:]
                cv_scr[...] = dv_acc[BLOCK:]
            else:
                for p in pairs:
                    dk_ref[:, cols[p]] = (ck_scr[:, cols[p]] + dk2[p][:BLOCK]).astype(BF16)
                    dv_ref[:, cols[p]] = (cv_scr[:, cols[p]] + dv2[p][:BLOCK]).astype(BF16)
                    ck_scr[:, cols[p]] = dk2[p][BLOCK:]
                    cv_scr[:, cols[p]] = dv2[p][BLOCK:]
            kp_ref[...] = kc_ref[...]
            vp_ref[...] = vc_ref[...]

        @pl.when(i == nb)
        def _():
            dk_ref[...] = ck_scr[...].astype(BF16)
            dv_ref[...] = cv_scr[...].astype(BF16)

    last = nb - 1
    cur = lambda width: pl.BlockSpec((None, BLOCK, width), lambda r, i: (r, jnp.minimum(i, last), 0))
    prev = lambda width: pl.BlockSpec((None, BLOCK, width), lambda r, i: (r, jnp.clip(i - 1, 0, last), 0))
    in_specs = [cur(HALF_WIDTH), cur(ck), cur(ck), cur(HALF_WIDTH), cur(HALF_WIDTH), cur(HALF_WIDTH)]
    args = [q, k, v, d_o, lse, delta]
    out_specs = [cur(HALF_WIDTH), prev(ck), prev(ck)]
    out_shape = [jax.ShapeDtypeStruct((n_seq, length, HALF_WIDTH), BF16),
                 jax.ShapeDtypeStruct((n_seq, length, ck), BF16), jax.ShapeDtypeStruct((n_seq, length, ck), BF16)]
    if has_sinks:
        in_specs.append(pl.BlockSpec(sink_rows.shape, lambda r, i: (0, 0)))
        args.append(sink_rows)
        out_specs.append(pl.BlockSpec(sink_rows.shape, lambda r, i: (0, 0)))
        out_shape.append(jax.ShapeDtypeStruct(sink_rows.shape, F32))
    return pl.pallas_call(
        body, name=name, grid=(n_seq, nb + 1), in_specs=in_specs,
        out_specs=tuple(out_specs), out_shape=tuple(out_shape),
        scratch_shapes=[pltpu.VMEM((BLOCK, ck), F32), pltpu.VMEM((BLOCK, ck), F32),
                        pltpu.VMEM((BLOCK, ck), BF16), pltpu.VMEM((BLOCK, ck), BF16)],
        compiler_params=_params(("arbitrary", "arbitrary")),
    )(*args)


def _tail(oa, ob1, lb1, ob4, lb4, ob16, lb16, gate_a, gate_b, x, target, w_out, bones):
    tm = ROW_TILE

    def body(oa_ref, ob1_ref, lb1_ref, ob4_ref, lb4_ref, ob16_ref, lb16_ref, ga_ref, gb_ref, x_ref, t_ref, w_ref, bo_ref,
             loss_ref, dy_ref, gwo_ref, doa_ref, dla_ref, dga_ref, dgb_ref,
             dob_ref, dob4_ref, dob16_ref, dlb_ref, dlb4_ref, dlb16_ref, lse_ref, lse4_ref, lse16_ref,
             s_f):
        i = pl.program_id(0)
        o4, o16 = _unfold_load(ob4_ref, s_f, 4, tm), _unfold_load(ob16_ref, s_f, 16, tm)
        l4, l16 = _unfold_load(lb4_ref, s_f, 4, tm), _unfold_load(lb16_ref, s_f, 16, tm)
        o1, l1 = ob1_ref[...].astype(F32), lb1_ref[...]
        mx = jnp.maximum(jnp.maximum(l1, l4), l16)
        e1, e4, e16 = jnp.exp(l1 - mx), jnp.exp(l4 - mx), jnp.exp(l16 - mx)
        den = e1 + e4 + e16
        ob = (e1 * o1 + e4 * o4 + e16 * o16) / den
        lse_b = mx + jnp.log(den)

        oa, ga, gb = oa_ref[...].astype(F32), ga_ref[...], gb_ref[...]
        sa, sb = _sigmoid(ga), _sigmoid(gb)
        mixed = jnp.concatenate(_unpair_tiles(_tiles(oa * (ga * sa))) + [ob * (gb * sb)], axis=1)
        mixed_bf = mixed.astype(BF16)
        w = w_ref[...]
        yv = x_ref[...] + jnp.dot(mixed_bf, w, preferred_element_type=F32)
        err = yv - t_ref[...]
        sq = jnp.sum(err * err, axis=0, keepdims=True)
        dy = err * (1.0 / D_MODEL)
        dy_ref[...] = dy
        dy_bf = dy.astype(BF16)
        gw = jnp.dot(mixed.T.astype(BF16), dy_bf, preferred_element_type=F32)

        @pl.when(i == 0)
        def _():
            loss_ref[...] = sq
            gwo_ref[...] = gw

        @pl.when(i > 0)
        def _():
            loss_ref[...] += sq
            gwo_ref[...] += gw

        dmix = lax.dot_general(dy_bf, w, (((1,), (1,)), ((), ())), preferred_element_type=F32)
        dma = jnp.concatenate(_pair_tiles(_tiles(dmix[:, :HALF_WIDTH])), axis=1)
        dmb = dmix[:, HALF_WIDTH:]
        bo = bo_ref[...]

        def head_delta(d_o, o):
            prod = d_o * o
            return jnp.concatenate([_head_sum(prod[:, j * PAIR:(j + 1) * PAIR], bo) for j in range(N_PAIRS)], axis=1)

        doa = dma * (ga * sa)
        doa_ref[...] = doa.astype(BF16)
        dla_ref[...] = head_delta(doa, oa)
        dga_ref[...] = dma * oa * (sa * (1.0 + ga * (1.0 - sa)))
        dob = dmb * (gb * sb)
        dgb_ref[...] = dmb * ob * (sb * (1.0 + gb * (1.0 - sb)))
        dlb = head_delta(dob, ob)
        dob_ref[...] = dob.astype(BF16)
        _fold_store(dob, s_f, dob4_ref, dob16_ref, tm)
        dlb_ref[...] = dlb
        _fold_store(dlb, s_f, dlb4_ref, dlb16_ref, tm)
        lse_ref[...] = lse_b
        _fold_store(lse_b, s_f, lse4_ref, lse16_ref, tm)

    row = lambda width: pl.BlockSpec((tm, width), lambda i: (i, 0))
    full = lambda a: pl.BlockSpec(a.shape, lambda i: (0,) * a.ndim)
    fb_shapes, fb_specs = _fold_specs(tm, BF16)
    ff_shapes, ff_specs = _fold_specs(tm, F32)
    nat = lambda dtype: jax.ShapeDtypeStruct((SEQ, HALF_WIDTH), dtype)
    return pl.pallas_call(
        body, name="tail", grid=(SEQ // tm,),
        in_specs=[row(HALF_WIDTH), row(HALF_WIDTH), row(HALF_WIDTH), ff_specs[0], ff_specs[0], ff_specs[1], ff_specs[1],
                  row(HALF_WIDTH), row(HALF_WIDTH), row(D_MODEL), row(D_MODEL), full(w_out), full(bones)],
        out_specs=(pl.BlockSpec((1, D_MODEL), lambda i: (0, 0)), row(D_MODEL),
                   pl.BlockSpec((D_MODEL, D_MODEL), lambda i: (0, 0)),
                   row(HALF_WIDTH), row(HALF_WIDTH), row(HALF_WIDTH), row(HALF_WIDTH),
                   row(HALF_WIDTH), *fb_specs, row(HALF_WIDTH), *ff_specs, row(HALF_WIDTH), *ff_specs),
        out_shape=(jax.ShapeDtypeStruct((1, D_MODEL), F32), jax.ShapeDtypeStruct((SEQ, D_MODEL), F32),
                   jax.ShapeDtypeStruct((D_MODEL, D_MODEL), F32),
                   nat(BF16), nat(F32), nat(F32), nat(F32),
                   nat(BF16), *fb_shapes, nat(F32), *ff_shapes, nat(F32), *ff_shapes),
        scratch_shapes=[_fold_scratch(tm)],
        compiler_params=_params(("arbitrary",)),
    )(oa, ob1, lb1, ob4, lb4, ob16, lb16, gate_a, gate_b, x, target, w_out, bones)


def _dproj_assemble(dqa, dka, dva, dga, dgb, dq1, dk1, dv1, dq4, dk4, dv4, dq16, dk16, dv16, tqa, tqb, tkb, tka,
                    qkg, cos4, sin4, bmean):
    tm = ROW_TILE

    def norm_rope_bwd(d_out, t, g, cos, sin, bm, scale):
        d_r = d_out * scale
        dyv = d_r * cos + _swap_halves(d_r * sin)
        rr = lax.rsqrt(_head_sum(t * t, bm) + EPS)
        that = t * rr
        dgain = jnp.sum(dyv * that, axis=0, keepdims=True)
        gdy = dyv * g
        dt = rr * (gdy - that * _head_sum(that * gdy, bm))
        return dt, dgain

    def body(dqa_ref, dka_ref, dva_ref, dga_ref, dgb_ref, dq1_ref, dk1_ref, dv1_ref, dq4_ref, dk4_ref, dv4_ref,
             dq16_ref, dk16_ref, dv16_ref, tqa_ref, tqb_ref, tkb_ref, tka_ref, qkg_ref, cos_ref, sin_ref, bm_ref,
             dproj_ref, dqkg_ref, s_f):
        i = pl.program_id(0)
        cos, sin, bm = cos_ref[...], sin_ref[...], bm_ref[...]

        def merged(nat_ref, f4_ref, f16_ref):
            return nat_ref[...].astype(F32) + _unfold_load(f4_ref, s_f, 4, tm) + _unfold_load(f16_ref, s_f, 16, tm)

        @pl.when(i == 0)
        def _():
            dqkg_ref[...] = jnp.zeros_like(dqkg_ref)

        def through(d_out, t, row, scale, c0, paired=False):
            g = qkg_ref[row:row + 1, :]
            tot = jnp.zeros((1, PAIR), F32)
            dts = []
            for j in range(d_out.shape[1] // PAIR):
                cols = slice(j * PAIR, (j + 1) * PAIR)
                dt, dg = norm_rope_bwd(d_out[:, cols], t[:, cols], g, cos, sin, bm, scale)
                dts.append(dt)
                tot = tot + dg
            if paired:
                dts = _unpair_tiles(dts)
            for j, dt in enumerate(dts):
                dproj_ref[:, c0 + j * PAIR:c0 + (j + 1) * PAIR] = dt.astype(BF16)
            dqkg_ref[row:row + 1, :] += tot

        through(dqa_ref[...].astype(F32), tqa_ref[...], 0, HEAD_DIM ** -0.5, C_QA, paired=True)
        through(dka_ref[...].astype(F32), tka_ref[...], 1, 1.0, C_KA)
        through(merged(dq1_ref, dq4_ref, dq16_ref), tqb_ref[...], 2, HEAD_DIM ** -0.5, C_QB)
        through(merged(dk1_ref, dk4_ref, dk16_ref), tkb_ref[...], 3, 1.0, C_KB)
        dproj_ref[:, C_VB:C_VB + HALF_WIDTH] = merged(dv1_ref, dv4_ref, dv16_ref).astype(BF16)
        dproj_ref[:, C_GA:C_GA + HALF_WIDTH] = jnp.concatenate(_unpair_tiles(_tiles(dga_ref[...])), axis=1).astype(BF16)
        dproj_ref[:, C_GB:C_GB + HALF_WIDTH] = dgb_ref[...].astype(BF16)
        dproj_ref[:, C_VA:C_VA + KV_A_WIDTH] = dva_ref[...].astype(BF16)

    row = lambda width: pl.BlockSpec((tm, width), lambda i: (i, 0))
    full = lambda a: pl.BlockSpec(a.shape, lambda i: (0,) * a.ndim)
    _, ff_specs = _fold_specs(tm, F32)
    return pl.pallas_call(
        body, name="dproj_assemble", grid=(SEQ // tm,),
        in_specs=[row(HALF_WIDTH), row(KV_A_WIDTH), row(KV_A_WIDTH), row(HALF_WIDTH), row(HALF_WIDTH),
                  row(HALF_WIDTH), row(HALF_WIDTH), row(HALF_WIDTH), ff_specs[0], ff_specs[0], ff_specs[0],
                  ff_specs[1], ff_specs[1], ff_specs[1],
                  row(HALF_WIDTH), row(HALF_WIDTH), row(HALF_WIDTH), row(KV_A_WIDTH),
                  full(qkg), row(PAIR), row(PAIR), full(bmean)],
        out_specs=(row(IN_WIDTH), pl.BlockSpec((SMALL_ROWS, PAIR), lambda i: (0, 0))),
        out_shape=(jax.ShapeDtypeStruct((SEQ, IN_WIDTH), BF16), jax.ShapeDtypeStruct((SMALL_ROWS, PAIR), F32)),
        scratch_shapes=[_fold_scratch(tm)],
        compiler_params=_params(("arbitrary",)),
    )(dqa, dka, dva, dga, dgb, dq1, dk1, dv1, dq4, dk4, dv4, dq16, dk16, dv16, tqa, tqb, tkb, tka, qkg, cos4, sin4, bmean)


def _input_grad(dproj, w, x, gain, dy):
    tm = ROW_TILE

    def body(dp_ref, w_ref, x_ref, g_ref, dy_ref, gx_ref, dgain_ref):
        i = pl.program_id(0)
        dh = lax.dot_general(dp_ref[...], w_ref[...], (((1,), (1,)), ((), ())), preferred_element_type=F32)
        xf = x_ref[...]
        r = lax.rsqrt(jnp.mean(xf * xf, axis=-1, keepdims=True) + EPS)
        xhat = xf * r
        dg = jnp.sum(dh * xhat, axis=0, keepdims=True)
        dxh = dh * g_ref[...]
        dx = r * (dxh - xhat * jnp.mean(dxh * xhat, axis=-1, keepdims=True))
        gx_ref[...] = dy_ref[...] + dx

        @pl.when(i == 0)
        def _():
            dgain_ref[...] = dg

        @pl.when(i > 0)
        def _():
            dgain_ref[...] += dg

    row = lambda width: pl.BlockSpec((tm, width), lambda i: (i, 0))
    full = lambda a: pl.BlockSpec(a.shape, lambda i: (0,) * a.ndim)
    return pl.pallas_call(
        body, name="input_grad", grid=(SEQ // tm,),
        in_specs=[row(IN_WIDTH), full(w), row(D_MODEL), full(gain), row(D_MODEL)],
        out_specs=(row(D_MODEL), pl.BlockSpec((1, D_MODEL), lambda i: (0, 0))),
        out_shape=(jax.ShapeDtypeStruct((SEQ, D_MODEL), F32), jax.ShapeDtypeStruct((1, D_MODEL), F32)),
        compiler_params=_params(("arbitrary",)),
    )(dproj, w, x, gain, dy)


def _weight_grad(h_t, dproj):
    tk = 512
    cb = IN_WIDTH // 2
    n_k = SEQ // tk

    def body(ht_ref, dp_ref, out_ref):
        k = pl.program_id(1)
        upd = jnp.dot(ht_ref[...], dp_ref[...], preferred_element_type=F32)

        @pl.when(k == 0)
        def _():
            out_ref[...] = upd

        @pl.when(k > 0)
        def _():
            out_ref[...] += upd

    return pl.pallas_call(
        body, name="weight_grad", grid=(2, n_k),
        in_specs=[pl.BlockSpec((D_MODEL, tk), lambda j, k: (0, k)), pl.BlockSpec((tk, cb), lambda j, k: (k, j))],
        out_specs=pl.BlockSpec((D_MODEL, cb), lambda j, k: (0, j)),
        out_shape=jax.ShapeDtypeStruct((D_MODEL, IN_WIDTH), F32),
        compiler_params=_params(("arbitrary", "arbitrary")),
    )(h_t, dproj)


def _adamw(name, w, g, m, v):
    def body(w_ref, g_ref, m_ref, v_ref, d_ref, nm_ref, nv_ref):
        gv = g_ref[...]
        nm = ADAM_B1 * m_ref[...] + (1.0 - ADAM_B1) * gv
        nv = ADAM_B2 * v_ref[...] + (1.0 - ADAM_B2) * jnp.square(gv)
        m_hat = nm / (1.0 - ADAM_B1 ** ADAM_STEP)
        v_hat = nv / (1.0 - ADAM_B2 ** ADAM_STEP)
        d_ref[...] = -ADAM_LR * (m_hat / (jnp.sqrt(v_hat) + ADAM_EPS) + ADAM_WD * w_ref[...])
        nm_ref[...] = nm
        nv_ref[...] = nv

    vmem = pl.BlockSpec(memory_space=pltpu.VMEM)
    out = jax.ShapeDtypeStruct(w.shape, F32)
    return pl.pallas_call(
        body, name=name, in_specs=[vmem] * 4, out_specs=(vmem,) * 3, out_shape=(out,) * 3,
        compiler_params=pltpu.CompilerParams(vmem_limit_bytes=VMEM_LIMIT),
    )(w, g, m, v)


SMALL_USED = D_MODEL + 4 * HEAD_DIM + 8


def _pack_small(norm_gain, qa, ka, sinks, qb, kb, extra=None):
    parts = [norm_gain.reshape(-1), qa.reshape(-1), ka.reshape(-1), sinks.reshape(-1), qb.reshape(-1), kb.reshape(-1)]
    if extra is not None:
        parts.append(extra.reshape(-1))
    flat = jnp.concatenate(parts)
    flat = jnp.pad(flat, (0, SMALL_ROWS * SMALL_COLS - flat.shape[0]))
    return flat.reshape(SMALL_ROWS, SMALL_COLS)


def _unpack_small(a):
    flat = a.reshape(-1)
    sizes = (D_MODEL, HEAD_DIM, HEAD_DIM, 8, HEAD_DIM, HEAD_DIM)
    out, off = [], 0
    for s in sizes:
        out.append(flat[off:off + s].reshape(1, s))
        off += s
    return out


def _fold_heads(row):
    return row[0, :HEAD_DIM] + row[0, HEAD_DIM:]


def kernel(x, norm_gain, w_in, q_norm_a, k_norm_a, sinks_a, q_norm_b, k_norm_b, w_out, loss_target, m_norm_gain, m_w_in, m_q_norm_a, m_k_norm_a, m_sinks_a, m_q_norm_b, m_k_norm_b, m_w_out, v_norm_gain, v_w_in, v_q_norm_a, v_k_norm_a, v_sinks_a, v_q_norm_b, v_k_norm_b, v_w_out):
    x2, tgt = x[0], loss_target[0]
    w_in_sh, w_out_sh = w_in[0], w_out[0]

    gathered_in, gathered_out = _all_gather_weights(w_in_sh, w_out_sh)
    w_full = gathered_in.transpose(1, 0, 2).reshape(D_MODEL, IN_WIDTH)
    wo_full = gathered_out.reshape(D_MODEL, D_MODEL)

    inv = np.float32(ROPE_THETA) ** (-np.arange(HEAD_DIM // 2, dtype=np.float32) / np.float32(HEAD_DIM // 2))
    ang = np.arange(SEQ, dtype=np.float32)[:, None] * inv[None, :].astype(np.float32)
    cos, sin = np.cos(ang).astype(np.float32), np.sin(ang).astype(np.float32)
    cos4 = jnp.asarray(np.concatenate([cos, cos, cos, cos], axis=1))
    sin4 = jnp.asarray(np.concatenate([-sin, sin, -sin, sin], axis=1))
    blockdiag = np.kron(np.eye(2, dtype=np.float32), np.ones((HEAD_DIM, HEAD_DIM), np.float32))
    bmean = jnp.asarray(blockdiag / HEAD_DIM, dtype=BF16)
    bones = jnp.asarray(blockdiag, dtype=BF16)
    two = lambda g: jnp.concatenate([g, g], axis=1)
    qkg = jnp.concatenate([two(q_norm_a), two(k_norm_a), two(q_norm_b), two(k_norm_b),
                           jnp.zeros((SMALL_ROWS - 4, PAIR), F32)], axis=0)
    sinks_paired = jnp.stack([sinks_a[0, :N_PAIRS], sinks_a[0, N_PAIRS:]], axis=1)
    sink_rows = jnp.concatenate([jnp.repeat(sinks_paired, BLOCK, axis=1),
                                 jnp.zeros((SMALL_ROWS - N_PAIRS, 2 * BLOCK), F32)], axis=0)

    (tqa, tka, tqb, tkb, gate_a, gate_b, h_t, qa, ka, va, qb, kb, vb, qb4, qb16, kb4, kb16, vb4, vb16) = _proj_fwd(
        x2, norm_gain, w_full, qkg, cos4, sin4, bmean)
    oa, la = _attn_fwd("attn_a_fwd", qa[None], ka[None], va[None], sink_rows, BLOCK - 1)
    ob1, lb1 = _attn_fwd("attn_b1_fwd", qb[None], kb[None], vb[None], None, BLOCK)
    ob4, lb4 = _attn_fwd("attn_b4_fwd", qb4, kb4, vb4, None, BLOCK)
    ob16, lb16 = _attn_fwd("attn_b16_fwd", qb16, kb16, vb16, None, BLOCK)
    (loss_cols, dy, gwo, doa, dla, dga, dgb, dob, dob4, dob16, dlb, dlb4, dlb16, lse_b, lse4, lse16) = _tail(
        oa[0], ob1[0], lb1[0], ob4, lb4, ob16, lb16, gate_a, gate_b, x2, tgt, wo_full, bones)

    dqa, dka, dva, dsink = _attn_bwd("attn_a_bwd", qa[None], ka[None], va[None], doa[None], la, dla[None], sink_rows, BLOCK - 1)
    dq1, dk1, dv1 = _attn_bwd("attn_b1_bwd", qb[None], kb[None], vb[None], dob[None], lse_b[None], dlb[None], None, BLOCK)
    dq4, dk4, dv4 = _attn_bwd("attn_b4_bwd", qb4, kb4, vb4, dob4, lse4, dlb4, None, BLOCK)
    dq16, dk16, dv16 = _attn_bwd("attn_b16_bwd", qb16, kb16, vb16, dob16, lse16, dlb16, None, BLOCK)
    dproj, dqkg = _dproj_assemble(dqa[0], dka[0], dva[0], dga, dgb, dq1[0], dk1[0], dv1[0], dq4, dk4, dv4,
                                  dq16, dk16, dv16, tqa, tqb, tkb, tka, qkg, cos4, sin4, bmean)
    grad_x, dgain = _input_grad(dproj, w_full, x2, norm_gain, dy)
    gw_in = _weight_grad(h_t, dproj)

    blocks_in = gw_in.reshape(D_MODEL, N_DEV, SHARD_IN).transpose(1, 0, 2)
    blocks_out = gwo.reshape(N_DEV, SHARD_OUT, D_MODEL)
    g_sinks = jnp.concatenate([jnp.sum(dsink[:N_PAIRS, :BLOCK], axis=1), jnp.sum(dsink[:N_PAIRS, BLOCK:], axis=1)])
    small = _pack_small(dgain, _fold_heads(dqkg[0:1]), _fold_heads(dqkg[1:2]), g_sinks,
                        _fold_heads(dqkg[2:3]), _fold_heads(dqkg[3:4]), extra=0.5 * jnp.sum(loss_cols) / D_MODEL)
    g_w_in, g_w_out, small_red = _reduce_scatter_grads(blocks_in, blocks_out, small)
    g_small = _unpack_small(small_red)

    d_in, nm_in, nv_in = _adamw("adamw_w_in", w_in_sh, g_w_in, m_w_in[0], v_w_in[0])
    d_out, nm_out, nv_out = _adamw("adamw_w_out", w_out_sh, g_w_out, m_w_out[0], v_w_out[0])
    d_s, nm_s, nv_s = _adamw(
        "adamw_small",
        _pack_small(norm_gain, q_norm_a, k_norm_a, sinks_a, q_norm_b, k_norm_b), small_red,
        _pack_small(m_norm_gain, m_q_norm_a, m_k_norm_a, m_sinks_a, m_q_norm_b, m_k_norm_b),
        _pack_small(v_norm_gain, v_q_norm_a, v_k_norm_a, v_sinks_a, v_q_norm_b, v_k_norm_b))
    d_small, nm_small, nv_small = _unpack_small(d_s), _unpack_small(nm_s), _unpack_small(nv_s)

    loss = small_red.reshape(-1)[SMALL_USED]

    def assemble(small_list, big_in, big_out):
        ng, qa_, ka_, sk_, qb_, kb_ = small_list
        return [ng, big_in[None], qa_, ka_, sk_, qb_, kb_, big_out[None]]

    return (loss, grad_x[None], *assemble(g_small, g_w_in, g_w_out), *assemble(d_small, d_in, d_out),
            *assemble(nm_small, nm_in, nm_out), *assemble(nv_small, nv_in, nv_out))
```

```python
import functools

import numpy as np
import jax
import jax.numpy as jnp
from jax import lax
from jax.experimental import pallas as pl
from jax.experimental.pallas import tpu as pltpu

F32 = jnp.float32
BF16 = jnp.bfloat16

SEQ = 4096
D_MODEL = 1024
HEAD_DIM = 64
PAIR = 2 * HEAD_DIM
N_PAIRS = 4
HALF_WIDTH = N_PAIRS * PAIR
KV_A_WIDTH = 128
IN_WIDTH = 3328
BLOCK = 128
EPS = 1e-6
NEG = -1e30
ROPE_THETA = 10000.0
N_DEV = 8
SHARD_IN = IN_WIDTH // N_DEV
SHARD_OUT = D_MODEL // N_DEV
PAYLOAD = SHARD_IN + SHARD_OUT
SMALL_ROWS, SMALL_COLS = 8, 256

C_QA, C_KA, C_VA, C_GA, C_QB, C_KB, C_VB, C_GB = 0, 512, 640, 768, 1280, 1792, 2304, 2816

ADAM_LR = 0.001
ADAM_B1 = 0.9
ADAM_B2 = 0.999
ADAM_EPS = 1e-08
ADAM_WD = 0.01
ADAM_STEP = 10

ROW_TILE = 256
FWD_BLOCKS_PER_STEP = 4
BWD_BLOCKS_PER_STEP = 4
VMEM_LIMIT = 56 * 1024 * 1024

MESH = pl.DeviceIdType.MESH


def _params(sem, vmem=VMEM_LIMIT):
    return pltpu.CompilerParams(dimension_semantics=sem, vmem_limit_bytes=vmem)


def _head_sum(v, bm):
    hi = v.astype(BF16)
    lo = (v - hi.astype(F32)).astype(BF16)
    return (jnp.dot(hi, bm, preferred_element_type=F32) + jnp.dot(lo, bm, preferred_element_type=F32))


def _swap_halves(y):
    lane = lax.broadcasted_iota(jnp.int32, y.shape, 1)
    first = (lane & 32) == 0
    return jnp.where(first, pltpu.roll(y, 96, 1), pltpu.roll(y, 32, 1))


def _sigmoid(g):
    return 1.0 / (1.0 + jnp.exp(-g))


def _tiles(a):
    return [a[:, j * PAIR:(j + 1) * PAIR] for j in range(N_PAIRS)]


def _pair_tiles(t):
    low = lax.broadcasted_iota(jnp.int32, t[0].shape, 1) < HEAD_DIM
    r = [pltpu.roll(a, HEAD_DIM, 1) for a in t]
    return [jnp.where(low, t[0], r[2]), jnp.where(low, r[0], t[2]), jnp.where(low, t[1], r[3]), jnp.where(low, r[1], t[3])]


def _unpair_tiles(p):
    low = lax.broadcasted_iota(jnp.int32, p[0].shape, 1) < HEAD_DIM
    r = [pltpu.roll(a, HEAD_DIM, 1) for a in p]
    return [jnp.where(low, p[0], r[1]), jnp.where(low, p[2], r[3]), jnp.where(low, r[0], p[1]), jnp.where(low, r[2], p[3])]


def _all_gather_weights(w_in_sh, w_out_sh):
    shapes = (w_in_sh.shape, w_out_sh.shape)

    def body(a_ref, b_ref, out_a, out_b, mine_a, mine_b, send_sems, recv_sems):
        x, y, c = lax.axis_index("x"), lax.axis_index("y"), lax.axis_index("c")
        me, sibling = (x, y, c), (x, y, 1 - c)
        chips = [(1 - x, y), (x, 1 - y), (1 - x, 1 - y)]
        parts = ((mine_a, out_a), (mine_b, out_b))

        def slot(a, px, py, pc):
            return parts[a][1].at[4 * px + 2 * py + pc]

        def copy(a, k, block, to, from_mine=False):
            return pltpu.make_async_remote_copy(
                src_ref=parts[a][0] if from_mine else slot(a, *block), dst_ref=slot(a, *block),
                send_sem=send_sems.at[a, k], recv_sem=recv_sems.at[a, k], device_id=to, device_id_type=MESH)

        mine_a[...] = a_ref[...].astype(BF16)
        mine_b[...] = b_ref[...].astype(BF16)
        both = (0, 1)
        first = [copy(a, 0, me, sibling, True) for a in both]
        first += [copy(a, 1 + j, me, (*chip, c), True) for j, chip in enumerate(chips) for a in both]
        for cp in first:
            cp.start()
        slot(0, *me)[...] = mine_a[...]
        slot(1, *me)[...] = mine_b[...]
        passed = []
        for j, chip in enumerate(chips):
            for a in both:
                copy(a, 1 + j, (*chip, c), me).wait_recv()
                passed.append(copy(a, 4 + j, (*chip, c), sibling))
                passed[-1].start()
        for a in both:
            copy(a, 0, sibling, me).wait_recv()
            for j, chip in enumerate(chips):
                copy(a, 4 + j, (*chip, 1 - c), me).wait_recv()
        for cp in first + passed:
            cp.wait_send()

    vmem = pl.BlockSpec(memory_space=pltpu.VMEM)
    return pl.pallas_call(
        body, name="ag_weights",
        out_shape=tuple(jax.ShapeDtypeStruct((N_DEV,) + s, BF16) for s in shapes),
        in_specs=[vmem, vmem], out_specs=(vmem, vmem),
        scratch_shapes=[pltpu.VMEM(shapes[0], BF16), pltpu.VMEM(shapes[1], BF16),
                        pltpu.SemaphoreType.DMA((2, 7)), pltpu.SemaphoreType.DMA((2, 7))],
        compiler_params=pltpu.CompilerParams(vmem_limit_bytes=VMEM_LIMIT),
    )(w_in_sh, w_out_sh)


def _reduce_scatter_grads(blocks_in, blocks_out, small):
    shapes = (blocks_in.shape[1:], blocks_out.shape[1:])

    def body(ga_hbm, gb_hbm, small_ref, out_a, out_b, small_out_ref,
             part_a, part_b, sib_a, sib_b, wire_a, wire_b, chips_a, chips_b, small_all,
             load_sems, sib_send, sib_recv, chip_send, chip_recv, small_send, small_recv):
        x, y, c = lax.axis_index("x"), lax.axis_index("y"), lax.axis_index("c")
        sibling = (x, y, 1 - c)
        chips = [(x, y), (1 - x, y), (x, 1 - y), (1 - x, 1 - y)]
        my_id = 4 * x + 2 * y + c
        g_hbm, part, from_sib = (ga_hbm, gb_hbm), (part_a, part_b), (sib_a, sib_b)
        to_wire, from_chips, out = (wire_a, wire_b), (chips_a, chips_b), (out_a, out_b)
        both = (0, 1)

        def blk(a, chip, core):
            return g_hbm[a].at[4 * chip[0] + 2 * chip[1] + core]

        small_all[my_id] = small_ref[...]
        small_copies = []
        for rel in range(1, N_DEV):
            dx, dy, dc = (rel >> 2) & 1, (rel >> 1) & 1, rel & 1
            to = (1 - x if dx else x, 1 - y if dy else y, 1 - c if dc else c)
            small_copies.append(pltpu.make_async_remote_copy(
                src_ref=small_ref, dst_ref=small_all.at[my_id],
                send_sem=small_send.at[rel - 1], recv_sem=small_recv.at[rel - 1], device_id=to, device_id_type=MESH))
        for cp in small_copies:
            cp.start()

        loads = [[pltpu.make_async_copy(blk(a, chips[k], c), part[a].at[k], load_sems.at[a, k]) for k in range(4)] for a in both]
        to_sib = [[pltpu.make_async_remote_copy(
            src_ref=blk(a, chips[k], 1 - c), dst_ref=from_sib[a].at[k], send_sem=sib_send.at[a, k], recv_sem=sib_recv.at[a, k],
            device_id=sibling, device_id_type=MESH) for k in range(4)] for a in both]
        for k in (1, 2, 3, 0):
            for a in both:
                loads[a][k].start()
                to_sib[a][k].start()

        to_chips = [[pltpu.make_async_remote_copy(
            src_ref=to_wire[a].at[k - 1], dst_ref=from_chips[a].at[k - 1],
            send_sem=chip_send.at[a, k - 1], recv_sem=chip_recv.at[a, k - 1],
            device_id=(*chips[k], c), device_id_type=MESH) for k in range(1, 4)] for a in both]
        for k in (1, 2, 3):
            for a in both:
                loads[a][k].wait()
                to_sib[a][k].wait_recv()
                to_wire[a][k - 1] = (part[a][k] + from_sib[a][k]).astype(BF16)
                to_chips[a][k - 1].start()
        for a in both:
            loads[a][0].wait()
            to_sib[a][0].wait_recv()
            acc = part[a][0] + from_sib[a][0]
            for k in range(3):
                to_chips[a][k].wait_recv()
                acc = acc + from_chips[a][k].astype(F32)
            out[a][...] = acc

        for cp in small_copies:
            cp.wait_recv()
        tot = small_all[0]
        for d in range(1, N_DEV):
            tot = tot + small_all[d]
        small_out_ref[...] = tot
        for cp in to_sib[0] + to_sib[1] + to_chips[0] + to_chips[1] + small_copies:
            cp.wait_send()

    vmem = pl.BlockSpec(memory_space=pltpu.VMEM)
    hbm = pl.BlockSpec(memory_space=pl.ANY)
    buf = lambda n, dtype: [pltpu.VMEM((n,) + s, dtype) for s in shapes]
    return pl.pallas_call(
        body, name="rs_grads",
        out_shape=(jax.ShapeDtypeStruct(shapes[0], F32), jax.ShapeDtypeStruct(shapes[1], F32),
                   jax.ShapeDtypeStruct((SMALL_ROWS, SMALL_COLS), F32)),
        in_specs=[hbm, hbm, vmem], out_specs=(vmem, vmem, vmem),
        scratch_shapes=[*buf(4, F32), *buf(4, F32), *buf(3, BF16), *buf(3, BF16),
                        pltpu.VMEM((N_DEV, SMALL_ROWS, SMALL_COLS), F32),
                        pltpu.SemaphoreType.DMA((2, 4)), pltpu.SemaphoreType.DMA((2, 4)), pltpu.SemaphoreType.DMA((2, 4)),
                        pltpu.SemaphoreType.DMA((2, 3)), pltpu.SemaphoreType.DMA((2, 3)),
                        pltpu.SemaphoreType.DMA((7,)), pltpu.SemaphoreType.DMA((7,))],
        compiler_params=pltpu.CompilerParams(vmem_limit_bytes=VMEM_LIMIT),
    )(blocks_in, blocks_out, small)


def _fold_scratch(tm):
    return pltpu.VMEM((N_PAIRS, tm, PAIR), F32)


def _fold_store(val, scr, out4, out16, tm):
    for j in range(N_PAIRS):
        scr[j] = val[:, j * PAIR:(j + 1) * PAIR]
    for dil, out in ((4, out4), (16, out16)):
        for r in range(dil):
            for j in range(N_PAIRS):
                out[r, :, j * PAIR:(j + 1) * PAIR] = scr[j, pl.ds(r, tm // dil, stride=dil), :].astype(out.dtype)


def _unfold_load(src, scr, dil, tm):
    for r in range(dil):
        for j in range(N_PAIRS):
            scr[j, pl.ds(r, tm // dil, stride=dil), :] = src[r, :, j * PAIR:(j + 1) * PAIR].astype(F32)
    return jnp.concatenate([scr[j] for j in range(N_PAIRS)], axis=1)


def _fold_specs(tm, dtype):
    shapes = (jax.ShapeDtypeStruct((4, SEQ // 4, HALF_WIDTH), dtype), jax.ShapeDtypeStruct((16, SEQ // 16, HALF_WIDTH), dtype))
    specs = (pl.BlockSpec((4, tm // 4, HALF_WIDTH), lambda i: (0, i, 0)),
             pl.BlockSpec((16, tm // 16, HALF_WIDTH), lambda i: (0, i, 0)))
    return shapes, specs


def _proj_fwd(x, gain, w, qkg, cos4, sin4, bmean):
    tm = ROW_TILE

    def norm_rope(t, g, cos, sin, bm, scale):
        rr = lax.rsqrt(_head_sum(t * t, bm) + EPS)
        yv = t * rr * g
        return (yv * cos + _swap_halves(yv) * sin) * scale

    def body(x_ref, g_ref, w_ref, qkg_ref, cos_ref, sin_ref, bm_ref,
             tqa_ref, tka_ref, tqb_ref, tkb_ref, ga_ref, gb_ref, ht_ref, qa_ref, ka_ref, va_ref, qb_ref, kb_ref, vb_ref,
             qb4_ref, qb16_ref, kb4_ref, kb16_ref, vb4_ref, vb16_ref, proj, scr):
        xf = x_ref[...]
        r = lax.rsqrt(jnp.mean(xf * xf, axis=-1, keepdims=True) + EPS)
        hf = xf * r * g_ref[...]
        ht_ref[...] = hf.T.astype(BF16)
        cos, sin, bm = cos_ref[...], sin_ref[...], bm_ref[...]
        proj[...] = jnp.dot(hf.astype(BF16), w_ref[...], preferred_element_type=F32)

        def roped(tiles, row, scale):
            g = qkg_ref[row:row + 1, :]
            return jnp.concatenate([norm_rope(t, g, cos, sin, bm, scale) for t in tiles], axis=1)

        tqa = _pair_tiles(_tiles(proj[:, C_QA:C_QA + HALF_WIDTH]))
        tqa_ref[...] = jnp.concatenate(tqa, axis=1)
        qa_ref[...] = roped(tqa, 0, HEAD_DIM ** -0.5).astype(BF16)
        ga_ref[...] = jnp.concatenate(_pair_tiles(_tiles(proj[:, C_GA:C_GA + HALF_WIDTH])), axis=1)
        gb_ref[...] = proj[:, C_GB:C_GB + HALF_WIDTH]
        tqb = proj[:, C_QB:C_QB + HALF_WIDTH]
        tqb_ref[...] = tqb
        qb = roped(_tiles(tqb), 2, HEAD_DIM ** -0.5)
        qb_ref[...] = qb.astype(BF16)
        _fold_store(qb, scr, qb4_ref, qb16_ref, tm)
        tkb = proj[:, C_KB:C_KB + HALF_WIDTH]
        tkb_ref[...] = tkb
        kb = roped(_tiles(tkb), 3, 1.0)
        kb_ref[...] = kb.astype(BF16)
        _fold_store(kb, scr, kb4_ref, kb16_ref, tm)
        vb = proj[:, C_VB:C_VB + HALF_WIDTH]
        vb_ref[...] = vb.astype(BF16)
        _fold_store(vb, scr, vb4_ref, vb16_ref, tm)
        tka = proj[:, C_KA:C_KA + KV_A_WIDTH]
        tka_ref[...] = tka
        ka_ref[...] = roped([tka], 1, 1.0).astype(BF16)
        va_ref[...] = proj[:, C_VA:C_VA + KV_A_WIDTH].astype(BF16)

    row = lambda width: pl.BlockSpec((tm, width), lambda i: (i, 0))
    full = lambda a: pl.BlockSpec(a.shape, lambda i: (0,) * a.ndim)
    nat = lambda width, dtype=BF16: jax.ShapeDtypeStruct((SEQ, width), dtype)
    f_shapes, f_specs = _fold_specs(tm, BF16)
    return pl.pallas_call(
        body, name="proj_fwd", grid=(SEQ // tm,),
        in_specs=[row(D_MODEL), full(gain), full(w), full(qkg), row(PAIR), row(PAIR), full(bmean)],
        out_specs=(row(HALF_WIDTH), row(KV_A_WIDTH), row(HALF_WIDTH), row(HALF_WIDTH), row(HALF_WIDTH), row(HALF_WIDTH),
                   pl.BlockSpec((D_MODEL, tm), lambda i: (0, i)),
                   row(HALF_WIDTH), row(KV_A_WIDTH), row(KV_A_WIDTH), row(HALF_WIDTH), row(HALF_WIDTH), row(HALF_WIDTH),
                   *f_specs, *f_specs, *f_specs),
        out_shape=(nat(HALF_WIDTH, F32), nat(KV_A_WIDTH, F32), nat(HALF_WIDTH, F32), nat(HALF_WIDTH, F32),
                   nat(HALF_WIDTH, F32), nat(HALF_WIDTH, F32),
                   jax.ShapeDtypeStruct((D_MODEL, SEQ), BF16),
                   nat(HALF_WIDTH), nat(KV_A_WIDTH), nat(KV_A_WIDTH), nat(HALF_WIDTH), nat(HALF_WIDTH), nat(HALF_WIDTH),
                   *f_shapes, *f_shapes, *f_shapes),
        scratch_shapes=[pltpu.VMEM((tm, IN_WIDTH), F32), _fold_scratch(tm)],
        compiler_params=_params(("arbitrary",)),
    )(x, gain, w, qkg, cos4, sin4, bmean)


def _band_mask(i, max_dist):
    j = lax.broadcasted_iota(jnp.int32, (2 * BLOCK, 2 * BLOCK), 0)
    c = lax.broadcasted_iota(jnp.int32, (2 * BLOCK, 2 * BLOCK), 1)
    dist = (c & (BLOCK - 1)) + BLOCK - j
    return (dist >= 0) & (dist <= max_dist) & ((j >= BLOCK) | (i > 0))


def _stack_heads(t):
    lane = lax.broadcasted_iota(jnp.int32, t.shape, 1)
    low = lane < HEAD_DIM
    zero = jnp.zeros_like(t)
    return jnp.concatenate([jnp.where(low, t, zero), jnp.where(low, zero, t)], axis=0)


def _unstack_t(t):
    return jnp.concatenate([t[:HEAD_DIM, :BLOCK], t[HEAD_DIM:, BLOCK:]], axis=0).T


def _rows_to_pair(row):
    return jnp.concatenate([jnp.broadcast_to(row[:, :BLOCK], (HEAD_DIM, BLOCK)),
                            jnp.broadcast_to(row[:, BLOCK:], (HEAD_DIM, BLOCK))], axis=0).T


def _pair_to_rows(t):
    tt = t.T
    return jnp.concatenate([tt[0:1, :], tt[HEAD_DIM:HEAD_DIM + 1, :]], axis=1)


def _attn_fwd(name, q, k, v, sink_rows, max_dist):
    n_seq, length, _ = q.shape
    ck = k.shape[2]
    nb = length // BLOCK
    shared = ck == PAIR
    has_sinks = sink_rows is not None

    qb = FWD_BLOCKS_PER_STEP

    def body(*refs):
        if has_sinks:
            q_ref, kc_ref, vc_ref, sink_ref, o_ref, lse_ref, kp_ref, vp_ref = refs
        else:
            q_ref, kc_ref, vc_ref, o_ref, lse_ref, kp_ref, vp_ref = refs
        step = pl.program_id(0)

        @pl.when(step == 0)
        def _():
            kp_ref[...] = jnp.zeros_like(kp_ref)
            vp_ref[...] = jnp.zeros_like(vp_ref)

        valid = [_band_mask((step * qb + b) & (nb - 1), max_dist) for b in range(qb)]
        cols = [slice(p * PAIR, (p + 1) * PAIR) for p in range(N_PAIRS)]
        kcols = [slice(0, PAIR) if shared else c for c in cols]
        rows = [slice(b * BLOCK, (b + 1) * BLOCK) for b in range(qb)]
        units = [(b, p) for b in range(qb) for p in range(N_PAIRS)]
        n = range(len(units))

        def window(prev_ref, cur_ref, b, kc):
            before = prev_ref[:, kc] if b == 0 else cur_ref[rows[b - 1], kc]
            return jnp.concatenate([before, cur_ref[rows[b], kc]], axis=0)

        st = [lax.dot_general(window(kp_ref, kc_ref, b, kcols[p]), _stack_heads(q_ref[rows[b], cols[p]]),
                              (((1,), (1,)), ((), ())), preferred_element_type=F32) for b, p in units]
        st = [jnp.where(valid[units[u][0]], st[u], NEG) for u in n]
        m = [jnp.max(s, axis=0, keepdims=True) for s in st]
        if has_sinks:
            sk = [sink_ref[p:p + 1, :] for _, p in units]
            m = [jnp.maximum(m[u], sk[u]) for u in n]
        pt = [jnp.exp(st[u] - m[u]) for u in n]
        l = [jnp.sum(t, axis=0, keepdims=True) for t in pt]
        if has_sinks:
            l = [l[u] + jnp.exp(sk[u] - m[u]) for u in n]
        v2t = [window(vp_ref, vc_ref, b, kcols[p]).astype(F32).T.astype(BF16) for b, p in units]
        ot = [jnp.dot(v2t[u], pt[u].astype(BF16), preferred_element_type=F32) / l[u] for u in n]
        for u, (b, p) in enumerate(units):
            o_ref[rows[b], cols[p]] = _unstack_t(ot[u]).astype(BF16)
            lse_ref[rows[b], cols[p]] = _rows_to_pair(m[u] + jnp.log(l[u]))
        kp_ref[...] = kc_ref[rows[-1], :]
        vp_ref[...] = vc_ref[rows[-1], :]

    cur = lambda width: pl.BlockSpec((qb * BLOCK, width), lambda s: (s, 0))
    flat = lambda a: a.reshape(n_seq * length, a.shape[2])
    in_specs = [cur(HALF_WIDTH), cur(ck), cur(ck)]
    args = [flat(q), flat(k), flat(v)]
    if has_sinks:
        in_specs.append(pl.BlockSpec(sink_rows.shape, lambda s: (0, 0)))
        args.append(sink_rows)
    out = lambda dtype: jax.ShapeDtypeStruct((n_seq * length, HALF_WIDTH), dtype)
    o, lse = pl.pallas_call(
        body, name=name, grid=(n_seq * nb // qb,), in_specs=in_specs,
        out_specs=(cur(HALF_WIDTH), cur(HALF_WIDTH)), out_shape=(out(BF16), out(F32)),
        scratch_shapes=[pltpu.VMEM((BLOCK, ck), BF16), pltpu.VMEM((BLOCK, ck), BF16)],
        compiler_params=_params(("arbitrary",)),
    )(*args)
    return o.reshape(n_seq, length, HALF_WIDTH), lse.reshape(n_seq, length, HALF_WIDTH)


def _attn_bwd(name, q, k, v, d_o, lse, delta, sink_rows, max_dist):
    n_seq, length, _ = q.shape
    ck = k.shape[2]
    nb = length // BLOCK
    n_blocks = n_seq * nb
    n_rows = n_seq * length
    shared = ck == PAIR
    has_sinks = sink_rows is not None
    qb = BWD_BLOCKS_PER_STEP
    n_steps = n_blocks // qb

    def body(*refs):
        if has_sinks:
            (q_ref, kc_ref, vc_ref, do_ref, lse_ref, dl_ref, sink_ref,
             dq_ref, dk_ref, dv_ref, dsink_ref, ck_scr, cv_scr, kp_ref, vp_ref) = refs
        else:
            (q_ref, kc_ref, vc_ref, do_ref, lse_ref, dl_ref,
             dq_ref, dk_ref, dv_ref, ck_scr, cv_scr, kp_ref, vp_ref) = refs
        step = pl.program_id(0)

        @pl.when(step == 0)
        def _():
            ck_scr[...] = jnp.zeros_like(ck_scr)
            cv_scr[...] = jnp.zeros_like(cv_scr)
            kp_ref[...] = jnp.zeros_like(kp_ref)
            vp_ref[...] = jnp.zeros_like(vp_ref)
            if has_sinks:
                dsink_ref[...] = jnp.zeros_like(dsink_ref)

        valid = [_band_mask((step * qb + b) & (nb - 1), max_dist) for b in range(qb)]
        cols = [slice(p * PAIR, (p + 1) * PAIR) for p in range(N_PAIRS)]
        kcols = [slice(0, PAIR) if shared else c for c in cols]
        rows = [slice(b * BLOCK, (b + 1) * BLOCK) for b in range(qb)]
        units = [(b, p) for b in range(qb) for p in range(N_PAIRS)]
        n = range(len(units))
        nt = (((1,), (1,)), ((), ()))

        def window(prev_ref, cur_ref, b, kc):
            before = prev_ref[:, kc] if b == 0 else cur_ref[rows[b - 1], kc]
            return jnp.concatenate([before, cur_ref[rows[b], kc]], axis=0)

        q_st = [_stack_heads(q_ref[rows[b], cols[p]]) for b, p in units]
        do_st = [_stack_heads(do_ref[rows[b], cols[p]]) for b, p in units]
        k2 = [window(kp_ref, kc_ref, b, kcols[p]) for b, p in units]
        v2 = [window(vp_ref, vc_ref, b, kcols[p]) for b, p in units]
        st = [lax.dot_general(k2[u], q_st[u], nt, preferred_element_type=F32) for u in n]
        dpt = [lax.dot_general(v2[u], do_st[u], nt, preferred_element_type=F32) for u in n]
        lse_row = [_pair_to_rows(lse_ref[rows[b], cols[p]]) for b, p in units]
        dl_row = [_pair_to_rows(dl_ref[rows[b], cols[p]]) for b, p in units]
        pt = [jnp.exp(jnp.where(valid[units[u][0]], st[u], NEG) - lse_row[u]) for u in n]
        dst = [(pt[u] * (dpt[u] - dl_row[u])).astype(BF16) for u in n]
        ptb = [t.astype(BF16) for t in pt]
        dv2 = [jnp.dot(ptb[u], do_st[u], preferred_element_type=F32) for u in n]
        dk2 = [jnp.dot(dst[u], q_st[u], preferred_element_type=F32) for u in n]
        k2t = [k2[u].astype(F32).T.astype(BF16) for u in n]
        dqt = [jnp.dot(k2t[u], dst[u], preferred_element_type=F32) for u in n]
        for u, (b, p) in enumerate(units):
            dq_ref[rows[b], cols[p]] = _unstack_t(dqt[u]).astype(BF16)
        if has_sinks:
            for u, (b, p) in enumerate(units):
                p_sink = jnp.exp(sink_ref[p:p + 1, :] - lse_row[u])
                dsink_ref[p:p + 1, :] = dsink_ref[p:p + 1, :] - p_sink * dl_row[u]

        def total(parts, w, group):
            sel = [u for u, (b, p) in enumerate(units) if (shared or p == group)]
            terms = ([parts[u][:BLOCK] for u in sel if units[u][0] == w]
                     + [parts[u][BLOCK:] for u in sel if units[u][0] == w - 1])
            tot = terms[0]
            for t in terms[1:]:
                tot = tot + t
            return tot

        first_row = step * (qb * BLOCK)
        for acc_ref, out_ref, parts in ((ck_scr, dk_ref, dk2), (cv_scr, dv_ref, dv2)):
            for group in range(1 if shared else N_PAIRS):
                kc = kcols[group]

                @pl.when(step > 0)
                def _():
                    out_ref[pl.ds(pl.multiple_of(first_row - BLOCK, BLOCK), BLOCK), kc] = (
                        acc_ref[:, kc] + total(parts, 0, group)).astype(BF16)

                for w in range(1, qb):
                    out_ref[pl.ds(pl.multiple_of(first_row + (w - 1) * BLOCK, BLOCK), BLOCK), kc] = (
                        total(parts, w, group).astype(BF16))
                acc_ref[:, kc] = total(parts, qb, group)

        @pl.when(step == n_steps - 1)
        def _():
            dk_ref[pl.ds(n_rows - BLOCK, BLOCK), :] = ck_scr[...].astype(BF16)
            dv_ref[pl.ds(n_rows - BLOCK, BLOCK), :] = cv_scr[...].astype(BF16)

        kp_ref[...] = kc_ref[rows[-1], :]
        vp_ref[...] = vc_ref[rows[-1], :]

    cur = lambda width: pl.BlockSpec((qb * BLOCK, width), lambda s: (s, 0))
    whole = lambda width: pl.BlockSpec((n_rows, width), lambda s: (0, 0))
    flat = lambda a: a.reshape(n_rows, a.shape[2])
    in_specs = [cur(HALF_WIDTH), cur(ck), cur(ck), cur(HALF_WIDTH), cur(HALF_WIDTH), cur(HALF_WIDTH)]
    args = [flat(a) for a in (q, k, v, d_o, lse, delta)]
    out_specs = [cur(HALF_WIDTH), whole(ck), whole(ck)]
    out_shape = [jax.ShapeDtypeStruct((n_rows, HALF_WIDTH), BF16),
                 jax.ShapeDtypeStruct((n_rows, ck), BF16), jax.ShapeDtypeStruct((n_rows, ck), BF16)]
    if has_sinks:
        in_specs.append(pl.BlockSpec(sink_rows.shape, lambda s: (0, 0)))
        args.append(sink_rows)
        out_specs.append(pl.BlockSpec(sink_rows.shape, lambda s: (0, 0)))
        out_shape.append(jax.ShapeDtypeStruct(sink_rows.shape, F32))
    outs = pl.pallas_call(
        body, name=name, grid=(n_steps,), in_specs=in_specs,
        out_specs=tuple(out_specs), out_shape=tuple(out_shape),
        scratch_shapes=[pltpu.VMEM((BLOCK, ck), F32), pltpu.VMEM((BLOCK, ck), F32),
                        pltpu.VMEM((BLOCK, ck), BF16), pltpu.VMEM((BLOCK, ck), BF16)],
        compiler_params=_params(("arbitrary",)),
    )(*args)
    return tuple(o.reshape(n_seq, length, o.shape[1]) for o in outs[:3]) + tuple(outs[3:])


def _tail(oa, ob1, lb1, ob4, lb4, ob16, lb16, gate_a, gate_b, x, target, w_out, bones):
    tm = ROW_TILE

    def body(oa_ref, ob1_ref, lb1_ref, ob4_ref, lb4_ref, ob16_ref, lb16_ref, ga_ref, gb_ref, x_ref, t_ref, w_ref, bo_ref,
             loss_ref, dy_ref, gwo_ref, doa_ref, dla_ref, dga_ref, dgb_ref,
             dob_ref, dob4_ref, dob16_ref, dlb_ref, dlb4_ref, dlb16_ref, lse_ref, lse4_ref, lse16_ref,
             s_f):
        i = pl.program_id(0)
        o4, o16 = _unfold_load(ob4_ref, s_f, 4, tm), _unfold_load(ob16_ref, s_f, 16, tm)
        l4, l16 = _unfold_load(lb4_ref, s_f, 4, tm), _unfold_load(lb16_ref, s_f, 16, tm)
        o1, l1 = ob1_ref[...].astype(F32), lb1_ref[...]
        mx = jnp.maximum(jnp.maximum(l1, l4), l16)
        e1, e4, e16 = jnp.exp(l1 - mx), jnp.exp(l4 - mx), jnp.exp(l16 - mx)
        den = e1 + e4 + e16
        ob = (e1 * o1 + e4 * o4 + e16 * o16) / den
        lse_b = mx + jnp.log(den)

        oa, ga, gb = oa_ref[...].astype(F32), ga_ref[...], gb_ref[...]
        sa, sb = _sigmoid(ga), _sigmoid(gb)
        mixed = jnp.concatenate(_unpair_tiles(_tiles(oa * (ga * sa))) + [ob * (gb * sb)], axis=1)
        mixed_bf = mixed.astype(BF16)
        w = w_ref[...]
        yv = x_ref[...] + jnp.dot(mixed_bf, w, preferred_element_type=F32)
        err = yv - t_ref[...]
        sq = jnp.sum(err * err, axis=0, keepdims=True)
        dy = err * (1.0 / D_MODEL)
        dy_ref[...] = dy
        dy_bf = dy.astype(BF16)
        gw = jnp.dot(mixed.T.astype(BF16), dy_bf, preferred_element_type=F32)

        @pl.when(i == 0)
        def _():
            loss_ref[...] = sq
            gwo_ref[...] = gw

        @pl.when(i > 0)
        def _():
            loss_ref[...] += sq
            gwo_ref[...] += gw

        dmix = lax.dot_general(dy_bf, w, (((1,), (1,)), ((), ())), preferred_element_type=F32)
        dma = jnp.concatenate(_pair_tiles(_tiles(dmix[:, :HALF_WIDTH])), axis=1)
        dmb = dmix[:, HALF_WIDTH:]
        bo = bo_ref[...]

        def head_delta(d_o, o):
            prod = d_o * o
            return jnp.concatenate([_head_sum(prod[:, j * PAIR:(j + 1) * PAIR], bo) for j in range(N_PAIRS)], axis=1)

        doa = dma * (ga * sa)
        doa_ref[...] = doa.astype(BF16)
        dla_ref[...] = head_delta(doa, oa)
        dga_ref[...] = dma * oa * (sa * (1.0 + ga * (1.0 - sa)))
        dob = dmb * (gb * sb)
        dgb_ref[...] = dmb * ob * (sb * (1.0 + gb * (1.0 - sb)))
        dlb = head_delta(dob, ob)
        dob_ref[...] = dob.astype(BF16)
        _fold_store(dob, s_f, dob4_ref, dob16_ref, tm)
        dlb_ref[...] = dlb
        _fold_store(dlb, s_f, dlb4_ref, dlb16_ref, tm)
        lse_ref[...] = lse_b
        _fold_store(lse_b, s_f, lse4_ref, lse16_ref, tm)

    row = lambda width: pl.BlockSpec((tm, width), lambda i: (i, 0))
    full = lambda a: pl.BlockSpec(a.shape, lambda i: (0,) * a.ndim)
    fb_shapes, fb_specs = _fold_specs(tm, BF16)
    ff_shapes, ff_specs = _fold_specs(tm, F32)
    nat = lambda dtype: jax.ShapeDtypeStruct((SEQ, HALF_WIDTH), dtype)
    return pl.pallas_call(
        body, name="tail", grid=(SEQ // tm,),
        in_specs=[row(HALF_WIDTH), row(HALF_WIDTH), row(HALF_WIDTH), ff_specs[0], ff_specs[0], ff_specs[1], ff_specs[1],
                  row(HALF_WIDTH), row(HALF_WIDTH), row(D_MODEL), row(D_MODEL), full(w_out), full(bones)],
        out_specs=(pl.BlockSpec((1, D_MODEL), lambda i: (0, 0)), row(D_MODEL),
                   pl.BlockSpec((D_MODEL, D_MODEL), lambda i: (0, 0)),
                   row(HALF_WIDTH), row(HALF_WIDTH), row(HALF_WIDTH), row(HALF_WIDTH),
                   row(HALF_WIDTH), *fb_specs, row(HALF_WIDTH), *ff_specs, row(HALF_WIDTH), *ff_specs),
        out_shape=(jax.ShapeDtypeStruct((1, D_MODEL), F32), jax.ShapeDtypeStruct((SEQ, D_MODEL), F32),
                   jax.ShapeDtypeStruct((D_MODEL, D_MODEL), F32),
                   nat(BF16), nat(F32), nat(F32), nat(F32),
                   nat(BF16), *fb_shapes, nat(F32), *ff_shapes, nat(F32), *ff_shapes),
        scratch_shapes=[_fold_scratch(tm)],
        compiler_params=_params(("arbitrary",)),
    )(oa, ob1, lb1, ob4, lb4, ob16, lb16, gate_a, gate_b, x, target, w_out, bones)


def _dproj_assemble(dqa, dka, dva, dga, dgb, dq1, dk1, dv1, dq4, dk4, dv4, dq16, dk16, dv16, tqa, tqb, tkb, tka,
                    qkg, cos4, sin4, bmean):
    tm = ROW_TILE

    def norm_rope_bwd(d_out, t, g, cos, sin, bm, scale):
        d_r = d_out * scale
        dyv = d_r * cos + _swap_halves(d_r * sin)
        rr = lax.rsqrt(_head_sum(t * t, bm) + EPS)
        that = t * rr
        dgain = jnp.sum(dyv * that, axis=0, keepdims=True)
        gdy = dyv * g
        dt = rr * (gdy - that * _head_sum(that * gdy, bm))
        return dt, dgain

    def body(dqa_ref, dka_ref, dva_ref, dga_ref, dgb_ref, dq1_ref, dk1_ref, dv1_ref, dq4_ref, dk4_ref, dv4_ref,
             dq16_ref, dk16_ref, dv16_ref, tqa_ref, tqb_ref, tkb_ref, tka_ref, qkg_ref, cos_ref, sin_ref, bm_ref,
             dproj_ref, dqkg_ref, s_f):
        i = pl.program_id(0)
        cos, sin, bm = cos_ref[...], sin_ref[...], bm_ref[...]

        def merged(nat_ref, f4_ref, f16_ref):
            return nat_ref[...].astype(F32) + _unfold_load(f4_ref, s_f, 4, tm) + _unfold_load(f16_ref, s_f, 16, tm)

        @pl.when(i == 0)
        def _():
            dqkg_ref[...] = jnp.zeros_like(dqkg_ref)

        def through(d_out, t, row, scale, c0, paired=False):
            g = qkg_ref[row:row + 1, :]
            tot = jnp.zeros((1, PAIR), F32)
            dts = []
            for j in range(d_out.shape[1] // PAIR):
                cols = slice(j * PAIR, (j + 1) * PAIR)
                dt, dg = norm_rope_bwd(d_out[:, cols], t[:, cols], g, cos, sin, bm, scale)
                dts.append(dt)
                tot = tot + dg
            if paired:
                dts = _unpair_tiles(dts)
            for j, dt in enumerate(dts):
                dproj_ref[:, c0 + j * PAIR:c0 + (j + 1) * PAIR] = dt.astype(BF16)
            dqkg_ref[row:row + 1, :] += tot

        through(dqa_ref[...].astype(F32), tqa_ref[...], 0, HEAD_DIM ** -0.5, C_QA, paired=True)
        through(dka_ref[...].astype(F32), tka_ref[...], 1, 1.0, C_KA)
        through(merged(dq1_ref, dq4_ref, dq16_ref), tqb_ref[...], 2, HEAD_DIM ** -0.5, C_QB)
        through(merged(dk1_ref, dk4_ref, dk16_ref), tkb_ref[...], 3, 1.0, C_KB)
        dproj_ref[:, C_VB:C_VB + HALF_WIDTH] = merged(dv1_ref, dv4_ref, dv16_ref).astype(BF16)
        dproj_ref[:, C_GA:C_GA + HALF_WIDTH] = jnp.concatenate(_unpair_tiles(_tiles(dga_ref[...])), axis=1).astype(BF16)
        dproj_ref[:, C_GB:C_GB + HALF_WIDTH] = dgb_ref[...].astype(BF16)
        dproj_ref[:, C_VA:C_VA + KV_A_WIDTH] = dva_ref[...].astype(BF16)

    row = lambda width: pl.BlockSpec((tm, width), lambda i: (i, 0))
    full = lambda a: pl.BlockSpec(a.shape, lambda i: (0,) * a.ndim)
    _, ff_specs = _fold_specs(tm, F32)
    return pl.pallas_call(
        body, name="dproj_assemble", grid=(SEQ // tm,),
        in_specs=[row(HALF_WIDTH), row(KV_A_WIDTH), row(KV_A_WIDTH), row(HALF_WIDTH), row(HALF_WIDTH),
                  row(HALF_WIDTH), row(HALF_WIDTH), row(HALF_WIDTH), ff_specs[0], ff_specs[0], ff_specs[0],
                  ff_specs[1], ff_specs[1], ff_specs[1],
                  row(HALF_WIDTH), row(HALF_WIDTH), row(HALF_WIDTH), row(KV_A_WIDTH),
                  full(qkg), row(PAIR), row(PAIR), full(bmean)],
        out_specs=(row(IN_WIDTH), pl.BlockSpec((SMALL_ROWS, PAIR), lambda i: (0, 0))),
        out_shape=(jax.ShapeDtypeStruct((SEQ, IN_WIDTH), BF16), jax.ShapeDtypeStruct((SMALL_ROWS, PAIR), F32)),
        scratch_shapes=[_fold_scratch(tm)],
        compiler_params=_params(("arbitrary",)),
    )(dqa, dka, dva, dga, dgb, dq1, dk1, dv1, dq4, dk4, dv4, dq16, dk16, dv16, tqa, tqb, tkb, tka, qkg, cos4, sin4, bmean)


def _input_grad(dproj, w, x, gain, dy):
    tm = ROW_TILE

    def body(dp_ref, w_ref, x_ref, g_ref, dy_ref, gx_ref, dgain_ref):
        i = pl.program_id(0)
        dh = lax.dot_general(dp_ref[...], w_ref[...], (((1,), (1,)), ((), ())), preferred_element_type=F32)
        xf = x_ref[...]
        r = lax.rsqrt(jnp.mean(xf * xf, axis=-1, keepdims=True) + EPS)
        xhat = xf * r
        dg = jnp.sum(dh * xhat, axis=0, keepdims=True)
        dxh = dh * g_ref[...]
        dx = r * (dxh - xhat * jnp.mean(dxh * xhat, axis=-1, keepdims=True))
        gx_ref[...] = dy_ref[...] + dx

        @pl.when(i == 0)
        def _():
            dgain_ref[...] = dg

        @pl.when(i > 0)
        def _():
            dgain_ref[...] += dg

    row = lambda width: pl.BlockSpec((tm, width), lambda i: (i, 0))
    full = lambda a: pl.BlockSpec(a.shape, lambda i: (0,) * a.ndim)
    return pl.pallas_call(
        body, name="input_grad", grid=(SEQ // tm,),
        in_specs=[row(IN_WIDTH), full(w), row(D_MODEL), full(gain), row(D_MODEL)],
        out_specs=(row(D_MODEL), pl.BlockSpec((1, D_MODEL), lambda i: (0, 0))),
        out_shape=(jax.ShapeDtypeStruct((SEQ, D_MODEL), F32), jax.ShapeDtypeStruct((1, D_MODEL), F32)),
        compiler_params=_params(("arbitrary",)),
    )(dproj, w, x, gain, dy)


def _weight_grad(h_t, dproj):
    tk = 512
    cb = IN_WIDTH // 2
    n_k = SEQ // tk

    def body(ht_ref, dp_ref, out_ref):
        k = pl.program_id(1)
        upd = jnp.dot(ht_ref[...], dp_ref[...], preferred_element_type=F32)

        @pl.when(k == 0)
        def _():
            out_ref[...] = upd

        @pl.when(k > 0)
        def _():
            out_ref[...] += upd

    return pl.pallas_call(
        body, name="weight_grad", grid=(2, n_k),
        in_specs=[pl.BlockSpec((D_MODEL, tk), lambda j, k: (0, k)), pl.BlockSpec((tk, cb), lambda j, k: (k, j))],
        out_specs=pl.BlockSpec((D_MODEL, cb), lambda j, k: (0, j)),
        out_shape=jax.ShapeDtypeStruct((D_MODEL, IN_WIDTH), F32),
        compiler_params=_params(("arbitrary", "arbitrary")),
    )(h_t, dproj)


def _adamw(name, w, g, m, v):
    def body(w_ref, g_ref, m_ref, v_ref, d_ref, nm_ref, nv_ref):
        gv = g_ref[...]
        nm = ADAM_B1 * m_ref[...] + (1.0 - ADAM_B1) * gv
        nv = ADAM_B2 * v_ref[...] + (1.0 - ADAM_B2) * jnp.square(gv)
        m_hat = nm / (1.0 - ADAM_B1 ** ADAM_STEP)
        v_hat = nv / (1.0 - ADAM_B2 ** ADAM_STEP)
        d_ref[...] = -ADAM_LR * (m_hat / (jnp.sqrt(v_hat) + ADAM_EPS) + ADAM_WD * w_ref[...])
        nm_ref[...] = nm
        nv_ref[...] = nv

    vmem = pl.BlockSpec(memory_space=pltpu.VMEM)
    out = jax.ShapeDtypeStruct(w.shape, F32)
    return pl.pallas_call(
        body, name=name, in_specs=[vmem] * 4, out_specs=(vmem,) * 3, out_shape=(out,) * 3,
        compiler_params=pltpu.CompilerParams(vmem_limit_bytes=VMEM_LIMIT),
    )(w, g, m, v)


SMALL_USED = D_MODEL + 4 * HEAD_DIM + 8


def _pack_small(norm_gain, qa, ka, sinks, qb, kb, extra=None):
    parts = [norm_gain.reshape(-1), qa.reshape(-1), ka.reshape(-1), sinks.reshape(-1), qb.reshape(-1), kb.reshape(-1)]
    if extra is not None:
        parts.append(extra.reshape(-1))
    flat = jnp.concatenate(parts)
    flat = jnp.pad(flat, (0, SMALL_ROWS * SMALL_COLS - flat.shape[0]))
    return flat.reshape(SMALL_ROWS, SMALL_COLS)


def _unpack_small(a):
    flat = a.reshape(-1)
    sizes = (D_MODEL, HEAD_DIM, HEAD_DIM, 8, HEAD_DIM, HEAD_DIM)
    out, off = [], 0
    for s in sizes:
        out.append(flat[off:off + s].reshape(1, s))
        off += s
    return out


def _fold_heads(row):
    return row[0, :HEAD_DIM] + row[0, HEAD_DIM:]


def kernel(x, norm_gain, w_in, q_norm_a, k_norm_a, sinks_a, q_norm_b, k_norm_b, w_out, loss_target, m_norm_gain, m_w_in, m_q_norm_a, m_k_norm_a, m_sinks_a, m_q_norm_b, m_k_norm_b, m_w_out, v_norm_gain, v_w_in, v_q_norm_a, v_k_norm_a, v_sinks_a, v_q_norm_b, v_k_norm_b, v_w_out):
    x2, tgt = x[0], loss_target[0]
    w_in_sh, w_out_sh = w_in[0], w_out[0]

    gathered_in, gathered_out = _all_gather_weights(w_in_sh, w_out_sh)
    w_full = gathered_in.transpose(1, 0, 2).reshape(D_MODEL, IN_WIDTH)
    wo_full = gathered_out.reshape(D_MODEL, D_MODEL)

    inv = np.float32(ROPE_THETA) ** (-np.arange(HEAD_DIM // 2, dtype=np.float32) / np.float32(HEAD_DIM // 2))
    ang = np.arange(SEQ, dtype=np.float32)[:, None] * inv[None, :].astype(np.float32)
    cos, sin = np.cos(ang).astype(np.float32), np.sin(ang).astype(np.float32)
    cos4 = jnp.asarray(np.concatenate([cos, cos, cos, cos], axis=1))
    sin4 = jnp.asarray(np.concatenate([-sin, sin, -sin, sin], axis=1))
    blockdiag = np.kron(np.eye(2, dtype=np.float32), np.ones((HEAD_DIM, HEAD_DIM), np.float32))
    bmean = jnp.asarray(blockdiag / HEAD_DIM, dtype=BF16)
    bones = jnp.asarray(blockdiag, dtype=BF16)
    two = lambda g: jnp.concatenate([g, g], axis=1)
    qkg = jnp.concatenate([two(q_norm_a), two(k_norm_a), two(q_norm_b), two(k_norm_b),
                           jnp.zeros((SMALL_ROWS - 4, PAIR), F32)], axis=0)
    sinks_paired = jnp.stack([sinks_a[0, :N_PAIRS], sinks_a[0, N_PAIRS:]], axis=1)
    sink_rows = jnp.concatenate([jnp.repeat(sinks_paired, BLOCK, axis=1),
                                 jnp.zeros((SMALL_ROWS - N_PAIRS, 2 * BLOCK), F32)], axis=0)

    (tqa, tka, tqb, tkb, gate_a, gate_b, h_t, qa, ka, va, qb, kb, vb, qb4, qb16, kb4, kb16, vb4, vb16) = _proj_fwd(
        x2, norm_gain, w_full, qkg, cos4, sin4, bmean)
    oa, la = _attn_fwd("attn_a_fwd", qa[None], ka[None], va[None], sink_rows, BLOCK - 1)
    ob1, lb1 = _attn_fwd("attn_b1_fwd", qb[None], kb[None], vb[None], None, BLOCK)
    ob4, lb4 = _attn_fwd("attn_b4_fwd", qb4, kb4, vb4, None, BLOCK)
    ob16, lb16 = _attn_fwd("attn_b16_fwd", qb16, kb16, vb16, None, BLOCK)
    (loss_cols, dy, gwo, doa, dla, dga, dgb, dob, dob4, dob16, dlb, dlb4, dlb16, lse_b, lse4, lse16) = _tail(
        oa[0], ob1[0], lb1[0], ob4, lb4, ob16, lb16, gate_a, gate_b, x2, tgt, wo_full, bones)

    dqa, dka, dva, dsink = _attn_bwd("attn_a_bwd", qa[None], ka[None], va[None], doa[None], la, dla[None], sink_rows, BLOCK - 1)
    dq1, dk1, dv1 = _attn_bwd("attn_b1_bwd", qb[None], kb[None], vb[None], dob[None], lse_b[None], dlb[None], None, BLOCK)
    dq4, dk4, dv4 = _attn_bwd("attn_b4_bwd", qb4, kb4, vb4, dob4, lse4, dlb4, None, BLOCK)
    dq16, dk16, dv16 = _attn_bwd("attn_b16_bwd", qb16, kb16, vb16, dob16, lse16, dlb16, None, BLOCK)
    dproj, dqkg = _dproj_assemble(dqa[0], dka[0], dva[0], dga, dgb, dq1[0], dk1[0], dv1[0], dq4, dk4, dv4,
                                  dq16, dk16, dv16, tqa, tqb, tkb, tka, qkg, cos4, sin4, bmean)
    grad_x, dgain = _input_grad(dproj, w_full, x2, norm_gain, dy)
    gw_in = _weight_grad(h_t, dproj)

    blocks_in = gw_in.reshape(D_MODEL, N_DEV, SHARD_IN).transpose(1, 0, 2)
    blocks_out = gwo.reshape(N_DEV, SHARD_OUT, D_MODEL)
    g_sinks = jnp.concatenate([jnp.sum(dsink[:N_PAIRS, :BLOCK], axis=1), jnp.sum(dsink[:N_PAIRS, BLOCK:], axis=1)])
    small = _pack_small(dgain, _fold_heads(dqkg[0:1]), _fold_heads(dqkg[1:2]), g_sinks,
                        _fold_heads(dqkg[2:3]), _fold_heads(dqkg[3:4]), extra=0.5 * jnp.sum(loss_cols) / D_MODEL)
    g_w_in, g_w_out, small_red = _reduce_scatter_grads(blocks_in, blocks_out, small)
    g_small = _unpack_small(small_red)

    d_in, nm_in, nv_in = _adamw("adamw_w_in", w_in_sh, g_w_in, m_w_in[0], v_w_in[0])
    d_out, nm_out, nv_out = _adamw("adamw_w_out", w_out_sh, g_w_out, m_w_out[0], v_w_out[0])
    d_s, nm_s, nv_s = _adamw(
        "adamw_small",
        _pack_small(norm_gain, q_norm_a, k_norm_a, sinks_a, q_norm_b, k_norm_b), small_red,
        _pack_small(m_norm_gain, m_q_norm_a, m_k_norm_a, m_sinks_a, m_q_norm_b, m_k_norm_b),
        _pack_small(v_norm_gain, v_q_norm_a, v_k_norm_a, v_sinks_a, v_q_norm_b, v_k_norm_b))
    d_small, nm_small, nv_small = _unpack_small(d_s), _unpack_small(nm_s), _unpack_small(nv_s)

    loss = small_red.reshape(-1)[SMALL_USED]

    def assemble(small_list, big_in, big_out):
        ng, qa_, ka_, sk_, qb_, kb_ = small_list
        return [ng, big_in[None], qa_, ka_, sk_, qb_, kb_, big_out[None]]

    return (loss, grad_x[None], *assemble(g_small, g_w_in, g_w_out), *assemble(d_small, d_in, d_out),
            *assemble(nm_small, nm_in, nm_out), *assemble(nv_small, nv_in, nv_out))
```

```python
import functools

import numpy as np
import jax
import jax.numpy as jnp
from jax import lax
from jax.experimental import pallas as pl
from jax.experimental.pallas import tpu as pltpu

F32 = jnp.float32
BF16 = jnp.bfloat16

SEQ = 4096
D_MODEL = 1024
HEAD_DIM = 64
PAIR = 2 * HEAD_DIM
N_PAIRS = 4
HALF_WIDTH = N_PAIRS * PAIR
KV_A_WIDTH = 128
IN_WIDTH = 3328
BLOCK = 128
EPS = 1e-6
NEG = -1e30
ROPE_THETA = 10000.0
N_DEV = 8
SHARD_IN = IN_WIDTH // N_DEV
SHARD_OUT = D_MODEL // N_DEV
PAYLOAD = SHARD_IN + SHARD_OUT
SMALL_ROWS, SMALL_COLS = 8, 256

C_QA, C_KA, C_VA, C_GA, C_QB, C_KB, C_VB, C_GB = 0, 512, 640, 768, 1280, 1792, 2304, 2816

ADAM_LR = 0.001
ADAM_B1 = 0.9
ADAM_B2 = 0.999
ADAM_EPS = 1e-08
ADAM_WD = 0.01
ADAM_STEP = 10

ROW_TILE = 256
FWD_BLOCKS_PER_STEP = 4
BWD_BLOCKS_PER_STEP = 4
VMEM_LIMIT = 56 * 1024 * 1024

MESH = pl.DeviceIdType.MESH


def _params(sem, vmem=VMEM_LIMIT):
    return pltpu.CompilerParams(dimension_semantics=sem, vmem_limit_bytes=vmem)


def _head_sum(v, bm):
    hi = v.astype(BF16)
    lo = (v - hi.astype(F32)).astype(BF16)
    return (jnp.dot(hi, bm, preferred_element_type=F32) + jnp.dot(lo, bm, preferred_element_type=F32))


def _swap_halves(y):
    lane = lax.broadcasted_iota(jnp.int32, y.shape, 1)
    first = (lane & 32) == 0
    return jnp.where(first, pltpu.roll(y, 96, 1), pltpu.roll(y, 32, 1))


def _sigmoid(g):
    return 1.0 / (1.0 + jnp.exp(-g))


def _tiles(a):
    return [a[:, j * PAIR:(j + 1) * PAIR] for j in range(N_PAIRS)]


def _pair_tiles(t):
    low = lax.broadcasted_iota(jnp.int32, t[0].shape, 1) < HEAD_DIM
    r = [pltpu.roll(a, HEAD_DIM, 1) for a in t]
    return [jnp.where(low, t[0], r[2]), jnp.where(low, r[0], t[2]), jnp.where(low, t[1], r[3]), jnp.where(low, r[1], t[3])]


def _unpair_tiles(p):
    low = lax.broadcasted_iota(jnp.int32, p[0].shape, 1) < HEAD_DIM
    r = [pltpu.roll(a, HEAD_DIM, 1) for a in p]
    return [jnp.where(low, p[0], r[1]), jnp.where(low, p[2], r[3]), jnp.where(low, r[0], p[1]), jnp.where(low, r[2], p[3])]


def _all_gather_weights(w_in_sh, w_out_sh):
    shapes = (w_in_sh.shape, w_out_sh.shape)

    def body(a_ref, b_ref, out_a, out_b, mine_a, mine_b, send_sems, recv_sems):
        x, y, c = lax.axis_index("x"), lax.axis_index("y"), lax.axis_index("c")
        me, sibling = (x, y, c), (x, y, 1 - c)
        chips = [(1 - x, y), (x, 1 - y), (1 - x, 1 - y)]
        parts = ((mine_a, out_a), (mine_b, out_b))

        def slot(a, px, py, pc):
            return parts[a][1].at[4 * px + 2 * py + pc]

        def copy(a, k, block, to, from_mine=False):
            return pltpu.make_async_remote_copy(
                src_ref=parts[a][0] if from_mine else slot(a, *block), dst_ref=slot(a, *block),
                send_sem=send_sems.at[a, k], recv_sem=recv_sems.at[a, k], device_id=to, device_id_type=MESH)

        mine_a[...] = a_ref[...].astype(BF16)
        mine_b[...] = b_ref[...].astype(BF16)
        both = (0, 1)
        first = [copy(a, 0, me, sibling, True) for a in both]
        first += [copy(a, 1 + j, me, (*chip, c), True) for j, chip in enumerate(chips) for a in both]
        for cp in first:
            cp.start()
        slot(0, *me)[...] = mine_a[...]
        slot(1, *me)[...] = mine_b[...]
        passed = []
        for j, chip in enumerate(chips):
            for a in both:
                copy(a, 1 + j, (*chip, c), me).wait_recv()
                passed.append(copy(a, 4 + j, (*chip, c), sibling))
                passed[-1].start()
        for a in both:
            copy(a, 0, sibling, me).wait_recv()
            for j, chip in enumerate(chips):
                copy(a, 4 + j, (*chip, 1 - c), me).wait_recv()
        for cp in first + passed:
            cp.wait_send()

    vmem = pl.BlockSpec(memory_space=pltpu.VMEM)
    return pl.pallas_call(
        body, name="ag_weights",
        out_shape=tuple(jax.ShapeDtypeStruct((N_DEV,) + s, BF16) for s in shapes),
        in_specs=[vmem, vmem], out_specs=(vmem, vmem),
        scratch_shapes=[pltpu.VMEM(shapes[0], BF16), pltpu.VMEM(shapes[1], BF16),
                        pltpu.SemaphoreType.DMA((2, 7)), pltpu.SemaphoreType.DMA((2, 7))],
        compiler_params=pltpu.CompilerParams(vmem_limit_bytes=VMEM_LIMIT),
    )(w_in_sh, w_out_sh)


def _reduce_scatter_grads(blocks_in, blocks_out, small):
    shapes = (blocks_in.shape[1:], blocks_out.shape[1:])

    def body(ga_hbm, gb_hbm, small_ref, out_a, out_b, small_out_ref,
             part_a, part_b, sib_a, sib_b, wire_a, wire_b, chips_a, chips_b, small_all,
             load_sems, sib_send, sib_recv, chip_send, chip_recv, small_send, small_recv):
        x, y, c = lax.axis_index("x"), lax.axis_index("y"), lax.axis_index("c")
        sibling = (x, y, 1 - c)
        chips = [(x, y), (1 - x, y), (x, 1 - y), (1 - x, 1 - y)]
        my_id = 4 * x + 2 * y + c
        g_hbm, part, from_sib = (ga_hbm, gb_hbm), (part_a, part_b), (sib_a, sib_b)
        to_wire, from_chips, out = (wire_a, wire_b), (chips_a, chips_b), (out_a, out_b)
        both = (0, 1)

        def blk(a, chip, core):
            return g_hbm[a].at[4 * chip[0] + 2 * chip[1] + core]

        small_all[my_id] = small_ref[...]
        small_copies = []
        for rel in range(1, N_DEV):
            dx, dy, dc = (rel >> 2) & 1, (rel >> 1) & 1, rel & 1
            to = (1 - x if dx else x, 1 - y if dy else y, 1 - c if dc else c)
            small_copies.append(pltpu.make_async_remote_copy(
                src_ref=small_ref, dst_ref=small_all.at[my_id],
                send_sem=small_send.at[rel - 1], recv_sem=small_recv.at[rel - 1], device_id=to, device_id_type=MESH))
        for cp in small_copies:
            cp.start()

        loads = [[pltpu.make_async_copy(blk(a, chips[k], c), part[a].at[k], load_sems.at[a, k]) for k in range(4)] for a in both]
        to_sib = [[pltpu.make_async_remote_copy(
            src_ref=blk(a, chips[k], 1 - c), dst_ref=from_sib[a].at[k], send_sem=sib_send.at[a, k], recv_sem=sib_recv.at[a, k],
            device_id=sibling, device_id_type=MESH) for k in range(4)] for a in both]
        for k in (1, 2, 3, 0):
            for a in both:
                loads[a][k].start()
                to_sib[a][k].start()

        to_chips = [[pltpu.make_async_remote_copy(
            src_ref=to_wire[a].at[k - 1], dst_ref=from_chips[a].at[k - 1],
            send_sem=chip_send.at[a, k - 1], recv_sem=chip_recv.at[a, k - 1],
            device_id=(*chips[k], c), device_id_type=MESH) for k in range(1, 4)] for a in both]
        for k in (1, 2, 3):
            for a in both:
                loads[a][k].wait()
                to_sib[a][k].wait_recv()
                to_wire[a][k - 1] = (part[a][k] + from_sib[a][k]).astype(BF16)
                to_chips[a][k - 1].start()
        for a in both:
            loads[a][0].wait()
            to_sib[a][0].wait_recv()
            acc = part[a][0] + from_sib[a][0]
            for k in range(3):
                to_chips[a][k].wait_recv()
                acc = acc + from_chips[a][k].astype(F32)
            out[a][...] = acc

        for cp in small_copies:
            cp.wait_recv()
        tot = small_all[0]
        for d in range(1, N_DEV):
            tot = tot + small_all[d]
        small_out_ref[...] = tot
        for cp in to_sib[0] + to_sib[1] + to_chips[0] + to_chips[1] + small_copies:
            cp.wait_send()

    vmem = pl.BlockSpec(memory_space=pltpu.VMEM)
    hbm = pl.BlockSpec(memory_space=pl.ANY)
    buf = lambda n, dtype: [pltpu.VMEM((n,) + s, dtype) for s in shapes]
    return pl.pallas_call(
        body, name="rs_grads",
        out_shape=(jax.ShapeDtypeStruct(shapes[0], F32), jax.ShapeDtypeStruct(shapes[1], F32),
                   jax.ShapeDtypeStruct((SMALL_ROWS, SMALL_COLS), F32)),
        in_specs=[hbm, hbm, vmem], out_specs=(vmem, vmem, vmem),
        scratch_shapes=[*buf(4, F32), *buf(4, F32), *buf(3, BF16), *buf(3, BF16),
                        pltpu.VMEM((N_DEV, SMALL_ROWS, SMALL_COLS), F32),
                        pltpu.SemaphoreType.DMA((2, 4)), pltpu.SemaphoreType.DMA((2, 4)), pltpu.SemaphoreType.DMA((2, 4)),
                        pltpu.SemaphoreType.DMA((2, 3)), pltpu.SemaphoreType.DMA((2, 3)),
                        pltpu.SemaphoreType.DMA((7,)), pltpu.SemaphoreType.DMA((7,))],
        compiler_params=pltpu.CompilerParams(vmem_limit_bytes=VMEM_LIMIT),
    )(blocks_in, blocks_out, small)


def _fold_scratch(tm):
    return pltpu.VMEM((N_PAIRS, tm, PAIR), F32)


def _fold_store(val, scr, out4, out16, tm):
    for j in range(N_PAIRS):
        scr[j] = val[:, j * PAIR:(j + 1) * PAIR]
    for dil, out in ((4, out4), (16, out16)):
        for r in range(dil):
            for j in range(N_PAIRS):
                out[r, :, j * PAIR:(j + 1) * PAIR] = scr[j, pl.ds(r, tm // dil, stride=dil), :].astype(out.dtype)


def _unfold_load(src, scr, dil, tm):
    for r in range(dil):
        for j in range(N_PAIRS):
            scr[j, pl.ds(r, tm // dil, stride=dil), :] = src[r, :, j * PAIR:(j + 1) * PAIR].astype(F32)
    return jnp.concatenate([scr[j] for j in range(N_PAIRS)], axis=1)


def _fold_specs(tm, dtype):
    shapes = (jax.ShapeDtypeStruct((4, SEQ // 4, HALF_WIDTH), dtype), jax.ShapeDtypeStruct((16, SEQ // 16, HALF_WIDTH), dtype))
    specs = (pl.BlockSpec((4, tm // 4, HALF_WIDTH), lambda i: (0, i, 0)),
             pl.BlockSpec((16, tm // 16, HALF_WIDTH), lambda i: (0, i, 0)))
    return shapes, specs


def _proj_fwd(x, gain, w, qkg, cos4, sin4, bmean):
    tm = ROW_TILE

    def norm_rope(t, g, cos, sin, bm, scale):
        rr = lax.rsqrt(_head_sum(t * t, bm) + EPS)
        yv = t * rr * g
        return (yv * cos + _swap_halves(yv) * sin) * scale

    def body(x_ref, g_ref, w_ref, qkg_ref, cos_ref, sin_ref, bm_ref,
             tqa_ref, tka_ref, tqb_ref, tkb_ref, ga_ref, gb_ref, ht_ref, qa_ref, ka_ref, va_ref, qb_ref, kb_ref, vb_ref,
             qb4_ref, qb16_ref, kb4_ref, kb16_ref, vb4_ref, vb16_ref, proj, scr):
        xf = x_ref[...]
        r = lax.rsqrt(jnp.mean(xf * xf, axis=-1, keepdims=True) + EPS)
        hf = xf * r * g_ref[...]
        ht_ref[...] = hf.T.astype(BF16)
        cos, sin, bm = cos_ref[...], sin_ref[...], bm_ref[...]
        proj[...] = jnp.dot(hf.astype(BF16), w_ref[...], preferred_element_type=F32)

        def roped(tiles, row, scale):
            g = qkg_ref[row:row + 1, :]
            return jnp.concatenate([norm_rope(t, g, cos, sin, bm, scale) for t in tiles], axis=1)

        tqa = _pair_tiles(_tiles(proj[:, C_QA:C_QA + HALF_WIDTH]))
        tqa_ref[...] = jnp.concatenate(tqa, axis=1)
        qa_ref[...] = roped(tqa, 0, HEAD_DIM ** -0.5).astype(BF16)
        ga_ref[...] = jnp.concatenate(_pair_tiles(_tiles(proj[:, C_GA:C_GA + HALF_WIDTH])), axis=1)
        gb_ref[...] = proj[:, C_GB:C_GB + HALF_WIDTH]
        tqb = proj[:, C_QB:C_QB + HALF_WIDTH]
        tqb_ref[...] = tqb
        qb = roped(_tiles(tqb), 2, HEAD_DIM ** -0.5)
        qb_ref[...] = qb.astype(BF16)
        _fold_store(qb, scr, qb4_ref, qb16_ref, tm)
        tkb = proj[:, C_KB:C_KB + HALF_WIDTH]
        tkb_ref[...] = tkb
        kb = roped(_tiles(tkb), 3, 1.0)
        kb_ref[...] = kb.astype(BF16)
        _fold_store(kb, scr, kb4_ref, kb16_ref, tm)
        vb = proj[:, C_VB:C_VB + HALF_WIDTH]
        vb_ref[...] = vb.astype(BF16)
        _fold_store(vb, scr, vb4_ref, vb16_ref, tm)
        tka = proj[:, C_KA:C_KA + KV_A_WIDTH]
        tka_ref[...] = tka
        ka_ref[...] = roped([tka], 1, 1.0).astype(BF16)
        va_ref[...] = proj[:, C_VA:C_VA + KV_A_WIDTH].astype(BF16)

    row = lambda width: pl.BlockSpec((tm, width), lambda i: (i, 0))
    full = lambda a: pl.BlockSpec(a.shape, lambda i: (0,) * a.ndim)
    nat = lambda width, dtype=BF16: jax.ShapeDtypeStruct((SEQ, width), dtype)
    f_shapes, f_specs = _fold_specs(tm, BF16)
    return pl.pallas_call(
        body, name="proj_fwd", grid=(SEQ // tm,),
        in_specs=[row(D_MODEL), full(gain), full(w), full(qkg), row(PAIR), row(PAIR), full(bmean)],
        out_specs=(row(HALF_WIDTH), row(KV_A_WIDTH), row(HALF_WIDTH), row(HALF_WIDTH), row(HALF_WIDTH), row(HALF_WIDTH),
                   pl.BlockSpec((D_MODEL, tm), lambda i: (0, i)),
                   row(HALF_WIDTH), row(KV_A_WIDTH), row(KV_A_WIDTH), row(HALF_WIDTH), row(HALF_WIDTH), row(HALF_WIDTH),
                   *f_specs, *f_specs, *f_specs),
        out_shape=(nat(HALF_WIDTH, F32), nat(KV_A_WIDTH, F32), nat(HALF_WIDTH, F32), nat(HALF_WIDTH, F32),
                   nat(HALF_WIDTH, F32), nat(HALF_WIDTH, F32),
                   jax.ShapeDtypeStruct((D_MODEL, SEQ), BF16),
                   nat(HALF_WIDTH), nat(KV_A_WIDTH), nat(KV_A_WIDTH), nat(HALF_WIDTH), nat(HALF_WIDTH), nat(HALF_WIDTH),
                   *f_shapes, *f_shapes, *f_shapes),
        scratch_shapes=[pltpu.VMEM((tm, IN_WIDTH), F32), _fold_scratch(tm)],
        compiler_params=_params(("arbitrary",)),
    )(x, gain, w, qkg, cos4, sin4, bmean)


def _band_mask(i, max_dist):
    j = lax.broadcasted_iota(jnp.int32, (2 * BLOCK, 2 * BLOCK), 0)
    c = lax.broadcasted_iota(jnp.int32, (2 * BLOCK, 2 * BLOCK), 1)
    dist = (c & (BLOCK - 1)) + BLOCK - j
    return (dist >= 0) & (dist <= max_dist) & ((j >= BLOCK) | (i > 0))


def _stack_heads(t):
    lane = lax.broadcasted_iota(jnp.int32, t.shape, 1)
    low = lane < HEAD_DIM
    zero = jnp.zeros_like(t)
    return jnp.concatenate([jnp.where(low, t, zero), jnp.where(low, zero, t)], axis=0)


def _unstack_t(t):
    return jnp.concatenate([t[:HEAD_DIM, :BLOCK], t[HEAD_DIM:, BLOCK:]], axis=0).T


def _rows_to_pair(row):
    return jnp.concatenate([jnp.broadcast_to(row[:, :BLOCK], (HEAD_DIM, BLOCK)),
                            jnp.broadcast_to(row[:, BLOCK:], (HEAD_DIM, BLOCK))], axis=0).T


def _pair_to_rows(t):
    tt = t.T
    return jnp.concatenate([tt[0:1, :], tt[HEAD_DIM:HEAD_DIM + 1, :]], axis=1)


def _attn_fwd(name, q, k, v, sink_rows, max_dist):
    n_seq, length, _ = q.shape
    ck = k.shape[2]
    nb = length // BLOCK
    shared = ck == PAIR
    has_sinks = sink_rows is not None

    qb = FWD_BLOCKS_PER_STEP

    def body(*refs):
        if has_sinks:
            q_ref, kc_ref, vc_ref, sink_ref, o_ref, lse_ref, kp_ref, vp_ref = refs
        else:
            q_ref, kc_ref, vc_ref, o_ref, lse_ref, kp_ref, vp_ref = refs
        step = pl.program_id(0)

        @pl.when(step == 0)
        def _():
            kp_ref[...] = jnp.zeros_like(kp_ref)
            vp_ref[...] = jnp.zeros_like(vp_ref)

        valid = [_band_mask((step * qb + b) & (nb - 1), max_dist) for b in range(qb)]
        cols = [slice(p * PAIR, (p + 1) * PAIR) for p in range(N_PAIRS)]
        kcols = [slice(0, PAIR) if shared else c for c in cols]
        rows = [slice(b * BLOCK, (b + 1) * BLOCK) for b in range(qb)]
        units = [(b, p) for b in range(qb) for p in range(N_PAIRS)]
        n = range(len(units))

        def window(prev_ref, cur_ref, b, kc):
            before = prev_ref[:, kc] if b == 0 else cur_ref[rows[b - 1], kc]
            return jnp.concatenate([before, cur_ref[rows[b], kc]], axis=0)

        st = [lax.dot_general(window(kp_ref, kc_ref, b, kcols[p]), _stack_heads(q_ref[rows[b], cols[p]]),
                              (((1,), (1,)), ((), ())), preferred_element_type=F32) for b, p in units]
        st = [jnp.where(valid[units[u][0]], st[u], NEG) for u in n]
        m = [jnp.max(s, axis=0, keepdims=True) for s in st]
        if has_sinks:
            sk = [sink_ref[p:p + 1, :] for _, p in units]
            m = [jnp.maximum(m[u], sk[u]) for u in n]
        pt = [jnp.exp(st[u] - m[u]) for u in n]
        l = [jnp.sum(t, axis=0, keepdims=True) for t in pt]
        if has_sinks:
            l = [l[u] + jnp.exp(sk[u] - m[u]) for u in n]
        v2t = [window(vp_ref, vc_ref, b, kcols[p]).astype(F32).T.astype(BF16) for b, p in units]
        ot = [jnp.dot(v2t[u], pt[u].astype(BF16), preferred_element_type=F32) / l[u] for u in n]
        for u, (b, p) in enumerate(units):
            o_ref[rows[b], cols[p]] = _unstack_t(ot[u]).astype(BF16)
            lse_ref[rows[b], cols[p]] = _rows_to_pair(m[u] + jnp.log(l[u]))
        kp_ref[...] = kc_ref[rows[-1], :]
        vp_ref[...] = vc_ref[rows[-1], :]

    cur = lambda width: pl.BlockSpec((qb * BLOCK, width), lambda s: (s, 0))
    flat = lambda a: a.reshape(n_seq * length, a.shape[2])
    in_specs = [cur(HALF_WIDTH), cur(ck), cur(ck)]
    args = [flat(q), flat(k), flat(v)]
    if has_sinks:
        in_specs.append(pl.BlockSpec(sink_rows.shape, lambda s: (0, 0)))
        args.append(sink_rows)
    out = lambda dtype: jax.ShapeDtypeStruct((n_seq * length, HALF_WIDTH), dtype)
    o, lse = pl.pallas_call(
        body, name=name, grid=(n_seq * nb // qb,), in_specs=in_specs,
        out_specs=(cur(HALF_WIDTH), cur(HALF_WIDTH)), out_shape=(out(BF16), out(F32)),
        scratch_shapes=[pltpu.VMEM((BLOCK, ck), BF16), pltpu.VMEM((BLOCK, ck), BF16)],
        compiler_params=_params(("arbitrary",)),
    )(*args)
    return o.reshape(n_seq, length, HALF_WIDTH), lse.reshape(n_seq, length, HALF_WIDTH)


def _attn_bwd(name, q, k, v, d_o, lse, delta, sink_rows, max_dist):
    n_seq, length, _ = q.shape
    ck = k.shape[2]
    nb = length // BLOCK
    n_blocks = n_seq * nb
    n_rows = n_seq * length
    shared = ck == PAIR
    has_sinks = sink_rows is not None
    qb = BWD_BLOCKS_PER_STEP
    n_steps = n_blocks // qb

    def body(*refs):
        if has_sinks:
            (q_ref, kc_ref, vc_ref, do_ref, lse_ref, dl_ref, sink_ref,
             dq_ref, dk_ref, dv_ref, dsink_ref, ck_scr, cv_scr, kp_ref, vp_ref) = refs
        else:
            (q_ref, kc_ref, vc_ref, do_ref, lse_ref, dl_ref,
             dq_ref, dk_ref, dv_ref, ck_scr, cv_scr, kp_ref, vp_ref) = refs
        step = pl.program_id(0)

        @pl.when(step == 0)
        def _():
            ck_scr[...] = jnp.zeros_like(ck_scr)
            cv_scr[...] = jnp.zeros_like(cv_scr)
            kp_ref[...] = jnp.zeros_like(kp_ref)
            vp_ref[...] = jnp.zeros_like(vp_ref)
            if has_sinks:
                dsink_ref[...] = jnp.zeros_like(dsink_ref)

        valid = [_band_mask((step * qb + b) & (nb - 1), max_dist) for b in range(qb)]
        cols = [slice(p * PAIR, (p + 1) * PAIR) for p in range(N_PAIRS)]
        kcols = [slice(0, PAIR) if shared else c for c in cols]
        rows = [slice(b * BLOCK, (b + 1) * BLOCK) for b in range(qb)]
        units = [(b, p) for b in range(qb) for p in range(N_PAIRS)]
        n = range(len(units))
        nt = (((1,), (1,)), ((), ()))

        def window(prev_ref, cur_ref, b, kc):
            before = prev_ref[:, kc] if b == 0 else cur_ref[rows[b - 1], kc]
            return jnp.concatenate([before, cur_ref[rows[b], kc]], axis=0)

        q_st = [_stack_heads(q_ref[rows[b], cols[p]]) for b, p in units]
        do_st = [_stack_heads(do_ref[rows[b], cols[p]]) for b, p in units]
        k2 = [window(kp_ref, kc_ref, b, kcols[p]) for b, p in units]
        v2 = [window(vp_ref, vc_ref, b, kcols[p]) for b, p in units]
        st = [lax.dot_general(k2[u], q_st[u], nt, preferred_element_type=F32) for u in n]
        dpt = [lax.dot_general(v2[u], do_st[u], nt, preferred_element_type=F32) for u in n]
        lse_row = [_pair_to_rows(lse_ref[rows[b], cols[p]]) for b, p in units]
        dl_row = [_pair_to_rows(dl_ref[rows[b], cols[p]]) for b, p in units]
        pt = [jnp.exp(jnp.where(valid[units[u][0]], st[u], NEG) - lse_row[u]) for u in n]
        dst = [(pt[u] * (dpt[u] - dl_row[u])).astype(BF16) for u in n]
        ptb = [t.astype(BF16) for t in pt]
        dv2 = [jnp.dot(ptb[u], do_st[u], preferred_element_type=F32) for u in n]
        dk2 = [jnp.dot(dst[u], q_st[u], preferred_element_type=F32) for u in n]
        k2t = [k2[u].astype(F32).T.astype(BF16) for u in n]
        dqt = [jnp.dot(k2t[u], dst[u], preferred_element_type=F32) for u in n]
        for u, (b, p) in enumerate(units):
            dq_ref[rows[b], cols[p]] = _unstack_t(dqt[u]).astype(BF16)
        if has_sinks:
            for u, (b, p) in enumerate(units):
                p_sink = jnp.exp(sink_ref[p:p + 1, :] - lse_row[u])
                dsink_ref[p:p + 1, :] = dsink_ref[p:p + 1, :] - p_sink * dl_row[u]

        def total(parts, w, group):
            sel = [u for u, (b, p) in enumerate(units) if (shared or p == group)]
            terms = ([parts[u][:BLOCK] for u in sel if units[u][0] == w]
                     + [parts[u][BLOCK:] for u in sel if units[u][0] == w - 1])
            tot = terms[0]
            for t in terms[1:]:
                tot = tot + t
            return tot

        first_row = step * (qb * BLOCK)
        for acc_ref, out_ref, parts in ((ck_scr, dk_ref, dk2), (cv_scr, dv_ref, dv2)):
            for group in range(1 if shared else N_PAIRS):
                kc = kcols[group]

                @pl.when(step > 0)
                def _():
                    out_ref[pl.ds(pl.multiple_of(first_row - BLOCK, BLOCK), BLOCK), kc] = (
                        acc_ref[:, kc] + total(parts, 0, group)).astype(BF16)

                for w in range(1, qb):
                    out_ref[pl.ds(pl.multiple_of(first_row + (w - 1) * BLOCK, BLOCK), BLOCK), kc] = (
                        total(parts, w, group).astype(BF16))
                acc_ref[:, kc] = total(parts, qb, group)

        @pl.when(step == n_steps - 1)
        def _():
            dk_ref[pl.ds(n_rows - BLOCK, BLOCK), :] = ck_scr[...].astype(BF16)
            dv_ref[pl.ds(n_rows - BLOCK, BLOCK), :] = cv_scr[...].astype(BF16)

        kp_ref[...] = kc_ref[rows[-1], :]
        vp_ref[...] = vc_ref[rows[-1], :]

    cur = lambda width: pl.BlockSpec((qb * BLOCK, width), lambda s: (s, 0))
    whole = lambda width: pl.BlockSpec((n_rows, width), lambda s: (0, 0))
    flat = lambda a: a.reshape(n_rows, a.shape[2])
    in_specs = [cur(HALF_WIDTH), cur(ck), cur(ck), cur(HALF_WIDTH), cur(HALF_WIDTH), cur(HALF_WIDTH)]
    args = [flat(a) for a in (q, k, v, d_o, lse, delta)]
    out_specs = [cur(HALF_WIDTH), whole(ck), whole(ck)]
    out_shape = [jax.ShapeDtypeStruct((n_rows, HALF_WIDTH), BF16),
                 jax.ShapeDtypeStruct((n_rows, ck), BF16), jax.ShapeDtypeStruct((n_rows, ck), BF16)]
    if has_sinks:
        in_specs.append(pl.BlockSpec(sink_rows.shape, lambda s: (0, 0)))
        args.append(sink_rows)
        out_specs.append(pl.BlockSpec(sink_rows.shape, lambda s: (0, 0)))
        out_shape.append(jax.ShapeDtypeStruct(sink_rows.shape, F32))
    outs = pl.pallas_call(
        body, name=name, grid=(n_steps,), in_specs=in_specs,
        out_specs=tuple(out_specs), out_shape=tuple(out_shape),
        scratch_shapes=[pltpu.VMEM((BLOCK, ck), F32), pltpu.VMEM((BLOCK, ck), F32),
                        pltpu.VMEM((BLOCK, ck), BF16), pltpu.VMEM((BLOCK, ck), BF16)],
        compiler_params=_params(("arbitrary",)),
    )(*args)
    return tuple(o.reshape(n_seq, length, o.shape[1]) for o in outs[:3]) + tuple(outs[3:])


def _tail(oa, ob1, lb1, ob4, lb4, ob16, lb16, gate_a, gate_b, x, target, w_out, bones):
    tm = ROW_TILE

    def body(oa_ref, ob1_ref, lb1_ref, ob4_ref, lb4_ref, ob16_ref, lb16_ref, ga_ref, gb_ref, x_ref, t_ref, w_ref, bo_ref,
             loss_ref, dy_ref, gwo_ref, doa_ref, dla_ref, dga_ref, dgb_ref,
             dob_ref, dob4_ref, dob16_ref, dlb_ref, dlb4_ref, dlb16_ref, lse_ref, lse4_ref, lse16_ref,
             s_f):
        i = pl.program_id(0)
        o4, o16 = _unfold_load(ob4_ref, s_f, 4, tm), _unfold_load(ob16_ref, s_f, 16, tm)
        l4, l16 = _unfold_load(lb4_ref, s_f, 4, tm), _unfold_load(lb16_ref, s_f, 16, tm)
        o1, l1 = ob1_ref[...].astype(F32), lb1_ref[...]
        mx = jnp.maximum(jnp.maximum(l1, l4), l16)
        e1, e4, e16 = jnp.exp(l1 - mx), jnp.exp(l4 - mx), jnp.exp(l16 - mx)
        den = e1 + e4 + e16
        ob = (e1 * o1 + e4 * o4 + e16 * o16) / den
        lse_b = mx + jnp.log(den)

        oa, ga, gb = oa_ref[...].astype(F32), ga_ref[...], gb_ref[...]
        sa, sb = _sigmoid(ga), _sigmoid(gb)
        mixed = jnp.concatenate(_unpair_tiles(_tiles(oa * (ga * sa))) + [ob * (gb * sb)], axis=1)
        mixed_bf = mixed.astype(BF16)
        w = w_ref[...]
        yv = x_ref[...] + jnp.dot(mixed_bf, w, preferred_element_type=F32)
        err = yv - t_ref[...]
        sq = jnp.sum(err * err, axis=0, keepdims=True)
        dy = err * (1.0 / D_MODEL)
        dy_ref[...] = dy
        dy_bf = dy.astype(BF16)
        gw = jnp.dot(mixed.T.astype(BF16), dy_bf, preferred_element_type=F32)

        @pl.when(i == 0)
        def _():
            loss_ref[...] = sq
            gwo_ref[...] = gw

        @pl.when(i > 0)
        def _():
            loss_ref[...] += sq
            gwo_ref[...] += gw

        dmix = lax.dot_general(dy_bf, w, (((1,), (1,)), ((), ())), preferred_element_type=F32)
        dma = jnp.concatenate(_pair_tiles(_tiles(dmix[:, :HALF_WIDTH])), axis=1)
        dmb = dmix[:, HALF_WIDTH:]
        bo = bo_ref[...]

        def head_delta(d_o, o):
            prod = d_o * o
            return jnp.concatenate([_head_sum(prod[:, j * PAIR:(j + 1) * PAIR], bo) for j in range(N_PAIRS)], axis=1)

        doa = dma * (ga * sa)
        doa_ref[...] = doa.astype(BF16)
        dla_ref[...] = head_delta(doa, oa)
        dga_ref[...] = dma * oa * (sa * (1.0 + ga * (1.0 - sa)))
        dob = dmb * (gb * sb)
        dgb_ref[...] = dmb * ob * (sb * (1.0 + gb * (1.0 - sb)))
        dlb = head_delta(dob, ob)
        dob_ref[...] = dob.astype(BF16)
        _fold_store(dob, s_f, dob4_ref, dob16_ref, tm)
        dlb_ref[...] = dlb
        _fold_store(dlb, s_f, dlb4_ref, dlb16_ref, tm)
        lse_ref[...] = lse_b
        _fold_store(lse_b, s_f, lse4_ref, lse16_ref, tm)

    row = lambda width: pl.BlockSpec((tm, width), lambda i: (i, 0))
    full = lambda a: pl.BlockSpec(a.shape, lambda i: (0,) * a.ndim)
    fb_shapes, fb_specs = _fold_specs(tm, BF16)
    ff_shapes, ff_specs = _fold_specs(tm, F32)
    nat = lambda dtype: jax.ShapeDtypeStruct((SEQ, HALF_WIDTH), dtype)
    return pl.pallas_call(
        body, name="tail", grid=(SEQ // tm,),
        in_specs=[row(HALF_WIDTH), row(HALF_WIDTH), row(HALF_WIDTH), ff_specs[0], ff_specs[0], ff_specs[1], ff_specs[1],
                  row(HALF_WIDTH), row(HALF_WIDTH), row(D_MODEL), row(D_MODEL), full(w_out), full(bones)],
        out_specs=(pl.BlockSpec((1, D_MODEL), lambda i: (0, 0)), row(D_MODEL),
                   pl.BlockSpec((D_MODEL, D_MODEL), lambda i: (0, 0)),
                   row(HALF_WIDTH), row(HALF_WIDTH), row(HALF_WIDTH), row(HALF_WIDTH),
                   row(HALF_WIDTH), *fb_specs, row(HALF_WIDTH), *ff_specs, row(HALF_WIDTH), *ff_specs),
        out_shape=(jax.ShapeDtypeStruct((1, D_MODEL), F32), jax.ShapeDtypeStruct((SEQ, D_MODEL), F32),
                   jax.ShapeDtypeStruct((D_MODEL, D_MODEL), F32),
                   nat(BF16), nat(F32), nat(F32), nat(F32),
                   nat(BF16), *fb_shapes, nat(F32), *ff_shapes, nat(F32), *ff_shapes),
        scratch_shapes=[_fold_scratch(tm)],
        compiler_params=_params(("arbitrary",)),
    )(oa, ob1, lb1, ob4, lb4, ob16, lb16, gate_a, gate_b, x, target, w_out, bones)


def _dproj_assemble(dqa, dka, dva, dga, dgb, dq1, dk1, dv1, dq4, dk4, dv4, dq16, dk16, dv16, tqa, tqb, tkb, tka,
                    qkg, cos4, sin4, bmean):
    tm = ROW_TILE

    def norm_rope_bwd(d_out, t, g, cos, sin, bm, scale):
        d_r = d_out * scale
        dyv = d_r * cos + _swap_halves(d_r * sin)
        rr = lax.rsqrt(_head_sum(t * t, bm) + EPS)
        that = t * rr
        dgain = jnp.sum(dyv * that, axis=0, keepdims=True)
        gdy = dyv * g
        dt = rr * (gdy - that * _head_sum(that * gdy, bm))
        return dt, dgain

    def body(dqa_ref, dka_ref, dva_ref, dga_ref, dgb_ref, dq1_ref, dk1_ref, dv1_ref, dq4_ref, dk4_ref, dv4_ref,
             dq16_ref, dk16_ref, dv16_ref, tqa_ref, tqb_ref, tkb_ref, tka_ref, qkg_ref, cos_ref, sin_ref, bm_ref,
             dproj_ref, dqkg_ref, s_f):
        i = pl.program_id(0)
        cos, sin, bm = cos_ref[...], sin_ref[...], bm_ref[...]

        def merged(nat_ref, f4_ref, f16_ref):
            return nat_ref[...].astype(F32) + _unfold_load(f4_ref, s_f, 4, tm) + _unfold_load(f16_ref, s_f, 16, tm)

        @pl.when(i == 0)
        def _():
            dqkg_ref[...] = jnp.zeros_like(dqkg_ref)

        def through(d_out, t, row, scale, c0, paired=False):
            g = qkg_ref[row:row + 1, :]
            tot = jnp.zeros((1, PAIR), F32)
            dts = []
            for j in range(d_out.shape[1] // PAIR):
                cols = slice(j * PAIR, (j + 1) * PAIR)
                dt, dg = norm_rope_bwd(d_out[:, cols], t[:, cols], g, cos, sin, bm, scale)
                dts.append(dt)
                tot = tot + dg
            if paired:
                dts = _unpair_tiles(dts)
            for j, dt in enumerate(dts):
                dproj_ref[:, c0 + j * PAIR:c0 + (j + 1) * PAIR] = dt.astype(BF16)
            dqkg_ref[row:row + 1, :] += tot

        through(dqa_ref[...].astype(F32), tqa_ref[...], 0, HEAD_DIM ** -0.5, C_QA, paired=True)
        through(dka_ref[...].astype(F32), tka_ref[...], 1, 1.0, C_KA)
        through(merged(dq1_ref, dq4_ref, dq16_ref), tqb_ref[...], 2, HEAD_DIM ** -0.5, C_QB)
        through(merged(dk1_ref, dk4_ref, dk16_ref), tkb_ref[...], 3, 1.0, C_KB)
        dproj_ref[:, C_VB:C_VB + HALF_WIDTH] = merged(dv1_ref, dv4_ref, dv16_ref).astype(BF16)
        dproj_ref[:, C_GA:C_GA + HALF_WIDTH] = jnp.concatenate(_unpair_tiles(_tiles(dga_ref[...])), axis=1).astype(BF16)
        dproj_ref[:, C_GB:C_GB + HALF_WIDTH] = dgb_ref[...].astype(BF16)
        dproj_ref[:, C_VA:C_VA + KV_A_WIDTH] = dva_ref[...].astype(BF16)

    row = lambda width: pl.BlockSpec((tm, width), lambda i: (i, 0))
    full = lambda a: pl.BlockSpec(a.shape, lambda i: (0,) * a.ndim)
    _, ff_specs = _fold_specs(tm, F32)
    return pl.pallas_call(
        body, name="dproj_assemble", grid=(SEQ // tm,),
        in_specs=[row(HALF_WIDTH), row(KV_A_WIDTH), row(KV_A_WIDTH), row(HALF_WIDTH), row(HALF_WIDTH),
                  row(HALF_WIDTH), row(HALF_WIDTH), row(HALF_WIDTH), ff_specs[0], ff_specs[0], ff_specs[0],
                  ff_specs[1], ff_specs[1], ff_specs[1],
                  row(HALF_WIDTH), row(HALF_WIDTH), row(HALF_WIDTH), row(KV_A_WIDTH),
                  full(qkg), row(PAIR), row(PAIR), full(bmean)],
        out_specs=(row(IN_WIDTH), pl.BlockSpec((SMALL_ROWS, PAIR), lambda i: (0, 0))),
        out_shape=(jax.ShapeDtypeStruct((SEQ, IN_WIDTH), BF16), jax.ShapeDtypeStruct((SMALL_ROWS, PAIR), F32)),
        scratch_shapes=[_fold_scratch(tm)],
        compiler_params=_params(("arbitrary",)),
    )(dqa, dka, dva, dga, dgb, dq1, dk1, dv1, dq4, dk4, dv4, dq16, dk16, dv16, tqa, tqb, tkb, tka, qkg, cos4, sin4, bmean)


def _input_grad_reduce(dproj, w, x, gain, dy, blocks_in, blocks_out, small):
    tm = ROW_TILE
    n_steps = SEQ // tm
    stage2_step = 6
    shapes = (blocks_in.shape[1:], blocks_out.shape[1:])

    def body(dp_ref, w_ref, x_ref, g_ref, dy_ref, ga_hbm, gb_hbm, small_ref,
             gx_ref, out_a, out_b, small_out_ref, dgain_out_ref,
             part_a, part_b, sib_a, sib_b, wire_a, wire_b, chips_a, chips_b, small_all, dgain_acc, dgain_all,
             load_sems, sib_send, sib_recv, chip_send, chip_recv, small_send, small_recv, dgain_send, dgain_recv):
        i = pl.program_id(0)
        x, y, c = lax.axis_index("x"), lax.axis_index("y"), lax.axis_index("c")
        sibling = (x, y, 1 - c)
        chips = [(x, y), (1 - x, y), (x, 1 - y), (1 - x, 1 - y)]
        my_id = 4 * x + 2 * y + c
        g_hbm, part, from_sib = (ga_hbm, gb_hbm), (part_a, part_b), (sib_a, sib_b)
        to_wire, from_chips, out = (wire_a, wire_b), (chips_a, chips_b), (out_a, out_b)
        both = (0, 1)

        def blk(a, chip, core):
            return g_hbm[a].at[4 * chip[0] + 2 * chip[1] + core]

        def to_all(src, dst_all, send, recv):
            copies = []
            for rel in range(1, N_DEV):
                dx, dy_, dc = (rel >> 2) & 1, (rel >> 1) & 1, rel & 1
                to = (1 - x if dx else x, 1 - y if dy_ else y, 1 - c if dc else c)
                copies.append(pltpu.make_async_remote_copy(
                    src_ref=src, dst_ref=dst_all.at[my_id], send_sem=send.at[rel - 1], recv_sem=recv.at[rel - 1],
                    device_id=to, device_id_type=MESH))
            return copies

        small_copies = to_all(small_all.at[my_id], small_all, small_send, small_recv)
        dgain_copies = to_all(dgain_acc, dgain_all, dgain_send, dgain_recv)
        loads = [[pltpu.make_async_copy(blk(a, chips[k], c), part[a].at[k], load_sems.at[a, k]) for k in range(4)] for a in both]
        to_sib = [[pltpu.make_async_remote_copy(
            src_ref=blk(a, chips[k], 1 - c), dst_ref=from_sib[a].at[k], send_sem=sib_send.at[a, k], recv_sem=sib_recv.at[a, k],
            device_id=sibling, device_id_type=MESH) for k in range(4)] for a in both]
        to_chips = [[pltpu.make_async_remote_copy(
            src_ref=to_wire[a].at[k - 1], dst_ref=from_chips[a].at[k - 1],
            send_sem=chip_send.at[a, k - 1], recv_sem=chip_recv.at[a, k - 1],
            device_id=(*chips[k], c), device_id_type=MESH) for k in range(1, 4)] for a in both]

        @pl.when(i == 0)
        def _():
            small_all[my_id] = small_ref[...]
            for cp in small_copies:
                cp.start()
            for k in (1, 2, 3, 0):
                for a in both:
                    loads[a][k].start()
                    to_sib[a][k].start()

        @pl.when(i == stage2_step)
        def _():
            for k in (1, 2, 3):
                for a in both:
                    loads[a][k].wait()
                    to_sib[a][k].wait_recv()
                    to_wire[a][k - 1] = (part[a][k].astype(F32) + from_sib[a][k].astype(F32)).astype(BF16)
                    to_chips[a][k - 1].start()

        dh = lax.dot_general(dp_ref[...], w_ref[...], (((1,), (1,)), ((), ())), preferred_element_type=F32)
        xf = x_ref[...]
        r = lax.rsqrt(jnp.mean(xf * xf, axis=-1, keepdims=True) + EPS)
        xhat = xf * r
        dg = jnp.sum(dh * xhat, axis=0, keepdims=True)
        dxh = dh * g_ref[...]
        dx = r * (dxh - xhat * jnp.mean(dxh * xhat, axis=-1, keepdims=True))
        gx_ref[...] = dy_ref[...] + dx

        @pl.when(i == 0)
        def _():
            dgain_acc[...] = dg

        @pl.when(i > 0)
        def _():
            dgain_acc[...] += dg

        @pl.when(i == n_steps - 1)
        def _():
            dgain_all[my_id] = dgain_acc[...]
            for cp in dgain_copies:
                cp.start()
            for a in both:
                loads[a][0].wait()
                to_sib[a][0].wait_recv()
                acc = part[a][0].astype(F32) + from_sib[a][0].astype(F32)
                for k in range(3):
                    to_chips[a][k].wait_recv()
                    acc = acc + from_chips[a][k].astype(F32)
                out[a][...] = acc
            for copies, gathered, dst in ((small_copies, small_all, small_out_ref), (dgain_copies, dgain_all, dgain_out_ref)):
                for cp in copies:
                    cp.wait_recv()
                tot = gathered[0]
                for d in range(1, N_DEV):
                    tot = tot + gathered[d]
                dst[...] = tot
            for cp in to_sib[0] + to_sib[1] + to_chips[0] + to_chips[1] + small_copies + dgain_copies:
                cp.wait_send()

    row = lambda width: pl.BlockSpec((tm, width), lambda i: (i, 0))
    full = lambda a: pl.BlockSpec(a.shape, lambda i: (0,) * a.ndim)
    whole = lambda shape: pl.BlockSpec(shape, lambda i: (0,) * len(shape))
    hbm = pl.BlockSpec(memory_space=pl.ANY)
    dtypes = (blocks_in.dtype, blocks_out.dtype)
    buf = lambda n, dts: [pltpu.VMEM((n,) + s, dt) for s, dt in zip(shapes, dts)]
    return pl.pallas_call(
        body, name="input_grad_rs", grid=(n_steps,),
        in_specs=[row(IN_WIDTH), full(w), row(D_MODEL), full(gain), row(D_MODEL), hbm, hbm, full(small)],
        out_specs=(row(D_MODEL), whole(shapes[0]), whole(shapes[1]), whole((SMALL_ROWS, SMALL_COLS)), whole((1, D_MODEL))),
        out_shape=(jax.ShapeDtypeStruct((SEQ, D_MODEL), F32), jax.ShapeDtypeStruct(shapes[0], F32),
                   jax.ShapeDtypeStruct(shapes[1], F32), jax.ShapeDtypeStruct((SMALL_ROWS, SMALL_COLS), F32),
                   jax.ShapeDtypeStruct((1, D_MODEL), F32)),
        scratch_shapes=[*buf(4, dtypes), *buf(4, dtypes), *buf(3, (BF16, BF16)), *buf(3, (BF16, BF16)),
                        pltpu.VMEM((N_DEV, SMALL_ROWS, SMALL_COLS), F32),
                        pltpu.VMEM((1, D_MODEL), F32), pltpu.VMEM((N_DEV, 1, D_MODEL), F32),
                        pltpu.SemaphoreType.DMA((2, 4)), pltpu.SemaphoreType.DMA((2, 4)), pltpu.SemaphoreType.DMA((2, 4)),
                        pltpu.SemaphoreType.DMA((2, 3)), pltpu.SemaphoreType.DMA((2, 3)),
                        pltpu.SemaphoreType.DMA((7,)), pltpu.SemaphoreType.DMA((7,)),
                        pltpu.SemaphoreType.DMA((7,)), pltpu.SemaphoreType.DMA((7,))],
        compiler_params=_params(("arbitrary",)),
    )(dproj, w, x, gain, dy, blocks_in, blocks_out, small)


def _weight_grad(h_t, dproj):
    tk = 512
    cb = IN_WIDTH // 2
    n_k = SEQ // tk

    def body(ht_ref, dp_ref, out_ref, acc):
        k = pl.program_id(1)
        upd = jnp.dot(ht_ref[...], dp_ref[...], preferred_element_type=F32)

        @pl.when(k == 0)
        def _():
            acc[...] = upd

        @pl.when((k > 0) & (k < n_k - 1))
        def _():
            acc[...] += upd

        @pl.when(k == n_k - 1)
        def _():
            out_ref[...] = (acc[...] + upd).astype(BF16)

    return pl.pallas_call(
        body, name="weight_grad", grid=(2, n_k),
        in_specs=[pl.BlockSpec((D_MODEL, tk), lambda j, k: (0, k)), pl.BlockSpec((tk, cb), lambda j, k: (k, j))],
        out_specs=pl.BlockSpec((D_MODEL, cb), lambda j, k: (0, j)),
        out_shape=jax.ShapeDtypeStruct((D_MODEL, IN_WIDTH), BF16),
        scratch_shapes=[pltpu.VMEM((D_MODEL, cb), F32)],
        compiler_params=_params(("arbitrary", "arbitrary")),
    )(h_t, dproj)


def _adamw(name, w, g, m, v):
    def body(w_ref, g_ref, m_ref, v_ref, d_ref, nm_ref, nv_ref):
        gv = g_ref[...]
        nm = ADAM_B1 * m_ref[...] + (1.0 - ADAM_B1) * gv
        nv = ADAM_B2 * v_ref[...] + (1.0 - ADAM_B2) * jnp.square(gv)
        m_hat = nm / (1.0 - ADAM_B1 ** ADAM_STEP)
        v_hat = nv / (1.0 - ADAM_B2 ** ADAM_STEP)
        d_ref[...] = -ADAM_LR * (m_hat / (jnp.sqrt(v_hat) + ADAM_EPS) + ADAM_WD * w_ref[...])
        nm_ref[...] = nm
        nv_ref[...] = nv

    vmem = pl.BlockSpec(memory_space=pltpu.VMEM)
    out = jax.ShapeDtypeStruct(w.shape, F32)
    return pl.pallas_call(
        body, name=name, in_specs=[vmem] * 4, out_specs=(vmem,) * 3, out_shape=(out,) * 3,
        compiler_params=pltpu.CompilerParams(vmem_limit_bytes=VMEM_LIMIT),
    )(w, g, m, v)


SMALL_USED = D_MODEL + 4 * HEAD_DIM + 8


def _pack_small(norm_gain, qa, ka, sinks, qb, kb, extra=None):
    parts = [norm_gain.reshape(-1), qa.reshape(-1), ka.reshape(-1), sinks.reshape(-1), qb.reshape(-1), kb.reshape(-1)]
    if extra is not None:
        parts.append(extra.reshape(-1))
    flat = jnp.concatenate(parts)
    flat = jnp.pad(flat, (0, SMALL_ROWS * SMALL_COLS - flat.shape[0]))
    return flat.reshape(SMALL_ROWS, SMALL_COLS)


def _unpack_small(a):
    flat = a.reshape(-1)
    sizes = (D_MODEL, HEAD_DIM, HEAD_DIM, 8, HEAD_DIM, HEAD_DIM)
    out, off = [], 0
    for s in sizes:
        out.append(flat[off:off + s].reshape(1, s))
        off += s
    return out


def _fold_heads(row):
    return row[0, :HEAD_DIM] + row[0, HEAD_DIM:]


def kernel(x, norm_gain, w_in, q_norm_a, k_norm_a, sinks_a, q_norm_b, k_norm_b, w_out, loss_target, m_norm_gain, m_w_in, m_q_norm_a, m_k_norm_a, m_sinks_a, m_q_norm_b, m_k_norm_b, m_w_out, v_norm_gain, v_w_in, v_q_norm_a, v_k_norm_a, v_sinks_a, v_q_norm_b, v_k_norm_b, v_w_out):
    x2, tgt = x[0], loss_target[0]
    w_in_sh, w_out_sh = w_in[0], w_out[0]

    gathered_in, gathered_out = _all_gather_weights(w_in_sh, w_out_sh)
    w_full = gathered_in.transpose(1, 0, 2).reshape(D_MODEL, IN_WIDTH)
    wo_full = gathered_out.reshape(D_MODEL, D_MODEL)

    inv = np.float32(ROPE_THETA) ** (-np.arange(HEAD_DIM // 2, dtype=np.float32) / np.float32(HEAD_DIM // 2))
    ang = np.arange(SEQ, dtype=np.float32)[:, None] * inv[None, :].astype(np.float32)
    cos, sin = np.cos(ang).astype(np.float32), np.sin(ang).astype(np.float32)
    cos4 = jnp.asarray(np.concatenate([cos, cos, cos, cos], axis=1))
    sin4 = jnp.asarray(np.concatenate([-sin, sin, -sin, sin], axis=1))
    blockdiag = np.kron(np.eye(2, dtype=np.float32), np.ones((HEAD_DIM, HEAD_DIM), np.float32))
    bmean = jnp.asarray(blockdiag / HEAD_DIM, dtype=BF16)
    bones = jnp.asarray(blockdiag, dtype=BF16)
    two = lambda g: jnp.concatenate([g, g], axis=1)
    qkg = jnp.concatenate([two(q_norm_a), two(k_norm_a), two(q_norm_b), two(k_norm_b),
                           jnp.zeros((SMALL_ROWS - 4, PAIR), F32)], axis=0)
    sinks_paired = jnp.stack([sinks_a[0, :N_PAIRS], sinks_a[0, N_PAIRS:]], axis=1)
    sink_rows = jnp.concatenate([jnp.repeat(sinks_paired, BLOCK, axis=1),
                                 jnp.zeros((SMALL_ROWS - N_PAIRS, 2 * BLOCK), F32)], axis=0)

    (tqa, tka, tqb, tkb, gate_a, gate_b, h_t, qa, ka, va, qb, kb, vb, qb4, qb16, kb4, kb16, vb4, vb16) = _proj_fwd(
        x2, norm_gain, w_full, qkg, cos4, sin4, bmean)
    oa, la = _attn_fwd("attn_a_fwd", qa[None], ka[None], va[None], sink_rows, BLOCK - 1)
    ob1, lb1 = _attn_fwd("attn_b1_fwd", qb[None], kb[None], vb[None], None, BLOCK)
    ob4, lb4 = _attn_fwd("attn_b4_fwd", qb4, kb4, vb4, None, BLOCK)
    ob16, lb16 = _attn_fwd("attn_b16_fwd", qb16, kb16, vb16, None, BLOCK)
    (loss_cols, dy, gwo, doa, dla, dga, dgb, dob, dob4, dob16, dlb, dlb4, dlb16, lse_b, lse4, lse16) = _tail(
        oa[0], ob1[0], lb1[0], ob4, lb4, ob16, lb16, gate_a, gate_b, x2, tgt, wo_full, bones)

    dqa, dka, dva, dsink = _attn_bwd("attn_a_bwd", qa[None], ka[None], va[None], doa[None], la, dla[None], sink_rows, BLOCK - 1)
    dq1, dk1, dv1 = _attn_bwd("attn_b1_bwd", qb[None], kb[None], vb[None], dob[None], lse_b[None], dlb[None], None, BLOCK)
    dq4, dk4, dv4 = _attn_bwd("attn_b4_bwd", qb4, kb4, vb4, dob4, lse4, dlb4, None, BLOCK)
    dq16, dk16, dv16 = _attn_bwd("attn_b16_bwd", qb16, kb16, vb16, dob16, lse16, dlb16, None, BLOCK)
    dproj, dqkg = _dproj_assemble(dqa[0], dka[0], dva[0], dga, dgb, dq1[0], dk1[0], dv1[0], dq4, dk4, dv4,
                                  dq16, dk16, dv16, tqa, tqb, tkb, tka, qkg, cos4, sin4, bmean)
    gw_in = _weight_grad(h_t, dproj)

    blocks_in = gw_in.reshape(D_MODEL, N_DEV, SHARD_IN).transpose(1, 0, 2)
    blocks_out = gwo.reshape(N_DEV, SHARD_OUT, D_MODEL)
    g_sinks = jnp.concatenate([jnp.sum(dsink[:N_PAIRS, :BLOCK], axis=1), jnp.sum(dsink[:N_PAIRS, BLOCK:], axis=1)])
    small = _pack_small(jnp.zeros((D_MODEL,), F32), _fold_heads(dqkg[0:1]), _fold_heads(dqkg[1:2]), g_sinks,
                        _fold_heads(dqkg[2:3]), _fold_heads(dqkg[3:4]), extra=0.5 * jnp.sum(loss_cols) / D_MODEL)
    grad_x, g_w_in, g_w_out, small_red, dgain_red = _input_grad_reduce(
        dproj, w_full, x2, norm_gain, dy, blocks_in, blocks_out, small)
    n_gain_rows = D_MODEL // SMALL_COLS
    small_red = jnp.concatenate([dgain_red.reshape(n_gain_rows, SMALL_COLS), small_red[n_gain_rows:]], axis=0)
    g_small = _unpack_small(small_red)

    d_in, nm_in, nv_in = _adamw("adamw_w_in", w_in_sh, g_w_in, m_w_in[0], v_w_in[0])
    d_out, nm_out, nv_out = _adamw("adamw_w_out", w_out_sh, g_w_out, m_w_out[0], v_w_out[0])
    d_s, nm_s, nv_s = _adamw(
        "adamw_small",
        _pack_small(norm_gain, q_norm_a, k_norm_a, sinks_a, q_norm_b, k_norm_b), small_red,
        _pack_small(m_norm_gain, m_q_norm_a, m_k_norm_a, m_sinks_a, m_q_norm_b, m_k_norm_b),
        _pack_small(v_norm_gain, v_q_norm_a, v_k_norm_a, v_sinks_a, v_q_norm_b, v_k_norm_b))
    d_small, nm_small, nv_small = _unpack_small(d_s), _unpack_small(nm_s), _unpack_small(nv_s)

    loss = small_red.reshape(-1)[SMALL_USED]

    def assemble(small_list, big_in, big_out):
        ng, qa_, ka_, sk_, qb_, kb_ = small_list
        return [ng, big_in[None], qa_, ka_, sk_, qb_, kb_, big_out[None]]

    return (loss, grad_x[None], *assemble(g_small, g_w_in, g_w_out), *assemble(d_small, d_in, d_out),
            *assemble(nm_small, nm_in, nm_out), *assemble(nv_small, nv_in, nv_out))
```

```python
import functools

import numpy as np
import jax
import jax.numpy as jnp
from jax import lax
from jax.experimental import pallas as pl
from jax.experimental.pallas import tpu as pltpu

F32 = jnp.float32
BF16 = jnp.bfloat16

SEQ = 4096
D_MODEL = 1024
HEAD_DIM = 64
PAIR = 2 * HEAD_DIM
N_PAIRS = 4
HALF_WIDTH = N_PAIRS * PAIR
KV_A_WIDTH = 128
IN_WIDTH = 3328
BLOCK = 128
EPS = 1e-6
NEG = -1e30
ROPE_THETA = 10000.0
N_DEV = 8
SHARD_IN = IN_WIDTH // N_DEV
SHARD_OUT = D_MODEL // N_DEV
PAYLOAD = SHARD_IN + SHARD_OUT
SMALL_ROWS, SMALL_COLS = 8, 256

C_QA, C_KA, C_VA, C_GA, C_QB, C_KB, C_VB, C_GB = 0, 512, 640, 768, 1280, 1792, 2304, 2816

ADAM_LR = 0.001
ADAM_B1 = 0.9
ADAM_B2 = 0.999
ADAM_EPS = 1e-08
ADAM_WD = 0.01
ADAM_STEP = 10

ROW_TILE = 256
FWD_BLOCKS_PER_STEP = 4
BWD_BLOCKS_PER_STEP = 4
VMEM_LIMIT = 56 * 1024 * 1024

MESH = pl.DeviceIdType.MESH


def _params(sem, vmem=VMEM_LIMIT):
    return pltpu.CompilerParams(dimension_semantics=sem, vmem_limit_bytes=vmem)


def _head_sum(v, bm):
    hi = v.astype(BF16)
    lo = (v - hi.astype(F32)).astype(BF16)
    return (jnp.dot(hi, bm, preferred_element_type=F32) + jnp.dot(lo, bm, preferred_element_type=F32))


def _swap_halves(y):
    lane = lax.broadcasted_iota(jnp.int32, y.shape, 1)
    first = (lane & 32) == 0
    return jnp.where(first, pltpu.roll(y, 96, 1), pltpu.roll(y, 32, 1))


def _sigmoid(g):
    return 1.0 / (1.0 + jnp.exp(-g))


def _tiles(a):
    return [a[:, j * PAIR:(j + 1) * PAIR] for j in range(N_PAIRS)]


def _pair_tiles(t):
    low = lax.broadcasted_iota(jnp.int32, t[0].shape, 1) < HEAD_DIM
    r = [pltpu.roll(a, HEAD_DIM, 1) for a in t]
    return [jnp.where(low, t[0], r[2]), jnp.where(low, r[0], t[2]), jnp.where(low, t[1], r[3]), jnp.where(low, r[1], t[3])]


def _unpair_tiles(p):
    low = lax.broadcasted_iota(jnp.int32, p[0].shape, 1) < HEAD_DIM
    r = [pltpu.roll(a, HEAD_DIM, 1) for a in p]
    return [jnp.where(low, p[0], r[1]), jnp.where(low, p[2], r[3]), jnp.where(low, r[0], p[1]), jnp.where(low, r[2], p[3])]


def _gather_plan(mine_ref, out_ref, send_sems, recv_sems):
    x, y, c = lax.axis_index("x"), lax.axis_index("y"), lax.axis_index("c")
    me, sibling = (x, y, c), (x, y, 1 - c)
    chips = [(1 - x, y), (x, 1 - y), (1 - x, 1 - y)]

    def slot(px, py, pc):
        return out_ref.at[4 * px + 2 * py + pc]

    def copy(k, block, to, from_mine=False):
        return pltpu.make_async_remote_copy(
            src_ref=mine_ref if from_mine else slot(*block), dst_ref=slot(*block),
            send_sem=send_sems.at[k], recv_sem=recv_sems.at[k], device_id=to, device_id_type=MESH)

    first = [copy(0, me, sibling, True)] + [copy(1 + j, me, (*chip, c), True) for j, chip in enumerate(chips)]
    landed = [copy(1 + j, (*chip, c), me) for j, chip in enumerate(chips)]
    passed = [copy(4 + j, (*chip, c), sibling) for j, chip in enumerate(chips)]
    from_sibling = [copy(0, sibling, me)] + [copy(4 + j, (*chip, 1 - c), me) for j, chip in enumerate(chips)]
    return slot(*me), first, landed, passed, from_sibling


GATHER_SCRATCH = [pltpu.SemaphoreType.DMA((7,)), pltpu.SemaphoreType.DMA((7,))]


def _all_gather_w_in(w_in_sh):
    def body(w_ref, out_ref, mine_ref, send_sems, recv_sems):
        mine_ref[...] = w_ref[...].astype(BF16)
        my_slot, first, landed, passed, from_sibling = _gather_plan(mine_ref, out_ref, send_sems, recv_sems)
        for cp in first:
            cp.start()
        my_slot[...] = mine_ref[...]
        for arrival, forward in zip(landed, passed):
            arrival.wait_recv()
            forward.start()
        for arrival in from_sibling:
            arrival.wait_recv()
        for cp in first + passed:
            cp.wait_send()

    vmem = pl.BlockSpec(memory_space=pltpu.VMEM)
    return pl.pallas_call(
        body, name="ag_w_in",
        out_shape=jax.ShapeDtypeStruct((N_DEV,) + w_in_sh.shape, BF16),
        in_specs=[vmem], out_specs=vmem,
        scratch_shapes=[pltpu.VMEM(w_in_sh.shape, BF16)] + GATHER_SCRATCH,
        compiler_params=pltpu.CompilerParams(vmem_limit_bytes=VMEM_LIMIT),
    )(w_in_sh)


def _reduce_scatter_grads(blocks_in, blocks_out, small):
    shapes = (blocks_in.shape[1:], blocks_out.shape[1:])

    def body(ga_hbm, gb_hbm, small_ref, out_a, out_b, small_out_ref,
             part_a, part_b, sib_a, sib_b, wire_a, wire_b, chips_a, chips_b, small_all,
             load_sems, sib_send, sib_recv, chip_send, chip_recv, small_send, small_recv):
        x, y, c = lax.axis_index("x"), lax.axis_index("y"), lax.axis_index("c")
        sibling = (x, y, 1 - c)
        chips = [(x, y), (1 - x, y), (x, 1 - y), (1 - x, 1 - y)]
        my_id = 4 * x + 2 * y + c
        g_hbm, part, from_sib = (ga_hbm, gb_hbm), (part_a, part_b), (sib_a, sib_b)
        to_wire, from_chips, out = (wire_a, wire_b), (chips_a, chips_b), (out_a, out_b)
        both = (0, 1)

        def blk(a, chip, core):
            return g_hbm[a].at[4 * chip[0] + 2 * chip[1] + core]

        small_all[my_id] = small_ref[...]
        small_copies = []
        for rel in range(1, N_DEV):
            dx, dy, dc = (rel >> 2) & 1, (rel >> 1) & 1, rel & 1
            to = (1 - x if dx else x, 1 - y if dy else y, 1 - c if dc else c)
            small_copies.append(pltpu.make_async_remote_copy(
                src_ref=small_ref, dst_ref=small_all.at[my_id],
                send_sem=small_send.at[rel - 1], recv_sem=small_recv.at[rel - 1], device_id=to, device_id_type=MESH))
        for cp in small_copies:
            cp.start()

        loads = [[pltpu.make_async_copy(blk(a, chips[k], c), part[a].at[k], load_sems.at[a, k]) for k in range(4)] for a in both]
        to_sib = [[pltpu.make_async_remote_copy(
            src_ref=blk(a, chips[k], 1 - c), dst_ref=from_sib[a].at[k], send_sem=sib_send.at[a, k], recv_sem=sib_recv.at[a, k],
            device_id=sibling, device_id_type=MESH) for k in range(4)] for a in both]
        for k in (1, 2, 3, 0):
            for a in both:
                loads[a][k].start()
                to_sib[a][k].start()

        to_chips = [[pltpu.make_async_remote_copy(
            src_ref=to_wire[a].at[k - 1], dst_ref=from_chips[a].at[k - 1],
            send_sem=chip_send.at[a, k - 1], recv_sem=chip_recv.at[a, k - 1],
            device_id=(*chips[k], c), device_id_type=MESH) for k in range(1, 4)] for a in both]
        for k in (1, 2, 3):
            for a in both:
                loads[a][k].wait()
                to_sib[a][k].wait_recv()
                to_wire[a][k - 1] = (part[a][k] + from_sib[a][k]).astype(BF16)
                to_chips[a][k - 1].start()
        for a in both:
            loads[a][0].wait()
            to_sib[a][0].wait_recv()
            acc = part[a][0] + from_sib[a][0]
            for k in range(3):
                to_chips[a][k].wait_recv()
                acc = acc + from_chips[a][k].astype(F32)
            out[a][...] = acc

        for cp in small_copies:
            cp.wait_recv()
        tot = small_all[0]
        for d in range(1, N_DEV):
            tot = tot + small_all[d]
        small_out_ref[...] = tot
        for cp in to_sib[0] + to_sib[1] + to_chips[0] + to_chips[1] + small_copies:
            cp.wait_send()

    vmem = pl.BlockSpec(memory_space=pltpu.VMEM)
    hbm = pl.BlockSpec(memory_space=pl.ANY)
    buf = lambda n, dtype: [pltpu.VMEM((n,) + s, dtype) for s in shapes]
    return pl.pallas_call(
        body, name="rs_grads",
        out_shape=(jax.ShapeDtypeStruct(shapes[0], F32), jax.ShapeDtypeStruct(shapes[1], F32),
                   jax.ShapeDtypeStruct((SMALL_ROWS, SMALL_COLS), F32)),
        in_specs=[hbm, hbm, vmem], out_specs=(vmem, vmem, vmem),
        scratch_shapes=[*buf(4, F32), *buf(4, F32), *buf(3, BF16), *buf(3, BF16),
                        pltpu.VMEM((N_DEV, SMALL_ROWS, SMALL_COLS), F32),
                        pltpu.SemaphoreType.DMA((2, 4)), pltpu.SemaphoreType.DMA((2, 4)), pltpu.SemaphoreType.DMA((2, 4)),
                        pltpu.SemaphoreType.DMA((2, 3)), pltpu.SemaphoreType.DMA((2, 3)),
                        pltpu.SemaphoreType.DMA((7,)), pltpu.SemaphoreType.DMA((7,))],
        compiler_params=pltpu.CompilerParams(vmem_limit_bytes=VMEM_LIMIT),
    )(blocks_in, blocks_out, small)


def _fold_scratch(tm):
    return pltpu.VMEM((N_PAIRS, tm, PAIR), F32)


def _fold_store(val, scr, out4, out16, tm):
    for j in range(N_PAIRS):
        scr[j] = val[:, j * PAIR:(j + 1) * PAIR]
    for dil, out in ((4, out4), (16, out16)):
        for r in range(dil):
            for j in range(N_PAIRS):
                out[r, :, j * PAIR:(j + 1) * PAIR] = scr[j, pl.ds(r, tm // dil, stride=dil), :].astype(out.dtype)


def _unfold_load(src, scr, dil, tm):
    for r in range(dil):
        for j in range(N_PAIRS):
            scr[j, pl.ds(r, tm // dil, stride=dil), :] = src[r, :, j * PAIR:(j + 1) * PAIR].astype(F32)
    return jnp.concatenate([scr[j] for j in range(N_PAIRS)], axis=1)


def _fold_specs(tm, dtype):
    shapes = (jax.ShapeDtypeStruct((4, SEQ // 4, HALF_WIDTH), dtype), jax.ShapeDtypeStruct((16, SEQ // 16, HALF_WIDTH), dtype))
    specs = (pl.BlockSpec((4, tm // 4, HALF_WIDTH), lambda i: (0, i, 0)),
             pl.BlockSpec((16, tm // 16, HALF_WIDTH), lambda i: (0, i, 0)))
    return shapes, specs


def _proj_fwd(x, gain, w, qkg, cos4, sin4, bmean, w_out_sh):
    tm = ROW_TILE
    n_steps = SEQ // tm

    def norm_rope(t, g, cos, sin, bm, scale):
        rr = lax.rsqrt(_head_sum(t * t, bm) + EPS)
        yv = t * rr * g
        return (yv * cos + _swap_halves(yv) * sin) * scale

    def body(x_ref, g_ref, w_ref, qkg_ref, cos_ref, sin_ref, bm_ref, wo_ref,
             tqa_ref, tka_ref, tqb_ref, tkb_ref, ga_ref, gb_ref, ht_ref, qa_ref, ka_ref, va_ref, qb_ref, kb_ref, vb_ref,
             qb4_ref, qb16_ref, kb4_ref, kb16_ref, vb4_ref, vb16_ref, wo_all_ref,
             proj, scr, wo_mine, wo_all, send_sems, recv_sems):
        i = pl.program_id(0)
        my_slot, first, landed, passed, from_sibling = _gather_plan(wo_mine, wo_all, send_sems, recv_sems)

        @pl.when(i == 0)
        def _():
            wo_mine[...] = wo_ref[...].astype(BF16)
            for cp in first:
                cp.start()
            my_slot[...] = wo_mine[...]

        @pl.when(i == n_steps // 2)
        def _():
            for arrival, forward in zip(landed, passed):
                arrival.wait_recv()
                forward.start()

        xf = x_ref[...]
        r = lax.rsqrt(jnp.mean(xf * xf, axis=-1, keepdims=True) + EPS)
        hf = xf * r * g_ref[...]
        ht_ref[...] = hf.T.astype(BF16)
        cos, sin, bm = cos_ref[...], sin_ref[...], bm_ref[...]
        proj[...] = jnp.dot(hf.astype(BF16), w_ref[...], preferred_element_type=F32)

        def roped(tiles, row, scale):
            g = qkg_ref[row:row + 1, :]
            return jnp.concatenate([norm_rope(t, g, cos, sin, bm, scale) for t in tiles], axis=1)

        tqa = _pair_tiles(_tiles(proj[:, C_QA:C_QA + HALF_WIDTH]))
        tqa_ref[...] = jnp.concatenate(tqa, axis=1)
        qa_ref[...] = roped(tqa, 0, HEAD_DIM ** -0.5).astype(BF16)
        ga_ref[...] = jnp.concatenate(_pair_tiles(_tiles(proj[:, C_GA:C_GA + HALF_WIDTH])), axis=1)
        gb_ref[...] = proj[:, C_GB:C_GB + HALF_WIDTH]
        tqb = proj[:, C_QB:C_QB + HALF_WIDTH]
        tqb_ref[...] = tqb
        qb = roped(_tiles(tqb), 2, HEAD_DIM ** -0.5)
        qb_ref[...] = qb.astype(BF16)
        _fold_store(qb, scr, qb4_ref, qb16_ref, tm)
        tkb = proj[:, C_KB:C_KB + HALF_WIDTH]
        tkb_ref[...] = tkb
        kb = roped(_tiles(tkb), 3, 1.0)
        kb_ref[...] = kb.astype(BF16)
        _fold_store(kb, scr, kb4_ref, kb16_ref, tm)
        vb = proj[:, C_VB:C_VB + HALF_WIDTH]
        vb_ref[...] = vb.astype(BF16)
        _fold_store(vb, scr, vb4_ref, vb16_ref, tm)
        tka = proj[:, C_KA:C_KA + KV_A_WIDTH]
        tka_ref[...] = tka
        ka_ref[...] = roped([tka], 1, 1.0).astype(BF16)
        va_ref[...] = proj[:, C_VA:C_VA + KV_A_WIDTH].astype(BF16)

        @pl.when(i == n_steps - 1)
        def _():
            for arrival in from_sibling:
                arrival.wait_recv()
            for cp in first + passed:
                cp.wait_send()
            wo_all_ref[...] = wo_all[...]

    row = lambda width: pl.BlockSpec((tm, width), lambda i: (i, 0))
    full = lambda a: pl.BlockSpec(a.shape, lambda i: (0,) * a.ndim)
    nat = lambda width, dtype=BF16: jax.ShapeDtypeStruct((SEQ, width), dtype)
    f_shapes, f_specs = _fold_specs(tm, BF16)
    return pl.pallas_call(
        body, name="proj_fwd", grid=(SEQ // tm,),
        in_specs=[row(D_MODEL), full(gain), full(w), full(qkg), row(PAIR), row(PAIR), full(bmean), full(w_out_sh)],
        out_specs=(row(HALF_WIDTH), row(KV_A_WIDTH), row(HALF_WIDTH), row(HALF_WIDTH), row(HALF_WIDTH), row(HALF_WIDTH),
                   pl.BlockSpec((D_MODEL, tm), lambda i: (0, i)),
                   row(HALF_WIDTH), row(KV_A_WIDTH), row(KV_A_WIDTH), row(HALF_WIDTH), row(HALF_WIDTH), row(HALF_WIDTH),
                   *f_specs, *f_specs, *f_specs,
                   pl.BlockSpec((N_DEV,) + w_out_sh.shape, lambda i: (0, 0, 0))),
        out_shape=(nat(HALF_WIDTH, F32), nat(KV_A_WIDTH, F32), nat(HALF_WIDTH, F32), nat(HALF_WIDTH, F32),
                   nat(HALF_WIDTH, F32), nat(HALF_WIDTH, F32),
                   jax.ShapeDtypeStruct((D_MODEL, SEQ), BF16),
                   nat(HALF_WIDTH), nat(KV_A_WIDTH), nat(KV_A_WIDTH), nat(HALF_WIDTH), nat(HALF_WIDTH), nat(HALF_WIDTH),
                   *f_shapes, *f_shapes, *f_shapes,
                   jax.ShapeDtypeStruct((N_DEV,) + w_out_sh.shape, BF16)),
        scratch_shapes=[pltpu.VMEM((tm, IN_WIDTH), F32), _fold_scratch(tm), pltpu.VMEM(w_out_sh.shape, BF16),
                        pltpu.VMEM((N_DEV,) + w_out_sh.shape, BF16)] + GATHER_SCRATCH,
        compiler_params=_params(("arbitrary",)),
    )(x, gain, w, qkg, cos4, sin4, bmean, w_out_sh)


def _band_mask(i, max_dist):
    j = lax.broadcasted_iota(jnp.int32, (2 * BLOCK, 2 * BLOCK), 0)
    c = lax.broadcasted_iota(jnp.int32, (2 * BLOCK, 2 * BLOCK), 1)
    dist = (c & (BLOCK - 1)) + BLOCK - j
    return (dist >= 0) & (dist <= max_dist) & ((j >= BLOCK) | (i > 0))


def _stack_heads(t):
    lane = lax.broadcasted_iota(jnp.int32, t.shape, 1)
    low = lane < HEAD_DIM
    zero = jnp.zeros_like(t)
    return jnp.concatenate([jnp.where(low, t, zero), jnp.where(low, zero, t)], axis=0)


def _unstack_t(t):
    return jnp.concatenate([t[:HEAD_DIM, :BLOCK], t[HEAD_DIM:, BLOCK:]], axis=0).T


def _rows_to_pair(row):
    return jnp.concatenate([jnp.broadcast_to(row[:, :BLOCK], (HEAD_DIM, BLOCK)),
                            jnp.broadcast_to(row[:, BLOCK:], (HEAD_DIM, BLOCK))], axis=0).T


def _pair_to_rows(t):
    tt = t.T
    return jnp.concatenate([tt[0:1, :], tt[HEAD_DIM:HEAD_DIM + 1, :]], axis=1)


def _attn_fwd(name, q, k, v, sink_rows, max_dist):
    n_seq, length, _ = q.shape
    ck = k.shape[2]
    nb = length // BLOCK
    shared = ck == PAIR
    has_sinks = sink_rows is not None

    qb = FWD_BLOCKS_PER_STEP

    def body(*refs):
        if has_sinks:
            q_ref, kc_ref, vc_ref, sink_ref, o_ref, lse_ref, kp_ref, vp_ref = refs
        else:
            q_ref, kc_ref, vc_ref, o_ref, lse_ref, kp_ref, vp_ref = refs
        step = pl.program_id(0)

        @pl.when(step == 0)
        def _():
            kp_ref[...] = jnp.zeros_like(kp_ref)
            vp_ref[...] = jnp.zeros_like(vp_ref)

        valid = [_band_mask((step * qb + b) & (nb - 1), max_dist) for b in range(qb)]
        cols = [slice(p * PAIR, (p + 1) * PAIR) for p in range(N_PAIRS)]
        kcols = [slice(0, PAIR) if shared else c for c in cols]
        rows = [slice(b * BLOCK, (b + 1) * BLOCK) for b in range(qb)]
        units = [(b, p) for b in range(qb) for p in range(N_PAIRS)]
        n = range(len(units))

        def window(prev_ref, cur_ref, b, kc):
            before = prev_ref[:, kc] if b == 0 else cur_ref[rows[b - 1], kc]
            return jnp.concatenate([before, cur_ref[rows[b], kc]], axis=0)

        st = [lax.dot_general(window(kp_ref, kc_ref, b, kcols[p]), _stack_heads(q_ref[rows[b], cols[p]]),
                              (((1,), (1,)), ((), ())), preferred_element_type=F32) for b, p in units]
        st = [jnp.where(valid[units[u][0]], st[u], NEG) for u in n]
        m = [jnp.max(s, axis=0, keepdims=True) for s in st]
        if has_sinks:
            sk = [sink_ref[p:p + 1, :] for _, p in units]
            m = [jnp.maximum(m[u], sk[u]) for u in n]
        pt = [jnp.exp(st[u] - m[u]) for u in n]
        l = [jnp.sum(t, axis=0, keepdims=True) for t in pt]
        if has_sinks:
            l = [l[u] + jnp.exp(sk[u] - m[u]) for u in n]
        v2t = [window(vp_ref, vc_ref, b, kcols[p]).astype(F32).T.astype(BF16) for b, p in units]
        ot = [jnp.dot(v2t[u], pt[u].astype(BF16), preferred_element_type=F32) / l[u] for u in n]
        for u, (b, p) in enumerate(units):
            o_ref[rows[b], cols[p]] = _unstack_t(ot[u]).astype(BF16)
            lse_ref[rows[b], cols[p]] = _rows_to_pair(m[u] + jnp.log(l[u]))
        kp_ref[...] = kc_ref[rows[-1], :]
        vp_ref[...] = vc_ref[rows[-1], :]

    cur = lambda width: pl.BlockSpec((qb * BLOCK, width), lambda s: (s, 0))
    flat = lambda a: a.reshape(n_seq * length, a.shape[2])
    in_specs = [cur(HALF_WIDTH), cur(ck), cur(ck)]
    args = [flat(q), flat(k), flat(v)]
    if has_sinks:
        in_specs.append(pl.BlockSpec(sink_rows.shape, lambda s: (0, 0)))
        args.append(sink_rows)
    out = lambda dtype: jax.ShapeDtypeStruct((n_seq * length, HALF_WIDTH), dtype)
    o, lse = pl.pallas_call(
        body, name=name, grid=(n_seq * nb // qb,), in_specs=in_specs,
        out_specs=(cur(HALF_WIDTH), cur(HALF_WIDTH)), out_shape=(out(BF16), out(F32)),
        scratch_shapes=[pltpu.VMEM((BLOCK, ck), BF16), pltpu.VMEM((BLOCK, ck), BF16)],
        compiler_params=_params(("arbitrary",)),
    )(*args)
    return o.reshape(n_seq, length, HALF_WIDTH), lse.reshape(n_seq, length, HALF_WIDTH)


def _attn_bwd(name, q, k, v, d_o, lse, delta, sink_rows, max_dist):
    n_seq, length, _ = q.shape
    ck = k.shape[2]
    nb = length // BLOCK
    n_blocks = n_seq * nb
    n_rows = n_seq * length
    shared = ck == PAIR
    has_sinks = sink_rows is not None
    qb = BWD_BLOCKS_PER_STEP
    n_steps = n_blocks // qb

    def body(*refs):
        if has_sinks:
            (q_ref, kc_ref, vc_ref, do_ref, lse_ref, dl_ref, sink_ref,
             dq_ref, dk_ref, dv_ref, dsink_ref, ck_scr, cv_scr, kp_ref, vp_ref) = refs
        else:
            (q_ref, kc_ref, vc_ref, do_ref, lse_ref, dl_ref,
             dq_ref, dk_ref, dv_ref, ck_scr, cv_scr, kp_ref, vp_ref) = refs
        step = pl.program_id(0)

        @pl.when(step == 0)
        def _():
            ck_scr[...] = jnp.zeros_like(ck_scr)
            cv_scr[...] = jnp.zeros_like(cv_scr)
            kp_ref[...] = jnp.zeros_like(kp_ref)
            vp_ref[...] = jnp.zeros_like(vp_ref)
            if has_sinks:
                dsink_ref[...] = jnp.zeros_like(dsink_ref)

        valid = [_band_mask((step * qb + b) & (nb - 1), max_dist) for b in range(qb)]
        cols = [slice(p * PAIR, (p + 1) * PAIR) for p in range(N_PAIRS)]
        kcols = [slice(0, PAIR) if shared else c for c in cols]
        rows = [slice(b * BLOCK, (b + 1) * BLOCK) for b in range(qb)]
        units = [(b, p) for b in range(qb) for p in range(N_PAIRS)]
        n = range(len(units))
        nt = (((1,), (1,)), ((), ()))

        def window(prev_ref, cur_ref, b, kc):
            before = prev_ref[:, kc] if b == 0 else cur_ref[rows[b - 1], kc]
            return jnp.concatenate([before, cur_ref[rows[b], kc]], axis=0)

        q_st = [_stack_heads(q_ref[rows[b], cols[p]]) for b, p in units]
        do_st = [_stack_heads(do_ref[rows[b], cols[p]]) for b, p in units]
        k2 = [window(kp_ref, kc_ref, b, kcols[p]) for b, p in units]
        v2 = [window(vp_ref, vc_ref, b, kcols[p]) for b, p in units]
        st = [lax.dot_general(k2[u], q_st[u], nt, preferred_element_type=F32) for u in n]
        dpt = [lax.dot_general(v2[u], do_st[u], nt, preferred_element_type=F32) for u in n]
        lse_row = [_pair_to_rows(lse_ref[rows[b], cols[p]]) for b, p in units]
        dl_row = [_pair_to_rows(dl_ref[rows[b], cols[p]]) for b, p in units]
        pt = [jnp.exp(jnp.where(valid[units[u][0]], st[u], NEG) - lse_row[u]) for u in n]
        dst = [(pt[u] * (dpt[u] - dl_row[u])).astype(BF16) for u in n]
        ptb = [t.astype(BF16) for t in pt]
        dv2 = [jnp.dot(ptb[u], do_st[u], preferred_element_type=F32) for u in n]
        dk2 = [jnp.dot(dst[u], q_st[u], preferred_element_type=F32) for u in n]
        k2t = [k2[u].astype(F32).T.astype(BF16) for u in n]
        dqt = [jnp.dot(k2t[u], dst[u], preferred_element_type=F32) for u in n]
        for u, (b, p) in enumerate(units):
            dq_ref[rows[b], cols[p]] = _unstack_t(dqt[u]).astype(BF16)
        if has_sinks:
            for u, (b, p) in enumerate(units):
                p_sink = jnp.exp(sink_ref[p:p + 1, :] - lse_row[u])
                dsink_ref[p:p + 1, :] = dsink_ref[p:p + 1, :] - p_sink * dl_row[u]

        def total(parts, w, group):
            sel = [u for u, (b, p) in enumerate(units) if (shared or p == group)]
            terms = ([parts[u][:BLOCK] for u in sel if units[u][0] == w]
                     + [parts[u][BLOCK:] for u in sel if units[u][0] == w - 1])
            tot = terms[0]
            for t in terms[1:]:
                tot = tot + t
            return tot

        first_row = step * (qb * BLOCK)
        for acc_ref, out_ref, parts in ((ck_scr, dk_ref, dk2), (cv_scr, dv_ref, dv2)):
            for group in range(1 if shared else N_PAIRS):
                kc = kcols[group]

                @pl.when(step > 0)
                def _():
                    out_ref[pl.ds(pl.multiple_of(first_row - BLOCK, BLOCK), BLOCK), kc] = (
                        acc_ref[:, kc] + total(parts, 0, group)).astype(BF16)

                for w in range(1, qb):
                    out_ref[pl.ds(pl.multiple_of(first_row + (w - 1) * BLOCK, BLOCK), BLOCK), kc] = (
                        total(parts, w, group).astype(BF16))
                acc_ref[:, kc] = total(parts, qb, group)

        @pl.when(step == n_steps - 1)
        def _():
            dk_ref[pl.ds(n_rows - BLOCK, BLOCK), :] = ck_scr[...].astype(BF16)
            dv_ref[pl.ds(n_rows - BLOCK, BLOCK), :] = cv_scr[...].astype(BF16)

        kp_ref[...] = kc_ref[rows[-1], :]
        vp_ref[...] = vc_ref[rows[-1], :]

    cur = lambda width: pl.BlockSpec((qb * BLOCK, width), lambda s: (s, 0))
    whole = lambda width: pl.BlockSpec((n_rows, width), lambda s: (0, 0))
    flat = lambda a: a.reshape(n_rows, a.shape[2])
    in_specs = [cur(HALF_WIDTH), cur(ck), cur(ck), cur(HALF_WIDTH), cur(HALF_WIDTH), cur(HALF_WIDTH)]
    args = [flat(a) for a in (q, k, v, d_o, lse, delta)]
    out_specs = [cur(HALF_WIDTH), whole(ck), whole(ck)]
    out_shape = [jax.ShapeDtypeStruct((n_rows, HALF_WIDTH), BF16),
                 jax.ShapeDtypeStruct((n_rows, ck), BF16), jax.ShapeDtypeStruct((n_rows, ck), BF16)]
    if has_sinks:
        in_specs.append(pl.BlockSpec(sink_rows.shape, lambda s: (0, 0)))
        args.append(sink_rows)
        out_specs.append(pl.BlockSpec(sink_rows.shape, lambda s: (0, 0)))
        out_shape.append(jax.ShapeDtypeStruct(sink_rows.shape, F32))
    outs = pl.pallas_call(
        body, name=name, grid=(n_steps,), in_specs=in_specs,
        out_specs=tuple(out_specs), out_shape=tuple(out_shape),
        scratch_shapes=[pltpu.VMEM((BLOCK, ck), F32), pltpu.VMEM((BLOCK, ck), F32),
                        pltpu.VMEM((BLOCK, ck), BF16), pltpu.VMEM((BLOCK, ck), BF16)],
        compiler_params=_params(("arbitrary",)),
    )(*args)
    return tuple(o.reshape(n_seq, length, o.shape[1]) for o in outs[:3]) + tuple(outs[3:])


def _tail(oa, ob1, lb1, ob4, lb4, ob16, lb16, gate_a, gate_b, x, target, w_out, bones):
    tm = ROW_TILE

    def body(oa_ref, ob1_ref, lb1_ref, ob4_ref, lb4_ref, ob16_ref, lb16_ref, ga_ref, gb_ref, x_ref, t_ref, w_ref, bo_ref,
             loss_ref, dy_ref, gwo_ref, doa_ref, dla_ref, dga_ref, dgb_ref,
             dob_ref, dob4_ref, dob16_ref, dlb_ref, dlb4_ref, dlb16_ref, lse_ref, lse4_ref, lse16_ref,
             s_f):
        i = pl.program_id(0)
        o4, o16 = _unfold_load(ob4_ref, s_f, 4, tm), _unfold_load(ob16_ref, s_f, 16, tm)
        l4, l16 = _unfold_load(lb4_ref, s_f, 4, tm), _unfold_load(lb16_ref, s_f, 16, tm)
        o1, l1 = ob1_ref[...].astype(F32), lb1_ref[...]
        mx = jnp.maximum(jnp.maximum(l1, l4), l16)
        e1, e4, e16 = jnp.exp(l1 - mx), jnp.exp(l4 - mx), jnp.exp(l16 - mx)
        den = e1 + e4 + e16
        ob = (e1 * o1 + e4 * o4 + e16 * o16) / den
        lse_b = mx + jnp.log(den)

        oa, ga, gb = oa_ref[...].astype(F32), ga_ref[...], gb_ref[...]
        sa, sb = _sigmoid(ga), _sigmoid(gb)
        mixed = jnp.concatenate(_unpair_tiles(_tiles(oa * (ga * sa))) + [ob * (gb * sb)], axis=1)
        mixed_bf = mixed.astype(BF16)
        w = w_ref[...]
        yv = x_ref[...] + jnp.dot(mixed_bf, w, preferred_element_type=F32)
        err = yv - t_ref[...]
        sq = jnp.sum(err * err, axis=0, keepdims=True)
        dy = err * (1.0 / D_MODEL)
        dy_ref[...] = dy
        dy_bf = dy.astype(BF16)
        gw = jnp.dot(mixed.T.astype(BF16), dy_bf, preferred_element_type=F32)

        @pl.when(i == 0)
        def _():
            loss_ref[...] = sq
            gwo_ref[...] = gw

        @pl.when(i > 0)
        def _():
            loss_ref[...] += sq
            gwo_ref[...] += gw

        dmix = lax.dot_general(dy_bf, w, (((1,), (1,)), ((), ())), preferred_element_type=F32)
        dma = jnp.concatenate(_pair_tiles(_tiles(dmix[:, :HALF_WIDTH])), axis=1)
        dmb = dmix[:, HALF_WIDTH:]
        bo = bo_ref[...]

        def head_delta(d_o, o):
            prod = d_o * o
            return jnp.concatenate([_head_sum(prod[:, j * PAIR:(j + 1) * PAIR], bo) for j in range(N_PAIRS)], axis=1)

        doa = dma * (ga * sa)
        doa_ref[...] = doa.astype(BF16)
        dla_ref[...] = head_delta(doa, oa)
        dga_ref[...] = dma * oa * (sa * (1.0 + ga * (1.0 - sa)))
        dob = dmb * (gb * sb)
        dgb_ref[...] = dmb * ob * (sb * (1.0 + gb * (1.0 - sb)))
        dlb = head_delta(dob, ob)
        dob_ref[...] = dob.astype(BF16)
        _fold_store(dob, s_f, dob4_ref, dob16_ref, tm)
        dlb_ref[...] = dlb
        _fold_store(dlb, s_f, dlb4_ref, dlb16_ref, tm)
        lse_ref[...] = lse_b
        _fold_store(lse_b, s_f, lse4_ref, lse16_ref, tm)

    row = lambda width: pl.BlockSpec((tm, width), lambda i: (i, 0))
    full = lambda a: pl.BlockSpec(a.shape, lambda i: (0,) * a.ndim)
    fb_shapes, fb_specs = _fold_specs(tm, BF16)
    ff_shapes, ff_specs = _fold_specs(tm, F32)
    nat = lambda dtype: jax.ShapeDtypeStruct((SEQ, HALF_WIDTH), dtype)
    return pl.pallas_call(
        body, name="tail", grid=(SEQ // tm,),
        in_specs=[row(HALF_WIDTH), row(HALF_WIDTH), row(HALF_WIDTH), ff_specs[0], ff_specs[0], ff_specs[1], ff_specs[1],
                  row(HALF_WIDTH), row(HALF_WIDTH), row(D_MODEL), row(D_MODEL), full(w_out), full(bones)],
        out_specs=(pl.BlockSpec((1, D_MODEL), lambda i: (0, 0)), row(D_MODEL),
                   pl.BlockSpec((D_MODEL, D_MODEL), lambda i: (0, 0)),
                   row(HALF_WIDTH), row(HALF_WIDTH), row(HALF_WIDTH), row(HALF_WIDTH),
                   row(HALF_WIDTH), *fb_specs, row(HALF_WIDTH), *ff_specs, row(HALF_WIDTH), *ff_specs),
        out_shape=(jax.ShapeDtypeStruct((1, D_MODEL), F32), jax.ShapeDtypeStruct((SEQ, D_MODEL), F32),
                   jax.ShapeDtypeStruct((D_MODEL, D_MODEL), F32),
                   nat(BF16), nat(F32), nat(F32), nat(F32),
                   nat(BF16), *fb_shapes, nat(F32), *ff_shapes, nat(F32), *ff_shapes),
        scratch_shapes=[_fold_scratch(tm)],
        compiler_params=_params(("arbitrary",)),
    )(oa, ob1, lb1, ob4, lb4, ob16, lb16, gate_a, gate_b, x, target, w_out, bones)


def _dproj_assemble(dqa, dka, dva, dga, dgb, dq1, dk1, dv1, dq4, dk4, dv4, dq16, dk16, dv16, tqa, tqb, tkb, tka,
                    qkg, cos4, sin4, bmean):
    tm = ROW_TILE

    def norm_rope_bwd(d_out, t, g, cos, sin, bm, scale):
        d_r = d_out * scale
        dyv = d_r * cos + _swap_halves(d_r * sin)
        rr = lax.rsqrt(_head_sum(t * t, bm) + EPS)
        that = t * rr
        dgain = jnp.sum(dyv * that, axis=0, keepdims=True)
        gdy = dyv * g
        dt = rr * (gdy - that * _head_sum(that * gdy, bm))
        return dt, dgain

    def body(dqa_ref, dka_ref, dva_ref, dga_ref, dgb_ref, dq1_ref, dk1_ref, dv1_ref, dq4_ref, dk4_ref, dv4_ref,
             dq16_ref, dk16_ref, dv16_ref, tqa_ref, tqb_ref, tkb_ref, tka_ref, qkg_ref, cos_ref, sin_ref, bm_ref,
             dproj_ref, dqkg_ref, s_f):
        i = pl.program_id(0)
        cos, sin, bm = cos_ref[...], sin_ref[...], bm_ref[...]

        def merged(nat_ref, f4_ref, f16_ref):
            return nat_ref[...].astype(F32) + _unfold_load(f4_ref, s_f, 4, tm) + _unfold_load(f16_ref, s_f, 16, tm)

        @pl.when(i == 0)
        def _():
            dqkg_ref[...] = jnp.zeros_like(dqkg_ref)

        def through(d_out, t, row, scale, c0, paired=False):
            g = qkg_ref[row:row + 1, :]
            tot = jnp.zeros((1, PAIR), F32)
            dts = []
            for j in range(d_out.shape[1] // PAIR):
                cols = slice(j * PAIR, (j + 1) * PAIR)
                dt, dg = norm_rope_bwd(d_out[:, cols], t[:, cols], g, cos, sin, bm, scale)
                dts.append(dt)
                tot = tot + dg
            if paired:
                dts = _unpair_tiles(dts)
            for j, dt in enumerate(dts):
                dproj_ref[:, c0 + j * PAIR:c0 + (j + 1) * PAIR] = dt.astype(BF16)
            dqkg_ref[row:row + 1, :] += tot

        through(dqa_ref[...].astype(F32), tqa_ref[...], 0, HEAD_DIM ** -0.5, C_QA, paired=True)
        through(dka_ref[...].astype(F32), tka_ref[...], 1, 1.0, C_KA)
        through(merged(dq1_ref, dq4_ref, dq16_ref), tqb_ref[...], 2, HEAD_DIM ** -0.5, C_QB)
        through(merged(dk1_ref, dk4_ref, dk16_ref), tkb_ref[...], 3, 1.0, C_KB)
        dproj_ref[:, C_VB:C_VB + HALF_WIDTH] = merged(dv1_ref, dv4_ref, dv16_ref).astype(BF16)
        dproj_ref[:, C_GA:C_GA + HALF_WIDTH] = jnp.concatenate(_unpair_tiles(_tiles(dga_ref[...])), axis=1).astype(BF16)
        dproj_ref[:, C_GB:C_GB + HALF_WIDTH] = dgb_ref[...].astype(BF16)
        dproj_ref[:, C_VA:C_VA + KV_A_WIDTH] = dva_ref[...].astype(BF16)

    row = lambda width: pl.BlockSpec((tm, width), lambda i: (i, 0))
    full = lambda a: pl.BlockSpec(a.shape, lambda i: (0,) * a.ndim)
    _, ff_specs = _fold_specs(tm, F32)
    return pl.pallas_call(
        body, name="dproj_assemble", grid=(SEQ // tm,),
        in_specs=[row(HALF_WIDTH), row(KV_A_WIDTH), row(KV_A_WIDTH), row(HALF_WIDTH), row(HALF_WIDTH),
                  row(HALF_WIDTH), row(HALF_WIDTH), row(HALF_WIDTH), ff_specs[0], ff_specs[0], ff_specs[0],
                  ff_specs[1], ff_specs[1], ff_specs[1],
                  row(HALF_WIDTH), row(HALF_WIDTH), row(HALF_WIDTH), row(KV_A_WIDTH),
                  full(qkg), row(PAIR), row(PAIR), full(bmean)],
        out_specs=(row(IN_WIDTH), pl.BlockSpec((SMALL_ROWS, PAIR), lambda i: (0, 0))),
        out_shape=(jax.ShapeDtypeStruct((SEQ, IN_WIDTH), BF16), jax.ShapeDtypeStruct((SMALL_ROWS, PAIR), F32)),
        scratch_shapes=[_fold_scratch(tm)],
        compiler_params=_params(("arbitrary",)),
    )(dqa, dka, dva, dga, dgb, dq1, dk1, dv1, dq4, dk4, dv4, dq16, dk16, dv16, tqa, tqb, tkb, tka, qkg, cos4, sin4, bmean)


def _input_grad_reduce(dproj, w, x, gain, dy, blocks_in, blocks_out, small):
    tm = ROW_TILE
    n_steps = SEQ // tm
    stage2_step = 3
    shapes = (blocks_in.shape[1:], blocks_out.shape[1:])

    def body(dp_ref, w_ref, x_ref, g_ref, dy_ref, ga_hbm, gb_hbm, small_ref,
             gx_ref, out_a, out_b, small_out_ref, dgain_out_ref,
             part_a, part_b, sib_a, sib_b, wire_a, wire_b, chips_a, chips_b, small_all, dgain_acc, dgain_all,
             load_sems, sib_send, sib_recv, chip_send, chip_recv, small_send, small_recv, dgain_send, dgain_recv):
        i = pl.program_id(0)
        x, y, c = lax.axis_index("x"), lax.axis_index("y"), lax.axis_index("c")
        sibling = (x, y, 1 - c)
        chips = [(x, y), (1 - x, y), (x, 1 - y), (1 - x, 1 - y)]
        my_id = 4 * x + 2 * y + c
        g_hbm, part, from_sib = (ga_hbm, gb_hbm), (part_a, part_b), (sib_a, sib_b)
        to_wire, from_chips, out = (wire_a, wire_b), (chips_a, chips_b), (out_a, out_b)
        both = (0, 1)

        def blk(a, chip, core):
            return g_hbm[a].at[4 * chip[0] + 2 * chip[1] + core]

        def to_all(src, dst_all, send, recv):
            copies = []
            for rel in range(1, N_DEV):
                dx, dy_, dc = (rel >> 2) & 1, (rel >> 1) & 1, rel & 1
                to = (1 - x if dx else x, 1 - y if dy_ else y, 1 - c if dc else c)
                copies.append(pltpu.make_async_remote_copy(
                    src_ref=src, dst_ref=dst_all.at[my_id], send_sem=send.at[rel - 1], recv_sem=recv.at[rel - 1],
                    device_id=to, device_id_type=MESH))
            return copies

        small_copies = to_all(small_all.at[my_id], small_all, small_send, small_recv)
        dgain_copies = to_all(dgain_acc, dgain_all, dgain_send, dgain_recv)
        loads = [[pltpu.make_async_copy(blk(a, chips[k], c), part[a].at[k], load_sems.at[a, k]) for k in range(4)] for a in both]
        to_sib = [[pltpu.make_async_remote_copy(
            src_ref=blk(a, chips[k], 1 - c), dst_ref=from_sib[a].at[k], send_sem=sib_send.at[a, k], recv_sem=sib_recv.at[a, k],
            device_id=sibling, device_id_type=MESH) for k in range(4)] for a in both]
        to_chips = [[pltpu.make_async_remote_copy(
            src_ref=to_wire[a].at[k - 1], dst_ref=from_chips[a].at[k - 1],
            send_sem=chip_send.at[a, k - 1], recv_sem=chip_recv.at[a, k - 1],
            device_id=(*chips[k], c), device_id_type=MESH) for k in range(1, 4)] for a in both]

        @pl.when(i == 0)
        def _():
            small_all[my_id] = small_ref[...]
            for cp in small_copies:
                cp.start()
            for k in (1, 2, 3, 0):
                for a in both:
                    loads[a][k].start()
                    to_sib[a][k].start()

        @pl.when(i == stage2_step)
        def _():
            for k in (1, 2, 3):
                for a in both:
                    loads[a][k].wait()
                    to_sib[a][k].wait_recv()
                    to_wire[a][k - 1] = (part[a][k].astype(F32) + from_sib[a][k].astype(F32)).astype(BF16)
                    to_chips[a][k - 1].start()

        dh = lax.dot_general(dp_ref[...], w_ref[...], (((1,), (1,)), ((), ())), preferred_element_type=F32)
        xf = x_ref[...]
        r = lax.rsqrt(jnp.mean(xf * xf, axis=-1, keepdims=True) + EPS)
        xhat = xf * r
        dg = jnp.sum(dh * xhat, axis=0, keepdims=True)
        dxh = dh * g_ref[...]
        dx = r * (dxh - xhat * jnp.mean(dxh * xhat, axis=-1, keepdims=True))
        gx_ref[...] = dy_ref[...] + dx

        @pl.when(i == 0)
        def _():
            dgain_acc[...] = dg

        @pl.when(i > 0)
        def _():
            dgain_acc[...] += dg

        @pl.when(i == n_steps - 1)
        def _():
            dgain_all[my_id] = dgain_acc[...]
            for cp in dgain_copies:
                cp.start()
            for a in both:
                loads[a][0].wait()
                to_sib[a][0].wait_recv()
                acc = part[a][0].astype(F32) + from_sib[a][0].astype(F32)
                for k in range(3):
                    to_chips[a][k].wait_recv()
                    acc = acc + from_chips[a][k].astype(F32)
                out[a][...] = acc
            for copies, gathered, dst in ((small_copies, small_all, small_out_ref), (dgain_copies, dgain_all, dgain_out_ref)):
                for cp in copies:
                    cp.wait_recv()
                tot = gathered[0]
                for d in range(1, N_DEV):
                    tot = tot + gathered[d]
                dst[...] = tot
            for cp in to_sib[0] + to_sib[1] + to_chips[0] + to_chips[1] + small_copies + dgain_copies:
                cp.wait_send()

    row = lambda width: pl.BlockSpec((tm, width), lambda i: (i, 0))
    full = lambda a: pl.BlockSpec(a.shape, lambda i: (0,) * a.ndim)
    whole = lambda shape: pl.BlockSpec(shape, lambda i: (0,) * len(shape))
    hbm = pl.BlockSpec(memory_space=pl.ANY)
    dtypes = (blocks_in.dtype, blocks_out.dtype)
    buf = lambda n, dts: [pltpu.VMEM((n,) + s, dt) for s, dt in zip(shapes, dts)]
    return pl.pallas_call(
        body, name="input_grad_rs", grid=(n_steps,),
        in_specs=[row(IN_WIDTH), full(w), row(D_MODEL), full(gain), row(D_MODEL), hbm, hbm, full(small)],
        out_specs=(row(D_MODEL), whole(shapes[0]), whole(shapes[1]), whole((SMALL_ROWS, SMALL_COLS)), whole((1, D_MODEL))),
        out_shape=(jax.ShapeDtypeStruct((SEQ, D_MODEL), F32), jax.ShapeDtypeStruct(shapes[0], F32),
                   jax.ShapeDtypeStruct(shapes[1], F32), jax.ShapeDtypeStruct((SMALL_ROWS, SMALL_COLS), F32),
                   jax.ShapeDtypeStruct((1, D_MODEL), F32)),
        scratch_shapes=[*buf(4, dtypes), *buf(4, dtypes), *buf(3, (BF16, BF16)), *buf(3, (BF16, BF16)),
                        pltpu.VMEM((N_DEV, SMALL_ROWS, SMALL_COLS), F32),
                        pltpu.VMEM((1, D_MODEL), F32), pltpu.VMEM((N_DEV, 1, D_MODEL), F32),
                        pltpu.SemaphoreType.DMA((2, 4)), pltpu.SemaphoreType.DMA((2, 4)), pltpu.SemaphoreType.DMA((2, 4)),
                        pltpu.SemaphoreType.DMA((2, 3)), pltpu.SemaphoreType.DMA((2, 3)),
                        pltpu.SemaphoreType.DMA((7,)), pltpu.SemaphoreType.DMA((7,)),
                        pltpu.SemaphoreType.DMA((7,)), pltpu.SemaphoreType.DMA((7,))],
        compiler_params=_params(("arbitrary",)),
    )(dproj, w, x, gain, dy, blocks_in, blocks_out, small)


def _weight_grad(h_t, dproj):
    tk = 512
    cb = IN_WIDTH // 2
    n_k = SEQ // tk

    def body(ht_ref, dp_ref, out_ref, acc):
        k = pl.program_id(1)
        upd = jnp.dot(ht_ref[...], dp_ref[...], preferred_element_type=F32)

        @pl.when(k == 0)
        def _():
            acc[...] = upd

        @pl.when((k > 0) & (k < n_k - 1))
        def _():
            acc[...] += upd

        @pl.when(k == n_k - 1)
        def _():
            out_ref[...] = (acc[...] + upd).astype(BF16)

    return pl.pallas_call(
        body, name="weight_grad", grid=(2, n_k),
        in_specs=[pl.BlockSpec((D_MODEL, tk), lambda j, k: (0, k)), pl.BlockSpec((tk, cb), lambda j, k: (k, j))],
        out_specs=pl.BlockSpec((D_MODEL, cb), lambda j, k: (0, j)),
        out_shape=jax.ShapeDtypeStruct((D_MODEL, IN_WIDTH), BF16),
        scratch_shapes=[pltpu.VMEM((D_MODEL, cb), F32)],
        compiler_params=_params(("arbitrary", "arbitrary")),
    )(h_t, dproj)


def _adamw(name, w, g, m, v):
    def body(w_ref, g_ref, m_ref, v_ref, d_ref, nm_ref, nv_ref):
        gv = g_ref[...]
        nm = ADAM_B1 * m_ref[...] + (1.0 - ADAM_B1) * gv
        nv = ADAM_B2 * v_ref[...] + (1.0 - ADAM_B2) * jnp.square(gv)
        m_hat = nm / (1.0 - ADAM_B1 ** ADAM_STEP)
        v_hat = nv / (1.0 - ADAM_B2 ** ADAM_STEP)
        d_ref[...] = -ADAM_LR * (m_hat / (jnp.sqrt(v_hat) + ADAM_EPS) + ADAM_WD * w_ref[...])
        nm_ref[...] = nm
        nv_ref[...] = nv

    vmem = pl.BlockSpec(memory_space=pltpu.VMEM)
    out = jax.ShapeDtypeStruct(w.shape, F32)
    return pl.pallas_call(
        body, name=name, in_specs=[vmem] * 4, out_specs=(vmem,) * 3, out_shape=(out,) * 3,
        compiler_params=pltpu.CompilerParams(vmem_limit_bytes=VMEM_LIMIT),
    )(w, g, m, v)


SMALL_USED = D_MODEL + 4 * HEAD_DIM + 8


def _pack_small(norm_gain, qa, ka, sinks, qb, kb, extra=None):
    parts = [norm_gain.reshape(-1), qa.reshape(-1), ka.reshape(-1), sinks.reshape(-1), qb.reshape(-1), kb.reshape(-1)]
    if extra is not None:
        parts.append(extra.reshape(-1))
    flat = jnp.concatenate(parts)
    flat = jnp.pad(flat, (0, SMALL_ROWS * SMALL_COLS - flat.shape[0]))
    return flat.reshape(SMALL_ROWS, SMALL_COLS)


def _unpack_small(a):
    flat = a.reshape(-1)
    sizes = (D_MODEL, HEAD_DIM, HEAD_DIM, 8, HEAD_DIM, HEAD_DIM)
    out, off = [], 0
    for s in sizes:
        out.append(flat[off:off + s].reshape(1, s))
        off += s
    return out


def _fold_heads(row):
    return row[0, :HEAD_DIM] + row[0, HEAD_DIM:]


def kernel(x, norm_gain, w_in, q_norm_a, k_norm_a, sinks_a, q_norm_b, k_norm_b, w_out, loss_target, m_norm_gain, m_w_in, m_q_norm_a, m_k_norm_a, m_sinks_a, m_q_norm_b, m_k_norm_b, m_w_out, v_norm_gain, v_w_in, v_q_norm_a, v_k_norm_a, v_sinks_a, v_q_norm_b, v_k_norm_b, v_w_out):
    x2, tgt = x[0], loss_target[0]
    w_in_sh, w_out_sh = w_in[0], w_out[0]

    w_full = _all_gather_w_in(w_in_sh).transpose(1, 0, 2).reshape(D_MODEL, IN_WIDTH)

    inv = np.float32(ROPE_THETA) ** (-np.arange(HEAD_DIM // 2, dtype=np.float32) / np.float32(HEAD_DIM // 2))
    ang = np.arange(SEQ, dtype=np.float32)[:, None] * inv[None, :].astype(np.float32)
    cos, sin = np.cos(ang).astype(np.float32), np.sin(ang).astype(np.float32)
    cos4 = jnp.asarray(np.concatenate([cos, cos, cos, cos], axis=1))
    sin4 = jnp.asarray(np.concatenate([-sin, sin, -sin, sin], axis=1))
    blockdiag = np.kron(np.eye(2, dtype=np.float32), np.ones((HEAD_DIM, HEAD_DIM), np.float32))
    bmean = jnp.asarray(blockdiag / HEAD_DIM, dtype=BF16)
    bones = jnp.asarray(blockdiag, dtype=BF16)
    two = lambda g: jnp.concatenate([g, g], axis=1)
    qkg = jnp.concatenate([two(q_norm_a), two(k_norm_a), two(q_norm_b), two(k_norm_b),
                           jnp.zeros((SMALL_ROWS - 4, PAIR), F32)], axis=0)
    sinks_paired = jnp.stack([sinks_a[0, :N_PAIRS], sinks_a[0, N_PAIRS:]], axis=1)
    sink_rows = jnp.concatenate([jnp.repeat(sinks_paired, BLOCK, axis=1),
                                 jnp.zeros((SMALL_ROWS - N_PAIRS, 2 * BLOCK), F32)], axis=0)

    (tqa, tka, tqb, tkb, gate_a, gate_b, h_t, qa, ka, va, qb, kb, vb, qb4, qb16, kb4, kb16, vb4, vb16,
     gathered_out) = _proj_fwd(x2, norm_gain, w_full, qkg, cos4, sin4, bmean, w_out_sh)
    wo_full = gathered_out.reshape(D_MODEL, D_MODEL)
    oa, la = _attn_fwd("attn_a_fwd", qa[None], ka[None], va[None], sink_rows, BLOCK - 1)
    ob1, lb1 = _attn_fwd("attn_b1_fwd", qb[None], kb[None], vb[None], None, BLOCK)
    ob4, lb4 = _attn_fwd("attn_b4_fwd", qb4, kb4, vb4, None, BLOCK)
    ob16, lb16 = _attn_fwd("attn_b16_fwd", qb16, kb16, vb16, None, BLOCK)
    (loss_cols, dy, gwo, doa, dla, dga, dgb, dob, dob4, dob16, dlb, dlb4, dlb16, lse_b, lse4, lse16) = _tail(
        oa[0], ob1[0], lb1[0], ob4, lb4, ob16, lb16, gate_a, gate_b, x2, tgt, wo_full, bones)

    dqa, dka, dva, dsink = _attn_bwd("attn_a_bwd", qa[None], ka[None], va[None], doa[None], la, dla[None], sink_rows, BLOCK - 1)
    dq1, dk1, dv1 = _attn_bwd("attn_b1_bwd", qb[None], kb[None], vb[None], dob[None], lse_b[None], dlb[None], None, BLOCK)
    dq4, dk4, dv4 = _attn_bwd("attn_b4_bwd", qb4, kb4, vb4, dob4, lse4, dlb4, None, BLOCK)
    dq16, dk16, dv16 = _attn_bwd("attn_b16_bwd", qb16, kb16, vb16, dob16, lse16, dlb16, None, BLOCK)
    dproj, dqkg = _dproj_assemble(dqa[0], dka[0], dva[0], dga, dgb, dq1[0], dk1[0], dv1[0], dq4, dk4, dv4,
                                  dq16, dk16, dv16, tqa, tqb, tkb, tka, qkg, cos4, sin4, bmean)
    gw_in = _weight_grad(h_t, dproj)

    blocks_in = gw_in.reshape(D_MODEL, N_DEV, SHARD_IN).transpose(1, 0, 2)
    blocks_out = gwo.reshape(N_DEV, SHARD_OUT, D_MODEL)
    g_sinks = jnp.concatenate([jnp.sum(dsink[:N_PAIRS, :BLOCK], axis=1), jnp.sum(dsink[:N_PAIRS, BLOCK:], axis=1)])
    small = _pack_small(jnp.zeros((D_MODEL,), F32), _fold_heads(dqkg[0:1]), _fold_heads(dqkg[1:2]), g_sinks,
                        _fold_heads(dqkg[2:3]), _fold_heads(dqkg[3:4]), extra=0.5 * jnp.sum(loss_cols) / D_MODEL)
    grad_x, g_w_in, g_w_out, small_red, dgain_red = _input_grad_reduce(
        dproj, w_full, x2, norm_gain, dy, blocks_in, blocks_out, small)
    n_gain_rows = D_MODEL // SMALL_COLS
    small_red = jnp.concatenate([dgain_red.reshape(n_gain_rows, SMALL_COLS), small_red[n_gain_rows:]], axis=0)
    g_small = _unpack_small(small_red)

    d_in, nm_in, nv_in = _adamw("adamw_w_in", w_in_sh, g_w_in, m_w_in[0], v_w_in[0])
    d_out, nm_out, nv_out = _adamw("adamw_w_out", w_out_sh, g_w_out, m_w_out[0], v_w_out[0])
    d_s, nm_s, nv_s = _adamw(
        "adamw_small",
        _pack_small(norm_gain, q_norm_a, k_norm_a, sinks_a, q_norm_b, k_norm_b), small_red,
        _pack_small(m_norm_gain, m_q_norm_a, m_k_norm_a, m_sinks_a, m_q_norm_b, m_k_norm_b),
        _pack_small(v_norm_gain, v_q_norm_a, v_k_norm_a, v_sinks_a, v_q_norm_b, v_k_norm_b))
    d_small, nm_small, nv_small = _unpack_small(d_s), _unpack_small(nm_s), _unpack_small(nv_s)

    loss = small_red.reshape(-1)[SMALL_USED]

    def assemble(small_list, big_in, big_out):
        ng, qa_, ka_, sk_, qb_, kb_ = small_list
        return [ng, big_in[None], qa_, ka_, sk_, qb_, kb_, big_out[None]]

    return (loss, grad_x[None], *assemble(g_small, g_w_in, g_w_out), *assemble(d_small, d_in, d_out),
            *assemble(nm_small, nm_in, nm_out), *assemble(nv_small, nv_in, nv_out))
```

```python
import functools

import numpy as np
import jax
import jax.numpy as jnp
from jax import lax
from jax.experimental import pallas as pl
from jax.experimental.pallas import tpu as pltpu

F32 = jnp.float32
BF16 = jnp.bfloat16

SEQ = 4096
D_MODEL = 1024
HEAD_DIM = 64
PAIR = 2 * HEAD_DIM
N_PAIRS = 4
HALF_WIDTH = N_PAIRS * PAIR
KV_A_WIDTH = 128
IN_WIDTH = 3328
BLOCK = 128
EPS = 1e-6
NEG = -1e30
ROPE_THETA = 10000.0
N_DEV = 8
SHARD_IN = IN_WIDTH // N_DEV
SHARD_OUT = D_MODEL // N_DEV
PAYLOAD = SHARD_IN + SHARD_OUT
SMALL_ROWS, SMALL_COLS = 8, 256

C_QA, C_KA, C_VA, C_GA, C_QB, C_KB, C_VB, C_GB = 0, 512, 640, 768, 1280, 1792, 2304, 2816

ADAM_LR = 0.001
ADAM_B1 = 0.9
ADAM_B2 = 0.999
ADAM_EPS = 1e-08
ADAM_WD = 0.01
ADAM_STEP = 10

ROW_TILE = 256
FWD_BLOCKS_PER_STEP = 4
BWD_BLOCKS_PER_STEP = 4
VMEM_LIMIT = 56 * 1024 * 1024

MESH = pl.DeviceIdType.MESH


def _params(sem, vmem=VMEM_LIMIT):
    return pltpu.CompilerParams(dimension_semantics=sem, vmem_limit_bytes=vmem)


def _head_sum(v, bm):
    hi = v.astype(BF16)
    lo = (v - hi.astype(F32)).astype(BF16)
    return (jnp.dot(hi, bm, preferred_element_type=F32) + jnp.dot(lo, bm, preferred_element_type=F32))


def _swap_halves(y):
    lane = lax.broadcasted_iota(jnp.int32, y.shape, 1)
    first = (lane & 32) == 0
    return jnp.where(first, pltpu.roll(y, 96, 1), pltpu.roll(y, 32, 1))


def _sigmoid(g):
    return 1.0 / (1.0 + jnp.exp(-g))


def _tiles(a):
    return [a[:, j * PAIR:(j + 1) * PAIR] for j in range(N_PAIRS)]


def _pair_tiles(t):
    low = lax.broadcasted_iota(jnp.int32, t[0].shape, 1) < HEAD_DIM
    r = [pltpu.roll(a, HEAD_DIM, 1) for a in t]
    return [jnp.where(low, t[0], r[2]), jnp.where(low, r[0], t[2]), jnp.where(low, t[1], r[3]), jnp.where(low, r[1], t[3])]


def _unpair_tiles(p):
    low = lax.broadcasted_iota(jnp.int32, p[0].shape, 1) < HEAD_DIM
    r = [pltpu.roll(a, HEAD_DIM, 1) for a in p]
    return [jnp.where(low, p[0], r[1]), jnp.where(low, p[2], r[3]), jnp.where(low, r[0], p[1]), jnp.where(low, r[2], p[3])]


def _gather_plan(mine_ref, out_ref, send_sems, recv_sems):
    x, y, c = lax.axis_index("x"), lax.axis_index("y"), lax.axis_index("c")
    me, sibling = (x, y, c), (x, y, 1 - c)
    chips = [(1 - x, y), (x, 1 - y), (1 - x, 1 - y)]

    def slot(px, py, pc):
        return out_ref.at[4 * px + 2 * py + pc]

    def copy(k, block, to, from_mine=False):
        return pltpu.make_async_remote_copy(
            src_ref=mine_ref if from_mine else slot(*block), dst_ref=slot(*block),
            send_sem=send_sems.at[k], recv_sem=recv_sems.at[k], device_id=to, device_id_type=MESH)

    first = [copy(0, me, sibling, True)] + [copy(1 + j, me, (*chip, c), True) for j, chip in enumerate(chips)]
    landed = [copy(1 + j, (*chip, c), me) for j, chip in enumerate(chips)]
    passed = [copy(4 + j, (*chip, c), sibling) for j, chip in enumerate(chips)]
    from_sibling = [copy(0, sibling, me)] + [copy(4 + j, (*chip, 1 - c), me) for j, chip in enumerate(chips)]
    return slot(*me), first, landed, passed, from_sibling


GATHER_SCRATCH = [pltpu.SemaphoreType.DMA((7,)), pltpu.SemaphoreType.DMA((7,))]


def _all_gather_w_in(w_in_sh):
    rows, cols = w_in_sh.shape

    def body(w_ref, out_ref, mine_ref, blocks, send_sems, recv_sems):
        mine_ref[...] = w_ref[...].astype(BF16)
        my_slot, first, landed, passed, from_sibling = _gather_plan(mine_ref, blocks, send_sems, recv_sems)
        for cp in first:
            cp.start()
        my_slot[...] = mine_ref[...]
        for arrival, forward in zip(landed, passed):
            arrival.wait_recv()
            forward.start()
        for arrival in from_sibling:
            arrival.wait_recv()
        for cp in first + passed:
            cp.wait_send()
        for d in range(N_DEV):
            out_ref[:, d * cols:(d + 1) * cols] = blocks[d]

    vmem = pl.BlockSpec(memory_space=pltpu.VMEM)
    return pl.pallas_call(
        body, name="ag_w_in",
        out_shape=jax.ShapeDtypeStruct((rows, N_DEV * cols), BF16),
        in_specs=[vmem], out_specs=vmem,
        scratch_shapes=[pltpu.VMEM((rows, cols), BF16), pltpu.VMEM((N_DEV, rows, cols), BF16)] + GATHER_SCRATCH,
        compiler_params=pltpu.CompilerParams(vmem_limit_bytes=VMEM_LIMIT),
    )(w_in_sh)


def _reduce_scatter_grads(blocks_in, blocks_out, small):
    shapes = (blocks_in.shape[1:], blocks_out.shape[1:])

    def body(ga_hbm, gb_hbm, small_ref, out_a, out_b, small_out_ref,
             part_a, part_b, sib_a, sib_b, wire_a, wire_b, chips_a, chips_b, small_all,
             load_sems, sib_send, sib_recv, chip_send, chip_recv, small_send, small_recv):
        x, y, c = lax.axis_index("x"), lax.axis_index("y"), lax.axis_index("c")
        sibling = (x, y, 1 - c)
        chips = [(x, y), (1 - x, y), (x, 1 - y), (1 - x, 1 - y)]
        my_id = 4 * x + 2 * y + c
        g_hbm, part, from_sib = (ga_hbm, gb_hbm), (part_a, part_b), (sib_a, sib_b)
        to_wire, from_chips, out = (wire_a, wire_b), (chips_a, chips_b), (out_a, out_b)
        both = (0, 1)

        def blk(a, chip, core):
            return g_hbm[a].at[4 * chip[0] + 2 * chip[1] + core]

        small_all[my_id] = small_ref[...]
        small_copies = []
        for rel in range(1, N_DEV):
            dx, dy, dc = (rel >> 2) & 1, (rel >> 1) & 1, rel & 1
            to = (1 - x if dx else x, 1 - y if dy else y, 1 - c if dc else c)
            small_copies.append(pltpu.make_async_remote_copy(
                src_ref=small_ref, dst_ref=small_all.at[my_id],
                send_sem=small_send.at[rel - 1], recv_sem=small_recv.at[rel - 1], device_id=to, device_id_type=MESH))
        for cp in small_copies:
            cp.start()

        loads = [[pltpu.make_async_copy(blk(a, chips[k], c), part[a].at[k], load_sems.at[a, k]) for k in range(4)] for a in both]
        to_sib = [[pltpu.make_async_remote_copy(
            src_ref=blk(a, chips[k], 1 - c), dst_ref=from_sib[a].at[k], send_sem=sib_send.at[a, k], recv_sem=sib_recv.at[a, k],
            device_id=sibling, device_id_type=MESH) for k in range(4)] for a in both]
        for k in (1, 2, 3, 0):
            for a in both:
                loads[a][k].start()
                to_sib[a][k].start()

        to_chips = [[pltpu.make_async_remote_copy(
            src_ref=to_wire[a].at[k - 1], dst_ref=from_chips[a].at[k - 1],
            send_sem=chip_send.at[a, k - 1], recv_sem=chip_recv.at[a, k - 1],
            device_id=(*chips[k], c), device_id_type=MESH) for k in range(1, 4)] for a in both]
        for k in (1, 2, 3):
            for a in both:
                loads[a][k].wait()
                to_sib[a][k].wait_recv()
                to_wire[a][k - 1] = (part[a][k] + from_sib[a][k]).astype(BF16)
                to_chips[a][k - 1].start()
        for a in both:
            loads[a][0].wait()
            to_sib[a][0].wait_recv()
            acc = part[a][0] + from_sib[a][0]
            for k in range(3):
                to_chips[a][k].wait_recv()
                acc = acc + from_chips[a][k].astype(F32)
            out[a][...] = acc

        for cp in small_copies:
            cp.wait_recv()
        tot = small_all[0]
        for d in range(1, N_DEV):
            tot = tot + small_all[d]
        small_out_ref[...] = tot
        for cp in to_sib[0] + to_sib[1] + to_chips[0] + to_chips[1] + small_copies:
            cp.wait_send()

    vmem = pl.BlockSpec(memory_space=pltpu.VMEM)
    hbm = pl.BlockSpec(memory_space=pl.ANY)
    buf = lambda n, dtype: [pltpu.VMEM((n,) + s, dtype) for s in shapes]
    return pl.pallas_call(
        body, name="rs_grads",
        out_shape=(jax.ShapeDtypeStruct(shapes[0], F32), jax.ShapeDtypeStruct(shapes[1], F32),
                   jax.ShapeDtypeStruct((SMALL_ROWS, SMALL_COLS), F32)),
        in_specs=[hbm, hbm, vmem], out_specs=(vmem, vmem, vmem),
        scratch_shapes=[*buf(4, F32), *buf(4, F32), *buf(3, BF16), *buf(3, BF16),
                        pltpu.VMEM((N_DEV, SMALL_ROWS, SMALL_COLS), F32),
                        pltpu.SemaphoreType.DMA((2, 4)), pltpu.SemaphoreType.DMA((2, 4)), pltpu.SemaphoreType.DMA((2, 4)),
                        pltpu.SemaphoreType.DMA((2, 3)), pltpu.SemaphoreType.DMA((2, 3)),
                        pltpu.SemaphoreType.DMA((7,)), pltpu.SemaphoreType.DMA((7,))],
        compiler_params=pltpu.CompilerParams(vmem_limit_bytes=VMEM_LIMIT),
    )(blocks_in, blocks_out, small)


def _fold_scratch(tm):
    return pltpu.VMEM((N_PAIRS, tm, PAIR), F32)


def _fold_store(val, scr, out4, out16, tm):
    for j in range(N_PAIRS):
        scr[j] = val[:, j * PAIR:(j + 1) * PAIR]
    for dil, out in ((4, out4), (16, out16)):
        for r in range(dil):
            for j in range(N_PAIRS):
                out[r, :, j * PAIR:(j + 1) * PAIR] = scr[j, pl.ds(r, tm // dil, stride=dil), :].astype(out.dtype)


def _unfold_load(src, scr, dil, tm):
    for r in range(dil):
        for j in range(N_PAIRS):
            scr[j, pl.ds(r, tm // dil, stride=dil), :] = src[r, :, j * PAIR:(j + 1) * PAIR].astype(F32)
    return jnp.concatenate([scr[j] for j in range(N_PAIRS)], axis=1)


def _fold_specs(tm, dtype):
    shapes = (jax.ShapeDtypeStruct((4, SEQ // 4, HALF_WIDTH), dtype), jax.ShapeDtypeStruct((16, SEQ // 16, HALF_WIDTH), dtype))
    specs = (pl.BlockSpec((4, tm // 4, HALF_WIDTH), lambda i: (0, i, 0)),
             pl.BlockSpec((16, tm // 16, HALF_WIDTH), lambda i: (0, i, 0)))
    return shapes, specs


def _proj_fwd(x, gain, w, qkg, cos4, sin4, bmean, w_out_sh):
    tm = ROW_TILE
    n_steps = SEQ // tm

    def norm_rope(t, g, cos, sin, bm, scale):
        rr = lax.rsqrt(_head_sum(t * t, bm) + EPS)
        yv = t * rr * g
        return (yv * cos + _swap_halves(yv) * sin) * scale

    def body(x_ref, g_ref, w_ref, qkg_ref, cos_ref, sin_ref, bm_ref, wo_ref,
             tqa_ref, tka_ref, tqb_ref, tkb_ref, ga_ref, gb_ref, ht_ref, qa_ref, ka_ref, va_ref, qb_ref, kb_ref, vb_ref,
             qb4_ref, qb16_ref, kb4_ref, kb16_ref, vb4_ref, vb16_ref, wo_all_ref,
             proj, scr, wo_mine, wo_all, send_sems, recv_sems):
        i = pl.program_id(0)
        my_slot, first, landed, passed, from_sibling = _gather_plan(wo_mine, wo_all, send_sems, recv_sems)

        @pl.when(i == 0)
        def _():
            wo_mine[...] = wo_ref[...].astype(BF16)
            for cp in first:
                cp.start()
            my_slot[...] = wo_mine[...]

        @pl.when(i == n_steps // 2)
        def _():
            for arrival, forward in zip(landed, passed):
                arrival.wait_recv()
                forward.start()

        xf = x_ref[...]
        r = lax.rsqrt(jnp.mean(xf * xf, axis=-1, keepdims=True) + EPS)
        hf = xf * r * g_ref[...]
        ht_ref[...] = hf.T.astype(BF16)
        cos, sin, bm = cos_ref[...], sin_ref[...], bm_ref[...]
        proj[...] = jnp.dot(hf.astype(BF16), w_ref[...], preferred_element_type=F32)

        def roped(tiles, row, scale):
            g = qkg_ref[row:row + 1, :]
            return jnp.concatenate([norm_rope(t, g, cos, sin, bm, scale) for t in tiles], axis=1)

        tqa = _pair_tiles(_tiles(proj[:, C_QA:C_QA + HALF_WIDTH]))
        tqa_ref[...] = jnp.concatenate(tqa, axis=1)
        qa_ref[...] = roped(tqa, 0, HEAD_DIM ** -0.5).astype(BF16)
        ga_ref[...] = jnp.concatenate(_pair_tiles(_tiles(proj[:, C_GA:C_GA + HALF_WIDTH])), axis=1)
        gb_ref[...] = proj[:, C_GB:C_GB + HALF_WIDTH]
        tqb = proj[:, C_QB:C_QB + HALF_WIDTH]
        tqb_ref[...] = tqb
        qb = roped(_tiles(tqb), 2, HEAD_DIM ** -0.5)
        qb_ref[...] = qb.astype(BF16)
        _fold_store(qb, scr, qb4_ref, qb16_ref, tm)
        tkb = proj[:, C_KB:C_KB + HALF_WIDTH]
        tkb_ref[...] = tkb
        kb = roped(_tiles(tkb), 3, 1.0)
        kb_ref[...] = kb.astype(BF16)
        _fold_store(kb, scr, kb4_ref, kb16_ref, tm)
        vb = proj[:, C_VB:C_VB + HALF_WIDTH]
        vb_ref[...] = vb.astype(BF16)
        _fold_store(vb, scr, vb4_ref, vb16_ref, tm)
        tka = proj[:, C_KA:C_KA + KV_A_WIDTH]
        tka_ref[...] = tka
        ka_ref[...] = roped([tka], 1, 1.0).astype(BF16)
        va_ref[...] = proj[:, C_VA:C_VA + KV_A_WIDTH].astype(BF16)

        @pl.when(i == n_steps - 1)
        def _():
            for arrival in from_sibling:
                arrival.wait_recv()
            for cp in first + passed:
                cp.wait_send()
            wo_all_ref[...] = wo_all[...]

    row = lambda width: pl.BlockSpec((tm, width), lambda i: (i, 0))
    full = lambda a: pl.BlockSpec(a.shape, lambda i: (0,) * a.ndim)
    nat = lambda width, dtype=BF16: jax.ShapeDtypeStruct((SEQ, width), dtype)
    f_shapes, f_specs = _fold_specs(tm, BF16)
    return pl.pallas_call(
        body, name="proj_fwd", grid=(SEQ // tm,),
        in_specs=[row(D_MODEL), full(gain), full(w), full(qkg), row(PAIR), row(PAIR), full(bmean), full(w_out_sh)],
        out_specs=(row(HALF_WIDTH), row(KV_A_WIDTH), row(HALF_WIDTH), row(HALF_WIDTH), row(HALF_WIDTH), row(HALF_WIDTH),
                   pl.BlockSpec((D_MODEL, tm), lambda i: (0, i)),
                   row(HALF_WIDTH), row(KV_A_WIDTH), row(KV_A_WIDTH), row(HALF_WIDTH), row(HALF_WIDTH), row(HALF_WIDTH),
                   *f_specs, *f_specs, *f_specs,
                   pl.BlockSpec((N_DEV,) + w_out_sh.shape, lambda i: (0, 0, 0))),
        out_shape=(nat(HALF_WIDTH, F32), nat(KV_A_WIDTH, F32), nat(HALF_WIDTH, F32), nat(HALF_WIDTH, F32),
                   nat(HALF_WIDTH, F32), nat(HALF_WIDTH, F32),
                   jax.ShapeDtypeStruct((D_MODEL, SEQ), BF16),
                   nat(HALF_WIDTH), nat(KV_A_WIDTH), nat(KV_A_WIDTH), nat(HALF_WIDTH), nat(HALF_WIDTH), nat(HALF_WIDTH),
                   *f_shapes, *f_shapes, *f_shapes,
                   jax.ShapeDtypeStruct((N_DEV,) + w_out_sh.shape, BF16)),
        scratch_shapes=[pltpu.VMEM((tm, IN_WIDTH), F32), _fold_scratch(tm), pltpu.VMEM(w_out_sh.shape, BF16),
                        pltpu.VMEM((N_DEV,) + w_out_sh.shape, BF16)] + GATHER_SCRATCH,
        compiler_params=_params(("arbitrary",)),
    )(x, gain, w, qkg, cos4, sin4, bmean, w_out_sh)


def _band_mask(i, max_dist):
    j = lax.broadcasted_iota(jnp.int32, (2 * BLOCK, 2 * BLOCK), 0)
    c = lax.broadcasted_iota(jnp.int32, (2 * BLOCK, 2 * BLOCK), 1)
    dist = (c & (BLOCK - 1)) + BLOCK - j
    return (dist >= 0) & (dist <= max_dist) & ((j >= BLOCK) | (i > 0))


def _stack_heads(t):
    lane = lax.broadcasted_iota(jnp.int32, t.shape, 1)
    low = lane < HEAD_DIM
    zero = jnp.zeros_like(t)
    return jnp.concatenate([jnp.where(low, t, zero), jnp.where(low, zero, t)], axis=0)


def _unstack_t(t):
    return jnp.concatenate([t[:HEAD_DIM, :BLOCK], t[HEAD_DIM:, BLOCK:]], axis=0).T


def _rows_to_pair(row):
    return jnp.concatenate([jnp.broadcast_to(row[:, :BLOCK], (HEAD_DIM, BLOCK)),
                            jnp.broadcast_to(row[:, BLOCK:], (HEAD_DIM, BLOCK))], axis=0).T


def _pair_to_rows(t):
    tt = t.T
    return jnp.concatenate([tt[0:1, :], tt[HEAD_DIM:HEAD_DIM + 1, :]], axis=1)


def _attn_fwd(name, q, k, v, sink_rows, max_dist):
    n_seq, length, _ = q.shape
    ck = k.shape[2]
    nb = length // BLOCK
    shared = ck == PAIR
    has_sinks = sink_rows is not None

    qb = FWD_BLOCKS_PER_STEP

    def body(*refs):
        if has_sinks:
            q_ref, kc_ref, vc_ref, sink_ref, o_ref, lse_ref, kp_ref, vp_ref = refs
        else:
            q_ref, kc_ref, vc_ref, o_ref, lse_ref, kp_ref, vp_ref = refs
        step = pl.program_id(0)

        @pl.when(step == 0)
        def _():
            kp_ref[...] = jnp.zeros_like(kp_ref)
            vp_ref[...] = jnp.zeros_like(vp_ref)

        valid = [_band_mask((step * qb + b) & (nb - 1), max_dist) for b in range(qb)]
        cols = [slice(p * PAIR, (p + 1) * PAIR) for p in range(N_PAIRS)]
        kcols = [slice(0, PAIR) if shared else c for c in cols]
        rows = [slice(b * BLOCK, (b + 1) * BLOCK) for b in range(qb)]
        units = [(b, p) for b in range(qb) for p in range(N_PAIRS)]
        n = range(len(units))

        def window(prev_ref, cur_ref, b, kc):
            before = prev_ref[:, kc] if b == 0 else cur_ref[rows[b - 1], kc]
            return jnp.concatenate([before, cur_ref[rows[b], kc]], axis=0)

        st = [lax.dot_general(window(kp_ref, kc_ref, b, kcols[p]), _stack_heads(q_ref[rows[b], cols[p]]),
                              (((1,), (1,)), ((), ())), preferred_element_type=F32) for b, p in units]
        st = [jnp.where(valid[units[u][0]], st[u], NEG) for u in n]
        m = [jnp.max(s, axis=0, keepdims=True) for s in st]
        if has_sinks:
            sk = [sink_ref[p:p + 1, :] for _, p in units]
            m = [jnp.maximum(m[u], sk[u]) for u in n]
        pt = [jnp.exp(st[u] - m[u]) for u in n]
        l = [jnp.sum(t, axis=0, keepdims=True) for t in pt]
        if has_sinks:
            l = [l[u] + jnp.exp(sk[u] - m[u]) for u in n]
        v2t = [window(vp_ref, vc_ref, b, kcols[p]).astype(F32).T.astype(BF16) for b, p in units]
        ot = [jnp.dot(v2t[u], pt[u].astype(BF16), preferred_element_type=F32) / l[u] for u in n]
        for u, (b, p) in enumerate(units):
            o_ref[rows[b], cols[p]] = _unstack_t(ot[u]).astype(BF16)
            lse_ref[rows[b], cols[p]] = _rows_to_pair(m[u] + jnp.log(l[u]))
        kp_ref[...] = kc_ref[rows[-1], :]
        vp_ref[...] = vc_ref[rows[-1], :]

    cur = lambda width: pl.BlockSpec((qb * BLOCK, width), lambda s: (s, 0))
    flat = lambda a: a.reshape(n_seq * length, a.shape[2])
    in_specs = [cur(HALF_WIDTH), cur(ck), cur(ck)]
    args = [flat(q), flat(k), flat(v)]
    if has_sinks:
        in_specs.append(pl.BlockSpec(sink_rows.shape, lambda s: (0, 0)))
        args.append(sink_rows)
    out = lambda dtype: jax.ShapeDtypeStruct((n_seq * length, HALF_WIDTH), dtype)
    o, lse = pl.pallas_call(
        body, name=name, grid=(n_seq * nb // qb,), in_specs=in_specs,
        out_specs=(cur(HALF_WIDTH), cur(HALF_WIDTH)), out_shape=(out(BF16), out(F32)),
        scratch_shapes=[pltpu.VMEM((BLOCK, ck), BF16), pltpu.VMEM((BLOCK, ck), BF16)],
        compiler_params=_params(("arbitrary",)),
    )(*args)
    return o.reshape(n_seq, length, HALF_WIDTH), lse.reshape(n_seq, length, HALF_WIDTH)


def _attn_bwd(name, q, k, v, d_o, lse, delta, sink_rows, max_dist):
    n_seq, length, _ = q.shape
    ck = k.shape[2]
    nb = length // BLOCK
    n_blocks = n_seq * nb
    n_rows = n_seq * length
    shared = ck == PAIR
    has_sinks = sink_rows is not None
    qb = BWD_BLOCKS_PER_STEP
    n_steps = n_blocks // qb

    def body(*refs):
        if has_sinks:
            (q_ref, kc_ref, vc_ref, do_ref, lse_ref, dl_ref, sink_ref,
             dq_ref, dk_ref, dv_ref, dsink_ref, ck_scr, cv_scr, kp_ref, vp_ref) = refs
        else:
            (q_ref, kc_ref, vc_ref, do_ref, lse_ref, dl_ref,
             dq_ref, dk_ref, dv_ref, ck_scr, cv_scr, kp_ref, vp_ref) = refs
        step = pl.program_id(0)

        @pl.when(step == 0)
        def _():
            ck_scr[...] = jnp.zeros_like(ck_scr)
            cv_scr[...] = jnp.zeros_like(cv_scr)
            kp_ref[...] = jnp.zeros_like(kp_ref)
            vp_ref[...] = jnp.zeros_like(vp_ref)
            if has_sinks:
                dsink_ref[...] = jnp.zeros_like(dsink_ref)

        valid = [_band_mask((step * qb + b) & (nb - 1), max_dist) for b in range(qb)]
        cols = [slice(p * PAIR, (p + 1) * PAIR) for p in range(N_PAIRS)]
        kcols = [slice(0, PAIR) if shared else c for c in cols]
        rows = [slice(b * BLOCK, (b + 1) * BLOCK) for b in range(qb)]
        units = [(b, p) for b in range(qb) for p in range(N_PAIRS)]
        n = range(len(units))
        nt = (((1,), (1,)), ((), ()))

        def window(prev_ref, cur_ref, b, kc):
            before = prev_ref[:, kc] if b == 0 else cur_ref[rows[b - 1], kc]
            return jnp.concatenate([before, cur_ref[rows[b], kc]], axis=0)

        q_st = [_stack_heads(q_ref[rows[b], cols[p]]) for b, p in units]
        do_st = [_stack_heads(do_ref[rows[b], cols[p]]) for b, p in units]
        k2 = [window(kp_ref, kc_ref, b, kcols[p]) for b, p in units]
        v2 = [window(vp_ref, vc_ref, b, kcols[p]) for b, p in units]
        st = [lax.dot_general(k2[u], q_st[u], nt, preferred_element_type=F32) for u in n]
        dpt = [lax.dot_general(v2[u], do_st[u], nt, preferred_element_type=F32) for u in n]
        lse_row = [_pair_to_rows(lse_ref[rows[b], cols[p]]) for b, p in units]
        dl_row = [_pair_to_rows(dl_ref[rows[b], cols[p]]) for b, p in units]
        pt = [jnp.exp(jnp.where(valid[units[u][0]], st[u], NEG) - lse_row[u]) for u in n]
        dst = [(pt[u] * (dpt[u] - dl_row[u])).astype(BF16) for u in n]
        ptb = [t.astype(BF16) for t in pt]
        dv2 = [jnp.dot(ptb[u], do_st[u], preferred_element_type=F32) for u in n]
        dk2 = [jnp.dot(dst[u], q_st[u], preferred_element_type=F32) for u in n]
        k2t = [k2[u].astype(F32).T.astype(BF16) for u in n]
        dqt = [jnp.dot(k2t[u], dst[u], preferred_element_type=F32) for u in n]
        for u, (b, p) in enumerate(units):
            dq_ref[rows[b], cols[p]] = _unstack_t(dqt[u]).astype(BF16)
        if has_sinks:
            for u, (b, p) in enumerate(units):
                p_sink = jnp.exp(sink_ref[p:p + 1, :] - lse_row[u])
                dsink_ref[p:p + 1, :] = dsink_ref[p:p + 1, :] - p_sink * dl_row[u]

        def total(parts, w, group):
            sel = [u for u, (b, p) in enumerate(units) if (shared or p == group)]
            terms = ([parts[u][:BLOCK] for u in sel if units[u][0] == w]
                     + [parts[u][BLOCK:] for u in sel if units[u][0] == w - 1])
            tot = terms[0]
            for t in terms[1:]:
                tot = tot + t
            return tot

        first_row = step * (qb * BLOCK)
        for acc_ref, out_ref, parts in ((ck_scr, dk_ref, dk2), (cv_scr, dv_ref, dv2)):
            for group in range(1 if shared else N_PAIRS):
                kc = kcols[group]

                @pl.when(step > 0)
                def _():
                    out_ref[pl.ds(pl.multiple_of(first_row - BLOCK, BLOCK), BLOCK), kc] = (
                        acc_ref[:, kc] + total(parts, 0, group)).astype(BF16)

                for w in range(1, qb):
                    out_ref[pl.ds(pl.multiple_of(first_row + (w - 1) * BLOCK, BLOCK), BLOCK), kc] = (
                        total(parts, w, group).astype(BF16))
                acc_ref[:, kc] = total(parts, qb, group)

        @pl.when(step == n_steps - 1)
        def _():
            dk_ref[pl.ds(n_rows - BLOCK, BLOCK), :] = ck_scr[...].astype(BF16)
            dv_ref[pl.ds(n_rows - BLOCK, BLOCK), :] = cv_scr[...].astype(BF16)

        kp_ref[...] = kc_ref[rows[-1], :]
        vp_ref[...] = vc_ref[rows[-1], :]

    cur = lambda width: pl.BlockSpec((qb * BLOCK, width), lambda s: (s, 0))
    whole = lambda width: pl.BlockSpec((n_rows, width), lambda s: (0, 0))
    flat = lambda a: a.reshape(n_rows, a.shape[2])
    in_specs = [cur(HALF_WIDTH), cur(ck), cur(ck), cur(HALF_WIDTH), cur(HALF_WIDTH), cur(HALF_WIDTH)]
    args = [flat(a) for a in (q, k, v, d_o, lse, delta)]
    out_specs = [cur(HALF_WIDTH), whole(ck), whole(ck)]
    out_shape = [jax.ShapeDtypeStruct((n_rows, HALF_WIDTH), BF16),
                 jax.ShapeDtypeStruct((n_rows, ck), BF16), jax.ShapeDtypeStruct((n_rows, ck), BF16)]
    if has_sinks:
        in_specs.append(pl.BlockSpec(sink_rows.shape, lambda s: (0, 0)))
        args.append(sink_rows)
        out_specs.append(pl.BlockSpec(sink_rows.shape, lambda s: (0, 0)))
        out_shape.append(jax.ShapeDtypeStruct(sink_rows.shape, F32))
    outs = pl.pallas_call(
        body, name=name, grid=(n_steps,), in_specs=in_specs,
        out_specs=tuple(out_specs), out_shape=tuple(out_shape),
        scratch_shapes=[pltpu.VMEM((BLOCK, ck), F32), pltpu.VMEM((BLOCK, ck), F32),
                        pltpu.VMEM((BLOCK, ck), BF16), pltpu.VMEM((BLOCK, ck), BF16)],
        compiler_params=_params(("arbitrary",)),
    )(*args)
    return tuple(o.reshape(n_seq, length, o.shape[1]) for o in outs[:3]) + tuple(outs[3:])


def _tail(oa, ob1, lb1, ob4, lb4, ob16, lb16, gate_a, gate_b, x, target, w_out, bones):
    tm = ROW_TILE

    def body(oa_ref, ob1_ref, lb1_ref, ob4_ref, lb4_ref, ob16_ref, lb16_ref, ga_ref, gb_ref, x_ref, t_ref, w_ref, bo_ref,
             loss_ref, dy_ref, gwo_ref, doa_ref, dla_ref, dga_ref, dgb_ref,
             dob_ref, dob4_ref, dob16_ref, dlb_ref, dlb4_ref, dlb16_ref, lse_ref, lse4_ref, lse16_ref,
             s_f):
        i = pl.program_id(0)
        o4, o16 = _unfold_load(ob4_ref, s_f, 4, tm), _unfold_load(ob16_ref, s_f, 16, tm)
        l4, l16 = _unfold_load(lb4_ref, s_f, 4, tm), _unfold_load(lb16_ref, s_f, 16, tm)
        o1, l1 = ob1_ref[...].astype(F32), lb1_ref[...]
        mx = jnp.maximum(jnp.maximum(l1, l4), l16)
        e1, e4, e16 = jnp.exp(l1 - mx), jnp.exp(l4 - mx), jnp.exp(l16 - mx)
        den = e1 + e4 + e16
        ob = (e1 * o1 + e4 * o4 + e16 * o16) / den
        lse_b = mx + jnp.log(den)

        oa, ga, gb = oa_ref[...].astype(F32), ga_ref[...], gb_ref[...]
        sa, sb = _sigmoid(ga), _sigmoid(gb)
        mixed = jnp.concatenate(_unpair_tiles(_tiles(oa * (ga * sa))) + [ob * (gb * sb)], axis=1)
        mixed_bf = mixed.astype(BF16)
        w = w_ref[...]
        yv = x_ref[...] + jnp.dot(mixed_bf, w, preferred_element_type=F32)
        err = yv - t_ref[...]
        sq = jnp.sum(err * err, axis=0, keepdims=True)
        dy = err * (1.0 / D_MODEL)
        dy_ref[...] = dy
        dy_bf = dy.astype(BF16)
        gw = jnp.dot(mixed.T.astype(BF16), dy_bf, preferred_element_type=F32)

        @pl.when(i == 0)
        def _():
            loss_ref[...] = sq
            gwo_ref[...] = gw

        @pl.when(i > 0)
        def _():
            loss_ref[...] += sq
            gwo_ref[...] += gw

        dmix = lax.dot_general(dy_bf, w, (((1,), (1,)), ((), ())), preferred_element_type=F32)
        dma = jnp.concatenate(_pair_tiles(_tiles(dmix[:, :HALF_WIDTH])), axis=1)
        dmb = dmix[:, HALF_WIDTH:]
        bo = bo_ref[...]

        def head_delta(d_o, o):
            prod = d_o * o
            return jnp.concatenate([_head_sum(prod[:, j * PAIR:(j + 1) * PAIR], bo) for j in range(N_PAIRS)], axis=1)

        doa = dma * (ga * sa)
        doa_ref[...] = doa.astype(BF16)
        dla_ref[...] = head_delta(doa, oa)
        dga_ref[...] = dma * oa * (sa * (1.0 + ga * (1.0 - sa)))
        dob = dmb * (gb * sb)
        dgb_ref[...] = dmb * ob * (sb * (1.0 + gb * (1.0 - sb)))
        dlb = head_delta(dob, ob)
        dob_ref[...] = dob.astype(BF16)
        _fold_store(dob, s_f, dob4_ref, dob16_ref, tm)
        dlb_ref[...] = dlb
        _fold_store(dlb, s_f, dlb4_ref, dlb16_ref, tm)
        lse_ref[...] = lse_b
        _fold_store(lse_b, s_f, lse4_ref, lse16_ref, tm)

    row = lambda width: pl.BlockSpec((tm, width), lambda i: (i, 0))
    full = lambda a: pl.BlockSpec(a.shape, lambda i: (0,) * a.ndim)
    fb_shapes, fb_specs = _fold_specs(tm, BF16)
    ff_shapes, ff_specs = _fold_specs(tm, F32)
    nat = lambda dtype: jax.ShapeDtypeStruct((SEQ, HALF_WIDTH), dtype)
    return pl.pallas_call(
        body, name="tail", grid=(SEQ // tm,),
        in_specs=[row(HALF_WIDTH), row(HALF_WIDTH), row(HALF_WIDTH), ff_specs[0], ff_specs[0], ff_specs[1], ff_specs[1],
                  row(HALF_WIDTH), row(HALF_WIDTH), row(D_MODEL), row(D_MODEL), full(w_out), full(bones)],
        out_specs=(pl.BlockSpec((1, D_MODEL), lambda i: (0, 0)), row(D_MODEL),
                   pl.BlockSpec((D_MODEL, D_MODEL), lambda i: (0, 0)),
                   row(HALF_WIDTH), row(HALF_WIDTH), row(HALF_WIDTH), row(HALF_WIDTH),
                   row(HALF_WIDTH), *fb_specs, row(HALF_WIDTH), *ff_specs, row(HALF_WIDTH), *ff_specs),
        out_shape=(jax.ShapeDtypeStruct((1, D_MODEL), F32), jax.ShapeDtypeStruct((SEQ, D_MODEL), F32),
                   jax.ShapeDtypeStruct((D_MODEL, D_MODEL), F32),
                   nat(BF16), nat(F32), nat(F32), nat(F32),
                   nat(BF16), *fb_shapes, nat(F32), *ff_shapes, nat(F32), *ff_shapes),
        scratch_shapes=[_fold_scratch(tm)],
        compiler_params=_params(("arbitrary",)),
    )(oa, ob1, lb1, ob4, lb4, ob16, lb16, gate_a, gate_b, x, target, w_out, bones)


def _dproj_assemble(dqa, dka, dva, dga, dgb, dq1, dk1, dv1, dq4, dk4, dv4, dq16, dk16, dv16, tqa, tqb, tkb, tka,
                    qkg, cos4, sin4, bmean):
    tm = ROW_TILE

    def norm_rope_bwd(d_out, t, g, cos, sin, bm, scale):
        d_r = d_out * scale
        dyv = d_r * cos + _swap_halves(d_r * sin)
        rr = lax.rsqrt(_head_sum(t * t, bm) + EPS)
        that = t * rr
        dgain = jnp.sum(dyv * that, axis=0, keepdims=True)
        gdy = dyv * g
        dt = rr * (gdy - that * _head_sum(that * gdy, bm))
        return dt, dgain

    def body(dqa_ref, dka_ref, dva_ref, dga_ref, dgb_ref, dq1_ref, dk1_ref, dv1_ref, dq4_ref, dk4_ref, dv4_ref,
             dq16_ref, dk16_ref, dv16_ref, tqa_ref, tqb_ref, tkb_ref, tka_ref, qkg_ref, cos_ref, sin_ref, bm_ref,
             dproj_ref, dqkg_ref, s_f):
        i = pl.program_id(0)
        cos, sin, bm = cos_ref[...], sin_ref[...], bm_ref[...]

        def merged(nat_ref, f4_ref, f16_ref):
            return nat_ref[...].astype(F32) + _unfold_load(f4_ref, s_f, 4, tm) + _unfold_load(f16_ref, s_f, 16, tm)

        @pl.when(i == 0)
        def _():
            dqkg_ref[...] = jnp.zeros_like(dqkg_ref)

        def through(d_out, t, row, scale, c0, paired=False):
            g = qkg_ref[row:row + 1, :]
            tot = jnp.zeros((1, PAIR), F32)
            dts = []
            for j in range(d_out.shape[1] // PAIR):
                cols = slice(j * PAIR, (j + 1) * PAIR)
                dt, dg = norm_rope_bwd(d_out[:, cols], t[:, cols], g, cos, sin, bm, scale)
                dts.append(dt)
                tot = tot + dg
            if paired:
                dts = _unpair_tiles(dts)
            for j, dt in enumerate(dts):
                dproj_ref[:, c0 + j * PAIR:c0 + (j + 1) * PAIR] = dt.astype(BF16)
            dqkg_ref[row:row + 1, :] += tot

        through(dqa_ref[...].astype(F32), tqa_ref[...], 0, HEAD_DIM ** -0.5, C_QA, paired=True)
        through(dka_ref[...].astype(F32), tka_ref[...], 1, 1.0, C_KA)
        through(merged(dq1_ref, dq4_ref, dq16_ref), tqb_ref[...], 2, HEAD_DIM ** -0.5, C_QB)
        through(merged(dk1_ref, dk4_ref, dk16_ref), tkb_ref[...], 3, 1.0, C_KB)
        dproj_ref[:, C_VB:C_VB + HALF_WIDTH] = merged(dv1_ref, dv4_ref, dv16_ref).astype(BF16)
        dproj_ref[:, C_GA:C_GA + HALF_WIDTH] = jnp.concatenate(_unpair_tiles(_tiles(dga_ref[...])), axis=1).astype(BF16)
        dproj_ref[:, C_GB:C_GB + HALF_WIDTH] = dgb_ref[...].astype(BF16)
        dproj_ref[:, C_VA:C_VA + KV_A_WIDTH] = dva_ref[...].astype(BF16)

    row = lambda width: pl.BlockSpec((tm, width), lambda i: (i, 0))
    full = lambda a: pl.BlockSpec(a.shape, lambda i: (0,) * a.ndim)
    _, ff_specs = _fold_specs(tm, F32)
    return pl.pallas_call(
        body, name="dproj_assemble", grid=(SEQ // tm,),
        in_specs=[row(HALF_WIDTH), row(KV_A_WIDTH), row(KV_A_WIDTH), row(HALF_WIDTH), row(HALF_WIDTH),
                  row(HALF_WIDTH), row(HALF_WIDTH), row(HALF_WIDTH), ff_specs[0], ff_specs[0], ff_specs[0],
                  ff_specs[1], ff_specs[1], ff_specs[1],
                  row(HALF_WIDTH), row(HALF_WIDTH), row(HALF_WIDTH), row(KV_A_WIDTH),
                  full(qkg), row(PAIR), row(PAIR), full(bmean)],
        out_specs=(row(IN_WIDTH), pl.BlockSpec((SMALL_ROWS, PAIR), lambda i: (0, 0))),
        out_shape=(jax.ShapeDtypeStruct((SEQ, IN_WIDTH), BF16), jax.ShapeDtypeStruct((SMALL_ROWS, PAIR), F32)),
        scratch_shapes=[_fold_scratch(tm)],
        compiler_params=_params(("arbitrary",)),
    )(dqa, dka, dva, dga, dgb, dq1, dk1, dv1, dq4, dk4, dv4, dq16, dk16, dv16, tqa, tqb, tkb, tka, qkg, cos4, sin4, bmean)


def _input_grad_reduce(dproj, w, x, gain, dy, blocks_in, blocks_out, small):
    tm = ROW_TILE
    n_steps = SEQ // tm
    stage2_step = 3
    shapes = (blocks_in.shape[1:], blocks_out.shape[1:])

    def body(dp_ref, w_ref, x_ref, g_ref, dy_ref, ga_hbm, gb_hbm, small_ref,
             gx_ref, out_a, out_b, small_out_ref, dgain_out_ref,
             part_a, part_b, sib_a, sib_b, wire_a, wire_b, chips_a, chips_b, small_all, dgain_acc, dgain_all,
             load_sems, sib_send, sib_recv, chip_send, chip_recv, small_send, small_recv, dgain_send, dgain_recv):
        i = pl.program_id(0)
        x, y, c = lax.axis_index("x"), lax.axis_index("y"), lax.axis_index("c")
        sibling = (x, y, 1 - c)
        chips = [(x, y), (1 - x, y), (x, 1 - y), (1 - x, 1 - y)]
        my_id = 4 * x + 2 * y + c
        g_hbm, part, from_sib = (ga_hbm, gb_hbm), (part_a, part_b), (sib_a, sib_b)
        to_wire, from_chips, out = (wire_a, wire_b), (chips_a, chips_b), (out_a, out_b)
        both = (0, 1)

        def blk(a, chip, core):
            return g_hbm[a].at[4 * chip[0] + 2 * chip[1] + core]

        def to_all(src, dst_all, send, recv):
            copies = []
            for rel in range(1, N_DEV):
                dx, dy_, dc = (rel >> 2) & 1, (rel >> 1) & 1, rel & 1
                to = (1 - x if dx else x, 1 - y if dy_ else y, 1 - c if dc else c)
                copies.append(pltpu.make_async_remote_copy(
                    src_ref=src, dst_ref=dst_all.at[my_id], send_sem=send.at[rel - 1], recv_sem=recv.at[rel - 1],
                    device_id=to, device_id_type=MESH))
            return copies

        small_copies = to_all(small_all.at[my_id], small_all, small_send, small_recv)
        dgain_copies = to_all(dgain_acc, dgain_all, dgain_send, dgain_recv)
        loads = [[pltpu.make_async_copy(blk(a, chips[k], c), part[a].at[k], load_sems.at[a, k]) for k in range(4)] for a in both]
        to_sib = [[pltpu.make_async_remote_copy(
            src_ref=blk(a, chips[k], 1 - c), dst_ref=from_sib[a].at[k], send_sem=sib_send.at[a, k], recv_sem=sib_recv.at[a, k],
            device_id=sibling, device_id_type=MESH) for k in range(4)] for a in both]
        to_chips = [[pltpu.make_async_remote_copy(
            src_ref=to_wire[a].at[k - 1], dst_ref=from_chips[a].at[k - 1],
            send_sem=chip_send.at[a, k - 1], recv_sem=chip_recv.at[a, k - 1],
            device_id=(*chips[k], c), device_id_type=MESH) for k in range(1, 4)] for a in both]

        @pl.when(i == 0)
        def _():
            small_all[my_id] = small_ref[...]
            for cp in small_copies:
                cp.start()
            for k in (1, 2, 3, 0):
                for a in both:
                    loads[a][k].start()
                    to_sib[a][k].start()

        @pl.when(i == stage2_step)
        def _():
            for k in (1, 2, 3):
                for a in both:
                    loads[a][k].wait()
                    to_sib[a][k].wait_recv()
                    to_wire[a][k - 1] = (part[a][k].astype(F32) + from_sib[a][k].astype(F32)).astype(BF16)
                    to_chips[a][k - 1].start()

        dh = lax.dot_general(dp_ref[...], w_ref[...], (((1,), (1,)), ((), ())), preferred_element_type=F32)
        xf = x_ref[...]
        r = lax.rsqrt(jnp.mean(xf * xf, axis=-1, keepdims=True) + EPS)
        xhat = xf * r
        dg = jnp.sum(dh * xhat, axis=0, keepdims=True)
        dxh = dh * g_ref[...]
        dx = r * (dxh - xhat * jnp.mean(dxh * xhat, axis=-1, keepdims=True))
        gx_ref[...] = dy_ref[...] + dx

        @pl.when(i == 0)
        def _():
            dgain_acc[...] = dg

        @pl.when(i > 0)
        def _():
            dgain_acc[...] += dg

        @pl.when(i == n_steps - 1)
        def _():
            dgain_all[my_id] = dgain_acc[...]
            for cp in dgain_copies:
                cp.start()
            for a in both:
                loads[a][0].wait()
                to_sib[a][0].wait_recv()
                acc = part[a][0].astype(F32) + from_sib[a][0].astype(F32)
                for k in range(3):
                    to_chips[a][k].wait_recv()
                    acc = acc + from_chips[a][k].astype(F32)
                out[a][...] = acc
            for copies, gathered, dst in ((small_copies, small_all, small_out_ref), (dgain_copies, dgain_all, dgain_out_ref)):
                for cp in copies:
                    cp.wait_recv()
                tot = gathered[0]
                for d in range(1, N_DEV):
                    tot = tot + gathered[d]
                dst[...] = tot
            for cp in to_sib[0] + to_sib[1] + to_chips[0] + to_chips[1] + small_copies + dgain_copies:
                cp.wait_send()

    row = lambda width: pl.BlockSpec((tm, width), lambda i: (i, 0))
    full = lambda a: pl.BlockSpec(a.shape, lambda i: (0,) * a.ndim)
    whole = lambda shape: pl.BlockSpec(shape, lambda i: (0,) * len(shape))
    hbm = pl.BlockSpec(memory_space=pl.ANY)
    dtypes = (blocks_in.dtype, blocks_out.dtype)
    buf = lambda n, dts: [pltpu.VMEM((n,) + s, dt) for s, dt in zip(shapes, dts)]
    return pl.pallas_call(
        body, name="input_grad_rs", grid=(n_steps,),
        in_specs=[row(IN_WIDTH), full(w), row(D_MODEL), full(gain), row(D_MODEL), hbm, hbm, full(small)],
        out_specs=(row(D_MODEL), whole(shapes[0]), whole(shapes[1]), whole((SMALL_ROWS, SMALL_COLS)), whole((1, D_MODEL))),
        out_shape=(jax.ShapeDtypeStruct((SEQ, D_MODEL), F32), jax.ShapeDtypeStruct(shapes[0], F32),
                   jax.ShapeDtypeStruct(shapes[1], F32), jax.ShapeDtypeStruct((SMALL_ROWS, SMALL_COLS), F32),
                   jax.ShapeDtypeStruct((1, D_MODEL), F32)),
        scratch_shapes=[*buf(4, dtypes), *buf(4, dtypes), *buf(3, (BF16, BF16)), *buf(3, (BF16, BF16)),
                        pltpu.VMEM((N_DEV, SMALL_ROWS, SMALL_COLS), F32),
                        pltpu.VMEM((1, D_MODEL), F32), pltpu.VMEM((N_DEV, 1, D_MODEL), F32),
                        pltpu.SemaphoreType.DMA((2, 4)), pltpu.SemaphoreType.DMA((2, 4)), pltpu.SemaphoreType.DMA((2, 4)),
                        pltpu.SemaphoreType.DMA((2, 3)), pltpu.SemaphoreType.DMA((2, 3)),
                        pltpu.SemaphoreType.DMA((7,)), pltpu.SemaphoreType.DMA((7,)),
                        pltpu.SemaphoreType.DMA((7,)), pltpu.SemaphoreType.DMA((7,))],
        compiler_params=_params(("arbitrary",)),
    )(dproj, w, x, gain, dy, blocks_in, blocks_out, small)


def _weight_grad(h_t, dproj):
    tk = 1024
    cb = IN_WIDTH // 2
    n_k = SEQ // tk

    def body(ht_ref, dp_ref, out_ref, acc):
        k = pl.program_id(1)
        upd = jnp.dot(ht_ref[...], dp_ref[...], preferred_element_type=F32)

        @pl.when(k == 0)
        def _():
            acc[...] = upd

        @pl.when(k > 0)
        def _():
            acc[...] += upd

        @pl.when(k == n_k - 1)
        def _():
            for b in range(N_DEV // 2):
                out_ref[b] = acc[:, b * SHARD_IN:(b + 1) * SHARD_IN].astype(BF16)

    return pl.pallas_call(
        body, name="weight_grad", grid=(2, n_k),
        in_specs=[pl.BlockSpec((D_MODEL, tk), lambda j, k: (0, k)), pl.BlockSpec((tk, cb), lambda j, k: (k, j))],
        out_specs=pl.BlockSpec((N_DEV // 2, D_MODEL, SHARD_IN), lambda j, k: (j, 0, 0)),
        out_shape=jax.ShapeDtypeStruct((N_DEV, D_MODEL, SHARD_IN), BF16),
        scratch_shapes=[pltpu.VMEM((D_MODEL, cb), F32)],
        compiler_params=_params(("arbitrary", "arbitrary")),
    )(h_t, dproj)


def _adamw(name, w, g, m, v):
    def body(w_ref, g_ref, m_ref, v_ref, d_ref, nm_ref, nv_ref):
        gv = g_ref[...]
        nm = ADAM_B1 * m_ref[...] + (1.0 - ADAM_B1) * gv
        nv = ADAM_B2 * v_ref[...] + (1.0 - ADAM_B2) * jnp.square(gv)
        m_hat = nm / (1.0 - ADAM_B1 ** ADAM_STEP)
        v_hat = nv / (1.0 - ADAM_B2 ** ADAM_STEP)
        d_ref[...] = -ADAM_LR * (m_hat / (jnp.sqrt(v_hat) + ADAM_EPS) + ADAM_WD * w_ref[...])
        nm_ref[...] = nm
        nv_ref[...] = nv

    vmem = pl.BlockSpec(memory_space=pltpu.VMEM)
    out = jax.ShapeDtypeStruct(w.shape, F32)
    return pl.pallas_call(
        body, name=name, in_specs=[vmem] * 4, out_specs=(vmem,) * 3, out_shape=(out,) * 3,
        compiler_params=pltpu.CompilerParams(vmem_limit_bytes=VMEM_LIMIT),
    )(w, g, m, v)


SMALL_USED = D_MODEL + 4 * HEAD_DIM + 8


def _pack_small(norm_gain, qa, ka, sinks, qb, kb, extra=None):
    parts = [norm_gain.reshape(-1), qa.reshape(-1), ka.reshape(-1), sinks.reshape(-1), qb.reshape(-1), kb.reshape(-1)]
    if extra is not None:
        parts.append(extra.reshape(-1))
    flat = jnp.concatenate(parts)
    flat = jnp.pad(flat, (0, SMALL_ROWS * SMALL_COLS - flat.shape[0]))
    return flat.reshape(SMALL_ROWS, SMALL_COLS)


def _unpack_small(a):
    flat = a.reshape(-1)
    sizes = (D_MODEL, HEAD_DIM, HEAD_DIM, 8, HEAD_DIM, HEAD_DIM)
    out, off = [], 0
    for s in sizes:
        out.append(flat[off:off + s].reshape(1, s))
        off += s
    return out


def _fold_heads(row):
    return row[0, :HEAD_DIM] + row[0, HEAD_DIM:]


def kernel(x, norm_gain, w_in, q_norm_a, k_norm_a, sinks_a, q_norm_b, k_norm_b, w_out, loss_target, m_norm_gain, m_w_in, m_q_norm_a, m_k_norm_a, m_sinks_a, m_q_norm_b, m_k_norm_b, m_w_out, v_norm_gain, v_w_in, v_q_norm_a, v_k_norm_a, v_sinks_a, v_q_norm_b, v_k_norm_b, v_w_out):
    x2, tgt = x[0], loss_target[0]
    w_in_sh, w_out_sh = w_in[0], w_out[0]

    w_full = _all_gather_w_in(w_in_sh)

    inv = np.float32(ROPE_THETA) ** (-np.arange(HEAD_DIM // 2, dtype=np.float32) / np.float32(HEAD_DIM // 2))
    ang = np.arange(SEQ, dtype=np.float32)[:, None] * inv[None, :].astype(np.float32)
    cos, sin = np.cos(ang).astype(np.float32), np.sin(ang).astype(np.float32)
    cos4 = jnp.asarray(np.concatenate([cos, cos, cos, cos], axis=1))
    sin4 = jnp.asarray(np.concatenate([-sin, sin, -sin, sin], axis=1))
    blockdiag = np.kron(np.eye(2, dtype=np.float32), np.ones((HEAD_DIM, HEAD_DIM), np.float32))
    bmean = jnp.asarray(blockdiag / HEAD_DIM, dtype=BF16)
    bones = jnp.asarray(blockdiag, dtype=BF16)
    two = lambda g: jnp.concatenate([g, g], axis=1)
    qkg = jnp.concatenate([two(q_norm_a), two(k_norm_a), two(q_norm_b), two(k_norm_b),
                           jnp.zeros((SMALL_ROWS - 4, PAIR), F32)], axis=0)
    sinks_paired = jnp.stack([sinks_a[0, :N_PAIRS], sinks_a[0, N_PAIRS:]], axis=1)
    sink_rows = jnp.concatenate([jnp.repeat(sinks_paired, BLOCK, axis=1),
                                 jnp.zeros((SMALL_ROWS - N_PAIRS, 2 * BLOCK), F32)], axis=0)

    (tqa, tka, tqb, tkb, gate_a, gate_b, h_t, qa, ka, va, qb, kb, vb, qb4, qb16, kb4, kb16, vb4, vb16,
     gathered_out) = _proj_fwd(x2, norm_gain, w_full, qkg, cos4, sin4, bmean, w_out_sh)
    wo_full = gathered_out.reshape(D_MODEL, D_MODEL)
    oa, la = _attn_fwd("attn_a_fwd", qa[None], ka[None], va[None], sink_rows, BLOCK - 1)
    ob1, lb1 = _attn_fwd("attn_b1_fwd", qb[None], kb[None], vb[None], None, BLOCK)
    ob4, lb4 = _attn_fwd("attn_b4_fwd", qb4, kb4, vb4, None, BLOCK)
    ob16, lb16 = _attn_fwd("attn_b16_fwd", qb16, kb16, vb16, None, BLOCK)
    (loss_cols, dy, gwo, doa, dla, dga, dgb, dob, dob4, dob16, dlb, dlb4, dlb16, lse_b, lse4, lse16) = _tail(
        oa[0], ob1[0], lb1[0], ob4, lb4, ob16, lb16, gate_a, gate_b, x2, tgt, wo_full, bones)

    dqa, dka, dva, dsink = _attn_bwd("attn_a_bwd", qa[None], ka[None], va[None], doa[None], la, dla[None], sink_rows, BLOCK - 1)
    dq1, dk1, dv1 = _attn_bwd("attn_b1_bwd", qb[None], kb[None], vb[None], dob[None], lse_b[None], dlb[None], None, BLOCK)
    dq4, dk4, dv4 = _attn_bwd("attn_b4_bwd", qb4, kb4, vb4, dob4, lse4, dlb4, None, BLOCK)
    dq16, dk16, dv16 = _attn_bwd("attn_b16_bwd", qb16, kb16, vb16, dob16, lse16, dlb16, None, BLOCK)
    dproj, dqkg = _dproj_assemble(dqa[0], dka[0], dva[0], dga, dgb, dq1[0], dk1[0], dv1[0], dq4, dk4, dv4,
                                  dq16, dk16, dv16, tqa, tqb, tkb, tka, qkg, cos4, sin4, bmean)
    gw_in = _weight_grad(h_t, dproj)

    blocks_in = gw_in
    blocks_out = gwo.reshape(N_DEV, SHARD_OUT, D_MODEL)
    g_sinks = jnp.concatenate([jnp.sum(dsink[:N_PAIRS, :BLOCK], axis=1), jnp.sum(dsink[:N_PAIRS, BLOCK:], axis=1)])
    small = _pack_small(jnp.zeros((D_MODEL,), F32), _fold_heads(dqkg[0:1]), _fold_heads(dqkg[1:2]), g_sinks,
                        _fold_heads(dqkg[2:3]), _fold_heads(dqkg[3:4]), extra=0.5 * jnp.sum(loss_cols) / D_MODEL)
    grad_x, g_w_in, g_w_out, small_red, dgain_red = _input_grad_reduce(
        dproj, w_full, x2, norm_gain, dy, blocks_in, blocks_out, small)
    n_gain_rows = D_MODEL // SMALL_COLS
    small_red = jnp.concatenate([dgain_red.reshape(n_gain_rows, SMALL_COLS), small_red[n_gain_rows:]], axis=0)
    g_small = _unpack_small(small_red)

    d_in, nm_in, nv_in = _adamw("adamw_w_in", w_in_sh, g_w_in, m_w_in[0], v_w_in[0])
    d_out, nm_out, nv_out = _adamw("adamw_w_out", w_out_sh, g_w_out, m_w_out[0], v_w_out[0])
    d_s, nm_s, nv_s = _adamw(
        "adamw_small",
        _pack_small(norm_gain, q_norm_a, k_norm_a, sinks_a, q_norm_b, k_norm_b), small_red,
        _pack_small(m_norm_gain, m_q_norm_a, m_k_norm_a, m_sinks_a, m_q_norm_b, m_k_norm_b),
        _pack_small(v_norm_gain, v_q_norm_a, v_k_norm_a, v_sinks_a, v_q_norm_b, v_k_norm_b))
    d_small, nm_small, nv_small = _unpack_small(d_s), _unpack_small(nm_s), _unpack_small(nv_s)

    loss = small_red.reshape(-1)[SMALL_USED]

    def assemble(small_list, big_in, big_out):
        ng, qa_, ka_, sk_, qb_, kb_ = small_list
        return [ng, big_in[None], qa_, ka_, sk_, qb_, kb_, big_out[None]]

    return (loss, grad_x[None], *assemble(g_small, g_w_in, g_w_out), *assemble(d_small, d_in, d_out),
            *assemble(nm_small, nm_in, nm_out), *assemble(nv_small, nv_in, nv_out))
```

```python
import functools

import numpy as np
import jax
import jax.numpy as jnp
from jax import lax
from jax.experimental import pallas as pl
from jax.experimental.pallas import tpu as pltpu

F32 = jnp.float32
BF16 = jnp.bfloat16

SEQ = 4096
D_MODEL = 1024
HEAD_DIM = 64
PAIR = 2 * HEAD_DIM
N_PAIRS = 4
HALF_WIDTH = N_PAIRS * PAIR
KV_A_WIDTH = 128
IN_WIDTH = 3328
BLOCK = 128
STAT_REP = 16
STAT_WIDTH = 128
EPS = 1e-6
NEG = -1e30
ROPE_THETA = 10000.0
N_DEV = 8
SHARD_IN = IN_WIDTH // N_DEV
SHARD_OUT = D_MODEL // N_DEV
PAYLOAD = SHARD_IN + SHARD_OUT
SMALL_ROWS, SMALL_COLS = 8, 256

C_QA, C_KA, C_VA, C_GA, C_QB, C_KB, C_VB, C_GB = 0, 512, 640, 768, 1280, 1792, 2304, 2816

ADAM_LR = 0.001
ADAM_B1 = 0.9
ADAM_B2 = 0.999
ADAM_EPS = 1e-08
ADAM_WD = 0.01
ADAM_STEP = 10

ROW_TILE = 256
FWD_BLOCKS_PER_STEP = 4
BWD_BLOCKS_PER_STEP = 4
VMEM_LIMIT = 56 * 1024 * 1024

MESH = pl.DeviceIdType.MESH


def _params(sem, vmem=VMEM_LIMIT):
    return pltpu.CompilerParams(dimension_semantics=sem, vmem_limit_bytes=vmem)


def _head_sum(v, bm):
    hi = v.astype(BF16)
    lo = (v - hi.astype(F32)).astype(BF16)
    return (jnp.dot(hi, bm, preferred_element_type=F32) + jnp.dot(lo, bm, preferred_element_type=F32))


def _swap_halves(y):
    lane = lax.broadcasted_iota(jnp.int32, y.shape, 1)
    first = (lane & 32) == 0
    return jnp.where(first, pltpu.roll(y, 96, 1), pltpu.roll(y, 32, 1))


def _sigmoid(g):
    return 1.0 / (1.0 + jnp.exp(-g))


def _tiles(a):
    return [a[:, j * PAIR:(j + 1) * PAIR] for j in range(N_PAIRS)]


def _pair_tiles(t):
    low = lax.broadcasted_iota(jnp.int32, t[0].shape, 1) < HEAD_DIM
    r = [pltpu.roll(a, HEAD_DIM, 1) for a in t]
    return [jnp.where(low, t[0], r[2]), jnp.where(low, r[0], t[2]), jnp.where(low, t[1], r[3]), jnp.where(low, r[1], t[3])]


def _unpair_tiles(p):
    low = lax.broadcasted_iota(jnp.int32, p[0].shape, 1) < HEAD_DIM
    r = [pltpu.roll(a, HEAD_DIM, 1) for a in p]
    return [jnp.where(low, p[0], r[1]), jnp.where(low, p[2], r[3]), jnp.where(low, r[0], p[1]), jnp.where(low, r[2], p[3])]


def _gather_plan(mine_ref, out_ref, send_sems, recv_sems):
    x, y, c = lax.axis_index("x"), lax.axis_index("y"), lax.axis_index("c")
    me, sibling = (x, y, c), (x, y, 1 - c)
    chips = [(1 - x, y), (x, 1 - y), (1 - x, 1 - y)]

    def slot(px, py, pc):
        return out_ref.at[4 * px + 2 * py + pc]

    def copy(k, block, to, from_mine=False):
        return pltpu.make_async_remote_copy(
            src_ref=mine_ref if from_mine else slot(*block), dst_ref=slot(*block),
            send_sem=send_sems.at[k], recv_sem=recv_sems.at[k], device_id=to, device_id_type=MESH)

    first = [copy(0, me, sibling, True)] + [copy(1 + j, me, (*chip, c), True) for j, chip in enumerate(chips)]
    landed = [copy(1 + j, (*chip, c), me) for j, chip in enumerate(chips)]
    passed = [copy(4 + j, (*chip, c), sibling) for j, chip in enumerate(chips)]
    from_sibling = [copy(0, sibling, me)] + [copy(4 + j, (*chip, 1 - c), me) for j, chip in enumerate(chips)]
    return slot(*me), first, landed, passed, from_sibling


GATHER_SCRATCH = [pltpu.SemaphoreType.DMA((7,)), pltpu.SemaphoreType.DMA((7,))]


def _all_gather_w_in(w_in_sh):
    rows, cols = w_in_sh.shape

    def body(w_ref, out_ref, mine_ref, blocks, send_sems, recv_sems):
        mine_ref[...] = w_ref[...].astype(BF16)
        my_slot, first, landed, passed, from_sibling = _gather_plan(mine_ref, blocks, send_sems, recv_sems)
        for cp in first:
            cp.start()
        my_slot[...] = mine_ref[...]
        for arrival, forward in zip(landed, passed):
            arrival.wait_recv()
            forward.start()
        for arrival in from_sibling:
            arrival.wait_recv()
        for cp in first + passed:
            cp.wait_send()
        for d in range(N_DEV):
            out_ref[:, d * cols:(d + 1) * cols] = blocks[d]

    vmem = pl.BlockSpec(memory_space=pltpu.VMEM)
    return pl.pallas_call(
        body, name="ag_w_in",
        out_shape=jax.ShapeDtypeStruct((rows, N_DEV * cols), BF16),
        in_specs=[vmem], out_specs=vmem,
        scratch_shapes=[pltpu.VMEM((rows, cols), BF16), pltpu.VMEM((N_DEV, rows, cols), BF16)] + GATHER_SCRATCH,
        compiler_params=pltpu.CompilerParams(vmem_limit_bytes=VMEM_LIMIT),
    )(w_in_sh)


def _reduce_scatter_grads(blocks_in, blocks_out, small):
    shapes = (blocks_in.shape[1:], blocks_out.shape[1:])

    def body(ga_hbm, gb_hbm, small_ref, out_a, out_b, small_out_ref,
             part_a, part_b, sib_a, sib_b, wire_a, wire_b, chips_a, chips_b, small_all,
             load_sems, sib_send, sib_recv, chip_send, chip_recv, small_send, small_recv):
        x, y, c = lax.axis_index("x"), lax.axis_index("y"), lax.axis_index("c")
        sibling = (x, y, 1 - c)
        chips = [(x, y), (1 - x, y), (x, 1 - y), (1 - x, 1 - y)]
        my_id = 4 * x + 2 * y + c
        g_hbm, part, from_sib = (ga_hbm, gb_hbm), (part_a, part_b), (sib_a, sib_b)
        to_wire, from_chips, out = (wire_a, wire_b), (chips_a, chips_b), (out_a, out_b)
        both = (0, 1)

        def blk(a, chip, core):
            return g_hbm[a].at[4 * chip[0] + 2 * chip[1] + core]

        small_all[my_id] = small_ref[...]
        small_copies = []
        for rel in range(1, N_DEV):
            dx, dy, dc = (rel >> 2) & 1, (rel >> 1) & 1, rel & 1
            to = (1 - x if dx else x, 1 - y if dy else y, 1 - c if dc else c)
            small_copies.append(pltpu.make_async_remote_copy(
                src_ref=small_ref, dst_ref=small_all.at[my_id],
                send_sem=small_send.at[rel - 1], recv_sem=small_recv.at[rel - 1], device_id=to, device_id_type=MESH))
        for cp in small_copies:
            cp.start()

        loads = [[pltpu.make_async_copy(blk(a, chips[k], c), part[a].at[k], load_sems.at[a, k]) for k in range(4)] for a in both]
        to_sib = [[pltpu.make_async_remote_copy(
            src_ref=blk(a, chips[k], 1 - c), dst_ref=from_sib[a].at[k], send_sem=sib_send.at[a, k], recv_sem=sib_recv.at[a, k],
            device_id=sibling, device_id_type=MESH) for k in range(4)] for a in both]
        for k in (1, 2, 3, 0):
            for a in both:
                loads[a][k].start()
                to_sib[a][k].start()

        to_chips = [[pltpu.make_async_remote_copy(
            src_ref=to_wire[a].at[k - 1], dst_ref=from_chips[a].at[k - 1],
            send_sem=chip_send.at[a, k - 1], recv_sem=chip_recv.at[a, k - 1],
            device_id=(*chips[k], c), device_id_type=MESH) for k in range(1, 4)] for a in both]
        for k in (1, 2, 3):
            for a in both:
                loads[a][k].wait()
                to_sib[a][k].wait_recv()
                to_wire[a][k - 1] = (part[a][k] + from_sib[a][k]).astype(BF16)
                to_chips[a][k - 1].start()
        for a in both:
            loads[a][0].wait()
            to_sib[a][0].wait_recv()
            acc = part[a][0] + from_sib[a][0]
            for k in range(3):
                to_chips[a][k].wait_recv()
                acc = acc + from_chips[a][k].astype(F32)
            out[a][...] = acc

        for cp in small_copies:
            cp.wait_recv()
        tot = small_all[0]
        for d in range(1, N_DEV):
            tot = tot + small_all[d]
        small_out_ref[...] = tot
        for cp in to_sib[0] + to_sib[1] + to_chips[0] + to_chips[1] + small_copies:
            cp.wait_send()

    vmem = pl.BlockSpec(memory_space=pltpu.VMEM)
    hbm = pl.BlockSpec(memory_space=pl.ANY)
    buf = lambda n, dtype: [pltpu.VMEM((n,) + s, dtype) for s in shapes]
    return pl.pallas_call(
        body, name="rs_grads",
        out_shape=(jax.ShapeDtypeStruct(shapes[0], F32), jax.ShapeDtypeStruct(shapes[1], F32),
                   jax.ShapeDtypeStruct((SMALL_ROWS, SMALL_COLS), F32)),
        in_specs=[hbm, hbm, vmem], out_specs=(vmem, vmem, vmem),
        scratch_shapes=[*buf(4, F32), *buf(4, F32), *buf(3, BF16), *buf(3, BF16),
                        pltpu.VMEM((N_DEV, SMALL_ROWS, SMALL_COLS), F32),
                        pltpu.SemaphoreType.DMA((2, 4)), pltpu.SemaphoreType.DMA((2, 4)), pltpu.SemaphoreType.DMA((2, 4)),
                        pltpu.SemaphoreType.DMA((2, 3)), pltpu.SemaphoreType.DMA((2, 3)),
                        pltpu.SemaphoreType.DMA((7,)), pltpu.SemaphoreType.DMA((7,))],
        compiler_params=pltpu.CompilerParams(vmem_limit_bytes=VMEM_LIMIT),
    )(blocks_in, blocks_out, small)


def _fold_scratch(tm):
    return pltpu.VMEM((N_PAIRS, tm, PAIR), F32)


def _fold_store(val, scr, out4, out16, tm):
    groups = range(val.shape[1] // PAIR)
    for j in groups:
        scr[j] = val[:, j * PAIR:(j + 1) * PAIR]
    for dil, out in ((4, out4), (16, out16)):
        for r in range(dil):
            for j in groups:
                out[r, :, j * PAIR:(j + 1) * PAIR] = scr[j, pl.ds(r, tm // dil, stride=dil), :].astype(out.dtype)


def _unfold_load(src, scr, dil, tm):
    groups = range(src.shape[2] // PAIR)
    for r in range(dil):
        for j in groups:
            scr[j, pl.ds(r, tm // dil, stride=dil), :] = src[r, :, j * PAIR:(j + 1) * PAIR].astype(F32)
    return jnp.concatenate([scr[j] for j in groups], axis=1)


def _fold_specs(tm, dtype, width=HALF_WIDTH):
    shapes = (jax.ShapeDtypeStruct((4, SEQ // 4, width), dtype), jax.ShapeDtypeStruct((16, SEQ // 16, width), dtype))
    specs = (pl.BlockSpec((4, tm // 4, width), lambda i: (0, i, 0)),
             pl.BlockSpec((16, tm // 16, width), lambda i: (0, i, 0)))
    return shapes, specs


def _proj_fwd(x, gain, w, qkg, cos4, sin4, bmean, w_out_sh):
    tm = ROW_TILE
    n_steps = SEQ // tm

    def norm_rope(t, g, cos, sin, bm, scale):
        rr = lax.rsqrt(_head_sum(t * t, bm) + EPS)
        yv = t * rr * g
        return (yv * cos + _swap_halves(yv) * sin) * scale

    def body(x_ref, g_ref, w_ref, qkg_ref, cos_ref, sin_ref, bm_ref, wo_ref,
             tqa_ref, tka_ref, tqb_ref, tkb_ref, ga_ref, gb_ref, ht_ref, qa_ref, ka_ref, va_ref, qb_ref, kb_ref, vb_ref,
             qb4_ref, qb16_ref, kb4_ref, kb16_ref, vb4_ref, vb16_ref, wo_all_ref,
             proj, scr, wo_mine, wo_all, send_sems, recv_sems):
        i = pl.program_id(0)
        my_slot, first, landed, passed, from_sibling = _gather_plan(wo_mine, wo_all, send_sems, recv_sems)

        @pl.when(i == 0)
        def _():
            wo_mine[...] = wo_ref[...].astype(BF16)
            for cp in first:
                cp.start()
            my_slot[...] = wo_mine[...]

        @pl.when(i == n_steps // 2)
        def _():
            for arrival, forward in zip(landed, passed):
                arrival.wait_recv()
                forward.start()

        xf = x_ref[...]
        r = lax.rsqrt(jnp.mean(xf * xf, axis=-1, keepdims=True) + EPS)
        hf = xf * r * g_ref[...]
        ht_ref[...] = hf.T.astype(BF16)
        cos, sin, bm = cos_ref[...], sin_ref[...], bm_ref[...]
        proj[...] = jnp.dot(hf.astype(BF16), w_ref[...], preferred_element_type=F32)

        def roped(tiles, row, scale):
            g = qkg_ref[row:row + 1, :]
            return jnp.concatenate([norm_rope(t, g, cos, sin, bm, scale) for t in tiles], axis=1)

        tqa = _pair_tiles(_tiles(proj[:, C_QA:C_QA + HALF_WIDTH]))
        tqa_ref[...] = jnp.concatenate(tqa, axis=1)
        qa_ref[...] = roped(tqa, 0, HEAD_DIM ** -0.5).astype(BF16)
        ga_ref[...] = jnp.concatenate(_pair_tiles(_tiles(proj[:, C_GA:C_GA + HALF_WIDTH])), axis=1)
        gb_ref[...] = proj[:, C_GB:C_GB + HALF_WIDTH]
        tqb = proj[:, C_QB:C_QB + HALF_WIDTH]
        tqb_ref[...] = tqb
        qb = roped(_tiles(tqb), 2, HEAD_DIM ** -0.5)
        qb_ref[...] = qb.astype(BF16)
        _fold_store(qb, scr, qb4_ref, qb16_ref, tm)
        tkb = proj[:, C_KB:C_KB + HALF_WIDTH]
        tkb_ref[...] = tkb
        kb = roped(_tiles(tkb), 3, 1.0)
        kb_ref[...] = kb.astype(BF16)
        _fold_store(kb, scr, kb4_ref, kb16_ref, tm)
        vb = proj[:, C_VB:C_VB + HALF_WIDTH]
        vb_ref[...] = vb.astype(BF16)
        _fold_store(vb, scr, vb4_ref, vb16_ref, tm)
        tka = proj[:, C_KA:C_KA + KV_A_WIDTH]
        tka_ref[...] = tka
        ka_ref[...] = roped([tka], 1, 1.0).astype(BF16)
        va_ref[...] = proj[:, C_VA:C_VA + KV_A_WIDTH].astype(BF16)

        @pl.when(i == n_steps - 1)
        def _():
            for arrival in from_sibling:
                arrival.wait_recv()
            for cp in first + passed:
                cp.wait_send()
            wo_all_ref[...] = wo_all[...]

    row = lambda width: pl.BlockSpec((tm, width), lambda i: (i, 0))
    full = lambda a: pl.BlockSpec(a.shape, lambda i: (0,) * a.ndim)
    nat = lambda width, dtype=BF16: jax.ShapeDtypeStruct((SEQ, width), dtype)
    f_shapes, f_specs = _fold_specs(tm, BF16)
    return pl.pallas_call(
        body, name="proj_fwd", grid=(SEQ // tm,),
        in_specs=[row(D_MODEL), full(gain), full(w), full(qkg), row(PAIR), row(PAIR), full(bmean), full(w_out_sh)],
        out_specs=(row(HALF_WIDTH), row(KV_A_WIDTH), row(HALF_WIDTH), row(HALF_WIDTH), row(HALF_WIDTH), row(HALF_WIDTH),
                   pl.BlockSpec((D_MODEL, tm), lambda i: (0, i)),
                   row(HALF_WIDTH), row(KV_A_WIDTH), row(KV_A_WIDTH), row(HALF_WIDTH), row(HALF_WIDTH), row(HALF_WIDTH),
                   *f_specs, *f_specs, *f_specs,
                   pl.BlockSpec((N_DEV,) + w_out_sh.shape, lambda i: (0, 0, 0))),
        out_shape=(nat(HALF_WIDTH, F32), nat(KV_A_WIDTH, F32), nat(HALF_WIDTH, F32), nat(HALF_WIDTH, F32),
                   nat(HALF_WIDTH, F32), nat(HALF_WIDTH, F32),
                   jax.ShapeDtypeStruct((D_MODEL, SEQ), BF16),
                   nat(HALF_WIDTH), nat(KV_A_WIDTH), nat(KV_A_WIDTH), nat(HALF_WIDTH), nat(HALF_WIDTH), nat(HALF_WIDTH),
                   *f_shapes, *f_shapes, *f_shapes,
                   jax.ShapeDtypeStruct((N_DEV,) + w_out_sh.shape, BF16)),
        scratch_shapes=[pltpu.VMEM((tm, IN_WIDTH), F32), _fold_scratch(tm), pltpu.VMEM(w_out_sh.shape, BF16),
                        pltpu.VMEM((N_DEV,) + w_out_sh.shape, BF16)] + GATHER_SCRATCH,
        compiler_params=_params(("arbitrary",)),
    )(x, gain, w, qkg, cos4, sin4, bmean, w_out_sh)


def _band_mask(i, max_dist):
    j = lax.broadcasted_iota(jnp.int32, (2 * BLOCK, 2 * BLOCK), 0)
    c = lax.broadcasted_iota(jnp.int32, (2 * BLOCK, 2 * BLOCK), 1)
    dist = (c & (BLOCK - 1)) + BLOCK - j
    return (dist >= 0) & (dist <= max_dist) & ((j >= BLOCK) | (i > 0))


def _stack_heads(t):
    lane = lax.broadcasted_iota(jnp.int32, t.shape, 1)
    low = lane < HEAD_DIM
    zero = jnp.zeros_like(t)
    return jnp.concatenate([jnp.where(low, t, zero), jnp.where(low, zero, t)], axis=0)


def _unstack_t(t):
    return jnp.concatenate([t[:HEAD_DIM, :BLOCK], t[HEAD_DIM:, BLOCK:]], axis=0).T


def _rows_to_stats(rows):
    parts = []
    for row in rows:
        parts.append(jnp.broadcast_to(row[:, :BLOCK], (STAT_REP, BLOCK)))
        parts.append(jnp.broadcast_to(row[:, BLOCK:], (STAT_REP, BLOCK)))
    return jnp.concatenate(parts, axis=0).T


def _stats_to_rows(t):
    tt = t.T
    return [jnp.concatenate([tt[2 * p * STAT_REP:2 * p * STAT_REP + 1, :],
                             tt[(2 * p + 1) * STAT_REP:(2 * p + 1) * STAT_REP + 1, :]], axis=1) for p in range(N_PAIRS)]


def _attn_fwd(name, q, k, v, sink_rows, max_dist):
    n_seq, length, _ = q.shape
    ck = k.shape[2]
    nb = length // BLOCK
    shared = ck == PAIR
    has_sinks = sink_rows is not None

    qb = FWD_BLOCKS_PER_STEP

    def body(*refs):
        if has_sinks:
            q_ref, kc_ref, vc_ref, sink_ref, o_ref, lse_ref, kp_ref, vp_ref = refs
        else:
            q_ref, kc_ref, vc_ref, o_ref, lse_ref, kp_ref, vp_ref = refs
        step = pl.program_id(0)

        @pl.when(step == 0)
        def _():
            kp_ref[...] = jnp.zeros_like(kp_ref)
            vp_ref[...] = jnp.zeros_like(vp_ref)

        valid = [_band_mask((step * qb + b) & (nb - 1), max_dist) for b in range(qb)]
        cols = [slice(p * PAIR, (p + 1) * PAIR) for p in range(N_PAIRS)]
        kcols = [slice(0, PAIR) if shared else c for c in cols]
        rows = [slice(b * BLOCK, (b + 1) * BLOCK) for b in range(qb)]
        units = [(b, p) for b in range(qb) for p in range(N_PAIRS)]
        n = range(len(units))

        def window(prev_ref, cur_ref, b, kc):
            before = prev_ref[:, kc] if b == 0 else cur_ref[rows[b - 1], kc]
            return jnp.concatenate([before, cur_ref[rows[b], kc]], axis=0)

        st = [lax.dot_general(window(kp_ref, kc_ref, b, kcols[p]), _stack_heads(q_ref[rows[b], cols[p]]),
                              (((1,), (1,)), ((), ())), preferred_element_type=F32) for b, p in units]
        st = [jnp.where(valid[units[u][0]], st[u], NEG) for u in n]
        m = [jnp.max(s, axis=0, keepdims=True) for s in st]
        if has_sinks:
            sk = [sink_ref[p:p + 1, :] for _, p in units]
            m = [jnp.maximum(m[u], sk[u]) for u in n]
        pt = [jnp.exp(st[u] - m[u]) for u in n]
        l = [jnp.sum(t, axis=0, keepdims=True) for t in pt]
        if has_sinks:
            l = [l[u] + jnp.exp(sk[u] - m[u]) for u in n]
        v2t = [window(vp_ref, vc_ref, b, kcols[p]).astype(F32).T.astype(BF16) for b, p in units]
        ot = [jnp.dot(v2t[u], pt[u].astype(BF16), preferred_element_type=F32) / l[u] for u in n]
        for u, (b, p) in enumerate(units):
            o_ref[rows[b], cols[p]] = _unstack_t(ot[u]).astype(BF16)
        for b in range(qb):
            lse_ref[rows[b], :] = _rows_to_stats([m[u] + jnp.log(l[u]) for u in n if units[u][0] == b])
        kp_ref[...] = kc_ref[rows[-1], :]
        vp_ref[...] = vc_ref[rows[-1], :]

    cur = lambda width: pl.BlockSpec((qb * BLOCK, width), lambda s: (s, 0))
    flat = lambda a: a.reshape(n_seq * length, a.shape[2])
    in_specs = [cur(HALF_WIDTH), cur(ck), cur(ck)]
    args = [flat(q), flat(k), flat(v)]
    if has_sinks:
        in_specs.append(pl.BlockSpec(sink_rows.shape, lambda s: (0, 0)))
        args.append(sink_rows)
    o, lse = pl.pallas_call(
        body, name=name, grid=(n_seq * nb // qb,), in_specs=in_specs,
        out_specs=(cur(HALF_WIDTH), cur(STAT_WIDTH)),
        out_shape=(jax.ShapeDtypeStruct((n_seq * length, HALF_WIDTH), BF16),
                   jax.ShapeDtypeStruct((n_seq * length, STAT_WIDTH), F32)),
        scratch_shapes=[pltpu.VMEM((BLOCK, ck), BF16), pltpu.VMEM((BLOCK, ck), BF16)],
        compiler_params=_params(("arbitrary",)),
    )(*args)
    return o.reshape(n_seq, length, HALF_WIDTH), lse.reshape(n_seq, length, STAT_WIDTH)


def _attn_bwd(name, q, k, v, d_o, lse, delta, sink_rows, max_dist):
    n_seq, length, _ = q.shape
    ck = k.shape[2]
    nb = length // BLOCK
    n_blocks = n_seq * nb
    n_rows = n_seq * length
    shared = ck == PAIR
    has_sinks = sink_rows is not None
    qb = BWD_BLOCKS_PER_STEP
    n_steps = n_blocks // qb

    def body(*refs):
        if has_sinks:
            (q_ref, kc_ref, vc_ref, do_ref, lse_ref, dl_ref, sink_ref,
             dq_ref, dk_ref, dv_ref, dsink_ref, ck_scr, cv_scr, kp_ref, vp_ref) = refs
        else:
            (q_ref, kc_ref, vc_ref, do_ref, lse_ref, dl_ref,
             dq_ref, dk_ref, dv_ref, ck_scr, cv_scr, kp_ref, vp_ref) = refs
        step = pl.program_id(0)

        @pl.when(step == 0)
        def _():
            ck_scr[...] = jnp.zeros_like(ck_scr)
            cv_scr[...] = jnp.zeros_like(cv_scr)
            kp_ref[...] = jnp.zeros_like(kp_ref)
            vp_ref[...] = jnp.zeros_like(vp_ref)
            if has_sinks:
                dsink_ref[...] = jnp.zeros_like(dsink_ref)

        valid = [_band_mask((step * qb + b) & (nb - 1), max_dist) for b in range(qb)]
        cols = [slice(p * PAIR, (p + 1) * PAIR) for p in range(N_PAIRS)]
        kcols = [slice(0, PAIR) if shared else c for c in cols]
        rows = [slice(b * BLOCK, (b + 1) * BLOCK) for b in range(qb)]
        units = [(b, p) for b in range(qb) for p in range(N_PAIRS)]
        n = range(len(units))
        nt = (((1,), (1,)), ((), ()))

        def window(prev_ref, cur_ref, b, kc):
            before = prev_ref[:, kc] if b == 0 else cur_ref[rows[b - 1], kc]
            return jnp.concatenate([before, cur_ref[rows[b], kc]], axis=0)

        q_st = [_stack_heads(q_ref[rows[b], cols[p]]) for b, p in units]
        do_st = [_stack_heads(do_ref[rows[b], cols[p]]) for b, p in units]
        k2 = [window(kp_ref, kc_ref, b, kcols[p]) for b, p in units]
        v2 = [window(vp_ref, vc_ref, b, kcols[p]) for b, p in units]
        st = [lax.dot_general(k2[u], q_st[u], nt, preferred_element_type=F32) for u in n]
        dpt = [lax.dot_general(v2[u], do_st[u], nt, preferred_element_type=F32) for u in n]
        lse_rows = [_stats_to_rows(lse_ref[rows[b], :]) for b in range(qb)]
        dl_rows = [_stats_to_rows(dl_ref[rows[b], :]) for b in range(qb)]
        lse_row = [lse_rows[b][p] for b, p in units]
        dl_row = [dl_rows[b][p] for b, p in units]
        pt = [jnp.exp(jnp.where(valid[units[u][0]], st[u], NEG) - lse_row[u]) for u in n]
        dst = [(pt[u] * (dpt[u] - dl_row[u])).astype(BF16) for u in n]
        ptb = [t.astype(BF16) for t in pt]
        dv2 = [jnp.dot(ptb[u], do_st[u], preferred_element_type=F32) for u in n]
        dk2 = [jnp.dot(dst[u], q_st[u], preferred_element_type=F32) for u in n]
        k2t = [k2[u].astype(F32).T.astype(BF16) for u in n]
        dqt = [jnp.dot(k2t[u], dst[u], preferred_element_type=F32) for u in n]
        for u, (b, p) in enumerate(units):
            dq_ref[rows[b], cols[p]] = _unstack_t(dqt[u]).astype(BF16)
        if has_sinks:
            for u, (b, p) in enumerate(units):
                p_sink = jnp.exp(sink_ref[p:p + 1, :] - lse_row[u])
                dsink_ref[p:p + 1, :] = dsink_ref[p:p + 1, :] - p_sink * dl_row[u]

        def total(parts, w, group):
            sel = [u for u, (b, p) in enumerate(units) if (shared or p == group)]
            terms = ([parts[u][:BLOCK] for u in sel if units[u][0] == w]
                     + [parts[u][BLOCK:] for u in sel if units[u][0] == w - 1])
            tot = terms[0]
            for t in terms[1:]:
                tot = tot + t
            return tot

        first_row = step * (qb * BLOCK)
        for acc_ref, out_ref, parts in ((ck_scr, dk_ref, dk2), (cv_scr, dv_ref, dv2)):
            for group in range(1 if shared else N_PAIRS):
                kc = kcols[group]

                @pl.when(step > 0)
                def _():
                    out_ref[pl.ds(pl.multiple_of(first_row - BLOCK, BLOCK), BLOCK), kc] = (
                        acc_ref[:, kc] + total(parts, 0, group)).astype(BF16)

                for w in range(1, qb):
                    out_ref[pl.ds(pl.multiple_of(first_row + (w - 1) * BLOCK, BLOCK), BLOCK), kc] = (
                        total(parts, w, group).astype(BF16))
                acc_ref[:, kc] = total(parts, qb, group)

        @pl.when(step == n_steps - 1)
        def _():
            dk_ref[pl.ds(n_rows - BLOCK, BLOCK), :] = ck_scr[...].astype(BF16)
            dv_ref[pl.ds(n_rows - BLOCK, BLOCK), :] = cv_scr[...].astype(BF16)

        kp_ref[...] = kc_ref[rows[-1], :]
        vp_ref[...] = vc_ref[rows[-1], :]

    cur = lambda width: pl.BlockSpec((qb * BLOCK, width), lambda s: (s, 0))
    whole = lambda width: pl.BlockSpec((n_rows, width), lambda s: (0, 0))
    flat = lambda a: a.reshape(n_rows, a.shape[2])
    in_specs = [cur(HALF_WIDTH), cur(ck), cur(ck), cur(HALF_WIDTH), cur(STAT_WIDTH), cur(STAT_WIDTH)]
    args = [flat(a) for a in (q, k, v, d_o, lse, delta)]
    out_specs = [cur(HALF_WIDTH), whole(ck), whole(ck)]
    out_shape = [jax.ShapeDtypeStruct((n_rows, HALF_WIDTH), BF16),
                 jax.ShapeDtypeStruct((n_rows, ck), BF16), jax.ShapeDtypeStruct((n_rows, ck), BF16)]
    if has_sinks:
        in_specs.append(pl.BlockSpec(sink_rows.shape, lambda s: (0, 0)))
        args.append(sink_rows)
        out_specs.append(pl.BlockSpec(sink_rows.shape, lambda s: (0, 0)))
        out_shape.append(jax.ShapeDtypeStruct(sink_rows.shape, F32))
    outs = pl.pallas_call(
        body, name=name, grid=(n_steps,), in_specs=in_specs,
        out_specs=tuple(out_specs), out_shape=tuple(out_shape),
        scratch_shapes=[pltpu.VMEM((BLOCK, ck), F32), pltpu.VMEM((BLOCK, ck), F32),
                        pltpu.VMEM((BLOCK, ck), BF16), pltpu.VMEM((BLOCK, ck), BF16)],
        compiler_params=_params(("arbitrary",)),
    )(*args)
    return tuple(o.reshape(n_seq, length, o.shape[1]) for o in outs[:3]) + tuple(outs[3:])


def _tail(oa, ob1, lb1, ob4, lb4, ob16, lb16, gate_a, gate_b, x, target, w_out, spread, gather):
    tm = ROW_TILE

    def split_dot(v, mat):
        hi = v.astype(BF16)
        lo = (v - hi.astype(F32)).astype(BF16)
        return jnp.dot(hi, mat, preferred_element_type=F32) + jnp.dot(lo, mat, preferred_element_type=F32)

    def body(oa_ref, ob1_ref, lb1_ref, ob4_ref, lb4_ref, ob16_ref, lb16_ref, ga_ref, gb_ref, x_ref, t_ref, w_ref,
             sp_ref, ga_mat_ref,
             loss_ref, dy_ref, gwo_ref, doa_ref, dla_ref, dga_ref, dgb_ref,
             dob_ref, dob4_ref, dob16_ref, dlb_ref, dlb4_ref, dlb16_ref, lse_ref, lse4_ref, lse16_ref,
             s_f, mix_keep, dy_keep):
        i = pl.program_id(0)
        sp, gat = sp_ref[...], ga_mat_ref[...]
        o4, o16 = _unfold_load(ob4_ref, s_f, 4, tm), _unfold_load(ob16_ref, s_f, 16, tm)
        l4, l16 = _unfold_load(lb4_ref, s_f, 4, tm), _unfold_load(lb16_ref, s_f, 16, tm)
        o1, l1 = ob1_ref[...].astype(F32), lb1_ref[...]
        mx = jnp.maximum(jnp.maximum(l1, l4), l16)
        e1, e4, e16 = jnp.exp(l1 - mx), jnp.exp(l4 - mx), jnp.exp(l16 - mx)
        den = e1 + e4 + e16
        inv = 1.0 / den
        ob = split_dot(e1 * inv, sp) * o1 + split_dot(e4 * inv, sp) * o4 + split_dot(e16 * inv, sp) * o16
        lse_b = mx + jnp.log(den)

        oa, ga, gb = oa_ref[...].astype(F32), ga_ref[...], gb_ref[...]
        sa, sb = _sigmoid(ga), _sigmoid(gb)
        mixed = jnp.concatenate(_unpair_tiles(_tiles(oa * (ga * sa))) + [ob * (gb * sb)], axis=1)
        mixed_bf = mixed.astype(BF16)
        w = w_ref[...]
        yv = x_ref[...] + jnp.dot(mixed_bf, w, preferred_element_type=F32)
        err = yv - t_ref[...]
        sq = jnp.sum(err * err, axis=0, keepdims=True)
        dy = err * (1.0 / D_MODEL)
        dy_ref[...] = dy
        dy_bf = dy.astype(BF16)
        mix_t = mixed.T.astype(BF16)

        @pl.when(i == 0)
        def _():
            loss_ref[...] = sq

        @pl.when(i > 0)
        def _():
            loss_ref[...] += sq

        @pl.when((i & 1) == 0)
        def _():
            mix_keep[...] = mix_t
            dy_keep[...] = dy_bf

        @pl.when((i & 1) == 1)
        def _():
            gw = jnp.dot(jnp.concatenate([mix_keep[...], mix_t], axis=1), jnp.concatenate([dy_keep[...], dy_bf], axis=0),
                         preferred_element_type=F32)

            @pl.when(i == 1)
            def _():
                gwo_ref[...] = gw

            @pl.when(i > 1)
            def _():
                gwo_ref[...] += gw

        dmix = lax.dot_general(dy_bf, w, (((1,), (1,)), ((), ())), preferred_element_type=F32)
        dma = jnp.concatenate(_pair_tiles(_tiles(dmix[:, :HALF_WIDTH])), axis=1)
        dmb = dmix[:, HALF_WIDTH:]

        doa = dma * (ga * sa)
        doa_ref[...] = doa.astype(BF16)
        dla_ref[...] = split_dot(doa * oa, gat)
        dga_ref[...] = (dma * oa * (sa * (1.0 + ga * (1.0 - sa)))).astype(BF16)
        dob = dmb * (gb * sb)
        dgb_ref[...] = (dmb * ob * (sb * (1.0 + gb * (1.0 - sb)))).astype(BF16)
        dlb = split_dot(dob * ob, gat)
        dob_ref[...] = dob.astype(BF16)
        _fold_store(dob, s_f, dob4_ref, dob16_ref, tm)
        dlb_ref[...] = dlb
        _fold_store(dlb, s_f, dlb4_ref, dlb16_ref, tm)
        lse_ref[...] = lse_b
        _fold_store(lse_b, s_f, lse4_ref, lse16_ref, tm)

    row = lambda width: pl.BlockSpec((tm, width), lambda i: (i, 0))
    full = lambda a: pl.BlockSpec(a.shape, lambda i: (0,) * a.ndim)
    fb_shapes, fb_specs = _fold_specs(tm, BF16)
    _, ff_specs = _fold_specs(tm, F32)
    st_shapes, st_specs = _fold_specs(tm, F32, STAT_WIDTH)
    nat = lambda dtype, width=HALF_WIDTH: jax.ShapeDtypeStruct((SEQ, width), dtype)
    return pl.pallas_call(
        body, name="tail", grid=(SEQ // tm,),
        in_specs=[row(HALF_WIDTH), row(HALF_WIDTH), row(STAT_WIDTH), ff_specs[0], st_specs[0], ff_specs[1], st_specs[1],
                  row(HALF_WIDTH), row(HALF_WIDTH), row(D_MODEL), row(D_MODEL), full(w_out), full(spread), full(gather)],
        out_specs=(pl.BlockSpec((1, D_MODEL), lambda i: (0, 0)), row(D_MODEL),
                   pl.BlockSpec((D_MODEL, D_MODEL), lambda i: (0, 0)),
                   row(HALF_WIDTH), row(STAT_WIDTH), row(HALF_WIDTH), row(HALF_WIDTH),
                   row(HALF_WIDTH), *fb_specs, row(STAT_WIDTH), *st_specs, row(STAT_WIDTH), *st_specs),
        out_shape=(jax.ShapeDtypeStruct((1, D_MODEL), F32), jax.ShapeDtypeStruct((SEQ, D_MODEL), F32),
                   jax.ShapeDtypeStruct((D_MODEL, D_MODEL), F32),
                   nat(BF16), nat(F32, STAT_WIDTH), nat(BF16), nat(BF16),
                   nat(BF16), *fb_shapes, nat(F32, STAT_WIDTH), *st_shapes, nat(F32, STAT_WIDTH), *st_shapes),
        scratch_shapes=[_fold_scratch(tm), pltpu.VMEM((D_MODEL, tm), BF16), pltpu.VMEM((tm, D_MODEL), BF16)],
        compiler_params=_params(("arbitrary",)),
    )(oa, ob1, lb1, ob4, lb4, ob16, lb16, gate_a, gate_b, x, target, w_out, spread, gather)


def _dproj_assemble(dqa, dka, dva, dga, dgb, dq1, dk1, dv1, dq4, dk4, dv4, dq16, dk16, dv16, tqa, tqb, tkb, tka,
                    qkg, cos4, sin4, bmean):
    tm = ROW_TILE

    def norm_rope_bwd(d_out, t, g, cos, sin, bm, scale):
        d_r = d_out * scale
        dyv = d_r * cos + _swap_halves(d_r * sin)
        rr = lax.rsqrt(_head_sum(t * t, bm) + EPS)
        that = t * rr
        dgain = jnp.sum(dyv * that, axis=0, keepdims=True)
        gdy = dyv * g
        dt = rr * (gdy - that * _head_sum(that * gdy, bm))
        return dt, dgain

    def body(dqa_ref, dka_ref, dva_ref, dga_ref, dgb_ref, dq1_ref, dk1_ref, dv1_ref, dq4_ref, dk4_ref, dv4_ref,
             dq16_ref, dk16_ref, dv16_ref, tqa_ref, tqb_ref, tkb_ref, tka_ref, qkg_ref, cos_ref, sin_ref, bm_ref,
             dproj_ref, dqkg_ref, s_f):
        i = pl.program_id(0)
        cos, sin, bm = cos_ref[...], sin_ref[...], bm_ref[...]

        def merged(nat_ref, f4_ref, f16_ref):
            return nat_ref[...].astype(F32) + _unfold_load(f4_ref, s_f, 4, tm) + _unfold_load(f16_ref, s_f, 16, tm)

        @pl.when(i == 0)
        def _():
            dqkg_ref[...] = jnp.zeros_like(dqkg_ref)

        def through(d_out, t, row, scale, c0, paired=False):
            g = qkg_ref[row:row + 1, :]
            tot = jnp.zeros((1, PAIR), F32)
            dts = []
            for j in range(d_out.shape[1] // PAIR):
                cols = slice(j * PAIR, (j + 1) * PAIR)
                dt, dg = norm_rope_bwd(d_out[:, cols], t[:, cols], g, cos, sin, bm, scale)
                dts.append(dt)
                tot = tot + dg
            if paired:
                dts = _unpair_tiles(dts)
            for j, dt in enumerate(dts):
                dproj_ref[:, c0 + j * PAIR:c0 + (j + 1) * PAIR] = dt.astype(BF16)
            dqkg_ref[row:row + 1, :] += tot

        through(dqa_ref[...].astype(F32), tqa_ref[...], 0, HEAD_DIM ** -0.5, C_QA, paired=True)
        through(dka_ref[...].astype(F32), tka_ref[...], 1, 1.0, C_KA)
        through(merged(dq1_ref, dq4_ref, dq16_ref), tqb_ref[...], 2, HEAD_DIM ** -0.5, C_QB)
        through(merged(dk1_ref, dk4_ref, dk16_ref), tkb_ref[...], 3, 1.0, C_KB)
        dproj_ref[:, C_VB:C_VB + HALF_WIDTH] = merged(dv1_ref, dv4_ref, dv16_ref).astype(BF16)
        dproj_ref[:, C_GA:C_GA + HALF_WIDTH] = jnp.concatenate(
            _unpair_tiles(_tiles(dga_ref[...].astype(F32))), axis=1).astype(BF16)
        dproj_ref[:, C_GB:C_GB + HALF_WIDTH] = dgb_ref[...].astype(BF16)
        dproj_ref[:, C_VA:C_VA + KV_A_WIDTH] = dva_ref[...].astype(BF16)

    row = lambda width: pl.BlockSpec((tm, width), lambda i: (i, 0))
    full = lambda a: pl.BlockSpec(a.shape, lambda i: (0,) * a.ndim)
    _, ff_specs = _fold_specs(tm, F32)
    return pl.pallas_call(
        body, name="dproj_assemble", grid=(SEQ // tm,),
        in_specs=[row(HALF_WIDTH), row(KV_A_WIDTH), row(KV_A_WIDTH), row(HALF_WIDTH), row(HALF_WIDTH),
                  row(HALF_WIDTH), row(HALF_WIDTH), row(HALF_WIDTH), ff_specs[0], ff_specs[0], ff_specs[0],
                  ff_specs[1], ff_specs[1], ff_specs[1],
                  row(HALF_WIDTH), row(HALF_WIDTH), row(HALF_WIDTH), row(KV_A_WIDTH),
                  full(qkg), row(PAIR), row(PAIR), full(bmean)],
        out_specs=(row(IN_WIDTH), pl.BlockSpec((SMALL_ROWS, PAIR), lambda i: (0, 0))),
        out_shape=(jax.ShapeDtypeStruct((SEQ, IN_WIDTH), BF16), jax.ShapeDtypeStruct((SMALL_ROWS, PAIR), F32)),
        scratch_shapes=[_fold_scratch(tm)],
        compiler_params=_params(("arbitrary",)),
    )(dqa, dka, dva, dga, dgb, dq1, dk1, dv1, dq4, dk4, dv4, dq16, dk16, dv16, tqa, tqb, tkb, tka, qkg, cos4, sin4, bmean)


def _input_grad_reduce(dproj, w, x, gain, dy, blocks_in, blocks_out, small):
    tm = ROW_TILE
    n_steps = SEQ // tm
    stage2_step = 3
    shapes = (blocks_in.shape[1:], blocks_out.shape[1:])

    def body(dp_ref, w_ref, x_ref, g_ref, dy_ref, ga_hbm, gb_hbm, small_ref,
             gx_ref, out_a, out_b, small_out_ref, dgain_out_ref,
             part_a, part_b, sib_a, sib_b, wire_a, wire_b, chips_a, chips_b, small_all, dgain_acc, dgain_all,
             load_sems, sib_send, sib_recv, chip_send, chip_recv, small_send, small_recv, dgain_send, dgain_recv):
        i = pl.program_id(0)
        x, y, c = lax.axis_index("x"), lax.axis_index("y"), lax.axis_index("c")
        sibling = (x, y, 1 - c)
        chips = [(x, y), (1 - x, y), (x, 1 - y), (1 - x, 1 - y)]
        my_id = 4 * x + 2 * y + c
        g_hbm, part, from_sib = (ga_hbm, gb_hbm), (part_a, part_b), (sib_a, sib_b)
        to_wire, from_chips, out = (wire_a, wire_b), (chips_a, chips_b), (out_a, out_b)
        both = (0, 1)

        def blk(a, chip, core):
            return g_hbm[a].at[4 * chip[0] + 2 * chip[1] + core]

        def to_all(src, dst_all, send, recv):
            copies = []
            for rel in range(1, N_DEV):
                dx, dy_, dc = (rel >> 2) & 1, (rel >> 1) & 1, rel & 1
                to = (1 - x if dx else x, 1 - y if dy_ else y, 1 - c if dc else c)
                copies.append(pltpu.make_async_remote_copy(
                    src_ref=src, dst_ref=dst_all.at[my_id], send_sem=send.at[rel - 1], recv_sem=recv.at[rel - 1],
                    device_id=to, device_id_type=MESH))
            return copies

        small_copies = to_all(small_all.at[my_id], small_all, small_send, small_recv)
        dgain_copies = to_all(dgain_acc, dgain_all, dgain_send, dgain_recv)
        loads = [[pltpu.make_async_copy(blk(a, chips[k], c), part[a].at[k], load_sems.at[a, k]) for k in range(4)] for a in both]
        to_sib = [[pltpu.make_async_remote_copy(
            src_ref=blk(a, chips[k], 1 - c), dst_ref=from_sib[a].at[k], send_sem=sib_send.at[a, k], recv_sem=sib_recv.at[a, k],
            device_id=sibling, device_id_type=MESH) for k in range(4)] for a in both]
        to_chips = [[pltpu.make_async_remote_copy(
            src_ref=to_wire[a].at[k - 1], dst_ref=from_chips[a].at[k - 1],
            send_sem=chip_send.at[a, k - 1], recv_sem=chip_recv.at[a, k - 1],
            device_id=(*chips[k], c), device_id_type=MESH) for k in range(1, 4)] for a in both]

        @pl.when(i == 0)
        def _():
            small_all[my_id] = small_ref[...]
            for cp in small_copies:
                cp.start()
            for k in (1, 2, 3, 0):
                for a in both:
                    loads[a][k].start()
                    to_sib[a][k].start()

        @pl.when(i == stage2_step)
        def _():
            for k in (1, 2, 3):
                for a in both:
                    loads[a][k].wait()
                    to_sib[a][k].wait_recv()
                    to_wire[a][k - 1] = (part[a][k].astype(F32) + from_sib[a][k].astype(F32)).astype(BF16)
                    to_chips[a][k - 1].start()

        dh = lax.dot_general(dp_ref[...], w_ref[...], (((1,), (1,)), ((), ())), preferred_element_type=F32)
        xf = x_ref[...]
        r = lax.rsqrt(jnp.mean(xf * xf, axis=-1, keepdims=True) + EPS)
        xhat = xf * r
        dg = jnp.sum(dh * xhat, axis=0, keepdims=True)
        dxh = dh * g_ref[...]
        dx = r * (dxh - xhat * jnp.mean(dxh * xhat, axis=-1, keepdims=True))
        gx_ref[...] = dy_ref[...] + dx

        @pl.when(i == 0)
        def _():
            dgain_acc[...] = dg

        @pl.when(i > 0)
        def _():
            dgain_acc[...] += dg

        @pl.when(i == n_steps - 1)
        def _():
            dgain_all[my_id] = dgain_acc[...]
            for cp in dgain_copies:
                cp.start()
            for a in both:
                loads[a][0].wait()
                to_sib[a][0].wait_recv()
                acc = part[a][0].astype(F32) + from_sib[a][0].astype(F32)
                for k in range(3):
                    to_chips[a][k].wait_recv()
                    acc = acc + from_chips[a][k].astype(F32)
                out[a][...] = acc
            for copies, gathered, dst in ((small_copies, small_all, small_out_ref), (dgain_copies, dgain_all, dgain_out_ref)):
                for cp in copies:
                    cp.wait_recv()
                tot = gathered[0]
                for d in range(1, N_DEV):
                    tot = tot + gathered[d]
                dst[...] = tot
            for cp in to_sib[0] + to_sib[1] + to_chips[0] + to_chips[1] + small_copies + dgain_copies:
                cp.wait_send()

    row = lambda width: pl.BlockSpec((tm, width), lambda i: (i, 0))
    full = lambda a: pl.BlockSpec(a.shape, lambda i: (0,) * a.ndim)
    whole = lambda shape: pl.BlockSpec(shape, lambda i: (0,) * len(shape))
    hbm = pl.BlockSpec(memory_space=pl.ANY)
    dtypes = (blocks_in.dtype, blocks_out.dtype)
    buf = lambda n, dts: [pltpu.VMEM((n,) + s, dt) for s, dt in zip(shapes, dts)]
    return pl.pallas_call(
        body, name="input_grad_rs", grid=(n_steps,),
        in_specs=[row(IN_WIDTH), full(w), row(D_MODEL), full(gain), row(D_MODEL), hbm, hbm, full(small)],
        out_specs=(row(D_MODEL), whole(shapes[0]), whole(shapes[1]), whole((SMALL_ROWS, SMALL_COLS)), whole((1, D_MODEL))),
        out_shape=(jax.ShapeDtypeStruct((SEQ, D_MODEL), F32), jax.ShapeDtypeStruct(shapes[0], F32),
                   jax.ShapeDtypeStruct(shapes[1], F32), jax.ShapeDtypeStruct((SMALL_ROWS, SMALL_COLS), F32),
                   jax.ShapeDtypeStruct((1, D_MODEL), F32)),
        scratch_shapes=[*buf(4, dtypes), *buf(4, dtypes), *buf(3, (BF16, BF16)), *buf(3, (BF16, BF16)),
                        pltpu.VMEM((N_DEV, SMALL_ROWS, SMALL_COLS), F32),
                        pltpu.VMEM((1, D_MODEL), F32), pltpu.VMEM((N_DEV, 1, D_MODEL), F32),
                        pltpu.SemaphoreType.DMA((2, 4)), pltpu.SemaphoreType.DMA((2, 4)), pltpu.SemaphoreType.DMA((2, 4)),
                        pltpu.SemaphoreType.DMA((2, 3)), pltpu.SemaphoreType.DMA((2, 3)),
                        pltpu.SemaphoreType.DMA((7,)), pltpu.SemaphoreType.DMA((7,)),
                        pltpu.SemaphoreType.DMA((7,)), pltpu.SemaphoreType.DMA((7,))],
        compiler_params=_params(("arbitrary",)),
    )(dproj, w, x, gain, dy, blocks_in, blocks_out, small)


def _weight_grad(h_t, dproj):
    tk = 1024
    cb = IN_WIDTH // 2
    n_k = SEQ // tk

    def body(ht_ref, dp_ref, out_ref, acc):
        k = pl.program_id(1)
        upd = jnp.dot(ht_ref[...], dp_ref[...], preferred_element_type=F32)

        @pl.when(k == 0)
        def _():
            acc[...] = upd

        @pl.when(k > 0)
        def _():
            acc[...] += upd

        @pl.when(k == n_k - 1)
        def _():
            for b in range(N_DEV // 2):
                out_ref[b] = acc[:, b * SHARD_IN:(b + 1) * SHARD_IN].astype(BF16)

    return pl.pallas_call(
        body, name="weight_grad", grid=(2, n_k),
        in_specs=[pl.BlockSpec((D_MODEL, tk), lambda j, k: (0, k)), pl.BlockSpec((tk, cb), lambda j, k: (k, j))],
        out_specs=pl.BlockSpec((N_DEV // 2, D_MODEL, SHARD_IN), lambda j, k: (j, 0, 0)),
        out_shape=jax.ShapeDtypeStruct((N_DEV, D_MODEL, SHARD_IN), BF16),
        scratch_shapes=[pltpu.VMEM((D_MODEL, cb), F32)],
        compiler_params=_params(("arbitrary", "arbitrary")),
    )(h_t, dproj)


def _adamw(name, w, g, m, v):
    def body(w_ref, g_ref, m_ref, v_ref, d_ref, nm_ref, nv_ref):
        gv = g_ref[...]
        nm = ADAM_B1 * m_ref[...] + (1.0 - ADAM_B1) * gv
        nv = ADAM_B2 * v_ref[...] + (1.0 - ADAM_B2) * jnp.square(gv)
        m_hat = nm / (1.0 - ADAM_B1 ** ADAM_STEP)
        v_hat = nv / (1.0 - ADAM_B2 ** ADAM_STEP)
        d_ref[...] = -ADAM_LR * (m_hat / (jnp.sqrt(v_hat) + ADAM_EPS) + ADAM_WD * w_ref[...])
        nm_ref[...] = nm
        nv_ref[...] = nv

    vmem = pl.BlockSpec(memory_space=pltpu.VMEM)
    out = jax.ShapeDtypeStruct(w.shape, F32)
    return pl.pallas_call(
        body, name=name, in_specs=[vmem] * 4, out_specs=(vmem,) * 3, out_shape=(out,) * 3,
        compiler_params=pltpu.CompilerParams(vmem_limit_bytes=VMEM_LIMIT),
    )(w, g, m, v)


SMALL_USED = D_MODEL + 4 * HEAD_DIM + 8


def _pack_small(norm_gain, qa, ka, sinks, qb, kb, extra=None):
    parts = [norm_gain.reshape(-1), qa.reshape(-1), ka.reshape(-1), sinks.reshape(-1), qb.reshape(-1), kb.reshape(-1)]
    if extra is not None:
        parts.append(extra.reshape(-1))
    flat = jnp.concatenate(parts)
    flat = jnp.pad(flat, (0, SMALL_ROWS * SMALL_COLS - flat.shape[0]))
    return flat.reshape(SMALL_ROWS, SMALL_COLS)


def _unpack_small(a):
    flat = a.reshape(-1)
    sizes = (D_MODEL, HEAD_DIM, HEAD_DIM, 8, HEAD_DIM, HEAD_DIM)
    out, off = [], 0
    for s in sizes:
        out.append(flat[off:off + s].reshape(1, s))
        off += s
    return out


def _fold_heads(row):
    return row[0, :HEAD_DIM] + row[0, HEAD_DIM:]


def kernel(x, norm_gain, w_in, q_norm_a, k_norm_a, sinks_a, q_norm_b, k_norm_b, w_out, loss_target, m_norm_gain, m_w_in, m_q_norm_a, m_k_norm_a, m_sinks_a, m_q_norm_b, m_k_norm_b, m_w_out, v_norm_gain, v_w_in, v_q_norm_a, v_k_norm_a, v_sinks_a, v_q_norm_b, v_k_norm_b, v_w_out):
    x2, tgt = x[0], loss_target[0]
    w_in_sh, w_out_sh = w_in[0], w_out[0]

    w_full = _all_gather_w_in(w_in_sh)

    inv = np.float32(ROPE_THETA) ** (-np.arange(HEAD_DIM // 2, dtype=np.float32) / np.float32(HEAD_DIM // 2))
    ang = np.arange(SEQ, dtype=np.float32)[:, None] * inv[None, :].astype(np.float32)
    cos, sin = np.cos(ang).astype(np.float32), np.sin(ang).astype(np.float32)
    cos4 = jnp.asarray(np.concatenate([cos, cos, cos, cos], axis=1))
    sin4 = jnp.asarray(np.concatenate([-sin, sin, -sin, sin], axis=1))
    blockdiag = np.kron(np.eye(2, dtype=np.float32), np.ones((HEAD_DIM, HEAD_DIM), np.float32))
    bmean = jnp.asarray(blockdiag / HEAD_DIM, dtype=BF16)
    gather_np = np.kron(np.eye(2 * N_PAIRS, dtype=np.float32), np.ones((HEAD_DIM, STAT_REP), np.float32))
    spread_np = np.kron(np.eye(2 * N_PAIRS, dtype=np.float32), np.ones((STAT_REP, HEAD_DIM), np.float32))
    spread_np[np.arange(STAT_WIDTH) % STAT_REP != 0] = 0.0
    gather, spread = jnp.asarray(gather_np, dtype=BF16), jnp.asarray(spread_np, dtype=BF16)
    two = lambda g: jnp.concatenate([g, g], axis=1)
    qkg = jnp.concatenate([two(q_norm_a), two(k_norm_a), two(q_norm_b), two(k_norm_b),
                           jnp.zeros((SMALL_ROWS - 4, PAIR), F32)], axis=0)
    sinks_paired = jnp.stack([sinks_a[0, :N_PAIRS], sinks_a[0, N_PAIRS:]], axis=1)
    sink_rows = jnp.concatenate([jnp.repeat(sinks_paired, BLOCK, axis=1),
                                 jnp.zeros((SMALL_ROWS - N_PAIRS, 2 * BLOCK), F32)], axis=0)

    (tqa, tka, tqb, tkb, gate_a, gate_b, h_t, qa, ka, va, qb, kb, vb, qb4, qb16, kb4, kb16, vb4, vb16,
     gathered_out) = _proj_fwd(x2, norm_gain, w_full, qkg, cos4, sin4, bmean, w_out_sh)
    wo_full = gathered_out.reshape(D_MODEL, D_MODEL)
    oa, la = _attn_fwd("attn_a_fwd", qa[None], ka[None], va[None], sink_rows, BLOCK - 1)
    ob1, lb1 = _attn_fwd("attn_b1_fwd", qb[None], kb[None], vb[None], None, BLOCK)
    ob4, lb4 = _attn_fwd("attn_b4_fwd", qb4, kb4, vb4, None, BLOCK)
    ob16, lb16 = _attn_fwd("attn_b16_fwd", qb16, kb16, vb16, None, BLOCK)
    (loss_cols, dy, gwo, doa, dla, dga, dgb, dob, dob4, dob16, dlb, dlb4, dlb16, lse_b, lse4, lse16) = _tail(
        oa[0], ob1[0], lb1[0], ob4, lb4, ob16, lb16, gate_a, gate_b, x2, tgt, wo_full, spread, gather)

    dqa, dka, dva, dsink = _attn_bwd("attn_a_bwd", qa[None], ka[None], va[None], doa[None], la, dla[None], sink_rows, BLOCK - 1)
    dq1, dk1, dv1 = _attn_bwd("attn_b1_bwd", qb[None], kb[None], vb[None], dob[None], lse_b[None], dlb[None], None, BLOCK)
    dq4, dk4, dv4 = _attn_bwd("attn_b4_bwd", qb4, kb4, vb4, dob4, lse4, dlb4, None, BLOCK)
    dq16, dk16, dv16 = _attn_bwd("attn_b16_bwd", qb16, kb16, vb16, dob16, lse16, dlb16, None, BLOCK)
    dproj, dqkg = _dproj_assemble(dqa[0], dka[0], dva[0], dga, dgb, dq1[0], dk1[0], dv1[0], dq4, dk4, dv4,
                                  dq16, dk16, dv16, tqa, tqb, tkb, tka, qkg, cos4, sin4, bmean)
    gw_in = _weight_grad(h_t, dproj)

    blocks_in = gw_in
    blocks_out = gwo.reshape(N_DEV, SHARD_OUT, D_MODEL)
    g_sinks = jnp.concatenate([jnp.sum(dsink[:N_PAIRS, :BLOCK], axis=1), jnp.sum(dsink[:N_PAIRS, BLOCK:], axis=1)])
    small = _pack_small(jnp.zeros((D_MODEL,), F32), _fold_heads(dqkg[0:1]), _fold_heads(dqkg[1:2]), g_sinks,
                        _fold_heads(dqkg[2:3]), _fold_heads(dqkg[3:4]), extra=0.5 * jnp.sum(loss_cols) / D_MODEL)
    grad_x, g_w_in, g_w_out, small_red, dgain_red = _input_grad_reduce(
        dproj, w_full, x2, norm_gain, dy, blocks_in, blocks_out, small)
    n_gain_rows = D_MODEL // SMALL_COLS
    small_red = jnp.concatenate([dgain_red.reshape(n_gain_rows, SMALL_COLS), small_red[n_gain_rows:]], axis=0)
    g_small = _unpack_small(small_red)

    d_in, nm_in, nv_in = _adamw("adamw_w_in", w_in_sh, g_w_in, m_w_in[0], v_w_in[0])
    d_out, nm_out, nv_out = _adamw("adamw_w_out", w_out_sh, g_w_out, m_w_out[0], v_w_out[0])
    d_s, nm_s, nv_s = _adamw(
        "adamw_small",
        _pack_small(norm_gain, q_norm_a, k_norm_a, sinks_a, q_norm_b, k_norm_b), small_red,
        _pack_small(m_norm_gain, m_q_norm_a, m_k_norm_a, m_sinks_a, m_q_norm_b, m_k_norm_b),
        _pack_small(v_norm_gain, v_q_norm_a, v_k_norm_a, v_sinks_a, v_q_norm_b, v_k_norm_b))
    d_small, nm_small, nv_small = _unpack_small(d_s), _unpack_small(nm_s), _unpack_small(nv_s)

    loss = small_red.reshape(-1)[SMALL_USED]

    def assemble(small_list, big_in, big_out):
        ng, qa_, ka_, sk_, qb_, kb_ = small_list
        return [ng, big_in[None], qa_, ka_, sk_, qb_, kb_, big_out[None]]

    return (loss, grad_x[None], *assemble(g_small, g_w_in, g_w_out), *assemble(d_small, d_in, d_out),
            *assemble(nm_small, nm_in, nm_out), *assemble(nv_small, nv_in, nv_out))
```

```python
import functools

import numpy as np
import jax
import jax.numpy as jnp
from jax import lax
from jax.experimental import pallas as pl
from jax.experimental.pallas import tpu as pltpu

F32 = jnp.float32
BF16 = jnp.bfloat16

SEQ = 4096
D_MODEL = 1024
HEAD_DIM = 64
PAIR = 2 * HEAD_DIM
N_PAIRS = 4
HALF_WIDTH = N_PAIRS * PAIR
KV_A_WIDTH = 128
IN_WIDTH = 3328
BLOCK = 128
STAT_REP = 16
STAT_WIDTH = 128
EPS = 1e-6
NEG = -1e30
ROPE_THETA = 10000.0
N_DEV = 8
SHARD_IN = IN_WIDTH // N_DEV
SHARD_OUT = D_MODEL // N_DEV
PAYLOAD = SHARD_IN + SHARD_OUT
SMALL_ROWS, SMALL_COLS = 8, 256

C_QA, C_KA, C_VA, C_GA, C_QB, C_KB, C_VB, C_GB = 0, 512, 640, 768, 1280, 1792, 2304, 2816

ADAM_LR = 0.001
ADAM_B1 = 0.9
ADAM_B2 = 0.999
ADAM_EPS = 1e-08
ADAM_WD = 0.01
ADAM_STEP = 10

ROW_TILE = 256
PROJ_ROW_TILE = 512
FWD_BLOCKS_PER_STEP = 4
BWD_BLOCKS_PER_STEP = 4
VMEM_LIMIT = 56 * 1024 * 1024

MESH = pl.DeviceIdType.MESH


def _params(sem, vmem=VMEM_LIMIT):
    return pltpu.CompilerParams(dimension_semantics=sem, vmem_limit_bytes=vmem)


def _head_sum(v, bm):
    hi = v.astype(BF16)
    lo = (v - hi.astype(F32)).astype(BF16)
    return (jnp.dot(hi, bm, preferred_element_type=F32) + jnp.dot(lo, bm, preferred_element_type=F32))


def _swap_halves(y):
    lane = lax.broadcasted_iota(jnp.int32, y.shape, 1)
    first = (lane & 32) == 0
    return jnp.where(first, pltpu.roll(y, 96, 1), pltpu.roll(y, 32, 1))


def _sigmoid(g):
    return 1.0 / (1.0 + jnp.exp(-g))


def _tiles(a):
    return [a[:, j * PAIR:(j + 1) * PAIR] for j in range(N_PAIRS)]


def _pair_tiles(t):
    low = lax.broadcasted_iota(jnp.int32, t[0].shape, 1) < HEAD_DIM
    r = [pltpu.roll(a, HEAD_DIM, 1) for a in t]
    return [jnp.where(low, t[0], r[2]), jnp.where(low, r[0], t[2]), jnp.where(low, t[1], r[3]), jnp.where(low, r[1], t[3])]


def _unpair_tiles(p):
    low = lax.broadcasted_iota(jnp.int32, p[0].shape, 1) < HEAD_DIM
    r = [pltpu.roll(a, HEAD_DIM, 1) for a in p]
    return [jnp.where(low, p[0], r[1]), jnp.where(low, p[2], r[3]), jnp.where(low, r[0], p[1]), jnp.where(low, r[2], p[3])]


def _gather_plan(mine_ref, out_ref, send_sems, recv_sems):
    x, y, c = lax.axis_index("x"), lax.axis_index("y"), lax.axis_index("c")
    me, sibling = (x, y, c), (x, y, 1 - c)
    chips = [(1 - x, y), (x, 1 - y), (1 - x, 1 - y)]

    def slot(px, py, pc):
        return out_ref.at[4 * px + 2 * py + pc]

    def copy(k, block, to, from_mine=False):
        return pltpu.make_async_remote_copy(
            src_ref=mine_ref if from_mine else slot(*block), dst_ref=slot(*block),
            send_sem=send_sems.at[k], recv_sem=recv_sems.at[k], device_id=to, device_id_type=MESH)

    first = [copy(0, me, sibling, True)] + [copy(1 + j, me, (*chip, c), True) for j, chip in enumerate(chips)]
    landed = [copy(1 + j, (*chip, c), me) for j, chip in enumerate(chips)]
    passed = [copy(4 + j, (*chip, c), sibling) for j, chip in enumerate(chips)]
    from_sibling = [copy(0, sibling, me)] + [copy(4 + j, (*chip, 1 - c), me) for j, chip in enumerate(chips)]
    return slot(*me), first, landed, passed, from_sibling


GATHER_SCRATCH = [pltpu.SemaphoreType.DMA((7,)), pltpu.SemaphoreType.DMA((7,))]


def _all_gather_w_in(w_in_sh):
    rows, cols = w_in_sh.shape

    def body(w_ref, out_ref, mine_ref, blocks, send_sems, recv_sems):
        mine_ref[...] = w_ref[...].astype(BF16)
        my_slot, first, landed, passed, from_sibling = _gather_plan(mine_ref, blocks, send_sems, recv_sems)
        for cp in first:
            cp.start()
        my_slot[...] = mine_ref[...]
        for arrival, forward in zip(landed, passed):
            arrival.wait_recv()
            forward.start()
        for arrival in from_sibling:
            arrival.wait_recv()
        for cp in first + passed:
            cp.wait_send()
        for d in range(N_DEV):
            out_ref[:, d * cols:(d + 1) * cols] = blocks[d]

    vmem = pl.BlockSpec(memory_space=pltpu.VMEM)
    return pl.pallas_call(
        body, name="ag_w_in",
        out_shape=jax.ShapeDtypeStruct((rows, N_DEV * cols), BF16),
        in_specs=[vmem], out_specs=vmem,
        scratch_shapes=[pltpu.VMEM((rows, cols), BF16), pltpu.VMEM((N_DEV, rows, cols), BF16)] + GATHER_SCRATCH,
        compiler_params=pltpu.CompilerParams(vmem_limit_bytes=VMEM_LIMIT),
    )(w_in_sh)


def _reduce_scatter_grads(blocks_in, blocks_out, small):
    shapes = (blocks_in.shape[1:], blocks_out.shape[1:])

    def body(ga_hbm, gb_hbm, small_ref, out_a, out_b, small_out_ref,
             part_a, part_b, sib_a, sib_b, wire_a, wire_b, chips_a, chips_b, small_all,
             load_sems, sib_send, sib_recv, chip_send, chip_recv, small_send, small_recv):
        x, y, c = lax.axis_index("x"), lax.axis_index("y"), lax.axis_index("c")
        sibling = (x, y, 1 - c)
        chips = [(x, y), (1 - x, y), (x, 1 - y), (1 - x, 1 - y)]
        my_id = 4 * x + 2 * y + c
        g_hbm, part, from_sib = (ga_hbm, gb_hbm), (part_a, part_b), (sib_a, sib_b)
        to_wire, from_chips, out = (wire_a, wire_b), (chips_a, chips_b), (out_a, out_b)
        both = (0, 1)

        def blk(a, chip, core):
            return g_hbm[a].at[4 * chip[0] + 2 * chip[1] + core]

        small_all[my_id] = small_ref[...]
        small_copies = []
        for rel in range(1, N_DEV):
            dx, dy, dc = (rel >> 2) & 1, (rel >> 1) & 1, rel & 1
            to = (1 - x if dx else x, 1 - y if dy else y, 1 - c if dc else c)
            small_copies.append(pltpu.make_async_remote_copy(
                src_ref=small_ref, dst_ref=small_all.at[my_id],
                send_sem=small_send.at[rel - 1], recv_sem=small_recv.at[rel - 1], device_id=to, device_id_type=MESH))
        for cp in small_copies:
            cp.start()

        loads = [[pltpu.make_async_copy(blk(a, chips[k], c), part[a].at[k], load_sems.at[a, k]) for k in range(4)] for a in both]
        to_sib = [[pltpu.make_async_remote_copy(
            src_ref=blk(a, chips[k], 1 - c), dst_ref=from_sib[a].at[k], send_sem=sib_send.at[a, k], recv_sem=sib_recv.at[a, k],
            device_id=sibling, device_id_type=MESH) for k in range(4)] for a in both]
        for k in (1, 2, 3, 0):
            for a in both:
                loads[a][k].start()
                to_sib[a][k].start()

        to_chips = [[pltpu.make_async_remote_copy(
            src_ref=to_wire[a].at[k - 1], dst_ref=from_chips[a].at[k - 1],
            send_sem=chip_send.at[a, k - 1], recv_sem=chip_recv.at[a, k - 1],
            device_id=(*chips[k], c), device_id_type=MESH) for k in range(1, 4)] for a in both]
        for k in (1, 2, 3):
            for a in both:
                loads[a][k].wait()
                to_sib[a][k].wait_recv()
                to_wire[a][k - 1] = (part[a][k] + from_sib[a][k]).astype(BF16)
                to_chips[a][k - 1].start()
        for a in both:
            loads[a][0].wait()
            to_sib[a][0].wait_recv()
            acc = part[a][0] + from_sib[a][0]
            for k in range(3):
                to_chips[a][k].wait_recv()
                acc = acc + from_chips[a][k].astype(F32)
            out[a][...] = acc

        for cp in small_copies:
            cp.wait_recv()
        tot = small_all[0]
        for d in range(1, N_DEV):
            tot = tot + small_all[d]
        small_out_ref[...] = tot
        for cp in to_sib[0] + to_sib[1] + to_chips[0] + to_chips[1] + small_copies:
            cp.wait_send()

    vmem = pl.BlockSpec(memory_space=pltpu.VMEM)
    hbm = pl.BlockSpec(memory_space=pl.ANY)
    buf = lambda n, dtype: [pltpu.VMEM((n,) + s, dtype) for s in shapes]
    return pl.pallas_call(
        body, name="rs_grads",
        out_shape=(jax.ShapeDtypeStruct(shapes[0], F32), jax.ShapeDtypeStruct(shapes[1], F32),
                   jax.ShapeDtypeStruct((SMALL_ROWS, SMALL_COLS), F32)),
        in_specs=[hbm, hbm, vmem], out_specs=(vmem, vmem, vmem),
        scratch_shapes=[*buf(4, F32), *buf(4, F32), *buf(3, BF16), *buf(3, BF16),
                        pltpu.VMEM((N_DEV, SMALL_ROWS, SMALL_COLS), F32),
                        pltpu.SemaphoreType.DMA((2, 4)), pltpu.SemaphoreType.DMA((2, 4)), pltpu.SemaphoreType.DMA((2, 4)),
                        pltpu.SemaphoreType.DMA((2, 3)), pltpu.SemaphoreType.DMA((2, 3)),
                        pltpu.SemaphoreType.DMA((7,)), pltpu.SemaphoreType.DMA((7,))],
        compiler_params=pltpu.CompilerParams(vmem_limit_bytes=VMEM_LIMIT),
    )(blocks_in, blocks_out, small)


def _fold_scratch(tm):
    return pltpu.VMEM((N_PAIRS, tm, PAIR), F32)


def _fold_store(val, scr, out4, out16, tm):
    groups = range(val.shape[1] // PAIR)
    for j in groups:
        scr[j] = val[:, j * PAIR:(j + 1) * PAIR]
    for dil, out in ((4, out4), (16, out16)):
        for r in range(dil):
            for j in groups:
                out[r, :, j * PAIR:(j + 1) * PAIR] = scr[j, pl.ds(r, tm // dil, stride=dil), :].astype(out.dtype)


def _unfold_load(src, scr, dil, tm):
    groups = range(src.shape[2] // PAIR)
    for r in range(dil):
        for j in groups:
            scr[j, pl.ds(r, tm // dil, stride=dil), :] = src[r, :, j * PAIR:(j + 1) * PAIR].astype(F32)
    return jnp.concatenate([scr[j] for j in groups], axis=1)


def _fold_specs(tm, dtype, width=HALF_WIDTH):
    shapes = (jax.ShapeDtypeStruct((4, SEQ // 4, width), dtype), jax.ShapeDtypeStruct((16, SEQ // 16, width), dtype))
    specs = (pl.BlockSpec((4, tm // 4, width), lambda i: (0, i, 0)),
             pl.BlockSpec((16, tm // 16, width), lambda i: (0, i, 0)))
    return shapes, specs


def _proj_fwd(x, gain, w, qkg, cos4, sin4, bmean, w_out_sh):
    tm = PROJ_ROW_TILE
    n_steps = SEQ // tm

    def norm_rope(t, g, cos, sin, bm, scale):
        rr = lax.rsqrt(_head_sum(t * t, bm) + EPS)
        yv = t * rr * g
        return (yv * cos + _swap_halves(yv) * sin) * scale

    def body(x_ref, g_ref, w_ref, qkg_ref, cos_ref, sin_ref, bm_ref, wo_ref,
             tqa_ref, tka_ref, tqb_ref, tkb_ref, ga_ref, gb_ref, ht_ref, qa_ref, ka_ref, va_ref, qb_ref, kb_ref, vb_ref,
             qb4_ref, qb16_ref, kb4_ref, kb16_ref, vb4_ref, vb16_ref, wo_all_ref,
             proj, scr, wo_mine, wo_all, send_sems, recv_sems):
        i = pl.program_id(0)
        my_slot, first, landed, passed, from_sibling = _gather_plan(wo_mine, wo_all, send_sems, recv_sems)

        @pl.when(i == 0)
        def _():
            wo_mine[...] = wo_ref[...].astype(BF16)
            for cp in first:
                cp.start()
            my_slot[...] = wo_mine[...]

        @pl.when(i == n_steps // 2)
        def _():
            for arrival, forward in zip(landed, passed):
                arrival.wait_recv()
                forward.start()

        xf = x_ref[...]
        r = lax.rsqrt(jnp.mean(xf * xf, axis=-1, keepdims=True) + EPS)
        hf = xf * r * g_ref[...]
        ht_ref[...] = hf.T.astype(BF16)
        cos, sin, bm = cos_ref[...], sin_ref[...], bm_ref[...]
        proj[...] = jnp.dot(hf.astype(BF16), w_ref[...], preferred_element_type=F32)

        def roped(tiles, row, scale):
            g = qkg_ref[row:row + 1, :]
            return jnp.concatenate([norm_rope(t, g, cos, sin, bm, scale) for t in tiles], axis=1)

        tqa = _pair_tiles(_tiles(proj[:, C_QA:C_QA + HALF_WIDTH]))
        tqa_ref[...] = jnp.concatenate(tqa, axis=1).astype(BF16)
        qa_ref[...] = roped(tqa, 0, HEAD_DIM ** -0.5).astype(BF16)
        ga_ref[...] = jnp.concatenate(_pair_tiles(_tiles(proj[:, C_GA:C_GA + HALF_WIDTH])), axis=1).astype(BF16)
        gb_ref[...] = proj[:, C_GB:C_GB + HALF_WIDTH].astype(BF16)
        tqb = proj[:, C_QB:C_QB + HALF_WIDTH]
        tqb_ref[...] = tqb.astype(BF16)
        qb = roped(_tiles(tqb), 2, HEAD_DIM ** -0.5)
        qb_ref[...] = qb.astype(BF16)
        _fold_store(qb, scr, qb4_ref, qb16_ref, tm)
        tkb = proj[:, C_KB:C_KB + HALF_WIDTH]
        tkb_ref[...] = tkb.astype(BF16)
        kb = roped(_tiles(tkb), 3, 1.0)
        kb_ref[...] = kb.astype(BF16)
        _fold_store(kb, scr, kb4_ref, kb16_ref, tm)
        vb = proj[:, C_VB:C_VB + HALF_WIDTH]
        vb_ref[...] = vb.astype(BF16)
        _fold_store(vb, scr, vb4_ref, vb16_ref, tm)
        tka = proj[:, C_KA:C_KA + KV_A_WIDTH]
        tka_ref[...] = tka.astype(BF16)
        ka_ref[...] = roped([tka], 1, 1.0).astype(BF16)
        va_ref[...] = proj[:, C_VA:C_VA + KV_A_WIDTH].astype(BF16)

        @pl.when(i == n_steps - 1)
        def _():
            for arrival in from_sibling:
                arrival.wait_recv()
            for cp in first + passed:
                cp.wait_send()
            wo_all_ref[...] = wo_all[...]

    row = lambda width: pl.BlockSpec((tm, width), lambda i: (i, 0))
    full = lambda a: pl.BlockSpec(a.shape, lambda i: (0,) * a.ndim)
    nat = lambda width, dtype=BF16: jax.ShapeDtypeStruct((SEQ, width), dtype)
    f_shapes, f_specs = _fold_specs(tm, BF16)
    return pl.pallas_call(
        body, name="proj_fwd", grid=(SEQ // tm,),
        in_specs=[row(D_MODEL), full(gain), full(w), full(qkg), row(PAIR), row(PAIR), full(bmean), full(w_out_sh)],
        out_specs=(row(HALF_WIDTH), row(KV_A_WIDTH), row(HALF_WIDTH), row(HALF_WIDTH), row(HALF_WIDTH), row(HALF_WIDTH),
                   pl.BlockSpec((D_MODEL, tm), lambda i: (0, i)),
                   row(HALF_WIDTH), row(KV_A_WIDTH), row(KV_A_WIDTH), row(HALF_WIDTH), row(HALF_WIDTH), row(HALF_WIDTH),
                   *f_specs, *f_specs, *f_specs,
                   pl.BlockSpec((N_DEV,) + w_out_sh.shape, lambda i: (0, 0, 0))),
        out_shape=(nat(HALF_WIDTH), nat(KV_A_WIDTH), nat(HALF_WIDTH), nat(HALF_WIDTH), nat(HALF_WIDTH), nat(HALF_WIDTH),
                   jax.ShapeDtypeStruct((D_MODEL, SEQ), BF16),
                   nat(HALF_WIDTH), nat(KV_A_WIDTH), nat(KV_A_WIDTH), nat(HALF_WIDTH), nat(HALF_WIDTH), nat(HALF_WIDTH),
                   *f_shapes, *f_shapes, *f_shapes,
                   jax.ShapeDtypeStruct((N_DEV,) + w_out_sh.shape, BF16)),
        scratch_shapes=[pltpu.VMEM((tm, IN_WIDTH), F32), _fold_scratch(tm), pltpu.VMEM(w_out_sh.shape, BF16),
                        pltpu.VMEM((N_DEV,) + w_out_sh.shape, BF16)] + GATHER_SCRATCH,
        compiler_params=_params(("arbitrary",)),
    )(x, gain, w, qkg, cos4, sin4, bmean, w_out_sh)


def _band_mask(i, max_dist):
    j = lax.broadcasted_iota(jnp.int32, (2 * BLOCK, 2 * BLOCK), 0)
    c = lax.broadcasted_iota(jnp.int32, (2 * BLOCK, 2 * BLOCK), 1)
    dist = (c & (BLOCK - 1)) + BLOCK - j
    return (dist >= 0) & (dist <= max_dist) & ((j >= BLOCK) | (i > 0))


def _stack_heads(t):
    lane = lax.broadcasted_iota(jnp.int32, t.shape, 1)
    low = lane < HEAD_DIM
    zero = jnp.zeros_like(t)
    return jnp.concatenate([jnp.where(low, t, zero), jnp.where(low, zero, t)], axis=0)


def _unstack_t(t):
    return jnp.concatenate([t[:HEAD_DIM, :BLOCK], t[HEAD_DIM:, BLOCK:]], axis=0).T


def _rows_to_stats(rows):
    parts = []
    for row in rows:
        parts.append(jnp.broadcast_to(row[:, :BLOCK], (STAT_REP, BLOCK)))
        parts.append(jnp.broadcast_to(row[:, BLOCK:], (STAT_REP, BLOCK)))
    return jnp.concatenate(parts, axis=0).T


def _stats_to_rows(t):
    tt = t.T
    return [jnp.concatenate([tt[2 * p * STAT_REP:2 * p * STAT_REP + 1, :],
                             tt[(2 * p + 1) * STAT_REP:(2 * p + 1) * STAT_REP + 1, :]], axis=1) for p in range(N_PAIRS)]


def _attn_fwd(name, q, k, v, sink_rows, max_dist):
    n_seq, length, _ = q.shape
    ck = k.shape[2]
    nb = length // BLOCK
    shared = ck == PAIR
    has_sinks = sink_rows is not None

    qb = FWD_BLOCKS_PER_STEP

    def body(*refs):
        if has_sinks:
            q_ref, kc_ref, vc_ref, sink_ref, o_ref, lse_ref, kp_ref, vp_ref = refs
        else:
            q_ref, kc_ref, vc_ref, o_ref, lse_ref, kp_ref, vp_ref = refs
        step = pl.program_id(0)

        @pl.when(step == 0)
        def _():
            kp_ref[...] = jnp.zeros_like(kp_ref)
            vp_ref[...] = jnp.zeros_like(vp_ref)

        valid = [_band_mask((step * qb + b) & (nb - 1), max_dist) for b in range(qb)]
        cols = [slice(p * PAIR, (p + 1) * PAIR) for p in range(N_PAIRS)]
        kcols = [slice(0, PAIR) if shared else c for c in cols]
        rows = [slice(b * BLOCK, (b + 1) * BLOCK) for b in range(qb)]
        units = [(b, p) for b in range(qb) for p in range(N_PAIRS)]
        n = range(len(units))

        def window(prev_ref, cur_ref, b, kc):
            before = prev_ref[:, kc] if b == 0 else cur_ref[rows[b - 1], kc]
            return jnp.concatenate([before, cur_ref[rows[b], kc]], axis=0)

        st = [lax.dot_general(window(kp_ref, kc_ref, b, kcols[p]), _stack_heads(q_ref[rows[b], cols[p]]),
                              (((1,), (1,)), ((), ())), preferred_element_type=F32) for b, p in units]
        st = [jnp.where(valid[units[u][0]], st[u], NEG) for u in n]
        m = [jnp.max(s, axis=0, keepdims=True) for s in st]
        if has_sinks:
            sk = [sink_ref[p:p + 1, :] for _, p in units]
            m = [jnp.maximum(m[u], sk[u]) for u in n]
        pt = [jnp.exp(st[u] - m[u]) for u in n]
        l = [jnp.sum(t, axis=0, keepdims=True) for t in pt]
        if has_sinks:
            l = [l[u] + jnp.exp(sk[u] - m[u]) for u in n]
        v2t = [window(vp_ref, vc_ref, b, kcols[p]).astype(F32).T.astype(BF16) for b, p in units]
        ot = [jnp.dot(v2t[u], pt[u].astype(BF16), preferred_element_type=F32) / l[u] for u in n]
        for u, (b, p) in enumerate(units):
            o_ref[rows[b], cols[p]] = _unstack_t(ot[u]).astype(BF16)
        for b in range(qb):
            lse_ref[rows[b], :] = _rows_to_stats([m[u] + jnp.log(l[u]) for u in n if units[u][0] == b])
        kp_ref[...] = kc_ref[rows[-1], :]
        vp_ref[...] = vc_ref[rows[-1], :]

    cur = lambda width: pl.BlockSpec((qb * BLOCK, width), lambda s: (s, 0))
    flat = lambda a: a.reshape(n_seq * length, a.shape[2])
    in_specs = [cur(HALF_WIDTH), cur(ck), cur(ck)]
    args = [flat(q), flat(k), flat(v)]
    if has_sinks:
        in_specs.append(pl.BlockSpec(sink_rows.shape, lambda s: (0, 0)))
        args.append(sink_rows)
    o, lse = pl.pallas_call(
        body, name=name, grid=(n_seq * nb // qb,), in_specs=in_specs,
        out_specs=(cur(HALF_WIDTH), cur(STAT_WIDTH)),
        out_shape=(jax.ShapeDtypeStruct((n_seq * length, HALF_WIDTH), BF16),
                   jax.ShapeDtypeStruct((n_seq * length, STAT_WIDTH), F32)),
        scratch_shapes=[pltpu.VMEM((BLOCK, ck), BF16), pltpu.VMEM((BLOCK, ck), BF16)],
        compiler_params=_params(("arbitrary",)),
    )(*args)
    return o.reshape(n_seq, length, HALF_WIDTH), lse.reshape(n_seq, length, STAT_WIDTH)


def _attn_bwd(name, q, k, v, d_o, lse, delta, sink_rows, max_dist):
    n_seq, length, _ = q.shape
    ck = k.shape[2]
    nb = length // BLOCK
    n_blocks = n_seq * nb
    n_rows = n_seq * length
    shared = ck == PAIR
    has_sinks = sink_rows is not None
    qb = BWD_BLOCKS_PER_STEP
    n_steps = n_blocks // qb

    def body(*refs):
        if has_sinks:
            (q_ref, kc_ref, vc_ref, do_ref, lse_ref, dl_ref, sink_ref,
             dq_ref, dk_ref, dv_ref, dsink_ref, ck_scr, cv_scr, kp_ref, vp_ref) = refs
        else:
            (q_ref, kc_ref, vc_ref, do_ref, lse_ref, dl_ref,
             dq_ref, dk_ref, dv_ref, ck_scr, cv_scr, kp_ref, vp_ref) = refs
        step = pl.program_id(0)

        @pl.when(step == 0)
        def _():
            ck_scr[...] = jnp.zeros_like(ck_scr)
            cv_scr[...] = jnp.zeros_like(cv_scr)
            kp_ref[...] = jnp.zeros_like(kp_ref)
            vp_ref[...] = jnp.zeros_like(vp_ref)
            if has_sinks:
                dsink_ref[...] = jnp.zeros_like(dsink_ref)

        valid = [_band_mask((step * qb + b) & (nb - 1), max_dist) for b in range(qb)]
        cols = [slice(p * PAIR, (p + 1) * PAIR) for p in range(N_PAIRS)]
        kcols = [slice(0, PAIR) if shared else c for c in cols]
        rows = [slice(b * BLOCK, (b + 1) * BLOCK) for b in range(qb)]
        units = [(b, p) for b in range(qb) for p in range(N_PAIRS)]
        n = range(len(units))
        nt = (((1,), (1,)), ((), ()))

        def window(prev_ref, cur_ref, b, kc):
            before = prev_ref[:, kc] if b == 0 else cur_ref[rows[b - 1], kc]
            return jnp.concatenate([before, cur_ref[rows[b], kc]], axis=0)

        q_st = [_stack_heads(q_ref[rows[b], cols[p]]) for b, p in units]
        do_st = [_stack_heads(do_ref[rows[b], cols[p]]) for b, p in units]
        k2 = [window(kp_ref, kc_ref, b, kcols[p]) for b, p in units]
        v2 = [window(vp_ref, vc_ref, b, kcols[p]) for b, p in units]
        st = [lax.dot_general(k2[u], q_st[u], nt, preferred_element_type=F32) for u in n]
        dpt = [lax.dot_general(v2[u], do_st[u], nt, preferred_element_type=F32) for u in n]
        lse_rows = [_stats_to_rows(lse_ref[rows[b], :]) for b in range(qb)]
        dl_rows = [_stats_to_rows(dl_ref[rows[b], :]) for b in range(qb)]
        lse_row = [lse_rows[b][p] for b, p in units]
        dl_row = [dl_rows[b][p] for b, p in units]
        pt = [jnp.exp(jnp.where(valid[units[u][0]], st[u], NEG) - lse_row[u]) for u in n]
        dst = [(pt[u] * (dpt[u] - dl_row[u])).astype(BF16) for u in n]
        ptb = [t.astype(BF16) for t in pt]
        dv2 = [jnp.dot(ptb[u], do_st[u], preferred_element_type=F32) for u in n]
        dk2 = [jnp.dot(dst[u], q_st[u], preferred_element_type=F32) for u in n]
        k2t = [k2[u].astype(F32).T.astype(BF16) for u in n]
        dqt = [jnp.dot(k2t[u], dst[u], preferred_element_type=F32) for u in n]
        for u, (b, p) in enumerate(units):
            dq_ref[rows[b], cols[p]] = _unstack_t(dqt[u]).astype(BF16)
        if has_sinks:
            for u, (b, p) in enumerate(units):
                p_sink = jnp.exp(sink_ref[p:p + 1, :] - lse_row[u])
                dsink_ref[p:p + 1, :] = dsink_ref[p:p + 1, :] - p_sink * dl_row[u]

        def total(parts, w, group):
            sel = [u for u, (b, p) in enumerate(units) if (shared or p == group)]
            terms = ([parts[u][:BLOCK] for u in sel if units[u][0] == w]
                     + [parts[u][BLOCK:] for u in sel if units[u][0] == w - 1])
            tot = terms[0]
            for t in terms[1:]:
                tot = tot + t
            return tot

        first_row = step * (qb * BLOCK)
        for acc_ref, out_ref, parts in ((ck_scr, dk_ref, dk2), (cv_scr, dv_ref, dv2)):
            for group in range(1 if shared else N_PAIRS):
                kc = kcols[group]

                @pl.when(step > 0)
                def _():
                    out_ref[pl.ds(pl.multiple_of(first_row - BLOCK, BLOCK), BLOCK), kc] = (
                        acc_ref[:, kc] + total(parts, 0, group)).astype(BF16)

                for w in range(1, qb):
                    out_ref[pl.ds(pl.multiple_of(first_row + (w - 1) * BLOCK, BLOCK), BLOCK), kc] = (
                        total(parts, w, group).astype(BF16))
                acc_ref[:, kc] = total(parts, qb, group)

        @pl.when(step == n_steps - 1)
        def _():
            dk_ref[pl.ds(n_rows - BLOCK, BLOCK), :] = ck_scr[...].astype(BF16)
            dv_ref[pl.ds(n_rows - BLOCK, BLOCK), :] = cv_scr[...].astype(BF16)

        kp_ref[...] = kc_ref[rows[-1], :]
        vp_ref[...] = vc_ref[rows[-1], :]

    cur = lambda width: pl.BlockSpec((qb * BLOCK, width), lambda s: (s, 0))
    whole = lambda width: pl.BlockSpec((n_rows, width), lambda s: (0, 0))
    flat = lambda a: a.reshape(n_rows, a.shape[2])
    in_specs = [cur(HALF_WIDTH), cur(ck), cur(ck), cur(HALF_WIDTH), cur(STAT_WIDTH), cur(STAT_WIDTH)]
    args = [flat(a) for a in (q, k, v, d_o, lse, delta)]
    out_specs = [cur(HALF_WIDTH), whole(ck), whole(ck)]
    out_shape = [jax.ShapeDtypeStruct((n_rows, HALF_WIDTH), BF16),
                 jax.ShapeDtypeStruct((n_rows, ck), BF16), jax.ShapeDtypeStruct((n_rows, ck), BF16)]
    if has_sinks:
        in_specs.append(pl.BlockSpec(sink_rows.shape, lambda s: (0, 0)))
        args.append(sink_rows)
        out_specs.append(pl.BlockSpec(sink_rows.shape, lambda s: (0, 0)))
        out_shape.append(jax.ShapeDtypeStruct(sink_rows.shape, F32))
    outs = pl.pallas_call(
        body, name=name, grid=(n_steps,), in_specs=in_specs,
        out_specs=tuple(out_specs), out_shape=tuple(out_shape),
        scratch_shapes=[pltpu.VMEM((BLOCK, ck), F32), pltpu.VMEM((BLOCK, ck), F32),
                        pltpu.VMEM((BLOCK, ck), BF16), pltpu.VMEM((BLOCK, ck), BF16)],
        compiler_params=_params(("arbitrary",)),
    )(*args)
    return tuple(o.reshape(n_seq, length, o.shape[1]) for o in outs[:3]) + tuple(outs[3:])


def _tail(oa, ob1, lb1, ob4, lb4, ob16, lb16, gate_a, gate_b, x, target, w_out, spread, gather):
    tm = ROW_TILE

    def split_dot(v, mat):
        hi = v.astype(BF16)
        lo = (v - hi.astype(F32)).astype(BF16)
        return jnp.dot(hi, mat, preferred_element_type=F32) + jnp.dot(lo, mat, preferred_element_type=F32)

    def body(oa_ref, ob1_ref, lb1_ref, ob4_ref, lb4_ref, ob16_ref, lb16_ref, ga_ref, gb_ref, x_ref, t_ref, w_ref,
             sp_ref, ga_mat_ref,
             loss_ref, dy_ref, gwo_ref, doa_ref, dla_ref, dga_ref, dgb_ref,
             dob_ref, dob4_ref, dob16_ref, dlb_ref, dlb4_ref, dlb16_ref, lse_ref, lse4_ref, lse16_ref,
             s_f, mix_keep, dy_keep):
        i = pl.program_id(0)
        sp, gat = sp_ref[...], ga_mat_ref[...]
        o4, o16 = _unfold_load(ob4_ref, s_f, 4, tm), _unfold_load(ob16_ref, s_f, 16, tm)
        l4, l16 = _unfold_load(lb4_ref, s_f, 4, tm), _unfold_load(lb16_ref, s_f, 16, tm)
        o1, l1 = ob1_ref[...].astype(F32), lb1_ref[...]
        mx = jnp.maximum(jnp.maximum(l1, l4), l16)
        e1, e4, e16 = jnp.exp(l1 - mx), jnp.exp(l4 - mx), jnp.exp(l16 - mx)
        den = e1 + e4 + e16
        inv = 1.0 / den
        ob = split_dot(e1 * inv, sp) * o1 + split_dot(e4 * inv, sp) * o4 + split_dot(e16 * inv, sp) * o16
        lse_b = mx + jnp.log(den)

        oa, ga, gb = oa_ref[...].astype(F32), ga_ref[...].astype(F32), gb_ref[...].astype(F32)
        sa, sb = _sigmoid(ga), _sigmoid(gb)
        mixed = jnp.concatenate(_unpair_tiles(_tiles(oa * (ga * sa))) + [ob * (gb * sb)], axis=1)
        mixed_bf = mixed.astype(BF16)
        w = w_ref[...]
        yv = x_ref[...] + jnp.dot(mixed_bf, w, preferred_element_type=F32)
        err = yv - t_ref[...]
        sq = jnp.sum(err * err, axis=0, keepdims=True)
        dy = err * (1.0 / D_MODEL)
        dy_ref[...] = dy
        dy_bf = dy.astype(BF16)
        mix_t = mixed.T.astype(BF16)

        @pl.when(i == 0)
        def _():
            loss_ref[...] = sq

        @pl.when(i > 0)
        def _():
            loss_ref[...] += sq

        @pl.when((i & 1) == 0)
        def _():
            mix_keep[...] = mix_t
            dy_keep[...] = dy_bf

        @pl.when((i & 1) == 1)
        def _():
            gw = jnp.dot(jnp.concatenate([mix_keep[...], mix_t], axis=1), jnp.concatenate([dy_keep[...], dy_bf], axis=0),
                         preferred_element_type=F32)

            @pl.when(i == 1)
            def _():
                gwo_ref[...] = gw

            @pl.when(i > 1)
            def _():
                gwo_ref[...] += gw

        dmix = lax.dot_general(dy_bf, w, (((1,), (1,)), ((), ())), preferred_element_type=F32)
        dma = jnp.concatenate(_pair_tiles(_tiles(dmix[:, :HALF_WIDTH])), axis=1)
        dmb = dmix[:, HALF_WIDTH:]

        doa = dma * (ga * sa)
        doa_ref[...] = doa.astype(BF16)
        dla_ref[...] = split_dot(doa * oa, gat)
        dga_ref[...] = (dma * oa * (sa * (1.0 + ga * (1.0 - sa)))).astype(BF16)
        dob = dmb * (gb * sb)
        dgb_ref[...] = (dmb * ob * (sb * (1.0 + gb * (1.0 - sb)))).astype(BF16)
        dlb = split_dot(dob * ob, gat)
        dob_ref[...] = dob.astype(BF16)
        _fold_store(dob, s_f, dob4_ref, dob16_ref, tm)
        dlb_ref[...] = dlb
        _fold_store(dlb, s_f, dlb4_ref, dlb16_ref, tm)
        lse_ref[...] = lse_b
        _fold_store(lse_b, s_f, lse4_ref, lse16_ref, tm)

    row = lambda width: pl.BlockSpec((tm, width), lambda i: (i, 0))
    full = lambda a: pl.BlockSpec(a.shape, lambda i: (0,) * a.ndim)
    fb_shapes, fb_specs = _fold_specs(tm, BF16)
    _, ff_specs = _fold_specs(tm, F32)
    st_shapes, st_specs = _fold_specs(tm, F32, STAT_WIDTH)
    nat = lambda dtype, width=HALF_WIDTH: jax.ShapeDtypeStruct((SEQ, width), dtype)
    return pl.pallas_call(
        body, name="tail", grid=(SEQ // tm,),
        in_specs=[row(HALF_WIDTH), row(HALF_WIDTH), row(STAT_WIDTH), ff_specs[0], st_specs[0], ff_specs[1], st_specs[1],
                  row(HALF_WIDTH), row(HALF_WIDTH), row(D_MODEL), row(D_MODEL), full(w_out), full(spread), full(gather)],
        out_specs=(pl.BlockSpec((1, D_MODEL), lambda i: (0, 0)), row(D_MODEL),
                   pl.BlockSpec((D_MODEL, D_MODEL), lambda i: (0, 0)),
                   row(HALF_WIDTH), row(STAT_WIDTH), row(HALF_WIDTH), row(HALF_WIDTH),
                   row(HALF_WIDTH), *fb_specs, row(STAT_WIDTH), *st_specs, row(STAT_WIDTH), *st_specs),
        out_shape=(jax.ShapeDtypeStruct((1, D_MODEL), F32), jax.ShapeDtypeStruct((SEQ, D_MODEL), F32),
                   jax.ShapeDtypeStruct((D_MODEL, D_MODEL), F32),
                   nat(BF16), nat(F32, STAT_WIDTH), nat(BF16), nat(BF16),
                   nat(BF16), *fb_shapes, nat(F32, STAT_WIDTH), *st_shapes, nat(F32, STAT_WIDTH), *st_shapes),
        scratch_shapes=[_fold_scratch(tm), pltpu.VMEM((D_MODEL, tm), BF16), pltpu.VMEM((tm, D_MODEL), BF16)],
        compiler_params=_params(("arbitrary",)),
    )(oa, ob1, lb1, ob4, lb4, ob16, lb16, gate_a, gate_b, x, target, w_out, spread, gather)


def _dproj_assemble(dqa, dka, dva, dga, dgb, dq1, dk1, dv1, dq4, dk4, dv4, dq16, dk16, dv16, tqa, tqb, tkb, tka,
                    qkg, cos4, sin4, bmean):
    tm = ROW_TILE

    def norm_rope_bwd(d_out, t, g, cos, sin, bm, scale):
        d_r = d_out * scale
        dyv = d_r * cos + _swap_halves(d_r * sin)
        rr = lax.rsqrt(_head_sum(t * t, bm) + EPS)
        that = t * rr
        dgain = jnp.sum(dyv * that, axis=0, keepdims=True)
        gdy = dyv * g
        dt = rr * (gdy - that * _head_sum(that * gdy, bm))
        return dt, dgain

    def body(dqa_ref, dka_ref, dva_ref, dga_ref, dgb_ref, dq1_ref, dk1_ref, dv1_ref, dq4_ref, dk4_ref, dv4_ref,
             dq16_ref, dk16_ref, dv16_ref, tqa_ref, tqb_ref, tkb_ref, tka_ref, qkg_ref, cos_ref, sin_ref, bm_ref,
             dproj_ref, dqkg_ref, s_f):
        i = pl.program_id(0)
        cos, sin, bm = cos_ref[...], sin_ref[...], bm_ref[...]

        def merged(nat_ref, f4_ref, f16_ref):
            return nat_ref[...].astype(F32) + _unfold_load(f4_ref, s_f, 4, tm) + _unfold_load(f16_ref, s_f, 16, tm)

        @pl.when(i == 0)
        def _():
            dqkg_ref[...] = jnp.zeros_like(dqkg_ref)

        def through(d_out, t, row, scale, c0, paired=False):
            g = qkg_ref[row:row + 1, :]
            tot = jnp.zeros((1, PAIR), F32)
            dts = []
            for j in range(d_out.shape[1] // PAIR):
                cols = slice(j * PAIR, (j + 1) * PAIR)
                dt, dg = norm_rope_bwd(d_out[:, cols], t[:, cols], g, cos, sin, bm, scale)
                dts.append(dt)
                tot = tot + dg
            if paired:
                dts = _unpair_tiles(dts)
            for j, dt in enumerate(dts):
                dproj_ref[:, c0 + j * PAIR:c0 + (j + 1) * PAIR] = dt.astype(BF16)
            dqkg_ref[row:row + 1, :] += tot

        through(dqa_ref[...].astype(F32), tqa_ref[...].astype(F32), 0, HEAD_DIM ** -0.5, C_QA, paired=True)
        through(dka_ref[...].astype(F32), tka_ref[...].astype(F32), 1, 1.0, C_KA)
        through(merged(dq1_ref, dq4_ref, dq16_ref), tqb_ref[...].astype(F32), 2, HEAD_DIM ** -0.5, C_QB)
        through(merged(dk1_ref, dk4_ref, dk16_ref), tkb_ref[...].astype(F32), 3, 1.0, C_KB)
        dproj_ref[:, C_VB:C_VB + HALF_WIDTH] = merged(dv1_ref, dv4_ref, dv16_ref).astype(BF16)
        dproj_ref[:, C_GA:C_GA + HALF_WIDTH] = jnp.concatenate(
            _unpair_tiles(_tiles(dga_ref[...].astype(F32))), axis=1).astype(BF16)
        dproj_ref[:, C_GB:C_GB + HALF_WIDTH] = dgb_ref[...].astype(BF16)
        dproj_ref[:, C_VA:C_VA + KV_A_WIDTH] = dva_ref[...].astype(BF16)

    row = lambda width: pl.BlockSpec((tm, width), lambda i: (i, 0))
    full = lambda a: pl.BlockSpec(a.shape, lambda i: (0,) * a.ndim)
    _, ff_specs = _fold_specs(tm, F32)
    return pl.pallas_call(
        body, name="dproj_assemble", grid=(SEQ // tm,),
        in_specs=[row(HALF_WIDTH), row(KV_A_WIDTH), row(KV_A_WIDTH), row(HALF_WIDTH), row(HALF_WIDTH),
                  row(HALF_WIDTH), row(HALF_WIDTH), row(HALF_WIDTH), ff_specs[0], ff_specs[0], ff_specs[0],
                  ff_specs[1], ff_specs[1], ff_specs[1],
                  row(HALF_WIDTH), row(HALF_WIDTH), row(HALF_WIDTH), row(KV_A_WIDTH),
                  full(qkg), row(PAIR), row(PAIR), full(bmean)],
        out_specs=(row(IN_WIDTH), pl.BlockSpec((SMALL_ROWS, PAIR), lambda i: (0, 0))),
        out_shape=(jax.ShapeDtypeStruct((SEQ, IN_WIDTH), BF16), jax.ShapeDtypeStruct((SMALL_ROWS, PAIR), F32)),
        scratch_shapes=[_fold_scratch(tm)],
        compiler_params=_params(("arbitrary",)),
    )(dqa, dka, dva, dga, dgb, dq1, dk1, dv1, dq4, dk4, dv4, dq16, dk16, dv16, tqa, tqb, tkb, tka, qkg, cos4, sin4, bmean)


def _input_grad_reduce(dproj, w, x, gain, dy, blocks_in, blocks_out, small):
    tm = ROW_TILE
    n_steps = SEQ // tm
    stage2_step = 3
    shapes = (blocks_in.shape[1:], blocks_out.shape[1:])

    def body(dp_ref, w_ref, x_ref, g_ref, dy_ref, ga_hbm, gb_hbm, small_ref,
             gx_ref, out_a, out_b, small_out_ref, dgain_out_ref,
             part_a, part_b, sib_a, sib_b, wire_a, wire_b, chips_a, chips_b, small_all, dgain_acc, dgain_all,
             load_sems, sib_send, sib_recv, chip_send, chip_recv, small_send, small_recv, dgain_send, dgain_recv):
        i = pl.program_id(0)
        x, y, c = lax.axis_index("x"), lax.axis_index("y"), lax.axis_index("c")
        sibling = (x, y, 1 - c)
        chips = [(x, y), (1 - x, y), (x, 1 - y), (1 - x, 1 - y)]
        my_id = 4 * x + 2 * y + c
        g_hbm, part, from_sib = (ga_hbm, gb_hbm), (part_a, part_b), (sib_a, sib_b)
        to_wire, from_chips, out = (wire_a, wire_b), (chips_a, chips_b), (out_a, out_b)
        both = (0, 1)

        def blk(a, chip, core):
            return g_hbm[a].at[4 * chip[0] + 2 * chip[1] + core]

        def to_all(src, dst_all, send, recv):
            copies = []
            for rel in range(1, N_DEV):
                dx, dy_, dc = (rel >> 2) & 1, (rel >> 1) & 1, rel & 1
                to = (1 - x if dx else x, 1 - y if dy_ else y, 1 - c if dc else c)
                copies.append(pltpu.make_async_remote_copy(
                    src_ref=src, dst_ref=dst_all.at[my_id], send_sem=send.at[rel - 1], recv_sem=recv.at[rel - 1],
                    device_id=to, device_id_type=MESH))
            return copies

        small_copies = to_all(small_all.at[my_id], small_all, small_send, small_recv)
        dgain_copies = to_all(dgain_acc, dgain_all, dgain_send, dgain_recv)
        loads = [[pltpu.make_async_copy(blk(a, chips[k], c), part[a].at[k], load_sems.at[a, k]) for k in range(4)] for a in both]
        to_sib = [[pltpu.make_async_remote_copy(
            src_ref=blk(a, chips[k], 1 - c), dst_ref=from_sib[a].at[k], send_sem=sib_send.at[a, k], recv_sem=sib_recv.at[a, k],
            device_id=sibling, device_id_type=MESH) for k in range(4)] for a in both]
        to_chips = [[pltpu.make_async_remote_copy(
            src_ref=to_wire[a].at[k - 1], dst_ref=from_chips[a].at[k - 1],
            send_sem=chip_send.at[a, k - 1], recv_sem=chip_recv.at[a, k - 1],
            device_id=(*chips[k], c), device_id_type=MESH) for k in range(1, 4)] for a in both]

        @pl.when(i == 0)
        def _():
            small_all[my_id] = small_ref[...]
            for cp in small_copies:
                cp.start()
            for k in (1, 2, 3, 0):
                for a in both:
                    loads[a][k].start()
                    to_sib[a][k].start()

        @pl.when(i == stage2_step)
        def _():
            for k in (1, 2, 3):
                for a in both:
                    loads[a][k].wait()
                    to_sib[a][k].wait_recv()
                    to_wire[a][k - 1] = (part[a][k].astype(F32) + from_sib[a][k].astype(F32)).astype(BF16)
                    to_chips[a][k - 1].start()

        dh = lax.dot_general(dp_ref[...], w_ref[...], (((1,), (1,)), ((), ())), preferred_element_type=F32)
        xf = x_ref[...]
        r = lax.rsqrt(jnp.mean(xf * xf, axis=-1, keepdims=True) + EPS)
        xhat = xf * r
        dg = jnp.sum(dh * xhat, axis=0, keepdims=True)
        dxh = dh * g_ref[...]
        dx = r * (dxh - xhat * jnp.mean(dxh * xhat, axis=-1, keepdims=True))
        gx_ref[...] = dy_ref[...] + dx

        @pl.when(i == 0)
        def _():
            dgain_acc[...] = dg

        @pl.when(i > 0)
        def _():
            dgain_acc[...] += dg

        @pl.when(i == n_steps - 1)
        def _():
            dgain_all[my_id] = dgain_acc[...]
            for cp in dgain_copies:
                cp.start()
            for a in both:
                loads[a][0].wait()
                to_sib[a][0].wait_recv()
                acc = part[a][0].astype(F32) + from_sib[a][0].astype(F32)
                for k in range(3):
                    to_chips[a][k].wait_recv()
                    acc = acc + from_chips[a][k].astype(F32)
                out[a][...] = acc
            for copies, gathered, dst in ((small_copies, small_all, small_out_ref), (dgain_copies, dgain_all, dgain_out_ref)):
                for cp in copies:
                    cp.wait_recv()
                tot = gathered[0]
                for d in range(1, N_DEV):
                    tot = tot + gathered[d]
                dst[...] = tot
            for cp in to_sib[0] + to_sib[1] + to_chips[0] + to_chips[1] + small_copies + dgain_copies:
                cp.wait_send()

    row = lambda width: pl.BlockSpec((tm, width), lambda i: (i, 0))
    full = lambda a: pl.BlockSpec(a.shape, lambda i: (0,) * a.ndim)
    whole = lambda shape: pl.BlockSpec(shape, lambda i: (0,) * len(shape))
    hbm = pl.BlockSpec(memory_space=pl.ANY)
    dtypes = (blocks_in.dtype, blocks_out.dtype)
    buf = lambda n, dts: [pltpu.VMEM((n,) + s, dt) for s, dt in zip(shapes, dts)]
    return pl.pallas_call(
        body, name="input_grad_rs", grid=(n_steps,),
        in_specs=[row(IN_WIDTH), full(w), row(D_MODEL), full(gain), row(D_MODEL), hbm, hbm, full(small)],
        out_specs=(row(D_MODEL), whole(shapes[0]), whole(shapes[1]), whole((SMALL_ROWS, SMALL_COLS)), whole((1, D_MODEL))),
        out_shape=(jax.ShapeDtypeStruct((SEQ, D_MODEL), F32), jax.ShapeDtypeStruct(shapes[0], F32),
                   jax.ShapeDtypeStruct(shapes[1], F32), jax.ShapeDtypeStruct((SMALL_ROWS, SMALL_COLS), F32),
                   jax.ShapeDtypeStruct((1, D_MODEL), F32)),
        scratch_shapes=[*buf(4, dtypes), *buf(4, dtypes), *buf(3, (BF16, BF16)), *buf(3, (BF16, BF16)),
                        pltpu.VMEM((N_DEV, SMALL_ROWS, SMALL_COLS), F32),
                        pltpu.VMEM((1, D_MODEL), F32), pltpu.VMEM((N_DEV, 1, D_MODEL), F32),
                        pltpu.SemaphoreType.DMA((2, 4)), pltpu.SemaphoreType.DMA((2, 4)), pltpu.SemaphoreType.DMA((2, 4)),
                        pltpu.SemaphoreType.DMA((2, 3)), pltpu.SemaphoreType.DMA((2, 3)),
                        pltpu.SemaphoreType.DMA((7,)), pltpu.SemaphoreType.DMA((7,)),
                        pltpu.SemaphoreType.DMA((7,)), pltpu.SemaphoreType.DMA((7,))],
        compiler_params=_params(("arbitrary",)),
    )(dproj, w, x, gain, dy, blocks_in, blocks_out, small)


def _weight_grad(h_t, dproj):
    tk = 1024
    cb = IN_WIDTH // 2
    n_k = SEQ // tk

    def body(ht_ref, dp_ref, out_ref, acc):
        k = pl.program_id(1)
        upd = jnp.dot(ht_ref[...], dp_ref[...], preferred_element_type=F32)

        @pl.when(k == 0)
        def _():
            acc[...] = upd

        @pl.when(k > 0)
        def _():
            acc[...] += upd

        @pl.when(k == n_k - 1)
        def _():
            for b in range(N_DEV // 2):
                out_ref[b] = acc[:, b * SHARD_IN:(b + 1) * SHARD_IN].astype(BF16)

    return pl.pallas_call(
        body, name="weight_grad", grid=(2, n_k),
        in_specs=[pl.BlockSpec((D_MODEL, tk), lambda j, k: (0, k)), pl.BlockSpec((tk, cb), lambda j, k: (k, j))],
        out_specs=pl.BlockSpec((N_DEV // 2, D_MODEL, SHARD_IN), lambda j, k: (j, 0, 0)),
        out_shape=jax.ShapeDtypeStruct((N_DEV, D_MODEL, SHARD_IN), BF16),
        scratch_shapes=[pltpu.VMEM((D_MODEL, cb), F32)],
        compiler_params=_params(("arbitrary", "arbitrary")),
    )(h_t, dproj)


def _adamw(name, w, g, m, v):
    def body(w_ref, g_ref, m_ref, v_ref, d_ref, nm_ref, nv_ref):
        gv = g_ref[...]
        nm = ADAM_B1 * m_ref[...] + (1.0 - ADAM_B1) * gv
        nv = ADAM_B2 * v_ref[...] + (1.0 - ADAM_B2) * jnp.square(gv)
        m_hat = nm / (1.0 - ADAM_B1 ** ADAM_STEP)
        v_hat = nv / (1.0 - ADAM_B2 ** ADAM_STEP)
        d_ref[...] = -ADAM_LR * (m_hat / (jnp.sqrt(v_hat) + ADAM_EPS) + ADAM_WD * w_ref[...])
        nm_ref[...] = nm
        nv_ref[...] = nv

    vmem = pl.BlockSpec(memory_space=pltpu.VMEM)
    out = jax.ShapeDtypeStruct(w.shape, F32)
    return pl.pallas_call(
        body, name=name, in_specs=[vmem] * 4, out_specs=(vmem,) * 3, out_shape=(out,) * 3,
        compiler_params=pltpu.CompilerParams(vmem_limit_bytes=VMEM_LIMIT),
    )(w, g, m, v)


SMALL_USED = D_MODEL + 4 * HEAD_DIM + 8


def _pack_small(norm_gain, qa, ka, sinks, qb, kb, extra=None):
    parts = [norm_gain.reshape(-1), qa.reshape(-1), ka.reshape(-1), sinks.reshape(-1), qb.reshape(-1), kb.reshape(-1)]
    if extra is not None:
        parts.append(extra.reshape(-1))
    flat = jnp.concatenate(parts)
    flat = jnp.pad(flat, (0, SMALL_ROWS * SMALL_COLS - flat.shape[0]))
    return flat.reshape(SMALL_ROWS, SMALL_COLS)


def _unpack_small(a):
    flat = a.reshape(-1)
    sizes = (D_MODEL, HEAD_DIM, HEAD_DIM, 8, HEAD_DIM, HEAD_DIM)
    out, off = [], 0
    for s in sizes:
        out.append(flat[off:off + s].reshape(1, s))
        off += s
    return out


def _fold_heads(row):
    return row[0, :HEAD_DIM] + row[0, HEAD_DIM:]


def kernel(x, norm_gain, w_in, q_norm_a, k_norm_a, sinks_a, q_norm_b, k_norm_b, w_out, loss_target, m_norm_gain, m_w_in, m_q_norm_a, m_k_norm_a, m_sinks_a, m_q_norm_b, m_k_norm_b, m_w_out, v_norm_gain, v_w_in, v_q_norm_a, v_k_norm_a, v_sinks_a, v_q_norm_b, v_k_norm_b, v_w_out):
    x2, tgt = x[0], loss_target[0]
    w_in_sh, w_out_sh = w_in[0], w_out[0]

    w_full = _all_gather_w_in(w_in_sh)

    inv = np.float32(ROPE_THETA) ** (-np.arange(HEAD_DIM // 2, dtype=np.float32) / np.float32(HEAD_DIM // 2))
    ang = np.arange(SEQ, dtype=np.float32)[:, None] * inv[None, :].astype(np.float32)
    cos, sin = np.cos(ang).astype(np.float32), np.sin(ang).astype(np.float32)
    cos4 = jnp.asarray(np.concatenate([cos, cos, cos, cos], axis=1))
    sin4 = jnp.asarray(np.concatenate([-sin, sin, -sin, sin], axis=1))
    blockdiag = np.kron(np.eye(2, dtype=np.float32), np.ones((HEAD_DIM, HEAD_DIM), np.float32))
    bmean = jnp.asarray(blockdiag / HEAD_DIM, dtype=BF16)
    gather_np = np.kron(np.eye(2 * N_PAIRS, dtype=np.float32), np.ones((HEAD_DIM, STAT_REP), np.float32))
    spread_np = np.kron(np.eye(2 * N_PAIRS, dtype=np.float32), np.ones((STAT_REP, HEAD_DIM), np.float32))
    spread_np[np.arange(STAT_WIDTH) % STAT_REP != 0] = 0.0
    gather, spread = jnp.asarray(gather_np, dtype=BF16), jnp.asarray(spread_np, dtype=BF16)
    two = lambda g: jnp.concatenate([g, g], axis=1)
    qkg = jnp.concatenate([two(q_norm_a), two(k_norm_a), two(q_norm_b), two(k_norm_b),
                           jnp.zeros((SMALL_ROWS - 4, PAIR), F32)], axis=0)
    sinks_paired = jnp.stack([sinks_a[0, :N_PAIRS], sinks_a[0, N_PAIRS:]], axis=1)
    sink_rows = jnp.concatenate([jnp.repeat(sinks_paired, BLOCK, axis=1),
                                 jnp.zeros((SMALL_ROWS - N_PAIRS, 2 * BLOCK), F32)], axis=0)

    (tqa, tka, tqb, tkb, gate_a, gate_b, h_t, qa, ka, va, qb, kb, vb, qb4, qb16, kb4, kb16, vb4, vb16,
     gathered_out) = _proj_fwd(x2, norm_gain, w_full, qkg, cos4, sin4, bmean, w_out_sh)
    wo_full = gathered_out.reshape(D_MODEL, D_MODEL)
    oa, la = _attn_fwd("attn_a_fwd", qa[None], ka[None], va[None], sink_rows, BLOCK - 1)
    ob1, lb1 = _attn_fwd("attn_b1_fwd", qb[None], kb[None], vb[None], None, BLOCK)
    ob4, lb4 = _attn_fwd("attn_b4_fwd", qb4, kb4, vb4, None, BLOCK)
    ob16, lb16 = _attn_fwd("attn_b16_fwd", qb16, kb16, vb16, None, BLOCK)
    (loss_cols, dy, gwo, doa, dla, dga, dgb, dob, dob4, dob16, dlb, dlb4, dlb16, lse_b, lse4, lse16) = _tail(
        oa[0], ob1[0], lb1[0], ob4, lb4, ob16, lb16, gate_a, gate_b, x2, tgt, wo_full, spread, gather)

    dqa, dka, dva, dsink = _attn_bwd("attn_a_bwd", qa[None], ka[None], va[None], doa[None], la, dla[None], sink_rows, BLOCK - 1)
    dq1, dk1, dv1 = _attn_bwd("attn_b1_bwd", qb[None], kb[None], vb[None], dob[None], lse_b[None], dlb[None], None, BLOCK)
    dq4, dk4, dv4 = _attn_bwd("attn_b4_bwd", qb4, kb4, vb4, dob4, lse4, dlb4, None, BLOCK)
    dq16, dk16, dv16 = _attn_bwd("attn_b16_bwd", qb16, kb16, vb16, dob16, lse16, dlb16, None, BLOCK)
    dproj, dqkg = _dproj_assemble(dqa[0], dka[0], dva[0], dga, dgb, dq1[0], dk1[0], dv1[0], dq4, dk4, dv4,
                                  dq16, dk16, dv16, tqa, tqb, tkb, tka, qkg, cos4, sin4, bmean)
    gw_in = _weight_grad(h_t, dproj)

    blocks_in = gw_in
    blocks_out = gwo.reshape(N_DEV, SHARD_OUT, D_MODEL)
    g_sinks = jnp.concatenate([jnp.sum(dsink[:N_PAIRS, :BLOCK], axis=1), jnp.sum(dsink[:N_PAIRS, BLOCK:], axis=1)])
    small = _pack_small(jnp.zeros((D_MODEL,), F32), _fold_heads(dqkg[0:1]), _fold_heads(dqkg[1:2]), g_sinks,
                        _fold_heads(dqkg[2:3]), _fold_heads(dqkg[3:4]), extra=0.5 * jnp.sum(loss_cols) / D_MODEL)
    grad_x, g_w_in, g_w_out, small_red, dgain_red = _input_grad_reduce(
        dproj, w_full, x2, norm_gain, dy, blocks_in, blocks_out, small)
    n_gain_rows = D_MODEL // SMALL_COLS
    small_red = jnp.concatenate([dgain_red.reshape(n_gain_rows, SMALL_COLS), small_red[n_gain_rows:]], axis=0)
    g_small = _unpack_small(small_red)

    d_in, nm_in, nv_in = _adamw("adamw_w_in", w_in_sh, g_w_in, m_w_in[0], v_w_in[0])
    d_out, nm_out, nv_out = _adamw("adamw_w_out", w_out_sh, g_w_out, m_w_out[0], v_w_out[0])
    d_s, nm_s, nv_s = _adamw(
        "adamw_small",
        _pack_small(norm_gain, q_norm_a, k_norm_a, sinks_a, q_norm_b, k_norm_b), small_red,
        _pack_small(m_norm_gain, m_q_norm_a, m_k_norm_a, m_sinks_a, m_q_norm_b, m_k_norm_b),
        _pack_small(v_norm_gain, v_q_norm_a, v_k_norm_a, v_sinks_a, v_q_norm_b, v_k_norm_b))
    d_small, nm_small, nv_small = _unpack_small(d_s), _unpack_small(nm_s), _unpack_small(nv_s)

    loss = small_red.reshape(-1)[SMALL_USED]

    def assemble(small_list, big_in, big_out):
        ng, qa_, ka_, sk_, qb_, kb_ = small_list
        return [ng, big_in[None], qa_, ka_, sk_, qb_, kb_, big_out[None]]

    return (loss, grad_x[None], *assemble(g_small, g_w_in, g_w_out), *assemble(d_small, d_in, d_out),
            *assemble(nm_small, nm_in, nm_out), *assemble(nv_small, nv_in, nv_out))
```

```python
import functools

import numpy as np
import jax
import jax.numpy as jnp
from jax import lax
from jax.experimental import pallas as pl
from jax.experimental.pallas import tpu as pltpu

F32 = jnp.float32
BF16 = jnp.bfloat16

SEQ = 4096
D_MODEL = 1024
HEAD_DIM = 64
PAIR = 2 * HEAD_DIM
N_PAIRS = 4
HALF_WIDTH = N_PAIRS * PAIR
KV_A_WIDTH = 128
IN_WIDTH = 3328
BLOCK = 128
STAT_REP = 16
STAT_WIDTH = 128
EPS = 1e-6
NEG = -1e30
ROPE_THETA = 10000.0
N_DEV = 8
SHARD_IN = IN_WIDTH // N_DEV
SHARD_OUT = D_MODEL // N_DEV
PAYLOAD = SHARD_IN + SHARD_OUT
SMALL_ROWS, SMALL_COLS = 8, 256

C_QA, C_KA, C_VA, C_GA, C_QB, C_KB, C_VB, C_GB = 0, 512, 640, 768, 1280, 1792, 2304, 2816

ADAM_LR = 0.001
ADAM_B1 = 0.9
ADAM_B2 = 0.999
ADAM_EPS = 1e-08
ADAM_WD = 0.01
ADAM_STEP = 10

ROW_TILE = 256
PROJ_ROW_TILE = 512
FWD_BLOCKS_PER_STEP = 4
BWD_BLOCKS_PER_STEP = 4
VMEM_LIMIT = 56 * 1024 * 1024

MESH = pl.DeviceIdType.MESH


def _params(sem, vmem=VMEM_LIMIT):
    return pltpu.CompilerParams(dimension_semantics=sem, vmem_limit_bytes=vmem)


def _head_sum(v, bm):
    hi = v.astype(BF16)
    lo = (v - hi.astype(F32)).astype(BF16)
    return (jnp.dot(hi, bm, preferred_element_type=F32) + jnp.dot(lo, bm, preferred_element_type=F32))


def _swap_halves(y):
    lane = lax.broadcasted_iota(jnp.int32, y.shape, 1)
    first = (lane & 32) == 0
    return jnp.where(first, pltpu.roll(y, 96, 1), pltpu.roll(y, 32, 1))


def _sigmoid(g):
    return 1.0 / (1.0 + jnp.exp(-g))


def _tiles(a):
    return [a[:, j * PAIR:(j + 1) * PAIR] for j in range(N_PAIRS)]


def _pair_tiles(t):
    low = lax.broadcasted_iota(jnp.int32, t[0].shape, 1) < HEAD_DIM
    r = [pltpu.roll(a, HEAD_DIM, 1) for a in t]
    return [jnp.where(low, t[0], r[2]), jnp.where(low, r[0], t[2]), jnp.where(low, t[1], r[3]), jnp.where(low, r[1], t[3])]


def _unpair_tiles(p):
    low = lax.broadcasted_iota(jnp.int32, p[0].shape, 1) < HEAD_DIM
    r = [pltpu.roll(a, HEAD_DIM, 1) for a in p]
    return [jnp.where(low, p[0], r[1]), jnp.where(low, p[2], r[3]), jnp.where(low, r[0], p[1]), jnp.where(low, r[2], p[3])]


def _routes():
    x, y, c = lax.axis_index("x"), lax.axis_index("y"), lax.axis_index("c")
    north = c == 1
    first = (jnp.where(north, 1 - x, x), jnp.where(north, y, 1 - y), c)
    other = (jnp.where(north, x, 1 - x), jnp.where(north, 1 - y, y), c)
    k_first = jnp.where(north, 1, 2)
    return first, other, k_first, 3 - k_first


def _gather_plan(mine_ref, out_ref, send_sems, recv_sems):
    x, y, c = lax.axis_index("x"), lax.axis_index("y"), lax.axis_index("c")
    me, sibling, diag = (x, y, c), (x, y, 1 - c), (1 - x, 1 - y, c)
    first, other, k_first, k_other = _routes()

    def slot(px, py, pc):
        return out_ref.at[4 * px + 2 * py + pc]

    def copy(k, block, to, from_mine=False):
        return pltpu.make_async_remote_copy(
            src_ref=mine_ref if from_mine else slot(*block), dst_ref=slot(*block),
            send_sem=send_sems.at[k], recv_sem=recv_sems.at[k], device_id=to, device_id_type=MESH)

    sends = [copy(0, me, sibling, True), copy(1, me, (1 - x, y, c), True), copy(2, me, (x, 1 - y, c), True)]
    stages = [(copy(k_first, first, me), [copy(3, first, other), copy(3 + k_first, first, sibling)]),
              (copy(k_other, other, me), [copy(3 + k_other, other, sibling)]),
              (copy(3, diag, me), [copy(6, diag, sibling)])]
    from_sibling = [copy(0, sibling, me), copy(4, (1 - x, y, 1 - c), me), copy(5, (x, 1 - y, 1 - c), me),
                    copy(6, (1 - x, 1 - y, 1 - c), me)]
    return slot(*me), sends, stages, from_sibling


GATHER_SCRATCH = [pltpu.SemaphoreType.DMA((7,)), pltpu.SemaphoreType.DMA((7,))]


def _all_gather_w_in(w_in_sh):
    rows, cols = w_in_sh.shape

    def body(w_ref, out_ref, mine_ref, blocks, send_sems, recv_sems):
        mine_ref[...] = w_ref[...].astype(BF16)
        my_slot, sends, stages, from_sibling = _gather_plan(mine_ref, blocks, send_sems, recv_sems)
        for cp in sends:
            cp.start()
        my_slot[...] = mine_ref[...]
        for arrival, forwards in stages:
            arrival.wait_recv()
            for cp in forwards:
                cp.start()
        for arrival in from_sibling:
            arrival.wait_recv()
        for cp in sends + [cp for _, forwards in stages for cp in forwards]:
            cp.wait_send()
        for d in range(N_DEV):
            out_ref[:, d * cols:(d + 1) * cols] = blocks[d]

    vmem = pl.BlockSpec(memory_space=pltpu.VMEM)
    return pl.pallas_call(
        body, name="ag_w_in",
        out_shape=jax.ShapeDtypeStruct((rows, N_DEV * cols), BF16),
        in_specs=[vmem], out_specs=vmem,
        scratch_shapes=[pltpu.VMEM((rows, cols), BF16), pltpu.VMEM((N_DEV, rows, cols), BF16)] + GATHER_SCRATCH,
        compiler_params=pltpu.CompilerParams(vmem_limit_bytes=VMEM_LIMIT),
    )(w_in_sh)


def _fold_scratch(tm):
    return pltpu.VMEM((N_PAIRS, tm, PAIR), F32)


def _fold_store(val, scr, out4, out16, tm):
    groups = range(val.shape[1] // PAIR)
    for j in groups:
        scr[j] = val[:, j * PAIR:(j + 1) * PAIR]
    for dil, out in ((4, out4), (16, out16)):
        for r in range(dil):
            for j in groups:
                out[r, :, j * PAIR:(j + 1) * PAIR] = scr[j, pl.ds(r, tm // dil, stride=dil), :].astype(out.dtype)


def _unfold_load(src, scr, dil, tm):
    groups = range(src.shape[2] // PAIR)
    for r in range(dil):
        for j in groups:
            scr[j, pl.ds(r, tm // dil, stride=dil), :] = src[r, :, j * PAIR:(j + 1) * PAIR].astype(F32)
    return jnp.concatenate([scr[j] for j in groups], axis=1)


def _fold_specs(tm, dtype, width=HALF_WIDTH):
    shapes = (jax.ShapeDtypeStruct((4, SEQ // 4, width), dtype), jax.ShapeDtypeStruct((16, SEQ // 16, width), dtype))
    specs = (pl.BlockSpec((4, tm // 4, width), lambda i: (0, i, 0)),
             pl.BlockSpec((16, tm // 16, width), lambda i: (0, i, 0)))
    return shapes, specs


def _proj_fwd(x, gain, w, qkg, cos4, sin4, bmean, w_out_sh):
    tm = PROJ_ROW_TILE
    n_steps = SEQ // tm

    def norm_rope(t, g, cos, sin, bm, scale):
        rr = lax.rsqrt(_head_sum(t * t, bm) + EPS)
        yv = t * rr * g
        return (yv * cos + _swap_halves(yv) * sin) * scale

    def body(x_ref, g_ref, w_ref, qkg_ref, cos_ref, sin_ref, bm_ref, wo_ref,
             tqa_ref, tka_ref, tqb_ref, tkb_ref, ga_ref, gb_ref, ht_ref, qa_ref, ka_ref, va_ref, qb_ref, kb_ref, vb_ref,
             qb4_ref, qb16_ref, kb4_ref, kb16_ref, vb4_ref, vb16_ref, wo_all_ref,
             proj, scr, wo_mine, wo_all, send_sems, recv_sems):
        i = pl.program_id(0)
        my_slot, sends, stages, from_sibling = _gather_plan(wo_mine, wo_all, send_sems, recv_sems)

        @pl.when(i == 0)
        def _():
            wo_mine[...] = wo_ref[...].astype(BF16)
            for cp in sends:
                cp.start()
            my_slot[...] = wo_mine[...]

        @pl.when(i == n_steps // 2)
        def _():
            for arrival, forwards in stages[:2]:
                arrival.wait_recv()
                for cp in forwards:
                    cp.start()

        xf = x_ref[...]
        r = lax.rsqrt(jnp.mean(xf * xf, axis=-1, keepdims=True) + EPS)
        hf = xf * r * g_ref[...]
        ht_ref[...] = hf.T.astype(BF16)
        cos, sin, bm = cos_ref[...], sin_ref[...], bm_ref[...]
        proj[...] = jnp.dot(hf.astype(BF16), w_ref[...], preferred_element_type=F32)

        def roped(tiles, row, scale):
            g = qkg_ref[row:row + 1, :]
            return jnp.concatenate([norm_rope(t, g, cos, sin, bm, scale) for t in tiles], axis=1)

        tqa = _pair_tiles(_tiles(proj[:, C_QA:C_QA + HALF_WIDTH]))
        tqa_ref[...] = jnp.concatenate(tqa, axis=1).astype(BF16)
        qa_ref[...] = roped(tqa, 0, HEAD_DIM ** -0.5).astype(BF16)
        ga_ref[...] = jnp.concatenate(_pair_tiles(_tiles(proj[:, C_GA:C_GA + HALF_WIDTH])), axis=1).astype(BF16)
        gb_ref[...] = proj[:, C_GB:C_GB + HALF_WIDTH].astype(BF16)
        tqb = proj[:, C_QB:C_QB + HALF_WIDTH]
        tqb_ref[...] = tqb.astype(BF16)
        qb = roped(_tiles(tqb), 2, HEAD_DIM ** -0.5)
        qb_ref[...] = qb.astype(BF16)
        _fold_store(qb, scr, qb4_ref, qb16_ref, tm)
        tkb = proj[:, C_KB:C_KB + HALF_WIDTH]
        tkb_ref[...] = tkb.astype(BF16)
        kb = roped(_tiles(tkb), 3, 1.0)
        kb_ref[...] = kb.astype(BF16)
        _fold_store(kb, scr, kb4_ref, kb16_ref, tm)
        vb = proj[:, C_VB:C_VB + HALF_WIDTH]
        vb_ref[...] = vb.astype(BF16)
        _fold_store(vb, scr, vb4_ref, vb16_ref, tm)
        tka = proj[:, C_KA:C_KA + KV_A_WIDTH]
        tka_ref[...] = tka.astype(BF16)
        ka_ref[...] = roped([tka], 1, 1.0).astype(BF16)
        va_ref[...] = proj[:, C_VA:C_VA + KV_A_WIDTH].astype(BF16)

        @pl.when(i == n_steps - 1)
        def _():
            arrival, forwards = stages[2]
            arrival.wait_recv()
            for cp in forwards:
                cp.start()
            for arrival in from_sibling:
                arrival.wait_recv()
            for cp in sends + [cp for _, forwards in stages for cp in forwards]:
                cp.wait_send()
            wo_all_ref[...] = wo_all[...]

    row = lambda width: pl.BlockSpec((tm, width), lambda i: (i, 0))
    full = lambda a: pl.BlockSpec(a.shape, lambda i: (0,) * a.ndim)
    nat = lambda width, dtype=BF16: jax.ShapeDtypeStruct((SEQ, width), dtype)
    f_shapes, f_specs = _fold_specs(tm, BF16)
    return pl.pallas_call(
        body, name="proj_fwd", grid=(SEQ // tm,),
        in_specs=[row(D_MODEL), full(gain), full(w), full(qkg), row(PAIR), row(PAIR), full(bmean), full(w_out_sh)],
        out_specs=(row(HALF_WIDTH), row(KV_A_WIDTH), row(HALF_WIDTH), row(HALF_WIDTH), row(HALF_WIDTH), row(HALF_WIDTH),
                   pl.BlockSpec((D_MODEL, tm), lambda i: (0, i)),
                   row(HALF_WIDTH), row(KV_A_WIDTH), row(KV_A_WIDTH), row(HALF_WIDTH), row(HALF_WIDTH), row(HALF_WIDTH),
                   *f_specs, *f_specs, *f_specs,
                   pl.BlockSpec((N_DEV,) + w_out_sh.shape, lambda i: (0, 0, 0))),
        out_shape=(nat(HALF_WIDTH), nat(KV_A_WIDTH), nat(HALF_WIDTH), nat(HALF_WIDTH), nat(HALF_WIDTH), nat(HALF_WIDTH),
                   jax.ShapeDtypeStruct((D_MODEL, SEQ), BF16),
                   nat(HALF_WIDTH), nat(KV_A_WIDTH), nat(KV_A_WIDTH), nat(HALF_WIDTH), nat(HALF_WIDTH), nat(HALF_WIDTH),
                   *f_shapes, *f_shapes, *f_shapes,
                   jax.ShapeDtypeStruct((N_DEV,) + w_out_sh.shape, BF16)),
        scratch_shapes=[pltpu.VMEM((tm, IN_WIDTH), F32), _fold_scratch(tm), pltpu.VMEM(w_out_sh.shape, BF16),
                        pltpu.VMEM((N_DEV,) + w_out_sh.shape, BF16)] + GATHER_SCRATCH,
        compiler_params=_params(("arbitrary",)),
    )(x, gain, w, qkg, cos4, sin4, bmean, w_out_sh)


def _band_mask(i, max_dist):
    j = lax.broadcasted_iota(jnp.int32, (2 * BLOCK, 2 * BLOCK), 0)
    c = lax.broadcasted_iota(jnp.int32, (2 * BLOCK, 2 * BLOCK), 1)
    dist = (c & (BLOCK - 1)) + BLOCK - j
    return (dist >= 0) & (dist <= max_dist) & ((j >= BLOCK) | (i > 0))


def _stack_heads(t):
    lane = lax.broadcasted_iota(jnp.int32, t.shape, 1)
    low = lane < HEAD_DIM
    zero = jnp.zeros_like(t)
    return jnp.concatenate([jnp.where(low, t, zero), jnp.where(low, zero, t)], axis=0)


def _unstack_t(t):
    return jnp.concatenate([t[:HEAD_DIM, :BLOCK], t[HEAD_DIM:, BLOCK:]], axis=0).T


def _rows_to_stats(rows):
    parts = []
    for row in rows:
        parts.append(jnp.broadcast_to(row[:, :BLOCK], (STAT_REP, BLOCK)))
        parts.append(jnp.broadcast_to(row[:, BLOCK:], (STAT_REP, BLOCK)))
    return jnp.concatenate(parts, axis=0).T


def _stats_to_rows(t):
    tt = t.T
    return [jnp.concatenate([tt[2 * p * STAT_REP:2 * p * STAT_REP + 1, :],
                             tt[(2 * p + 1) * STAT_REP:(2 * p + 1) * STAT_REP + 1, :]], axis=1) for p in range(N_PAIRS)]


def _attn_fwd(name, q, k, v, sink_rows, max_dist):
    n_seq, length, _ = q.shape
    ck = k.shape[2]
    nb = length // BLOCK
    shared = ck == PAIR
    has_sinks = sink_rows is not None

    qb = FWD_BLOCKS_PER_STEP

    def body(*refs):
        if has_sinks:
            q_ref, kc_ref, vc_ref, sink_ref, o_ref, lse_ref, kp_ref, vp_ref = refs
        else:
            q_ref, kc_ref, vc_ref, o_ref, lse_ref, kp_ref, vp_ref = refs
        step = pl.program_id(0)

        @pl.when(step == 0)
        def _():
            kp_ref[...] = jnp.zeros_like(kp_ref)
            vp_ref[...] = jnp.zeros_like(vp_ref)

        valid = [_band_mask((step * qb + b) & (nb - 1), max_dist) for b in range(qb)]
        cols = [slice(p * PAIR, (p + 1) * PAIR) for p in range(N_PAIRS)]
        kcols = [slice(0, PAIR) if shared else c for c in cols]
        rows = [slice(b * BLOCK, (b + 1) * BLOCK) for b in range(qb)]
        units = [(b, p) for b in range(qb) for p in range(N_PAIRS)]
        n = range(len(units))

        def window(prev_ref, cur_ref, b, kc):
            before = prev_ref[:, kc] if b == 0 else cur_ref[rows[b - 1], kc]
            return jnp.concatenate([before, cur_ref[rows[b], kc]], axis=0)

        st = [lax.dot_general(window(kp_ref, kc_ref, b, kcols[p]), _stack_heads(q_ref[rows[b], cols[p]]),
                              (((1,), (1,)), ((), ())), preferred_element_type=F32) for b, p in units]
        st = [jnp.where(valid[units[u][0]], st[u], NEG) for u in n]
        m = [jnp.max(s, axis=0, keepdims=True) for s in st]
        if has_sinks:
            sk = [sink_ref[p:p + 1, :] for _, p in units]
            m = [jnp.maximum(m[u], sk[u]) for u in n]
        pt = [jnp.exp(st[u] - m[u]) for u in n]
        l = [jnp.sum(t, axis=0, keepdims=True) for t in pt]
        if has_sinks:
            l = [l[u] + jnp.exp(sk[u] - m[u]) for u in n]
        v2t = [window(vp_ref, vc_ref, b, kcols[p]).astype(F32).T.astype(BF16) for b, p in units]
        ot = [jnp.dot(v2t[u], pt[u].astype(BF16), preferred_element_type=F32) / l[u] for u in n]
        for u, (b, p) in enumerate(units):
            o_ref[rows[b], cols[p]] = _unstack_t(ot[u]).astype(BF16)
        for b in range(qb):
            lse_ref[rows[b], :] = _rows_to_stats([m[u] + jnp.log(l[u]) for u in n if units[u][0] == b])
        kp_ref[...] = kc_ref[rows[-1], :]
        vp_ref[...] = vc_ref[rows[-1], :]

    cur = lambda width: pl.BlockSpec((qb * BLOCK, width), lambda s: (s, 0))
    flat = lambda a: a.reshape(n_seq * length, a.shape[2])
    in_specs = [cur(HALF_WIDTH), cur(ck), cur(ck)]
    args = [flat(q), flat(k), flat(v)]
    if has_sinks:
        in_specs.append(pl.BlockSpec(sink_rows.shape, lambda s: (0, 0)))
        args.append(sink_rows)
    o, lse = pl.pallas_call(
        body, name=name, grid=(n_seq * nb // qb,), in_specs=in_specs,
        out_specs=(cur(HALF_WIDTH), cur(STAT_WIDTH)),
        out_shape=(jax.ShapeDtypeStruct((n_seq * length, HALF_WIDTH), BF16),
                   jax.ShapeDtypeStruct((n_seq * length, STAT_WIDTH), F32)),
        scratch_shapes=[pltpu.VMEM((BLOCK, ck), BF16), pltpu.VMEM((BLOCK, ck), BF16)],
        compiler_params=_params(("arbitrary",)),
    )(*args)
    return o.reshape(n_seq, length, HALF_WIDTH), lse.reshape(n_seq, length, STAT_WIDTH)


def _attn_bwd(name, q, k, v, d_o, lse, delta, sink_rows, max_dist):
    n_seq, length, _ = q.shape
    ck = k.shape[2]
    nb = length // BLOCK
    n_blocks = n_seq * nb
    n_rows = n_seq * length
    shared = ck == PAIR
    has_sinks = sink_rows is not None
    qb = BWD_BLOCKS_PER_STEP
    n_steps = n_blocks // qb

    def body(*refs):
        if has_sinks:
            (q_ref, kc_ref, vc_ref, do_ref, lse_ref, dl_ref, sink_ref,
             dq_ref, dk_ref, dv_ref, dsink_ref, ck_scr, cv_scr, kp_ref, vp_ref) = refs
        else:
            (q_ref, kc_ref, vc_ref, do_ref, lse_ref, dl_ref,
             dq_ref, dk_ref, dv_ref, ck_scr, cv_scr, kp_ref, vp_ref) = refs
        step = pl.program_id(0)

        @pl.when(step == 0)
        def _():
            ck_scr[...] = jnp.zeros_like(ck_scr)
            cv_scr[...] = jnp.zeros_like(cv_scr)
            kp_ref[...] = jnp.zeros_like(kp_ref)
            vp_ref[...] = jnp.zeros_like(vp_ref)
            if has_sinks:
                dsink_ref[...] = jnp.zeros_like(dsink_ref)

        valid = [_band_mask((step * qb + b) & (nb - 1), max_dist) for b in range(qb)]
        cols = [slice(p * PAIR, (p + 1) * PAIR) for p in range(N_PAIRS)]
        kcols = [slice(0, PAIR) if shared else c for c in cols]
        rows = [slice(b * BLOCK, (b + 1) * BLOCK) for b in range(qb)]
        units = [(b, p) for b in range(qb) for p in range(N_PAIRS)]
        n = range(len(units))
        nt = (((1,), (1,)), ((), ()))

        def window(prev_ref, cur_ref, b, kc):
            before = prev_ref[:, kc] if b == 0 else cur_ref[rows[b - 1], kc]
            return jnp.concatenate([before, cur_ref[rows[b], kc]], axis=0)

        q_st = [_stack_heads(q_ref[rows[b], cols[p]]) for b, p in units]
        do_st = [_stack_heads(do_ref[rows[b], cols[p]]) for b, p in units]
        k2 = [window(kp_ref, kc_ref, b, kcols[p]) for b, p in units]
        v2 = [window(vp_ref, vc_ref, b, kcols[p]) for b, p in units]
        st = [lax.dot_general(k2[u], q_st[u], nt, preferred_element_type=F32) for u in n]
        dpt = [lax.dot_general(v2[u], do_st[u], nt, preferred_element_type=F32) for u in n]
        lse_rows = [_stats_to_rows(lse_ref[rows[b], :]) for b in range(qb)]
        dl_rows = [_stats_to_rows(dl_ref[rows[b], :]) for b in range(qb)]
        lse_row = [lse_rows[b][p] for b, p in units]
        dl_row = [dl_rows[b][p] for b, p in units]
        pt = [jnp.exp(jnp.where(valid[units[u][0]], st[u], NEG) - lse_row[u]) for u in n]
        dst = [(pt[u] * (dpt[u] - dl_row[u])).astype(BF16) for u in n]
        ptb = [t.astype(BF16) for t in pt]
        dv2 = [jnp.dot(ptb[u], do_st[u], preferred_element_type=F32) for u in n]
        dk2 = [jnp.dot(dst[u], q_st[u], preferred_element_type=F32) for u in n]
        k2t = [k2[u].astype(F32).T.astype(BF16) for u in n]
        dqt = [jnp.dot(k2t[u], dst[u], preferred_element_type=F32) for u in n]
        for u, (b, p) in enumerate(units):
            dq_ref[rows[b], cols[p]] = _unstack_t(dqt[u]).astype(BF16)
        if has_sinks:
            for u, (b, p) in enumerate(units):
                p_sink = jnp.exp(sink_ref[p:p + 1, :] - lse_row[u])
                dsink_ref[p:p + 1, :] = dsink_ref[p:p + 1, :] - p_sink * dl_row[u]

        def total(parts, w, group):
            sel = [u for u, (b, p) in enumerate(units) if (shared or p == group)]
            terms = ([parts[u][:BLOCK] for u in sel if units[u][0] == w]
                     + [parts[u][BLOCK:] for u in sel if units[u][0] == w - 1])
            tot = terms[0]
            for t in terms[1:]:
                tot = tot + t
            return tot

        first_row = step * (qb * BLOCK)
        for acc_ref, out_ref, parts in ((ck_scr, dk_ref, dk2), (cv_scr, dv_ref, dv2)):
            for group in range(1 if shared else N_PAIRS):
                kc = kcols[group]

                @pl.when(step > 0)
                def _():
                    out_ref[pl.ds(pl.multiple_of(first_row - BLOCK, BLOCK), BLOCK), kc] = (
                        acc_ref[:, kc] + total(parts, 0, group)).astype(BF16)

                for w in range(1, qb):
                    out_ref[pl.ds(pl.multiple_of(first_row + (w - 1) * BLOCK, BLOCK), BLOCK), kc] = (
                        total(parts, w, group).astype(BF16))
                acc_ref[:, kc] = total(parts, qb, group)

        @pl.when(step == n_steps - 1)
        def _():
            dk_ref[pl.ds(n_rows - BLOCK, BLOCK), :] = ck_scr[...].astype(BF16)
            dv_ref[pl.ds(n_rows - BLOCK, BLOCK), :] = cv_scr[...].astype(BF16)

        kp_ref[...] = kc_ref[rows[-1], :]
        vp_ref[...] = vc_ref[rows[-1], :]

    cur = lambda width: pl.BlockSpec((qb * BLOCK, width), lambda s: (s, 0))
    whole = lambda width: pl.BlockSpec((n_rows, width), lambda s: (0, 0))
    flat = lambda a: a.reshape(n_rows, a.shape[2])
    in_specs = [cur(HALF_WIDTH), cur(ck), cur(ck), cur(HALF_WIDTH), cur(STAT_WIDTH), cur(STAT_WIDTH)]
    args = [flat(a) for a in (q, k, v, d_o, lse, delta)]
    out_specs = [cur(HALF_WIDTH), whole(ck), whole(ck)]
    out_shape = [jax.ShapeDtypeStruct((n_rows, HALF_WIDTH), BF16),
                 jax.ShapeDtypeStruct((n_rows, ck), BF16), jax.ShapeDtypeStruct((n_rows, ck), BF16)]
    if has_sinks:
        in_specs.append(pl.BlockSpec(sink_rows.shape, lambda s: (0, 0)))
        args.append(sink_rows)
        out_specs.append(pl.BlockSpec(sink_rows.shape, lambda s: (0, 0)))
        out_shape.append(jax.ShapeDtypeStruct(sink_rows.shape, F32))
    outs = pl.pallas_call(
        body, name=name, grid=(n_steps,), in_specs=in_specs,
        out_specs=tuple(out_specs), out_shape=tuple(out_shape),
        scratch_shapes=[pltpu.VMEM((BLOCK, ck), F32), pltpu.VMEM((BLOCK, ck), F32),
                        pltpu.VMEM((BLOCK, ck), BF16), pltpu.VMEM((BLOCK, ck), BF16)],
        compiler_params=_params(("arbitrary",)),
    )(*args)
    return tuple(o.reshape(n_seq, length, o.shape[1]) for o in outs[:3]) + tuple(outs[3:])


def _tail(oa, ob1, lb1, ob4, lb4, ob16, lb16, gate_a, gate_b, x, target, w_out, spread, gather):
    tm = ROW_TILE

    def split_dot(v, mat):
        hi = v.astype(BF16)
        lo = (v - hi.astype(F32)).astype(BF16)
        return jnp.dot(hi, mat, preferred_element_type=F32) + jnp.dot(lo, mat, preferred_element_type=F32)

    def body(oa_ref, ob1_ref, lb1_ref, ob4_ref, lb4_ref, ob16_ref, lb16_ref, ga_ref, gb_ref, x_ref, t_ref, w_ref,
             sp_ref, ga_mat_ref,
             loss_ref, dy_ref, gwo_ref, doa_ref, dla_ref, dga_ref, dgb_ref,
             dob_ref, dob4_ref, dob16_ref, dlb_ref, dlb4_ref, dlb16_ref, lse_ref, lse4_ref, lse16_ref,
             s_f, mix_keep, dy_keep):
        i = pl.program_id(0)
        sp, gat = sp_ref[...], ga_mat_ref[...]
        o4, o16 = _unfold_load(ob4_ref, s_f, 4, tm), _unfold_load(ob16_ref, s_f, 16, tm)
        l4, l16 = _unfold_load(lb4_ref, s_f, 4, tm), _unfold_load(lb16_ref, s_f, 16, tm)
        o1, l1 = ob1_ref[...].astype(F32), lb1_ref[...]
        mx = jnp.maximum(jnp.maximum(l1, l4), l16)
        e1, e4, e16 = jnp.exp(l1 - mx), jnp.exp(l4 - mx), jnp.exp(l16 - mx)
        den = e1 + e4 + e16
        inv = 1.0 / den
        ob = split_dot(e1 * inv, sp) * o1 + split_dot(e4 * inv, sp) * o4 + split_dot(e16 * inv, sp) * o16
        lse_b = mx + jnp.log(den)

        oa, ga, gb = oa_ref[...].astype(F32), ga_ref[...].astype(F32), gb_ref[...].astype(F32)
        sa, sb = _sigmoid(ga), _sigmoid(gb)
        mixed = jnp.concatenate(_unpair_tiles(_tiles(oa * (ga * sa))) + [ob * (gb * sb)], axis=1)
        mixed_bf = mixed.astype(BF16)
        w = w_ref[...]
        yv = x_ref[...] + jnp.dot(mixed_bf, w, preferred_element_type=F32)
        err = yv - t_ref[...]
        sq = jnp.sum(err * err, axis=0, keepdims=True)
        dy = err * (1.0 / D_MODEL)
        dy_ref[...] = dy
        dy_bf = dy.astype(BF16)
        mix_t = mixed.T.astype(BF16)

        @pl.when(i == 0)
        def _():
            loss_ref[...] = sq

        @pl.when(i > 0)
        def _():
            loss_ref[...] += sq

        @pl.when((i & 1) == 0)
        def _():
            mix_keep[...] = mix_t
            dy_keep[...] = dy_bf

        @pl.when((i & 1) == 1)
        def _():
            gw = jnp.dot(jnp.concatenate([mix_keep[...], mix_t], axis=1), jnp.concatenate([dy_keep[...], dy_bf], axis=0),
                         preferred_element_type=F32)

            @pl.when(i == 1)
            def _():
                gwo_ref[...] = gw

            @pl.when(i > 1)
            def _():
                gwo_ref[...] += gw

        dmix = lax.dot_general(dy_bf, w, (((1,), (1,)), ((), ())), preferred_element_type=F32)
        dma = jnp.concatenate(_pair_tiles(_tiles(dmix[:, :HALF_WIDTH])), axis=1)
        dmb = dmix[:, HALF_WIDTH:]

        doa = dma * (ga * sa)
        doa_ref[...] = doa.astype(BF16)
        dla_ref[...] = split_dot(doa * oa, gat)
        dga_ref[...] = (dma * oa * (sa * (1.0 + ga * (1.0 - sa)))).astype(BF16)
        dob = dmb * (gb * sb)
        dgb_ref[...] = (dmb * ob * (sb * (1.0 + gb * (1.0 - sb)))).astype(BF16)
        dlb = split_dot(dob * ob, gat)
        dob_ref[...] = dob.astype(BF16)
        _fold_store(dob, s_f, dob4_ref, dob16_ref, tm)
        dlb_ref[...] = dlb
        _fold_store(dlb, s_f, dlb4_ref, dlb16_ref, tm)
        lse_ref[...] = lse_b
        _fold_store(lse_b, s_f, lse4_ref, lse16_ref, tm)

    row = lambda width: pl.BlockSpec((tm, width), lambda i: (i, 0))
    full = lambda a: pl.BlockSpec(a.shape, lambda i: (0,) * a.ndim)
    fb_shapes, fb_specs = _fold_specs(tm, BF16)
    _, ff_specs = _fold_specs(tm, F32)
    st_shapes, st_specs = _fold_specs(tm, F32, STAT_WIDTH)
    nat = lambda dtype, width=HALF_WIDTH: jax.ShapeDtypeStruct((SEQ, width), dtype)
    return pl.pallas_call(
        body, name="tail", grid=(SEQ // tm,),
        in_specs=[row(HALF_WIDTH), row(HALF_WIDTH), row(STAT_WIDTH), ff_specs[0], st_specs[0], ff_specs[1], st_specs[1],
                  row(HALF_WIDTH), row(HALF_WIDTH), row(D_MODEL), row(D_MODEL), full(w_out), full(spread), full(gather)],
        out_specs=(pl.BlockSpec((1, D_MODEL), lambda i: (0, 0)), row(D_MODEL),
                   pl.BlockSpec((D_MODEL, D_MODEL), lambda i: (0, 0)),
                   row(HALF_WIDTH), row(STAT_WIDTH), row(HALF_WIDTH), row(HALF_WIDTH),
                   row(HALF_WIDTH), *fb_specs, row(STAT_WIDTH), *st_specs, row(STAT_WIDTH), *st_specs),
        out_shape=(jax.ShapeDtypeStruct((1, D_MODEL), F32), jax.ShapeDtypeStruct((SEQ, D_MODEL), F32),
                   jax.ShapeDtypeStruct((D_MODEL, D_MODEL), F32),
                   nat(BF16), nat(F32, STAT_WIDTH), nat(BF16), nat(BF16),
                   nat(BF16), *fb_shapes, nat(F32, STAT_WIDTH), *st_shapes, nat(F32, STAT_WIDTH), *st_shapes),
        scratch_shapes=[_fold_scratch(tm), pltpu.VMEM((D_MODEL, tm), BF16), pltpu.VMEM((tm, D_MODEL), BF16)],
        compiler_params=_params(("arbitrary",)),
    )(oa, ob1, lb1, ob4, lb4, ob16, lb16, gate_a, gate_b, x, target, w_out, spread, gather)


def _dproj_assemble(dqa, dka, dva, dga, dgb, dq1, dk1, dv1, dq4, dk4, dv4, dq16, dk16, dv16, tqa, tqb, tkb, tka,
                    qkg, cos4, sin4, bmean):
    tm = ROW_TILE

    def norm_rope_bwd(d_out, t, g, cos, sin, bm, scale):
        d_r = d_out * scale
        dyv = d_r * cos + _swap_halves(d_r * sin)
        rr = lax.rsqrt(_head_sum(t * t, bm) + EPS)
        that = t * rr
        dgain = jnp.sum(dyv * that, axis=0, keepdims=True)
        gdy = dyv * g
        dt = rr * (gdy - that * _head_sum(that * gdy, bm))
        return dt, dgain

    def body(dqa_ref, dka_ref, dva_ref, dga_ref, dgb_ref, dq1_ref, dk1_ref, dv1_ref, dq4_ref, dk4_ref, dv4_ref,
             dq16_ref, dk16_ref, dv16_ref, tqa_ref, tqb_ref, tkb_ref, tka_ref, qkg_ref, cos_ref, sin_ref, bm_ref,
             dproj_ref, dqkg_ref, s_f):
        i = pl.program_id(0)
        cos, sin, bm = cos_ref[...], sin_ref[...], bm_ref[...]

        def merged(nat_ref, f4_ref, f16_ref):
            return nat_ref[...].astype(F32) + _unfold_load(f4_ref, s_f, 4, tm) + _unfold_load(f16_ref, s_f, 16, tm)

        @pl.when(i == 0)
        def _():
            dqkg_ref[...] = jnp.zeros_like(dqkg_ref)

        def through(d_out, t, row, scale, c0, paired=False):
            g = qkg_ref[row:row + 1, :]
            tot = jnp.zeros((1, PAIR), F32)
            dts = []
            for j in range(d_out.shape[1] // PAIR):
                cols = slice(j * PAIR, (j + 1) * PAIR)
                dt, dg = norm_rope_bwd(d_out[:, cols], t[:, cols], g, cos, sin, bm, scale)
                dts.append(dt)
                tot = tot + dg
            if paired:
                dts = _unpair_tiles(dts)
            for j, dt in enumerate(dts):
                dproj_ref[:, c0 + j * PAIR:c0 + (j + 1) * PAIR] = dt.astype(BF16)
            dqkg_ref[row:row + 1, :] += tot

        through(dqa_ref[...].astype(F32), tqa_ref[...].astype(F32), 0, HEAD_DIM ** -0.5, C_QA, paired=True)
        through(dka_ref[...].astype(F32), tka_ref[...].astype(F32), 1, 1.0, C_KA)
        through(merged(dq1_ref, dq4_ref, dq16_ref), tqb_ref[...].astype(F32), 2, HEAD_DIM ** -0.5, C_QB)
        through(merged(dk1_ref, dk4_ref, dk16_ref), tkb_ref[...].astype(F32), 3, 1.0, C_KB)
        dproj_ref[:, C_VB:C_VB + HALF_WIDTH] = merged(dv1_ref, dv4_ref, dv16_ref).astype(BF16)
        dproj_ref[:, C_GA:C_GA + HALF_WIDTH] = jnp.concatenate(
            _unpair_tiles(_tiles(dga_ref[...].astype(F32))), axis=1).astype(BF16)
        dproj_ref[:, C_GB:C_GB + HALF_WIDTH] = dgb_ref[...].astype(BF16)
        dproj_ref[:, C_VA:C_VA + KV_A_WIDTH] = dva_ref[...].astype(BF16)

    row = lambda width: pl.BlockSpec((tm, width), lambda i: (i, 0))
    full = lambda a: pl.BlockSpec(a.shape, lambda i: (0,) * a.ndim)
    _, ff_specs = _fold_specs(tm, F32)
    return pl.pallas_call(
        body, name="dproj_assemble", grid=(SEQ // tm,),
        in_specs=[row(HALF_WIDTH), row(KV_A_WIDTH), row(KV_A_WIDTH), row(HALF_WIDTH), row(HALF_WIDTH),
                  row(HALF_WIDTH), row(HALF_WIDTH), row(HALF_WIDTH), ff_specs[0], ff_specs[0], ff_specs[0],
                  ff_specs[1], ff_specs[1], ff_specs[1],
                  row(HALF_WIDTH), row(HALF_WIDTH), row(HALF_WIDTH), row(KV_A_WIDTH),
                  full(qkg), row(PAIR), row(PAIR), full(bmean)],
        out_specs=(row(IN_WIDTH), pl.BlockSpec((SMALL_ROWS, PAIR), lambda i: (0, 0))),
        out_shape=(jax.ShapeDtypeStruct((SEQ, IN_WIDTH), BF16), jax.ShapeDtypeStruct((SMALL_ROWS, PAIR), F32)),
        scratch_shapes=[_fold_scratch(tm)],
        compiler_params=_params(("arbitrary",)),
    )(dqa, dka, dva, dga, dgb, dq1, dk1, dv1, dq4, dk4, dv4, dq16, dk16, dv16, tqa, tqb, tkb, tka, qkg, cos4, sin4, bmean)


def _input_grad_reduce(dproj, w, x, gain, dy, blocks_in, blocks_out, small):
    tm = ROW_TILE
    n_steps = SEQ // tm
    stage2_step, stage3_step = 3, 8
    shapes = (blocks_in.shape[1:], blocks_out.shape[1:])

    def body(dp_ref, w_ref, x_ref, g_ref, dy_ref, ga_hbm, gb_hbm, small_ref,
             gx_ref, out_a, out_b, small_out_ref, dgain_out_ref,
             part_a, part_b, sib_a, sib_b, wire_a, wire_b, chips_a, chips_b, small_all, dgain_acc, dgain_all,
             load_sems, sib_send, sib_recv, chip_send, chip_recv, small_send, small_recv, dgain_send, dgain_recv):
        i = pl.program_id(0)
        x, y, c = lax.axis_index("x"), lax.axis_index("y"), lax.axis_index("c")
        sibling = (x, y, 1 - c)
        chips = [(x, y), (1 - x, y), (x, 1 - y), (1 - x, 1 - y)]
        my_id = 4 * x + 2 * y + c
        g_hbm, part, from_sib = (ga_hbm, gb_hbm), (part_a, part_b), (sib_a, sib_b)
        to_wire, from_chips, out = (wire_a, wire_b), (chips_a, chips_b), (out_a, out_b)
        both = (0, 1)

        def blk(a, chip, core):
            return g_hbm[a].at[4 * chip[0] + 2 * chip[1] + core]

        def to_all(src, dst_all, send, recv):
            copies = []
            for rel in range(1, N_DEV):
                dx, dy_, dc = (rel >> 2) & 1, (rel >> 1) & 1, rel & 1
                to = (1 - x if dx else x, 1 - y if dy_ else y, 1 - c if dc else c)
                copies.append(pltpu.make_async_remote_copy(
                    src_ref=src, dst_ref=dst_all.at[my_id], send_sem=send.at[rel - 1], recv_sem=recv.at[rel - 1],
                    device_id=to, device_id_type=MESH))
            return copies

        small_copies = to_all(small_all.at[my_id], small_all, small_send, small_recv)
        dgain_copies = to_all(dgain_acc, dgain_all, dgain_send, dgain_recv)
        loads = [[pltpu.make_async_copy(blk(a, chips[k], c), part[a].at[k], load_sems.at[a, k]) for k in range(4)] for a in both]
        to_sib = [[pltpu.make_async_remote_copy(
            src_ref=blk(a, chips[k], 1 - c), dst_ref=from_sib[a].at[k], send_sem=sib_send.at[a, k], recv_sem=sib_recv.at[a, k],
            device_id=sibling, device_id_type=MESH) for k in range(4)] for a in both]
        first, other, k_first, k_other = _routes()
        to_chips = [[pltpu.make_async_remote_copy(
            src_ref=to_wire[a].at[s], dst_ref=from_chips[a].at[s], send_sem=chip_send.at[a, s], recv_sem=chip_recv.at[a, s],
            device_id=(first, first, other)[s], device_id_type=MESH) for s in range(3)] for a in both]

        def chip_partial(a, k):
            return part[a][k].astype(F32) + from_sib[a][k].astype(F32)

        @pl.when(i == 0)
        def _():
            small_all[my_id] = small_ref[...]
            for cp in small_copies:
                cp.start()
            for k in (1, 2, 3, 0):
                for a in both:
                    loads[a][k].start()
                    to_sib[a][k].start()

        @pl.when(i == stage2_step)
        def _():
            for k in (1, 2, 3):
                for a in both:
                    loads[a][k].wait()
                    to_sib[a][k].wait_recv()
            for s, k in ((0, 3), (1, k_first)):
                for a in both:
                    to_wire[a][s] = chip_partial(a, k).astype(BF16)
                    to_chips[a][s].start()

        @pl.when(i == stage3_step)
        def _():
            for a in both:
                to_chips[a][0].wait_recv()
                to_wire[a][2] = (chip_partial(a, k_other) + from_chips[a][0].astype(F32)).astype(BF16)
                to_chips[a][2].start()

        dh = lax.dot_general(dp_ref[...], w_ref[...], (((1,), (1,)), ((), ())), preferred_element_type=F32)
        xf = x_ref[...]
        r = lax.rsqrt(jnp.mean(xf * xf, axis=-1, keepdims=True) + EPS)
        xhat = xf * r
        dg = jnp.sum(dh * xhat, axis=0, keepdims=True)
        dxh = dh * g_ref[...]
        dx = r * (dxh - xhat * jnp.mean(dxh * xhat, axis=-1, keepdims=True))
        gx_ref[...] = dy_ref[...] + dx

        @pl.when(i == 0)
        def _():
            dgain_acc[...] = dg

        @pl.when(i > 0)
        def _():
            dgain_acc[...] += dg

        @pl.when(i == n_steps - 1)
        def _():
            dgain_all[my_id] = dgain_acc[...]
            for cp in dgain_copies:
                cp.start()
            for a in both:
                loads[a][0].wait()
                to_sib[a][0].wait_recv()
                acc = chip_partial(a, 0)
                for s in (1, 2):
                    to_chips[a][s].wait_recv()
                    acc = acc + from_chips[a][s].astype(F32)
                out[a][...] = acc
            for copies, gathered, dst in ((small_copies, small_all, small_out_ref), (dgain_copies, dgain_all, dgain_out_ref)):
                for cp in copies:
                    cp.wait_recv()
                tot = gathered[0]
                for d in range(1, N_DEV):
                    tot = tot + gathered[d]
                dst[...] = tot
            for cp in to_sib[0] + to_sib[1] + to_chips[0] + to_chips[1] + small_copies + dgain_copies:
                cp.wait_send()

    row = lambda width: pl.BlockSpec((tm, width), lambda i: (i, 0))
    full = lambda a: pl.BlockSpec(a.shape, lambda i: (0,) * a.ndim)
    whole = lambda shape: pl.BlockSpec(shape, lambda i: (0,) * len(shape))
    hbm = pl.BlockSpec(memory_space=pl.ANY)
    dtypes = (blocks_in.dtype, blocks_out.dtype)
    buf = lambda n, dts: [pltpu.VMEM((n,) + s, dt) for s, dt in zip(shapes, dts)]
    return pl.pallas_call(
        body, name="input_grad_rs", grid=(n_steps,),
        in_specs=[row(IN_WIDTH), full(w), row(D_MODEL), full(gain), row(D_MODEL), hbm, hbm, full(small)],
        out_specs=(row(D_MODEL), whole(shapes[0]), whole(shapes[1]), whole((SMALL_ROWS, SMALL_COLS)), whole((1, D_MODEL))),
        out_shape=(jax.ShapeDtypeStruct((SEQ, D_MODEL), F32), jax.ShapeDtypeStruct(shapes[0], F32),
                   jax.ShapeDtypeStruct(shapes[1], F32), jax.ShapeDtypeStruct((SMALL_ROWS, SMALL_COLS), F32),
                   jax.ShapeDtypeStruct((1, D_MODEL), F32)),
        scratch_shapes=[*buf(4, dtypes), *buf(4, dtypes), *buf(3, (BF16, BF16)), *buf(3, (BF16, BF16)),
                        pltpu.VMEM((N_DEV, SMALL_ROWS, SMALL_COLS), F32),
                        pltpu.VMEM((1, D_MODEL), F32), pltpu.VMEM((N_DEV, 1, D_MODEL), F32),
                        pltpu.SemaphoreType.DMA((2, 4)), pltpu.SemaphoreType.DMA((2, 4)), pltpu.SemaphoreType.DMA((2, 4)),
                        pltpu.SemaphoreType.DMA((2, 3)), pltpu.SemaphoreType.DMA((2, 3)),
                        pltpu.SemaphoreType.DMA((7,)), pltpu.SemaphoreType.DMA((7,)),
                        pltpu.SemaphoreType.DMA((7,)), pltpu.SemaphoreType.DMA((7,))],
        compiler_params=_params(("arbitrary",)),
    )(dproj, w, x, gain, dy, blocks_in, blocks_out, small)


def _weight_grad(h_t, dproj):
    tk = 1024
    cb = IN_WIDTH // 2
    n_k = SEQ // tk

    def body(ht_ref, dp_ref, out_ref, acc):
        k = pl.program_id(1)
        upd = jnp.dot(ht_ref[...], dp_ref[...], preferred_element_type=F32)

        @pl.when(k == 0)
        def _():
            acc[...] = upd

        @pl.when(k > 0)
        def _():
            acc[...] += upd

        @pl.when(k == n_k - 1)
        def _():
            for b in range(N_DEV // 2):
                out_ref[b] = acc[:, b * SHARD_IN:(b + 1) * SHARD_IN].astype(BF16)

    return pl.pallas_call(
        body, name="weight_grad", grid=(2, n_k),
        in_specs=[pl.BlockSpec((D_MODEL, tk), lambda j, k: (0, k)), pl.BlockSpec((tk, cb), lambda j, k: (k, j))],
        out_specs=pl.BlockSpec((N_DEV // 2, D_MODEL, SHARD_IN), lambda j, k: (j, 0, 0)),
        out_shape=jax.ShapeDtypeStruct((N_DEV, D_MODEL, SHARD_IN), BF16),
        scratch_shapes=[pltpu.VMEM((D_MODEL, cb), F32)],
        compiler_params=_params(("arbitrary", "arbitrary")),
    )(h_t, dproj)


def _adamw(name, w, g, m, v):
    def body(w_ref, g_ref, m_ref, v_ref, d_ref, nm_ref, nv_ref):
        gv = g_ref[...]
        nm = ADAM_B1 * m_ref[...] + (1.0 - ADAM_B1) * gv
        nv = ADAM_B2 * v_ref[...] + (1.0 - ADAM_B2) * jnp.square(gv)
        m_hat = nm / (1.0 - ADAM_B1 ** ADAM_STEP)
        v_hat = nv / (1.0 - ADAM_B2 ** ADAM_STEP)
        d_ref[...] = -ADAM_LR * (m_hat / (jnp.sqrt(v_hat) + ADAM_EPS) + ADAM_WD * w_ref[...])
        nm_ref[...] = nm
        nv_ref[...] = nv

    vmem = pl.BlockSpec(memory_space=pltpu.VMEM)
    out = jax.ShapeDtypeStruct(w.shape, F32)
    return pl.pallas_call(
        body, name=name, in_specs=[vmem] * 4, out_specs=(vmem,) * 3, out_shape=(out,) * 3,
        compiler_params=pltpu.CompilerParams(vmem_limit_bytes=VMEM_LIMIT),
    )(w, g, m, v)


SMALL_USED = D_MODEL + 4 * HEAD_DIM + 8


def _pack_small(norm_gain, qa, ka, sinks, qb, kb, extra=None):
    parts = [norm_gain.reshape(-1), qa.reshape(-1), ka.reshape(-1), sinks.reshape(-1), qb.reshape(-1), kb.reshape(-1)]
    if extra is not None:
        parts.append(extra.reshape(-1))
    flat = jnp.concatenate(parts)
    flat = jnp.pad(flat, (0, SMALL_ROWS * SMALL_COLS - flat.shape[0]))
    return flat.reshape(SMALL_ROWS, SMALL_COLS)


def _unpack_small(a):
    flat = a.reshape(-1)
    sizes = (D_MODEL, HEAD_DIM, HEAD_DIM, 8, HEAD_DIM, HEAD_DIM)
    out, off = [], 0
    for s in sizes:
        out.append(flat[off:off + s].reshape(1, s))
        off += s
    return out


def _fold_heads(row):
    return row[0, :HEAD_DIM] + row[0, HEAD_DIM:]


def kernel(x, norm_gain, w_in, q_norm_a, k_norm_a, sinks_a, q_norm_b, k_norm_b, w_out, loss_target, m_norm_gain, m_w_in, m_q_norm_a, m_k_norm_a, m_sinks_a, m_q_norm_b, m_k_norm_b, m_w_out, v_norm_gain, v_w_in, v_q_norm_a, v_k_norm_a, v_sinks_a, v_q_norm_b, v_k_norm_b, v_w_out):
    x2, tgt = x[0], loss_target[0]
    w_in_sh, w_out_sh = w_in[0], w_out[0]

    w_full = _all_gather_w_in(w_in_sh)

    inv = np.float32(ROPE_THETA) ** (-np.arange(HEAD_DIM // 2, dtype=np.float32) / np.float32(HEAD_DIM // 2))
    ang = np.arange(SEQ, dtype=np.float32)[:, None] * inv[None, :].astype(np.float32)
    cos, sin = np.cos(ang).astype(np.float32), np.sin(ang).astype(np.float32)
    cos4 = jnp.asarray(np.concatenate([cos, cos, cos, cos], axis=1))
    sin4 = jnp.asarray(np.concatenate([-sin, sin, -sin, sin], axis=1))
    blockdiag = np.kron(np.eye(2, dtype=np.float32), np.ones((HEAD_DIM, HEAD_DIM), np.float32))
    bmean = jnp.asarray(blockdiag / HEAD_DIM, dtype=BF16)
    gather_np = np.kron(np.eye(2 * N_PAIRS, dtype=np.float32), np.ones((HEAD_DIM, STAT_REP), np.float32))
    spread_np = np.kron(np.eye(2 * N_PAIRS, dtype=np.float32), np.ones((STAT_REP, HEAD_DIM), np.float32))
    spread_np[np.arange(STAT_WIDTH) % STAT_REP != 0] = 0.0
    gather, spread = jnp.asarray(gather_np, dtype=BF16), jnp.asarray(spread_np, dtype=BF16)
    two = lambda g: jnp.concatenate([g, g], axis=1)
    qkg = jnp.concatenate([two(q_norm_a), two(k_norm_a), two(q_norm_b), two(k_norm_b),
                           jnp.zeros((SMALL_ROWS - 4, PAIR), F32)], axis=0)
    sinks_paired = jnp.stack([sinks_a[0, :N_PAIRS], sinks_a[0, N_PAIRS:]], axis=1)
    sink_rows = jnp.concatenate([jnp.repeat(sinks_paired, BLOCK, axis=1),
                                 jnp.zeros((SMALL_ROWS - N_PAIRS, 2 * BLOCK), F32)], axis=0)

    (tqa, tka, tqb, tkb, gate_a, gate_b, h_t, qa, ka, va, qb, kb, vb, qb4, qb16, kb4, kb16, vb4, vb16,
     gathered_out) = _proj_fwd(x2, norm_gain, w_full, qkg, cos4, sin4, bmean, w_out_sh)
    wo_full = gathered_out.reshape(D_MODEL, D_MODEL)
    oa, la = _attn_fwd("attn_a_fwd", qa[None], ka[None], va[None], sink_rows, BLOCK - 1)
    ob1, lb1 = _attn_fwd("attn_b1_fwd", qb[None], kb[None], vb[None], None, BLOCK)
    ob4, lb4 = _attn_fwd("attn_b4_fwd", qb4, kb4, vb4, None, BLOCK)
    ob16, lb16 = _attn_fwd("attn_b16_fwd", qb16, kb16, vb16, None, BLOCK)
    (loss_cols, dy, gwo, doa, dla, dga, dgb, dob, dob4, dob16, dlb, dlb4, dlb16, lse_b, lse4, lse16) = _tail(
        oa[0], ob1[0], lb1[0], ob4, lb4, ob16, lb16, gate_a, gate_b, x2, tgt, wo_full, spread, gather)

    dqa, dka, dva, dsink = _attn_bwd("attn_a_bwd", qa[None], ka[None], va[None], doa[None], la, dla[None], sink_rows, BLOCK - 1)
    dq1, dk1, dv1 = _attn_bwd("attn_b1_bwd", qb[None], kb[None], vb[None], dob[None], lse_b[None], dlb[None], None, BLOCK)
    dq4, dk4, dv4 = _attn_bwd("attn_b4_bwd", qb4, kb4, vb4, dob4, lse4, dlb4, None, BLOCK)
    dq16, dk16, dv16 = _attn_bwd("attn_b16_bwd", qb16, kb16, vb16, dob16, lse16, dlb16, None, BLOCK)
    dproj, dqkg = _dproj_assemble(dqa[0], dka[0], dva[0], dga, dgb, dq1[0], dk1[0], dv1[0], dq4, dk4, dv4,
                                  dq16, dk16, dv16, tqa, tqb, tkb, tka, qkg, cos4, sin4, bmean)
    gw_in = _weight_grad(h_t, dproj)

    blocks_in = gw_in
    blocks_out = gwo.reshape(N_DEV, SHARD_OUT, D_MODEL)
    g_sinks = jnp.concatenate([jnp.sum(dsink[:N_PAIRS, :BLOCK], axis=1), jnp.sum(dsink[:N_PAIRS, BLOCK:], axis=1)])
    small = _pack_small(jnp.zeros((D_MODEL,), F32), _fold_heads(dqkg[0:1]), _fold_heads(dqkg[1:2]), g_sinks,
                        _fold_heads(dqkg[2:3]), _fold_heads(dqkg[3:4]), extra=0.5 * jnp.sum(loss_cols) / D_MODEL)
    grad_x, g_w_in, g_w_out, small_red, dgain_red = _input_grad_reduce(
        dproj, w_full, x2, norm_gain, dy, blocks_in, blocks_out, small)
    n_gain_rows = D_MODEL // SMALL_COLS
    small_red = jnp.concatenate([dgain_red.reshape(n_gain_rows, SMALL_COLS), small_red[n_gain_rows:]], axis=0)
    g_small = _unpack_small(small_red)

    d_in, nm_in, nv_in = _adamw("adamw_w_in", w_in_sh, g_w_in, m_w_in[0], v_w_in[0])
    d_out, nm_out, nv_out = _adamw("adamw_w_out", w_out_sh, g_w_out, m_w_out[0], v_w_out[0])
    d_s, nm_s, nv_s = _adamw(
        "adamw_small",
        _pack_small(norm_gain, q_norm_a, k_norm_a, sinks_a, q_norm_b, k_norm_b), small_red,
        _pack_small(m_norm_gain, m_q_norm_a, m_k_norm_a, m_sinks_a, m_q_norm_b, m_k_norm_b),
        _pack_small(v_norm_gain, v_q_norm_a, v_k_norm_a, v_sinks_a, v_q_norm_b, v_k_norm_b))
    d_small, nm_small, nv_small = _unpack_small(d_s), _unpack_small(nm_s), _unpack_small(nv_s)

    loss = small_red.reshape(-1)[SMALL_USED]

    def assemble(small_list, big_in, big_out):
        ng, qa_, ka_, sk_, qb_, kb_ = small_list
        return [ng, big_in[None], qa_, ka_, sk_, qb_, kb_, big_out[None]]

    return (loss, grad_x[None], *assemble(g_small, g_w_in, g_w_out), *assemble(d_small, d_in, d_out),
            *assemble(nm_small, nm_in, nm_out), *assemble(nv_small, nv_in, nv_out))
```

```python
import functools

import numpy as np
import jax
import jax.numpy as jnp
from jax import lax
from jax.experimental import pallas as pl
from jax.experimental.pallas import tpu as pltpu

F32 = jnp.float32
BF16 = jnp.bfloat16

SEQ = 4096
D_MODEL = 1024
HEAD_DIM = 64
PAIR = 2 * HEAD_DIM
N_PAIRS = 4
HALF_WIDTH = N_PAIRS * PAIR
KV_A_WIDTH = 128
IN_WIDTH = 3328
BLOCK = 128
STAT_REP = 16
STAT_WIDTH = 128
EPS = 1e-6
NEG = -1e30
ROPE_THETA = 10000.0
N_DEV = 8
SHARD_IN = IN_WIDTH // N_DEV
SHARD_OUT = D_MODEL // N_DEV
PAYLOAD = SHARD_IN + SHARD_OUT
SMALL_ROWS, SMALL_COLS = 8, 256

C_QA, C_KA, C_VA, C_GA, C_QB, C_KB, C_VB, C_GB = 0, 512, 640, 768, 1280, 1792, 2304, 2816

ADAM_LR = 0.001
ADAM_B1 = 0.9
ADAM_B2 = 0.999
ADAM_EPS = 1e-08
ADAM_WD = 0.01
ADAM_STEP = 10

ROW_TILE = 256
PROJ_ROW_TILE = 512
FWD_BLOCKS_PER_STEP = 8
BWD_BLOCKS_PER_STEP = 8
VMEM_LIMIT = 56 * 1024 * 1024

MESH = pl.DeviceIdType.MESH


def _params(sem, vmem=VMEM_LIMIT):
    return pltpu.CompilerParams(dimension_semantics=sem, vmem_limit_bytes=vmem)


def _head_sum(v, bm):
    hi = v.astype(BF16)
    lo = (v - hi.astype(F32)).astype(BF16)
    return (jnp.dot(hi, bm, preferred_element_type=F32) + jnp.dot(lo, bm, preferred_element_type=F32))


def _swap_halves(y):
    lane = lax.broadcasted_iota(jnp.int32, y.shape, 1)
    first = (lane & 32) == 0
    return jnp.where(first, pltpu.roll(y, 96, 1), pltpu.roll(y, 32, 1))


def _sigmoid(g):
    return 1.0 / (1.0 + jnp.exp(-g))


def _tiles(a):
    return [a[:, j * PAIR:(j + 1) * PAIR] for j in range(N_PAIRS)]


def _pair_tiles(t):
    low = lax.broadcasted_iota(jnp.int32, t[0].shape, 1) < HEAD_DIM
    r = [pltpu.roll(a, HEAD_DIM, 1) for a in t]
    return [jnp.where(low, t[0], r[2]), jnp.where(low, r[0], t[2]), jnp.where(low, t[1], r[3]), jnp.where(low, r[1], t[3])]


def _unpair_tiles(p):
    low = lax.broadcasted_iota(jnp.int32, p[0].shape, 1) < HEAD_DIM
    r = [pltpu.roll(a, HEAD_DIM, 1) for a in p]
    return [jnp.where(low, p[0], r[1]), jnp.where(low, p[2], r[3]), jnp.where(low, r[0], p[1]), jnp.where(low, r[2], p[3])]


def _routes():
    x, y, c = lax.axis_index("x"), lax.axis_index("y"), lax.axis_index("c")
    north = c == 1
    first = (jnp.where(north, 1 - x, x), jnp.where(north, y, 1 - y), c)
    other = (jnp.where(north, x, 1 - x), jnp.where(north, 1 - y, y), c)
    k_first = jnp.where(north, 1, 2)
    return first, other, k_first, 3 - k_first


def _gather_plan(mine_ref, out_ref, send_sems, recv_sems):
    x, y, c = lax.axis_index("x"), lax.axis_index("y"), lax.axis_index("c")
    me, sibling, diag = (x, y, c), (x, y, 1 - c), (1 - x, 1 - y, c)
    first, other, k_first, k_other = _routes()

    def slot(px, py, pc):
        return out_ref.at[4 * px + 2 * py + pc]

    def copy(k, block, to, from_mine=False):
        return pltpu.make_async_remote_copy(
            src_ref=mine_ref if from_mine else slot(*block), dst_ref=slot(*block),
            send_sem=send_sems.at[k], recv_sem=recv_sems.at[k], device_id=to, device_id_type=MESH)

    sends = [copy(0, me, sibling, True), copy(1, me, (1 - x, y, c), True), copy(2, me, (x, 1 - y, c), True)]
    stages = [(copy(k_first, first, me), [copy(3, first, other), copy(3 + k_first, first, sibling)]),
              (copy(k_other, other, me), [copy(3 + k_other, other, sibling)]),
              (copy(3, diag, me), [copy(6, diag, sibling)])]
    from_sibling = [copy(0, sibling, me), copy(4, (1 - x, y, 1 - c), me), copy(5, (x, 1 - y, 1 - c), me),
                    copy(6, (1 - x, 1 - y, 1 - c), me)]
    return slot(*me), sends, stages, from_sibling


GATHER_SCRATCH = [pltpu.SemaphoreType.DMA((7,)), pltpu.SemaphoreType.DMA((7,))]


def _all_gather_w_in(w_in_sh):
    rows, cols = w_in_sh.shape

    def body(w_ref, out_ref, mine_ref, blocks, send_sems, recv_sems):
        mine_ref[...] = w_ref[...].astype(BF16)
        my_slot, sends, stages, from_sibling = _gather_plan(mine_ref, blocks, send_sems, recv_sems)
        for cp in sends:
            cp.start()
        my_slot[...] = mine_ref[...]
        for arrival, forwards in stages:
            arrival.wait_recv()
            for cp in forwards:
                cp.start()
        for arrival in from_sibling:
            arrival.wait_recv()
        for cp in sends + [cp for _, forwards in stages for cp in forwards]:
            cp.wait_send()
        for d in range(N_DEV):
            out_ref[:, d * cols:(d + 1) * cols] = blocks[d]

    vmem = pl.BlockSpec(memory_space=pltpu.VMEM)
    return pl.pallas_call(
        body, name="ag_w_in",
        out_shape=jax.ShapeDtypeStruct((rows, N_DEV * cols), BF16),
        in_specs=[vmem], out_specs=vmem,
        scratch_shapes=[pltpu.VMEM((rows, cols), BF16), pltpu.VMEM((N_DEV, rows, cols), BF16)] + GATHER_SCRATCH,
        compiler_params=pltpu.CompilerParams(vmem_limit_bytes=VMEM_LIMIT),
    )(w_in_sh)


def _fold_scratch(tm):
    return pltpu.VMEM((N_PAIRS, tm, PAIR), F32)


def _fold_store(val, scr, out4, out16, tm):
    groups = range(val.shape[1] // PAIR)
    for j in groups:
        scr[j] = val[:, j * PAIR:(j + 1) * PAIR]
    for dil, out in ((4, out4), (16, out16)):
        for r in range(dil):
            for j in groups:
                out[r, :, j * PAIR:(j + 1) * PAIR] = scr[j, pl.ds(r, tm // dil, stride=dil), :].astype(out.dtype)


def _unfold_load(src, scr, dil, tm):
    groups = range(src.shape[2] // PAIR)
    for r in range(dil):
        for j in groups:
            scr[j, pl.ds(r, tm // dil, stride=dil), :] = src[r, :, j * PAIR:(j + 1) * PAIR].astype(F32)
    return jnp.concatenate([scr[j] for j in groups], axis=1)


def _fold_specs(tm, dtype, width=HALF_WIDTH):
    shapes = (jax.ShapeDtypeStruct((4, SEQ // 4, width), dtype), jax.ShapeDtypeStruct((16, SEQ // 16, width), dtype))
    specs = (pl.BlockSpec((4, tm // 4, width), lambda i: (0, i, 0)),
             pl.BlockSpec((16, tm // 16, width), lambda i: (0, i, 0)))
    return shapes, specs


def _proj_fwd(x, gain, w, qkg, cos4, sin4, bmean, w_out_sh):
    tm = PROJ_ROW_TILE
    n_steps = SEQ // tm

    def norm_rope(t, g, cos, sin, bm, scale):
        rr = lax.rsqrt(_head_sum(t * t, bm) + EPS)
        yv = t * rr * g
        return (yv * cos + _swap_halves(yv) * sin) * scale

    def body(x_ref, g_ref, w_ref, qkg_ref, cos_ref, sin_ref, bm_ref, wo_ref,
             tqa_ref, tka_ref, tqb_ref, tkb_ref, ga_ref, gb_ref, ht_ref, qa_ref, ka_ref, va_ref, qb_ref, kb_ref, vb_ref,
             qb4_ref, qb16_ref, kb4_ref, kb16_ref, vb4_ref, vb16_ref, wo_all_ref,
             proj, scr, wo_mine, wo_all, send_sems, recv_sems):
        i = pl.program_id(0)
        my_slot, sends, stages, from_sibling = _gather_plan(wo_mine, wo_all, send_sems, recv_sems)

        @pl.when(i == 0)
        def _():
            wo_mine[...] = wo_ref[...].astype(BF16)
            for cp in sends:
                cp.start()
            my_slot[...] = wo_mine[...]

        @pl.when(i == n_steps // 2)
        def _():
            for arrival, forwards in stages[:2]:
                arrival.wait_recv()
                for cp in forwards:
                    cp.start()

        xf = x_ref[...]
        r = lax.rsqrt(jnp.mean(xf * xf, axis=-1, keepdims=True) + EPS)
        hf = xf * r * g_ref[...]
        ht_ref[...] = hf.T.astype(BF16)
        cos, sin, bm = cos_ref[...], sin_ref[...], bm_ref[...]
        proj[...] = jnp.dot(hf.astype(BF16), w_ref[...], preferred_element_type=F32)

        def roped(tiles, row, scale):
            g = qkg_ref[row:row + 1, :]
            return jnp.concatenate([norm_rope(t, g, cos, sin, bm, scale) for t in tiles], axis=1)

        tqa = _pair_tiles(_tiles(proj[:, C_QA:C_QA + HALF_WIDTH]))
        tqa_ref[...] = jnp.concatenate(tqa, axis=1).astype(BF16)
        qa_ref[...] = roped(tqa, 0, HEAD_DIM ** -0.5).astype(BF16)
        ga_ref[...] = jnp.concatenate(_pair_tiles(_tiles(proj[:, C_GA:C_GA + HALF_WIDTH])), axis=1).astype(BF16)
        gb_ref[...] = proj[:, C_GB:C_GB + HALF_WIDTH].astype(BF16)
        tqb = proj[:, C_QB:C_QB + HALF_WIDTH]
        tqb_ref[...] = tqb.astype(BF16)
        qb = roped(_tiles(tqb), 2, HEAD_DIM ** -0.5)
        qb_ref[...] = qb.astype(BF16)
        _fold_store(qb, scr, qb4_ref, qb16_ref, tm)
        tkb = proj[:, C_KB:C_KB + HALF_WIDTH]
        tkb_ref[...] = tkb.astype(BF16)
        kb = roped(_tiles(tkb), 3, 1.0)
        kb_ref[...] = kb.astype(BF16)
        _fold_store(kb, scr, kb4_ref, kb16_ref, tm)
        vb = proj[:, C_VB:C_VB + HALF_WIDTH]
        vb_ref[...] = vb.astype(BF16)
        _fold_store(vb, scr, vb4_ref, vb16_ref, tm)
        tka = proj[:, C_KA:C_KA + KV_A_WIDTH]
        tka_ref[...] = tka.astype(BF16)
        ka_ref[...] = roped([tka], 1, 1.0).astype(BF16)
        va_ref[...] = proj[:, C_VA:C_VA + KV_A_WIDTH].astype(BF16)

        @pl.when(i == n_steps - 1)
        def _():
            arrival, forwards = stages[2]
            arrival.wait_recv()
            for cp in forwards:
                cp.start()
            for arrival in from_sibling:
                arrival.wait_recv()
            for cp in sends + [cp for _, forwards in stages for cp in forwards]:
                cp.wait_send()
            wo_all_ref[...] = wo_all[...]

    row = lambda width: pl.BlockSpec((tm, width), lambda i: (i, 0))
    full = lambda a: pl.BlockSpec(a.shape, lambda i: (0,) * a.ndim)
    nat = lambda width, dtype=BF16: jax.ShapeDtypeStruct((SEQ, width), dtype)
    f_shapes, f_specs = _fold_specs(tm, BF16)
    return pl.pallas_call(
        body, name="proj_fwd", grid=(SEQ // tm,),
        in_specs=[row(D_MODEL), full(gain), full(w), full(qkg), row(PAIR), row(PAIR), full(bmean), full(w_out_sh)],
        out_specs=(row(HALF_WIDTH), row(KV_A_WIDTH), row(HALF_WIDTH), row(HALF_WIDTH), row(HALF_WIDTH), row(HALF_WIDTH),
                   pl.BlockSpec((D_MODEL, tm), lambda i: (0, i)),
                   row(HALF_WIDTH), row(KV_A_WIDTH), row(KV_A_WIDTH), row(HALF_WIDTH), row(HALF_WIDTH), row(HALF_WIDTH),
                   *f_specs, *f_specs, *f_specs,
                   pl.BlockSpec((N_DEV,) + w_out_sh.shape, lambda i: (0, 0, 0))),
        out_shape=(nat(HALF_WIDTH), nat(KV_A_WIDTH), nat(HALF_WIDTH), nat(HALF_WIDTH), nat(HALF_WIDTH), nat(HALF_WIDTH),
                   jax.ShapeDtypeStruct((D_MODEL, SEQ), BF16),
                   nat(HALF_WIDTH), nat(KV_A_WIDTH), nat(KV_A_WIDTH), nat(HALF_WIDTH), nat(HALF_WIDTH), nat(HALF_WIDTH),
                   *f_shapes, *f_shapes, *f_shapes,
                   jax.ShapeDtypeStruct((N_DEV,) + w_out_sh.shape, BF16)),
        scratch_shapes=[pltpu.VMEM((tm, IN_WIDTH), F32), _fold_scratch(tm), pltpu.VMEM(w_out_sh.shape, BF16),
                        pltpu.VMEM((N_DEV,) + w_out_sh.shape, BF16)] + GATHER_SCRATCH,
        compiler_params=_params(("arbitrary",)),
    )(x, gain, w, qkg, cos4, sin4, bmean, w_out_sh)


def _band_mask(i, max_dist):
    j = lax.broadcasted_iota(jnp.int32, (2 * BLOCK, 2 * BLOCK), 0)
    c = lax.broadcasted_iota(jnp.int32, (2 * BLOCK, 2 * BLOCK), 1)
    dist = (c & (BLOCK - 1)) + BLOCK - j
    return (dist >= 0) & (dist <= max_dist) & ((j >= BLOCK) | (i > 0))


def _stack_heads(t):
    lane = lax.broadcasted_iota(jnp.int32, t.shape, 1)
    low = lane < HEAD_DIM
    zero = jnp.zeros_like(t)
    return jnp.concatenate([jnp.where(low, t, zero), jnp.where(low, zero, t)], axis=0)


def _unstack_t(t):
    return jnp.concatenate([t[:HEAD_DIM, :BLOCK], t[HEAD_DIM:, BLOCK:]], axis=0).T


def _rows_to_stats(rows):
    parts = []
    for row in rows:
        parts.append(jnp.broadcast_to(row[:, :BLOCK], (STAT_REP, BLOCK)))
        parts.append(jnp.broadcast_to(row[:, BLOCK:], (STAT_REP, BLOCK)))
    return jnp.concatenate(parts, axis=0).T


def _stats_to_rows(t):
    tt = t.T
    return [jnp.concatenate([tt[2 * p * STAT_REP:2 * p * STAT_REP + 1, :],
                             tt[(2 * p + 1) * STAT_REP:(2 * p + 1) * STAT_REP + 1, :]], axis=1) for p in range(N_PAIRS)]


def _attn_fwd(name, q, k, v, sink_rows, max_dist):
    n_seq, length, _ = q.shape
    ck = k.shape[2]
    nb = length // BLOCK
    shared = ck == PAIR
    has_sinks = sink_rows is not None

    qb = FWD_BLOCKS_PER_STEP

    def body(*refs):
        if has_sinks:
            q_ref, kc_ref, vc_ref, sink_ref, o_ref, lse_ref, kp_ref, vp_ref = refs
        else:
            q_ref, kc_ref, vc_ref, o_ref, lse_ref, kp_ref, vp_ref = refs
        step = pl.program_id(0)

        @pl.when(step == 0)
        def _():
            kp_ref[...] = jnp.zeros_like(kp_ref)
            vp_ref[...] = jnp.zeros_like(vp_ref)

        valid = [_band_mask((step * qb + b) & (nb - 1), max_dist) for b in range(qb)]
        cols = [slice(p * PAIR, (p + 1) * PAIR) for p in range(N_PAIRS)]
        kcols = [slice(0, PAIR) if shared else c for c in cols]
        rows = [slice(b * BLOCK, (b + 1) * BLOCK) for b in range(qb)]
        units = [(b, p) for b in range(qb) for p in range(N_PAIRS)]
        n = range(len(units))

        def window(prev_ref, cur_ref, b, kc):
            before = prev_ref[:, kc] if b == 0 else cur_ref[rows[b - 1], kc]
            return jnp.concatenate([before, cur_ref[rows[b], kc]], axis=0)

        st = [lax.dot_general(window(kp_ref, kc_ref, b, kcols[p]), _stack_heads(q_ref[rows[b], cols[p]]),
                              (((1,), (1,)), ((), ())), preferred_element_type=F32) for b, p in units]
        st = [jnp.where(valid[units[u][0]], st[u], NEG) for u in n]
        m = [jnp.max(s, axis=0, keepdims=True) for s in st]
        if has_sinks:
            sk = [sink_ref[p:p + 1, :] for _, p in units]
            m = [jnp.maximum(m[u], sk[u]) for u in n]
        pt = [jnp.exp(st[u] - m[u]) for u in n]
        l = [jnp.sum(t, axis=0, keepdims=True) for t in pt]
        if has_sinks:
            l = [l[u] + jnp.exp(sk[u] - m[u]) for u in n]
        v2t = [window(vp_ref, vc_ref, b, kcols[p]).astype(F32).T.astype(BF16) for b, p in units]
        ot = [jnp.dot(v2t[u], pt[u].astype(BF16), preferred_element_type=F32) / l[u] for u in n]
        for u, (b, p) in enumerate(units):
            o_ref[rows[b], cols[p]] = _unstack_t(ot[u]).astype(BF16)
        for b in range(qb):
            lse_ref[rows[b], :] = _rows_to_stats([m[u] + jnp.log(l[u]) for u in n if units[u][0] == b])
        kp_ref[...] = kc_ref[rows[-1], :]
        vp_ref[...] = vc_ref[rows[-1], :]

    cur = lambda width: pl.BlockSpec((qb * BLOCK, width), lambda s: (s, 0))
    flat = lambda a: a.reshape(n_seq * length, a.shape[2])
    in_specs = [cur(HALF_WIDTH), cur(ck), cur(ck)]
    args = [flat(q), flat(k), flat(v)]
    if has_sinks:
        in_specs.append(pl.BlockSpec(sink_rows.shape, lambda s: (0, 0)))
        args.append(sink_rows)
    o, lse = pl.pallas_call(
        body, name=name, grid=(n_seq * nb // qb,), in_specs=in_specs,
        out_specs=(cur(HALF_WIDTH), cur(STAT_WIDTH)),
        out_shape=(jax.ShapeDtypeStruct((n_seq * length, HALF_WIDTH), BF16),
                   jax.ShapeDtypeStruct((n_seq * length, STAT_WIDTH), F32)),
        scratch_shapes=[pltpu.VMEM((BLOCK, ck), BF16), pltpu.VMEM((BLOCK, ck), BF16)],
        compiler_params=_params(("arbitrary",)),
    )(*args)
    return o.reshape(n_seq, length, HALF_WIDTH), lse.reshape(n_seq, length, STAT_WIDTH)


def _attn_bwd(name, q, k, v, d_o, lse, delta, sink_rows, max_dist):
    n_seq, length, _ = q.shape
    ck = k.shape[2]
    nb = length // BLOCK
    n_blocks = n_seq * nb
    n_rows = n_seq * length
    shared = ck == PAIR
    has_sinks = sink_rows is not None
    qb = BWD_BLOCKS_PER_STEP
    n_steps = n_blocks // qb

    def body(*refs):
        if has_sinks:
            (q_ref, kc_ref, vc_ref, do_ref, lse_ref, dl_ref, sink_ref,
             dq_ref, dk_ref, dv_ref, dsink_ref, ck_scr, cv_scr, kp_ref, vp_ref) = refs
        else:
            (q_ref, kc_ref, vc_ref, do_ref, lse_ref, dl_ref,
             dq_ref, dk_ref, dv_ref, ck_scr, cv_scr, kp_ref, vp_ref) = refs
        step = pl.program_id(0)

        @pl.when(step == 0)
        def _():
            ck_scr[...] = jnp.zeros_like(ck_scr)
            cv_scr[...] = jnp.zeros_like(cv_scr)
            kp_ref[...] = jnp.zeros_like(kp_ref)
            vp_ref[...] = jnp.zeros_like(vp_ref)
            if has_sinks:
                dsink_ref[...] = jnp.zeros_like(dsink_ref)

        valid = [_band_mask((step * qb + b) & (nb - 1), max_dist) for b in range(qb)]
        cols = [slice(p * PAIR, (p + 1) * PAIR) for p in range(N_PAIRS)]
        kcols = [slice(0, PAIR) if shared else c for c in cols]
        rows = [slice(b * BLOCK, (b + 1) * BLOCK) for b in range(qb)]
        units = [(b, p) for b in range(qb) for p in range(N_PAIRS)]
        n = range(len(units))
        nt = (((1,), (1,)), ((), ()))

        def window(prev_ref, cur_ref, b, kc):
            before = prev_ref[:, kc] if b == 0 else cur_ref[rows[b - 1], kc]
            return jnp.concatenate([before, cur_ref[rows[b], kc]], axis=0)

        q_st = [_stack_heads(q_ref[rows[b], cols[p]]) for b, p in units]
        do_st = [_stack_heads(do_ref[rows[b], cols[p]]) for b, p in units]
        k2 = [window(kp_ref, kc_ref, b, kcols[p]) for b, p in units]
        v2 = [window(vp_ref, vc_ref, b, kcols[p]) for b, p in units]
        st = [lax.dot_general(k2[u], q_st[u], nt, preferred_element_type=F32) for u in n]
        dpt = [lax.dot_general(v2[u], do_st[u], nt, preferred_element_type=F32) for u in n]
        lse_rows = [_stats_to_rows(lse_ref[rows[b], :]) for b in range(qb)]
        dl_rows = [_stats_to_rows(dl_ref[rows[b], :]) for b in range(qb)]
        lse_row = [lse_rows[b][p] for b, p in units]
        dl_row = [dl_rows[b][p] for b, p in units]
        pt = [jnp.exp(jnp.where(valid[units[u][0]], st[u], NEG) - lse_row[u]) for u in n]
        dst = [(pt[u] * (dpt[u] - dl_row[u])).astype(BF16) for u in n]
        ptb = [t.astype(BF16) for t in pt]
        dv2 = [jnp.dot(ptb[u], do_st[u], preferred_element_type=F32) for u in n]
        dk2 = [jnp.dot(dst[u], q_st[u], preferred_element_type=F32) for u in n]
        k2t = [k2[u].astype(F32).T.astype(BF16) for u in n]
        dqt = [jnp.dot(k2t[u], dst[u], preferred_element_type=F32) for u in n]
        for u, (b, p) in enumerate(units):
            dq_ref[rows[b], cols[p]] = _unstack_t(dqt[u]).astype(BF16)
        if has_sinks:
            for u, (b, p) in enumerate(units):
                p_sink = jnp.exp(sink_ref[p:p + 1, :] - lse_row[u])
                dsink_ref[p:p + 1, :] = dsink_ref[p:p + 1, :] - p_sink * dl_row[u]

        def total(parts, w, group):
            sel = [u for u, (b, p) in enumerate(units) if (shared or p == group)]
            terms = ([parts[u][:BLOCK] for u in sel if units[u][0] == w]
                     + [parts[u][BLOCK:] for u in sel if units[u][0] == w - 1])
            tot = terms[0]
            for t in terms[1:]:
                tot = tot + t
            return tot

        first_row = step * (qb * BLOCK)
        for acc_ref, out_ref, parts in ((ck_scr, dk_ref, dk2), (cv_scr, dv_ref, dv2)):
            for group in range(1 if shared else N_PAIRS):
                kc = kcols[group]

                @pl.when(step > 0)
                def _():
                    out_ref[pl.ds(pl.multiple_of(first_row - BLOCK, BLOCK), BLOCK), kc] = (
                        acc_ref[:, kc] + total(parts, 0, group)).astype(BF16)

                for w in range(1, qb):
                    out_ref[pl.ds(pl.multiple_of(first_row + (w - 1) * BLOCK, BLOCK), BLOCK), kc] = (
                        total(parts, w, group).astype(BF16))
                acc_ref[:, kc] = total(parts, qb, group)

        @pl.when(step == n_steps - 1)
        def _():
            dk_ref[pl.ds(n_rows - BLOCK, BLOCK), :] = ck_scr[...].astype(BF16)
            dv_ref[pl.ds(n_rows - BLOCK, BLOCK), :] = cv_scr[...].astype(BF16)

        kp_ref[...] = kc_ref[rows[-1], :]
        vp_ref[...] = vc_ref[rows[-1], :]

    cur = lambda width: pl.BlockSpec((qb * BLOCK, width), lambda s: (s, 0))
    whole = lambda width: pl.BlockSpec((n_rows, width), lambda s: (0, 0))
    flat = lambda a: a.reshape(n_rows, a.shape[2])
    in_specs = [cur(HALF_WIDTH), cur(ck), cur(ck), cur(HALF_WIDTH), cur(STAT_WIDTH), cur(STAT_WIDTH)]
    args = [flat(a) for a in (q, k, v, d_o, lse, delta)]
    out_specs = [cur(HALF_WIDTH), whole(ck), whole(ck)]
    out_shape = [jax.ShapeDtypeStruct((n_rows, HALF_WIDTH), BF16),
                 jax.ShapeDtypeStruct((n_rows, ck), BF16), jax.ShapeDtypeStruct((n_rows, ck), BF16)]
    if has_sinks:
        in_specs.append(pl.BlockSpec(sink_rows.shape, lambda s: (0, 0)))
        args.append(sink_rows)
        out_specs.append(pl.BlockSpec(sink_rows.shape, lambda s: (0, 0)))
        out_shape.append(jax.ShapeDtypeStruct(sink_rows.shape, F32))
    outs = pl.pallas_call(
        body, name=name, grid=(n_steps,), in_specs=in_specs,
        out_specs=tuple(out_specs), out_shape=tuple(out_shape),
        scratch_shapes=[pltpu.VMEM((BLOCK, ck), F32), pltpu.VMEM((BLOCK, ck), F32),
                        pltpu.VMEM((BLOCK, ck), BF16), pltpu.VMEM((BLOCK, ck), BF16)],
        compiler_params=_params(("arbitrary",)),
    )(*args)
    return tuple(o.reshape(n_seq, length, o.shape[1]) for o in outs[:3]) + tuple(outs[3:])


def _tail(oa, ob1, lb1, ob4, lb4, ob16, lb16, gate_a, gate_b, x, target, w_out, spread, gather):
    tm = ROW_TILE

    def split_dot(v, mat):
        hi = v.astype(BF16)
        lo = (v - hi.astype(F32)).astype(BF16)
        return jnp.dot(hi, mat, preferred_element_type=F32) + jnp.dot(lo, mat, preferred_element_type=F32)

    def body(oa_ref, ob1_ref, lb1_ref, ob4_ref, lb4_ref, ob16_ref, lb16_ref, ga_ref, gb_ref, x_ref, t_ref, w_ref,
             sp_ref, ga_mat_ref,
             loss_ref, dy_ref, gwo_ref, doa_ref, dla_ref, dga_ref, dgb_ref,
             dob_ref, dob4_ref, dob16_ref, dlb_ref, dlb4_ref, dlb16_ref, lse_ref, lse4_ref, lse16_ref,
             s_f, mix_keep, dy_keep):
        i = pl.program_id(0)
        sp, gat = sp_ref[...], ga_mat_ref[...]
        o4, o16 = _unfold_load(ob4_ref, s_f, 4, tm), _unfold_load(ob16_ref, s_f, 16, tm)
        l4, l16 = _unfold_load(lb4_ref, s_f, 4, tm), _unfold_load(lb16_ref, s_f, 16, tm)
        o1, l1 = ob1_ref[...].astype(F32), lb1_ref[...]
        mx = jnp.maximum(jnp.maximum(l1, l4), l16)
        e1, e4, e16 = jnp.exp(l1 - mx), jnp.exp(l4 - mx), jnp.exp(l16 - mx)
        den = e1 + e4 + e16
        inv = 1.0 / den
        ob = split_dot(e1 * inv, sp) * o1 + split_dot(e4 * inv, sp) * o4 + split_dot(e16 * inv, sp) * o16
        lse_b = mx + jnp.log(den)

        oa, ga, gb = oa_ref[...].astype(F32), ga_ref[...].astype(F32), gb_ref[...].astype(F32)
        sa, sb = _sigmoid(ga), _sigmoid(gb)
        mixed = jnp.concatenate(_unpair_tiles(_tiles(oa * (ga * sa))) + [ob * (gb * sb)], axis=1)
        mixed_bf = mixed.astype(BF16)
        w = w_ref[...]
        yv = x_ref[...] + jnp.dot(mixed_bf, w, preferred_element_type=F32)
        err = yv - t_ref[...]
        sq = jnp.sum(err * err, axis=0, keepdims=True)
        dy = err * (1.0 / D_MODEL)
        dy_ref[...] = dy
        dy_bf = dy.astype(BF16)
        mix_t = mixed.T.astype(BF16)

        @pl.when(i == 0)
        def _():
            loss_ref[...] = sq

        @pl.when(i > 0)
        def _():
            loss_ref[...] += sq

        @pl.when((i & 1) == 0)
        def _():
            mix_keep[...] = mix_t
            dy_keep[...] = dy_bf

        @pl.when((i & 1) == 1)
        def _():
            gw = jnp.dot(jnp.concatenate([mix_keep[...], mix_t], axis=1), jnp.concatenate([dy_keep[...], dy_bf], axis=0),
                         preferred_element_type=F32)

            @pl.when(i == 1)
            def _():
                gwo_ref[...] = gw

            @pl.when(i > 1)
            def _():
                gwo_ref[...] += gw

        dmix = lax.dot_general(dy_bf, w, (((1,), (1,)), ((), ())), preferred_element_type=F32)
        dma = jnp.concatenate(_pair_tiles(_tiles(dmix[:, :HALF_WIDTH])), axis=1)
        dmb = dmix[:, HALF_WIDTH:]

        doa = dma * (ga * sa)
        doa_ref[...] = doa.astype(BF16)
        dla_ref[...] = split_dot(doa * oa, gat)
        dga_ref[...] = (dma * oa * (sa * (1.0 + ga * (1.0 - sa)))).astype(BF16)
        dob = dmb * (gb * sb)
        dgb_ref[...] = (dmb * ob * (sb * (1.0 + gb * (1.0 - sb)))).astype(BF16)
        dlb = split_dot(dob * ob, gat)
        dob_ref[...] = dob.astype(BF16)
        _fold_store(dob, s_f, dob4_ref, dob16_ref, tm)
        dlb_ref[...] = dlb
        _fold_store(dlb, s_f, dlb4_ref, dlb16_ref, tm)
        lse_ref[...] = lse_b
        _fold_store(lse_b, s_f, lse4_ref, lse16_ref, tm)

    row = lambda width: pl.BlockSpec((tm, width), lambda i: (i, 0))
    full = lambda a: pl.BlockSpec(a.shape, lambda i: (0,) * a.ndim)
    fb_shapes, fb_specs = _fold_specs(tm, BF16)
    _, ff_specs = _fold_specs(tm, F32)
    st_shapes, st_specs = _fold_specs(tm, F32, STAT_WIDTH)
    nat = lambda dtype, width=HALF_WIDTH: jax.ShapeDtypeStruct((SEQ, width), dtype)
    return pl.pallas_call(
        body, name="tail", grid=(SEQ // tm,),
        in_specs=[row(HALF_WIDTH), row(HALF_WIDTH), row(STAT_WIDTH), ff_specs[0], st_specs[0], ff_specs[1], st_specs[1],
                  row(HALF_WIDTH), row(HALF_WIDTH), row(D_MODEL), row(D_MODEL), full(w_out), full(spread), full(gather)],
        out_specs=(pl.BlockSpec((1, D_MODEL), lambda i: (0, 0)), row(D_MODEL),
                   pl.BlockSpec((D_MODEL, D_MODEL), lambda i: (0, 0)),
                   row(HALF_WIDTH), row(STAT_WIDTH), row(HALF_WIDTH), row(HALF_WIDTH),
                   row(HALF_WIDTH), *fb_specs, row(STAT_WIDTH), *st_specs, row(STAT_WIDTH), *st_specs),
        out_shape=(jax.ShapeDtypeStruct((1, D_MODEL), F32), jax.ShapeDtypeStruct((SEQ, D_MODEL), F32),
                   jax.ShapeDtypeStruct((D_MODEL, D_MODEL), F32),
                   nat(BF16), nat(F32, STAT_WIDTH), nat(BF16), nat(BF16),
                   nat(BF16), *fb_shapes, nat(F32, STAT_WIDTH), *st_shapes, nat(F32, STAT_WIDTH), *st_shapes),
        scratch_shapes=[_fold_scratch(tm), pltpu.VMEM((D_MODEL, tm), BF16), pltpu.VMEM((tm, D_MODEL), BF16)],
        compiler_params=_params(("arbitrary",)),
    )(oa, ob1, lb1, ob4, lb4, ob16, lb16, gate_a, gate_b, x, target, w_out, spread, gather)


def _dproj_assemble(dqa, dka, dva, dga, dgb, dq1, dk1, dv1, dq4, dk4, dv4, dq16, dk16, dv16, tqa, tqb, tkb, tka,
                    qkg, cos4, sin4, bmean):
    tm = ROW_TILE

    def norm_rope_bwd(d_out, t, g, cos, sin, bm, scale):
        d_r = d_out * scale
        dyv = d_r * cos + _swap_halves(d_r * sin)
        rr = lax.rsqrt(_head_sum(t * t, bm) + EPS)
        that = t * rr
        dgain = jnp.sum(dyv * that, axis=0, keepdims=True)
        gdy = dyv * g
        dt = rr * (gdy - that * _head_sum(that * gdy, bm))
        return dt, dgain

    def body(dqa_ref, dka_ref, dva_ref, dga_ref, dgb_ref, dq1_ref, dk1_ref, dv1_ref, dq4_ref, dk4_ref, dv4_ref,
             dq16_ref, dk16_ref, dv16_ref, tqa_ref, tqb_ref, tkb_ref, tka_ref, qkg_ref, cos_ref, sin_ref, bm_ref,
             dproj_ref, dqkg_ref, s_f):
        i = pl.program_id(0)
        cos, sin, bm = cos_ref[...], sin_ref[...], bm_ref[...]

        def merged(nat_ref, f4_ref, f16_ref):
            return nat_ref[...].astype(F32) + _unfold_load(f4_ref, s_f, 4, tm) + _unfold_load(f16_ref, s_f, 16, tm)

        @pl.when(i == 0)
        def _():
            dqkg_ref[...] = jnp.zeros_like(dqkg_ref)

        def through(d_out, t, row, scale, c0, paired=False):
            g = qkg_ref[row:row + 1, :]
            tot = jnp.zeros((1, PAIR), F32)
            dts = []
            for j in range(d_out.shape[1] // PAIR):
                cols = slice(j * PAIR, (j + 1) * PAIR)
                dt, dg = norm_rope_bwd(d_out[:, cols], t[:, cols], g, cos, sin, bm, scale)
                dts.append(dt)
                tot = tot + dg
            if paired:
                dts = _unpair_tiles(dts)
            for j, dt in enumerate(dts):
                dproj_ref[:, c0 + j * PAIR:c0 + (j + 1) * PAIR] = dt.astype(BF16)
            dqkg_ref[row:row + 1, :] += tot

        through(dqa_ref[...].astype(F32), tqa_ref[...].astype(F32), 0, HEAD_DIM ** -0.5, C_QA, paired=True)
        through(dka_ref[...].astype(F32), tka_ref[...].astype(F32), 1, 1.0, C_KA)
        through(merged(dq1_ref, dq4_ref, dq16_ref), tqb_ref[...].astype(F32), 2, HEAD_DIM ** -0.5, C_QB)
        through(merged(dk1_ref, dk4_ref, dk16_ref), tkb_ref[...].astype(F32), 3, 1.0, C_KB)
        dproj_ref[:, C_VB:C_VB + HALF_WIDTH] = merged(dv1_ref, dv4_ref, dv16_ref).astype(BF16)
        dproj_ref[:, C_GA:C_GA + HALF_WIDTH] = jnp.concatenate(
            _unpair_tiles(_tiles(dga_ref[...].astype(F32))), axis=1).astype(BF16)
        dproj_ref[:, C_GB:C_GB + HALF_WIDTH] = dgb_ref[...].astype(BF16)
        dproj_ref[:, C_VA:C_VA + KV_A_WIDTH] = dva_ref[...].astype(BF16)

    row = lambda width: pl.BlockSpec((tm, width), lambda i: (i, 0))
    full = lambda a: pl.BlockSpec(a.shape, lambda i: (0,) * a.ndim)
    _, ff_specs = _fold_specs(tm, F32)
    return pl.pallas_call(
        body, name="dproj_assemble", grid=(SEQ // tm,),
        in_specs=[row(HALF_WIDTH), row(KV_A_WIDTH), row(KV_A_WIDTH), row(HALF_WIDTH), row(HALF_WIDTH),
                  row(HALF_WIDTH), row(HALF_WIDTH), row(HALF_WIDTH), ff_specs[0], ff_specs[0], ff_specs[0],
                  ff_specs[1], ff_specs[1], ff_specs[1],
                  row(HALF_WIDTH), row(HALF_WIDTH), row(HALF_WIDTH), row(KV_A_WIDTH),
                  full(qkg), row(PAIR), row(PAIR), full(bmean)],
        out_specs=(row(IN_WIDTH), pl.BlockSpec((SMALL_ROWS, PAIR), lambda i: (0, 0))),
        out_shape=(jax.ShapeDtypeStruct((SEQ, IN_WIDTH), BF16), jax.ShapeDtypeStruct((SMALL_ROWS, PAIR), F32)),
        scratch_shapes=[_fold_scratch(tm)],
        compiler_params=_params(("arbitrary",)),
    )(dqa, dka, dva, dga, dgb, dq1, dk1, dv1, dq4, dk4, dv4, dq16, dk16, dv16, tqa, tqb, tkb, tka, qkg, cos4, sin4, bmean)


def _input_grad_reduce(dproj, w, x, gain, dy, blocks_in, blocks_out, small):
    tm = ROW_TILE
    n_steps = SEQ // tm
    stage2_step, stage3_step = 3, 8
    shapes = (blocks_in.shape[1:], blocks_out.shape[1:])

    def body(dp_ref, w_ref, x_ref, g_ref, dy_ref, ga_hbm, gb_hbm, small_ref,
             gx_ref, out_a, out_b, small_out_ref, dgain_out_ref,
             part_a, part_b, sib_a, sib_b, wire_a, wire_b, chips_a, chips_b, small_all, dgain_acc, dgain_all,
             load_sems, sib_send, sib_recv, chip_send, chip_recv, small_send, small_recv, dgain_send, dgain_recv):
        i = pl.program_id(0)
        x, y, c = lax.axis_index("x"), lax.axis_index("y"), lax.axis_index("c")
        sibling = (x, y, 1 - c)
        chips = [(x, y), (1 - x, y), (x, 1 - y), (1 - x, 1 - y)]
        my_id = 4 * x + 2 * y + c
        g_hbm, part, from_sib = (ga_hbm, gb_hbm), (part_a, part_b), (sib_a, sib_b)
        to_wire, from_chips, out = (wire_a, wire_b), (chips_a, chips_b), (out_a, out_b)
        both = (0, 1)

        def blk(a, chip, core):
            return g_hbm[a].at[4 * chip[0] + 2 * chip[1] + core]

        def to_all(src, dst_all, send, recv):
            copies = []
            for rel in range(1, N_DEV):
                dx, dy_, dc = (rel >> 2) & 1, (rel >> 1) & 1, rel & 1
                to = (1 - x if dx else x, 1 - y if dy_ else y, 1 - c if dc else c)
                copies.append(pltpu.make_async_remote_copy(
                    src_ref=src, dst_ref=dst_all.at[my_id], send_sem=send.at[rel - 1], recv_sem=recv.at[rel - 1],
                    device_id=to, device_id_type=MESH))
            return copies

        small_copies = to_all(small_all.at[my_id], small_all, small_send, small_recv)
        dgain_copies = to_all(dgain_acc, dgain_all, dgain_send, dgain_recv)
        loads = [[pltpu.make_async_copy(blk(a, chips[k], c), part[a].at[k], load_sems.at[a, k]) for k in range(4)] for a in both]
        to_sib = [[pltpu.make_async_remote_copy(
            src_ref=blk(a, chips[k], 1 - c), dst_ref=from_sib[a].at[k], send_sem=sib_send.at[a, k], recv_sem=sib_recv.at[a, k],
            device_id=sibling, device_id_type=MESH) for k in range(4)] for a in both]
        first, other, k_first, k_other = _routes()
        to_chips = [[pltpu.make_async_remote_copy(
            src_ref=to_wire[a].at[s], dst_ref=from_chips[a].at[s], send_sem=chip_send.at[a, s], recv_sem=chip_recv.at[a, s],
            device_id=(first, first, other)[s], device_id_type=MESH) for s in range(3)] for a in both]

        def chip_partial(a, k):
            return part[a][k].astype(F32) + from_sib[a][k].astype(F32)

        @pl.when(i == 0)
        def _():
            small_all[my_id] = small_ref[...]
            for cp in small_copies:
                cp.start()
            for k in (1, 2, 3, 0):
                for a in both:
                    loads[a][k].start()
                    to_sib[a][k].start()

        @pl.when(i == stage2_step)
        def _():
            for k in (1, 2, 3):
                for a in both:
                    loads[a][k].wait()
                    to_sib[a][k].wait_recv()
            for s, k in ((0, 3), (1, k_first)):
                for a in both:
                    to_wire[a][s] = chip_partial(a, k).astype(BF16)
                    to_chips[a][s].start()

        @pl.when(i == stage3_step)
        def _():
            for a in both:
                to_chips[a][0].wait_recv()
                to_wire[a][2] = (chip_partial(a, k_other) + from_chips[a][0].astype(F32)).astype(BF16)
                to_chips[a][2].start()

        dh = lax.dot_general(dp_ref[...], w_ref[...], (((1,), (1,)), ((), ())), preferred_element_type=F32)
        xf = x_ref[...]
        r = lax.rsqrt(jnp.mean(xf * xf, axis=-1, keepdims=True) + EPS)
        xhat = xf * r
        dg = jnp.sum(dh * xhat, axis=0, keepdims=True)
        dxh = dh * g_ref[...]
        dx = r * (dxh - xhat * jnp.mean(dxh * xhat, axis=-1, keepdims=True))
        gx_ref[...] = dy_ref[...] + dx

        @pl.when(i == 0)
        def _():
            dgain_acc[...] = dg

        @pl.when(i > 0)
        def _():
            dgain_acc[...] += dg

        @pl.when(i == n_steps - 1)
        def _():
            dgain_all[my_id] = dgain_acc[...]
            for cp in dgain_copies:
                cp.start()
            for a in both:
                loads[a][0].wait()
                to_sib[a][0].wait_recv()
                acc = chip_partial(a, 0)
                for s in (1, 2):
                    to_chips[a][s].wait_recv()
                    acc = acc + from_chips[a][s].astype(F32)
                out[a][...] = acc
            for copies, gathered, dst in ((small_copies, small_all, small_out_ref), (dgain_copies, dgain_all, dgain_out_ref)):
                for cp in copies:
                    cp.wait_recv()
                tot = gathered[0]
                for d in range(1, N_DEV):
                    tot = tot + gathered[d]
                dst[...] = tot
            for cp in to_sib[0] + to_sib[1] + to_chips[0] + to_chips[1] + small_copies + dgain_copies:
                cp.wait_send()

    row = lambda width: pl.BlockSpec((tm, width), lambda i: (i, 0))
    full = lambda a: pl.BlockSpec(a.shape, lambda i: (0,) * a.ndim)
    whole = lambda shape: pl.BlockSpec(shape, lambda i: (0,) * len(shape))
    hbm = pl.BlockSpec(memory_space=pl.ANY)
    dtypes = (blocks_in.dtype, blocks_out.dtype)
    buf = lambda n, dts: [pltpu.VMEM((n,) + s, dt) for s, dt in zip(shapes, dts)]
    return pl.pallas_call(
        body, name="input_grad_rs", grid=(n_steps,),
        in_specs=[row(IN_WIDTH), full(w), row(D_MODEL), full(gain), row(D_MODEL), hbm, hbm, full(small)],
        out_specs=(row(D_MODEL), whole(shapes[0]), whole(shapes[1]), whole((SMALL_ROWS, SMALL_COLS)), whole((1, D_MODEL))),
        out_shape=(jax.ShapeDtypeStruct((SEQ, D_MODEL), F32), jax.ShapeDtypeStruct(shapes[0], F32),
                   jax.ShapeDtypeStruct(shapes[1], F32), jax.ShapeDtypeStruct((SMALL_ROWS, SMALL_COLS), F32),
                   jax.ShapeDtypeStruct((1, D_MODEL), F32)),
        scratch_shapes=[*buf(4, dtypes), *buf(4, dtypes), *buf(3, (BF16, BF16)), *buf(3, (BF16, BF16)),
                        pltpu.VMEM((N_DEV, SMALL_ROWS, SMALL_COLS), F32),
                        pltpu.VMEM((1, D_MODEL), F32), pltpu.VMEM((N_DEV, 1, D_MODEL), F32),
                        pltpu.SemaphoreType.DMA((2, 4)), pltpu.SemaphoreType.DMA((2, 4)), pltpu.SemaphoreType.DMA((2, 4)),
                        pltpu.SemaphoreType.DMA((2, 3)), pltpu.SemaphoreType.DMA((2, 3)),
                        pltpu.SemaphoreType.DMA((7,)), pltpu.SemaphoreType.DMA((7,)),
                        pltpu.SemaphoreType.DMA((7,)), pltpu.SemaphoreType.DMA((7,))],
        compiler_params=_params(("arbitrary",)),
    )(dproj, w, x, gain, dy, blocks_in, blocks_out, small)


def _weight_grad(h_t, dproj):
    tk = 1024
    cb = IN_WIDTH // 2
    n_k = SEQ // tk

    def body(ht_ref, dp_ref, out_ref, acc):
        k = pl.program_id(1)
        upd = jnp.dot(ht_ref[...], dp_ref[...], preferred_element_type=F32)

        @pl.when(k == 0)
        def _():
            acc[...] = upd

        @pl.when(k > 0)
        def _():
            acc[...] += upd

        @pl.when(k == n_k - 1)
        def _():
            for b in range(N_DEV // 2):
                out_ref[b] = acc[:, b * SHARD_IN:(b + 1) * SHARD_IN].astype(BF16)

    return pl.pallas_call(
        body, name="weight_grad", grid=(2, n_k),
        in_specs=[pl.BlockSpec((D_MODEL, tk), lambda j, k: (0, k)), pl.BlockSpec((tk, cb), lambda j, k: (k, j))],
        out_specs=pl.BlockSpec((N_DEV // 2, D_MODEL, SHARD_IN), lambda j, k: (j, 0, 0)),
        out_shape=jax.ShapeDtypeStruct((N_DEV, D_MODEL, SHARD_IN), BF16),
        scratch_shapes=[pltpu.VMEM((D_MODEL, cb), F32)],
        compiler_params=_params(("arbitrary", "arbitrary")),
    )(h_t, dproj)


def _adamw(name, w, g, m, v):
    def body(w_ref, g_ref, m_ref, v_ref, d_ref, nm_ref, nv_ref):
        gv = g_ref[...]
        nm = ADAM_B1 * m_ref[...] + (1.0 - ADAM_B1) * gv
        nv = ADAM_B2 * v_ref[...] + (1.0 - ADAM_B2) * jnp.square(gv)
        m_hat = nm / (1.0 - ADAM_B1 ** ADAM_STEP)
        v_hat = nv / (1.0 - ADAM_B2 ** ADAM_STEP)
        d_ref[...] = -ADAM_LR * (m_hat / (jnp.sqrt(v_hat) + ADAM_EPS) + ADAM_WD * w_ref[...])
        nm_ref[...] = nm
        nv_ref[...] = nv

    vmem = pl.BlockSpec(memory_space=pltpu.VMEM)
    out = jax.ShapeDtypeStruct(w.shape, F32)
    return pl.pallas_call(
        body, name=name, in_specs=[vmem] * 4, out_specs=(vmem,) * 3, out_shape=(out,) * 3,
        compiler_params=pltpu.CompilerParams(vmem_limit_bytes=VMEM_LIMIT),
    )(w, g, m, v)


SMALL_USED = D_MODEL + 4 * HEAD_DIM + 8


def _pack_small(norm_gain, qa, ka, sinks, qb, kb, extra=None):
    parts = [norm_gain.reshape(-1), qa.reshape(-1), ka.reshape(-1), sinks.reshape(-1), qb.reshape(-1), kb.reshape(-1)]
    if extra is not None:
        parts.append(extra.reshape(-1))
    flat = jnp.concatenate(parts)
    flat = jnp.pad(flat, (0, SMALL_ROWS * SMALL_COLS - flat.shape[0]))
    return flat.reshape(SMALL_ROWS, SMALL_COLS)


def _unpack_small(a):
    flat = a.reshape(-1)
    sizes = (D_MODEL, HEAD_DIM, HEAD_DIM, 8, HEAD_DIM, HEAD_DIM)
    out, off = [], 0
    for s in sizes:
        out.append(flat[off:off + s].reshape(1, s))
        off += s
    return out


def _fold_heads(row):
    return row[0, :HEAD_DIM] + row[0, HEAD_DIM:]


def kernel(x, norm_gain, w_in, q_norm_a, k_norm_a, sinks_a, q_norm_b, k_norm_b, w_out, loss_target, m_norm_gain, m_w_in, m_q_norm_a, m_k_norm_a, m_sinks_a, m_q_norm_b, m_k_norm_b, m_w_out, v_norm_gain, v_w_in, v_q_norm_a, v_k_norm_a, v_sinks_a, v_q_norm_b, v_k_norm_b, v_w_out):
    x2, tgt = x[0], loss_target[0]
    w_in_sh, w_out_sh = w_in[0], w_out[0]

    w_full = _all_gather_w_in(w_in_sh)

    inv = np.float32(ROPE_THETA) ** (-np.arange(HEAD_DIM // 2, dtype=np.float32) / np.float32(HEAD_DIM // 2))
    ang = np.arange(SEQ, dtype=np.float32)[:, None] * inv[None, :].astype(np.float32)
    cos, sin = np.cos(ang).astype(np.float32), np.sin(ang).astype(np.float32)
    cos4 = jnp.asarray(np.concatenate([cos, cos, cos, cos], axis=1))
    sin4 = jnp.asarray(np.concatenate([-sin, sin, -sin, sin], axis=1))
    blockdiag = np.kron(np.eye(2, dtype=np.float32), np.ones((HEAD_DIM, HEAD_DIM), np.float32))
    bmean = jnp.asarray(blockdiag / HEAD_DIM, dtype=BF16)
    gather_np = np.kron(np.eye(2 * N_PAIRS, dtype=np.float32), np.ones((HEAD_DIM, STAT_REP), np.float32))
    spread_np = np.kron(np.eye(2 * N_PAIRS, dtype=np.float32), np.ones((STAT_REP, HEAD_DIM), np.float32))
    spread_np[np.arange(STAT_WIDTH) % STAT_REP != 0] = 0.0
    gather, spread = jnp.asarray(gather_np, dtype=BF16), jnp.asarray(spread_np, dtype=BF16)
    two = lambda g: jnp.concatenate([g, g], axis=1)
    qkg = jnp.concatenate([two(q_norm_a), two(k_norm_a), two(q_norm_b), two(k_norm_b),
                           jnp.zeros((SMALL_ROWS - 4, PAIR), F32)], axis=0)
    sinks_paired = jnp.stack([sinks_a[0, :N_PAIRS], sinks_a[0, N_PAIRS:]], axis=1)
    sink_rows = jnp.concatenate([jnp.repeat(sinks_paired, BLOCK, axis=1),
                                 jnp.zeros((SMALL_ROWS - N_PAIRS, 2 * BLOCK), F32)], axis=0)

    (tqa, tka, tqb, tkb, gate_a, gate_b, h_t, qa, ka, va, qb, kb, vb, qb4, qb16, kb4, kb16, vb4, vb16,
     gathered_out) = _proj_fwd(x2, norm_gain, w_full, qkg, cos4, sin4, bmean, w_out_sh)
    wo_full = gathered_out.reshape(D_MODEL, D_MODEL)
    oa, la = _attn_fwd("attn_a_fwd", qa[None], ka[None], va[None], sink_rows, BLOCK - 1)
    ob1, lb1 = _attn_fwd("attn_b1_fwd", qb[None], kb[None], vb[None], None, BLOCK)
    ob4, lb4 = _attn_fwd("attn_b4_fwd", qb4, kb4, vb4, None, BLOCK)
    ob16, lb16 = _attn_fwd("attn_b16_fwd", qb16, kb16, vb16, None, BLOCK)
    (loss_cols, dy, gwo, doa, dla, dga, dgb, dob, dob4, dob16, dlb, dlb4, dlb16, lse_b, lse4, lse16) = _tail(
        oa[0], ob1[0], lb1[0], ob4, lb4, ob16, lb16, gate_a, gate_b, x2, tgt, wo_full, spread, gather)

    dqa, dka, dva, dsink = _attn_bwd("attn_a_bwd", qa[None], ka[None], va[None], doa[None], la, dla[None], sink_rows, BLOCK - 1)
    dq1, dk1, dv1 = _attn_bwd("attn_b1_bwd", qb[None], kb[None], vb[None], dob[None], lse_b[None], dlb[None], None, BLOCK)
    dq4, dk4, dv4 = _attn_bwd("attn_b4_bwd", qb4, kb4, vb4, dob4, lse4, dlb4, None, BLOCK)
    dq16, dk16, dv16 = _attn_bwd("attn_b16_bwd", qb16, kb16, vb16, dob16, lse16, dlb16, None, BLOCK)
    dproj, dqkg = _dproj_assemble(dqa[0], dka[0], dva[0], dga, dgb, dq1[0], dk1[0], dv1[0], dq4, dk4, dv4,
                                  dq16, dk16, dv16, tqa, tqb, tkb, tka, qkg, cos4, sin4, bmean)
    gw_in = _weight_grad(h_t, dproj)

    blocks_in = gw_in
    blocks_out = gwo.reshape(N_DEV, SHARD_OUT, D_MODEL)
    g_sinks = jnp.concatenate([jnp.sum(dsink[:N_PAIRS, :BLOCK], axis=1), jnp.sum(dsink[:N_PAIRS, BLOCK:], axis=1)])
    small = _pack_small(jnp.zeros((D_MODEL,), F32), _fold_heads(dqkg[0:1]), _fold_heads(dqkg[1:2]), g_sinks,
                        _fold_heads(dqkg[2:3]), _fold_heads(dqkg[3:4]), extra=0.5 * jnp.sum(loss_cols) / D_MODEL)
    grad_x, g_w_in, g_w_out, small_red, dgain_red = _input_grad_reduce(
        dproj, w_full, x2, norm_gain, dy, blocks_in, blocks_out, small)
    n_gain_rows = D_MODEL // SMALL_COLS
    small_red = jnp.concatenate([dgain_red.reshape(n_gain_rows, SMALL_COLS), small_red[n_gain_rows:]], axis=0)
    g_small = _unpack_small(small_red)

    d_in, nm_in, nv_in = _adamw("adamw_w_in", w_in_sh, g_w_in, m_w_in[0], v_w_in[0])
    d_out, nm_out, nv_out = _adamw("adamw_w_out", w_out_sh, g_w_out, m_w_out[0], v_w_out[0])
    d_s, nm_s, nv_s = _adamw(
        "adamw_small",
        _pack_small(norm_gain, q_norm_a, k_norm_a, sinks_a, q_norm_b, k_norm_b), small_red,
        _pack_small(m_norm_gain, m_q_norm_a, m_k_norm_a, m_sinks_a, m_q_norm_b, m_k_norm_b),
        _pack_small(v_norm_gain, v_q_norm_a, v_k_norm_a, v_sinks_a, v_q_norm_b, v_k_norm_b))
    d_small, nm_small, nv_small = _unpack_small(d_s), _unpack_small(nm_s), _unpack_small(nv_s)

    loss = small_red.reshape(-1)[SMALL_USED]

    def assemble(small_list, big_in, big_out):
        ng, qa_, ka_, sk_, qb_, kb_ = small_list
        return [ng, big_in[None], qa_, ka_, sk_, qb_, kb_, big_out[None]]

    return (loss, grad_x[None], *assemble(g_small, g_w_in, g_w_out), *assemble(d_small, d_in, d_out),
            *assemble(nm_small, nm_in, nm_out), *assemble(nv_small, nv_in, nv_out))
```

```python
import functools

import numpy as np
import jax
import jax.numpy as jnp
from jax import lax
from jax.experimental import pallas as pl
from jax.experimental.pallas import tpu as pltpu

F32 = jnp.float32
BF16 = jnp.bfloat16

SEQ = 4096
D_MODEL = 1024
HEAD_DIM = 64
PAIR = 2 * HEAD_DIM
N_PAIRS = 4
HALF_WIDTH = N_PAIRS * PAIR
KV_A_WIDTH = 128
IN_WIDTH = 3328
BLOCK = 128
STAT_REP = 16
STAT_WIDTH = 128
EPS = 1e-6
NEG = -1e30
ROPE_THETA = 10000.0
N_DEV = 8
SHARD_IN = IN_WIDTH // N_DEV
SHARD_OUT = D_MODEL // N_DEV
PAYLOAD = SHARD_IN + SHARD_OUT
SMALL_ROWS, SMALL_COLS = 8, 256

C_QA, C_KA, C_VA, C_GA, C_QB, C_KB, C_VB, C_GB = 0, 512, 640, 768, 1280, 1792, 2304, 2816

ADAM_LR = 0.001
ADAM_B1 = 0.9
ADAM_B2 = 0.999
ADAM_EPS = 1e-08
ADAM_WD = 0.01
ADAM_STEP = 10

ROW_TILE = 256
PROJ_ROW_TILE = 512
FWD_BLOCKS_PER_STEP = 8
BWD_BLOCKS_PER_STEP = 8
VMEM_LIMIT = 56 * 1024 * 1024

MESH = pl.DeviceIdType.MESH


def _params(sem, vmem=VMEM_LIMIT):
    return pltpu.CompilerParams(dimension_semantics=sem, vmem_limit_bytes=vmem)


def _head_sum(v, bm):
    hi = v.astype(BF16)
    lo = (v - hi.astype(F32)).astype(BF16)
    return (jnp.dot(hi, bm, preferred_element_type=F32) + jnp.dot(lo, bm, preferred_element_type=F32))


def _swap_halves(y):
    lane = lax.broadcasted_iota(jnp.int32, y.shape, 1)
    first = (lane & 32) == 0
    return jnp.where(first, pltpu.roll(y, 96, 1), pltpu.roll(y, 32, 1))


def _sigmoid(g):
    return 1.0 / (1.0 + jnp.exp(-g))


def _tiles(a):
    return [a[:, j * PAIR:(j + 1) * PAIR] for j in range(N_PAIRS)]


def _pair_tiles(t):
    low = lax.broadcasted_iota(jnp.int32, t[0].shape, 1) < HEAD_DIM
    r = [pltpu.roll(a, HEAD_DIM, 1) for a in t]
    return [jnp.where(low, t[0], r[2]), jnp.where(low, r[0], t[2]), jnp.where(low, t[1], r[3]), jnp.where(low, r[1], t[3])]


def _unpair_tiles(p):
    low = lax.broadcasted_iota(jnp.int32, p[0].shape, 1) < HEAD_DIM
    r = [pltpu.roll(a, HEAD_DIM, 1) for a in p]
    return [jnp.where(low, p[0], r[1]), jnp.where(low, p[2], r[3]), jnp.where(low, r[0], p[1]), jnp.where(low, r[2], p[3])]


def _routes():
    x, y, c = lax.axis_index("x"), lax.axis_index("y"), lax.axis_index("c")
    north = c == 1
    first = (jnp.where(north, 1 - x, x), jnp.where(north, y, 1 - y), c)
    other = (jnp.where(north, x, 1 - x), jnp.where(north, 1 - y, y), c)
    k_first = jnp.where(north, 1, 2)
    return first, other, k_first, 3 - k_first


def _gather_plan(mine_ref, out_ref, send_sems, recv_sems):
    x, y, c = lax.axis_index("x"), lax.axis_index("y"), lax.axis_index("c")
    me, sibling, diag = (x, y, c), (x, y, 1 - c), (1 - x, 1 - y, c)
    first, other, k_first, k_other = _routes()

    def slot(px, py, pc):
        return out_ref.at[4 * px + 2 * py + pc]

    def copy(k, block, to, from_mine=False):
        return pltpu.make_async_remote_copy(
            src_ref=mine_ref if from_mine else slot(*block), dst_ref=slot(*block),
            send_sem=send_sems.at[k], recv_sem=recv_sems.at[k], device_id=to, device_id_type=MESH)

    sends = [copy(0, me, sibling, True), copy(1, me, (1 - x, y, c), True), copy(2, me, (x, 1 - y, c), True)]
    stages = [(copy(k_first, first, me), [copy(3, first, other), copy(3 + k_first, first, sibling)]),
              (copy(k_other, other, me), [copy(3 + k_other, other, sibling)]),
              (copy(3, diag, me), [copy(6, diag, sibling)])]
    from_sibling = [copy(0, sibling, me), copy(4, (1 - x, y, 1 - c), me), copy(5, (x, 1 - y, 1 - c), me),
                    copy(6, (1 - x, 1 - y, 1 - c), me)]
    return slot(*me), sends, stages, from_sibling


GATHER_SCRATCH = [pltpu.SemaphoreType.DMA((7,)), pltpu.SemaphoreType.DMA((7,))]


def _all_gather_w_in(w_in_sh):
    rows, cols = w_in_sh.shape

    def body(w_ref, out_ref, mine_ref, blocks, send_sems, recv_sems):
        mine_ref[...] = w_ref[...].astype(BF16)
        my_slot, sends, stages, from_sibling = _gather_plan(mine_ref, blocks, send_sems, recv_sems)
        for cp in sends:
            cp.start()
        my_slot[...] = mine_ref[...]
        for arrival, forwards in stages:
            arrival.wait_recv()
            for cp in forwards:
                cp.start()
        for arrival in from_sibling:
            arrival.wait_recv()
        for cp in sends + [cp for _, forwards in stages for cp in forwards]:
            cp.wait_send()
        for d in range(N_DEV):
            out_ref[:, d * cols:(d + 1) * cols] = blocks[d]

    vmem = pl.BlockSpec(memory_space=pltpu.VMEM)
    return pl.pallas_call(
        body, name="ag_w_in",
        out_shape=jax.ShapeDtypeStruct((rows, N_DEV * cols), BF16),
        in_specs=[vmem], out_specs=vmem,
        scratch_shapes=[pltpu.VMEM((rows, cols), BF16), pltpu.VMEM((N_DEV, rows, cols), BF16)] + GATHER_SCRATCH,
        compiler_params=pltpu.CompilerParams(vmem_limit_bytes=VMEM_LIMIT),
    )(w_in_sh)


def _fold_scratch(tm):
    return pltpu.VMEM((N_PAIRS, tm, PAIR), F32)


def _fold_store(val, scr, out4, out16, tm):
    groups = range(val.shape[1] // PAIR)
    for j in groups:
        scr[j] = val[:, j * PAIR:(j + 1) * PAIR]
    for dil, out in ((4, out4), (16, out16)):
        for r in range(dil):
            for j in groups:
                out[r, :, j * PAIR:(j + 1) * PAIR] = scr[j, pl.ds(r, tm // dil, stride=dil), :].astype(out.dtype)


def _unfold_load(src, scr, dil, tm):
    groups = range(src.shape[2] // PAIR)
    for r in range(dil):
        for j in groups:
            scr[j, pl.ds(r, tm // dil, stride=dil), :] = src[r, :, j * PAIR:(j + 1) * PAIR].astype(F32)
    return jnp.concatenate([scr[j] for j in groups], axis=1)


def _fold_specs(tm, dtype, width=HALF_WIDTH):
    shapes = (jax.ShapeDtypeStruct((4, SEQ // 4, width), dtype), jax.ShapeDtypeStruct((16, SEQ // 16, width), dtype))
    specs = (pl.BlockSpec((4, tm // 4, width), lambda i: (0, i, 0)),
             pl.BlockSpec((16, tm // 16, width), lambda i: (0, i, 0)))
    return shapes, specs


def _proj_fwd(x, gain, w, qkg, cos4, sin4, bmean, w_out_sh):
    tm = PROJ_ROW_TILE
    n_steps = SEQ // tm

    def norm_rope(t, g, cos, sin, bm, scale):
        rr = lax.rsqrt(_head_sum(t * t, bm) + EPS)
        yv = t * rr * g
        return (yv * cos + _swap_halves(yv) * sin) * scale

    def body(x_ref, g_ref, w_ref, qkg_ref, cos_ref, sin_ref, bm_ref, wo_ref,
             tqa_ref, tka_ref, tqb_ref, tkb_ref, ga_ref, gb_ref, ht_ref, qa_ref, ka_ref, va_ref, qb_ref, kb_ref, vb_ref,
             qb4_ref, qb16_ref, kb4_ref, kb16_ref, vb4_ref, vb16_ref, wo_all_ref,
             proj, scr, wo_mine, wo_all, send_sems, recv_sems):
        i = pl.program_id(0)
        my_slot, sends, stages, from_sibling = _gather_plan(wo_mine, wo_all, send_sems, recv_sems)

        @pl.when(i == 0)
        def _():
            wo_mine[...] = wo_ref[...].astype(BF16)
            for cp in sends:
                cp.start()
            my_slot[...] = wo_mine[...]

        @pl.when(i == n_steps // 2)
        def _():
            for arrival, forwards in stages[:2]:
                arrival.wait_recv()
                for cp in forwards:
                    cp.start()

        xf = x_ref[...]
        r = lax.rsqrt(jnp.mean(xf * xf, axis=-1, keepdims=True) + EPS)
        hf = xf * r * g_ref[...]
        ht_ref[...] = hf.T.astype(BF16)
        cos, sin, bm = cos_ref[...], sin_ref[...], bm_ref[...]
        proj[...] = jnp.dot(hf.astype(BF16), w_ref[...], preferred_element_type=F32)

        def roped(tiles, row, scale):
            g = qkg_ref[row:row + 1, :]
            return jnp.concatenate([norm_rope(t, g, cos, sin, bm, scale) for t in tiles], axis=1)

        tqa = _pair_tiles(_tiles(proj[:, C_QA:C_QA + HALF_WIDTH]))
        tqa_ref[...] = jnp.concatenate(tqa, axis=1).astype(BF16)
        qa_ref[...] = roped(tqa, 0, HEAD_DIM ** -0.5).astype(BF16)
        ga_ref[...] = jnp.concatenate(_pair_tiles(_tiles(proj[:, C_GA:C_GA + HALF_WIDTH])), axis=1).astype(BF16)
        gb_ref[...] = proj[:, C_GB:C_GB + HALF_WIDTH].astype(BF16)
        tqb = proj[:, C_QB:C_QB + HALF_WIDTH]
        tqb_ref[...] = tqb.astype(BF16)
        qb = roped(_tiles(tqb), 2, HEAD_DIM ** -0.5)
        qb_ref[...] = qb.astype(BF16)
        _fold_store(qb, scr, qb4_ref, qb16_ref, tm)
        tkb = proj[:, C_KB:C_KB + HALF_WIDTH]
        tkb_ref[...] = tkb.astype(BF16)
        kb = roped(_tiles(tkb), 3, 1.0)
        kb_ref[...] = kb.astype(BF16)
        _fold_store(kb, scr, kb4_ref, kb16_ref, tm)
        vb = proj[:, C_VB:C_VB + HALF_WIDTH]
        vb_ref[...] = vb.astype(BF16)
        _fold_store(vb, scr, vb4_ref, vb16_ref, tm)
        tka = proj[:, C_KA:C_KA + KV_A_WIDTH]
        tka_ref[...] = tka.astype(BF16)
        ka_ref[...] = roped([tka], 1, 1.0).astype(BF16)
        va_ref[...] = proj[:, C_VA:C_VA + KV_A_WIDTH].astype(BF16)

        @pl.when(i == n_steps - 1)
        def _():
            arrival, forwards = stages[2]
            arrival.wait_recv()
            for cp in forwards:
                cp.start()
            for arrival in from_sibling:
                arrival.wait_recv()
            for cp in sends + [cp for _, forwards in stages for cp in forwards]:
                cp.wait_send()
            wo_all_ref[...] = wo_all[...]

    row = lambda width: pl.BlockSpec((tm, width), lambda i: (i, 0))
    full = lambda a: pl.BlockSpec(a.shape, lambda i: (0,) * a.ndim)
    nat = lambda width, dtype=BF16: jax.ShapeDtypeStruct((SEQ, width), dtype)
    f_shapes, f_specs = _fold_specs(tm, BF16)
    return pl.pallas_call(
        body, name="proj_fwd", grid=(SEQ // tm,),
        in_specs=[row(D_MODEL), full(gain), full(w), full(qkg), row(PAIR), row(PAIR), full(bmean), full(w_out_sh)],
        out_specs=(row(HALF_WIDTH), row(KV_A_WIDTH), row(HALF_WIDTH), row(HALF_WIDTH), row(HALF_WIDTH), row(HALF_WIDTH),
                   pl.BlockSpec((D_MODEL, tm), lambda i: (0, i)),
                   row(HALF_WIDTH), row(KV_A_WIDTH), row(KV_A_WIDTH), row(HALF_WIDTH), row(HALF_WIDTH), row(HALF_WIDTH),
                   *f_specs, *f_specs, *f_specs,
                   pl.BlockSpec((N_DEV,) + w_out_sh.shape, lambda i: (0, 0, 0))),
        out_shape=(nat(HALF_WIDTH), nat(KV_A_WIDTH), nat(HALF_WIDTH), nat(HALF_WIDTH), nat(HALF_WIDTH), nat(HALF_WIDTH),
                   jax.ShapeDtypeStruct((D_MODEL, SEQ), BF16),
                   nat(HALF_WIDTH), nat(KV_A_WIDTH), nat(KV_A_WIDTH), nat(HALF_WIDTH), nat(HALF_WIDTH), nat(HALF_WIDTH),
                   *f_shapes, *f_shapes, *f_shapes,
                   jax.ShapeDtypeStruct((N_DEV,) + w_out_sh.shape, BF16)),
        scratch_shapes=[pltpu.VMEM((tm, IN_WIDTH), F32), _fold_scratch(tm), pltpu.VMEM(w_out_sh.shape, BF16),
                        pltpu.VMEM((N_DEV,) + w_out_sh.shape, BF16)] + GATHER_SCRATCH,
        compiler_params=_params(("arbitrary",)),
    )(x, gain, w, qkg, cos4, sin4, bmean, w_out_sh)


def _band_mask(i, max_dist):
    j = lax.broadcasted_iota(jnp.int32, (2 * BLOCK, 2 * BLOCK), 0)
    c = lax.broadcasted_iota(jnp.int32, (2 * BLOCK, 2 * BLOCK), 1)
    dist = (c & (BLOCK - 1)) + BLOCK - j
    return (dist >= 0) & (dist <= max_dist) & ((j >= BLOCK) | (i > 0))


def _stack_heads(t):
    lane = lax.broadcasted_iota(jnp.int32, t.shape, 1)
    low = lane < HEAD_DIM
    zero = jnp.zeros_like(t)
    return jnp.concatenate([jnp.where(low, t, zero), jnp.where(low, zero, t)], axis=0)


def _unstack_t(t):
    return jnp.concatenate([t[:HEAD_DIM, :BLOCK], t[HEAD_DIM:, BLOCK:]], axis=0).T


def _rows_to_stats(rows):
    parts = []
    for row in rows:
        parts.append(jnp.broadcast_to(row[:, :BLOCK], (STAT_REP, BLOCK)))
        parts.append(jnp.broadcast_to(row[:, BLOCK:], (STAT_REP, BLOCK)))
    return jnp.concatenate(parts, axis=0).T


def _stats_to_rows(t):
    tt = t.T
    return [jnp.concatenate([tt[2 * p * STAT_REP:2 * p * STAT_REP + 1, :],
                             tt[(2 * p + 1) * STAT_REP:(2 * p + 1) * STAT_REP + 1, :]], axis=1) for p in range(N_PAIRS)]


def _attn_fwd(name, patterns):
    qb = FWD_BLOCKS_PER_STEP
    steps = SEQ // (qb * BLOCK)

    def one_pattern(step, nb, shared, max_dist, q_ref, kc_ref, vc_ref, sink_ref, o_ref, lse_ref, kp_ref, vp_ref):
        has_sinks = sink_ref is not None

        @pl.when(step == 0)
        def _():
            kp_ref[...] = jnp.zeros_like(kp_ref)
            vp_ref[...] = jnp.zeros_like(vp_ref)

        valid = [_band_mask((step * qb + b) & (nb - 1), max_dist) for b in range(qb)]
        cols = [slice(p * PAIR, (p + 1) * PAIR) for p in range(N_PAIRS)]
        kcols = [slice(0, PAIR) if shared else c for c in cols]
        rows = [slice(b * BLOCK, (b + 1) * BLOCK) for b in range(qb)]
        units = [(b, p) for b in range(qb) for p in range(N_PAIRS)]
        n = range(len(units))

        def window(prev_ref, cur_ref, b, kc):
            before = prev_ref[:, kc] if b == 0 else cur_ref[rows[b - 1], kc]
            return jnp.concatenate([before, cur_ref[rows[b], kc]], axis=0)

        st = [lax.dot_general(window(kp_ref, kc_ref, b, kcols[p]), _stack_heads(q_ref[rows[b], cols[p]]),
                              (((1,), (1,)), ((), ())), preferred_element_type=F32) for b, p in units]
        st = [jnp.where(valid[units[u][0]], st[u], NEG) for u in n]
        m = [jnp.max(s, axis=0, keepdims=True) for s in st]
        if has_sinks:
            sk = [sink_ref[p:p + 1, :] for _, p in units]
            m = [jnp.maximum(m[u], sk[u]) for u in n]
        pt = [jnp.exp(st[u] - m[u]) for u in n]
        l = [jnp.sum(t, axis=0, keepdims=True) for t in pt]
        if has_sinks:
            l = [l[u] + jnp.exp(sk[u] - m[u]) for u in n]
        v2t = [window(vp_ref, vc_ref, b, kcols[p]).astype(F32).T.astype(BF16) for b, p in units]
        ot = [jnp.dot(v2t[u], pt[u].astype(BF16), preferred_element_type=F32) / l[u] for u in n]
        for u, (b, p) in enumerate(units):
            o_ref[rows[b], cols[p]] = _unstack_t(ot[u]).astype(BF16)
        for b in range(qb):
            lse_ref[rows[b], :] = _rows_to_stats([m[u] + jnp.log(l[u]) for u in n if units[u][0] == b])
        kp_ref[...] = kc_ref[rows[-1], :]
        vp_ref[...] = vc_ref[rows[-1], :]

    n_in = [4 if sinks is not None else 3 for _, _, _, sinks, _ in patterns]

    def body(*refs):
        ins, rest = refs[:sum(n_in)], refs[sum(n_in):]
        outs, scratch = rest[:2 * len(patterns)], rest[2 * len(patterns):]
        step = pl.program_id(0)
        first = 0
        for p, (q, k, _, sinks, max_dist) in enumerate(patterns):
            mine = ins[first:first + n_in[p]]
            first += n_in[p]
            sink_ref = mine[3] if sinks is not None else None
            kp_ref, vp_ref = scratch[2 * p], scratch[2 * p + 1]

            @pl.when((step >= p * steps) & (step < (p + 1) * steps))
            def _():
                one_pattern(step - p * steps, q.shape[1] // BLOCK, k.shape[2] == PAIR, max_dist,
                            mine[0], mine[1], mine[2], sink_ref, outs[2 * p], outs[2 * p + 1], kp_ref, vp_ref)

    def during(p, width):
        return pl.BlockSpec((qb * BLOCK, width), lambda s: (jnp.clip(s - p * steps, 0, steps - 1), 0))

    flat = lambda a: a.reshape(SEQ, a.shape[2])
    in_specs, args, out_specs, out_shape, scratch = [], [], [], [], []
    for p, (q, k, v, sinks, _) in enumerate(patterns):
        ck = k.shape[2]
        in_specs += [during(p, HALF_WIDTH), during(p, ck), during(p, ck)]
        args += [flat(q), flat(k), flat(v)]
        if sinks is not None:
            in_specs.append(pl.BlockSpec(sinks.shape, lambda s: (0, 0)))
            args.append(sinks)
        out_specs += [during(p, HALF_WIDTH), during(p, STAT_WIDTH)]
        out_shape += [jax.ShapeDtypeStruct((SEQ, HALF_WIDTH), BF16), jax.ShapeDtypeStruct((SEQ, STAT_WIDTH), F32)]
        scratch += [pltpu.VMEM((BLOCK, ck), BF16), pltpu.VMEM((BLOCK, ck), BF16)]
    outs = pl.pallas_call(
        body, name=name, grid=(len(patterns) * steps,), in_specs=in_specs,
        out_specs=tuple(out_specs), out_shape=tuple(out_shape), scratch_shapes=scratch,
        compiler_params=_params(("arbitrary",)),
    )(*args)
    return [(outs[2 * p].reshape(q.shape), outs[2 * p + 1].reshape(q.shape[0], q.shape[1], STAT_WIDTH))
            for p, (q, _, _, _, _) in enumerate(patterns)]


def _attn_bwd(name, q, k, v, d_o, lse, delta, sink_rows, max_dist):
    n_seq, length, _ = q.shape
    ck = k.shape[2]
    nb = length // BLOCK
    n_blocks = n_seq * nb
    n_rows = n_seq * length
    shared = ck == PAIR
    has_sinks = sink_rows is not None
    qb = BWD_BLOCKS_PER_STEP
    n_steps = n_blocks // qb

    def body(*refs):
        if has_sinks:
            (q_ref, kc_ref, vc_ref, do_ref, lse_ref, dl_ref, sink_ref,
             dq_ref, dk_ref, dv_ref, dsink_ref, ck_scr, cv_scr, kp_ref, vp_ref) = refs
        else:
            (q_ref, kc_ref, vc_ref, do_ref, lse_ref, dl_ref,
             dq_ref, dk_ref, dv_ref, ck_scr, cv_scr, kp_ref, vp_ref) = refs
        step = pl.program_id(0)

        @pl.when(step == 0)
        def _():
            ck_scr[...] = jnp.zeros_like(ck_scr)
            cv_scr[...] = jnp.zeros_like(cv_scr)
            kp_ref[...] = jnp.zeros_like(kp_ref)
            vp_ref[...] = jnp.zeros_like(vp_ref)
            if has_sinks:
                dsink_ref[...] = jnp.zeros_like(dsink_ref)

        valid = [_band_mask((step * qb + b) & (nb - 1), max_dist) for b in range(qb)]
        cols = [slice(p * PAIR, (p + 1) * PAIR) for p in range(N_PAIRS)]
        kcols = [slice(0, PAIR) if shared else c for c in cols]
        rows = [slice(b * BLOCK, (b + 1) * BLOCK) for b in range(qb)]
        units = [(b, p) for b in range(qb) for p in range(N_PAIRS)]
        n = range(len(units))
        nt = (((1,), (1,)), ((), ()))

        def window(prev_ref, cur_ref, b, kc):
            before = prev_ref[:, kc] if b == 0 else cur_ref[rows[b - 1], kc]
            return jnp.concatenate([before, cur_ref[rows[b], kc]], axis=0)

        q_st = [_stack_heads(q_ref[rows[b], cols[p]]) for b, p in units]
        do_st = [_stack_heads(do_ref[rows[b], cols[p]]) for b, p in units]
        k2 = [window(kp_ref, kc_ref, b, kcols[p]) for b, p in units]
        v2 = [window(vp_ref, vc_ref, b, kcols[p]) for b, p in units]
        st = [lax.dot_general(k2[u], q_st[u], nt, preferred_element_type=F32) for u in n]
        dpt = [lax.dot_general(v2[u], do_st[u], nt, preferred_element_type=F32) for u in n]
        lse_rows = [_stats_to_rows(lse_ref[rows[b], :]) for b in range(qb)]
        dl_rows = [_stats_to_rows(dl_ref[rows[b], :]) for b in range(qb)]
        lse_row = [lse_rows[b][p] for b, p in units]
        dl_row = [dl_rows[b][p] for b, p in units]
        pt = [jnp.exp(jnp.where(valid[units[u][0]], st[u], NEG) - lse_row[u]) for u in n]
        dst = [(pt[u] * (dpt[u] - dl_row[u])).astype(BF16) for u in n]
        ptb = [t.astype(BF16) for t in pt]
        dv2 = [jnp.dot(ptb[u], do_st[u], preferred_element_type=F32) for u in n]
        dk2 = [jnp.dot(dst[u], q_st[u], preferred_element_type=F32) for u in n]
        k2t = [k2[u].astype(F32).T.astype(BF16) for u in n]
        dqt = [jnp.dot(k2t[u], dst[u], preferred_element_type=F32) for u in n]
        for u, (b, p) in enumerate(units):
            dq_ref[rows[b], cols[p]] = _unstack_t(dqt[u]).astype(BF16)
        if has_sinks:
            for u, (b, p) in enumerate(units):
                p_sink = jnp.exp(sink_ref[p:p + 1, :] - lse_row[u])
                dsink_ref[p:p + 1, :] = dsink_ref[p:p + 1, :] - p_sink * dl_row[u]

        def total(parts, w, group):
            sel = [u for u, (b, p) in enumerate(units) if (shared or p == group)]
            terms = ([parts[u][:BLOCK] for u in sel if units[u][0] == w]
                     + [parts[u][BLOCK:] for u in sel if units[u][0] == w - 1])
            tot = terms[0]
            for t in terms[1:]:
                tot = tot + t
            return tot

        first_row = step * (qb * BLOCK)
        for acc_ref, out_ref, parts in ((ck_scr, dk_ref, dk2), (cv_scr, dv_ref, dv2)):
            for group in range(1 if shared else N_PAIRS):
                kc = kcols[group]

                @pl.when(step > 0)
                def _():
                    out_ref[pl.ds(pl.multiple_of(first_row - BLOCK, BLOCK), BLOCK), kc] = (
                        acc_ref[:, kc] + total(parts, 0, group)).astype(BF16)

                for w in range(1, qb):
                    out_ref[pl.ds(pl.multiple_of(first_row + (w - 1) * BLOCK, BLOCK), BLOCK), kc] = (
                        total(parts, w, group).astype(BF16))
                acc_ref[:, kc] = total(parts, qb, group)

        @pl.when(step == n_steps - 1)
        def _():
            dk_ref[pl.ds(n_rows - BLOCK, BLOCK), :] = ck_scr[...].astype(BF16)
            dv_ref[pl.ds(n_rows - BLOCK, BLOCK), :] = cv_scr[...].astype(BF16)

        kp_ref[...] = kc_ref[rows[-1], :]
        vp_ref[...] = vc_ref[rows[-1], :]

    cur = lambda width: pl.BlockSpec((qb * BLOCK, width), lambda s: (s, 0))
    whole = lambda width: pl.BlockSpec((n_rows, width), lambda s: (0, 0))
    flat = lambda a: a.reshape(n_rows, a.shape[2])
    in_specs = [cur(HALF_WIDTH), cur(ck), cur(ck), cur(HALF_WIDTH), cur(STAT_WIDTH), cur(STAT_WIDTH)]
    args = [flat(a) for a in (q, k, v, d_o, lse, delta)]
    out_specs = [cur(HALF_WIDTH), whole(ck), whole(ck)]
    out_shape = [jax.ShapeDtypeStruct((n_rows, HALF_WIDTH), BF16),
                 jax.ShapeDtypeStruct((n_rows, ck), BF16), jax.ShapeDtypeStruct((n_rows, ck), BF16)]
    if has_sinks:
        in_specs.append(pl.BlockSpec(sink_rows.shape, lambda s: (0, 0)))
        args.append(sink_rows)
        out_specs.append(pl.BlockSpec(sink_rows.shape, lambda s: (0, 0)))
        out_shape.append(jax.ShapeDtypeStruct(sink_rows.shape, F32))
    outs = pl.pallas_call(
        body, name=name, grid=(n_steps,), in_specs=in_specs,
        out_specs=tuple(out_specs), out_shape=tuple(out_shape),
        scratch_shapes=[pltpu.VMEM((BLOCK, ck), F32), pltpu.VMEM((BLOCK, ck), F32),
                        pltpu.VMEM((BLOCK, ck), BF16), pltpu.VMEM((BLOCK, ck), BF16)],
        compiler_params=_params(("arbitrary",)),
    )(*args)
    return tuple(o.reshape(n_seq, length, o.shape[1]) for o in outs[:3]) + tuple(outs[3:])


def _tail(oa, ob1, lb1, ob4, lb4, ob16, lb16, gate_a, gate_b, x, target, w_out, spread, gather):
    tm = ROW_TILE

    def split_dot(v, mat):
        hi = v.astype(BF16)
        lo = (v - hi.astype(F32)).astype(BF16)
        return jnp.dot(hi, mat, preferred_element_type=F32) + jnp.dot(lo, mat, preferred_element_type=F32)

    def body(oa_ref, ob1_ref, lb1_ref, ob4_ref, lb4_ref, ob16_ref, lb16_ref, ga_ref, gb_ref, x_ref, t_ref, w_ref,
             sp_ref, ga_mat_ref,
             loss_ref, dy_ref, gwo_ref, doa_ref, dla_ref, dga_ref, dgb_ref,
             dob_ref, dob4_ref, dob16_ref, dlb_ref, dlb4_ref, dlb16_ref, lse_ref, lse4_ref, lse16_ref,
             s_f, mix_keep, dy_keep):
        i = pl.program_id(0)
        sp, gat = sp_ref[...], ga_mat_ref[...]
        o4, o16 = _unfold_load(ob4_ref, s_f, 4, tm), _unfold_load(ob16_ref, s_f, 16, tm)
        l4, l16 = _unfold_load(lb4_ref, s_f, 4, tm), _unfold_load(lb16_ref, s_f, 16, tm)
        o1, l1 = ob1_ref[...].astype(F32), lb1_ref[...]
        mx = jnp.maximum(jnp.maximum(l1, l4), l16)
        e1, e4, e16 = jnp.exp(l1 - mx), jnp.exp(l4 - mx), jnp.exp(l16 - mx)
        den = e1 + e4 + e16
        inv = 1.0 / den
        ob = split_dot(e1 * inv, sp) * o1 + split_dot(e4 * inv, sp) * o4 + split_dot(e16 * inv, sp) * o16
        lse_b = mx + jnp.log(den)

        oa, ga, gb = oa_ref[...].astype(F32), ga_ref[...].astype(F32), gb_ref[...].astype(F32)
        sa, sb = _sigmoid(ga), _sigmoid(gb)
        mixed = jnp.concatenate(_unpair_tiles(_tiles(oa * (ga * sa))) + [ob * (gb * sb)], axis=1)
        mixed_bf = mixed.astype(BF16)
        w = w_ref[...]
        yv = x_ref[...] + jnp.dot(mixed_bf, w, preferred_element_type=F32)
        err = yv - t_ref[...]
        sq = jnp.sum(err * err, axis=0, keepdims=True)
        dy = err * (1.0 / D_MODEL)
        dy_ref[...] = dy
        dy_bf = dy.astype(BF16)
        mix_t = mixed.T.astype(BF16)

        @pl.when(i == 0)
        def _():
            loss_ref[...] = sq

        @pl.when(i > 0)
        def _():
            loss_ref[...] += sq

        @pl.when((i & 1) == 0)
        def _():
            mix_keep[...] = mix_t
            dy_keep[...] = dy_bf

        @pl.when((i & 1) == 1)
        def _():
            gw = jnp.dot(jnp.concatenate([mix_keep[...], mix_t], axis=1), jnp.concatenate([dy_keep[...], dy_bf], axis=0),
                         preferred_element_type=F32)

            @pl.when(i == 1)
            def _():
                gwo_ref[...] = gw

            @pl.when(i > 1)
            def _():
                gwo_ref[...] += gw

        dmix = lax.dot_general(dy_bf, w, (((1,), (1,)), ((), ())), preferred_element_type=F32)
        dma = jnp.concatenate(_pair_tiles(_tiles(dmix[:, :HALF_WIDTH])), axis=1)
        dmb = dmix[:, HALF_WIDTH:]

        doa = dma * (ga * sa)
        doa_ref[...] = doa.astype(BF16)
        dla_ref[...] = split_dot(doa * oa, gat)
        dga_ref[...] = (dma * oa * (sa * (1.0 + ga * (1.0 - sa)))).astype(BF16)
        dob = dmb * (gb * sb)
        dgb_ref[...] = (dmb * ob * (sb * (1.0 + gb * (1.0 - sb)))).astype(BF16)
        dlb = split_dot(dob * ob, gat)
        dob_ref[...] = dob.astype(BF16)
        _fold_store(dob, s_f, dob4_ref, dob16_ref, tm)
        dlb_ref[...] = dlb
        _fold_store(dlb, s_f, dlb4_ref, dlb16_ref, tm)
        lse_ref[...] = lse_b
        _fold_store(lse_b, s_f, lse4_ref, lse16_ref, tm)

    row = lambda width: pl.BlockSpec((tm, width), lambda i: (i, 0))
    full = lambda a: pl.BlockSpec(a.shape, lambda i: (0,) * a.ndim)
    fb_shapes, fb_specs = _fold_specs(tm, BF16)
    _, ff_specs = _fold_specs(tm, F32)
    st_shapes, st_specs = _fold_specs(tm, F32, STAT_WIDTH)
    nat = lambda dtype, width=HALF_WIDTH: jax.ShapeDtypeStruct((SEQ, width), dtype)
    return pl.pallas_call(
        body, name="tail", grid=(SEQ // tm,),
        in_specs=[row(HALF_WIDTH), row(HALF_WIDTH), row(STAT_WIDTH), ff_specs[0], st_specs[0], ff_specs[1], st_specs[1],
                  row(HALF_WIDTH), row(HALF_WIDTH), row(D_MODEL), row(D_MODEL), full(w_out), full(spread), full(gather)],
        out_specs=(pl.BlockSpec((1, D_MODEL), lambda i: (0, 0)), row(D_MODEL),
                   pl.BlockSpec((D_MODEL, D_MODEL), lambda i: (0, 0)),
                   row(HALF_WIDTH), row(STAT_WIDTH), row(HALF_WIDTH), row(HALF_WIDTH),
                   row(HALF_WIDTH), *fb_specs, row(STAT_WIDTH), *st_specs, row(STAT_WIDTH), *st_specs),
        out_shape=(jax.ShapeDtypeStruct((1, D_MODEL), F32), jax.ShapeDtypeStruct((SEQ, D_MODEL), F32),
                   jax.ShapeDtypeStruct((D_MODEL, D_MODEL), F32),
                   nat(BF16), nat(F32, STAT_WIDTH), nat(BF16), nat(BF16),
                   nat(BF16), *fb_shapes, nat(F32, STAT_WIDTH), *st_shapes, nat(F32, STAT_WIDTH), *st_shapes),
        scratch_shapes=[_fold_scratch(tm), pltpu.VMEM((D_MODEL, tm), BF16), pltpu.VMEM((tm, D_MODEL), BF16)],
        compiler_params=_params(("arbitrary",)),
    )(oa, ob1, lb1, ob4, lb4, ob16, lb16, gate_a, gate_b, x, target, w_out, spread, gather)


def _dproj_assemble(dqa, dka, dva, dga, dgb, dq1, dk1, dv1, dq4, dk4, dv4, dq16, dk16, dv16, tqa, tqb, tkb, tka,
                    qkg, cos4, sin4, bmean):
    tm = ROW_TILE

    def norm_rope_bwd(d_out, t, g, cos, sin, bm, scale):
        d_r = d_out * scale
        dyv = d_r * cos + _swap_halves(d_r * sin)
        rr = lax.rsqrt(_head_sum(t * t, bm) + EPS)
        that = t * rr
        dgain = jnp.sum(dyv * that, axis=0, keepdims=True)
        gdy = dyv * g
        dt = rr * (gdy - that * _head_sum(that * gdy, bm))
        return dt, dgain

    def body(dqa_ref, dka_ref, dva_ref, dga_ref, dgb_ref, dq1_ref, dk1_ref, dv1_ref, dq4_ref, dk4_ref, dv4_ref,
             dq16_ref, dk16_ref, dv16_ref, tqa_ref, tqb_ref, tkb_ref, tka_ref, qkg_ref, cos_ref, sin_ref, bm_ref,
             dproj_ref, dqkg_ref, s_f):
        i = pl.program_id(0)
        cos, sin, bm = cos_ref[...], sin_ref[...], bm_ref[...]

        def merged(nat_ref, f4_ref, f16_ref):
            return nat_ref[...].astype(F32) + _unfold_load(f4_ref, s_f, 4, tm) + _unfold_load(f16_ref, s_f, 16, tm)

        @pl.when(i == 0)
        def _():
            dqkg_ref[...] = jnp.zeros_like(dqkg_ref)

        def through(d_out, t, row, scale, c0, paired=False):
            g = qkg_ref[row:row + 1, :]
            tot = jnp.zeros((1, PAIR), F32)
            dts = []
            for j in range(d_out.shape[1] // PAIR):
                cols = slice(j * PAIR, (j + 1) * PAIR)
                dt, dg = norm_rope_bwd(d_out[:, cols], t[:, cols], g, cos, sin, bm, scale)
                dts.append(dt)
                tot = tot + dg
            if paired:
                dts = _unpair_tiles(dts)
            for j, dt in enumerate(dts):
                dproj_ref[:, c0 + j * PAIR:c0 + (j + 1) * PAIR] = dt.astype(BF16)
            dqkg_ref[row:row + 1, :] += tot

        through(dqa_ref[...].astype(F32), tqa_ref[...].astype(F32), 0, HEAD_DIM ** -0.5, C_QA, paired=True)
        through(dka_ref[...].astype(F32), tka_ref[...].astype(F32), 1, 1.0, C_KA)
        through(merged(dq1_ref, dq4_ref, dq16_ref), tqb_ref[...].astype(F32), 2, HEAD_DIM ** -0.5, C_QB)
        through(merged(dk1_ref, dk4_ref, dk16_ref), tkb_ref[...].astype(F32), 3, 1.0, C_KB)
        dproj_ref[:, C_VB:C_VB + HALF_WIDTH] = merged(dv1_ref, dv4_ref, dv16_ref).astype(BF16)
        dproj_ref[:, C_GA:C_GA + HALF_WIDTH] = jnp.concatenate(
            _unpair_tiles(_tiles(dga_ref[...].astype(F32))), axis=1).astype(BF16)
        dproj_ref[:, C_GB:C_GB + HALF_WIDTH] = dgb_ref[...].astype(BF16)
        dproj_ref[:, C_VA:C_VA + KV_A_WIDTH] = dva_ref[...].astype(BF16)

    row = lambda width: pl.BlockSpec((tm, width), lambda i: (i, 0))
    full = lambda a: pl.BlockSpec(a.shape, lambda i: (0,) * a.ndim)
    _, ff_specs = _fold_specs(tm, F32)
    return pl.pallas_call(
        body, name="dproj_assemble", grid=(SEQ // tm,),
        in_specs=[row(HALF_WIDTH), row(KV_A_WIDTH), row(KV_A_WIDTH), row(HALF_WIDTH), row(HALF_WIDTH),
                  row(HALF_WIDTH), row(HALF_WIDTH), row(HALF_WIDTH), ff_specs[0], ff_specs[0], ff_specs[0],
                  ff_specs[1], ff_specs[1], ff_specs[1],
                  row(HALF_WIDTH), row(HALF_WIDTH), row(HALF_WIDTH), row(KV_A_WIDTH),
                  full(qkg), row(PAIR), row(PAIR), full(bmean)],
        out_specs=(row(IN_WIDTH), pl.BlockSpec((SMALL_ROWS, PAIR), lambda i: (0, 0))),
        out_shape=(jax.ShapeDtypeStruct((SEQ, IN_WIDTH), BF16), jax.ShapeDtypeStruct((SMALL_ROWS, PAIR), F32)),
        scratch_shapes=[_fold_scratch(tm)],
        compiler_params=_params(("arbitrary",)),
    )(dqa, dka, dva, dga, dgb, dq1, dk1, dv1, dq4, dk4, dv4, dq16, dk16, dv16, tqa, tqb, tkb, tka, qkg, cos4, sin4, bmean)


def _input_grad_reduce(dproj, w, x, gain, dy, blocks_in, blocks_out, small):
    tm = ROW_TILE
    n_steps = SEQ // tm
    stage2_step, stage3_step = 3, 8
    shapes = (blocks_in.shape[1:], blocks_out.shape[1:])

    def body(dp_ref, w_ref, x_ref, g_ref, dy_ref, ga_hbm, gb_hbm, small_ref,
             gx_ref, out_a, out_b, small_out_ref, dgain_out_ref,
             part_a, part_b, sib_a, sib_b, wire_a, wire_b, chips_a, chips_b, small_all, dgain_acc, dgain_all,
             load_sems, sib_send, sib_recv, chip_send, chip_recv, small_send, small_recv, dgain_send, dgain_recv):
        i = pl.program_id(0)
        x, y, c = lax.axis_index("x"), lax.axis_index("y"), lax.axis_index("c")
        sibling = (x, y, 1 - c)
        chips = [(x, y), (1 - x, y), (x, 1 - y), (1 - x, 1 - y)]
        my_id = 4 * x + 2 * y + c
        g_hbm, part, from_sib = (ga_hbm, gb_hbm), (part_a, part_b), (sib_a, sib_b)
        to_wire, from_chips, out = (wire_a, wire_b), (chips_a, chips_b), (out_a, out_b)
        both = (0, 1)

        def blk(a, chip, core):
            return g_hbm[a].at[4 * chip[0] + 2 * chip[1] + core]

        def to_all(src, dst_all, send, recv):
            copies = []
            for rel in range(1, N_DEV):
                dx, dy_, dc = (rel >> 2) & 1, (rel >> 1) & 1, rel & 1
                to = (1 - x if dx else x, 1 - y if dy_ else y, 1 - c if dc else c)
                copies.append(pltpu.make_async_remote_copy(
                    src_ref=src, dst_ref=dst_all.at[my_id], send_sem=send.at[rel - 1], recv_sem=recv.at[rel - 1],
                    device_id=to, device_id_type=MESH))
            return copies

        small_copies = to_all(small_all.at[my_id], small_all, small_send, small_recv)
        dgain_copies = to_all(dgain_acc, dgain_all, dgain_send, dgain_recv)
        loads = [[pltpu.make_async_copy(blk(a, chips[k], c), part[a].at[k], load_sems.at[a, k]) for k in range(4)] for a in both]
        to_sib = [[pltpu.make_async_remote_copy(
            src_ref=blk(a, chips[k], 1 - c), dst_ref=from_sib[a].at[k], send_sem=sib_send.at[a, k], recv_sem=sib_recv.at[a, k],
            device_id=sibling, device_id_type=MESH) for k in range(4)] for a in both]
        first, other, k_first, k_other = _routes()
        to_chips = [[pltpu.make_async_remote_copy(
            src_ref=to_wire[a].at[s], dst_ref=from_chips[a].at[s], send_sem=chip_send.at[a, s], recv_sem=chip_recv.at[a, s],
            device_id=(first, first, other)[s], device_id_type=MESH) for s in range(3)] for a in both]

        def chip_partial(a, k):
            return part[a][k].astype(F32) + from_sib[a][k].astype(F32)

        @pl.when(i == 0)
        def _():
            small_all[my_id] = small_ref[...]
            for cp in small_copies:
                cp.start()
            for k in (1, 2, 3, 0):
                for a in both:
                    loads[a][k].start()
                    to_sib[a][k].start()

        @pl.when(i == stage2_step)
        def _():
            for k in (1, 2, 3):
                for a in both:
                    loads[a][k].wait()
                    to_sib[a][k].wait_recv()
            for s, k in ((0, 3), (1, k_first)):
                for a in both:
                    to_wire[a][s] = chip_partial(a, k).astype(BF16)
                    to_chips[a][s].start()

        @pl.when(i == stage3_step)
        def _():
            for a in both:
                to_chips[a][0].wait_recv()
                to_wire[a][2] = (chip_partial(a, k_other) + from_chips[a][0].astype(F32)).astype(BF16)
                to_chips[a][2].start()

        dh = lax.dot_general(dp_ref[...], w_ref[...], (((1,), (1,)), ((), ())), preferred_element_type=F32)
        xf = x_ref[...]
        r = lax.rsqrt(jnp.mean(xf * xf, axis=-1, keepdims=True) + EPS)
        xhat = xf * r
        dg = jnp.sum(dh * xhat, axis=0, keepdims=True)
        dxh = dh * g_ref[...]
        dx = r * (dxh - xhat * jnp.mean(dxh * xhat, axis=-1, keepdims=True))
        gx_ref[...] = dy_ref[...] + dx

        @pl.when(i == 0)
        def _():
            dgain_acc[...] = dg

        @pl.when(i > 0)
        def _():
            dgain_acc[...] += dg

        @pl.when(i == n_steps - 1)
        def _():
            dgain_all[my_id] = dgain_acc[...]
            for cp in dgain_copies:
                cp.start()
            for a in both:
                loads[a][0].wait()
                to_sib[a][0].wait_recv()
                acc = chip_partial(a, 0)
                for s in (1, 2):
                    to_chips[a][s].wait_recv()
                    acc = acc + from_chips[a][s].astype(F32)
                out[a][...] = acc
            for copies, gathered, dst in ((small_copies, small_all, small_out_ref), (dgain_copies, dgain_all, dgain_out_ref)):
                for cp in copies:
                    cp.wait_recv()
                tot = gathered[0]
                for d in range(1, N_DEV):
                    tot = tot + gathered[d]
                dst[...] = tot
            for cp in to_sib[0] + to_sib[1] + to_chips[0] + to_chips[1] + small_copies + dgain_copies:
                cp.wait_send()

    row = lambda width: pl.BlockSpec((tm, width), lambda i: (i, 0))
    full = lambda a: pl.BlockSpec(a.shape, lambda i: (0,) * a.ndim)
    whole = lambda shape: pl.BlockSpec(shape, lambda i: (0,) * len(shape))
    hbm = pl.BlockSpec(memory_space=pl.ANY)
    dtypes = (blocks_in.dtype, blocks_out.dtype)
    buf = lambda n, dts: [pltpu.VMEM((n,) + s, dt) for s, dt in zip(shapes, dts)]
    return pl.pallas_call(
        body, name="input_grad_rs", grid=(n_steps,),
        in_specs=[row(IN_WIDTH), full(w), row(D_MODEL), full(gain), row(D_MODEL), hbm, hbm, full(small)],
        out_specs=(row(D_MODEL), whole(shapes[0]), whole(shapes[1]), whole((SMALL_ROWS, SMALL_COLS)), whole((1, D_MODEL))),
        out_shape=(jax.ShapeDtypeStruct((SEQ, D_MODEL), F32), jax.ShapeDtypeStruct(shapes[0], F32),
                   jax.ShapeDtypeStruct(shapes[1], F32), jax.ShapeDtypeStruct((SMALL_ROWS, SMALL_COLS), F32),
                   jax.ShapeDtypeStruct((1, D_MODEL), F32)),
        scratch_shapes=[*buf(4, dtypes), *buf(4, dtypes), *buf(3, (BF16, BF16)), *buf(3, (BF16, BF16)),
                        pltpu.VMEM((N_DEV, SMALL_ROWS, SMALL_COLS), F32),
                        pltpu.VMEM((1, D_MODEL), F32), pltpu.VMEM((N_DEV, 1, D_MODEL), F32),
                        pltpu.SemaphoreType.DMA((2, 4)), pltpu.SemaphoreType.DMA((2, 4)), pltpu.SemaphoreType.DMA((2, 4)),
                        pltpu.SemaphoreType.DMA((2, 3)), pltpu.SemaphoreType.DMA((2, 3)),
                        pltpu.SemaphoreType.DMA((7,)), pltpu.SemaphoreType.DMA((7,)),
                        pltpu.SemaphoreType.DMA((7,)), pltpu.SemaphoreType.DMA((7,))],
        compiler_params=_params(("arbitrary",)),
    )(dproj, w, x, gain, dy, blocks_in, blocks_out, small)


def _weight_grad(h_t, dproj):
    tk = 1024
    cb = IN_WIDTH // 2
    n_k = SEQ // tk

    def body(ht_ref, dp_ref, out_ref, acc):
        k = pl.program_id(1)
        upd = jnp.dot(ht_ref[...], dp_ref[...], preferred_element_type=F32)

        @pl.when(k == 0)
        def _():
            acc[...] = upd

        @pl.when(k > 0)
        def _():
            acc[...] += upd

        @pl.when(k == n_k - 1)
        def _():
            for b in range(N_DEV // 2):
                out_ref[b] = acc[:, b * SHARD_IN:(b + 1) * SHARD_IN].astype(BF16)

    return pl.pallas_call(
        body, name="weight_grad", grid=(2, n_k),
        in_specs=[pl.BlockSpec((D_MODEL, tk), lambda j, k: (0, k)), pl.BlockSpec((tk, cb), lambda j, k: (k, j))],
        out_specs=pl.BlockSpec((N_DEV // 2, D_MODEL, SHARD_IN), lambda j, k: (j, 0, 0)),
        out_shape=jax.ShapeDtypeStruct((N_DEV, D_MODEL, SHARD_IN), BF16),
        scratch_shapes=[pltpu.VMEM((D_MODEL, cb), F32)],
        compiler_params=_params(("arbitrary", "arbitrary")),
    )(h_t, dproj)


def _adamw(name, w, g, m, v):
    def body(w_ref, g_ref, m_ref, v_ref, d_ref, nm_ref, nv_ref):
        gv = g_ref[...]
        nm = ADAM_B1 * m_ref[...] + (1.0 - ADAM_B1) * gv
        nv = ADAM_B2 * v_ref[...] + (1.0 - ADAM_B2) * jnp.square(gv)
        m_hat = nm / (1.0 - ADAM_B1 ** ADAM_STEP)
        v_hat = nv / (1.0 - ADAM_B2 ** ADAM_STEP)
        d_ref[...] = -ADAM_LR * (m_hat / (jnp.sqrt(v_hat) + ADAM_EPS) + ADAM_WD * w_ref[...])
        nm_ref[...] = nm
        nv_ref[...] = nv

    vmem = pl.BlockSpec(memory_space=pltpu.VMEM)
    out = jax.ShapeDtypeStruct(w.shape, F32)
    return pl.pallas_call(
        body, name=name, in_specs=[vmem] * 4, out_specs=(vmem,) * 3, out_shape=(out,) * 3,
        compiler_params=pltpu.CompilerParams(vmem_limit_bytes=VMEM_LIMIT),
    )(w, g, m, v)


SMALL_USED = D_MODEL + 4 * HEAD_DIM + 8


def _pack_small(norm_gain, qa, ka, sinks, qb, kb, extra=None):
    parts = [norm_gain.reshape(-1), qa.reshape(-1), ka.reshape(-1), sinks.reshape(-1), qb.reshape(-1), kb.reshape(-1)]
    if extra is not None:
        parts.append(extra.reshape(-1))
    flat = jnp.concatenate(parts)
    flat = jnp.pad(flat, (0, SMALL_ROWS * SMALL_COLS - flat.shape[0]))
    return flat.reshape(SMALL_ROWS, SMALL_COLS)


def _unpack_small(a):
    flat = a.reshape(-1)
    sizes = (D_MODEL, HEAD_DIM, HEAD_DIM, 8, HEAD_DIM, HEAD_DIM)
    out, off = [], 0
    for s in sizes:
        out.append(flat[off:off + s].reshape(1, s))
        off += s
    return out


def _fold_heads(row):
    return row[0, :HEAD_DIM] + row[0, HEAD_DIM:]


def kernel(x, norm_gain, w_in, q_norm_a, k_norm_a, sinks_a, q_norm_b, k_norm_b, w_out, loss_target, m_norm_gain, m_w_in, m_q_norm_a, m_k_norm_a, m_sinks_a, m_q_norm_b, m_k_norm_b, m_w_out, v_norm_gain, v_w_in, v_q_norm_a, v_k_norm_a, v_sinks_a, v_q_norm_b, v_k_norm_b, v_w_out):
    x2, tgt = x[0], loss_target[0]
    w_in_sh, w_out_sh = w_in[0], w_out[0]

    w_full = _all_gather_w_in(w_in_sh)

    inv = np.float32(ROPE_THETA) ** (-np.arange(HEAD_DIM // 2, dtype=np.float32) / np.float32(HEAD_DIM // 2))
    ang = np.arange(SEQ, dtype=np.float32)[:, None] * inv[None, :].astype(np.float32)
    cos, sin = np.cos(ang).astype(np.float32), np.sin(ang).astype(np.float32)
    cos4 = jnp.asarray(np.concatenate([cos, cos, cos, cos], axis=1))
    sin4 = jnp.asarray(np.concatenate([-sin, sin, -sin, sin], axis=1))
    blockdiag = np.kron(np.eye(2, dtype=np.float32), np.ones((HEAD_DIM, HEAD_DIM), np.float32))
    bmean = jnp.asarray(blockdiag / HEAD_DIM, dtype=BF16)
    gather_np = np.kron(np.eye(2 * N_PAIRS, dtype=np.float32), np.ones((HEAD_DIM, STAT_REP), np.float32))
    spread_np = np.kron(np.eye(2 * N_PAIRS, dtype=np.float32), np.ones((STAT_REP, HEAD_DIM), np.float32))
    spread_np[np.arange(STAT_WIDTH) % STAT_REP != 0] = 0.0
    gather, spread = jnp.asarray(gather_np, dtype=BF16), jnp.asarray(spread_np, dtype=BF16)
    two = lambda g: jnp.concatenate([g, g], axis=1)
    qkg = jnp.concatenate([two(q_norm_a), two(k_norm_a), two(q_norm_b), two(k_norm_b),
                           jnp.zeros((SMALL_ROWS - 4, PAIR), F32)], axis=0)
    sinks_paired = jnp.stack([sinks_a[0, :N_PAIRS], sinks_a[0, N_PAIRS:]], axis=1)
    sink_rows = jnp.concatenate([jnp.repeat(sinks_paired, BLOCK, axis=1),
                                 jnp.zeros((SMALL_ROWS - N_PAIRS, 2 * BLOCK), F32)], axis=0)

    (tqa, tka, tqb, tkb, gate_a, gate_b, h_t, qa, ka, va, qb, kb, vb, qb4, qb16, kb4, kb16, vb4, vb16,
     gathered_out) = _proj_fwd(x2, norm_gain, w_full, qkg, cos4, sin4, bmean, w_out_sh)
    wo_full = gathered_out.reshape(D_MODEL, D_MODEL)
    (oa, la), (ob1, lb1), (ob4, lb4), (ob16, lb16) = _attn_fwd("attn_fwd", [
        (qa[None], ka[None], va[None], sink_rows, BLOCK - 1), (qb[None], kb[None], vb[None], None, BLOCK),
        (qb4, kb4, vb4, None, BLOCK), (qb16, kb16, vb16, None, BLOCK)])
    (loss_cols, dy, gwo, doa, dla, dga, dgb, dob, dob4, dob16, dlb, dlb4, dlb16, lse_b, lse4, lse16) = _tail(
        oa[0], ob1[0], lb1[0], ob4, lb4, ob16, lb16, gate_a, gate_b, x2, tgt, wo_full, spread, gather)

    dqa, dka, dva, dsink = _attn_bwd("attn_a_bwd", qa[None], ka[None], va[None], doa[None], la, dla[None], sink_rows, BLOCK - 1)
    dq1, dk1, dv1 = _attn_bwd("attn_b1_bwd", qb[None], kb[None], vb[None], dob[None], lse_b[None], dlb[None], None, BLOCK)
    dq4, dk4, dv4 = _attn_bwd("attn_b4_bwd", qb4, kb4, vb4, dob4, lse4, dlb4, None, BLOCK)
    dq16, dk16, dv16 = _attn_bwd("attn_b16_bwd", qb16, kb16, vb16, dob16, lse16, dlb16, None, BLOCK)
    dproj, dqkg = _dproj_assemble(dqa[0], dka[0], dva[0], dga, dgb, dq1[0], dk1[0], dv1[0], dq4, dk4, dv4,
                                  dq16, dk16, dv16, tqa, tqb, tkb, tka, qkg, cos4, sin4, bmean)
    gw_in = _weight_grad(h_t, dproj)

    blocks_in = gw_in
    blocks_out = gwo.reshape(N_DEV, SHARD_OUT, D_MODEL)
    g_sinks = jnp.concatenate([jnp.sum(dsink[:N_PAIRS, :BLOCK], axis=1), jnp.sum(dsink[:N_PAIRS, BLOCK:], axis=1)])
    small = _pack_small(jnp.zeros((D_MODEL,), F32), _fold_heads(dqkg[0:1]), _fold_heads(dqkg[1:2]), g_sinks,
                        _fold_heads(dqkg[2:3]), _fold_heads(dqkg[3:4]), extra=0.5 * jnp.sum(loss_cols) / D_MODEL)
    grad_x, g_w_in, g_w_out, small_red, dgain_red = _input_grad_reduce(
        dproj, w_full, x2, norm_gain, dy, blocks_in, blocks_out, small)
    n_gain_rows = D_MODEL // SMALL_COLS
    small_red = jnp.concatenate([dgain_red.reshape(n_gain_rows, SMALL_COLS), small_red[n_gain_rows:]], axis=0)
    g_small = _unpack_small(small_red)

    d_in, nm_in, nv_in = _adamw("adamw_w_in", w_in_sh, g_w_in, m_w_in[0], v_w_in[0])
    d_out, nm_out, nv_out = _adamw("adamw_w_out", w_out_sh, g_w_out, m_w_out[0], v_w_out[0])
    d_s, nm_s, nv_s = _adamw(
        "adamw_small",
        _pack_small(norm_gain, q_norm_a, k_norm_a, sinks_a, q_norm_b, k_norm_b), small_red,
        _pack_small(m_norm_gain, m_q_norm_a, m_k_norm_a, m_sinks_a, m_q_norm_b, m_k_norm_b),
        _pack_small(v_norm_gain, v_q_norm_a, v_k_norm_a, v_sinks_a, v_q_norm_b, v_k_norm_b))
    d_small, nm_small, nv_small = _unpack_small(d_s), _unpack_small(nm_s), _unpack_small(nv_s)

    loss = small_red.reshape(-1)[SMALL_USED]

    def assemble(small_list, big_in, big_out):
        ng, qa_, ka_, sk_, qb_, kb_ = small_list
        return [ng, big_in[None], qa_, ka_, sk_, qb_, kb_, big_out[None]]

    return (loss, grad_x[None], *assemble(g_small, g_w_in, g_w_out), *assemble(d_small, d_in, d_out),
            *assemble(nm_small, nm_in, nm_out), *assemble(nv_small, nv_in, nv_out))
```

```python
import functools

import numpy as np
import jax
import jax.numpy as jnp
from jax import lax
from jax.experimental import pallas as pl
from jax.experimental.pallas import tpu as pltpu

F32 = jnp.float32
BF16 = jnp.bfloat16

SEQ = 4096
D_MODEL = 1024
HEAD_DIM = 64
PAIR = 2 * HEAD_DIM
N_PAIRS = 4
HALF_WIDTH = N_PAIRS * PAIR
KV_A_WIDTH = 128
IN_WIDTH = 3328
BLOCK = 128
STAT_REP = 16
STAT_WIDTH = 128
EPS = 1e-6
NEG = -1e30
ROPE_THETA = 10000.0
LOG2E = 1.4426950408889634
LN2 = 0.6931471805599453
Q_SCALE = HEAD_DIM ** -0.5 * LOG2E
N_DEV = 8
SHARD_IN = IN_WIDTH // N_DEV
SHARD_OUT = D_MODEL // N_DEV
PAYLOAD = SHARD_IN + SHARD_OUT
SMALL_ROWS, SMALL_COLS = 8, 256

C_QA, C_KA, C_VA, C_GA, C_QB, C_KB, C_VB, C_GB = 0, 512, 640, 768, 1280, 1792, 2304, 2816

ADAM_LR = 0.001
ADAM_B1 = 0.9
ADAM_B2 = 0.999
ADAM_EPS = 1e-08
ADAM_WD = 0.01
ADAM_STEP = 10

ROW_TILE = 256
PROJ_ROW_TILE = 512
FWD_BLOCKS_PER_STEP = 8
BWD_BLOCKS_PER_STEP = 8
VMEM_LIMIT = 56 * 1024 * 1024

MESH = pl.DeviceIdType.MESH


def _params(sem, vmem=VMEM_LIMIT):
    return pltpu.CompilerParams(dimension_semantics=sem, vmem_limit_bytes=vmem)


def _head_sum(v, bm):
    hi = v.astype(BF16)
    lo = (v - hi.astype(F32)).astype(BF16)
    return (jnp.dot(hi, bm, preferred_element_type=F32) + jnp.dot(lo, bm, preferred_element_type=F32))


def _swap_halves(y):
    lane = lax.broadcasted_iota(jnp.int32, y.shape, 1)
    first = (lane & 32) == 0
    return jnp.where(first, pltpu.roll(y, 96, 1), pltpu.roll(y, 32, 1))


def _sigmoid(g):
    return 1.0 / (1.0 + jnp.exp(-g))


def _tiles(a):
    return [a[:, j * PAIR:(j + 1) * PAIR] for j in range(N_PAIRS)]


def _pair_tiles(t):
    low = lax.broadcasted_iota(jnp.int32, t[0].shape, 1) < HEAD_DIM
    r = [pltpu.roll(a, HEAD_DIM, 1) for a in t]
    return [jnp.where(low, t[0], r[2]), jnp.where(low, r[0], t[2]), jnp.where(low, t[1], r[3]), jnp.where(low, r[1], t[3])]


def _unpair_tiles(p):
    low = lax.broadcasted_iota(jnp.int32, p[0].shape, 1) < HEAD_DIM
    r = [pltpu.roll(a, HEAD_DIM, 1) for a in p]
    return [jnp.where(low, p[0], r[1]), jnp.where(low, p[2], r[3]), jnp.where(low, r[0], p[1]), jnp.where(low, r[2], p[3])]


def _routes():
    x, y, c = lax.axis_index("x"), lax.axis_index("y"), lax.axis_index("c")
    north = c == 1
    first = (jnp.where(north, 1 - x, x), jnp.where(north, y, 1 - y), c)
    other = (jnp.where(north, x, 1 - x), jnp.where(north, 1 - y, y), c)
    k_first = jnp.where(north, 1, 2)
    return first, other, k_first, 3 - k_first


def _gather_plan(mine_ref, out_ref, send_sems, recv_sems):
    x, y, c = lax.axis_index("x"), lax.axis_index("y"), lax.axis_index("c")
    me, sibling, diag = (x, y, c), (x, y, 1 - c), (1 - x, 1 - y, c)
    first, other, k_first, k_other = _routes()

    def slot(px, py, pc):
        return out_ref.at[4 * px + 2 * py + pc]

    def copy(k, block, to, from_mine=False):
        return pltpu.make_async_remote_copy(
            src_ref=mine_ref if from_mine else slot(*block), dst_ref=slot(*block),
            send_sem=send_sems.at[k], recv_sem=recv_sems.at[k], device_id=to, device_id_type=MESH)

    sends = [copy(0, me, sibling, True), copy(1, me, (1 - x, y, c), True), copy(2, me, (x, 1 - y, c), True)]
    stages = [(copy(k_first, first, me), [copy(3, first, other), copy(3 + k_first, first, sibling)]),
              (copy(k_other, other, me), [copy(3 + k_other, other, sibling)]),
              (copy(3, diag, me), [copy(6, diag, sibling)])]
    from_sibling = [copy(0, sibling, me), copy(4, (1 - x, y, 1 - c), me), copy(5, (x, 1 - y, 1 - c), me),
                    copy(6, (1 - x, 1 - y, 1 - c), me)]
    return slot(*me), sends, stages, from_sibling


GATHER_SCRATCH = [pltpu.SemaphoreType.DMA((7,)), pltpu.SemaphoreType.DMA((7,))]


def _all_gather_w_in(w_in_sh):
    rows, cols = w_in_sh.shape

    def body(w_ref, out_ref, mine_ref, blocks, send_sems, recv_sems):
        mine_ref[...] = w_ref[...].astype(BF16)
        my_slot, sends, stages, from_sibling = _gather_plan(mine_ref, blocks, send_sems, recv_sems)
        for cp in sends:
            cp.start()
        my_slot[...] = mine_ref[...]
        for arrival, forwards in stages:
            arrival.wait_recv()
            for cp in forwards:
                cp.start()
        for arrival in from_sibling:
            arrival.wait_recv()
        for cp in sends + [cp for _, forwards in stages for cp in forwards]:
            cp.wait_send()
        for d in range(N_DEV):
            out_ref[:, d * cols:(d + 1) * cols] = blocks[d]

    vmem = pl.BlockSpec(memory_space=pltpu.VMEM)
    return pl.pallas_call(
        body, name="ag_w_in",
        out_shape=jax.ShapeDtypeStruct((rows, N_DEV * cols), BF16),
        in_specs=[vmem], out_specs=vmem,
        scratch_shapes=[pltpu.VMEM((rows, cols), BF16), pltpu.VMEM((N_DEV, rows, cols), BF16)] + GATHER_SCRATCH,
        compiler_params=pltpu.CompilerParams(vmem_limit_bytes=VMEM_LIMIT),
    )(w_in_sh)


def _fold_scratch(tm):
    return pltpu.VMEM((N_PAIRS, tm, PAIR), F32)


def _fold_store(val, scr, out4, out16, tm):
    groups = range(val.shape[1] // PAIR)
    for j in groups:
        scr[j] = val[:, j * PAIR:(j + 1) * PAIR]
    for dil, out in ((4, out4), (16, out16)):
        for r in range(dil):
            for j in groups:
                out[r, :, j * PAIR:(j + 1) * PAIR] = scr[j, pl.ds(r, tm // dil, stride=dil), :].astype(out.dtype)


def _unfold_load(src, scr, dil, tm):
    groups = range(src.shape[2] // PAIR)
    for r in range(dil):
        for j in groups:
            scr[j, pl.ds(r, tm // dil, stride=dil), :] = src[r, :, j * PAIR:(j + 1) * PAIR].astype(F32)
    return jnp.concatenate([scr[j] for j in groups], axis=1)


def _fold_specs(tm, dtype, width=HALF_WIDTH):
    shapes = (jax.ShapeDtypeStruct((4, SEQ // 4, width), dtype), jax.ShapeDtypeStruct((16, SEQ // 16, width), dtype))
    specs = (pl.BlockSpec((4, tm // 4, width), lambda i: (0, i, 0)),
             pl.BlockSpec((16, tm // 16, width), lambda i: (0, i, 0)))
    return shapes, specs


def _proj_fwd(x, gain, w, qkg, cos4, sin4, bmean, w_out_sh):
    tm = PROJ_ROW_TILE
    n_steps = SEQ // tm

    def norm_rope(t, g, cos, sin, bm, scale):
        rr = lax.rsqrt(_head_sum(t * t, bm) + EPS)
        yv = t * rr * g
        return (yv * cos + _swap_halves(yv) * sin) * scale

    def body(x_ref, g_ref, w_ref, qkg_ref, cos_ref, sin_ref, bm_ref, wo_ref,
             tqa_ref, tka_ref, tqb_ref, tkb_ref, ga_ref, gb_ref, ht_ref, qa_ref, ka_ref, va_ref, qb_ref, kb_ref, vb_ref,
             qb4_ref, qb16_ref, kb4_ref, kb16_ref, vb4_ref, vb16_ref, wo_all_ref,
             proj, scr, wo_mine, wo_all, send_sems, recv_sems):
        i = pl.program_id(0)
        my_slot, sends, stages, from_sibling = _gather_plan(wo_mine, wo_all, send_sems, recv_sems)

        @pl.when(i == 0)
        def _():
            wo_mine[...] = wo_ref[...].astype(BF16)
            for cp in sends:
                cp.start()
            my_slot[...] = wo_mine[...]

        @pl.when(i == n_steps // 2)
        def _():
            for arrival, forwards in stages[:2]:
                arrival.wait_recv()
                for cp in forwards:
                    cp.start()

        xf = x_ref[...]
        r = lax.rsqrt(jnp.mean(xf * xf, axis=-1, keepdims=True) + EPS)
        hf = xf * r * g_ref[...]
        ht_ref[...] = hf.T.astype(BF16)
        cos, sin, bm = cos_ref[...], sin_ref[...], bm_ref[...]
        proj[...] = jnp.dot(hf.astype(BF16), w_ref[...], preferred_element_type=F32)

        def roped(tiles, row, scale):
            g = qkg_ref[row:row + 1, :]
            return jnp.concatenate([norm_rope(t, g, cos, sin, bm, scale) for t in tiles], axis=1)

        tqa = _pair_tiles(_tiles(proj[:, C_QA:C_QA + HALF_WIDTH]))
        tqa_ref[...] = jnp.concatenate(tqa, axis=1).astype(BF16)
        qa_ref[...] = roped(tqa, 0, Q_SCALE).astype(BF16)
        ga_ref[...] = jnp.concatenate(_pair_tiles(_tiles(proj[:, C_GA:C_GA + HALF_WIDTH])), axis=1).astype(BF16)
        gb_ref[...] = proj[:, C_GB:C_GB + HALF_WIDTH].astype(BF16)
        tqb = proj[:, C_QB:C_QB + HALF_WIDTH]
        tqb_ref[...] = tqb.astype(BF16)
        qb = roped(_tiles(tqb), 2, Q_SCALE)
        qb_ref[...] = qb.astype(BF16)
        _fold_store(qb, scr, qb4_ref, qb16_ref, tm)
        tkb = proj[:, C_KB:C_KB + HALF_WIDTH]
        tkb_ref[...] = tkb.astype(BF16)
        kb = roped(_tiles(tkb), 3, 1.0)
        kb_ref[...] = kb.astype(BF16)
        _fold_store(kb, scr, kb4_ref, kb16_ref, tm)
        vb = proj[:, C_VB:C_VB + HALF_WIDTH]
        vb_ref[...] = vb.astype(BF16)
        _fold_store(vb, scr, vb4_ref, vb16_ref, tm)
        tka = proj[:, C_KA:C_KA + KV_A_WIDTH]
        tka_ref[...] = tka.astype(BF16)
        ka_ref[...] = roped([tka], 1, 1.0).astype(BF16)
        va_ref[...] = proj[:, C_VA:C_VA + KV_A_WIDTH].astype(BF16)

        @pl.when(i == n_steps - 1)
        def _():
            arrival, forwards = stages[2]
            arrival.wait_recv()
            for cp in forwards:
                cp.start()
            for arrival in from_sibling:
                arrival.wait_recv()
            for cp in sends + [cp for _, forwards in stages for cp in forwards]:
                cp.wait_send()
            wo_all_ref[...] = wo_all[...]

    row = lambda width: pl.BlockSpec((tm, width), lambda i: (i, 0))
    full = lambda a: pl.BlockSpec(a.shape, lambda i: (0,) * a.ndim)
    nat = lambda width, dtype=BF16: jax.ShapeDtypeStruct((SEQ, width), dtype)
    f_shapes, f_specs = _fold_specs(tm, BF16)
    return pl.pallas_call(
        body, name="proj_fwd", grid=(SEQ // tm,),
        in_specs=[row(D_MODEL), full(gain), full(w), full(qkg), row(PAIR), row(PAIR), full(bmean), full(w_out_sh)],
        out_specs=(row(HALF_WIDTH), row(KV_A_WIDTH), row(HALF_WIDTH), row(HALF_WIDTH), row(HALF_WIDTH), row(HALF_WIDTH),
                   pl.BlockSpec((D_MODEL, tm), lambda i: (0, i)),
                   row(HALF_WIDTH), row(KV_A_WIDTH), row(KV_A_WIDTH), row(HALF_WIDTH), row(HALF_WIDTH), row(HALF_WIDTH),
                   *f_specs, *f_specs, *f_specs,
                   pl.BlockSpec((N_DEV,) + w_out_sh.shape, lambda i: (0, 0, 0))),
        out_shape=(nat(HALF_WIDTH), nat(KV_A_WIDTH), nat(HALF_WIDTH), nat(HALF_WIDTH), nat(HALF_WIDTH), nat(HALF_WIDTH),
                   jax.ShapeDtypeStruct((D_MODEL, SEQ), BF16),
                   nat(HALF_WIDTH), nat(KV_A_WIDTH), nat(KV_A_WIDTH), nat(HALF_WIDTH), nat(HALF_WIDTH), nat(HALF_WIDTH),
                   *f_shapes, *f_shapes, *f_shapes,
                   jax.ShapeDtypeStruct((N_DEV,) + w_out_sh.shape, BF16)),
        scratch_shapes=[pltpu.VMEM((tm, IN_WIDTH), F32), _fold_scratch(tm), pltpu.VMEM(w_out_sh.shape, BF16),
                        pltpu.VMEM((N_DEV,) + w_out_sh.shape, BF16)] + GATHER_SCRATCH,
        compiler_params=_params(("arbitrary",)),
    )(x, gain, w, qkg, cos4, sin4, bmean, w_out_sh)


def _band_mask(i, max_dist):
    j = lax.broadcasted_iota(jnp.int32, (2 * BLOCK, 2 * BLOCK), 0)
    c = lax.broadcasted_iota(jnp.int32, (2 * BLOCK, 2 * BLOCK), 1)
    dist = (c & (BLOCK - 1)) + BLOCK - j
    return (dist >= 0) & (dist <= max_dist) & ((j >= BLOCK) | (i > 0))


def _stack_heads(t):
    lane = lax.broadcasted_iota(jnp.int32, t.shape, 1)
    low = lane < HEAD_DIM
    zero = jnp.zeros_like(t)
    return jnp.concatenate([jnp.where(low, t, zero), jnp.where(low, zero, t)], axis=0)


def _stack_heads_t(t):
    tt = t.astype(F32).T
    low = lax.broadcasted_iota(jnp.int32, tt.shape, 0) < HEAD_DIM
    zero = jnp.zeros_like(tt)
    return jnp.concatenate([jnp.where(low, tt, zero), jnp.where(low, zero, tt)], axis=1).astype(BF16)


def _unstack_t(t):
    return jnp.concatenate([t[:HEAD_DIM, :BLOCK], t[HEAD_DIM:, BLOCK:]], axis=0).T


def _rows_to_stats(rows):
    parts = []
    for row in rows:
        parts.append(jnp.broadcast_to(row[:, :BLOCK], (STAT_REP, BLOCK)))
        parts.append(jnp.broadcast_to(row[:, BLOCK:], (STAT_REP, BLOCK)))
    return jnp.concatenate(parts, axis=0).T


def _stats_to_rows(t):
    tt = t.T
    return [jnp.concatenate([tt[2 * p * STAT_REP:2 * p * STAT_REP + 1, :],
                             tt[(2 * p + 1) * STAT_REP:(2 * p + 1) * STAT_REP + 1, :]], axis=1) for p in range(N_PAIRS)]


def _attn_fwd(name, patterns):
    qb = FWD_BLOCKS_PER_STEP
    steps = SEQ // (qb * BLOCK)

    def one_pattern(step, nb, shared, max_dist, q_ref, kc_ref, vc_ref, sink_ref, o_ref, lse_ref, kp_ref, vp_ref):
        has_sinks = sink_ref is not None

        @pl.when(step == 0)
        def _():
            kp_ref[...] = jnp.zeros_like(kp_ref)
            vp_ref[...] = jnp.zeros_like(vp_ref)

        valid = [_band_mask((step * qb + b) & (nb - 1), max_dist) for b in range(qb)]
        cols = [slice(p * PAIR, (p + 1) * PAIR) for p in range(N_PAIRS)]
        kcols = [slice(0, PAIR) if shared else c for c in cols]
        rows = [slice(b * BLOCK, (b + 1) * BLOCK) for b in range(qb)]
        units = [(b, p) for b in range(qb) for p in range(N_PAIRS)]
        n = range(len(units))

        def window(prev_ref, cur_ref, b, kc):
            before = prev_ref[:, kc] if b == 0 else cur_ref[rows[b - 1], kc]
            return jnp.concatenate([before, cur_ref[rows[b], kc]], axis=0)

        st = [jnp.dot(window(kp_ref, kc_ref, b, kcols[p]), _stack_heads_t(q_ref[rows[b], cols[p]]),
                      preferred_element_type=F32) for b, p in units]
        st = [jnp.where(valid[units[u][0]], st[u], NEG) for u in n]
        m = [jnp.max(s, axis=0, keepdims=True) for s in st]
        if has_sinks:
            sk = [sink_ref[p:p + 1, :] for _, p in units]
            m = [jnp.maximum(m[u], sk[u]) for u in n]
        pt = [jnp.exp2(st[u] - m[u]) for u in n]
        l = [jnp.sum(t, axis=0, keepdims=True) for t in pt]
        if has_sinks:
            l = [l[u] + jnp.exp2(sk[u] - m[u]) for u in n]
        v2t = [window(vp_ref, vc_ref, b, kcols[p]).astype(F32).T.astype(BF16) for b, p in units]
        ot = [jnp.dot(v2t[u], pt[u].astype(BF16), preferred_element_type=F32) / l[u] for u in n]
        for u, (b, p) in enumerate(units):
            o_ref[rows[b], cols[p]] = _unstack_t(ot[u]).astype(BF16)
        for b in range(qb):
            lse_ref[rows[b], :] = _rows_to_stats([m[u] + jnp.log2(l[u]) for u in n if units[u][0] == b])
        kp_ref[...] = kc_ref[rows[-1], :]
        vp_ref[...] = vc_ref[rows[-1], :]

    n_in = [4 if sinks is not None else 3 for _, _, _, sinks, _ in patterns]

    def body(*refs):
        ins, rest = refs[:sum(n_in)], refs[sum(n_in):]
        outs, scratch = rest[:2 * len(patterns)], rest[2 * len(patterns):]
        step = pl.program_id(0)
        first = 0
        for p, (q, k, _, sinks, max_dist) in enumerate(patterns):
            mine = ins[first:first + n_in[p]]
            first += n_in[p]
            sink_ref = mine[3] if sinks is not None else None
            kp_ref, vp_ref = scratch[2 * p], scratch[2 * p + 1]

            @pl.when((step >= p * steps) & (step < (p + 1) * steps))
            def _():
                one_pattern(step - p * steps, q.shape[1] // BLOCK, k.shape[2] == PAIR, max_dist,
                            mine[0], mine[1], mine[2], sink_ref, outs[2 * p], outs[2 * p + 1], kp_ref, vp_ref)

    def during(p, width):
        return pl.BlockSpec((qb * BLOCK, width), lambda s: (jnp.clip(s - p * steps, 0, steps - 1), 0))

    flat = lambda a: a.reshape(SEQ, a.shape[2])
    in_specs, args, out_specs, out_shape, scratch = [], [], [], [], []
    for p, (q, k, v, sinks, _) in enumerate(patterns):
        ck = k.shape[2]
        in_specs += [during(p, HALF_WIDTH), during(p, ck), during(p, ck)]
        args += [flat(q), flat(k), flat(v)]
        if sinks is not None:
            in_specs.append(pl.BlockSpec(sinks.shape, lambda s: (0, 0)))
            args.append(sinks)
        out_specs += [during(p, HALF_WIDTH), during(p, STAT_WIDTH)]
        out_shape += [jax.ShapeDtypeStruct((SEQ, HALF_WIDTH), BF16), jax.ShapeDtypeStruct((SEQ, STAT_WIDTH), F32)]
        scratch += [pltpu.VMEM((BLOCK, ck), BF16), pltpu.VMEM((BLOCK, ck), BF16)]
    outs = pl.pallas_call(
        body, name=name, grid=(len(patterns) * steps,), in_specs=in_specs,
        out_specs=tuple(out_specs), out_shape=tuple(out_shape), scratch_shapes=scratch,
        compiler_params=_params(("arbitrary",)),
    )(*args)
    return [(outs[2 * p].reshape(q.shape), outs[2 * p + 1].reshape(q.shape[0], q.shape[1], STAT_WIDTH))
            for p, (q, _, _, _, _) in enumerate(patterns)]


def _attn_bwd(name, q, k, v, d_o, lse, delta, sink_rows, max_dist):
    n_seq, length, _ = q.shape
    ck = k.shape[2]
    nb = length // BLOCK
    n_blocks = n_seq * nb
    n_rows = n_seq * length
    shared = ck == PAIR
    has_sinks = sink_rows is not None
    qb = BWD_BLOCKS_PER_STEP
    n_steps = n_blocks // qb

    def body(*refs):
        if has_sinks:
            (q_ref, kc_ref, vc_ref, do_ref, lse_ref, dl_ref, sink_ref,
             dq_ref, dk_ref, dv_ref, dsink_ref, ck_scr, cv_scr, kp_ref, vp_ref) = refs
        else:
            (q_ref, kc_ref, vc_ref, do_ref, lse_ref, dl_ref,
             dq_ref, dk_ref, dv_ref, ck_scr, cv_scr, kp_ref, vp_ref) = refs
        step = pl.program_id(0)

        @pl.when(step == 0)
        def _():
            ck_scr[...] = jnp.zeros_like(ck_scr)
            cv_scr[...] = jnp.zeros_like(cv_scr)
            kp_ref[...] = jnp.zeros_like(kp_ref)
            vp_ref[...] = jnp.zeros_like(vp_ref)
            if has_sinks:
                dsink_ref[...] = jnp.zeros_like(dsink_ref)

        valid = [_band_mask((step * qb + b) & (nb - 1), max_dist) for b in range(qb)]
        cols = [slice(p * PAIR, (p + 1) * PAIR) for p in range(N_PAIRS)]
        kcols = [slice(0, PAIR) if shared else c for c in cols]
        rows = [slice(b * BLOCK, (b + 1) * BLOCK) for b in range(qb)]
        units = [(b, p) for b in range(qb) for p in range(N_PAIRS)]
        n = range(len(units))
        nt = (((1,), (1,)), ((), ()))

        def window(prev_ref, cur_ref, b, kc):
            before = prev_ref[:, kc] if b == 0 else cur_ref[rows[b - 1], kc]
            return jnp.concatenate([before, cur_ref[rows[b], kc]], axis=0)

        q_st = [_stack_heads(q_ref[rows[b], cols[p]]) for b, p in units]
        do_st = [_stack_heads(do_ref[rows[b], cols[p]]) for b, p in units]
        k2 = [window(kp_ref, kc_ref, b, kcols[p]) for b, p in units]
        v2 = [window(vp_ref, vc_ref, b, kcols[p]) for b, p in units]
        st = [lax.dot_general(k2[u], q_st[u], nt, preferred_element_type=F32) for u in n]
        dpt = [lax.dot_general(v2[u], do_st[u], nt, preferred_element_type=F32) for u in n]
        lse_rows = [_stats_to_rows(lse_ref[rows[b], :]) for b in range(qb)]
        dl_rows = [_stats_to_rows(dl_ref[rows[b], :]) for b in range(qb)]
        lse_row = [lse_rows[b][p] for b, p in units]
        dl_row = [dl_rows[b][p] for b, p in units]
        pt = [jnp.exp2(jnp.where(valid[units[u][0]], st[u], NEG) - lse_row[u]) for u in n]
        dst = [(pt[u] * (dpt[u] - dl_row[u])).astype(BF16) for u in n]
        ptb = [t.astype(BF16) for t in pt]
        dv2 = [jnp.dot(ptb[u], do_st[u], preferred_element_type=F32) for u in n]
        dk2 = [jnp.dot(dst[u], q_st[u], preferred_element_type=F32) for u in n]
        k2t = [k2[u].astype(F32).T.astype(BF16) for u in n]
        dqt = [jnp.dot(k2t[u], dst[u], preferred_element_type=F32) for u in n]
        for u, (b, p) in enumerate(units):
            dq_ref[rows[b], cols[p]] = _unstack_t(dqt[u]).astype(BF16)
        if has_sinks:
            for u, (b, p) in enumerate(units):
                p_sink = jnp.exp2(sink_ref[p:p + 1, :] - lse_row[u])
                dsink_ref[p:p + 1, :] = dsink_ref[p:p + 1, :] - p_sink * dl_row[u]

        def total(parts, w, group):
            sel = [u for u, (b, p) in enumerate(units) if (shared or p == group)]
            terms = ([parts[u][:BLOCK] for u in sel if units[u][0] == w]
                     + [parts[u][BLOCK:] for u in sel if units[u][0] == w - 1])
            tot = terms[0]
            for t in terms[1:]:
                tot = tot + t
            return tot

        first_row = step * (qb * BLOCK)
        for acc_ref, out_ref, parts in ((ck_scr, dk_ref, dk2), (cv_scr, dv_ref, dv2)):
            for group in range(1 if shared else N_PAIRS):
                kc = kcols[group]

                @pl.when(step > 0)
                def _():
                    out_ref[pl.ds(pl.multiple_of(first_row - BLOCK, BLOCK), BLOCK), kc] = (
                        acc_ref[:, kc] + total(parts, 0, group)).astype(BF16)

                for w in range(1, qb):
                    out_ref[pl.ds(pl.multiple_of(first_row + (w - 1) * BLOCK, BLOCK), BLOCK), kc] = (
                        total(parts, w, group).astype(BF16))
                acc_ref[:, kc] = total(parts, qb, group)

        @pl.when(step == n_steps - 1)
        def _():
            dk_ref[pl.ds(n_rows - BLOCK, BLOCK), :] = ck_scr[...].astype(BF16)
            dv_ref[pl.ds(n_rows - BLOCK, BLOCK), :] = cv_scr[...].astype(BF16)

        kp_ref[...] = kc_ref[rows[-1], :]
        vp_ref[...] = vc_ref[rows[-1], :]

    cur = lambda width: pl.BlockSpec((qb * BLOCK, width), lambda s: (s, 0))
    whole = lambda width: pl.BlockSpec((n_rows, width), lambda s: (0, 0))
    flat = lambda a: a.reshape(n_rows, a.shape[2])
    in_specs = [cur(HALF_WIDTH), cur(ck), cur(ck), cur(HALF_WIDTH), cur(STAT_WIDTH), cur(STAT_WIDTH)]
    args = [flat(a) for a in (q, k, v, d_o, lse, delta)]
    out_specs = [cur(HALF_WIDTH), whole(ck), whole(ck)]
    out_shape = [jax.ShapeDtypeStruct((n_rows, HALF_WIDTH), BF16),
                 jax.ShapeDtypeStruct((n_rows, ck), BF16), jax.ShapeDtypeStruct((n_rows, ck), BF16)]
    if has_sinks:
        in_specs.append(pl.BlockSpec(sink_rows.shape, lambda s: (0, 0)))
        args.append(sink_rows)
        out_specs.append(pl.BlockSpec(sink_rows.shape, lambda s: (0, 0)))
        out_shape.append(jax.ShapeDtypeStruct(sink_rows.shape, F32))
    outs = pl.pallas_call(
        body, name=name, grid=(n_steps,), in_specs=in_specs,
        out_specs=tuple(out_specs), out_shape=tuple(out_shape),
        scratch_shapes=[pltpu.VMEM((BLOCK, ck), F32), pltpu.VMEM((BLOCK, ck), F32),
                        pltpu.VMEM((BLOCK, ck), BF16), pltpu.VMEM((BLOCK, ck), BF16)],
        compiler_params=_params(("arbitrary",)),
    )(*args)
    return tuple(o.reshape(n_seq, length, o.shape[1]) for o in outs[:3]) + tuple(outs[3:])


def _tail(oa, ob1, lb1, ob4, lb4, ob16, lb16, gate_a, gate_b, x, target, w_out, spread, gather):
    tm = ROW_TILE

    def split_dot(v, mat):
        hi = v.astype(BF16)
        lo = (v - hi.astype(F32)).astype(BF16)
        return jnp.dot(hi, mat, preferred_element_type=F32) + jnp.dot(lo, mat, preferred_element_type=F32)

    def body(oa_ref, ob1_ref, lb1_ref, ob4_ref, lb4_ref, ob16_ref, lb16_ref, ga_ref, gb_ref, x_ref, t_ref, w_ref,
             sp_ref, ga_mat_ref,
             loss_ref, dy_ref, gwo_ref, doa_ref, dla_ref, dga_ref, dgb_ref,
             dob_ref, dob4_ref, dob16_ref, dlb_ref, dlb4_ref, dlb16_ref, lse_ref, lse4_ref, lse16_ref,
             s_f, mix_keep, dy_keep):
        i = pl.program_id(0)
        sp, gat = sp_ref[...], ga_mat_ref[...]
        o4, o16 = _unfold_load(ob4_ref, s_f, 4, tm), _unfold_load(ob16_ref, s_f, 16, tm)
        l4, l16 = _unfold_load(lb4_ref, s_f, 4, tm), _unfold_load(lb16_ref, s_f, 16, tm)
        o1, l1 = ob1_ref[...].astype(F32), lb1_ref[...]
        mx = jnp.maximum(jnp.maximum(l1, l4), l16)
        e1, e4, e16 = jnp.exp2(l1 - mx), jnp.exp2(l4 - mx), jnp.exp2(l16 - mx)
        den = e1 + e4 + e16
        inv = 1.0 / den
        ob = split_dot(e1 * inv, sp) * o1 + split_dot(e4 * inv, sp) * o4 + split_dot(e16 * inv, sp) * o16
        lse_b = mx + jnp.log2(den)

        oa, ga, gb = oa_ref[...].astype(F32), ga_ref[...].astype(F32), gb_ref[...].astype(F32)
        sa, sb = _sigmoid(ga), _sigmoid(gb)
        mixed = jnp.concatenate(_unpair_tiles(_tiles(oa * (ga * sa))) + [ob * (gb * sb)], axis=1)
        mixed_bf = mixed.astype(BF16)
        w = w_ref[...]
        yv = x_ref[...] + jnp.dot(mixed_bf, w, preferred_element_type=F32)
        err = yv - t_ref[...]
        sq = jnp.sum(err * err, axis=0, keepdims=True)
        dy = err * (1.0 / D_MODEL)
        dy_ref[...] = dy
        dy_bf = dy.astype(BF16)
        mix_t = mixed.T.astype(BF16)

        @pl.when(i == 0)
        def _():
            loss_ref[...] = sq

        @pl.when(i > 0)
        def _():
            loss_ref[...] += sq

        @pl.when((i & 1) == 0)
        def _():
            mix_keep[...] = mix_t
            dy_keep[...] = dy_bf

        @pl.when((i & 1) == 1)
        def _():
            gw = jnp.dot(jnp.concatenate([mix_keep[...], mix_t], axis=1), jnp.concatenate([dy_keep[...], dy_bf], axis=0),
                         preferred_element_type=F32)

            @pl.when(i == 1)
            def _():
                gwo_ref[...] = gw

            @pl.when(i > 1)
            def _():
                gwo_ref[...] += gw

        dmix = lax.dot_general(dy_bf, w, (((1,), (1,)), ((), ())), preferred_element_type=F32)
        dma = jnp.concatenate(_pair_tiles(_tiles(dmix[:, :HALF_WIDTH])), axis=1)
        dmb = dmix[:, HALF_WIDTH:]

        doa = dma * (ga * sa)
        doa_ref[...] = doa.astype(BF16)
        dla_ref[...] = split_dot(doa * oa, gat)
        dga_ref[...] = (dma * oa * (sa * (1.0 + ga * (1.0 - sa)))).astype(BF16)
        dob = dmb * (gb * sb)
        dgb_ref[...] = (dmb * ob * (sb * (1.0 + gb * (1.0 - sb)))).astype(BF16)
        dlb = split_dot(dob * ob, gat)
        dob_ref[...] = dob.astype(BF16)
        _fold_store(dob, s_f, dob4_ref, dob16_ref, tm)
        dlb_ref[...] = dlb
        _fold_store(dlb, s_f, dlb4_ref, dlb16_ref, tm)
        lse_ref[...] = lse_b
        _fold_store(lse_b, s_f, lse4_ref, lse16_ref, tm)

    row = lambda width: pl.BlockSpec((tm, width), lambda i: (i, 0))
    full = lambda a: pl.BlockSpec(a.shape, lambda i: (0,) * a.ndim)
    fb_shapes, fb_specs = _fold_specs(tm, BF16)
    _, ff_specs = _fold_specs(tm, F32)
    st_shapes, st_specs = _fold_specs(tm, F32, STAT_WIDTH)
    nat = lambda dtype, width=HALF_WIDTH: jax.ShapeDtypeStruct((SEQ, width), dtype)
    return pl.pallas_call(
        body, name="tail", grid=(SEQ // tm,),
        in_specs=[row(HALF_WIDTH), row(HALF_WIDTH), row(STAT_WIDTH), ff_specs[0], st_specs[0], ff_specs[1], st_specs[1],
                  row(HALF_WIDTH), row(HALF_WIDTH), row(D_MODEL), row(D_MODEL), full(w_out), full(spread), full(gather)],
        out_specs=(pl.BlockSpec((1, D_MODEL), lambda i: (0, 0)), row(D_MODEL),
                   pl.BlockSpec((D_MODEL, D_MODEL), lambda i: (0, 0)),
                   row(HALF_WIDTH), row(STAT_WIDTH), row(HALF_WIDTH), row(HALF_WIDTH),
                   row(HALF_WIDTH), *fb_specs, row(STAT_WIDTH), *st_specs, row(STAT_WIDTH), *st_specs),
        out_shape=(jax.ShapeDtypeStruct((1, D_MODEL), F32), jax.ShapeDtypeStruct((SEQ, D_MODEL), F32),
                   jax.ShapeDtypeStruct((D_MODEL, D_MODEL), F32),
                   nat(BF16), nat(F32, STAT_WIDTH), nat(BF16), nat(BF16),
                   nat(BF16), *fb_shapes, nat(F32, STAT_WIDTH), *st_shapes, nat(F32, STAT_WIDTH), *st_shapes),
        scratch_shapes=[_fold_scratch(tm), pltpu.VMEM((D_MODEL, tm), BF16), pltpu.VMEM((tm, D_MODEL), BF16)],
        compiler_params=_params(("arbitrary",)),
    )(oa, ob1, lb1, ob4, lb4, ob16, lb16, gate_a, gate_b, x, target, w_out, spread, gather)


def _dproj_assemble(dqa, dka, dva, dga, dgb, dq1, dk1, dv1, dq4, dk4, dv4, dq16, dk16, dv16, tqa, tqb, tkb, tka,
                    qkg, cos4, sin4, bmean):
    tm = ROW_TILE

    def norm_rope_bwd(d_out, t, g, cos, sin, bm, scale):
        d_r = d_out * scale
        dyv = d_r * cos + _swap_halves(d_r * sin)
        rr = lax.rsqrt(_head_sum(t * t, bm) + EPS)
        that = t * rr
        dgain = jnp.sum(dyv * that, axis=0, keepdims=True)
        gdy = dyv * g
        dt = rr * (gdy - that * _head_sum(that * gdy, bm))
        return dt, dgain

    def body(dqa_ref, dka_ref, dva_ref, dga_ref, dgb_ref, dq1_ref, dk1_ref, dv1_ref, dq4_ref, dk4_ref, dv4_ref,
             dq16_ref, dk16_ref, dv16_ref, tqa_ref, tqb_ref, tkb_ref, tka_ref, qkg_ref, cos_ref, sin_ref, bm_ref,
             dproj_ref, dqkg_ref, s_f):
        i = pl.program_id(0)
        cos, sin, bm = cos_ref[...], sin_ref[...], bm_ref[...]

        def merged(nat_ref, f4_ref, f16_ref):
            return nat_ref[...].astype(F32) + _unfold_load(f4_ref, s_f, 4, tm) + _unfold_load(f16_ref, s_f, 16, tm)

        @pl.when(i == 0)
        def _():
            dqkg_ref[...] = jnp.zeros_like(dqkg_ref)

        def through(d_out, t, row, scale, c0, paired=False):
            g = qkg_ref[row:row + 1, :]
            tot = jnp.zeros((1, PAIR), F32)
            dts = []
            for j in range(d_out.shape[1] // PAIR):
                cols = slice(j * PAIR, (j + 1) * PAIR)
                dt, dg = norm_rope_bwd(d_out[:, cols], t[:, cols], g, cos, sin, bm, scale)
                dts.append(dt)
                tot = tot + dg
            if paired:
                dts = _unpair_tiles(dts)
            for j, dt in enumerate(dts):
                dproj_ref[:, c0 + j * PAIR:c0 + (j + 1) * PAIR] = dt.astype(BF16)
            dqkg_ref[row:row + 1, :] += tot

        through(dqa_ref[...].astype(F32), tqa_ref[...].astype(F32), 0, HEAD_DIM ** -0.5, C_QA, paired=True)
        through(dka_ref[...].astype(F32), tka_ref[...].astype(F32), 1, LN2, C_KA)
        through(merged(dq1_ref, dq4_ref, dq16_ref), tqb_ref[...].astype(F32), 2, HEAD_DIM ** -0.5, C_QB)
        through(merged(dk1_ref, dk4_ref, dk16_ref), tkb_ref[...].astype(F32), 3, LN2, C_KB)
        dproj_ref[:, C_VB:C_VB + HALF_WIDTH] = merged(dv1_ref, dv4_ref, dv16_ref).astype(BF16)
        dproj_ref[:, C_GA:C_GA + HALF_WIDTH] = jnp.concatenate(
            _unpair_tiles(_tiles(dga_ref[...].astype(F32))), axis=1).astype(BF16)
        dproj_ref[:, C_GB:C_GB + HALF_WIDTH] = dgb_ref[...].astype(BF16)
        dproj_ref[:, C_VA:C_VA + KV_A_WIDTH] = dva_ref[...].astype(BF16)

    row = lambda width: pl.BlockSpec((tm, width), lambda i: (i, 0))
    full = lambda a: pl.BlockSpec(a.shape, lambda i: (0,) * a.ndim)
    _, ff_specs = _fold_specs(tm, F32)
    return pl.pallas_call(
        body, name="dproj_assemble", grid=(SEQ // tm,),
        in_specs=[row(HALF_WIDTH), row(KV_A_WIDTH), row(KV_A_WIDTH), row(HALF_WIDTH), row(HALF_WIDTH),
                  row(HALF_WIDTH), row(HALF_WIDTH), row(HALF_WIDTH), ff_specs[0], ff_specs[0], ff_specs[0],
                  ff_specs[1], ff_specs[1], ff_specs[1],
                  row(HALF_WIDTH), row(HALF_WIDTH), row(HALF_WIDTH), row(KV_A_WIDTH),
                  full(qkg), row(PAIR), row(PAIR), full(bmean)],
        out_specs=(row(IN_WIDTH), pl.BlockSpec((SMALL_ROWS, PAIR), lambda i: (0, 0))),
        out_shape=(jax.ShapeDtypeStruct((SEQ, IN_WIDTH), BF16), jax.ShapeDtypeStruct((SMALL_ROWS, PAIR), F32)),
        scratch_shapes=[_fold_scratch(tm)],
        compiler_params=_params(("arbitrary",)),
    )(dqa, dka, dva, dga, dgb, dq1, dk1, dv1, dq4, dk4, dv4, dq16, dk16, dv16, tqa, tqb, tkb, tka, qkg, cos4, sin4, bmean)


def _input_grad_reduce(dproj, w, x, gain, dy, blocks_in, blocks_out, small):
    tm = ROW_TILE
    n_steps = SEQ // tm
    stage2_step, stage3_step = 3, 8
    shapes = (blocks_in.shape[1:], blocks_out.shape[1:])

    def body(dp_ref, w_ref, x_ref, g_ref, dy_ref, ga_hbm, gb_hbm, small_ref,
             gx_ref, out_a, out_b, small_out_ref, dgain_out_ref,
             part_a, part_b, sib_a, sib_b, wire_a, wire_b, chips_a, chips_b, small_all, dgain_acc, dgain_all,
             load_sems, sib_send, sib_recv, chip_send, chip_recv, small_send, small_recv, dgain_send, dgain_recv):
        i = pl.program_id(0)
        x, y, c = lax.axis_index("x"), lax.axis_index("y"), lax.axis_index("c")
        sibling = (x, y, 1 - c)
        chips = [(x, y), (1 - x, y), (x, 1 - y), (1 - x, 1 - y)]
        my_id = 4 * x + 2 * y + c
        g_hbm, part, from_sib = (ga_hbm, gb_hbm), (part_a, part_b), (sib_a, sib_b)
        to_wire, from_chips, out = (wire_a, wire_b), (chips_a, chips_b), (out_a, out_b)
        both = (0, 1)

        def blk(a, chip, core):
            return g_hbm[a].at[4 * chip[0] + 2 * chip[1] + core]

        def to_all(src, dst_all, send, recv):
            copies = []
            for rel in range(1, N_DEV):
                dx, dy_, dc = (rel >> 2) & 1, (rel >> 1) & 1, rel & 1
                to = (1 - x if dx else x, 1 - y if dy_ else y, 1 - c if dc else c)
                copies.append(pltpu.make_async_remote_copy(
                    src_ref=src, dst_ref=dst_all.at[my_id], send_sem=send.at[rel - 1], recv_sem=recv.at[rel - 1],
                    device_id=to, device_id_type=MESH))
            return copies

        small_copies = to_all(small_all.at[my_id], small_all, small_send, small_recv)
        dgain_copies = to_all(dgain_acc, dgain_all, dgain_send, dgain_recv)
        loads = [[pltpu.make_async_copy(blk(a, chips[k], c), part[a].at[k], load_sems.at[a, k]) for k in range(4)] for a in both]
        to_sib = [[pltpu.make_async_remote_copy(
            src_ref=blk(a, chips[k], 1 - c), dst_ref=from_sib[a].at[k], send_sem=sib_send.at[a, k], recv_sem=sib_recv.at[a, k],
            device_id=sibling, device_id_type=MESH) for k in range(4)] for a in both]
        first, other, k_first, k_other = _routes()
        to_chips = [[pltpu.make_async_remote_copy(
            src_ref=to_wire[a].at[s], dst_ref=from_chips[a].at[s], send_sem=chip_send.at[a, s], recv_sem=chip_recv.at[a, s],
            device_id=(first, first, other)[s], device_id_type=MESH) for s in range(3)] for a in both]

        def chip_partial(a, k):
            return part[a][k].astype(F32) + from_sib[a][k].astype(F32)

        @pl.when(i == 0)
        def _():
            small_all[my_id] = small_ref[...]
            for cp in small_copies:
                cp.start()
            for k in (1, 2, 3, 0):
                for a in both:
                    loads[a][k].start()
                    to_sib[a][k].start()

        @pl.when(i == stage2_step)
        def _():
            for k in (1, 2, 3):
                for a in both:
                    loads[a][k].wait()
                    to_sib[a][k].wait_recv()
            for s, k in ((0, 3), (1, k_first)):
                for a in both:
                    to_wire[a][s] = chip_partial(a, k).astype(BF16)
                    to_chips[a][s].start()

        @pl.when(i == stage3_step)
        def _():
            for a in both:
                to_chips[a][0].wait_recv()
                to_wire[a][2] = (chip_partial(a, k_other) + from_chips[a][0].astype(F32)).astype(BF16)
                to_chips[a][2].start()

        dh = lax.dot_general(dp_ref[...], w_ref[...], (((1,), (1,)), ((), ())), preferred_element_type=F32)
        xf = x_ref[...]
        r = lax.rsqrt(jnp.mean(xf * xf, axis=-1, keepdims=True) + EPS)
        xhat = xf * r
        dg = jnp.sum(dh * xhat, axis=0, keepdims=True)
        dxh = dh * g_ref[...]
        dx = r * (dxh - xhat * jnp.mean(dxh * xhat, axis=-1, keepdims=True))
        gx_ref[...] = dy_ref[...] + dx

        @pl.when(i == 0)
        def _():
            dgain_acc[...] = dg

        @pl.when(i > 0)
        def _():
            dgain_acc[...] += dg

        @pl.when(i == n_steps - 1)
        def _():
            dgain_all[my_id] = dgain_acc[...]
            for cp in dgain_copies:
                cp.start()
            for a in both:
                loads[a][0].wait()
                to_sib[a][0].wait_recv()
                acc = chip_partial(a, 0)
                for s in (1, 2):
                    to_chips[a][s].wait_recv()
                    acc = acc + from_chips[a][s].astype(F32)
                out[a][...] = acc
            for copies, gathered, dst in ((small_copies, small_all, small_out_ref), (dgain_copies, dgain_all, dgain_out_ref)):
                for cp in copies:
                    cp.wait_recv()
                tot = gathered[0]
                for d in range(1, N_DEV):
                    tot = tot + gathered[d]
                dst[...] = tot
            for cp in to_sib[0] + to_sib[1] + to_chips[0] + to_chips[1] + small_copies + dgain_copies:
                cp.wait_send()

    row = lambda width: pl.BlockSpec((tm, width), lambda i: (i, 0))
    full = lambda a: pl.BlockSpec(a.shape, lambda i: (0,) * a.ndim)
    whole = lambda shape: pl.BlockSpec(shape, lambda i: (0,) * len(shape))
    hbm = pl.BlockSpec(memory_space=pl.ANY)
    dtypes = (blocks_in.dtype, blocks_out.dtype)
    buf = lambda n, dts: [pltpu.VMEM((n,) + s, dt) for s, dt in zip(shapes, dts)]
    return pl.pallas_call(
        body, name="input_grad_rs", grid=(n_steps,),
        in_specs=[row(IN_WIDTH), full(w), row(D_MODEL), full(gain), row(D_MODEL), hbm, hbm, full(small)],
        out_specs=(row(D_MODEL), whole(shapes[0]), whole(shapes[1]), whole((SMALL_ROWS, SMALL_COLS)), whole((1, D_MODEL))),
        out_shape=(jax.ShapeDtypeStruct((SEQ, D_MODEL), F32), jax.ShapeDtypeStruct(shapes[0], F32),
                   jax.ShapeDtypeStruct(shapes[1], F32), jax.ShapeDtypeStruct((SMALL_ROWS, SMALL_COLS), F32),
                   jax.ShapeDtypeStruct((1, D_MODEL), F32)),
        scratch_shapes=[*buf(4, dtypes), *buf(4, dtypes), *buf(3, (BF16, BF16)), *buf(3, (BF16, BF16)),
                        pltpu.VMEM((N_DEV, SMALL_ROWS, SMALL_COLS), F32),
                        pltpu.VMEM((1, D_MODEL), F32), pltpu.VMEM((N_DEV, 1, D_MODEL), F32),
                        pltpu.SemaphoreType.DMA((2, 4)), pltpu.SemaphoreType.DMA((2, 4)), pltpu.SemaphoreType.DMA((2, 4)),
                        pltpu.SemaphoreType.DMA((2, 3)), pltpu.SemaphoreType.DMA((2, 3)),
                        pltpu.SemaphoreType.DMA((7,)), pltpu.SemaphoreType.DMA((7,)),
                        pltpu.SemaphoreType.DMA((7,)), pltpu.SemaphoreType.DMA((7,))],
        compiler_params=_params(("arbitrary",)),
    )(dproj, w, x, gain, dy, blocks_in, blocks_out, small)


def _weight_grad(h_t, dproj):
    tk = 1024
    cb = IN_WIDTH // 2
    n_k = SEQ // tk

    def body(ht_ref, dp_ref, out_ref, acc):
        k = pl.program_id(1)
        upd = jnp.dot(ht_ref[...], dp_ref[...], preferred_element_type=F32)

        @pl.when(k == 0)
        def _():
            acc[...] = upd

        @pl.when(k > 0)
        def _():
            acc[...] += upd

        @pl.when(k == n_k - 1)
        def _():
            for b in range(N_DEV // 2):
                out_ref[b] = acc[:, b * SHARD_IN:(b + 1) * SHARD_IN].astype(BF16)

    return pl.pallas_call(
        body, name="weight_grad", grid=(2, n_k),
        in_specs=[pl.BlockSpec((D_MODEL, tk), lambda j, k: (0, k)), pl.BlockSpec((tk, cb), lambda j, k: (k, j))],
        out_specs=pl.BlockSpec((N_DEV // 2, D_MODEL, SHARD_IN), lambda j, k: (j, 0, 0)),
        out_shape=jax.ShapeDtypeStruct((N_DEV, D_MODEL, SHARD_IN), BF16),
        scratch_shapes=[pltpu.VMEM((D_MODEL, cb), F32)],
        compiler_params=_params(("arbitrary", "arbitrary")),
    )(h_t, dproj)


def _adamw(name, w, g, m, v):
    def body(w_ref, g_ref, m_ref, v_ref, d_ref, nm_ref, nv_ref):
        gv = g_ref[...]
        nm = ADAM_B1 * m_ref[...] + (1.0 - ADAM_B1) * gv
        nv = ADAM_B2 * v_ref[...] + (1.0 - ADAM_B2) * jnp.square(gv)
        m_hat = nm / (1.0 - ADAM_B1 ** ADAM_STEP)
        v_hat = nv / (1.0 - ADAM_B2 ** ADAM_STEP)
        d_ref[...] = -ADAM_LR * (m_hat / (jnp.sqrt(v_hat) + ADAM_EPS) + ADAM_WD * w_ref[...])
        nm_ref[...] = nm
        nv_ref[...] = nv

    vmem = pl.BlockSpec(memory_space=pltpu.VMEM)
    out = jax.ShapeDtypeStruct(w.shape, F32)
    return pl.pallas_call(
        body, name=name, in_specs=[vmem] * 4, out_specs=(vmem,) * 3, out_shape=(out,) * 3,
        compiler_params=pltpu.CompilerParams(vmem_limit_bytes=VMEM_LIMIT),
    )(w, g, m, v)


SMALL_USED = D_MODEL + 4 * HEAD_DIM + 8


def _pack_small(norm_gain, qa, ka, sinks, qb, kb, extra=None):
    parts = [norm_gain.reshape(-1), qa.reshape(-1), ka.reshape(-1), sinks.reshape(-1), qb.reshape(-1), kb.reshape(-1)]
    if extra is not None:
        parts.append(extra.reshape(-1))
    flat = jnp.concatenate(parts)
    flat = jnp.pad(flat, (0, SMALL_ROWS * SMALL_COLS - flat.shape[0]))
    return flat.reshape(SMALL_ROWS, SMALL_COLS)


def _unpack_small(a):
    flat = a.reshape(-1)
    sizes = (D_MODEL, HEAD_DIM, HEAD_DIM, 8, HEAD_DIM, HEAD_DIM)
    out, off = [], 0
    for s in sizes:
        out.append(flat[off:off + s].reshape(1, s))
        off += s
    return out


def _fold_heads(row):
    return row[0, :HEAD_DIM] + row[0, HEAD_DIM:]


def kernel(x, norm_gain, w_in, q_norm_a, k_norm_a, sinks_a, q_norm_b, k_norm_b, w_out, loss_target, m_norm_gain, m_w_in, m_q_norm_a, m_k_norm_a, m_sinks_a, m_q_norm_b, m_k_norm_b, m_w_out, v_norm_gain, v_w_in, v_q_norm_a, v_k_norm_a, v_sinks_a, v_q_norm_b, v_k_norm_b, v_w_out):
    x2, tgt = x[0], loss_target[0]
    w_in_sh, w_out_sh = w_in[0], w_out[0]

    w_full = _all_gather_w_in(w_in_sh)

    inv = np.float32(ROPE_THETA) ** (-np.arange(HEAD_DIM // 2, dtype=np.float32) / np.float32(HEAD_DIM // 2))
    ang = np.arange(SEQ, dtype=np.float32)[:, None] * inv[None, :].astype(np.float32)
    cos, sin = np.cos(ang).astype(np.float32), np.sin(ang).astype(np.float32)
    cos4 = jnp.asarray(np.concatenate([cos, cos, cos, cos], axis=1))
    sin4 = jnp.asarray(np.concatenate([-sin, sin, -sin, sin], axis=1))
    blockdiag = np.kron(np.eye(2, dtype=np.float32), np.ones((HEAD_DIM, HEAD_DIM), np.float32))
    bmean = jnp.asarray(blockdiag / HEAD_DIM, dtype=BF16)
    gather_np = np.kron(np.eye(2 * N_PAIRS, dtype=np.float32), np.ones((HEAD_DIM, STAT_REP), np.float32))
    spread_np = np.kron(np.eye(2 * N_PAIRS, dtype=np.float32), np.ones((STAT_REP, HEAD_DIM), np.float32))
    spread_np[np.arange(STAT_WIDTH) % STAT_REP != 0] = 0.0
    gather, spread = jnp.asarray(gather_np, dtype=BF16), jnp.asarray(spread_np, dtype=BF16)
    two = lambda g: jnp.concatenate([g, g], axis=1)
    qkg = jnp.concatenate([two(q_norm_a), two(k_norm_a), two(q_norm_b), two(k_norm_b),
                           jnp.zeros((SMALL_ROWS - 4, PAIR), F32)], axis=0)
    sinks_paired = jnp.stack([sinks_a[0, :N_PAIRS], sinks_a[0, N_PAIRS:]], axis=1) * LOG2E
    sink_rows = jnp.concatenate([jnp.repeat(sinks_paired, BLOCK, axis=1),
                                 jnp.zeros((SMALL_ROWS - N_PAIRS, 2 * BLOCK), F32)], axis=0)

    (tqa, tka, tqb, tkb, gate_a, gate_b, h_t, qa, ka, va, qb, kb, vb, qb4, qb16, kb4, kb16, vb4, vb16,
     gathered_out) = _proj_fwd(x2, norm_gain, w_full, qkg, cos4, sin4, bmean, w_out_sh)
    wo_full = gathered_out.reshape(D_MODEL, D_MODEL)
    (oa, la), (ob1, lb1), (ob4, lb4), (ob16, lb16) = _attn_fwd("attn_fwd", [
        (qa[None], ka[None], va[None], sink_rows, BLOCK - 1), (qb[None], kb[None], vb[None], None, BLOCK),
        (qb4, kb4, vb4, None, BLOCK), (qb16, kb16, vb16, None, BLOCK)])
    (loss_cols, dy, gwo, doa, dla, dga, dgb, dob, dob4, dob16, dlb, dlb4, dlb16, lse_b, lse4, lse16) = _tail(
        oa[0], ob1[0], lb1[0], ob4, lb4, ob16, lb16, gate_a, gate_b, x2, tgt, wo_full, spread, gather)

    dqa, dka, dva, dsink = _attn_bwd("attn_a_bwd", qa[None], ka[None], va[None], doa[None], la, dla[None], sink_rows, BLOCK - 1)
    dq1, dk1, dv1 = _attn_bwd("attn_b1_bwd", qb[None], kb[None], vb[None], dob[None], lse_b[None], dlb[None], None, BLOCK)
    dq4, dk4, dv4 = _attn_bwd("attn_b4_bwd", qb4, kb4, vb4, dob4, lse4, dlb4, None, BLOCK)
    dq16, dk16, dv16 = _attn_bwd("attn_b16_bwd", qb16, kb16, vb16, dob16, lse16, dlb16, None, BLOCK)
    dproj, dqkg = _dproj_assemble(dqa[0], dka[0], dva[0], dga, dgb, dq1[0], dk1[0], dv1[0], dq4, dk4, dv4,
                                  dq16, dk16, dv16, tqa, tqb, tkb, tka, qkg, cos4, sin4, bmean)
    gw_in = _weight_grad(h_t, dproj)

    blocks_in = gw_in
    blocks_out = gwo.reshape(N_DEV, SHARD_OUT, D_MODEL)
    g_sinks = jnp.concatenate([jnp.sum(dsink[:N_PAIRS, :BLOCK], axis=1), jnp.sum(dsink[:N_PAIRS, BLOCK:], axis=1)])
    small = _pack_small(jnp.zeros((D_MODEL,), F32), _fold_heads(dqkg[0:1]), _fold_heads(dqkg[1:2]), g_sinks,
                        _fold_heads(dqkg[2:3]), _fold_heads(dqkg[3:4]), extra=0.5 * jnp.sum(loss_cols) / D_MODEL)
    grad_x, g_w_in, g_w_out, small_red, dgain_red = _input_grad_reduce(
        dproj, w_full, x2, norm_gain, dy, blocks_in, blocks_out, small)
    n_gain_rows = D_MODEL // SMALL_COLS
    small_red = jnp.concatenate([dgain_red.reshape(n_gain_rows, SMALL_COLS), small_red[n_gain_rows:]], axis=0)
    g_small = _unpack_small(small_red)

    d_in, nm_in, nv_in = _adamw("adamw_w_in", w_in_sh, g_w_in, m_w_in[0], v_w_in[0])
    d_out, nm_out, nv_out = _adamw("adamw_w_out", w_out_sh, g_w_out, m_w_out[0], v_w_out[0])
    d_s, nm_s, nv_s = _adamw(
        "adamw_small",
        _pack_small(norm_gain, q_norm_a, k_norm_a, sinks_a, q_norm_b, k_norm_b), small_red,
        _pack_small(m_norm_gain, m_q_norm_a, m_k_norm_a, m_sinks_a, m_q_norm_b, m_k_norm_b),
        _pack_small(v_norm_gain, v_q_norm_a, v_k_norm_a, v_sinks_a, v_q_norm_b, v_k_norm_b))
    d_small, nm_small, nv_small = _unpack_small(d_s), _unpack_small(nm_s), _unpack_small(nv_s)

    loss = small_red.reshape(-1)[SMALL_USED]

    def assemble(small_list, big_in, big_out):
        ng, qa_, ka_, sk_, qb_, kb_ = small_list
        return [ng, big_in[None], qa_, ka_, sk_, qb_, kb_, big_out[None]]

    return (loss, grad_x[None], *assemble(g_small, g_w_in, g_w_out), *assemble(d_small, d_in, d_out),
            *assemble(nm_small, nm_in, nm_out), *assemble(nv_small, nv_in, nv_out))
```

```python
import functools

import numpy as np
import jax
import jax.numpy as jnp
from jax import lax
from jax.experimental import pallas as pl
from jax.experimental.pallas import tpu as pltpu

F32 = jnp.float32
BF16 = jnp.bfloat16

SEQ = 4096
D_MODEL = 1024
HEAD_DIM = 64
PAIR = 2 * HEAD_DIM
N_PAIRS = 4
HALF_WIDTH = N_PAIRS * PAIR
KV_A_WIDTH = 128
IN_WIDTH = 3328
BLOCK = 128
STAT_REP = 16
STAT_WIDTH = 128
EPS = 1e-6
NEG = -1e30
ROPE_THETA = 10000.0
LOG2E = 1.4426950408889634
LN2 = 0.6931471805599453
Q_SCALE = HEAD_DIM ** -0.5 * LOG2E
N_DEV = 8
SHARD_IN = IN_WIDTH // N_DEV
SHARD_OUT = D_MODEL // N_DEV
PAYLOAD = SHARD_IN + SHARD_OUT
SMALL_ROWS, SMALL_COLS = 8, 256

C_QA, C_KA, C_VA, C_GA, C_QB, C_KB, C_VB, C_GB = 0, 512, 640, 768, 1280, 1792, 2304, 2816

ADAM_LR = 0.001
ADAM_B1 = 0.9
ADAM_B2 = 0.999
ADAM_EPS = 1e-08
ADAM_WD = 0.01
ADAM_STEP = 10

ROW_TILE = 256
PROJ_ROW_TILE = 512
FWD_BLOCKS_PER_STEP = 8
BWD_BLOCKS_PER_STEP = 8
VMEM_LIMIT = 56 * 1024 * 1024

MESH = pl.DeviceIdType.MESH


def _params(sem, vmem=VMEM_LIMIT):
    return pltpu.CompilerParams(dimension_semantics=sem, vmem_limit_bytes=vmem)


def _head_sum(v, bm):
    hi = v.astype(BF16)
    lo = (v - hi.astype(F32)).astype(BF16)
    return (jnp.dot(hi, bm, preferred_element_type=F32) + jnp.dot(lo, bm, preferred_element_type=F32))


def _swap_halves(y):
    lane = lax.broadcasted_iota(jnp.int32, y.shape, 1)
    first = (lane & 32) == 0
    return jnp.where(first, pltpu.roll(y, 96, 1), pltpu.roll(y, 32, 1))


def _sigmoid(g):
    return 1.0 / (1.0 + jnp.exp(-g))


def _tiles(a):
    return [a[:, j * PAIR:(j + 1) * PAIR] for j in range(N_PAIRS)]


def _pair_tiles(t):
    low = lax.broadcasted_iota(jnp.int32, t[0].shape, 1) < HEAD_DIM
    r = [pltpu.roll(a, HEAD_DIM, 1) for a in t]
    return [jnp.where(low, t[0], r[2]), jnp.where(low, r[0], t[2]), jnp.where(low, t[1], r[3]), jnp.where(low, r[1], t[3])]


def _unpair_tiles(p):
    low = lax.broadcasted_iota(jnp.int32, p[0].shape, 1) < HEAD_DIM
    r = [pltpu.roll(a, HEAD_DIM, 1) for a in p]
    return [jnp.where(low, p[0], r[1]), jnp.where(low, p[2], r[3]), jnp.where(low, r[0], p[1]), jnp.where(low, r[2], p[3])]


def _routes():
    x, y, c = lax.axis_index("x"), lax.axis_index("y"), lax.axis_index("c")
    north = c == 1
    first = (jnp.where(north, 1 - x, x), jnp.where(north, y, 1 - y), c)
    other = (jnp.where(north, x, 1 - x), jnp.where(north, 1 - y, y), c)
    k_first = jnp.where(north, 1, 2)
    return first, other, k_first, 3 - k_first


def _gather_plan(mine_ref, out_ref, send_sems, recv_sems):
    x, y, c = lax.axis_index("x"), lax.axis_index("y"), lax.axis_index("c")
    me, sibling, diag = (x, y, c), (x, y, 1 - c), (1 - x, 1 - y, c)
    first, other, k_first, k_other = _routes()

    def slot(px, py, pc):
        return out_ref.at[4 * px + 2 * py + pc]

    def copy(k, block, to, from_mine=False):
        return pltpu.make_async_remote_copy(
            src_ref=mine_ref if from_mine else slot(*block), dst_ref=slot(*block),
            send_sem=send_sems.at[k], recv_sem=recv_sems.at[k], device_id=to, device_id_type=MESH)

    sends = [copy(0, me, sibling, True), copy(1, me, (1 - x, y, c), True), copy(2, me, (x, 1 - y, c), True)]
    stages = [(copy(k_first, first, me), [copy(3, first, other), copy(3 + k_first, first, sibling)]),
              (copy(k_other, other, me), [copy(3 + k_other, other, sibling)]),
              (copy(3, diag, me), [copy(6, diag, sibling)])]
    from_sibling = [copy(0, sibling, me), copy(4, (1 - x, y, 1 - c), me), copy(5, (x, 1 - y, 1 - c), me),
                    copy(6, (1 - x, 1 - y, 1 - c), me)]
    return slot(*me), sends, stages, from_sibling


GATHER_SCRATCH = [pltpu.SemaphoreType.DMA((7,)), pltpu.SemaphoreType.DMA((7,))]


def _all_gather_w_in(w_in_sh):
    rows, cols = w_in_sh.shape

    def body(w_ref, out_ref, mine_ref, blocks, send_sems, recv_sems):
        mine_ref[...] = w_ref[...].astype(BF16)
        my_slot, sends, stages, from_sibling = _gather_plan(mine_ref, blocks, send_sems, recv_sems)
        for cp in sends:
            cp.start()
        my_slot[...] = mine_ref[...]
        for arrival, forwards in stages:
            arrival.wait_recv()
            for cp in forwards:
                cp.start()
        for arrival in from_sibling:
            arrival.wait_recv()
        for cp in sends + [cp for _, forwards in stages for cp in forwards]:
            cp.wait_send()
        for d in range(N_DEV):
            out_ref[:, d * cols:(d + 1) * cols] = blocks[d]

    vmem = pl.BlockSpec(memory_space=pltpu.VMEM)
    return pl.pallas_call(
        body, name="ag_w_in",
        out_shape=jax.ShapeDtypeStruct((rows, N_DEV * cols), BF16),
        in_specs=[vmem], out_specs=vmem,
        scratch_shapes=[pltpu.VMEM((rows, cols), BF16), pltpu.VMEM((N_DEV, rows, cols), BF16)] + GATHER_SCRATCH,
        compiler_params=pltpu.CompilerParams(vmem_limit_bytes=VMEM_LIMIT),
    )(w_in_sh)


def _fold_scratch(tm):
    return pltpu.VMEM((N_PAIRS, tm, PAIR), F32)


def _fold_store(val, scr, out4, out16, tm):
    groups = range(val.shape[1] // PAIR)
    for j in groups:
        scr[j] = val[:, j * PAIR:(j + 1) * PAIR]
    for dil, out in ((4, out4), (16, out16)):
        for r in range(dil):
            for j in groups:
                out[r, :, j * PAIR:(j + 1) * PAIR] = scr[j, pl.ds(r, tm // dil, stride=dil), :].astype(out.dtype)


def _unfold_load(src, scr, dil, tm):
    groups = range(src.shape[2] // PAIR)
    for r in range(dil):
        for j in groups:
            scr[j, pl.ds(r, tm // dil, stride=dil), :] = src[r, :, j * PAIR:(j + 1) * PAIR].astype(F32)
    return jnp.concatenate([scr[j] for j in groups], axis=1)


def _fold_specs(tm, dtype, width=HALF_WIDTH):
    shapes = (jax.ShapeDtypeStruct((4, SEQ // 4, width), dtype), jax.ShapeDtypeStruct((16, SEQ // 16, width), dtype))
    specs = (pl.BlockSpec((4, tm // 4, width), lambda i: (0, i, 0)),
             pl.BlockSpec((16, tm // 16, width), lambda i: (0, i, 0)))
    return shapes, specs


def _proj_fwd(x, gain, w, qkg, cos4, sin4, bmean, w_out_sh):
    tm = PROJ_ROW_TILE
    n_steps = SEQ // tm

    def norm_rope(t, g, cos, sin, bm, scale):
        rr = lax.rsqrt(_head_sum(t * t, bm) + EPS)
        yv = t * rr * g
        return (yv * cos + _swap_halves(yv) * sin) * scale

    def body(x_ref, g_ref, w_ref, qkg_ref, cos_ref, sin_ref, bm_ref, wo_ref,
             tqa_ref, tka_ref, tqb_ref, tkb_ref, ga_ref, gb_ref, ht_ref, qa_ref, ka_ref, va_ref, qb_ref, kb_ref, vb_ref,
             qb4_ref, qb16_ref, kb4_ref, kb16_ref, vb4_ref, vb16_ref, wo_all_ref,
             proj, scr, wo_mine, wo_all, send_sems, recv_sems):
        i = pl.program_id(0)
        my_slot, sends, stages, from_sibling = _gather_plan(wo_mine, wo_all, send_sems, recv_sems)

        @pl.when(i == 0)
        def _():
            wo_mine[...] = wo_ref[...].astype(BF16)
            for cp in sends:
                cp.start()
            my_slot[...] = wo_mine[...]

        @pl.when(i == n_steps // 2)
        def _():
            for arrival, forwards in stages[:2]:
                arrival.wait_recv()
                for cp in forwards:
                    cp.start()

        xf = x_ref[...]
        r = lax.rsqrt(jnp.mean(xf * xf, axis=-1, keepdims=True) + EPS)
        hf = xf * r * g_ref[...]
        ht_ref[...] = hf.T.astype(BF16)
        cos, sin, bm = cos_ref[...], sin_ref[...], bm_ref[...]
        proj[...] = jnp.dot(hf.astype(BF16), w_ref[...], preferred_element_type=F32)

        def roped(tiles, row, scale):
            g = qkg_ref[row:row + 1, :]
            return jnp.concatenate([norm_rope(t, g, cos, sin, bm, scale) for t in tiles], axis=1)

        tqa = _pair_tiles(_tiles(proj[:, C_QA:C_QA + HALF_WIDTH]))
        tqa_ref[...] = jnp.concatenate(tqa, axis=1).astype(BF16)
        qa_ref[...] = roped(tqa, 0, Q_SCALE).astype(BF16)
        ga_ref[...] = jnp.concatenate(_pair_tiles(_tiles(proj[:, C_GA:C_GA + HALF_WIDTH])), axis=1).astype(BF16)
        gb_ref[...] = proj[:, C_GB:C_GB + HALF_WIDTH].astype(BF16)
        tqb = proj[:, C_QB:C_QB + HALF_WIDTH]
        tqb_ref[...] = tqb.astype(BF16)
        qb = roped(_tiles(tqb), 2, Q_SCALE)
        qb_ref[...] = qb.astype(BF16)
        _fold_store(qb, scr, qb4_ref, qb16_ref, tm)
        tkb = proj[:, C_KB:C_KB + HALF_WIDTH]
        tkb_ref[...] = tkb.astype(BF16)
        kb = roped(_tiles(tkb), 3, 1.0)
        kb_ref[...] = kb.astype(BF16)
        _fold_store(kb, scr, kb4_ref, kb16_ref, tm)
        vb = proj[:, C_VB:C_VB + HALF_WIDTH]
        vb_ref[...] = vb.astype(BF16)
        _fold_store(vb, scr, vb4_ref, vb16_ref, tm)
        tka = proj[:, C_KA:C_KA + KV_A_WIDTH]
        tka_ref[...] = tka.astype(BF16)
        ka_ref[...] = roped([tka], 1, 1.0).astype(BF16)
        va_ref[...] = proj[:, C_VA:C_VA + KV_A_WIDTH].astype(BF16)

        @pl.when(i == n_steps - 1)
        def _():
            arrival, forwards = stages[2]
            arrival.wait_recv()
            for cp in forwards:
                cp.start()
            for arrival in from_sibling:
                arrival.wait_recv()
            for cp in sends + [cp for _, forwards in stages for cp in forwards]:
                cp.wait_send()
            wo_all_ref[...] = wo_all[...]

    row = lambda width: pl.BlockSpec((tm, width), lambda i: (i, 0))
    full = lambda a: pl.BlockSpec(a.shape, lambda i: (0,) * a.ndim)
    nat = lambda width, dtype=BF16: jax.ShapeDtypeStruct((SEQ, width), dtype)
    f_shapes, f_specs = _fold_specs(tm, BF16)
    return pl.pallas_call(
        body, name="proj_fwd", grid=(SEQ // tm,),
        in_specs=[row(D_MODEL), full(gain), full(w), full(qkg), row(PAIR), row(PAIR), full(bmean), full(w_out_sh)],
        out_specs=(row(HALF_WIDTH), row(KV_A_WIDTH), row(HALF_WIDTH), row(HALF_WIDTH), row(HALF_WIDTH), row(HALF_WIDTH),
                   pl.BlockSpec((D_MODEL, tm), lambda i: (0, i)),
                   row(HALF_WIDTH), row(KV_A_WIDTH), row(KV_A_WIDTH), row(HALF_WIDTH), row(HALF_WIDTH), row(HALF_WIDTH),
                   *f_specs, *f_specs, *f_specs,
                   pl.BlockSpec((N_DEV,) + w_out_sh.shape, lambda i: (0, 0, 0))),
        out_shape=(nat(HALF_WIDTH), nat(KV_A_WIDTH), nat(HALF_WIDTH), nat(HALF_WIDTH), nat(HALF_WIDTH), nat(HALF_WIDTH),
                   jax.ShapeDtypeStruct((D_MODEL, SEQ), BF16),
                   nat(HALF_WIDTH), nat(KV_A_WIDTH), nat(KV_A_WIDTH), nat(HALF_WIDTH), nat(HALF_WIDTH), nat(HALF_WIDTH),
                   *f_shapes, *f_shapes, *f_shapes,
                   jax.ShapeDtypeStruct((N_DEV,) + w_out_sh.shape, BF16)),
        scratch_shapes=[pltpu.VMEM((tm, IN_WIDTH), F32), _fold_scratch(tm), pltpu.VMEM(w_out_sh.shape, BF16),
                        pltpu.VMEM((N_DEV,) + w_out_sh.shape, BF16)] + GATHER_SCRATCH,
        compiler_params=_params(("arbitrary",)),
    )(x, gain, w, qkg, cos4, sin4, bmean, w_out_sh)


def _fill_band_bias(bias_ref, max_dist):
    j = lax.broadcasted_iota(jnp.int32, (2 * BLOCK, 2 * BLOCK), 0)
    c = lax.broadcasted_iota(jnp.int32, (2 * BLOCK, 2 * BLOCK), 1)
    dist = (c & (BLOCK - 1)) + BLOCK - j
    band = (dist >= 0) & (dist <= max_dist)
    bias_ref[0] = jnp.where(band, 0.0, NEG)
    bias_ref[1] = jnp.where(band & (j >= BLOCK), 0.0, NEG)


def _band_bias(bias_ref, step, b, qb, nb):
    if nb < qb:
        return bias_ref[1 if b % nb == 0 else 0]
    if b > 0:
        return bias_ref[0]
    return bias_ref[jnp.where(((step * qb) & (nb - 1)) == 0, 1, 0)]


def _stack_heads(t):
    lane = lax.broadcasted_iota(jnp.int32, t.shape, 1)
    low = lane < HEAD_DIM
    zero = jnp.zeros_like(t)
    return jnp.concatenate([jnp.where(low, t, zero), jnp.where(low, zero, t)], axis=0)


def _stack_heads_t(t):
    tt = t.astype(F32).T
    low = lax.broadcasted_iota(jnp.int32, tt.shape, 0) < HEAD_DIM
    zero = jnp.zeros_like(tt)
    return jnp.concatenate([jnp.where(low, tt, zero), jnp.where(low, zero, tt)], axis=1).astype(BF16)


def _unstack_t(t):
    return jnp.concatenate([t[:HEAD_DIM, :BLOCK], t[HEAD_DIM:, BLOCK:]], axis=0).T


def _rows_to_stats(rows):
    parts = []
    for row in rows:
        parts.append(jnp.broadcast_to(row[:, :BLOCK], (STAT_REP, BLOCK)))
        parts.append(jnp.broadcast_to(row[:, BLOCK:], (STAT_REP, BLOCK)))
    return jnp.concatenate(parts, axis=0).T


def _stats_to_rows(t):
    tt = t.T
    return [jnp.concatenate([tt[2 * p * STAT_REP:2 * p * STAT_REP + 1, :],
                             tt[(2 * p + 1) * STAT_REP:(2 * p + 1) * STAT_REP + 1, :]], axis=1) for p in range(N_PAIRS)]


def _attn_fwd(name, patterns):
    qb = FWD_BLOCKS_PER_STEP
    steps = SEQ // (qb * BLOCK)

    def one_pattern(step, nb, shared, max_dist, q_ref, kc_ref, vc_ref, sink_ref, o_ref, lse_ref, kp_ref, vp_ref, bias_ref):
        has_sinks = sink_ref is not None

        @pl.when(step == 0)
        def _():
            kp_ref[...] = jnp.zeros_like(kp_ref)
            vp_ref[...] = jnp.zeros_like(vp_ref)
            _fill_band_bias(bias_ref, max_dist)

        cols = [slice(p * PAIR, (p + 1) * PAIR) for p in range(N_PAIRS)]
        kcols = [slice(0, PAIR) if shared else c for c in cols]
        rows = [slice(b * BLOCK, (b + 1) * BLOCK) for b in range(qb)]
        units = [(b, p) for b in range(qb) for p in range(N_PAIRS)]
        n = range(len(units))

        def window(prev_ref, cur_ref, b, kc):
            before = prev_ref[:, kc] if b == 0 else cur_ref[rows[b - 1], kc]
            return jnp.concatenate([before, cur_ref[rows[b], kc]], axis=0)

        st = [jnp.dot(window(kp_ref, kc_ref, b, kcols[p]), _stack_heads_t(q_ref[rows[b], cols[p]]),
                      preferred_element_type=F32) for b, p in units]
        st = [st[u] + _band_bias(bias_ref, step, units[u][0], qb, nb) for u in n]
        m = [jnp.max(s, axis=0, keepdims=True) for s in st]
        if has_sinks:
            sk = [sink_ref[p:p + 1, :] for _, p in units]
            m = [jnp.maximum(m[u], sk[u]) for u in n]
        pt = [jnp.exp2(st[u] - m[u]) for u in n]
        l = [jnp.sum(t, axis=0, keepdims=True) for t in pt]
        if has_sinks:
            l = [l[u] + jnp.exp2(sk[u] - m[u]) for u in n]
        v2t = [window(vp_ref, vc_ref, b, kcols[p]).astype(F32).T.astype(BF16) for b, p in units]
        ot = [jnp.dot(v2t[u], pt[u].astype(BF16), preferred_element_type=F32) / l[u] for u in n]
        for u, (b, p) in enumerate(units):
            o_ref[rows[b], cols[p]] = _unstack_t(ot[u]).astype(BF16)
        for b in range(qb):
            lse_ref[rows[b], :] = _rows_to_stats([m[u] + jnp.log2(l[u]) for u in n if units[u][0] == b])
        kp_ref[...] = kc_ref[rows[-1], :]
        vp_ref[...] = vc_ref[rows[-1], :]

    n_in = [4 if sinks is not None else 3 for _, _, _, sinks, _ in patterns]

    def body(*refs):
        ins, rest = refs[:sum(n_in)], refs[sum(n_in):]
        outs, scratch = rest[:2 * len(patterns)], rest[2 * len(patterns):]
        bias_ref = scratch[-1]
        step = pl.program_id(0)
        first = 0
        for p, (q, k, _, sinks, max_dist) in enumerate(patterns):
            mine = ins[first:first + n_in[p]]
            first += n_in[p]
            sink_ref = mine[3] if sinks is not None else None
            kp_ref, vp_ref = scratch[2 * p], scratch[2 * p + 1]

            @pl.when((step >= p * steps) & (step < (p + 1) * steps))
            def _():
                one_pattern(step - p * steps, q.shape[1] // BLOCK, k.shape[2] == PAIR, max_dist,
                            mine[0], mine[1], mine[2], sink_ref, outs[2 * p], outs[2 * p + 1], kp_ref, vp_ref, bias_ref)

    def during(p, width):
        return pl.BlockSpec((qb * BLOCK, width), lambda s: (jnp.clip(s - p * steps, 0, steps - 1), 0))

    flat = lambda a: a.reshape(SEQ, a.shape[2])
    in_specs, args, out_specs, out_shape, scratch = [], [], [], [], []
    for p, (q, k, v, sinks, _) in enumerate(patterns):
        ck = k.shape[2]
        in_specs += [during(p, HALF_WIDTH), during(p, ck), during(p, ck)]
        args += [flat(q), flat(k), flat(v)]
        if sinks is not None:
            in_specs.append(pl.BlockSpec(sinks.shape, lambda s: (0, 0)))
            args.append(sinks)
        out_specs += [during(p, HALF_WIDTH), during(p, STAT_WIDTH)]
        out_shape += [jax.ShapeDtypeStruct((SEQ, HALF_WIDTH), BF16), jax.ShapeDtypeStruct((SEQ, STAT_WIDTH), F32)]
        scratch += [pltpu.VMEM((BLOCK, ck), BF16), pltpu.VMEM((BLOCK, ck), BF16)]
    outs = pl.pallas_call(
        body, name=name, grid=(len(patterns) * steps,), in_specs=in_specs,
        out_specs=tuple(out_specs), out_shape=tuple(out_shape),
        scratch_shapes=scratch + [pltpu.VMEM((2, 2 * BLOCK, 2 * BLOCK), F32)],
        compiler_params=_params(("arbitrary",)),
    )(*args)
    return [(outs[2 * p].reshape(q.shape), outs[2 * p + 1].reshape(q.shape[0], q.shape[1], STAT_WIDTH))
            for p, (q, _, _, _, _) in enumerate(patterns)]


def _attn_bwd(name, q, k, v, d_o, lse, delta, sink_rows, max_dist):
    n_seq, length, _ = q.shape
    ck = k.shape[2]
    nb = length // BLOCK
    n_blocks = n_seq * nb
    n_rows = n_seq * length
    shared = ck == PAIR
    has_sinks = sink_rows is not None
    qb = BWD_BLOCKS_PER_STEP
    n_steps = n_blocks // qb

    def body(*refs):
        if has_sinks:
            (q_ref, kc_ref, vc_ref, do_ref, lse_ref, dl_ref, sink_ref,
             dq_ref, dk_ref, dv_ref, dsink_ref, ck_scr, cv_scr, kp_ref, vp_ref, bias_ref) = refs
        else:
            (q_ref, kc_ref, vc_ref, do_ref, lse_ref, dl_ref,
             dq_ref, dk_ref, dv_ref, ck_scr, cv_scr, kp_ref, vp_ref, bias_ref) = refs
        step = pl.program_id(0)

        @pl.when(step == 0)
        def _():
            ck_scr[...] = jnp.zeros_like(ck_scr)
            cv_scr[...] = jnp.zeros_like(cv_scr)
            kp_ref[...] = jnp.zeros_like(kp_ref)
            vp_ref[...] = jnp.zeros_like(vp_ref)
            if has_sinks:
                dsink_ref[...] = jnp.zeros_like(dsink_ref)
            _fill_band_bias(bias_ref, max_dist)

        cols = [slice(p * PAIR, (p + 1) * PAIR) for p in range(N_PAIRS)]
        kcols = [slice(0, PAIR) if shared else c for c in cols]
        rows = [slice(b * BLOCK, (b + 1) * BLOCK) for b in range(qb)]
        units = [(b, p) for b in range(qb) for p in range(N_PAIRS)]
        n = range(len(units))
        nt = (((1,), (1,)), ((), ()))

        def window(prev_ref, cur_ref, b, kc):
            before = prev_ref[:, kc] if b == 0 else cur_ref[rows[b - 1], kc]
            return jnp.concatenate([before, cur_ref[rows[b], kc]], axis=0)

        q_st = [_stack_heads(q_ref[rows[b], cols[p]]) for b, p in units]
        do_st = [_stack_heads(do_ref[rows[b], cols[p]]) for b, p in units]
        k2 = [window(kp_ref, kc_ref, b, kcols[p]) for b, p in units]
        v2 = [window(vp_ref, vc_ref, b, kcols[p]) for b, p in units]
        st = [lax.dot_general(k2[u], q_st[u], nt, preferred_element_type=F32) for u in n]
        dpt = [lax.dot_general(v2[u], do_st[u], nt, preferred_element_type=F32) for u in n]
        lse_rows = [_stats_to_rows(lse_ref[rows[b], :]) for b in range(qb)]
        dl_rows = [_stats_to_rows(dl_ref[rows[b], :]) for b in range(qb)]
        lse_row = [lse_rows[b][p] for b, p in units]
        dl_row = [dl_rows[b][p] for b, p in units]
        pt = [jnp.exp2(st[u] + _band_bias(bias_ref, step, units[u][0], qb, nb) - lse_row[u]) for u in n]
        dst = [(pt[u] * (dpt[u] - dl_row[u])).astype(BF16) for u in n]
        ptb = [t.astype(BF16) for t in pt]
        dv2 = [jnp.dot(ptb[u], do_st[u], preferred_element_type=F32) for u in n]
        dk2 = [jnp.dot(dst[u], q_st[u], preferred_element_type=F32) for u in n]
        k2t = [k2[u].astype(F32).T.astype(BF16) for u in n]
        dqt = [jnp.dot(k2t[u], dst[u], preferred_element_type=F32) for u in n]
        for u, (b, p) in enumerate(units):
            dq_ref[rows[b], cols[p]] = _unstack_t(dqt[u]).astype(BF16)
        if has_sinks:
            for u, (b, p) in enumerate(units):
                p_sink = jnp.exp2(sink_ref[p:p + 1, :] - lse_row[u])
                dsink_ref[p:p + 1, :] = dsink_ref[p:p + 1, :] - p_sink * dl_row[u]

        def total(parts, w, group):
            sel = [u for u, (b, p) in enumerate(units) if (shared or p == group)]
            terms = ([parts[u][:BLOCK] for u in sel if units[u][0] == w]
                     + [parts[u][BLOCK:] for u in sel if units[u][0] == w - 1])
            tot = terms[0]
            for t in terms[1:]:
                tot = tot + t
            return tot

        first_row = step * (qb * BLOCK)
        for acc_ref, out_ref, parts in ((ck_scr, dk_ref, dk2), (cv_scr, dv_ref, dv2)):
            for group in range(1 if shared else N_PAIRS):
                kc = kcols[group]

                @pl.when(step > 0)
                def _():
                    out_ref[pl.ds(pl.multiple_of(first_row - BLOCK, BLOCK), BLOCK), kc] = (
                        acc_ref[:, kc] + total(parts, 0, group)).astype(BF16)

                for w in range(1, qb):
                    out_ref[pl.ds(pl.multiple_of(first_row + (w - 1) * BLOCK, BLOCK), BLOCK), kc] = (
                        total(parts, w, group).astype(BF16))
                acc_ref[:, kc] = total(parts, qb, group)

        @pl.when(step == n_steps - 1)
        def _():
            dk_ref[pl.ds(n_rows - BLOCK, BLOCK), :] = ck_scr[...].astype(BF16)
            dv_ref[pl.ds(n_rows - BLOCK, BLOCK), :] = cv_scr[...].astype(BF16)

        kp_ref[...] = kc_ref[rows[-1], :]
        vp_ref[...] = vc_ref[rows[-1], :]

    cur = lambda width: pl.BlockSpec((qb * BLOCK, width), lambda s: (s, 0))
    whole = lambda width: pl.BlockSpec((n_rows, width), lambda s: (0, 0))
    flat = lambda a: a.reshape(n_rows, a.shape[2])
    in_specs = [cur(HALF_WIDTH), cur(ck), cur(ck), cur(HALF_WIDTH), cur(STAT_WIDTH), cur(STAT_WIDTH)]
    args = [flat(a) for a in (q, k, v, d_o, lse, delta)]
    out_specs = [cur(HALF_WIDTH), whole(ck), whole(ck)]
    out_shape = [jax.ShapeDtypeStruct((n_rows, HALF_WIDTH), BF16),
                 jax.ShapeDtypeStruct((n_rows, ck), BF16), jax.ShapeDtypeStruct((n_rows, ck), BF16)]
    if has_sinks:
        in_specs.append(pl.BlockSpec(sink_rows.shape, lambda s: (0, 0)))
        args.append(sink_rows)
        out_specs.append(pl.BlockSpec(sink_rows.shape, lambda s: (0, 0)))
        out_shape.append(jax.ShapeDtypeStruct(sink_rows.shape, F32))
    outs = pl.pallas_call(
        body, name=name, grid=(n_steps,), in_specs=in_specs,
        out_specs=tuple(out_specs), out_shape=tuple(out_shape),
        scratch_shapes=[pltpu.VMEM((BLOCK, ck), F32), pltpu.VMEM((BLOCK, ck), F32),
                        pltpu.VMEM((BLOCK, ck), BF16), pltpu.VMEM((BLOCK, ck), BF16),
                        pltpu.VMEM((2, 2 * BLOCK, 2 * BLOCK), F32)],
        compiler_params=_params(("arbitrary",)),
    )(*args)
    return tuple(o.reshape(n_seq, length, o.shape[1]) for o in outs[:3]) + tuple(outs[3:])


def _tail(oa, ob1, lb1, ob4, lb4, ob16, lb16, gate_a, gate_b, x, target, w_out, spread, gather):
    tm = ROW_TILE

    def split_dot(v, mat):
        hi = v.astype(BF16)
        lo = (v - hi.astype(F32)).astype(BF16)
        return jnp.dot(hi, mat, preferred_element_type=F32) + jnp.dot(lo, mat, preferred_element_type=F32)

    def body(oa_ref, ob1_ref, lb1_ref, ob4_ref, lb4_ref, ob16_ref, lb16_ref, ga_ref, gb_ref, x_ref, t_ref, w_ref,
             sp_ref, ga_mat_ref,
             loss_ref, dy_ref, gwo_ref, doa_ref, dla_ref, dga_ref, dgb_ref,
             dob_ref, dob4_ref, dob16_ref, dlb_ref, dlb4_ref, dlb16_ref, lse_ref, lse4_ref, lse16_ref,
             s_f, mix_keep, dy_keep):
        i = pl.program_id(0)
        sp, gat = sp_ref[...], ga_mat_ref[...]
        o4, o16 = _unfold_load(ob4_ref, s_f, 4, tm), _unfold_load(ob16_ref, s_f, 16, tm)
        l4, l16 = _unfold_load(lb4_ref, s_f, 4, tm), _unfold_load(lb16_ref, s_f, 16, tm)
        o1, l1 = ob1_ref[...].astype(F32), lb1_ref[...]
        mx = jnp.maximum(jnp.maximum(l1, l4), l16)
        e1, e4, e16 = jnp.exp2(l1 - mx), jnp.exp2(l4 - mx), jnp.exp2(l16 - mx)
        den = e1 + e4 + e16
        inv = 1.0 / den
        ob = split_dot(e1 * inv, sp) * o1 + split_dot(e4 * inv, sp) * o4 + split_dot(e16 * inv, sp) * o16
        lse_b = mx + jnp.log2(den)

        oa, ga, gb = oa_ref[...].astype(F32), ga_ref[...].astype(F32), gb_ref[...].astype(F32)
        sa, sb = _sigmoid(ga), _sigmoid(gb)
        mixed = jnp.concatenate(_unpair_tiles(_tiles(oa * (ga * sa))) + [ob * (gb * sb)], axis=1)
        mixed_bf = mixed.astype(BF16)
        w = w_ref[...]
        yv = x_ref[...] + jnp.dot(mixed_bf, w, preferred_element_type=F32)
        err = yv - t_ref[...]
        sq = jnp.sum(err * err, axis=0, keepdims=True)
        dy = err * (1.0 / D_MODEL)
        dy_ref[...] = dy
        dy_bf = dy.astype(BF16)
        mix_t = mixed.T.astype(BF16)

        @pl.when(i == 0)
        def _():
            loss_ref[...] = sq

        @pl.when(i > 0)
        def _():
            loss_ref[...] += sq

        @pl.when((i & 1) == 0)
        def _():
            mix_keep[...] = mix_t
            dy_keep[...] = dy_bf

        @pl.when((i & 1) == 1)
        def _():
            gw = jnp.dot(jnp.concatenate([mix_keep[...], mix_t], axis=1), jnp.concatenate([dy_keep[...], dy_bf], axis=0),
                         preferred_element_type=F32)

            @pl.when(i == 1)
            def _():
                gwo_ref[...] = gw

            @pl.when(i > 1)
            def _():
                gwo_ref[...] += gw

        dmix = lax.dot_general(dy_bf, w, (((1,), (1,)), ((), ())), preferred_element_type=F32)
        dma = jnp.concatenate(_pair_tiles(_tiles(dmix[:, :HALF_WIDTH])), axis=1)
        dmb = dmix[:, HALF_WIDTH:]

        doa = dma * (ga * sa)
        doa_ref[...] = doa.astype(BF16)
        dla_ref[...] = split_dot(doa * oa, gat)
        dga_ref[...] = (dma * oa * (sa * (1.0 + ga * (1.0 - sa)))).astype(BF16)
        dob = dmb * (gb * sb)
        dgb_ref[...] = (dmb * ob * (sb * (1.0 + gb * (1.0 - sb)))).astype(BF16)
        dlb = split_dot(dob * ob, gat)
        dob_ref[...] = dob.astype(BF16)
        _fold_store(dob, s_f, dob4_ref, dob16_ref, tm)
        dlb_ref[...] = dlb
        _fold_store(dlb, s_f, dlb4_ref, dlb16_ref, tm)
        lse_ref[...] = lse_b
        _fold_store(lse_b, s_f, lse4_ref, lse16_ref, tm)

    row = lambda width: pl.BlockSpec((tm, width), lambda i: (i, 0))
    full = lambda a: pl.BlockSpec(a.shape, lambda i: (0,) * a.ndim)
    fb_shapes, fb_specs = _fold_specs(tm, BF16)
    _, ff_specs = _fold_specs(tm, F32)
    st_shapes, st_specs = _fold_specs(tm, F32, STAT_WIDTH)
    nat = lambda dtype, width=HALF_WIDTH: jax.ShapeDtypeStruct((SEQ, width), dtype)
    return pl.pallas_call(
        body, name="tail", grid=(SEQ // tm,),
        in_specs=[row(HALF_WIDTH), row(HALF_WIDTH), row(STAT_WIDTH), ff_specs[0], st_specs[0], ff_specs[1], st_specs[1],
                  row(HALF_WIDTH), row(HALF_WIDTH), row(D_MODEL), row(D_MODEL), full(w_out), full(spread), full(gather)],
        out_specs=(pl.BlockSpec((1, D_MODEL), lambda i: (0, 0)), row(D_MODEL),
                   pl.BlockSpec((D_MODEL, D_MODEL), lambda i: (0, 0)),
                   row(HALF_WIDTH), row(STAT_WIDTH), row(HALF_WIDTH), row(HALF_WIDTH),
                   row(HALF_WIDTH), *fb_specs, row(STAT_WIDTH), *st_specs, row(STAT_WIDTH), *st_specs),
        out_shape=(jax.ShapeDtypeStruct((1, D_MODEL), F32), jax.ShapeDtypeStruct((SEQ, D_MODEL), F32),
                   jax.ShapeDtypeStruct((D_MODEL, D_MODEL), F32),
                   nat(BF16), nat(F32, STAT_WIDTH), nat(BF16), nat(BF16),
                   nat(BF16), *fb_shapes, nat(F32, STAT_WIDTH), *st_shapes, nat(F32, STAT_WIDTH), *st_shapes),
        scratch_shapes=[_fold_scratch(tm), pltpu.VMEM((D_MODEL, tm), BF16), pltpu.VMEM((tm, D_MODEL), BF16)],
        compiler_params=_params(("arbitrary",)),
    )(oa, ob1, lb1, ob4, lb4, ob16, lb16, gate_a, gate_b, x, target, w_out, spread, gather)


def _dproj_assemble(dqa, dka, dva, dga, dgb, dq1, dk1, dv1, dq4, dk4, dv4, dq16, dk16, dv16, tqa, tqb, tkb, tka,
                    qkg, cos4, sin4, bmean):
    tm = ROW_TILE

    def norm_rope_bwd(d_out, t, g, cos, sin, bm, scale):
        d_r = d_out * scale
        dyv = d_r * cos + _swap_halves(d_r * sin)
        rr = lax.rsqrt(_head_sum(t * t, bm) + EPS)
        that = t * rr
        dgain = jnp.sum(dyv * that, axis=0, keepdims=True)
        gdy = dyv * g
        dt = rr * (gdy - that * _head_sum(that * gdy, bm))
        return dt, dgain

    def body(dqa_ref, dka_ref, dva_ref, dga_ref, dgb_ref, dq1_ref, dk1_ref, dv1_ref, dq4_ref, dk4_ref, dv4_ref,
             dq16_ref, dk16_ref, dv16_ref, tqa_ref, tqb_ref, tkb_ref, tka_ref, qkg_ref, cos_ref, sin_ref, bm_ref,
             dproj_ref, dqkg_ref, s_f):
        i = pl.program_id(0)
        cos, sin, bm = cos_ref[...], sin_ref[...], bm_ref[...]

        def merged(nat_ref, f4_ref, f16_ref):
            return nat_ref[...].astype(F32) + _unfold_load(f4_ref, s_f, 4, tm) + _unfold_load(f16_ref, s_f, 16, tm)

        @pl.when(i == 0)
        def _():
            dqkg_ref[...] = jnp.zeros_like(dqkg_ref)

        def through(d_out, t, row, scale, c0, paired=False):
            g = qkg_ref[row:row + 1, :]
            tot = jnp.zeros((1, PAIR), F32)
            dts = []
            for j in range(d_out.shape[1] // PAIR):
                cols = slice(j * PAIR, (j + 1) * PAIR)
                dt, dg = norm_rope_bwd(d_out[:, cols], t[:, cols], g, cos, sin, bm, scale)
                dts.append(dt)
                tot = tot + dg
            if paired:
                dts = _unpair_tiles(dts)
            for j, dt in enumerate(dts):
                dproj_ref[:, c0 + j * PAIR:c0 + (j + 1) * PAIR] = dt.astype(BF16)
            dqkg_ref[row:row + 1, :] += tot

        through(dqa_ref[...].astype(F32), tqa_ref[...].astype(F32), 0, HEAD_DIM ** -0.5, C_QA, paired=True)
        through(dka_ref[...].astype(F32), tka_ref[...].astype(F32), 1, LN2, C_KA)
        through(merged(dq1_ref, dq4_ref, dq16_ref), tqb_ref[...].astype(F32), 2, HEAD_DIM ** -0.5, C_QB)
        through(merged(dk1_ref, dk4_ref, dk16_ref), tkb_ref[...].astype(F32), 3, LN2, C_KB)
        dproj_ref[:, C_VB:C_VB + HALF_WIDTH] = merged(dv1_ref, dv4_ref, dv16_ref).astype(BF16)
        dproj_ref[:, C_GA:C_GA + HALF_WIDTH] = jnp.concatenate(
            _unpair_tiles(_tiles(dga_ref[...].astype(F32))), axis=1).astype(BF16)
        dproj_ref[:, C_GB:C_GB + HALF_WIDTH] = dgb_ref[...].astype(BF16)
        dproj_ref[:, C_VA:C_VA + KV_A_WIDTH] = dva_ref[...].astype(BF16)

    row = lambda width: pl.BlockSpec((tm, width), lambda i: (i, 0))
    full = lambda a: pl.BlockSpec(a.shape, lambda i: (0,) * a.ndim)
    _, ff_specs = _fold_specs(tm, F32)
    return pl.pallas_call(
        body, name="dproj_assemble", grid=(SEQ // tm,),
        in_specs=[row(HALF_WIDTH), row(KV_A_WIDTH), row(KV_A_WIDTH), row(HALF_WIDTH), row(HALF_WIDTH),
                  row(HALF_WIDTH), row(HALF_WIDTH), row(HALF_WIDTH), ff_specs[0], ff_specs[0], ff_specs[0],
                  ff_specs[1], ff_specs[1], ff_specs[1],
                  row(HALF_WIDTH), row(HALF_WIDTH), row(HALF_WIDTH), row(KV_A_WIDTH),
                  full(qkg), row(PAIR), row(PAIR), full(bmean)],
        out_specs=(row(IN_WIDTH), pl.BlockSpec((SMALL_ROWS, PAIR), lambda i: (0, 0))),
        out_shape=(jax.ShapeDtypeStruct((SEQ, IN_WIDTH), BF16), jax.ShapeDtypeStruct((SMALL_ROWS, PAIR), F32)),
        scratch_shapes=[_fold_scratch(tm)],
        compiler_params=_params(("arbitrary",)),
    )(dqa, dka, dva, dga, dgb, dq1, dk1, dv1, dq4, dk4, dv4, dq16, dk16, dv16, tqa, tqb, tkb, tka, qkg, cos4, sin4, bmean)


def _input_grad_reduce(dproj, w, x, gain, dy, blocks_in, blocks_out, small):
    tm = ROW_TILE
    n_steps = SEQ // tm
    stage2_step, stage3_step = 3, 8
    shapes = (blocks_in.shape[1:], blocks_out.shape[1:])

    def body(dp_ref, w_ref, x_ref, g_ref, dy_ref, ga_hbm, gb_hbm, small_ref,
             gx_ref, out_a, out_b, small_out_ref, dgain_out_ref,
             part_a, part_b, sib_a, sib_b, wire_a, wire_b, chips_a, chips_b, small_all, dgain_acc, dgain_all,
             load_sems, sib_send, sib_recv, chip_send, chip_recv, small_send, small_recv, dgain_send, dgain_recv):
        i = pl.program_id(0)
        x, y, c = lax.axis_index("x"), lax.axis_index("y"), lax.axis_index("c")
        sibling = (x, y, 1 - c)
        chips = [(x, y), (1 - x, y), (x, 1 - y), (1 - x, 1 - y)]
        my_id = 4 * x + 2 * y + c
        g_hbm, part, from_sib = (ga_hbm, gb_hbm), (part_a, part_b), (sib_a, sib_b)
        to_wire, from_chips, out = (wire_a, wire_b), (chips_a, chips_b), (out_a, out_b)
        both = (0, 1)

        def blk(a, chip, core):
            return g_hbm[a].at[4 * chip[0] + 2 * chip[1] + core]

        def to_all(src, dst_all, send, recv):
            copies = []
            for rel in range(1, N_DEV):
                dx, dy_, dc = (rel >> 2) & 1, (rel >> 1) & 1, rel & 1
                to = (1 - x if dx else x, 1 - y if dy_ else y, 1 - c if dc else c)
                copies.append(pltpu.make_async_remote_copy(
                    src_ref=src, dst_ref=dst_all.at[my_id], send_sem=send.at[rel - 1], recv_sem=recv.at[rel - 1],
                    device_id=to, device_id_type=MESH))
            return copies

        small_copies = to_all(small_all.at[my_id], small_all, small_send, small_recv)
        dgain_copies = to_all(dgain_acc, dgain_all, dgain_send, dgain_recv)
        loads = [[pltpu.make_async_copy(blk(a, chips[k], c), part[a].at[k], load_sems.at[a, k]) for k in range(4)] for a in both]
        to_sib = [[pltpu.make_async_remote_copy(
            src_ref=blk(a, chips[k], 1 - c), dst_ref=from_sib[a].at[k], send_sem=sib_send.at[a, k], recv_sem=sib_recv.at[a, k],
            device_id=sibling, device_id_type=MESH) for k in range(4)] for a in both]
        first, other, k_first, k_other = _routes()
        to_chips = [[pltpu.make_async_remote_copy(
            src_ref=to_wire[a].at[s], dst_ref=from_chips[a].at[s], send_sem=chip_send.at[a, s], recv_sem=chip_recv.at[a, s],
            device_id=(first, first, other)[s], device_id_type=MESH) for s in range(3)] for a in both]

        def chip_partial(a, k):
            return part[a][k].astype(F32) + from_sib[a][k].astype(F32)

        @pl.when(i == 0)
        def _():
            small_all[my_id] = small_ref[...]
            for cp in small_copies:
                cp.start()
            for k in (1, 2, 3, 0):
                for a in both:
                    loads[a][k].start()
                    to_sib[a][k].start()

        @pl.when(i == stage2_step)
        def _():
            for k in (1, 2, 3):
                for a in both:
                    loads[a][k].wait()
                    to_sib[a][k].wait_recv()
            for s, k in ((0, 3), (1, k_first)):
                for a in both:
                    to_wire[a][s] = chip_partial(a, k).astype(BF16)
                    to_chips[a][s].start()

        @pl.when(i == stage3_step)
        def _():
            for a in both:
                to_chips[a][0].wait_recv()
                to_wire[a][2] = (chip_partial(a, k_other) + from_chips[a][0].astype(F32)).astype(BF16)
                to_chips[a][2].start()

        dh = lax.dot_general(dp_ref[...], w_ref[...], (((1,), (1,)), ((), ())), preferred_element_type=F32)
        xf = x_ref[...]
        r = lax.rsqrt(jnp.mean(xf * xf, axis=-1, keepdims=True) + EPS)
        xhat = xf * r
        dg = jnp.sum(dh * xhat, axis=0, keepdims=True)
        dxh = dh * g_ref[...]
        dx = r * (dxh - xhat * jnp.mean(dxh * xhat, axis=-1, keepdims=True))
        gx_ref[...] = dy_ref[...] + dx

        @pl.when(i == 0)
        def _():
            dgain_acc[...] = dg

        @pl.when(i > 0)
        def _():
            dgain_acc[...] += dg

        @pl.when(i == n_steps - 1)
        def _():
            dgain_all[my_id] = dgain_acc[...]
            for cp in dgain_copies:
                cp.start()
            for a in both:
                loads[a][0].wait()
                to_sib[a][0].wait_recv()
                acc = chip_partial(a, 0)
                for s in (1, 2):
                    to_chips[a][s].wait_recv()
                    acc = acc + from_chips[a][s].astype(F32)
                out[a][...] = acc
            for copies, gathered, dst in ((small_copies, small_all, small_out_ref), (dgain_copies, dgain_all, dgain_out_ref)):
                for cp in copies:
                    cp.wait_recv()
                tot = gathered[0]
                for d in range(1, N_DEV):
                    tot = tot + gathered[d]
                dst[...] = tot
            for cp in to_sib[0] + to_sib[1] + to_chips[0] + to_chips[1] + small_copies + dgain_copies:
                cp.wait_send()

    row = lambda width: pl.BlockSpec((tm, width), lambda i: (i, 0))
    full = lambda a: pl.BlockSpec(a.shape, lambda i: (0,) * a.ndim)
    whole = lambda shape: pl.BlockSpec(shape, lambda i: (0,) * len(shape))
    hbm = pl.BlockSpec(memory_space=pl.ANY)
    dtypes = (blocks_in.dtype, blocks_out.dtype)
    buf = lambda n, dts: [pltpu.VMEM((n,) + s, dt) for s, dt in zip(shapes, dts)]
    return pl.pallas_call(
        body, name="input_grad_rs", grid=(n_steps,),
        in_specs=[row(IN_WIDTH), full(w), row(D_MODEL), full(gain), row(D_MODEL), hbm, hbm, full(small)],
        out_specs=(row(D_MODEL), whole(shapes[0]), whole(shapes[1]), whole((SMALL_ROWS, SMALL_COLS)), whole((1, D_MODEL))),
        out_shape=(jax.ShapeDtypeStruct((SEQ, D_MODEL), F32), jax.ShapeDtypeStruct(shapes[0], F32),
                   jax.ShapeDtypeStruct(shapes[1], F32), jax.ShapeDtypeStruct((SMALL_ROWS, SMALL_COLS), F32),
                   jax.ShapeDtypeStruct((1, D_MODEL), F32)),
        scratch_shapes=[*buf(4, dtypes), *buf(4, dtypes), *buf(3, (BF16, BF16)), *buf(3, (BF16, BF16)),
                        pltpu.VMEM((N_DEV, SMALL_ROWS, SMALL_COLS), F32),
                        pltpu.VMEM((1, D_MODEL), F32), pltpu.VMEM((N_DEV, 1, D_MODEL), F32),
                        pltpu.SemaphoreType.DMA((2, 4)), pltpu.SemaphoreType.DMA((2, 4)), pltpu.SemaphoreType.DMA((2, 4)),
                        pltpu.SemaphoreType.DMA((2, 3)), pltpu.SemaphoreType.DMA((2, 3)),
                        pltpu.SemaphoreType.DMA((7,)), pltpu.SemaphoreType.DMA((7,)),
                        pltpu.SemaphoreType.DMA((7,)), pltpu.SemaphoreType.DMA((7,))],
        compiler_params=_params(("arbitrary",)),
    )(dproj, w, x, gain, dy, blocks_in, blocks_out, small)


def _weight_grad(h_t, dproj):
    tk = 1024
    cb = IN_WIDTH // 2
    n_k = SEQ // tk

    def body(ht_ref, dp_ref, out_ref, acc):
        k = pl.program_id(1)
        upd = jnp.dot(ht_ref[...], dp_ref[...], preferred_element_type=F32)

        @pl.when(k == 0)
        def _():
            acc[...] = upd

        @pl.when(k > 0)
        def _():
            acc[...] += upd

        @pl.when(k == n_k - 1)
        def _():
            for b in range(N_DEV // 2):
                out_ref[b] = acc[:, b * SHARD_IN:(b + 1) * SHARD_IN].astype(BF16)

    return pl.pallas_call(
        body, name="weight_grad", grid=(2, n_k),
        in_specs=[pl.BlockSpec((D_MODEL, tk), lambda j, k: (0, k)), pl.BlockSpec((tk, cb), lambda j, k: (k, j))],
        out_specs=pl.BlockSpec((N_DEV // 2, D_MODEL, SHARD_IN), lambda j, k: (j, 0, 0)),
        out_shape=jax.ShapeDtypeStruct((N_DEV, D_MODEL, SHARD_IN), BF16),
        scratch_shapes=[pltpu.VMEM((D_MODEL, cb), F32)],
        compiler_params=_params(("arbitrary", "arbitrary")),
    )(h_t, dproj)


def _adamw(name, w, g, m, v):
    def body(w_ref, g_ref, m_ref, v_ref, d_ref, nm_ref, nv_ref):
        gv = g_ref[...]
        nm = ADAM_B1 * m_ref[...] + (1.0 - ADAM_B1) * gv
        nv = ADAM_B2 * v_ref[...] + (1.0 - ADAM_B2) * jnp.square(gv)
        m_hat = nm / (1.0 - ADAM_B1 ** ADAM_STEP)
        v_hat = nv / (1.0 - ADAM_B2 ** ADAM_STEP)
        d_ref[...] = -ADAM_LR * (m_hat / (jnp.sqrt(v_hat) + ADAM_EPS) + ADAM_WD * w_ref[...])
        nm_ref[...] = nm
        nv_ref[...] = nv

    vmem = pl.BlockSpec(memory_space=pltpu.VMEM)
    out = jax.ShapeDtypeStruct(w.shape, F32)
    return pl.pallas_call(
        body, name=name, in_specs=[vmem] * 4, out_specs=(vmem,) * 3, out_shape=(out,) * 3,
        compiler_params=pltpu.CompilerParams(vmem_limit_bytes=VMEM_LIMIT),
    )(w, g, m, v)


SMALL_USED = D_MODEL + 4 * HEAD_DIM + 8


def _pack_small(norm_gain, qa, ka, sinks, qb, kb, extra=None):
    parts = [norm_gain.reshape(-1), qa.reshape(-1), ka.reshape(-1), sinks.reshape(-1), qb.reshape(-1), kb.reshape(-1)]
    if extra is not None:
        parts.append(extra.reshape(-1))
    flat = jnp.concatenate(parts)
    flat = jnp.pad(flat, (0, SMALL_ROWS * SMALL_COLS - flat.shape[0]))
    return flat.reshape(SMALL_ROWS, SMALL_COLS)


def _unpack_small(a):
    flat = a.reshape(-1)
    sizes = (D_MODEL, HEAD_DIM, HEAD_DIM, 8, HEAD_DIM, HEAD_DIM)
    out, off = [], 0
    for s in sizes:
        out.append(flat[off:off + s].reshape(1, s))
        off += s
    return out


def _fold_heads(row):
    return row[0, :HEAD_DIM] + row[0, HEAD_DIM:]


def kernel(x, norm_gain, w_in, q_norm_a, k_norm_a, sinks_a, q_norm_b, k_norm_b, w_out, loss_target, m_norm_gain, m_w_in, m_q_norm_a, m_k_norm_a, m_sinks_a, m_q_norm_b, m_k_norm_b, m_w_out, v_norm_gain, v_w_in, v_q_norm_a, v_k_norm_a, v_sinks_a, v_q_norm_b, v_k_norm_b, v_w_out):
    x2, tgt = x[0], loss_target[0]
    w_in_sh, w_out_sh = w_in[0], w_out[0]

    w_full = _all_gather_w_in(w_in_sh)

    inv = np.float32(ROPE_THETA) ** (-np.arange(HEAD_DIM // 2, dtype=np.float32) / np.float32(HEAD_DIM // 2))
    ang = np.arange(SEQ, dtype=np.float32)[:, None] * inv[None, :].astype(np.float32)
    cos, sin = np.cos(ang).astype(np.float32), np.sin(ang).astype(np.float32)
    cos4 = jnp.asarray(np.concatenate([cos, cos, cos, cos], axis=1))
    sin4 = jnp.asarray(np.concatenate([-sin, sin, -sin, sin], axis=1))
    blockdiag = np.kron(np.eye(2, dtype=np.float32), np.ones((HEAD_DIM, HEAD_DIM), np.float32))
    bmean = jnp.asarray(blockdiag / HEAD_DIM, dtype=BF16)
    gather_np = np.kron(np.eye(2 * N_PAIRS, dtype=np.float32), np.ones((HEAD_DIM, STAT_REP), np.float32))
    spread_np = np.kron(np.eye(2 * N_PAIRS, dtype=np.float32), np.ones((STAT_REP, HEAD_DIM), np.float32))
    spread_np[np.arange(STAT_WIDTH) % STAT_REP != 0] = 0.0
    gather, spread = jnp.asarray(gather_np, dtype=BF16), jnp.asarray(spread_np, dtype=BF16)
    two = lambda g: jnp.concatenate([g, g], axis=1)
    qkg = jnp.concatenate([two(q_norm_a), two(k_norm_a), two(q_norm_b), two(k_norm_b),
                           jnp.zeros((SMALL_ROWS - 4, PAIR), F32)], axis=0)
    sinks_paired = jnp.stack([sinks_a[0, :N_PAIRS], sinks_a[0, N_PAIRS:]], axis=1) * LOG2E
    sink_rows = jnp.concatenate([jnp.repeat(sinks_paired, BLOCK, axis=1),
                                 jnp.zeros((SMALL_ROWS - N_PAIRS, 2 * BLOCK), F32)], axis=0)

    (tqa, tka, tqb, tkb, gate_a, gate_b, h_t, qa, ka, va, qb, kb, vb, qb4, qb16, kb4, kb16, vb4, vb16,
     gathered_out) = _proj_fwd(x2, norm_gain, w_full, qkg, cos4, sin4, bmean, w_out_sh)
    wo_full = gathered_out.reshape(D_MODEL, D_MODEL)
    (oa, la), (ob1, lb1), (ob4, lb4), (ob16, lb16) = _attn_fwd("attn_fwd", [
        (qa[None], ka[None], va[None], sink_rows, BLOCK - 1), (qb[None], kb[None], vb[None], None, BLOCK),
        (qb4, kb4, vb4, None, BLOCK), (qb16, kb16, vb16, None, BLOCK)])
    (loss_cols, dy, gwo, doa, dla, dga, dgb, dob, dob4, dob16, dlb, dlb4, dlb16, lse_b, lse4, lse16) = _tail(
        oa[0], ob1[0], lb1[0], ob4, lb4, ob16, lb16, gate_a, gate_b, x2, tgt, wo_full, spread, gather)

    dqa, dka, dva, dsink = _attn_bwd("attn_a_bwd", qa[None], ka[None], va[None], doa[None], la, dla[None], sink_rows, BLOCK - 1)
    dq1, dk1, dv1 = _attn_bwd("attn_b1_bwd", qb[None], kb[None], vb[None], dob[None], lse_b[None], dlb[None], None, BLOCK)
    dq4, dk4, dv4 = _attn_bwd("attn_b4_bwd", qb4, kb4, vb4, dob4, lse4, dlb4, None, BLOCK)
    dq16, dk16, dv16 = _attn_bwd("attn_b16_bwd", qb16, kb16, vb16, dob16, lse16, dlb16, None, BLOCK)
    dproj, dqkg = _dproj_assemble(dqa[0], dka[0], dva[0], dga, dgb, dq1[0], dk1[0], dv1[0], dq4, dk4, dv4,
                                  dq16, dk16, dv16, tqa, tqb, tkb, tka, qkg, cos4, sin4, bmean)
    gw_in = _weight_grad(h_t, dproj)

    blocks_in = gw_in
    blocks_out = gwo.reshape(N_DEV, SHARD_OUT, D_MODEL)
    g_sinks = jnp.concatenate([jnp.sum(dsink[:N_PAIRS, :BLOCK], axis=1), jnp.sum(dsink[:N_PAIRS, BLOCK:], axis=1)])
    small = _pack_small(jnp.zeros((D_MODEL,), F32), _fold_heads(dqkg[0:1]), _fold_heads(dqkg[1:2]), g_sinks,
                        _fold_heads(dqkg[2:3]), _fold_heads(dqkg[3:4]), extra=0.5 * jnp.sum(loss_cols) / D_MODEL)
    grad_x, g_w_in, g_w_out, small_red, dgain_red = _input_grad_reduce(
        dproj, w_full, x2, norm_gain, dy, blocks_in, blocks_out, small)
    n_gain_rows = D_MODEL // SMALL_COLS
    small_red = jnp.concatenate([dgain_red.reshape(n_gain_rows, SMALL_COLS), small_red[n_gain_rows:]], axis=0)
    g_small = _unpack_small(small_red)

    d_in, nm_in, nv_in = _adamw("adamw_w_in", w_in_sh, g_w_in, m_w_in[0], v_w_in[0])
    d_out, nm_out, nv_out = _adamw("adamw_w_out", w_out_sh, g_w_out, m_w_out[0], v_w_out[0])
    d_s, nm_s, nv_s = _adamw(
        "adamw_small",
        _pack_small(norm_gain, q_norm_a, k_norm_a, sinks_a, q_norm_b, k_norm_b), small_red,
        _pack_small(m_norm_gain, m_q_norm_a, m_k_norm_a, m_sinks_a, m_q_norm_b, m_k_norm_b),
        _pack_small(v_norm_gain, v_q_norm_a, v_k_norm_a, v_sinks_a, v_q_norm_b, v_k_norm_b))
    d_small, nm_small, nv_small = _unpack_small(d_s), _unpack_small(nm_s), _unpack_small(nv_s)

    loss = small_red.reshape(-1)[SMALL_USED]

    def assemble(small_list, big_in, big_out):
        ng, qa_, ka_, sk_, qb_, kb_ = small_list
        return [ng, big_in[None], qa_, ka_, sk_, qb_, kb_, big_out[None]]

    return (loss, grad_x[None], *assemble(g_small, g_w_in, g_w_out), *assemble(d_small, d_in, d_out),
            *assemble(nm_small, nm_in, nm_out), *assemble(nv_small, nv_in, nv_out))
```

```python
import numpy as np
import jax
import jax.numpy as jnp
from jax import lax
from jax.experimental import pallas as pl
from jax.experimental.pallas import tpu as pltpu

F32 = jnp.float32
BF16 = jnp.bfloat16

SEQ = 4096
D_MODEL = 1024
HEAD_DIM = 64
PAIR = 2 * HEAD_DIM
N_PAIRS = 4
HALF_WIDTH = N_PAIRS * PAIR
KV_A_WIDTH = 128
IN_WIDTH = 3328
BLOCK = 128
STAT_REP = 16
STAT_WIDTH = 128
EPS = 1e-6
NEG = -1e30
ROPE_THETA = 10000.0
LOG2E = 1.4426950408889634
LN2 = 0.6931471805599453
Q_SCALE = HEAD_DIM ** -0.5 * LOG2E
N_DEV = 8
SHARD_IN = IN_WIDTH // N_DEV
SHARD_OUT = D_MODEL // N_DEV
SMALL_ROWS, SMALL_COLS = 8, 256

C_QA, C_KA, C_VA, C_GA, C_QB, C_KB, C_VB, C_GB = 0, 512, 640, 768, 1280, 1792, 2304, 2816

ADAM_LR = 0.001
ADAM_B1 = 0.9
ADAM_B2 = 0.999
ADAM_EPS = 1e-08
ADAM_WD = 0.01
ADAM_STEP = 10

ROW_TILE = 256
PROJ_ROW_TILE = 512
FWD_BLOCKS_PER_STEP = 8
BWD_BLOCKS_PER_STEP = 8
VMEM_LIMIT = 56 * 1024 * 1024

MESH = pl.DeviceIdType.MESH


def _params(sem, vmem=VMEM_LIMIT):
    return pltpu.CompilerParams(dimension_semantics=sem, vmem_limit_bytes=vmem)


def _head_sum(v, bm):
    hi = v.astype(BF16)
    lo = (v - hi.astype(F32)).astype(BF16)
    return (jnp.dot(hi, bm, preferred_element_type=F32) + jnp.dot(lo, bm, preferred_element_type=F32))


def _swap_halves(y):
    lane = lax.broadcasted_iota(jnp.int32, y.shape, 1)
    first = (lane & 32) == 0
    return jnp.where(first, pltpu.roll(y, 96, 1), pltpu.roll(y, 32, 1))


def _sigmoid(g):
    return 1.0 / (1.0 + jnp.exp(-g))


def _tiles(a):
    return [a[:, j * PAIR:(j + 1) * PAIR] for j in range(N_PAIRS)]


def _pair_tiles(t):
    low = lax.broadcasted_iota(jnp.int32, t[0].shape, 1) < HEAD_DIM
    r = [pltpu.roll(a, HEAD_DIM, 1) for a in t]
    return [jnp.where(low, t[0], r[2]), jnp.where(low, r[0], t[2]), jnp.where(low, t[1], r[3]), jnp.where(low, r[1], t[3])]


def _unpair_tiles(p):
    low = lax.broadcasted_iota(jnp.int32, p[0].shape, 1) < HEAD_DIM
    r = [pltpu.roll(a, HEAD_DIM, 1) for a in p]
    return [jnp.where(low, p[0], r[1]), jnp.where(low, p[2], r[3]), jnp.where(low, r[0], p[1]), jnp.where(low, r[2], p[3])]


def _routes():
    x, y, c = lax.axis_index("x"), lax.axis_index("y"), lax.axis_index("c")
    north = c == 1
    first = (jnp.where(north, 1 - x, x), jnp.where(north, y, 1 - y), c)
    other = (jnp.where(north, x, 1 - x), jnp.where(north, 1 - y, y), c)
    k_first = jnp.where(north, 1, 2)
    return first, other, k_first, 3 - k_first


def _gather_plan(mine_ref, out_ref, send_sems, recv_sems):
    x, y, c = lax.axis_index("x"), lax.axis_index("y"), lax.axis_index("c")
    me, sibling, diag = (x, y, c), (x, y, 1 - c), (1 - x, 1 - y, c)
    first, other, k_first, k_other = _routes()

    def slot(px, py, pc):
        return out_ref.at[4 * px + 2 * py + pc]

    def copy(k, block, to, from_mine=False):
        return pltpu.make_async_remote_copy(
            src_ref=mine_ref if from_mine else slot(*block), dst_ref=slot(*block),
            send_sem=send_sems.at[k], recv_sem=recv_sems.at[k], device_id=to, device_id_type=MESH)

    sends = [copy(0, me, sibling, True), copy(1, me, (1 - x, y, c), True), copy(2, me, (x, 1 - y, c), True)]
    stages = [(copy(k_first, first, me), [copy(3, first, other), copy(3 + k_first, first, sibling)]),
              (copy(k_other, other, me), [copy(3 + k_other, other, sibling)]),
              (copy(3, diag, me), [copy(6, diag, sibling)])]
    from_sibling = [copy(0, sibling, me), copy(4, (1 - x, y, 1 - c), me), copy(5, (x, 1 - y, 1 - c), me),
                    copy(6, (1 - x, 1 - y, 1 - c), me)]
    return slot(*me), sends, stages, from_sibling


GATHER_SCRATCH = [pltpu.SemaphoreType.DMA((7,)), pltpu.SemaphoreType.DMA((7,))]


def _all_gather_w_in(w_in_sh):
    rows, cols = w_in_sh.shape

    def body(w_ref, out_ref, mine_ref, blocks, send_sems, recv_sems):
        mine_ref[...] = w_ref[...].astype(BF16)
        my_slot, sends, stages, from_sibling = _gather_plan(mine_ref, blocks, send_sems, recv_sems)
        for cp in sends:
            cp.start()
        my_slot[...] = mine_ref[...]
        for arrival, forwards in stages:
            arrival.wait_recv()
            for cp in forwards:
                cp.start()
        for arrival in from_sibling:
            arrival.wait_recv()
        for cp in sends + [cp for _, forwards in stages for cp in forwards]:
            cp.wait_send()
        for d in range(N_DEV):
            out_ref[:, d * cols:(d + 1) * cols] = blocks[d]

    vmem = pl.BlockSpec(memory_space=pltpu.VMEM)
    return pl.pallas_call(
        body, name="ag_w_in",
        out_shape=jax.ShapeDtypeStruct((rows, N_DEV * cols), BF16),
        in_specs=[vmem], out_specs=vmem,
        scratch_shapes=[pltpu.VMEM((rows, cols), BF16), pltpu.VMEM((N_DEV, rows, cols), BF16)] + GATHER_SCRATCH,
        compiler_params=pltpu.CompilerParams(vmem_limit_bytes=VMEM_LIMIT),
    )(w_in_sh)


def _fold_scratch(tm):
    return pltpu.VMEM((N_PAIRS, tm, PAIR), F32)


def _fold_store(val, scr, out4, out16, tm):
    groups = range(val.shape[1] // PAIR)
    for j in groups:
        scr[j] = val[:, j * PAIR:(j + 1) * PAIR]
    for dil, out in ((4, out4), (16, out16)):
        for r in range(dil):
            for j in groups:
                out[r, :, j * PAIR:(j + 1) * PAIR] = scr[j, pl.ds(r, tm // dil, stride=dil), :].astype(out.dtype)


def _unfold_load(src, scr, dil, tm):
    groups = range(src.shape[2] // PAIR)
    for r in range(dil):
        for j in groups:
            scr[j, pl.ds(r, tm // dil, stride=dil), :] = src[r, :, j * PAIR:(j + 1) * PAIR].astype(F32)
    return jnp.concatenate([scr[j] for j in groups], axis=1)


def _fold_specs(tm, dtype, width=HALF_WIDTH):
    shapes = (jax.ShapeDtypeStruct((4, SEQ // 4, width), dtype), jax.ShapeDtypeStruct((16, SEQ // 16, width), dtype))
    specs = (pl.BlockSpec((4, tm // 4, width), lambda i: (0, i, 0)),
             pl.BlockSpec((16, tm // 16, width), lambda i: (0, i, 0)))
    return shapes, specs


def _proj_fwd(x, gain, w, qkg, cos4, sin4, bmean, w_out_sh):
    tm = PROJ_ROW_TILE
    n_steps = SEQ // tm

    def norm_rope(t, g, cos, sin, bm, scale):
        rr = lax.rsqrt(_head_sum(t * t, bm) + EPS)
        yv = t * rr * g
        return (yv * cos + _swap_halves(yv) * sin) * scale

    def body(x_ref, g_ref, w_ref, qkg_ref, cos_ref, sin_ref, bm_ref, wo_ref,
             tqa_ref, tka_ref, tqb_ref, tkb_ref, ga_ref, gb_ref, ht_ref, qa_ref, ka_ref, va_ref, qb_ref, kb_ref, vb_ref,
             qb4_ref, qb16_ref, kb4_ref, kb16_ref, vb4_ref, vb16_ref, wo_all_ref,
             proj, scr, wo_mine, wo_all, send_sems, recv_sems):
        i = pl.program_id(0)
        my_slot, sends, stages, from_sibling = _gather_plan(wo_mine, wo_all, send_sems, recv_sems)

        @pl.when(i == 0)
        def _():
            wo_mine[...] = wo_ref[...].astype(BF16)
            for cp in sends:
                cp.start()
            my_slot[...] = wo_mine[...]

        @pl.when(i == n_steps // 2)
        def _():
            for arrival, forwards in stages[:2]:
                arrival.wait_recv()
                for cp in forwards:
                    cp.start()

        xf = x_ref[...]
        r = lax.rsqrt(jnp.mean(xf * xf, axis=-1, keepdims=True) + EPS)
        hf = xf * r * g_ref[...]
        ht_ref[...] = hf.T.astype(BF16)
        cos, sin, bm = cos_ref[...], sin_ref[...], bm_ref[...]
        proj[...] = jnp.dot(hf.astype(BF16), w_ref[...], preferred_element_type=F32)

        def roped(tiles, row, scale):
            g = qkg_ref[row:row + 1, :]
            return jnp.concatenate([norm_rope(t, g, cos, sin, bm, scale) for t in tiles], axis=1)

        tqa = _pair_tiles(_tiles(proj[:, C_QA:C_QA + HALF_WIDTH]))
        tqa_ref[...] = jnp.concatenate(tqa, axis=1).astype(BF16)
        qa_ref[...] = roped(tqa, 0, Q_SCALE).astype(BF16)
        ga_ref[...] = jnp.concatenate(_pair_tiles(_tiles(proj[:, C_GA:C_GA + HALF_WIDTH])), axis=1).astype(BF16)
        gb_ref[...] = proj[:, C_GB:C_GB + HALF_WIDTH].astype(BF16)
        tqb = proj[:, C_QB:C_QB + HALF_WIDTH]
        tqb_ref[...] = tqb.astype(BF16)
        qb = roped(_tiles(tqb), 2, Q_SCALE)
        qb_ref[...] = qb.astype(BF16)
        _fold_store(qb, scr, qb4_ref, qb16_ref, tm)
        tkb = proj[:, C_KB:C_KB + HALF_WIDTH]
        tkb_ref[...] = tkb.astype(BF16)
        kb = roped(_tiles(tkb), 3, 1.0)
        kb_ref[...] = kb.astype(BF16)
        _fold_store(kb, scr, kb4_ref, kb16_ref, tm)
        vb = proj[:, C_VB:C_VB + HALF_WIDTH]
        vb_ref[...] = vb.astype(BF16)
        _fold_store(vb, scr, vb4_ref, vb16_ref, tm)
        tka = proj[:, C_KA:C_KA + KV_A_WIDTH]
        tka_ref[...] = tka.astype(BF16)
        ka_ref[...] = roped([tka], 1, 1.0).astype(BF16)
        va_ref[...] = proj[:, C_VA:C_VA + KV_A_WIDTH].astype(BF16)

        @pl.when(i == n_steps - 1)
        def _():
            arrival, forwards = stages[2]
            arrival.wait_recv()
            for cp in forwards:
                cp.start()
            for arrival in from_sibling:
                arrival.wait_recv()
            for cp in sends + [cp for _, forwards in stages for cp in forwards]:
                cp.wait_send()
            wo_all_ref[...] = wo_all[...]

    row = lambda width: pl.BlockSpec((tm, width), lambda i: (i, 0))
    full = lambda a: pl.BlockSpec(a.shape, lambda i: (0,) * a.ndim)
    nat = lambda width, dtype=BF16: jax.ShapeDtypeStruct((SEQ, width), dtype)
    f_shapes, f_specs = _fold_specs(tm, BF16)
    return pl.pallas_call(
        body, name="proj_fwd", grid=(SEQ // tm,),
        in_specs=[row(D_MODEL), full(gain), full(w), full(qkg), row(PAIR), row(PAIR), full(bmean), full(w_out_sh)],
        out_specs=(row(HALF_WIDTH), row(KV_A_WIDTH), row(HALF_WIDTH), row(HALF_WIDTH), row(HALF_WIDTH), row(HALF_WIDTH),
                   pl.BlockSpec((D_MODEL, tm), lambda i: (0, i)),
                   row(HALF_WIDTH), row(KV_A_WIDTH), row(KV_A_WIDTH), row(HALF_WIDTH), row(HALF_WIDTH), row(HALF_WIDTH),
                   *f_specs, *f_specs, *f_specs,
                   pl.BlockSpec((N_DEV,) + w_out_sh.shape, lambda i: (0, 0, 0))),
        out_shape=(nat(HALF_WIDTH), nat(KV_A_WIDTH), nat(HALF_WIDTH), nat(HALF_WIDTH), nat(HALF_WIDTH), nat(HALF_WIDTH),
                   jax.ShapeDtypeStruct((D_MODEL, SEQ), BF16),
                   nat(HALF_WIDTH), nat(KV_A_WIDTH), nat(KV_A_WIDTH), nat(HALF_WIDTH), nat(HALF_WIDTH), nat(HALF_WIDTH),
                   *f_shapes, *f_shapes, *f_shapes,
                   jax.ShapeDtypeStruct((N_DEV,) + w_out_sh.shape, BF16)),
        scratch_shapes=[pltpu.VMEM((tm, IN_WIDTH), F32), _fold_scratch(tm), pltpu.VMEM(w_out_sh.shape, BF16),
                        pltpu.VMEM((N_DEV,) + w_out_sh.shape, BF16)] + GATHER_SCRATCH,
        compiler_params=_params(("arbitrary",)),
    )(x, gain, w, qkg, cos4, sin4, bmean, w_out_sh)


def _fill_band_bias(bias_ref, max_dist):
    j = lax.broadcasted_iota(jnp.int32, (2 * BLOCK, 2 * BLOCK), 0)
    c = lax.broadcasted_iota(jnp.int32, (2 * BLOCK, 2 * BLOCK), 1)
    dist = (c & (BLOCK - 1)) + BLOCK - j
    band = (dist >= 0) & (dist <= max_dist)
    bias_ref[0] = jnp.where(band, 0.0, NEG)
    bias_ref[1] = jnp.where(band & (j >= BLOCK), 0.0, NEG)


def _band_bias(bias_ref, step, b, qb, nb):
    if nb < qb:
        return bias_ref[1 if b % nb == 0 else 0]
    if b > 0:
        return bias_ref[0]
    return bias_ref[jnp.where(((step * qb) & (nb - 1)) == 0, 1, 0)]


def _stack_heads(t):
    lane = lax.broadcasted_iota(jnp.int32, t.shape, 1)
    low = lane < HEAD_DIM
    zero = jnp.zeros_like(t)
    return jnp.concatenate([jnp.where(low, t, zero), jnp.where(low, zero, t)], axis=0)


def _stack_heads_t(t):
    tt = t.astype(F32).T
    low = lax.broadcasted_iota(jnp.int32, tt.shape, 0) < HEAD_DIM
    zero = jnp.zeros_like(tt)
    return jnp.concatenate([jnp.where(low, tt, zero), jnp.where(low, zero, tt)], axis=1).astype(BF16)


def _unstack_t(t):
    return jnp.concatenate([t[:HEAD_DIM, :BLOCK], t[HEAD_DIM:, BLOCK:]], axis=0).T


def _rows_to_stats(rows):
    parts = []
    for row in rows:
        parts.append(jnp.broadcast_to(row[:, :BLOCK], (STAT_REP, BLOCK)))
        parts.append(jnp.broadcast_to(row[:, BLOCK:], (STAT_REP, BLOCK)))
    return jnp.concatenate(parts, axis=0).T


def _stats_to_rows(t):
    tt = t.T
    return [jnp.concatenate([tt[2 * p * STAT_REP:2 * p * STAT_REP + 1, :],
                             tt[(2 * p + 1) * STAT_REP:(2 * p + 1) * STAT_REP + 1, :]], axis=1) for p in range(N_PAIRS)]


def _attn_fwd(name, patterns):
    qb = FWD_BLOCKS_PER_STEP
    steps = SEQ // (qb * BLOCK)

    def one_pattern(step, nb, shared, max_dist, q_ref, kc_ref, vc_ref, sink_ref, o_ref, lse_ref, kp_ref, vp_ref, bias_ref):
        has_sinks = sink_ref is not None

        @pl.when(step == 0)
        def _():
            kp_ref[...] = jnp.zeros_like(kp_ref)
            vp_ref[...] = jnp.zeros_like(vp_ref)
            _fill_band_bias(bias_ref, max_dist)

        cols = [slice(p * PAIR, (p + 1) * PAIR) for p in range(N_PAIRS)]
        kcols = [slice(0, PAIR) if shared else c for c in cols]
        rows = [slice(b * BLOCK, (b + 1) * BLOCK) for b in range(qb)]
        units = [(b, p) for b in range(qb) for p in range(N_PAIRS)]
        n = range(len(units))

        def window(prev_ref, cur_ref, b, kc):
            before = prev_ref[:, kc] if b == 0 else cur_ref[rows[b - 1], kc]
            return jnp.concatenate([before, cur_ref[rows[b], kc]], axis=0)

        st = [jnp.dot(window(kp_ref, kc_ref, b, kcols[p]), _stack_heads_t(q_ref[rows[b], cols[p]]),
                      preferred_element_type=F32) for b, p in units]
        st = [st[u] + _band_bias(bias_ref, step, units[u][0], qb, nb) for u in n]
        m = [jnp.max(s, axis=0, keepdims=True) for s in st]
        if has_sinks:
            sk = [sink_ref[p:p + 1, :] for _, p in units]
            m = [jnp.maximum(m[u], sk[u]) for u in n]
        pt = [jnp.exp2(st[u] - m[u]) for u in n]
        l = [jnp.sum(t, axis=0, keepdims=True) for t in pt]
        if has_sinks:
            l = [l[u] + jnp.exp2(sk[u] - m[u]) for u in n]
        v2t = [window(vp_ref, vc_ref, b, kcols[p]).astype(F32).T.astype(BF16) for b, p in units]
        ot = [jnp.dot(v2t[u], pt[u].astype(BF16), preferred_element_type=F32) / l[u] for u in n]
        for u, (b, p) in enumerate(units):
            o_ref[rows[b], cols[p]] = _unstack_t(ot[u]).astype(BF16)
        for b in range(qb):
            lse_ref[rows[b], :] = _rows_to_stats([m[u] + jnp.log2(l[u]) for u in n if units[u][0] == b])
        kp_ref[...] = kc_ref[rows[-1], :]
        vp_ref[...] = vc_ref[rows[-1], :]

    n_in = [4 if sinks is not None else 3 for _, _, _, sinks, _ in patterns]

    def body(*refs):
        ins, rest = refs[:sum(n_in)], refs[sum(n_in):]
        outs, scratch = rest[:2 * len(patterns)], rest[2 * len(patterns):]
        bias_ref = scratch[-1]
        step = pl.program_id(0)
        first = 0
        for p, (q, k, _, sinks, max_dist) in enumerate(patterns):
            mine = ins[first:first + n_in[p]]
            first += n_in[p]
            sink_ref = mine[3] if sinks is not None else None
            kp_ref, vp_ref = scratch[2 * p], scratch[2 * p + 1]

            @pl.when((step >= p * steps) & (step < (p + 1) * steps))
            def _():
                one_pattern(step - p * steps, q.shape[1] // BLOCK, k.shape[2] == PAIR, max_dist,
                            mine[0], mine[1], mine[2], sink_ref, outs[2 * p], outs[2 * p + 1], kp_ref, vp_ref, bias_ref)

    def during(p, width):
        return pl.BlockSpec((qb * BLOCK, width), lambda s: (jnp.clip(s - p * steps, 0, steps - 1), 0))

    flat = lambda a: a.reshape(SEQ, a.shape[2])
    in_specs, args, out_specs, out_shape, scratch = [], [], [], [], []
    for p, (q, k, v, sinks, _) in enumerate(patterns):
        ck = k.shape[2]
        in_specs += [during(p, HALF_WIDTH), during(p, ck), during(p, ck)]
        args += [flat(q), flat(k), flat(v)]
        if sinks is not None:
            in_specs.append(pl.BlockSpec(sinks.shape, lambda s: (0, 0)))
            args.append(sinks)
        out_specs += [during(p, HALF_WIDTH), during(p, STAT_WIDTH)]
        out_shape += [jax.ShapeDtypeStruct((SEQ, HALF_WIDTH), BF16), jax.ShapeDtypeStruct((SEQ, STAT_WIDTH), F32)]
        scratch += [pltpu.VMEM((BLOCK, ck), BF16), pltpu.VMEM((BLOCK, ck), BF16)]
    outs = pl.pallas_call(
        body, name=name, grid=(len(patterns) * steps,), in_specs=in_specs,
        out_specs=tuple(out_specs), out_shape=tuple(out_shape),
        scratch_shapes=scratch + [pltpu.VMEM((2, 2 * BLOCK, 2 * BLOCK), F32)],
        compiler_params=_params(("arbitrary",)),
    )(*args)
    return [(outs[2 * p].reshape(q.shape), outs[2 * p + 1].reshape(q.shape[0], q.shape[1], STAT_WIDTH))
            for p, (q, _, _, _, _) in enumerate(patterns)]


def _attn_bwd(name, q, k, v, d_o, lse, delta, sink_rows, max_dist):
    n_seq, length, _ = q.shape
    ck = k.shape[2]
    nb = length // BLOCK
    n_blocks = n_seq * nb
    n_rows = n_seq * length
    shared = ck == PAIR
    has_sinks = sink_rows is not None
    qb = BWD_BLOCKS_PER_STEP
    n_steps = n_blocks // qb

    def body(*refs):
        if has_sinks:
            (q_ref, kc_ref, vc_ref, do_ref, lse_ref, dl_ref, sink_ref,
             dq_ref, dk_ref, dv_ref, dsink_ref, ck_scr, cv_scr, kp_ref, vp_ref, bias_ref) = refs
        else:
            (q_ref, kc_ref, vc_ref, do_ref, lse_ref, dl_ref,
             dq_ref, dk_ref, dv_ref, ck_scr, cv_scr, kp_ref, vp_ref, bias_ref) = refs
        step = pl.program_id(0)

        @pl.when(step == 0)
        def _():
            ck_scr[...] = jnp.zeros_like(ck_scr)
            cv_scr[...] = jnp.zeros_like(cv_scr)
            kp_ref[...] = jnp.zeros_like(kp_ref)
            vp_ref[...] = jnp.zeros_like(vp_ref)
            if has_sinks:
                dsink_ref[...] = jnp.zeros_like(dsink_ref)
            _fill_band_bias(bias_ref, max_dist)

        cols = [slice(p * PAIR, (p + 1) * PAIR) for p in range(N_PAIRS)]
        kcols = [slice(0, PAIR) if shared else c for c in cols]
        rows = [slice(b * BLOCK, (b + 1) * BLOCK) for b in range(qb)]
        units = [(b, p) for b in range(qb) for p in range(N_PAIRS)]
        n = range(len(units))
        nt = (((1,), (1,)), ((), ()))

        def window(prev_ref, cur_ref, b, kc):
            before = prev_ref[:, kc] if b == 0 else cur_ref[rows[b - 1], kc]
            return jnp.concatenate([before, cur_ref[rows[b], kc]], axis=0)

        q_st = [_stack_heads(q_ref[rows[b], cols[p]]) for b, p in units]
        do_st = [_stack_heads(do_ref[rows[b], cols[p]]) for b, p in units]
        k2 = [window(kp_ref, kc_ref, b, kcols[p]) for b, p in units]
        v2 = [window(vp_ref, vc_ref, b, kcols[p]) for b, p in units]
        st = [lax.dot_general(k2[u], q_st[u], nt, preferred_element_type=F32) for u in n]
        dpt = [lax.dot_general(v2[u], do_st[u], nt, preferred_element_type=F32) for u in n]
        lse_rows = [_stats_to_rows(lse_ref[rows[b], :]) for b in range(qb)]
        dl_rows = [_stats_to_rows(dl_ref[rows[b], :]) for b in range(qb)]
        lse_row = [lse_rows[b][p] for b, p in units]
        dl_row = [dl_rows[b][p] for b, p in units]
        pt = [jnp.exp2(st[u] + _band_bias(bias_ref, step, units[u][0], qb, nb) - lse_row[u]) for u in n]
        dst = [(pt[u] * (dpt[u] - dl_row[u])).astype(BF16) for u in n]
        ptb = [t.astype(BF16) for t in pt]
        dv2 = [jnp.dot(ptb[u], do_st[u], preferred_element_type=F32) for u in n]
        dk2 = [jnp.dot(dst[u], q_st[u], preferred_element_type=F32) for u in n]
        k2t = [k2[u].astype(F32).T.astype(BF16) for u in n]
        dqt = [jnp.dot(k2t[u], dst[u], preferred_element_type=F32) for u in n]
        for u, (b, p) in enumerate(units):
            dq_ref[rows[b], cols[p]] = _unstack_t(dqt[u]).astype(BF16)
        if has_sinks:
            for u, (b, p) in enumerate(units):
                p_sink = jnp.exp2(sink_ref[p:p + 1, :] - lse_row[u])
                dsink_ref[p:p + 1, :] = dsink_ref[p:p + 1, :] - p_sink * dl_row[u]

        def total(parts, w, group):
            sel = [u for u, (b, p) in enumerate(units) if (shared or p == group)]
            terms = ([parts[u][:BLOCK] for u in sel if units[u][0] == w]
                     + [parts[u][BLOCK:] for u in sel if units[u][0] == w - 1])
            tot = terms[0]
            for t in terms[1:]:
                tot = tot + t
            return tot

        first_row = step * (qb * BLOCK)
        for acc_ref, out_ref, parts in ((ck_scr, dk_ref, dk2), (cv_scr, dv_ref, dv2)):
            for group in range(1 if shared else N_PAIRS):
                kc = kcols[group]

                @pl.when(step > 0)
                def _():
                    out_ref[pl.ds(pl.multiple_of(first_row - BLOCK, BLOCK), BLOCK), kc] = (
                        acc_ref[:, kc] + total(parts, 0, group)).astype(BF16)

                for w in range(1, qb):
                    out_ref[pl.ds(pl.multiple_of(first_row + (w - 1) * BLOCK, BLOCK), BLOCK), kc] = (
                        total(parts, w, group).astype(BF16))
                acc_ref[:, kc] = total(parts, qb, group)

        @pl.when(step == n_steps - 1)
        def _():
            dk_ref[pl.ds(n_rows - BLOCK, BLOCK), :] = ck_scr[...].astype(BF16)
            dv_ref[pl.ds(n_rows - BLOCK, BLOCK), :] = cv_scr[...].astype(BF16)

        kp_ref[...] = kc_ref[rows[-1], :]
        vp_ref[...] = vc_ref[rows[-1], :]

    cur = lambda width: pl.BlockSpec((qb * BLOCK, width), lambda s: (s, 0))
    whole = lambda width: pl.BlockSpec((n_rows, width), lambda s: (0, 0))
    flat = lambda a: a.reshape(n_rows, a.shape[2])
    in_specs = [cur(HALF_WIDTH), cur(ck), cur(ck), cur(HALF_WIDTH), cur(STAT_WIDTH), cur(STAT_WIDTH)]
    args = [flat(a) for a in (q, k, v, d_o, lse, delta)]
    out_specs = [cur(HALF_WIDTH), whole(ck), whole(ck)]
    out_shape = [jax.ShapeDtypeStruct((n_rows, HALF_WIDTH), BF16),
                 jax.ShapeDtypeStruct((n_rows, ck), BF16), jax.ShapeDtypeStruct((n_rows, ck), BF16)]
    if has_sinks:
        in_specs.append(pl.BlockSpec(sink_rows.shape, lambda s: (0, 0)))
        args.append(sink_rows)
        out_specs.append(pl.BlockSpec(sink_rows.shape, lambda s: (0, 0)))
        out_shape.append(jax.ShapeDtypeStruct(sink_rows.shape, F32))
    outs = pl.pallas_call(
        body, name=name, grid=(n_steps,), in_specs=in_specs,
        out_specs=tuple(out_specs), out_shape=tuple(out_shape),
        scratch_shapes=[pltpu.VMEM((BLOCK, ck), F32), pltpu.VMEM((BLOCK, ck), F32),
                        pltpu.VMEM((BLOCK, ck), BF16), pltpu.VMEM((BLOCK, ck), BF16),
                        pltpu.VMEM((2, 2 * BLOCK, 2 * BLOCK), F32)],
        compiler_params=_params(("arbitrary",)),
    )(*args)
    return tuple(o.reshape(n_seq, length, o.shape[1]) for o in outs[:3]) + tuple(outs[3:])


def _tail(oa, ob1, lb1, ob4, lb4, ob16, lb16, gate_a, gate_b, x, target, w_out, spread, gather):
    tm = ROW_TILE

    def split_dot(v, mat):
        hi = v.astype(BF16)
        lo = (v - hi.astype(F32)).astype(BF16)
        return jnp.dot(hi, mat, preferred_element_type=F32) + jnp.dot(lo, mat, preferred_element_type=F32)

    def body(oa_ref, ob1_ref, lb1_ref, ob4_ref, lb4_ref, ob16_ref, lb16_ref, ga_ref, gb_ref, x_ref, t_ref, w_ref,
             sp_ref, ga_mat_ref,
             loss_ref, dy_ref, gwo_ref, doa_ref, dla_ref, dga_ref, dgb_ref,
             dob_ref, dob4_ref, dob16_ref, dlb_ref, dlb4_ref, dlb16_ref, lse_ref, lse4_ref, lse16_ref,
             s_f, mix_keep, dy_keep):
        i = pl.program_id(0)
        sp, gat = sp_ref[...], ga_mat_ref[...]
        o4, o16 = _unfold_load(ob4_ref, s_f, 4, tm), _unfold_load(ob16_ref, s_f, 16, tm)
        l4, l16 = _unfold_load(lb4_ref, s_f, 4, tm), _unfold_load(lb16_ref, s_f, 16, tm)
        o1, l1 = ob1_ref[...].astype(F32), lb1_ref[...]
        mx = jnp.maximum(jnp.maximum(l1, l4), l16)
        e1, e4, e16 = jnp.exp2(l1 - mx), jnp.exp2(l4 - mx), jnp.exp2(l16 - mx)
        den = e1 + e4 + e16
        inv = 1.0 / den
        ob = split_dot(e1 * inv, sp) * o1 + split_dot(e4 * inv, sp) * o4 + split_dot(e16 * inv, sp) * o16
        lse_b = mx + jnp.log2(den)

        oa, ga, gb = oa_ref[...].astype(F32), ga_ref[...].astype(F32), gb_ref[...].astype(F32)
        sa, sb = _sigmoid(ga), _sigmoid(gb)
        mixed = jnp.concatenate(_unpair_tiles(_tiles(oa * (ga * sa))) + [ob * (gb * sb)], axis=1)
        mixed_bf = mixed.astype(BF16)
        w = w_ref[...]
        yv = x_ref[...] + jnp.dot(mixed_bf, w, preferred_element_type=F32)
        err = yv - t_ref[...]
        sq = jnp.sum(err * err, axis=0, keepdims=True)
        dy = err * (1.0 / D_MODEL)
        dy_ref[...] = dy
        dy_bf = dy.astype(BF16)
        mix_t = mixed.T.astype(BF16)

        @pl.when(i == 0)
        def _():
            loss_ref[...] = sq

        @pl.when(i > 0)
        def _():
            loss_ref[...] += sq

        @pl.when((i & 1) == 0)
        def _():
            mix_keep[...] = mix_t
            dy_keep[...] = dy_bf

        @pl.when((i & 1) == 1)
        def _():
            gw = jnp.dot(jnp.concatenate([mix_keep[...], mix_t], axis=1), jnp.concatenate([dy_keep[...], dy_bf], axis=0),
                         preferred_element_type=F32)

            @pl.when(i == 1)
            def _():
                gwo_ref[...] = gw

            @pl.when(i > 1)
            def _():
                gwo_ref[...] += gw

        dmix = lax.dot_general(dy_bf, w, (((1,), (1,)), ((), ())), preferred_element_type=F32)
        dma = jnp.concatenate(_pair_tiles(_tiles(dmix[:, :HALF_WIDTH])), axis=1)
        dmb = dmix[:, HALF_WIDTH:]

        doa = dma * (ga * sa)
        doa_ref[...] = doa.astype(BF16)
        dla_ref[...] = split_dot(doa * oa, gat)
        dga_ref[...] = (dma * oa * (sa * (1.0 + ga * (1.0 - sa)))).astype(BF16)
        dob = dmb * (gb * sb)
        dgb_ref[...] = (dmb * ob * (sb * (1.0 + gb * (1.0 - sb)))).astype(BF16)
        dlb = split_dot(dob * ob, gat)
        dob_ref[...] = dob.astype(BF16)
        _fold_store(dob, s_f, dob4_ref, dob16_ref, tm)
        dlb_ref[...] = dlb
        _fold_store(dlb, s_f, dlb4_ref, dlb16_ref, tm)
        lse_ref[...] = lse_b
        _fold_store(lse_b, s_f, lse4_ref, lse16_ref, tm)

    row = lambda width: pl.BlockSpec((tm, width), lambda i: (i, 0))
    full = lambda a: pl.BlockSpec(a.shape, lambda i: (0,) * a.ndim)
    fb_shapes, fb_specs = _fold_specs(tm, BF16)
    _, ff_specs = _fold_specs(tm, F32)
    st_shapes, st_specs = _fold_specs(tm, F32, STAT_WIDTH)
    nat = lambda dtype, width=HALF_WIDTH: jax.ShapeDtypeStruct((SEQ, width), dtype)
    return pl.pallas_call(
        body, name="tail", grid=(SEQ // tm,),
        in_specs=[row(HALF_WIDTH), row(HALF_WIDTH), row(STAT_WIDTH), ff_specs[0], st_specs[0], ff_specs[1], st_specs[1],
                  row(HALF_WIDTH), row(HALF_WIDTH), row(D_MODEL), row(D_MODEL), full(w_out), full(spread), full(gather)],
        out_specs=(pl.BlockSpec((1, D_MODEL), lambda i: (0, 0)), row(D_MODEL),
                   pl.BlockSpec((D_MODEL, D_MODEL), lambda i: (0, 0)),
                   row(HALF_WIDTH), row(STAT_WIDTH), row(HALF_WIDTH), row(HALF_WIDTH),
                   row(HALF_WIDTH), *fb_specs, row(STAT_WIDTH), *st_specs, row(STAT_WIDTH), *st_specs),
        out_shape=(jax.ShapeDtypeStruct((1, D_MODEL), F32), jax.ShapeDtypeStruct((SEQ, D_MODEL), F32),
                   jax.ShapeDtypeStruct((D_MODEL, D_MODEL), F32),
                   nat(BF16), nat(F32, STAT_WIDTH), nat(BF16), nat(BF16),
                   nat(BF16), *fb_shapes, nat(F32, STAT_WIDTH), *st_shapes, nat(F32, STAT_WIDTH), *st_shapes),
        scratch_shapes=[_fold_scratch(tm), pltpu.VMEM((D_MODEL, tm), BF16), pltpu.VMEM((tm, D_MODEL), BF16)],
        compiler_params=_params(("arbitrary",)),
    )(oa, ob1, lb1, ob4, lb4, ob16, lb16, gate_a, gate_b, x, target, w_out, spread, gather)


def _dproj_assemble(dqa, dka, dva, dga, dgb, dq1, dk1, dv1, dq4, dk4, dv4, dq16, dk16, dv16, tqa, tqb, tkb, tka,
                    qkg, cos4, sin4, bmean):
    tm = ROW_TILE

    def norm_rope_bwd(d_out, t, g, cos, sin, bm, scale):
        d_r = d_out * scale
        dyv = d_r * cos + _swap_halves(d_r * sin)
        rr = lax.rsqrt(_head_sum(t * t, bm) + EPS)
        that = t * rr
        dgain = jnp.sum(dyv * that, axis=0, keepdims=True)
        gdy = dyv * g
        dt = rr * (gdy - that * _head_sum(that * gdy, bm))
        return dt, dgain

    def body(dqa_ref, dka_ref, dva_ref, dga_ref, dgb_ref, dq1_ref, dk1_ref, dv1_ref, dq4_ref, dk4_ref, dv4_ref,
             dq16_ref, dk16_ref, dv16_ref, tqa_ref, tqb_ref, tkb_ref, tka_ref, qkg_ref, cos_ref, sin_ref, bm_ref,
             dproj_ref, dqkg_ref, s_f):
        i = pl.program_id(0)
        cos, sin, bm = cos_ref[...], sin_ref[...], bm_ref[...]

        def merged(nat_ref, f4_ref, f16_ref):
            return nat_ref[...].astype(F32) + _unfold_load(f4_ref, s_f, 4, tm) + _unfold_load(f16_ref, s_f, 16, tm)

        @pl.when(i == 0)
        def _():
            dqkg_ref[...] = jnp.zeros_like(dqkg_ref)

        def through(d_out, t, row, scale, c0, paired=False):
            g = qkg_ref[row:row + 1, :]
            tot = jnp.zeros((1, PAIR), F32)
            dts = []
            for j in range(d_out.shape[1] // PAIR):
                cols = slice(j * PAIR, (j + 1) * PAIR)
                dt, dg = norm_rope_bwd(d_out[:, cols], t[:, cols], g, cos, sin, bm, scale)
                dts.append(dt)
                tot = tot + dg
            if paired:
                dts = _unpair_tiles(dts)
            for j, dt in enumerate(dts):
                dproj_ref[:, c0 + j * PAIR:c0 + (j + 1) * PAIR] = dt.astype(BF16)
            dqkg_ref[row:row + 1, :] += tot

        through(dqa_ref[...].astype(F32), tqa_ref[...].astype(F32), 0, HEAD_DIM ** -0.5, C_QA, paired=True)
        through(dka_ref[...].astype(F32), tka_ref[...].astype(F32), 1, LN2, C_KA)
        through(merged(dq1_ref, dq4_ref, dq16_ref), tqb_ref[...].astype(F32), 2, HEAD_DIM ** -0.5, C_QB)
        through(merged(dk1_ref, dk4_ref, dk16_ref), tkb_ref[...].astype(F32), 3, LN2, C_KB)
        dproj_ref[:, C_VB:C_VB + HALF_WIDTH] = merged(dv1_ref, dv4_ref, dv16_ref).astype(BF16)
        dproj_ref[:, C_GA:C_GA + HALF_WIDTH] = jnp.concatenate(
            _unpair_tiles(_tiles(dga_ref[...].astype(F32))), axis=1).astype(BF16)
        dproj_ref[:, C_GB:C_GB + HALF_WIDTH] = dgb_ref[...].astype(BF16)
        dproj_ref[:, C_VA:C_VA + KV_A_WIDTH] = dva_ref[...].astype(BF16)

    row = lambda width: pl.BlockSpec((tm, width), lambda i: (i, 0))
    full = lambda a: pl.BlockSpec(a.shape, lambda i: (0,) * a.ndim)
    _, ff_specs = _fold_specs(tm, F32)
    return pl.pallas_call(
        body, name="dproj_assemble", grid=(SEQ // tm,),
        in_specs=[row(HALF_WIDTH), row(KV_A_WIDTH), row(KV_A_WIDTH), row(HALF_WIDTH), row(HALF_WIDTH),
                  row(HALF_WIDTH), row(HALF_WIDTH), row(HALF_WIDTH), ff_specs[0], ff_specs[0], ff_specs[0],
                  ff_specs[1], ff_specs[1], ff_specs[1],
                  row(HALF_WIDTH), row(HALF_WIDTH), row(HALF_WIDTH), row(KV_A_WIDTH),
                  full(qkg), row(PAIR), row(PAIR), full(bmean)],
        out_specs=(row(IN_WIDTH), pl.BlockSpec((SMALL_ROWS, PAIR), lambda i: (0, 0))),
        out_shape=(jax.ShapeDtypeStruct((SEQ, IN_WIDTH), BF16), jax.ShapeDtypeStruct((SMALL_ROWS, PAIR), F32)),
        scratch_shapes=[_fold_scratch(tm)],
        compiler_params=_params(("arbitrary",)),
    )(dqa, dka, dva, dga, dgb, dq1, dk1, dv1, dq4, dk4, dv4, dq16, dk16, dv16, tqa, tqb, tkb, tka, qkg, cos4, sin4, bmean)


def _input_grad_reduce(dproj, w, x, gain, dy, blocks_in, blocks_out, small):
    tm = ROW_TILE
    n_steps = SEQ // tm
    stage2_step, stage3_step = 3, 7
    shapes = (blocks_in.shape[1:], blocks_out.shape[1:])

    def body(dp_ref, w_ref, x_ref, g_ref, dy_ref, ga_hbm, gb_hbm, small_ref,
             gx_ref, out_a, out_b, small_out_ref, dgain_out_ref,
             part_a, part_b, sib_a, sib_b, wire_a, wire_b, chips_a, chips_b, small_all, dgain_acc, dgain_all,
             load_sems, sib_send, sib_recv, chip_send, chip_recv, small_send, small_recv, dgain_send, dgain_recv):
        i = pl.program_id(0)
        x, y, c = lax.axis_index("x"), lax.axis_index("y"), lax.axis_index("c")
        sibling = (x, y, 1 - c)
        chips = [(x, y), (1 - x, y), (x, 1 - y), (1 - x, 1 - y)]
        my_id = 4 * x + 2 * y + c
        g_hbm, part, from_sib = (ga_hbm, gb_hbm), (part_a, part_b), (sib_a, sib_b)
        to_wire, from_chips, out = (wire_a, wire_b), (chips_a, chips_b), (out_a, out_b)
        both = (0, 1)

        def blk(a, chip, core):
            return g_hbm[a].at[4 * chip[0] + 2 * chip[1] + core]

        def to_all(src, dst_all, send, recv):
            copies = []
            for rel in range(1, N_DEV):
                dx, dy_, dc = (rel >> 2) & 1, (rel >> 1) & 1, rel & 1
                to = (1 - x if dx else x, 1 - y if dy_ else y, 1 - c if dc else c)
                copies.append(pltpu.make_async_remote_copy(
                    src_ref=src, dst_ref=dst_all.at[my_id], send_sem=send.at[rel - 1], recv_sem=recv.at[rel - 1],
                    device_id=to, device_id_type=MESH))
            return copies

        small_copies = to_all(small_all.at[my_id], small_all, small_send, small_recv)
        dgain_copies = to_all(dgain_acc, dgain_all, dgain_send, dgain_recv)
        loads = [[pltpu.make_async_copy(blk(a, chips[k], c), part[a].at[k], load_sems.at[a, k]) for k in range(4)] for a in both]
        to_sib = [[pltpu.make_async_remote_copy(
            src_ref=blk(a, chips[k], 1 - c), dst_ref=from_sib[a].at[k], send_sem=sib_send.at[a, k], recv_sem=sib_recv.at[a, k],
            device_id=sibling, device_id_type=MESH) for k in range(4)] for a in both]
        first, other, k_first, k_other = _routes()
        to_chips = [[pltpu.make_async_remote_copy(
            src_ref=to_wire[a].at[s], dst_ref=from_chips[a].at[s], send_sem=chip_send.at[a, s], recv_sem=chip_recv.at[a, s],
            device_id=(first, first, other)[s], device_id_type=MESH) for s in range(3)] for a in both]

        def chip_partial(a, k):
            return part[a][k].astype(F32) + from_sib[a][k].astype(F32)

        @pl.when(i == 0)
        def _():
            small_all[my_id] = small_ref[...]
            for cp in small_copies:
                cp.start()
            for k in (1, 2, 3, 0):
                for a in both:
                    loads[a][k].start()
                    to_sib[a][k].start()

        @pl.when(i == stage2_step)
        def _():
            for k in (1, 2, 3):
                for a in both:
                    loads[a][k].wait()
                    to_sib[a][k].wait_recv()
            for s, k in ((0, 3), (1, k_first)):
                for a in both:
                    to_wire[a][s] = chip_partial(a, k).astype(BF16)
                    to_chips[a][s].start()

        @pl.when(i == stage3_step)
        def _():
            for a in both:
                to_chips[a][0].wait_recv()
                to_wire[a][2] = (chip_partial(a, k_other) + from_chips[a][0].astype(F32)).astype(BF16)
                to_chips[a][2].start()

        dh = lax.dot_general(dp_ref[...], w_ref[...], (((1,), (1,)), ((), ())), preferred_element_type=F32)
        xf = x_ref[...]
        r = lax.rsqrt(jnp.mean(xf * xf, axis=-1, keepdims=True) + EPS)
        xhat = xf * r
        dg = jnp.sum(dh * xhat, axis=0, keepdims=True)
        dxh = dh * g_ref[...]
        dx = r * (dxh - xhat * jnp.mean(dxh * xhat, axis=-1, keepdims=True))
        gx_ref[...] = dy_ref[...] + dx

        @pl.when(i == 0)
        def _():
            dgain_acc[...] = dg

        @pl.when(i > 0)
        def _():
            dgain_acc[...] += dg

        @pl.when(i == n_steps - 1)
        def _():
            dgain_all[my_id] = dgain_acc[...]
            for cp in dgain_copies:
                cp.start()
            for a in both:
                loads[a][0].wait()
                to_sib[a][0].wait_recv()
                acc = chip_partial(a, 0)
                for s in (1, 2):
                    to_chips[a][s].wait_recv()
                    acc = acc + from_chips[a][s].astype(F32)
                out[a][...] = acc
            for copies, gathered, dst in ((small_copies, small_all, small_out_ref), (dgain_copies, dgain_all, dgain_out_ref)):
                for cp in copies:
                    cp.wait_recv()
                tot = gathered[0]
                for d in range(1, N_DEV):
                    tot = tot + gathered[d]
                dst[...] = tot
            for cp in to_sib[0] + to_sib[1] + to_chips[0] + to_chips[1] + small_copies + dgain_copies:
                cp.wait_send()

    row = lambda width: pl.BlockSpec((tm, width), lambda i: (i, 0))
    full = lambda a: pl.BlockSpec(a.shape, lambda i: (0,) * a.ndim)
    whole = lambda shape: pl.BlockSpec(shape, lambda i: (0,) * len(shape))
    hbm = pl.BlockSpec(memory_space=pl.ANY)
    dtypes = (blocks_in.dtype, blocks_out.dtype)
    buf = lambda n, dts: [pltpu.VMEM((n,) + s, dt) for s, dt in zip(shapes, dts)]
    return pl.pallas_call(
        body, name="input_grad_rs", grid=(n_steps,),
        in_specs=[row(IN_WIDTH), full(w), row(D_MODEL), full(gain), row(D_MODEL), hbm, hbm, full(small)],
        out_specs=(row(D_MODEL), whole(shapes[0]), whole(shapes[1]), whole((SMALL_ROWS, SMALL_COLS)), whole((1, D_MODEL))),
        out_shape=(jax.ShapeDtypeStruct((SEQ, D_MODEL), F32), jax.ShapeDtypeStruct(shapes[0], F32),
                   jax.ShapeDtypeStruct(shapes[1], F32), jax.ShapeDtypeStruct((SMALL_ROWS, SMALL_COLS), F32),
                   jax.ShapeDtypeStruct((1, D_MODEL), F32)),
        scratch_shapes=[*buf(4, dtypes), *buf(4, dtypes), *buf(3, (BF16, BF16)), *buf(3, (BF16, BF16)),
                        pltpu.VMEM((N_DEV, SMALL_ROWS, SMALL_COLS), F32),
                        pltpu.VMEM((1, D_MODEL), F32), pltpu.VMEM((N_DEV, 1, D_MODEL), F32),
                        pltpu.SemaphoreType.DMA((2, 4)), pltpu.SemaphoreType.DMA((2, 4)), pltpu.SemaphoreType.DMA((2, 4)),
                        pltpu.SemaphoreType.DMA((2, 3)), pltpu.SemaphoreType.DMA((2, 3)),
                        pltpu.SemaphoreType.DMA((7,)), pltpu.SemaphoreType.DMA((7,)),
                        pltpu.SemaphoreType.DMA((7,)), pltpu.SemaphoreType.DMA((7,))],
        compiler_params=_params(("arbitrary",)),
    )(dproj, w, x, gain, dy, blocks_in, blocks_out, small)


def _weight_grad(h_t, dproj):
    tk = 1024
    cb = IN_WIDTH // 2
    n_k = SEQ // tk

    def body(ht_ref, dp_ref, out_ref, acc):
        k = pl.program_id(1)
        upd = jnp.dot(ht_ref[...], dp_ref[...], preferred_element_type=F32)

        @pl.when(k == 0)
        def _():
            acc[...] = upd

        @pl.when(k > 0)
        def _():
            acc[...] += upd

        @pl.when(k == n_k - 1)
        def _():
            for b in range(N_DEV // 2):
                out_ref[b] = acc[:, b * SHARD_IN:(b + 1) * SHARD_IN].astype(BF16)

    return pl.pallas_call(
        body, name="weight_grad", grid=(2, n_k),
        in_specs=[pl.BlockSpec((D_MODEL, tk), lambda j, k: (0, k)), pl.BlockSpec((tk, cb), lambda j, k: (k, j))],
        out_specs=pl.BlockSpec((N_DEV // 2, D_MODEL, SHARD_IN), lambda j, k: (j, 0, 0)),
        out_shape=jax.ShapeDtypeStruct((N_DEV, D_MODEL, SHARD_IN), BF16),
        scratch_shapes=[pltpu.VMEM((D_MODEL, cb), F32)],
        compiler_params=_params(("arbitrary", "arbitrary")),
    )(h_t, dproj)


def _adamw(groups):
    def body(*refs):
        ins, outs = refs[:4 * len(groups)], refs[4 * len(groups):]
        for i in range(len(groups)):
            w_ref, g_ref, m_ref, v_ref = ins[4 * i:4 * i + 4]
            d_ref, nm_ref, nv_ref = outs[3 * i:3 * i + 3]
            gv = g_ref[...]
            nm = ADAM_B1 * m_ref[...] + (1.0 - ADAM_B1) * gv
            nv = ADAM_B2 * v_ref[...] + (1.0 - ADAM_B2) * jnp.square(gv)
            m_hat = nm / (1.0 - ADAM_B1 ** ADAM_STEP)
            v_hat = nv / (1.0 - ADAM_B2 ** ADAM_STEP)
            d_ref[...] = -ADAM_LR * (m_hat / (jnp.sqrt(v_hat) + ADAM_EPS) + ADAM_WD * w_ref[...])
            nm_ref[...] = nm
            nv_ref[...] = nv

    vmem = pl.BlockSpec(memory_space=pltpu.VMEM)
    outs = pl.pallas_call(
        body, name="adamw", in_specs=[vmem] * (4 * len(groups)), out_specs=(vmem,) * (3 * len(groups)),
        out_shape=tuple(jax.ShapeDtypeStruct(w.shape, F32) for w, _, _, _ in groups for _ in range(3)),
        compiler_params=pltpu.CompilerParams(vmem_limit_bytes=VMEM_LIMIT),
    )(*[a for group in groups for a in group])
    return [tuple(outs[3 * i:3 * i + 3]) for i in range(len(groups))]


SMALL_USED = D_MODEL + 4 * HEAD_DIM + 8


def _pack_small(norm_gain, qa, ka, sinks, qb, kb, extra=None):
    parts = [norm_gain.reshape(-1), qa.reshape(-1), ka.reshape(-1), sinks.reshape(-1), qb.reshape(-1), kb.reshape(-1)]
    if extra is not None:
        parts.append(extra.reshape(-1))
    flat = jnp.concatenate(parts)
    flat = jnp.pad(flat, (0, SMALL_ROWS * SMALL_COLS - flat.shape[0]))
    return flat.reshape(SMALL_ROWS, SMALL_COLS)


def _unpack_small(a):
    flat = a.reshape(-1)
    sizes = (D_MODEL, HEAD_DIM, HEAD_DIM, 8, HEAD_DIM, HEAD_DIM)
    out, off = [], 0
    for s in sizes:
        out.append(flat[off:off + s].reshape(1, s))
        off += s
    return out


def _fold_heads(row):
    return row[0, :HEAD_DIM] + row[0, HEAD_DIM:]


def kernel(x, norm_gain, w_in, q_norm_a, k_norm_a, sinks_a, q_norm_b, k_norm_b, w_out, loss_target, m_norm_gain, m_w_in, m_q_norm_a, m_k_norm_a, m_sinks_a, m_q_norm_b, m_k_norm_b, m_w_out, v_norm_gain, v_w_in, v_q_norm_a, v_k_norm_a, v_sinks_a, v_q_norm_b, v_k_norm_b, v_w_out):
    x2, tgt = x[0], loss_target[0]
    w_in_sh, w_out_sh = w_in[0], w_out[0]

    w_full = _all_gather_w_in(w_in_sh)

    inv = np.float32(ROPE_THETA) ** (-np.arange(HEAD_DIM // 2, dtype=np.float32) / np.float32(HEAD_DIM // 2))
    ang = np.arange(SEQ, dtype=np.float32)[:, None] * inv[None, :].astype(np.float32)
    cos, sin = np.cos(ang).astype(np.float32), np.sin(ang).astype(np.float32)
    cos4 = jnp.asarray(np.concatenate([cos, cos, cos, cos], axis=1))
    sin4 = jnp.asarray(np.concatenate([-sin, sin, -sin, sin], axis=1))
    blockdiag = np.kron(np.eye(2, dtype=np.float32), np.ones((HEAD_DIM, HEAD_DIM), np.float32))
    bmean = jnp.asarray(blockdiag / HEAD_DIM, dtype=BF16)
    gather_np = np.kron(np.eye(2 * N_PAIRS, dtype=np.float32), np.ones((HEAD_DIM, STAT_REP), np.float32))
    spread_np = np.kron(np.eye(2 * N_PAIRS, dtype=np.float32), np.ones((STAT_REP, HEAD_DIM), np.float32))
    spread_np[np.arange(STAT_WIDTH) % STAT_REP != 0] = 0.0
    gather, spread = jnp.asarray(gather_np, dtype=BF16), jnp.asarray(spread_np, dtype=BF16)
    two = lambda g: jnp.concatenate([g, g], axis=1)
    qkg = jnp.concatenate([two(q_norm_a), two(k_norm_a), two(q_norm_b), two(k_norm_b),
                           jnp.zeros((SMALL_ROWS - 4, PAIR), F32)], axis=0)
    sinks_paired = jnp.stack([sinks_a[0, :N_PAIRS], sinks_a[0, N_PAIRS:]], axis=1) * LOG2E
    sink_rows = jnp.concatenate([jnp.repeat(sinks_paired, BLOCK, axis=1),
                                 jnp.zeros((SMALL_ROWS - N_PAIRS, 2 * BLOCK), F32)], axis=0)

    (tqa, tka, tqb, tkb, gate_a, gate_b, h_t, qa, ka, va, qb, kb, vb, qb4, qb16, kb4, kb16, vb4, vb16,
     gathered_out) = _proj_fwd(x2, norm_gain, w_full, qkg, cos4, sin4, bmean, w_out_sh)
    wo_full = gathered_out.reshape(D_MODEL, D_MODEL)
    (oa, la), (ob1, lb1), (ob4, lb4), (ob16, lb16) = _attn_fwd("attn_fwd", [
        (qa[None], ka[None], va[None], sink_rows, BLOCK - 1), (qb[None], kb[None], vb[None], None, BLOCK),
        (qb4, kb4, vb4, None, BLOCK), (qb16, kb16, vb16, None, BLOCK)])
    (loss_cols, dy, gwo, doa, dla, dga, dgb, dob, dob4, dob16, dlb, dlb4, dlb16, lse_b, lse4, lse16) = _tail(
        oa[0], ob1[0], lb1[0], ob4, lb4, ob16, lb16, gate_a, gate_b, x2, tgt, wo_full, spread, gather)

    dqa, dka, dva, dsink = _attn_bwd("attn_a_bwd", qa[None], ka[None], va[None], doa[None], la, dla[None], sink_rows, BLOCK - 1)
    dq1, dk1, dv1 = _attn_bwd("attn_b1_bwd", qb[None], kb[None], vb[None], dob[None], lse_b[None], dlb[None], None, BLOCK)
    dq4, dk4, dv4 = _attn_bwd("attn_b4_bwd", qb4, kb4, vb4, dob4, lse4, dlb4, None, BLOCK)
    dq16, dk16, dv16 = _attn_bwd("attn_b16_bwd", qb16, kb16, vb16, dob16, lse16, dlb16, None, BLOCK)
    dproj, dqkg = _dproj_assemble(dqa[0], dka[0], dva[0], dga, dgb, dq1[0], dk1[0], dv1[0], dq4, dk4, dv4,
                                  dq16, dk16, dv16, tqa, tqb, tkb, tka, qkg, cos4, sin4, bmean)
    gw_in = _weight_grad(h_t, dproj)

    blocks_in = gw_in
    blocks_out = gwo.reshape(N_DEV, SHARD_OUT, D_MODEL)
    g_sinks = jnp.concatenate([jnp.sum(dsink[:N_PAIRS, :BLOCK], axis=1), jnp.sum(dsink[:N_PAIRS, BLOCK:], axis=1)])
    small = _pack_small(jnp.zeros((D_MODEL,), F32), _fold_heads(dqkg[0:1]), _fold_heads(dqkg[1:2]), g_sinks,
                        _fold_heads(dqkg[2:3]), _fold_heads(dqkg[3:4]), extra=0.5 * jnp.sum(loss_cols) / D_MODEL)
    grad_x, g_w_in, g_w_out, small_red, dgain_red = _input_grad_reduce(
        dproj, w_full, x2, norm_gain, dy, blocks_in, blocks_out, small)
    n_gain_rows = D_MODEL // SMALL_COLS
    small_red = jnp.concatenate([dgain_red.reshape(n_gain_rows, SMALL_COLS), small_red[n_gain_rows:]], axis=0)
    g_small = _unpack_small(small_red)

    (d_in, nm_in, nv_in), (d_out, nm_out, nv_out), (d_s, nm_s, nv_s) = _adamw([
        (w_in_sh, g_w_in, m_w_in[0], v_w_in[0]),
        (w_out_sh, g_w_out, m_w_out[0], v_w_out[0]),
        (_pack_small(norm_gain, q_norm_a, k_norm_a, sinks_a, q_norm_b, k_norm_b), small_red,
         _pack_small(m_norm_gain, m_q_norm_a, m_k_norm_a, m_sinks_a, m_q_norm_b, m_k_norm_b),
         _pack_small(v_norm_gain, v_q_norm_a, v_k_norm_a, v_sinks_a, v_q_norm_b, v_k_norm_b))])
    d_small, nm_small, nv_small = _unpack_small(d_s), _unpack_small(nm_s), _unpack_small(nv_s)

    loss = small_red.reshape(-1)[SMALL_USED]

    def assemble(small_list, big_in, big_out):
        ng, qa_, ka_, sk_, qb_, kb_ = small_list
        return [ng, big_in[None], qa_, ka_, sk_, qb_, kb_, big_out[None]]

    return (loss, grad_x[None], *assemble(g_small, g_w_in, g_w_out), *assemble(d_small, d_in, d_out),
            *assemble(nm_small, nm_in, nm_out), *assemble(nv_small, nv_in, nv_out))
```

```python
import numpy as np
import jax
import jax.numpy as jnp
from jax import lax
from jax.experimental import pallas as pl
from jax.experimental.pallas import tpu as pltpu

F32 = jnp.float32
BF16 = jnp.bfloat16

SEQ = 4096
D_MODEL = 1024
HEAD_DIM = 64
PAIR = 2 * HEAD_DIM
N_PAIRS = 4
HALF_WIDTH = N_PAIRS * PAIR
KV_A_WIDTH = 128
IN_WIDTH = 3328
BLOCK = 128
STAT_REP = 16
STAT_WIDTH = 128
EPS = 1e-6
NEG = -1e30
ROPE_THETA = 10000.0
LOG2E = 1.4426950408889634
LN2 = 0.6931471805599453
Q_SCALE = HEAD_DIM ** -0.5 * LOG2E
N_DEV = 8
SHARD_IN = IN_WIDTH // N_DEV
SHARD_OUT = D_MODEL // N_DEV
SMALL_ROWS, SMALL_COLS = 8, 256

C_QA, C_KA, C_VA, C_GA, C_QB, C_KB, C_VB, C_GB = 0, 512, 640, 768, 1280, 1792, 2304, 2816

ADAM_LR = 0.001
ADAM_B1 = 0.9
ADAM_B2 = 0.999
ADAM_EPS = 1e-08
ADAM_WD = 0.01
ADAM_STEP = 10

ROW_TILE = 256
PROJ_ROW_TILE = 512
FWD_BLOCKS_PER_STEP = 8
BWD_BLOCKS_PER_STEP = 8
VMEM_LIMIT = 56 * 1024 * 1024

MESH = pl.DeviceIdType.MESH


def _params(sem, vmem=VMEM_LIMIT):
    return pltpu.CompilerParams(dimension_semantics=sem, vmem_limit_bytes=vmem)


def _head_sum(v, bm):
    return jnp.dot(v.astype(BF16), bm, preferred_element_type=F32)


def _swap_halves(y):
    lane = lax.broadcasted_iota(jnp.int32, y.shape, 1)
    first = (lane & 32) == 0
    return jnp.where(first, pltpu.roll(y, 96, 1), pltpu.roll(y, 32, 1))


def _sigmoid(g):
    return 1.0 / (1.0 + jnp.exp(-g))


def _tiles(a):
    return [a[:, j * PAIR:(j + 1) * PAIR] for j in range(N_PAIRS)]


def _pair_tiles(t):
    low = lax.broadcasted_iota(jnp.int32, t[0].shape, 1) < HEAD_DIM
    r = [pltpu.roll(a, HEAD_DIM, 1) for a in t]
    return [jnp.where(low, t[0], r[2]), jnp.where(low, r[0], t[2]), jnp.where(low, t[1], r[3]), jnp.where(low, r[1], t[3])]


def _unpair_tiles(p):
    low = lax.broadcasted_iota(jnp.int32, p[0].shape, 1) < HEAD_DIM
    r = [pltpu.roll(a, HEAD_DIM, 1) for a in p]
    return [jnp.where(low, p[0], r[1]), jnp.where(low, p[2], r[3]), jnp.where(low, r[0], p[1]), jnp.where(low, r[2], p[3])]


def _routes():
    x, y, c = lax.axis_index("x"), lax.axis_index("y"), lax.axis_index("c")
    north = c == 1
    first = (jnp.where(north, 1 - x, x), jnp.where(north, y, 1 - y), c)
    other = (jnp.where(north, x, 1 - x), jnp.where(north, 1 - y, y), c)
    k_first = jnp.where(north, 1, 2)
    return first, other, k_first, 3 - k_first


def _gather_plan(mine_ref, out_ref, send_sems, recv_sems):
    x, y, c = lax.axis_index("x"), lax.axis_index("y"), lax.axis_index("c")
    me, sibling, diag = (x, y, c), (x, y, 1 - c), (1 - x, 1 - y, c)
    first, other, k_first, k_other = _routes()

    def slot(px, py, pc):
        return out_ref.at[4 * px + 2 * py + pc]

    def copy(k, block, to, from_mine=False):
        return pltpu.make_async_remote_copy(
            src_ref=mine_ref if from_mine else slot(*block), dst_ref=slot(*block),
            send_sem=send_sems.at[k], recv_sem=recv_sems.at[k], device_id=to, device_id_type=MESH)

    sends = [copy(0, me, sibling, True), copy(1, me, (1 - x, y, c), True), copy(2, me, (x, 1 - y, c), True)]
    stages = [(copy(k_first, first, me), [copy(3, first, other), copy(3 + k_first, first, sibling)]),
              (copy(k_other, other, me), [copy(3 + k_other, other, sibling)]),
              (copy(3, diag, me), [copy(6, diag, sibling)])]
    from_sibling = [copy(0, sibling, me), copy(4, (1 - x, y, 1 - c), me), copy(5, (x, 1 - y, 1 - c), me),
                    copy(6, (1 - x, 1 - y, 1 - c), me)]
    return slot(*me), sends, stages, from_sibling


GATHER_SCRATCH = [pltpu.SemaphoreType.DMA((7,)), pltpu.SemaphoreType.DMA((7,))]


def _all_gather_w_in(w_in_sh):
    rows, cols = w_in_sh.shape

    def body(w_ref, out_ref, mine_ref, blocks, send_sems, recv_sems):
        mine_ref[...] = w_ref[...].astype(BF16)
        my_slot, sends, stages, from_sibling = _gather_plan(mine_ref, blocks, send_sems, recv_sems)
        for cp in sends:
            cp.start()
        my_slot[...] = mine_ref[...]
        for arrival, forwards in stages:
            arrival.wait_recv()
            for cp in forwards:
                cp.start()
        for arrival in from_sibling:
            arrival.wait_recv()
        for cp in sends + [cp for _, forwards in stages for cp in forwards]:
            cp.wait_send()
        for d in range(N_DEV):
            out_ref[:, d * cols:(d + 1) * cols] = blocks[d]

    vmem = pl.BlockSpec(memory_space=pltpu.VMEM)
    return pl.pallas_call(
        body, name="ag_w_in",
        out_shape=jax.ShapeDtypeStruct((rows, N_DEV * cols), BF16),
        in_specs=[vmem], out_specs=vmem,
        scratch_shapes=[pltpu.VMEM((rows, cols), BF16), pltpu.VMEM((N_DEV, rows, cols), BF16)] + GATHER_SCRATCH,
        compiler_params=pltpu.CompilerParams(vmem_limit_bytes=VMEM_LIMIT),
    )(w_in_sh)


def _fold_scratch(tm):
    return pltpu.VMEM((N_PAIRS, tm, PAIR), F32)


def _fold_store(val, scr, out4, out16, tm):
    groups = range(val.shape[1] // PAIR)
    for j in groups:
        scr[j] = val[:, j * PAIR:(j + 1) * PAIR]
    for dil, out in ((4, out4), (16, out16)):
        for r in range(dil):
            for j in groups:
                out[r, :, j * PAIR:(j + 1) * PAIR] = scr[j, pl.ds(r, tm // dil, stride=dil), :].astype(out.dtype)


def _unfold_load(src, scr, dil, tm):
    groups = range(src.shape[2] // PAIR)
    for r in range(dil):
        for j in groups:
            scr[j, pl.ds(r, tm // dil, stride=dil), :] = src[r, :, j * PAIR:(j + 1) * PAIR].astype(F32)
    return jnp.concatenate([scr[j] for j in groups], axis=1)


def _fold_specs(tm, dtype, width=HALF_WIDTH):
    shapes = (jax.ShapeDtypeStruct((4, SEQ // 4, width), dtype), jax.ShapeDtypeStruct((16, SEQ // 16, width), dtype))
    specs = (pl.BlockSpec((4, tm // 4, width), lambda i: (0, i, 0)),
             pl.BlockSpec((16, tm // 16, width), lambda i: (0, i, 0)))
    return shapes, specs


def _proj_fwd(x, gain, w, qkg, cos4, sin4, bmean, w_out_sh):
    tm = PROJ_ROW_TILE
    n_steps = SEQ // tm

    def norm_rope(t, g, cos, sin, bm, scale):
        rr = lax.rsqrt(_head_sum(t * t, bm) + EPS)
        yv = t * rr * g
        return (yv * cos + _swap_halves(yv) * sin) * scale

    def body(x_ref, g_ref, w_ref, qkg_ref, cos_ref, sin_ref, bm_ref, wo_ref,
             tqa_ref, tka_ref, tqb_ref, tkb_ref, ga_ref, gb_ref, ht_ref, qa_ref, ka_ref, va_ref, qb_ref, kb_ref, vb_ref,
             qb4_ref, qb16_ref, kb4_ref, kb16_ref, vb4_ref, vb16_ref, wo_all_ref,
             proj, scr, wo_mine, wo_all, send_sems, recv_sems):
        i = pl.program_id(0)
        my_slot, sends, stages, from_sibling = _gather_plan(wo_mine, wo_all, send_sems, recv_sems)

        @pl.when(i == 0)
        def _():
            wo_mine[...] = wo_ref[...].astype(BF16)
            for cp in sends:
                cp.start()
            my_slot[...] = wo_mine[...]

        @pl.when(i == n_steps // 2)
        def _():
            for arrival, forwards in stages[:2]:
                arrival.wait_recv()
                for cp in forwards:
                    cp.start()

        xf = x_ref[...]
        r = lax.rsqrt(jnp.mean(xf * xf, axis=-1, keepdims=True) + EPS)
        hf = xf * r * g_ref[...]
        ht_ref[...] = hf.T.astype(BF16)
        cos, sin, bm = cos_ref[...], sin_ref[...], bm_ref[...]
        proj[...] = jnp.dot(hf.astype(BF16), w_ref[...], preferred_element_type=F32)

        def roped(tiles, row, scale):
            g = qkg_ref[row:row + 1, :]
            return jnp.concatenate([norm_rope(t, g, cos, sin, bm, scale) for t in tiles], axis=1)

        tqa = _pair_tiles(_tiles(proj[:, C_QA:C_QA + HALF_WIDTH]))
        tqa_ref[...] = jnp.concatenate(tqa, axis=1).astype(BF16)
        qa_ref[...] = roped(tqa, 0, Q_SCALE).astype(BF16)
        ga_ref[...] = jnp.concatenate(_pair_tiles(_tiles(proj[:, C_GA:C_GA + HALF_WIDTH])), axis=1).astype(BF16)
        gb_ref[...] = proj[:, C_GB:C_GB + HALF_WIDTH].astype(BF16)
        tqb = proj[:, C_QB:C_QB + HALF_WIDTH]
        tqb_ref[...] = tqb.astype(BF16)
        qb = roped(_tiles(tqb), 2, Q_SCALE)
        qb_ref[...] = qb.astype(BF16)
        _fold_store(qb, scr, qb4_ref, qb16_ref, tm)
        tkb = proj[:, C_KB:C_KB + HALF_WIDTH]
        tkb_ref[...] = tkb.astype(BF16)
        kb = roped(_tiles(tkb), 3, 1.0)
        kb_ref[...] = kb.astype(BF16)
        _fold_store(kb, scr, kb4_ref, kb16_ref, tm)
        vb = proj[:, C_VB:C_VB + HALF_WIDTH]
        vb_ref[...] = vb.astype(BF16)
        _fold_store(vb, scr, vb4_ref, vb16_ref, tm)
        tka = proj[:, C_KA:C_KA + KV_A_WIDTH]
        tka_ref[...] = tka.astype(BF16)
        ka_ref[...] = roped([tka], 1, 1.0).astype(BF16)
        va_ref[...] = proj[:, C_VA:C_VA + KV_A_WIDTH].astype(BF16)

        @pl.when(i == n_steps - 1)
        def _():
            arrival, forwards = stages[2]
            arrival.wait_recv()
            for cp in forwards:
                cp.start()
            for arrival in from_sibling:
                arrival.wait_recv()
            for cp in sends + [cp for _, forwards in stages for cp in forwards]:
                cp.wait_send()
            wo_all_ref[...] = wo_all[...]

    row = lambda width: pl.BlockSpec((tm, width), lambda i: (i, 0))
    full = lambda a: pl.BlockSpec(a.shape, lambda i: (0,) * a.ndim)
    nat = lambda width, dtype=BF16: jax.ShapeDtypeStruct((SEQ, width), dtype)
    f_shapes, f_specs = _fold_specs(tm, BF16)
    return pl.pallas_call(
        body, name="proj_fwd", grid=(SEQ // tm,),
        in_specs=[row(D_MODEL), full(gain), full(w), full(qkg), row(PAIR), row(PAIR), full(bmean), full(w_out_sh)],
        out_specs=(row(HALF_WIDTH), row(KV_A_WIDTH), row(HALF_WIDTH), row(HALF_WIDTH), row(HALF_WIDTH), row(HALF_WIDTH),
                   pl.BlockSpec((D_MODEL, tm), lambda i: (0, i)),
                   row(HALF_WIDTH), row(KV_A_WIDTH), row(KV_A_WIDTH), row(HALF_WIDTH), row(HALF_WIDTH), row(HALF_WIDTH),
                   *f_specs, *f_specs, *f_specs,
                   pl.BlockSpec((N_DEV,) + w_out_sh.shape, lambda i: (0, 0, 0))),
        out_shape=(nat(HALF_WIDTH), nat(KV_A_WIDTH), nat(HALF_WIDTH), nat(HALF_WIDTH), nat(HALF_WIDTH), nat(HALF_WIDTH),
                   jax.ShapeDtypeStruct((D_MODEL, SEQ), BF16),
                   nat(HALF_WIDTH), nat(KV_A_WIDTH), nat(KV_A_WIDTH), nat(HALF_WIDTH), nat(HALF_WIDTH), nat(HALF_WIDTH),
                   *f_shapes, *f_shapes, *f_shapes,
                   jax.ShapeDtypeStruct((N_DEV,) + w_out_sh.shape, BF16)),
        scratch_shapes=[pltpu.VMEM((tm, IN_WIDTH), F32), _fold_scratch(tm), pltpu.VMEM(w_out_sh.shape, BF16),
                        pltpu.VMEM((N_DEV,) + w_out_sh.shape, BF16)] + GATHER_SCRATCH,
        compiler_params=_params(("arbitrary",)),
    )(x, gain, w, qkg, cos4, sin4, bmean, w_out_sh)


def _fill_band_bias(bias_ref, max_dist):
    j = lax.broadcasted_iota(jnp.int32, (2 * BLOCK, 2 * BLOCK), 0)
    c = lax.broadcasted_iota(jnp.int32, (2 * BLOCK, 2 * BLOCK), 1)
    dist = (c & (BLOCK - 1)) + BLOCK - j
    band = (dist >= 0) & (dist <= max_dist)
    bias_ref[0] = jnp.where(band, 0.0, NEG)
    bias_ref[1] = jnp.where(band & (j >= BLOCK), 0.0, NEG)


def _band_bias(bias_ref, step, b, qb, nb):
    if nb < qb:
        return bias_ref[1 if b % nb == 0 else 0]
    if b > 0:
        return bias_ref[0]
    return bias_ref[jnp.where(((step * qb) & (nb - 1)) == 0, 1, 0)]


def _stack_heads(t):
    lane = lax.broadcasted_iota(jnp.int32, t.shape, 1)
    low = lane < HEAD_DIM
    zero = jnp.zeros_like(t)
    return jnp.concatenate([jnp.where(low, t, zero), jnp.where(low, zero, t)], axis=0)


def _stack_heads_t(t):
    tt = t.astype(F32).T
    low = lax.broadcasted_iota(jnp.int32, tt.shape, 0) < HEAD_DIM
    zero = jnp.zeros_like(tt)
    return jnp.concatenate([jnp.where(low, tt, zero), jnp.where(low, zero, tt)], axis=1).astype(BF16)


def _unstack_t(t):
    return jnp.concatenate([t[:HEAD_DIM, :BLOCK], t[HEAD_DIM:, BLOCK:]], axis=0).T


def _rows_to_stats(rows):
    parts = []
    for row in rows:
        parts.append(jnp.broadcast_to(row[:, :BLOCK], (STAT_REP, BLOCK)))
        parts.append(jnp.broadcast_to(row[:, BLOCK:], (STAT_REP, BLOCK)))
    return jnp.concatenate(parts, axis=0).T


def _stats_to_rows(t):
    tt = t.T
    return [jnp.concatenate([tt[2 * p * STAT_REP:2 * p * STAT_REP + 1, :],
                             tt[(2 * p + 1) * STAT_REP:(2 * p + 1) * STAT_REP + 1, :]], axis=1) for p in range(N_PAIRS)]


def _attn_fwd(name, patterns):
    qb = FWD_BLOCKS_PER_STEP
    steps = SEQ // (qb * BLOCK)

    def one_pattern(step, nb, shared, max_dist, q_ref, kc_ref, vc_ref, sink_ref, o_ref, lse_ref, kp_ref, vp_ref, bias_ref):
        has_sinks = sink_ref is not None

        @pl.when(step == 0)
        def _():
            kp_ref[...] = jnp.zeros_like(kp_ref)
            vp_ref[...] = jnp.zeros_like(vp_ref)
            _fill_band_bias(bias_ref, max_dist)

        cols = [slice(p * PAIR, (p + 1) * PAIR) for p in range(N_PAIRS)]
        kcols = [slice(0, PAIR) if shared else c for c in cols]
        rows = [slice(b * BLOCK, (b + 1) * BLOCK) for b in range(qb)]
        units = [(b, p) for b in range(qb) for p in range(N_PAIRS)]
        n = range(len(units))

        def window(prev_ref, cur_ref, b, kc):
            before = prev_ref[:, kc] if b == 0 else cur_ref[rows[b - 1], kc]
            return jnp.concatenate([before, cur_ref[rows[b], kc]], axis=0)

        st = [jnp.dot(window(kp_ref, kc_ref, b, kcols[p]), _stack_heads_t(q_ref[rows[b], cols[p]]),
                      preferred_element_type=F32) for b, p in units]
        st = [st[u] + _band_bias(bias_ref, step, units[u][0], qb, nb) for u in n]
        m = [jnp.max(s, axis=0, keepdims=True) for s in st]
        if has_sinks:
            sk = [sink_ref[p:p + 1, :] for _, p in units]
            m = [jnp.maximum(m[u], sk[u]) for u in n]
        pt = [jnp.exp2(st[u] - m[u]) for u in n]
        l = [jnp.sum(t, axis=0, keepdims=True) for t in pt]
        if has_sinks:
            l = [l[u] + jnp.exp2(sk[u] - m[u]) for u in n]
        v2t = [window(vp_ref, vc_ref, b, kcols[p]).astype(F32).T.astype(BF16) for b, p in units]
        ot = [jnp.dot(v2t[u], pt[u].astype(BF16), preferred_element_type=F32) / l[u] for u in n]
        for u, (b, p) in enumerate(units):
            o_ref[rows[b], cols[p]] = _unstack_t(ot[u]).astype(BF16)
        for b in range(qb):
            lse_ref[rows[b], :] = _rows_to_stats([m[u] + jnp.log2(l[u]) for u in n if units[u][0] == b])
        kp_ref[...] = kc_ref[rows[-1], :]
        vp_ref[...] = vc_ref[rows[-1], :]

    n_in = [4 if sinks is not None else 3 for _, _, _, sinks, _ in patterns]

    def body(*refs):
        ins, rest = refs[:sum(n_in)], refs[sum(n_in):]
        outs, scratch = rest[:2 * len(patterns)], rest[2 * len(patterns):]
        bias_ref = scratch[-1]
        step = pl.program_id(0)
        first = 0
        for p, (q, k, _, sinks, max_dist) in enumerate(patterns):
            mine = ins[first:first + n_in[p]]
            first += n_in[p]
            sink_ref = mine[3] if sinks is not None else None
            kp_ref, vp_ref = scratch[2 * p], scratch[2 * p + 1]

            @pl.when((step >= p * steps) & (step < (p + 1) * steps))
            def _():
                one_pattern(step - p * steps, q.shape[1] // BLOCK, k.shape[2] == PAIR, max_dist,
                            mine[0], mine[1], mine[2], sink_ref, outs[2 * p], outs[2 * p + 1], kp_ref, vp_ref, bias_ref)

    def during(p, width):
        return pl.BlockSpec((qb * BLOCK, width), lambda s: (jnp.clip(s - p * steps, 0, steps - 1), 0))

    flat = lambda a: a.reshape(SEQ, a.shape[2])
    in_specs, args, out_specs, out_shape, scratch = [], [], [], [], []
    for p, (q, k, v, sinks, _) in enumerate(patterns):
        ck = k.shape[2]
        in_specs += [during(p, HALF_WIDTH), during(p, ck), during(p, ck)]
        args += [flat(q), flat(k), flat(v)]
        if sinks is not None:
            in_specs.append(pl.BlockSpec(sinks.shape, lambda s: (0, 0)))
            args.append(sinks)
        out_specs += [during(p, HALF_WIDTH), during(p, STAT_WIDTH)]
        out_shape += [jax.ShapeDtypeStruct((SEQ, HALF_WIDTH), BF16), jax.ShapeDtypeStruct((SEQ, STAT_WIDTH), F32)]
        scratch += [pltpu.VMEM((BLOCK, ck), BF16), pltpu.VMEM((BLOCK, ck), BF16)]
    outs = pl.pallas_call(
        body, name=name, grid=(len(patterns) * steps,), in_specs=in_specs,
        out_specs=tuple(out_specs), out_shape=tuple(out_shape),
        scratch_shapes=scratch + [pltpu.VMEM((2, 2 * BLOCK, 2 * BLOCK), F32)],
        compiler_params=_params(("arbitrary",)),
    )(*args)
    return [(outs[2 * p].reshape(q.shape), outs[2 * p + 1].reshape(q.shape[0], q.shape[1], STAT_WIDTH))
            for p, (q, _, _, _, _) in enumerate(patterns)]


def _attn_bwd(name, q, k, v, d_o, lse, delta, sink_rows, max_dist):
    n_seq, length, _ = q.shape
    ck = k.shape[2]
    nb = length // BLOCK
    n_blocks = n_seq * nb
    n_rows = n_seq * length
    shared = ck == PAIR
    has_sinks = sink_rows is not None
    qb = BWD_BLOCKS_PER_STEP
    n_steps = n_blocks // qb

    def body(*refs):
        if has_sinks:
            (q_ref, kc_ref, vc_ref, do_ref, lse_ref, dl_ref, sink_ref,
             dq_ref, dk_ref, dv_ref, dsink_ref, ck_scr, cv_scr, kp_ref, vp_ref, bias_ref) = refs
        else:
            (q_ref, kc_ref, vc_ref, do_ref, lse_ref, dl_ref,
             dq_ref, dk_ref, dv_ref, ck_scr, cv_scr, kp_ref, vp_ref, bias_ref) = refs
        step = pl.program_id(0)

        @pl.when(step == 0)
        def _():
            ck_scr[...] = jnp.zeros_like(ck_scr)
            cv_scr[...] = jnp.zeros_like(cv_scr)
            kp_ref[...] = jnp.zeros_like(kp_ref)
            vp_ref[...] = jnp.zeros_like(vp_ref)
            if has_sinks:
                dsink_ref[...] = jnp.zeros_like(dsink_ref)
            _fill_band_bias(bias_ref, max_dist)

        cols = [slice(p * PAIR, (p + 1) * PAIR) for p in range(N_PAIRS)]
        kcols = [slice(0, PAIR) if shared else c for c in cols]
        rows = [slice(b * BLOCK, (b + 1) * BLOCK) for b in range(qb)]
        units = [(b, p) for b in range(qb) for p in range(N_PAIRS)]
        n = range(len(units))
        nt = (((1,), (1,)), ((), ()))

        def window(prev_ref, cur_ref, b, kc):
            before = prev_ref[:, kc] if b == 0 else cur_ref[rows[b - 1], kc]
            return jnp.concatenate([before, cur_ref[rows[b], kc]], axis=0)

        q_st = [_stack_heads(q_ref[rows[b], cols[p]]) for b, p in units]
        do_st = [_stack_heads(do_ref[rows[b], cols[p]]) for b, p in units]
        k2 = [window(kp_ref, kc_ref, b, kcols[p]) for b, p in units]
        v2 = [window(vp_ref, vc_ref, b, kcols[p]) for b, p in units]
        st = [lax.dot_general(k2[u], q_st[u], nt, preferred_element_type=F32) for u in n]
        dpt = [lax.dot_general(v2[u], do_st[u], nt, preferred_element_type=F32) for u in n]
        lse_rows = [_stats_to_rows(lse_ref[rows[b], :]) for b in range(qb)]
        dl_rows = [_stats_to_rows(dl_ref[rows[b], :]) for b in range(qb)]
        lse_row = [lse_rows[b][p] for b, p in units]
        dl_row = [dl_rows[b][p] for b, p in units]
        pt = [jnp.exp2(st[u] + _band_bias(bias_ref, step, units[u][0], qb, nb) - lse_row[u]) for u in n]
        dst = [(pt[u] * (dpt[u] - dl_row[u])).astype(BF16) for u in n]
        ptb = [t.astype(BF16) for t in pt]
        dv2 = [jnp.dot(ptb[u], do_st[u], preferred_element_type=F32) for u in n]
        dk2 = [jnp.dot(dst[u], q_st[u], preferred_element_type=F32) for u in n]
        k2t = [k2[u].astype(F32).T.astype(BF16) for u in n]
        dqt = [jnp.dot(k2t[u], dst[u], preferred_element_type=F32) for u in n]
        for u, (b, p) in enumerate(units):
            dq_ref[rows[b], cols[p]] = _unstack_t(dqt[u]).astype(BF16)
        if has_sinks:
            for u, (b, p) in enumerate(units):
                p_sink = jnp.exp2(sink_ref[p:p + 1, :] - lse_row[u])
                dsink_ref[p:p + 1, :] = dsink_ref[p:p + 1, :] - p_sink * dl_row[u]

        def total(parts, w, group):
            sel = [u for u, (b, p) in enumerate(units) if (shared or p == group)]
            terms = ([parts[u][:BLOCK] for u in sel if units[u][0] == w]
                     + [parts[u][BLOCK:] for u in sel if units[u][0] == w - 1])
            tot = terms[0]
            for t in terms[1:]:
                tot = tot + t
            return tot

        first_row = step * (qb * BLOCK)
        for acc_ref, out_ref, parts in ((ck_scr, dk_ref, dk2), (cv_scr, dv_ref, dv2)):
            for group in range(1 if shared else N_PAIRS):
                kc = kcols[group]

                @pl.when(step > 0)
                def _():
                    out_ref[pl.ds(pl.multiple_of(first_row - BLOCK, BLOCK), BLOCK), kc] = (
                        acc_ref[:, kc] + total(parts, 0, group)).astype(BF16)

                for w in range(1, qb):
                    out_ref[pl.ds(pl.multiple_of(first_row + (w - 1) * BLOCK, BLOCK), BLOCK), kc] = (
                        total(parts, w, group).astype(BF16))
                acc_ref[:, kc] = total(parts, qb, group)

        @pl.when(step == n_steps - 1)
        def _():
            dk_ref[pl.ds(n_rows - BLOCK, BLOCK), :] = ck_scr[...].astype(BF16)
            dv_ref[pl.ds(n_rows - BLOCK, BLOCK), :] = cv_scr[...].astype(BF16)

        kp_ref[...] = kc_ref[rows[-1], :]
        vp_ref[...] = vc_ref[rows[-1], :]

    cur = lambda width: pl.BlockSpec((qb * BLOCK, width), lambda s: (s, 0))
    whole = lambda width: pl.BlockSpec((n_rows, width), lambda s: (0, 0))
    flat = lambda a: a.reshape(n_rows, a.shape[2])
    in_specs = [cur(HALF_WIDTH), cur(ck), cur(ck), cur(HALF_WIDTH), cur(STAT_WIDTH), cur(STAT_WIDTH)]
    args = [flat(a) for a in (q, k, v, d_o, lse, delta)]
    out_specs = [cur(HALF_WIDTH), whole(ck), whole(ck)]
    out_shape = [jax.ShapeDtypeStruct((n_rows, HALF_WIDTH), BF16),
                 jax.ShapeDtypeStruct((n_rows, ck), BF16), jax.ShapeDtypeStruct((n_rows, ck), BF16)]
    if has_sinks:
        in_specs.append(pl.BlockSpec(sink_rows.shape, lambda s: (0, 0)))
        args.append(sink_rows)
        out_specs.append(pl.BlockSpec(sink_rows.shape, lambda s: (0, 0)))
        out_shape.append(jax.ShapeDtypeStruct(sink_rows.shape, F32))
    outs = pl.pallas_call(
        body, name=name, grid=(n_steps,), in_specs=in_specs,
        out_specs=tuple(out_specs), out_shape=tuple(out_shape),
        scratch_shapes=[pltpu.VMEM((BLOCK, ck), F32), pltpu.VMEM((BLOCK, ck), F32),
                        pltpu.VMEM((BLOCK, ck), BF16), pltpu.VMEM((BLOCK, ck), BF16),
                        pltpu.VMEM((2, 2 * BLOCK, 2 * BLOCK), F32)],
        compiler_params=_params(("arbitrary",)),
    )(*args)
    return tuple(o.reshape(n_seq, length, o.shape[1]) for o in outs[:3]) + tuple(outs[3:])


def _tail(oa, ob1, lb1, ob4, lb4, ob16, lb16, gate_a, gate_b, x, target, w_out, spread, gather):
    tm = ROW_TILE

    def split_dot(v, mat):
        hi = v.astype(BF16)
        lo = (v - hi.astype(F32)).astype(BF16)
        return jnp.dot(hi, mat, preferred_element_type=F32) + jnp.dot(lo, mat, preferred_element_type=F32)

    def body(oa_ref, ob1_ref, lb1_ref, ob4_ref, lb4_ref, ob16_ref, lb16_ref, ga_ref, gb_ref, x_ref, t_ref, w_ref,
             sp_ref, ga_mat_ref,
             loss_ref, dy_ref, gwo_ref, doa_ref, dla_ref, dga_ref, dgb_ref,
             dob_ref, dob4_ref, dob16_ref, dlb_ref, dlb4_ref, dlb16_ref, lse_ref, lse4_ref, lse16_ref,
             s_f, mix_keep, dy_keep):
        i = pl.program_id(0)
        sp, gat = sp_ref[...], ga_mat_ref[...]
        o4, o16 = _unfold_load(ob4_ref, s_f, 4, tm), _unfold_load(ob16_ref, s_f, 16, tm)
        l4, l16 = _unfold_load(lb4_ref, s_f, 4, tm), _unfold_load(lb16_ref, s_f, 16, tm)
        o1, l1 = ob1_ref[...].astype(F32), lb1_ref[...]
        mx = jnp.maximum(jnp.maximum(l1, l4), l16)
        e1, e4, e16 = jnp.exp2(l1 - mx), jnp.exp2(l4 - mx), jnp.exp2(l16 - mx)
        den = e1 + e4 + e16
        inv = 1.0 / den
        ob = split_dot(e1 * inv, sp) * o1 + split_dot(e4 * inv, sp) * o4 + split_dot(e16 * inv, sp) * o16
        lse_b = mx + jnp.log2(den)

        oa, ga, gb = oa_ref[...].astype(F32), ga_ref[...].astype(F32), gb_ref[...].astype(F32)
        sa, sb = _sigmoid(ga), _sigmoid(gb)
        mixed = jnp.concatenate(_unpair_tiles(_tiles(oa * (ga * sa))) + [ob * (gb * sb)], axis=1)
        mixed_bf = mixed.astype(BF16)
        w = w_ref[...]
        yv = x_ref[...] + jnp.dot(mixed_bf, w, preferred_element_type=F32)
        err = yv - t_ref[...]
        sq = jnp.sum(err * err, axis=0, keepdims=True)
        dy = err * (1.0 / D_MODEL)
        dy_ref[...] = dy
        dy_bf = dy.astype(BF16)
        mix_t = mixed.T.astype(BF16)

        @pl.when(i == 0)
        def _():
            loss_ref[...] = sq

        @pl.when(i > 0)
        def _():
            loss_ref[...] += sq

        @pl.when((i & 1) == 0)
        def _():
            mix_keep[...] = mix_t
            dy_keep[...] = dy_bf

        @pl.when((i & 1) == 1)
        def _():
            gw = jnp.dot(jnp.concatenate([mix_keep[...], mix_t], axis=1), jnp.concatenate([dy_keep[...], dy_bf], axis=0),
                         preferred_element_type=F32)

            @pl.when(i == 1)
            def _():
                gwo_ref[...] = gw

            @pl.when(i > 1)
            def _():
                gwo_ref[...] += gw

        dmix = lax.dot_general(dy_bf, w, (((1,), (1,)), ((), ())), preferred_element_type=F32)
        dma = jnp.concatenate(_pair_tiles(_tiles(dmix[:, :HALF_WIDTH])), axis=1)
        dmb = dmix[:, HALF_WIDTH:]

        doa = dma * (ga * sa)
        doa_ref[...] = doa.astype(BF16)
        dla_ref[...] = split_dot(doa * oa, gat)
        dga_ref[...] = (dma * oa * (sa * (1.0 + ga * (1.0 - sa)))).astype(BF16)
        dob = dmb * (gb * sb)
        dgb_ref[...] = (dmb * ob * (sb * (1.0 + gb * (1.0 - sb)))).astype(BF16)
        dlb = split_dot(dob * ob, gat)
        dob_ref[...] = dob.astype(BF16)
        _fold_store(dob, s_f, dob4_ref, dob16_ref, tm)
        dlb_ref[...] = dlb
        _fold_store(dlb, s_f, dlb4_ref, dlb16_ref, tm)
        lse_ref[...] = lse_b
        _fold_store(lse_b, s_f, lse4_ref, lse16_ref, tm)

    row = lambda width: pl.BlockSpec((tm, width), lambda i: (i, 0))
    full = lambda a: pl.BlockSpec(a.shape, lambda i: (0,) * a.ndim)
    fb_shapes, fb_specs = _fold_specs(tm, BF16)
    _, ff_specs = _fold_specs(tm, F32)
    st_shapes, st_specs = _fold_specs(tm, F32, STAT_WIDTH)
    nat = lambda dtype, width=HALF_WIDTH: jax.ShapeDtypeStruct((SEQ, width), dtype)
    return pl.pallas_call(
        body, name="tail", grid=(SEQ // tm,),
        in_specs=[row(HALF_WIDTH), row(HALF_WIDTH), row(STAT_WIDTH), ff_specs[0], st_specs[0], ff_specs[1], st_specs[1],
                  row(HALF_WIDTH), row(HALF_WIDTH), row(D_MODEL), row(D_MODEL), full(w_out), full(spread), full(gather)],
        out_specs=(pl.BlockSpec((1, D_MODEL), lambda i: (0, 0)), row(D_MODEL),
                   pl.BlockSpec((D_MODEL, D_MODEL), lambda i: (0, 0)),
                   row(HALF_WIDTH), row(STAT_WIDTH), row(HALF_WIDTH), row(HALF_WIDTH),
                   row(HALF_WIDTH), *fb_specs, row(STAT_WIDTH), *st_specs, row(STAT_WIDTH), *st_specs),
        out_shape=(jax.ShapeDtypeStruct((1, D_MODEL), F32), jax.ShapeDtypeStruct((SEQ, D_MODEL), F32),
                   jax.ShapeDtypeStruct((D_MODEL, D_MODEL), F32),
                   nat(BF16), nat(F32, STAT_WIDTH), nat(BF16), nat(BF16),
                   nat(BF16), *fb_shapes, nat(F32, STAT_WIDTH), *st_shapes, nat(F32, STAT_WIDTH), *st_shapes),
        scratch_shapes=[_fold_scratch(tm), pltpu.VMEM((D_MODEL, tm), BF16), pltpu.VMEM((tm, D_MODEL), BF16)],
        compiler_params=_params(("arbitrary",)),
    )(oa, ob1, lb1, ob4, lb4, ob16, lb16, gate_a, gate_b, x, target, w_out, spread, gather)


def _dproj_assemble(dqa, dka, dva, dga, dgb, dq1, dk1, dv1, dq4, dk4, dv4, dq16, dk16, dv16, tqa, tqb, tkb, tka,
                    qkg, cos4, sin4, bmean):
    tm = ROW_TILE

    def norm_rope_bwd(d_out, t, g, cos, sin, bm, scale):
        d_r = d_out * scale
        dyv = d_r * cos + _swap_halves(d_r * sin)
        rr = lax.rsqrt(_head_sum(t * t, bm) + EPS)
        that = t * rr
        dgain = jnp.sum(dyv * that, axis=0, keepdims=True)
        gdy = dyv * g
        dt = rr * (gdy - that * _head_sum(that * gdy, bm))
        return dt, dgain

    def body(dqa_ref, dka_ref, dva_ref, dga_ref, dgb_ref, dq1_ref, dk1_ref, dv1_ref, dq4_ref, dk4_ref, dv4_ref,
             dq16_ref, dk16_ref, dv16_ref, tqa_ref, tqb_ref, tkb_ref, tka_ref, qkg_ref, cos_ref, sin_ref, bm_ref,
             dproj_ref, dqkg_ref, s_f):
        i = pl.program_id(0)
        cos, sin, bm = cos_ref[...], sin_ref[...], bm_ref[...]

        def merged(nat_ref, f4_ref, f16_ref):
            return nat_ref[...].astype(F32) + _unfold_load(f4_ref, s_f, 4, tm) + _unfold_load(f16_ref, s_f, 16, tm)

        @pl.when(i == 0)
        def _():
            dqkg_ref[...] = jnp.zeros_like(dqkg_ref)

        def through(d_out, t, row, scale, c0, paired=False):
            g = qkg_ref[row:row + 1, :]
            tot = jnp.zeros((1, PAIR), F32)
            dts = []
            for j in range(d_out.shape[1] // PAIR):
                cols = slice(j * PAIR, (j + 1) * PAIR)
                dt, dg = norm_rope_bwd(d_out[:, cols], t[:, cols], g, cos, sin, bm, scale)
                dts.append(dt)
                tot = tot + dg
            if paired:
                dts = _unpair_tiles(dts)
            for j, dt in enumerate(dts):
                dproj_ref[:, c0 + j * PAIR:c0 + (j + 1) * PAIR] = dt.astype(BF16)
            dqkg_ref[row:row + 1, :] += tot

        through(dqa_ref[...].astype(F32), tqa_ref[...].astype(F32), 0, HEAD_DIM ** -0.5, C_QA, paired=True)
        through(dka_ref[...].astype(F32), tka_ref[...].astype(F32), 1, LN2, C_KA)
        through(merged(dq1_ref, dq4_ref, dq16_ref), tqb_ref[...].astype(F32), 2, HEAD_DIM ** -0.5, C_QB)
        through(merged(dk1_ref, dk4_ref, dk16_ref), tkb_ref[...].astype(F32), 3, LN2, C_KB)
        dproj_ref[:, C_VB:C_VB + HALF_WIDTH] = merged(dv1_ref, dv4_ref, dv16_ref).astype(BF16)
        dproj_ref[:, C_GA:C_GA + HALF_WIDTH] = jnp.concatenate(
            _unpair_tiles(_tiles(dga_ref[...].astype(F32))), axis=1).astype(BF16)
        dproj_ref[:, C_GB:C_GB + HALF_WIDTH] = dgb_ref[...].astype(BF16)
        dproj_ref[:, C_VA:C_VA + KV_A_WIDTH] = dva_ref[...].astype(BF16)

    row = lambda width: pl.BlockSpec((tm, width), lambda i: (i, 0))
    full = lambda a: pl.BlockSpec(a.shape, lambda i: (0,) * a.ndim)
    _, ff_specs = _fold_specs(tm, F32)
    return pl.pallas_call(
        body, name="dproj_assemble", grid=(SEQ // tm,),
        in_specs=[row(HALF_WIDTH), row(KV_A_WIDTH), row(KV_A_WIDTH), row(HALF_WIDTH), row(HALF_WIDTH),
                  row(HALF_WIDTH), row(HALF_WIDTH), row(HALF_WIDTH), ff_specs[0], ff_specs[0], ff_specs[0],
                  ff_specs[1], ff_specs[1], ff_specs[1],
                  row(HALF_WIDTH), row(HALF_WIDTH), row(HALF_WIDTH), row(KV_A_WIDTH),
                  full(qkg), row(PAIR), row(PAIR), full(bmean)],
        out_specs=(row(IN_WIDTH), pl.BlockSpec((SMALL_ROWS, PAIR), lambda i: (0, 0))),
        out_shape=(jax.ShapeDtypeStruct((SEQ, IN_WIDTH), BF16), jax.ShapeDtypeStruct((SMALL_ROWS, PAIR), F32)),
        scratch_shapes=[_fold_scratch(tm)],
        compiler_params=_params(("arbitrary",)),
    )(dqa, dka, dva, dga, dgb, dq1, dk1, dv1, dq4, dk4, dv4, dq16, dk16, dv16, tqa, tqb, tkb, tka, qkg, cos4, sin4, bmean)


def _input_grad_reduce(dproj, w, x, gain, dy, blocks_in, blocks_out, small):
    tm = ROW_TILE
    n_steps = SEQ // tm
    stage2_step, stage3_step = 3, 7
    shapes = (blocks_in.shape[1:], blocks_out.shape[1:])

    def body(dp_ref, w_ref, x_ref, g_ref, dy_ref, ga_hbm, gb_hbm, small_ref,
             gx_ref, out_a, out_b, small_out_ref, dgain_out_ref,
             part_a, part_b, sib_a, sib_b, wire_a, wire_b, chips_a, chips_b, small_all, dgain_acc, dgain_all,
             load_sems, sib_send, sib_recv, chip_send, chip_recv, small_send, small_recv, dgain_send, dgain_recv):
        i = pl.program_id(0)
        x, y, c = lax.axis_index("x"), lax.axis_index("y"), lax.axis_index("c")
        sibling = (x, y, 1 - c)
        chips = [(x, y), (1 - x, y), (x, 1 - y), (1 - x, 1 - y)]
        my_id = 4 * x + 2 * y + c
        g_hbm, part, from_sib = (ga_hbm, gb_hbm), (part_a, part_b), (sib_a, sib_b)
        to_wire, from_chips, out = (wire_a, wire_b), (chips_a, chips_b), (out_a, out_b)
        both = (0, 1)

        def blk(a, chip, core):
            return g_hbm[a].at[4 * chip[0] + 2 * chip[1] + core]

        def to_all(src, dst_all, send, recv):
            copies = []
            for rel in range(1, N_DEV):
                dx, dy_, dc = (rel >> 2) & 1, (rel >> 1) & 1, rel & 1
                to = (1 - x if dx else x, 1 - y if dy_ else y, 1 - c if dc else c)
                copies.append(pltpu.make_async_remote_copy(
                    src_ref=src, dst_ref=dst_all.at[my_id], send_sem=send.at[rel - 1], recv_sem=recv.at[rel - 1],
                    device_id=to, device_id_type=MESH))
            return copies

        small_copies = to_all(small_all.at[my_id], small_all, small_send, small_recv)
        dgain_copies = to_all(dgain_acc, dgain_all, dgain_send, dgain_recv)
        loads = [[pltpu.make_async_copy(blk(a, chips[k], c), part[a].at[k], load_sems.at[a, k]) for k in range(4)] for a in both]
        to_sib = [[pltpu.make_async_remote_copy(
            src_ref=blk(a, chips[k], 1 - c), dst_ref=from_sib[a].at[k], send_sem=sib_send.at[a, k], recv_sem=sib_recv.at[a, k],
            device_id=sibling, device_id_type=MESH) for k in range(4)] for a in both]
        first, other, k_first, k_other = _routes()
        to_chips = [[pltpu.make_async_remote_copy(
            src_ref=to_wire[a].at[s], dst_ref=from_chips[a].at[s], send_sem=chip_send.at[a, s], recv_sem=chip_recv.at[a, s],
            device_id=(first, first, other)[s], device_id_type=MESH) for s in range(3)] for a in both]

        def chip_partial(a, k):
            return part[a][k].astype(F32) + from_sib[a][k].astype(F32)

        @pl.when(i == 0)
        def _():
            small_all[my_id] = small_ref[...]
            for cp in small_copies:
                cp.start()
            for k in (1, 2, 3, 0):
                for a in both:
                    loads[a][k].start()
                    to_sib[a][k].start()

        @pl.when(i == stage2_step)
        def _():
            for k in (1, 2, 3):
                for a in both:
                    loads[a][k].wait()
                    to_sib[a][k].wait_recv()
            for s, k in ((0, 3), (1, k_first)):
                for a in both:
                    to_wire[a][s] = chip_partial(a, k).astype(BF16)
                    to_chips[a][s].start()

        @pl.when(i == stage3_step)
        def _():
            for a in both:
                to_chips[a][0].wait_recv()
                to_wire[a][2] = (chip_partial(a, k_other) + from_chips[a][0].astype(F32)).astype(BF16)
                to_chips[a][2].start()

        dh = lax.dot_general(dp_ref[...], w_ref[...], (((1,), (1,)), ((), ())), preferred_element_type=F32)
        xf = x_ref[...]
        r = lax.rsqrt(jnp.mean(xf * xf, axis=-1, keepdims=True) + EPS)
        xhat = xf * r
        dg = jnp.sum(dh * xhat, axis=0, keepdims=True)
        dxh = dh * g_ref[...]
        dx = r * (dxh - xhat * jnp.mean(dxh * xhat, axis=-1, keepdims=True))
        gx_ref[...] = dy_ref[...] + dx

        @pl.when(i == 0)
        def _():
            dgain_acc[...] = dg

        @pl.when(i > 0)
        def _():
            dgain_acc[...] += dg

        @pl.when(i == n_steps - 1)
        def _():
            dgain_all[my_id] = dgain_acc[...]
            for cp in dgain_copies:
                cp.start()
            for a in both:
                loads[a][0].wait()
                to_sib[a][0].wait_recv()
                acc = chip_partial(a, 0)
                for s in (1, 2):
                    to_chips[a][s].wait_recv()
                    acc = acc + from_chips[a][s].astype(F32)
                out[a][...] = acc
            for copies, gathered, dst in ((small_copies, small_all, small_out_ref), (dgain_copies, dgain_all, dgain_out_ref)):
                for cp in copies:
                    cp.wait_recv()
                tot = gathered[0]
                for d in range(1, N_DEV):
                    tot = tot + gathered[d]
                dst[...] = tot
            for cp in to_sib[0] + to_sib[1] + to_chips[0] + to_chips[1] + small_copies + dgain_copies:
                cp.wait_send()

    row = lambda width: pl.BlockSpec((tm, width), lambda i: (i, 0))
    full = lambda a: pl.BlockSpec(a.shape, lambda i: (0,) * a.ndim)
    whole = lambda shape: pl.BlockSpec(shape, lambda i: (0,) * len(shape))
    hbm = pl.BlockSpec(memory_space=pl.ANY)
    dtypes = (blocks_in.dtype, blocks_out.dtype)
    buf = lambda n, dts: [pltpu.VMEM((n,) + s, dt) for s, dt in zip(shapes, dts)]
    return pl.pallas_call(
        body, name="input_grad_rs", grid=(n_steps,),
        in_specs=[row(IN_WIDTH), full(w), row(D_MODEL), full(gain), row(D_MODEL), hbm, hbm, full(small)],
        out_specs=(row(D_MODEL), whole(shapes[0]), whole(shapes[1]), whole((SMALL_ROWS, SMALL_COLS)), whole((1, D_MODEL))),
        out_shape=(jax.ShapeDtypeStruct((SEQ, D_MODEL), F32), jax.ShapeDtypeStruct(shapes[0], F32),
                   jax.ShapeDtypeStruct(shapes[1], F32), jax.ShapeDtypeStruct((SMALL_ROWS, SMALL_COLS), F32),
                   jax.ShapeDtypeStruct((1, D_MODEL), F32)),
        scratch_shapes=[*buf(4, dtypes), *buf(4, dtypes), *buf(3, (BF16, BF16)), *buf(3, (BF16, BF16)),
                        pltpu.VMEM((N_DEV, SMALL_ROWS, SMALL_COLS), F32),
                        pltpu.VMEM((1, D_MODEL), F32), pltpu.VMEM((N_DEV, 1, D_MODEL), F32),
                        pltpu.SemaphoreType.DMA((2, 4)), pltpu.SemaphoreType.DMA((2, 4)), pltpu.SemaphoreType.DMA((2, 4)),
                        pltpu.SemaphoreType.DMA((2, 3)), pltpu.SemaphoreType.DMA((2, 3)),
                        pltpu.SemaphoreType.DMA((7,)), pltpu.SemaphoreType.DMA((7,)),
                        pltpu.SemaphoreType.DMA((7,)), pltpu.SemaphoreType.DMA((7,))],
        compiler_params=_params(("arbitrary",)),
    )(dproj, w, x, gain, dy, blocks_in, blocks_out, small)


def _weight_grad(h_t, dproj):
    tk = 1024
    cb = IN_WIDTH // 2
    n_k = SEQ // tk

    def body(ht_ref, dp_ref, out_ref, acc):
        k = pl.program_id(1)
        upd = jnp.dot(ht_ref[...], dp_ref[...], preferred_element_type=F32)

        @pl.when(k == 0)
        def _():
            acc[...] = upd

        @pl.when(k > 0)
        def _():
            acc[...] += upd

        @pl.when(k == n_k - 1)
        def _():
            for b in range(N_DEV // 2):
                out_ref[b] = acc[:, b * SHARD_IN:(b + 1) * SHARD_IN].astype(BF16)

    return pl.pallas_call(
        body, name="weight_grad", grid=(2, n_k),
        in_specs=[pl.BlockSpec((D_MODEL, tk), lambda j, k: (0, k)), pl.BlockSpec((tk, cb), lambda j, k: (k, j))],
        out_specs=pl.BlockSpec((N_DEV // 2, D_MODEL, SHARD_IN), lambda j, k: (j, 0, 0)),
        out_shape=jax.ShapeDtypeStruct((N_DEV, D_MODEL, SHARD_IN), BF16),
        scratch_shapes=[pltpu.VMEM((D_MODEL, cb), F32)],
        compiler_params=_params(("arbitrary", "arbitrary")),
    )(h_t, dproj)


def _adamw(groups):
    def body(*refs):
        ins, outs = refs[:4 * len(groups)], refs[4 * len(groups):]
        for i in range(len(groups)):
            w_ref, g_ref, m_ref, v_ref = ins[4 * i:4 * i + 4]
            d_ref, nm_ref, nv_ref = outs[3 * i:3 * i + 3]
            gv = g_ref[...]
            nm = ADAM_B1 * m_ref[...] + (1.0 - ADAM_B1) * gv
            nv = ADAM_B2 * v_ref[...] + (1.0 - ADAM_B2) * jnp.square(gv)
            m_hat = nm / (1.0 - ADAM_B1 ** ADAM_STEP)
            v_hat = nv / (1.0 - ADAM_B2 ** ADAM_STEP)
            d_ref[...] = -ADAM_LR * (m_hat / (jnp.sqrt(v_hat) + ADAM_EPS) + ADAM_WD * w_ref[...])
            nm_ref[...] = nm
            nv_ref[...] = nv

    vmem = pl.BlockSpec(memory_space=pltpu.VMEM)
    outs = pl.pallas_call(
        body, name="adamw", in_specs=[vmem] * (4 * len(groups)), out_specs=(vmem,) * (3 * len(groups)),
        out_shape=tuple(jax.ShapeDtypeStruct(w.shape, F32) for w, _, _, _ in groups for _ in range(3)),
        compiler_params=pltpu.CompilerParams(vmem_limit_bytes=VMEM_LIMIT),
    )(*[a for group in groups for a in group])
    return [tuple(outs[3 * i:3 * i + 3]) for i in range(len(groups))]


SMALL_USED = D_MODEL + 4 * HEAD_DIM + 8


def _pack_small(norm_gain, qa, ka, sinks, qb, kb, extra=None):
    parts = [norm_gain.reshape(-1), qa.reshape(-1), ka.reshape(-1), sinks.reshape(-1), qb.reshape(-1), kb.reshape(-1)]
    if extra is not None:
        parts.append(extra.reshape(-1))
    flat = jnp.concatenate(parts)
    flat = jnp.pad(flat, (0, SMALL_ROWS * SMALL_COLS - flat.shape[0]))
    return flat.reshape(SMALL_ROWS, SMALL_COLS)


def _unpack_small(a):
    flat = a.reshape(-1)
    sizes = (D_MODEL, HEAD_DIM, HEAD_DIM, 8, HEAD_DIM, HEAD_DIM)
    out, off = [], 0
    for s in sizes:
        out.append(flat[off:off + s].reshape(1, s))
        off += s
    return out


def _fold_heads(row):
    return row[0, :HEAD_DIM] + row[0, HEAD_DIM:]


def kernel(x, norm_gain, w_in, q_norm_a, k_norm_a, sinks_a, q_norm_b, k_norm_b, w_out, loss_target, m_norm_gain, m_w_in, m_q_norm_a, m_k_norm_a, m_sinks_a, m_q_norm_b, m_k_norm_b, m_w_out, v_norm_gain, v_w_in, v_q_norm_a, v_k_norm_a, v_sinks_a, v_q_norm_b, v_k_norm_b, v_w_out):
    x2, tgt = x[0], loss_target[0]
    w_in_sh, w_out_sh = w_in[0], w_out[0]

    w_full = _all_gather_w_in(w_in_sh)

    inv = np.float32(ROPE_THETA) ** (-np.arange(HEAD_DIM // 2, dtype=np.float32) / np.float32(HEAD_DIM // 2))
    ang = np.arange(SEQ, dtype=np.float32)[:, None] * inv[None, :].astype(np.float32)
    cos, sin = np.cos(ang).astype(np.float32), np.sin(ang).astype(np.float32)
    cos4 = jnp.asarray(np.concatenate([cos, cos, cos, cos], axis=1))
    sin4 = jnp.asarray(np.concatenate([-sin, sin, -sin, sin], axis=1))
    blockdiag = np.kron(np.eye(2, dtype=np.float32), np.ones((HEAD_DIM, HEAD_DIM), np.float32))
    bmean = jnp.asarray(blockdiag / HEAD_DIM, dtype=BF16)
    gather_np = np.kron(np.eye(2 * N_PAIRS, dtype=np.float32), np.ones((HEAD_DIM, STAT_REP), np.float32))
    spread_np = np.kron(np.eye(2 * N_PAIRS, dtype=np.float32), np.ones((STAT_REP, HEAD_DIM), np.float32))
    spread_np[np.arange(STAT_WIDTH) % STAT_REP != 0] = 0.0
    gather, spread = jnp.asarray(gather_np, dtype=BF16), jnp.asarray(spread_np, dtype=BF16)
    two = lambda g: jnp.concatenate([g, g], axis=1)
    qkg = jnp.concatenate([two(q_norm_a), two(k_norm_a), two(q_norm_b), two(k_norm_b),
                           jnp.zeros((SMALL_ROWS - 4, PAIR), F32)], axis=0)
    sinks_paired = jnp.stack([sinks_a[0, :N_PAIRS], sinks_a[0, N_PAIRS:]], axis=1) * LOG2E
    sink_rows = jnp.concatenate([jnp.repeat(sinks_paired, BLOCK, axis=1),
                                 jnp.zeros((SMALL_ROWS - N_PAIRS, 2 * BLOCK), F32)], axis=0)

    (tqa, tka, tqb, tkb, gate_a, gate_b, h_t, qa, ka, va, qb, kb, vb, qb4, qb16, kb4, kb16, vb4, vb16,
     gathered_out) = _proj_fwd(x2, norm_gain, w_full, qkg, cos4, sin4, bmean, w_out_sh)
    wo_full = gathered_out.reshape(D_MODEL, D_MODEL)
    (oa, la), (ob1, lb1), (ob4, lb4), (ob16, lb16) = _attn_fwd("attn_fwd", [
        (qa[None], ka[None], va[None], sink_rows, BLOCK - 1), (qb[None], kb[None], vb[None], None, BLOCK),
        (qb4, kb4, vb4, None, BLOCK), (qb16, kb16, vb16, None, BLOCK)])
    (loss_cols, dy, gwo, doa, dla, dga, dgb, dob, dob4, dob16, dlb, dlb4, dlb16, lse_b, lse4, lse16) = _tail(
        oa[0], ob1[0], lb1[0], ob4, lb4, ob16, lb16, gate_a, gate_b, x2, tgt, wo_full, spread, gather)

    dqa, dka, dva, dsink = _attn_bwd("attn_a_bwd", qa[None], ka[None], va[None], doa[None], la, dla[None], sink_rows, BLOCK - 1)
    dq1, dk1, dv1 = _attn_bwd("attn_b1_bwd", qb[None], kb[None], vb[None], dob[None], lse_b[None], dlb[None], None, BLOCK)
    dq4, dk4, dv4 = _attn_bwd("attn_b4_bwd", qb4, kb4, vb4, dob4, lse4, dlb4, None, BLOCK)
    dq16, dk16, dv16 = _attn_bwd("attn_b16_bwd", qb16, kb16, vb16, dob16, lse16, dlb16, None, BLOCK)
    dproj, dqkg = _dproj_assemble(dqa[0], dka[0], dva[0], dga, dgb, dq1[0], dk1[0], dv1[0], dq4, dk4, dv4,
                                  dq16, dk16, dv16, tqa, tqb, tkb, tka, qkg, cos4, sin4, bmean)
    gw_in = _weight_grad(h_t, dproj)

    blocks_in = gw_in
    blocks_out = gwo.reshape(N_DEV, SHARD_OUT, D_MODEL)
    g_sinks = jnp.concatenate([jnp.sum(dsink[:N_PAIRS, :BLOCK], axis=1), jnp.sum(dsink[:N_PAIRS, BLOCK:], axis=1)])
    small = _pack_small(jnp.zeros((D_MODEL,), F32), _fold_heads(dqkg[0:1]), _fold_heads(dqkg[1:2]), g_sinks,
                        _fold_heads(dqkg[2:3]), _fold_heads(dqkg[3:4]), extra=0.5 * jnp.sum(loss_cols) / D_MODEL)
    grad_x, g_w_in, g_w_out, small_red, dgain_red = _input_grad_reduce(
        dproj, w_full, x2, norm_gain, dy, blocks_in, blocks_out, small)
    n_gain_rows = D_MODEL // SMALL_COLS
    small_red = jnp.concatenate([dgain_red.reshape(n_gain_rows, SMALL_COLS), small_red[n_gain_rows:]], axis=0)
    g_small = _unpack_small(small_red)

    (d_in, nm_in, nv_in), (d_out, nm_out, nv_out), (d_s, nm_s, nv_s) = _adamw([
        (w_in_sh, g_w_in, m_w_in[0], v_w_in[0]),
        (w_out_sh, g_w_out, m_w_out[0], v_w_out[0]),
        (_pack_small(norm_gain, q_norm_a, k_norm_a, sinks_a, q_norm_b, k_norm_b), small_red,
         _pack_small(m_norm_gain, m_q_norm_a, m_k_norm_a, m_sinks_a, m_q_norm_b, m_k_norm_b),
         _pack_small(v_norm_gain, v_q_norm_a, v_k_norm_a, v_sinks_a, v_q_norm_b, v_k_norm_b))])
    d_small, nm_small, nv_small = _unpack_small(d_s), _unpack_small(nm_s), _unpack_small(nv_s)

    loss = small_red.reshape(-1)[SMALL_USED]

    def assemble(small_list, big_in, big_out):
        ng, qa_, ka_, sk_, qb_, kb_ = small_list
        return [ng, big_in[None], qa_, ka_, sk_, qb_, kb_, big_out[None]]

    return (loss, grad_x[None], *assemble(g_small, g_w_in, g_w_out), *assemble(d_small, d_in, d_out),
            *assemble(nm_small, nm_in, nm_out), *assemble(nv_small, nv_in, nv_out))
```

```python
import numpy as np
import jax
import jax.numpy as jnp
from jax import lax
from jax.experimental import pallas as pl
from jax.experimental.pallas import tpu as pltpu

F32 = jnp.float32
BF16 = jnp.bfloat16

SEQ = 4096
D_MODEL = 1024
HEAD_DIM = 64
PAIR = 2 * HEAD_DIM
N_PAIRS = 4
HALF_WIDTH = N_PAIRS * PAIR
KV_A_WIDTH = 128
IN_WIDTH = 3328
LANES = 128
BLOCK = 128
STAT_REP = 16
STAT_WIDTH = 128
EPS = 1e-6
NEG = -1e30
ROPE_THETA = 10000.0
LOG2E = 1.4426950408889634
LN2 = 0.6931471805599453
Q_SCALE = HEAD_DIM ** -0.5 * LOG2E
N_DEV = 8
SHARD_IN = IN_WIDTH // N_DEV
SHARD_OUT = D_MODEL // N_DEV
SMALL_ROWS, SMALL_COLS = 8, 256

C_QA, C_KA, C_VA, C_GA, C_QB, C_KB, C_VB, C_GB = 0, 512, 640, 768, 1280, 1792, 2304, 2816

ADAM_LR = 0.001
ADAM_B1 = 0.9
ADAM_B2 = 0.999
ADAM_EPS = 1e-08
ADAM_WD = 0.01
ADAM_STEP = 10

ROW_TILE = 256
PROJ_ROW_TILE = 512
FWD_BLOCKS_PER_STEP = 8
BWD_BLOCKS_PER_STEP = 8
VMEM_LIMIT = 56 * 1024 * 1024

MESH = pl.DeviceIdType.MESH


def _params(sem, vmem=VMEM_LIMIT):
    return pltpu.CompilerParams(dimension_semantics=sem, vmem_limit_bytes=vmem)


def _head_sum(v, bm):
    return jnp.dot(v.astype(BF16), bm, preferred_element_type=F32)


def _swap_halves(y):
    lane = lax.broadcasted_iota(jnp.int32, y.shape, 1)
    first = (lane & 32) == 0
    return jnp.where(first, pltpu.roll(y, 96, 1), pltpu.roll(y, 32, 1))


def _sigmoid(g):
    return 1.0 / (1.0 + jnp.exp(-g))


def _tiles(a):
    return [a[:, j * PAIR:(j + 1) * PAIR] for j in range(N_PAIRS)]


def _pair_tiles(t):
    low = lax.broadcasted_iota(jnp.int32, t[0].shape, 1) < HEAD_DIM
    r = [pltpu.roll(a, HEAD_DIM, 1) for a in t]
    return [jnp.where(low, t[0], r[2]), jnp.where(low, r[0], t[2]), jnp.where(low, t[1], r[3]), jnp.where(low, r[1], t[3])]


def _unpair_tiles(p):
    low = lax.broadcasted_iota(jnp.int32, p[0].shape, 1) < HEAD_DIM
    r = [pltpu.roll(a, HEAD_DIM, 1) for a in p]
    return [jnp.where(low, p[0], r[1]), jnp.where(low, p[2], r[3]), jnp.where(low, r[0], p[1]), jnp.where(low, r[2], p[3])]


def _routes():
    x, y, c = lax.axis_index("x"), lax.axis_index("y"), lax.axis_index("c")
    north = c == 1
    first = (jnp.where(north, 1 - x, x), jnp.where(north, y, 1 - y), c)
    other = (jnp.where(north, x, 1 - x), jnp.where(north, 1 - y, y), c)
    k_first = jnp.where(north, 1, 2)
    return first, other, k_first, 3 - k_first


def _gather_plan(mine_ref, out_ref, send_sems, recv_sems):
    x, y, c = lax.axis_index("x"), lax.axis_index("y"), lax.axis_index("c")
    me, sibling, diag = (x, y, c), (x, y, 1 - c), (1 - x, 1 - y, c)
    first, other, k_first, k_other = _routes()

    def slot(px, py, pc):
        return out_ref.at[4 * px + 2 * py + pc]

    def copy(k, block, to, from_mine=False):
        return pltpu.make_async_remote_copy(
            src_ref=mine_ref if from_mine else slot(*block), dst_ref=slot(*block),
            send_sem=send_sems.at[k], recv_sem=recv_sems.at[k], device_id=to, device_id_type=MESH)

    sends = [copy(0, me, sibling, True), copy(1, me, (1 - x, y, c), True), copy(2, me, (x, 1 - y, c), True)]
    stages = [(copy(k_first, first, me), [copy(3, first, other), copy(3 + k_first, first, sibling)]),
              (copy(k_other, other, me), [copy(3 + k_other, other, sibling)]),
              (copy(3, diag, me), [copy(6, diag, sibling)])]
    from_sibling = [copy(0, sibling, me), copy(4, (1 - x, y, 1 - c), me), copy(5, (x, 1 - y, 1 - c), me),
                    copy(6, (1 - x, 1 - y, 1 - c), me)]
    return slot(*me), sends, stages, from_sibling


GATHER_SCRATCH = [pltpu.SemaphoreType.DMA((7,)), pltpu.SemaphoreType.DMA((7,))]


def _all_gather_w_in(w_in_sh):
    rows, cols = w_in_sh.shape

    def body(w_ref, out_ref, mine_ref, blocks, send_sems, recv_sems):
        mine_ref[...] = w_ref[...].astype(BF16)
        my_slot, sends, stages, from_sibling = _gather_plan(mine_ref, blocks, send_sems, recv_sems)
        for cp in sends:
            cp.start()
        my_slot[...] = mine_ref[...]
        for arrival, forwards in stages:
            arrival.wait_recv()
            for cp in forwards:
                cp.start()
        for arrival in from_sibling:
            arrival.wait_recv()
        for cp in sends + [cp for _, forwards in stages for cp in forwards]:
            cp.wait_send()
        for d in range(N_DEV):
            out_ref[:, d * cols:(d + 1) * cols] = blocks[d]

    vmem = pl.BlockSpec(memory_space=pltpu.VMEM)
    return pl.pallas_call(
        body, name="ag_w_in",
        out_shape=jax.ShapeDtypeStruct((rows, N_DEV * cols), BF16),
        in_specs=[vmem], out_specs=vmem,
        scratch_shapes=[pltpu.VMEM((rows, cols), BF16), pltpu.VMEM((N_DEV, rows, cols), BF16)] + GATHER_SCRATCH,
        compiler_params=pltpu.CompilerParams(vmem_limit_bytes=VMEM_LIMIT),
    )(w_in_sh)


def _fold_scratch(tm):
    return pltpu.VMEM((N_PAIRS, tm, PAIR), F32)


def _fold_store(val, scr, out4, out16, tm):
    groups = range(val.shape[1] // PAIR)
    for j in groups:
        scr[j] = val[:, j * PAIR:(j + 1) * PAIR]
    for dil, out in ((4, out4), (16, out16)):
        for r in range(dil):
            for j in groups:
                out[r, :, j * PAIR:(j + 1) * PAIR] = scr[j, pl.ds(r, tm // dil, stride=dil), :].astype(out.dtype)


def _unfold_load(src, scr, dil, tm):
    groups = range(src.shape[2] // PAIR)
    for r in range(dil):
        for j in groups:
            scr[j, pl.ds(r, tm // dil, stride=dil), :] = src[r, :, j * PAIR:(j + 1) * PAIR].astype(F32)
    return jnp.concatenate([scr[j] for j in groups], axis=1)


def _fold_specs(tm, dtype, width=HALF_WIDTH):
    shapes = (jax.ShapeDtypeStruct((4, SEQ // 4, width), dtype), jax.ShapeDtypeStruct((16, SEQ // 16, width), dtype))
    specs = (pl.BlockSpec((4, tm // 4, width), lambda i: (0, i, 0)),
             pl.BlockSpec((16, tm // 16, width), lambda i: (0, i, 0)))
    return shapes, specs


def _proj_fwd(x, gain, w, qkg, cos4, sin4, bmean, w_out_sh):
    tm = PROJ_ROW_TILE
    n_steps = SEQ // tm

    def norm_rope(t, g, cos, sin, bm, scale):
        rr = lax.rsqrt(_head_sum(t * t, bm) + EPS)
        yv = t * rr * g
        return (yv * cos + _swap_halves(yv) * sin) * scale

    def body(x_ref, g_ref, w_ref, qkg_ref, cos_ref, sin_ref, bm_ref, wo_ref,
             tqa_ref, tka_ref, tqb_ref, tkb_ref, ga_ref, gb_ref, ht_ref, qa_ref, ka_ref, va_ref, qb_ref, kb_ref, vb_ref,
             qb4_ref, qb16_ref, kb4_ref, kb16_ref, vb4_ref, vb16_ref, wo_all_ref,
             proj, scr, wo_mine, wo_all, send_sems, recv_sems):
        i = pl.program_id(0)
        my_slot, sends, stages, from_sibling = _gather_plan(wo_mine, wo_all, send_sems, recv_sems)

        @pl.when(i == 0)
        def _():
            wo_mine[...] = wo_ref[...].astype(BF16)
            for cp in sends:
                cp.start()
            my_slot[...] = wo_mine[...]

        @pl.when(i == n_steps // 2)
        def _():
            for arrival, forwards in stages[:2]:
                arrival.wait_recv()
                for cp in forwards:
                    cp.start()

        xf = x_ref[...]
        r = lax.rsqrt(jnp.mean(xf * xf, axis=-1, keepdims=True) + EPS)
        hf = xf * r * g_ref[...]
        ht_ref[...] = hf.T.astype(BF16)
        cos, sin, bm = cos_ref[...], sin_ref[...], bm_ref[...]
        proj[...] = jnp.dot(hf.astype(BF16), w_ref[...], preferred_element_type=F32)

        def roped(tiles, row, scale):
            g = qkg_ref[row:row + 1, :]
            return jnp.concatenate([norm_rope(t, g, cos, sin, bm, scale) for t in tiles], axis=1)

        tqa = _pair_tiles(_tiles(proj[:, C_QA:C_QA + HALF_WIDTH]))
        tqa_ref[...] = jnp.concatenate(tqa, axis=1).astype(BF16)
        qa_ref[...] = roped(tqa, 0, Q_SCALE).astype(BF16)
        ga_ref[...] = jnp.concatenate(_pair_tiles(_tiles(proj[:, C_GA:C_GA + HALF_WIDTH])), axis=1).astype(BF16)
        gb_ref[...] = proj[:, C_GB:C_GB + HALF_WIDTH].astype(BF16)
        tqb = proj[:, C_QB:C_QB + HALF_WIDTH]
        tqb_ref[...] = tqb.astype(BF16)
        qb = roped(_tiles(tqb), 2, Q_SCALE)
        qb_ref[...] = qb.astype(BF16)
        _fold_store(qb, scr, qb4_ref, qb16_ref, tm)
        tkb = proj[:, C_KB:C_KB + HALF_WIDTH]
        tkb_ref[...] = tkb.astype(BF16)
        kb = roped(_tiles(tkb), 3, 1.0)
        kb_ref[...] = kb.astype(BF16)
        _fold_store(kb, scr, kb4_ref, kb16_ref, tm)
        vb = proj[:, C_VB:C_VB + HALF_WIDTH]
        vb_ref[...] = vb.astype(BF16)
        _fold_store(vb, scr, vb4_ref, vb16_ref, tm)
        tka = proj[:, C_KA:C_KA + KV_A_WIDTH]
        tka_ref[...] = tka.astype(BF16)
        ka_ref[...] = roped([tka], 1, 1.0).astype(BF16)
        va_ref[...] = proj[:, C_VA:C_VA + KV_A_WIDTH].astype(BF16)

        @pl.when(i == n_steps - 1)
        def _():
            arrival, forwards = stages[2]
            arrival.wait_recv()
            for cp in forwards:
                cp.start()
            for arrival in from_sibling:
                arrival.wait_recv()
            for cp in sends + [cp for _, forwards in stages for cp in forwards]:
                cp.wait_send()
            wo_all_ref[...] = wo_all[...]

    row = lambda width: pl.BlockSpec((tm, width), lambda i: (i, 0))
    full = lambda a: pl.BlockSpec(a.shape, lambda i: (0,) * a.ndim)
    nat = lambda width, dtype=BF16: jax.ShapeDtypeStruct((SEQ, width), dtype)
    f_shapes, f_specs = _fold_specs(tm, BF16)
    return pl.pallas_call(
        body, name="proj_fwd", grid=(SEQ // tm,),
        in_specs=[row(D_MODEL), full(gain), full(w), full(qkg), row(PAIR), row(PAIR), full(bmean), full(w_out_sh)],
        out_specs=(row(HALF_WIDTH), row(KV_A_WIDTH), row(HALF_WIDTH), row(HALF_WIDTH), row(HALF_WIDTH), row(HALF_WIDTH),
                   pl.BlockSpec((D_MODEL, tm), lambda i: (0, i)),
                   row(HALF_WIDTH), row(KV_A_WIDTH), row(KV_A_WIDTH), row(HALF_WIDTH), row(HALF_WIDTH), row(HALF_WIDTH),
                   *f_specs, *f_specs, *f_specs,
                   pl.BlockSpec((N_DEV,) + w_out_sh.shape, lambda i: (0, 0, 0))),
        out_shape=(nat(HALF_WIDTH), nat(KV_A_WIDTH), nat(HALF_WIDTH), nat(HALF_WIDTH), nat(HALF_WIDTH), nat(HALF_WIDTH),
                   jax.ShapeDtypeStruct((D_MODEL, SEQ), BF16),
                   nat(HALF_WIDTH), nat(KV_A_WIDTH), nat(KV_A_WIDTH), nat(HALF_WIDTH), nat(HALF_WIDTH), nat(HALF_WIDTH),
                   *f_shapes, *f_shapes, *f_shapes,
                   jax.ShapeDtypeStruct((N_DEV,) + w_out_sh.shape, BF16)),
        scratch_shapes=[pltpu.VMEM((tm, IN_WIDTH), F32), _fold_scratch(tm), pltpu.VMEM(w_out_sh.shape, BF16),
                        pltpu.VMEM((N_DEV,) + w_out_sh.shape, BF16)] + GATHER_SCRATCH,
        compiler_params=_params(("arbitrary",)),
    )(x, gain, w, qkg, cos4, sin4, bmean, w_out_sh)


def _fill_band_bias(bias_ref, max_dist):
    j = lax.broadcasted_iota(jnp.int32, (2 * BLOCK, 2 * BLOCK), 0)
    c = lax.broadcasted_iota(jnp.int32, (2 * BLOCK, 2 * BLOCK), 1)
    dist = (c & (BLOCK - 1)) + BLOCK - j
    band = (dist >= 0) & (dist <= max_dist)
    bias_ref[0] = jnp.where(band, 0.0, NEG)
    bias_ref[1] = jnp.where(band & (j >= BLOCK), 0.0, NEG)


def _band_bias(bias_ref, step, b, qb, nb):
    if nb < qb:
        return bias_ref[1 if b % nb == 0 else 0]
    if b > 0:
        return bias_ref[0]
    return bias_ref[jnp.where(((step * qb) & (nb - 1)) == 0, 1, 0)]


def _stack_heads(t):
    lane = lax.broadcasted_iota(jnp.int32, t.shape, 1)
    low = lane < HEAD_DIM
    zero = jnp.zeros_like(t)
    return jnp.concatenate([jnp.where(low, t, zero), jnp.where(low, zero, t)], axis=0)


def _stack_heads_t(t):
    tt = t.astype(F32).T
    low = lax.broadcasted_iota(jnp.int32, tt.shape, 0) < HEAD_DIM
    zero = jnp.zeros_like(tt)
    return jnp.concatenate([jnp.where(low, tt, zero), jnp.where(low, zero, tt)], axis=1).astype(BF16)


def _unstack_t(t):
    return jnp.concatenate([t[:HEAD_DIM, :BLOCK], t[HEAD_DIM:, BLOCK:]], axis=0).T


def _rows_to_stats(rows):
    parts = []
    for row in rows:
        parts.append(jnp.broadcast_to(row[:, :BLOCK], (STAT_REP, BLOCK)))
        parts.append(jnp.broadcast_to(row[:, BLOCK:], (STAT_REP, BLOCK)))
    return jnp.concatenate(parts, axis=0).T


def _stats_to_rows(t):
    tt = t.T
    return [jnp.concatenate([tt[2 * p * STAT_REP:2 * p * STAT_REP + 1, :],
                             tt[(2 * p + 1) * STAT_REP:(2 * p + 1) * STAT_REP + 1, :]], axis=1) for p in range(N_PAIRS)]


def _attn_fwd(name, patterns):
    qb = FWD_BLOCKS_PER_STEP
    steps = SEQ // (qb * BLOCK)

    def one_pattern(step, nb, shared, max_dist, q_ref, kc_ref, vc_ref, sink_ref, o_ref, lse_ref, kp_ref, vp_ref, bias_ref):
        has_sinks = sink_ref is not None

        @pl.when(step == 0)
        def _():
            kp_ref[...] = jnp.zeros_like(kp_ref)
            vp_ref[...] = jnp.zeros_like(vp_ref)
            _fill_band_bias(bias_ref, max_dist)

        cols = [slice(p * PAIR, (p + 1) * PAIR) for p in range(N_PAIRS)]
        kcols = [slice(0, PAIR) if shared else c for c in cols]
        rows = [slice(b * BLOCK, (b + 1) * BLOCK) for b in range(qb)]
        units = [(b, p) for b in range(qb) for p in range(N_PAIRS)]
        n = range(len(units))

        def window(prev_ref, cur_ref, b, kc):
            before = prev_ref[:, kc] if b == 0 else cur_ref[rows[b - 1], kc]
            return jnp.concatenate([before, cur_ref[rows[b], kc]], axis=0)

        st = [jnp.dot(window(kp_ref, kc_ref, b, kcols[p]), _stack_heads_t(q_ref[rows[b], cols[p]]),
                      preferred_element_type=F32) for b, p in units]
        st = [st[u] + _band_bias(bias_ref, step, units[u][0], qb, nb) for u in n]
        m = [jnp.max(s, axis=0, keepdims=True) for s in st]
        if has_sinks:
            sk = [sink_ref[p:p + 1, :] for _, p in units]
            m = [jnp.maximum(m[u], sk[u]) for u in n]
        pt = [jnp.exp2(st[u] - m[u]) for u in n]
        l = [jnp.sum(t, axis=0, keepdims=True) for t in pt]
        if has_sinks:
            l = [l[u] + jnp.exp2(sk[u] - m[u]) for u in n]
        v2t = [window(vp_ref, vc_ref, b, kcols[p]).astype(F32).T.astype(BF16) for b, p in units]
        ot = [jnp.dot(v2t[u], pt[u].astype(BF16), preferred_element_type=F32) / l[u] for u in n]
        for u, (b, p) in enumerate(units):
            o_ref[rows[b], cols[p]] = _unstack_t(ot[u]).astype(BF16)
        for b in range(qb):
            lse_ref[rows[b], :] = _rows_to_stats([m[u] + jnp.log2(l[u]) for u in n if units[u][0] == b])
        kp_ref[...] = kc_ref[rows[-1], :]
        vp_ref[...] = vc_ref[rows[-1], :]

    n_in = [4 if sinks is not None else 3 for _, _, _, sinks, _ in patterns]

    def body(*refs):
        ins, rest = refs[:sum(n_in)], refs[sum(n_in):]
        outs, scratch = rest[:2 * len(patterns)], rest[2 * len(patterns):]
        bias_ref = scratch[-1]
        step = pl.program_id(0)
        first = 0
        for p, (q, k, _, sinks, max_dist) in enumerate(patterns):
            mine = ins[first:first + n_in[p]]
            first += n_in[p]
            sink_ref = mine[3] if sinks is not None else None
            kp_ref, vp_ref = scratch[2 * p], scratch[2 * p + 1]

            @pl.when((step >= p * steps) & (step < (p + 1) * steps))
            def _():
                one_pattern(step - p * steps, q.shape[1] // BLOCK, k.shape[2] == PAIR, max_dist,
                            mine[0], mine[1], mine[2], sink_ref, outs[2 * p], outs[2 * p + 1], kp_ref, vp_ref, bias_ref)

    def during(p, width):
        return pl.BlockSpec((qb * BLOCK, width), lambda s: (jnp.clip(s - p * steps, 0, steps - 1), 0))

    flat = lambda a: a.reshape(SEQ, a.shape[2])
    in_specs, args, out_specs, out_shape, scratch = [], [], [], [], []
    for p, (q, k, v, sinks, _) in enumerate(patterns):
        ck = k.shape[2]
        in_specs += [during(p, HALF_WIDTH), during(p, ck), during(p, ck)]
        args += [flat(q), flat(k), flat(v)]
        if sinks is not None:
            in_specs.append(pl.BlockSpec(sinks.shape, lambda s: (0, 0)))
            args.append(sinks)
        out_specs += [during(p, HALF_WIDTH), during(p, STAT_WIDTH)]
        out_shape += [jax.ShapeDtypeStruct((SEQ, HALF_WIDTH), BF16), jax.ShapeDtypeStruct((SEQ, STAT_WIDTH), F32)]
        scratch += [pltpu.VMEM((BLOCK, ck), BF16), pltpu.VMEM((BLOCK, ck), BF16)]
    outs = pl.pallas_call(
        body, name=name, grid=(len(patterns) * steps,), in_specs=in_specs,
        out_specs=tuple(out_specs), out_shape=tuple(out_shape),
        scratch_shapes=scratch + [pltpu.VMEM((2, 2 * BLOCK, 2 * BLOCK), F32)],
        compiler_params=_params(("arbitrary",)),
    )(*args)
    return [(outs[2 * p].reshape(q.shape), outs[2 * p + 1].reshape(q.shape[0], q.shape[1], STAT_WIDTH))
            for p, (q, _, _, _, _) in enumerate(patterns)]


def _attn_bwd(name, q, k, v, d_o, lse, delta, sink_rows, max_dist):
    n_seq, length, _ = q.shape
    ck = k.shape[2]
    nb = length // BLOCK
    n_blocks = n_seq * nb
    n_rows = n_seq * length
    shared = ck == PAIR
    has_sinks = sink_rows is not None
    qb = BWD_BLOCKS_PER_STEP
    n_steps = n_blocks // qb

    def body(*refs):
        if has_sinks:
            (q_ref, kc_ref, vc_ref, do_ref, lse_ref, dl_ref, sink_ref,
             dq_ref, dk_ref, dv_ref, dsink_ref, ck_scr, cv_scr, kp_ref, vp_ref, bias_ref) = refs
        else:
            (q_ref, kc_ref, vc_ref, do_ref, lse_ref, dl_ref,
             dq_ref, dk_ref, dv_ref, ck_scr, cv_scr, kp_ref, vp_ref, bias_ref) = refs
        step = pl.program_id(0)

        @pl.when(step == 0)
        def _():
            ck_scr[...] = jnp.zeros_like(ck_scr)
            cv_scr[...] = jnp.zeros_like(cv_scr)
            kp_ref[...] = jnp.zeros_like(kp_ref)
            vp_ref[...] = jnp.zeros_like(vp_ref)
            if has_sinks:
                dsink_ref[...] = jnp.zeros_like(dsink_ref)
            _fill_band_bias(bias_ref, max_dist)

        cols = [slice(p * PAIR, (p + 1) * PAIR) for p in range(N_PAIRS)]
        kcols = [slice(0, PAIR) if shared else c for c in cols]
        rows = [slice(b * BLOCK, (b + 1) * BLOCK) for b in range(qb)]
        units = [(b, p) for b in range(qb) for p in range(N_PAIRS)]
        n = range(len(units))
        nt = (((1,), (1,)), ((), ()))

        def window(prev_ref, cur_ref, b, kc):
            before = prev_ref[:, kc] if b == 0 else cur_ref[rows[b - 1], kc]
            return jnp.concatenate([before, cur_ref[rows[b], kc]], axis=0)

        q_st = [_stack_heads(q_ref[rows[b], cols[p]]) for b, p in units]
        do_st = [_stack_heads(do_ref[rows[b], cols[p]]) for b, p in units]
        k2 = [window(kp_ref, kc_ref, b, kcols[p]) for b, p in units]
        v2 = [window(vp_ref, vc_ref, b, kcols[p]) for b, p in units]
        st = [lax.dot_general(k2[u], q_st[u], nt, preferred_element_type=F32) for u in n]
        dpt = [lax.dot_general(v2[u], do_st[u], nt, preferred_element_type=F32) for u in n]
        lse_rows = [_stats_to_rows(lse_ref[rows[b], :]) for b in range(qb)]
        dl_rows = [_stats_to_rows(dl_ref[rows[b], :]) for b in range(qb)]
        lse_row = [lse_rows[b][p] for b, p in units]
        dl_row = [dl_rows[b][p] for b, p in units]
        pt = [jnp.exp2(st[u] + _band_bias(bias_ref, step, units[u][0], qb, nb) - lse_row[u]) for u in n]
        dst = [(pt[u] * (dpt[u] - dl_row[u])).astype(BF16) for u in n]
        ptb = [t.astype(BF16) for t in pt]
        dv2 = [jnp.dot(ptb[u], do_st[u], preferred_element_type=F32) for u in n]
        dk2 = [jnp.dot(dst[u], q_st[u], preferred_element_type=F32) for u in n]
        k2t = [k2[u].astype(F32).T.astype(BF16) for u in n]
        dqt = [jnp.dot(k2t[u], dst[u], preferred_element_type=F32) for u in n]
        for u, (b, p) in enumerate(units):
            dq_ref[rows[b], cols[p]] = _unstack_t(dqt[u]).astype(BF16)
        if has_sinks:
            for u, (b, p) in enumerate(units):
                p_sink = jnp.exp2(sink_ref[p:p + 1, :] - lse_row[u])
                dsink_ref[p:p + 1, :] = dsink_ref[p:p + 1, :] - p_sink * dl_row[u]

        def total(parts, w, group):
            sel = [u for u, (b, p) in enumerate(units) if (shared or p == group)]
            terms = ([parts[u][:BLOCK] for u in sel if units[u][0] == w]
                     + [parts[u][BLOCK:] for u in sel if units[u][0] == w - 1])
            tot = terms[0]
            for t in terms[1:]:
                tot = tot + t
            return tot

        first_row = step * (qb * BLOCK)
        for acc_ref, out_ref, parts in ((ck_scr, dk_ref, dk2), (cv_scr, dv_ref, dv2)):
            for group in range(1 if shared else N_PAIRS):
                kc = kcols[group]

                @pl.when(step > 0)
                def _():
                    out_ref[pl.ds(pl.multiple_of(first_row - BLOCK, BLOCK), BLOCK), kc] = (
                        acc_ref[:, kc] + total(parts, 0, group)).astype(BF16)

                for w in range(1, qb):
                    out_ref[pl.ds(pl.multiple_of(first_row + (w - 1) * BLOCK, BLOCK), BLOCK), kc] = (
                        total(parts, w, group).astype(BF16))
                acc_ref[:, kc] = total(parts, qb, group)

        @pl.when(step == n_steps - 1)
        def _():
            dk_ref[pl.ds(n_rows - BLOCK, BLOCK), :] = ck_scr[...].astype(BF16)
            dv_ref[pl.ds(n_rows - BLOCK, BLOCK), :] = cv_scr[...].astype(BF16)

        kp_ref[...] = kc_ref[rows[-1], :]
        vp_ref[...] = vc_ref[rows[-1], :]

    cur = lambda width: pl.BlockSpec((qb * BLOCK, width), lambda s: (s, 0))
    whole = lambda width: pl.BlockSpec((n_rows, width), lambda s: (0, 0))
    flat = lambda a: a.reshape(n_rows, a.shape[2])
    in_specs = [cur(HALF_WIDTH), cur(ck), cur(ck), cur(HALF_WIDTH), cur(STAT_WIDTH), cur(STAT_WIDTH)]
    args = [flat(a) for a in (q, k, v, d_o, lse, delta)]
    out_specs = [cur(HALF_WIDTH), whole(ck), whole(ck)]
    out_shape = [jax.ShapeDtypeStruct((n_rows, HALF_WIDTH), BF16),
                 jax.ShapeDtypeStruct((n_rows, ck), BF16), jax.ShapeDtypeStruct((n_rows, ck), BF16)]
    if has_sinks:
        in_specs.append(pl.BlockSpec(sink_rows.shape, lambda s: (0, 0)))
        args.append(sink_rows)
        out_specs.append(pl.BlockSpec(sink_rows.shape, lambda s: (0, 0)))
        out_shape.append(jax.ShapeDtypeStruct(sink_rows.shape, F32))
    outs = pl.pallas_call(
        body, name=name, grid=(n_steps,), in_specs=in_specs,
        out_specs=tuple(out_specs), out_shape=tuple(out_shape),
        scratch_shapes=[pltpu.VMEM((BLOCK, ck), F32), pltpu.VMEM((BLOCK, ck), F32),
                        pltpu.VMEM((BLOCK, ck), BF16), pltpu.VMEM((BLOCK, ck), BF16),
                        pltpu.VMEM((2, 2 * BLOCK, 2 * BLOCK), F32)],
        compiler_params=_params(("arbitrary",)),
    )(*args)
    return tuple(o.reshape(n_seq, length, o.shape[1]) for o in outs[:3]) + tuple(outs[3:])


def _tail(oa, ob1, lb1, ob4, lb4, ob16, lb16, gate_a, gate_b, x, target, w_out, spread, gather):
    tm = ROW_TILE

    def split_dot(v, mat):
        hi = v.astype(BF16)
        lo = (v - hi.astype(F32)).astype(BF16)
        return jnp.dot(hi, mat, preferred_element_type=F32) + jnp.dot(lo, mat, preferred_element_type=F32)

    def body(oa_ref, ob1_ref, lb1_ref, ob4_ref, lb4_ref, ob16_ref, lb16_ref, ga_ref, gb_ref, x_ref, t_ref, w_ref,
             sp_ref, ga_mat_ref,
             loss_ref, dy_ref, gwo_ref, doa_ref, dla_ref, dga_ref, dgb_ref,
             dob_ref, dob4_ref, dob16_ref, dlb_ref, dlb4_ref, dlb16_ref, lse_ref, lse4_ref, lse16_ref,
             s_f, mix_keep, dy_keep):
        i = pl.program_id(0)
        sp, gat = sp_ref[...], ga_mat_ref[...]
        o4, o16 = _unfold_load(ob4_ref, s_f, 4, tm), _unfold_load(ob16_ref, s_f, 16, tm)
        l4, l16 = _unfold_load(lb4_ref, s_f, 4, tm), _unfold_load(lb16_ref, s_f, 16, tm)
        o1, l1 = ob1_ref[...].astype(F32), lb1_ref[...]
        mx = jnp.maximum(jnp.maximum(l1, l4), l16)
        e1, e4, e16 = jnp.exp2(l1 - mx), jnp.exp2(l4 - mx), jnp.exp2(l16 - mx)
        den = e1 + e4 + e16
        inv = 1.0 / den
        ob = split_dot(e1 * inv, sp) * o1 + split_dot(e4 * inv, sp) * o4 + split_dot(e16 * inv, sp) * o16
        lse_b = mx + jnp.log2(den)

        oa, ga, gb = oa_ref[...].astype(F32), ga_ref[...].astype(F32), gb_ref[...].astype(F32)
        sa, sb = _sigmoid(ga), _sigmoid(gb)
        mixed = jnp.concatenate(_unpair_tiles(_tiles(oa * (ga * sa))) + [ob * (gb * sb)], axis=1)
        mixed_bf = mixed.astype(BF16)
        w = w_ref[...]
        yv = x_ref[...] + jnp.dot(mixed_bf, w, preferred_element_type=F32)
        err = yv - t_ref[...]
        sq = jnp.sum(err * err, axis=0, keepdims=True)
        dy = err * (1.0 / D_MODEL)
        dy_ref[...] = dy
        dy_bf = dy.astype(BF16)
        mix_t = mixed.T.astype(BF16)

        @pl.when(i == 0)
        def _():
            loss_ref[...] = sq

        @pl.when(i > 0)
        def _():
            loss_ref[...] += sq

        @pl.when((i & 1) == 0)
        def _():
            mix_keep[...] = mix_t
            dy_keep[...] = dy_bf

        @pl.when((i & 1) == 1)
        def _():
            gw = jnp.dot(jnp.concatenate([mix_keep[...], mix_t], axis=1), jnp.concatenate([dy_keep[...], dy_bf], axis=0),
                         preferred_element_type=F32)

            @pl.when(i == 1)
            def _():
                gwo_ref[...] = gw

            @pl.when(i > 1)
            def _():
                gwo_ref[...] += gw

        dmix = lax.dot_general(dy_bf, w, (((1,), (1,)), ((), ())), preferred_element_type=F32)
        dma = jnp.concatenate(_pair_tiles(_tiles(dmix[:, :HALF_WIDTH])), axis=1)
        dmb = dmix[:, HALF_WIDTH:]

        doa = dma * (ga * sa)
        doa_ref[...] = doa.astype(BF16)
        dla_ref[...] = split_dot(doa * oa, gat)
        dga_ref[...] = (dma * oa * (sa * (1.0 + ga * (1.0 - sa)))).astype(BF16)
        dob = dmb * (gb * sb)
        dgb_ref[...] = (dmb * ob * (sb * (1.0 + gb * (1.0 - sb)))).astype(BF16)
        dlb = split_dot(dob * ob, gat)
        dob_ref[...] = dob.astype(BF16)
        _fold_store(dob, s_f, dob4_ref, dob16_ref, tm)
        dlb_ref[...] = dlb
        _fold_store(dlb, s_f, dlb4_ref, dlb16_ref, tm)
        lse_ref[...] = lse_b
        _fold_store(lse_b, s_f, lse4_ref, lse16_ref, tm)

    row = lambda width: pl.BlockSpec((tm, width), lambda i: (i, 0))
    full = lambda a: pl.BlockSpec(a.shape, lambda i: (0,) * a.ndim)
    fb_shapes, fb_specs = _fold_specs(tm, BF16)
    _, ff_specs = _fold_specs(tm, F32)
    st_shapes, st_specs = _fold_specs(tm, F32, STAT_WIDTH)
    nat = lambda dtype, width=HALF_WIDTH: jax.ShapeDtypeStruct((SEQ, width), dtype)
    return pl.pallas_call(
        body, name="tail", grid=(SEQ // tm,),
        in_specs=[row(HALF_WIDTH), row(HALF_WIDTH), row(STAT_WIDTH), ff_specs[0], st_specs[0], ff_specs[1], st_specs[1],
                  row(HALF_WIDTH), row(HALF_WIDTH), row(D_MODEL), row(D_MODEL), full(w_out), full(spread), full(gather)],
        out_specs=(pl.BlockSpec((1, D_MODEL), lambda i: (0, 0)), row(D_MODEL),
                   pl.BlockSpec((D_MODEL, D_MODEL), lambda i: (0, 0)),
                   row(HALF_WIDTH), row(STAT_WIDTH), row(HALF_WIDTH), row(HALF_WIDTH),
                   row(HALF_WIDTH), *fb_specs, row(STAT_WIDTH), *st_specs, row(STAT_WIDTH), *st_specs),
        out_shape=(jax.ShapeDtypeStruct((1, D_MODEL), F32), jax.ShapeDtypeStruct((SEQ, D_MODEL), F32),
                   jax.ShapeDtypeStruct((D_MODEL, D_MODEL), F32),
                   nat(BF16), nat(F32, STAT_WIDTH), nat(BF16), nat(BF16),
                   nat(BF16), *fb_shapes, nat(F32, STAT_WIDTH), *st_shapes, nat(F32, STAT_WIDTH), *st_shapes),
        scratch_shapes=[_fold_scratch(tm), pltpu.VMEM((D_MODEL, tm), BF16), pltpu.VMEM((tm, D_MODEL), BF16)],
        compiler_params=_params(("arbitrary",)),
    )(oa, ob1, lb1, ob4, lb4, ob16, lb16, gate_a, gate_b, x, target, w_out, spread, gather)


def _dproj_assemble(dqa, dka, dva, dga, dgb, dq1, dk1, dv1, dq4, dk4, dv4, dq16, dk16, dv16, tqa, tqb, tkb, tka,
                    qkg, cos4, sin4, bmean):
    tm = ROW_TILE

    def norm_rope_bwd(d_out, t, g, cos, sin, bm, scale):
        d_r = d_out * scale
        dyv = d_r * cos + _swap_halves(d_r * sin)
        rr = lax.rsqrt(_head_sum(t * t, bm) + EPS)
        that = t * rr
        dgain = jnp.sum(dyv * that, axis=0, keepdims=True)
        gdy = dyv * g
        dt = rr * (gdy - that * _head_sum(that * gdy, bm))
        return dt, dgain

    def body(dqa_ref, dka_ref, dva_ref, dga_ref, dgb_ref, dq1_ref, dk1_ref, dv1_ref, dq4_ref, dk4_ref, dv4_ref,
             dq16_ref, dk16_ref, dv16_ref, tqa_ref, tqb_ref, tkb_ref, tka_ref, qkg_ref, cos_ref, sin_ref, bm_ref,
             dproj_ref, dqkg_ref, s_f):
        i = pl.program_id(0)
        cos, sin, bm = cos_ref[...], sin_ref[...], bm_ref[...]

        def merged(nat_ref, f4_ref, f16_ref):
            return nat_ref[...].astype(F32) + _unfold_load(f4_ref, s_f, 4, tm) + _unfold_load(f16_ref, s_f, 16, tm)

        @pl.when(i == 0)
        def _():
            dqkg_ref[...] = jnp.zeros_like(dqkg_ref)

        def through(d_out, t, row, scale, c0, paired=False):
            g = qkg_ref[row:row + 1, :]
            tot = jnp.zeros((1, PAIR), F32)
            dts = []
            for j in range(d_out.shape[1] // PAIR):
                cols = slice(j * PAIR, (j + 1) * PAIR)
                dt, dg = norm_rope_bwd(d_out[:, cols], t[:, cols], g, cos, sin, bm, scale)
                dts.append(dt)
                tot = tot + dg
            if paired:
                dts = _unpair_tiles(dts)
            for j, dt in enumerate(dts):
                dproj_ref[:, c0 + j * PAIR:c0 + (j + 1) * PAIR] = dt.astype(BF16)
            dqkg_ref[row:row + 1, :] += tot

        through(dqa_ref[...].astype(F32), tqa_ref[...].astype(F32), 0, HEAD_DIM ** -0.5, C_QA, paired=True)
        through(dka_ref[...].astype(F32), tka_ref[...].astype(F32), 1, LN2, C_KA)
        through(merged(dq1_ref, dq4_ref, dq16_ref), tqb_ref[...].astype(F32), 2, HEAD_DIM ** -0.5, C_QB)
        through(merged(dk1_ref, dk4_ref, dk16_ref), tkb_ref[...].astype(F32), 3, LN2, C_KB)
        dproj_ref[:, C_VB:C_VB + HALF_WIDTH] = merged(dv1_ref, dv4_ref, dv16_ref).astype(BF16)
        dproj_ref[:, C_GA:C_GA + HALF_WIDTH] = jnp.concatenate(
            _unpair_tiles(_tiles(dga_ref[...].astype(F32))), axis=1).astype(BF16)
        dproj_ref[:, C_GB:C_GB + HALF_WIDTH] = dgb_ref[...].astype(BF16)
        dproj_ref[:, C_VA:C_VA + KV_A_WIDTH] = dva_ref[...].astype(BF16)

    row = lambda width: pl.BlockSpec((tm, width), lambda i: (i, 0))
    full = lambda a: pl.BlockSpec(a.shape, lambda i: (0,) * a.ndim)
    _, ff_specs = _fold_specs(tm, F32)
    return pl.pallas_call(
        body, name="dproj_assemble", grid=(SEQ // tm,),
        in_specs=[row(HALF_WIDTH), row(KV_A_WIDTH), row(KV_A_WIDTH), row(HALF_WIDTH), row(HALF_WIDTH),
                  row(HALF_WIDTH), row(HALF_WIDTH), row(HALF_WIDTH), ff_specs[0], ff_specs[0], ff_specs[0],
                  ff_specs[1], ff_specs[1], ff_specs[1],
                  row(HALF_WIDTH), row(HALF_WIDTH), row(HALF_WIDTH), row(KV_A_WIDTH),
                  full(qkg), row(PAIR), row(PAIR), full(bmean)],
        out_specs=(row(IN_WIDTH), pl.BlockSpec((SMALL_ROWS, PAIR), lambda i: (0, 0))),
        out_shape=(jax.ShapeDtypeStruct((SEQ, IN_WIDTH), BF16), jax.ShapeDtypeStruct((SMALL_ROWS, PAIR), F32)),
        scratch_shapes=[_fold_scratch(tm)],
        compiler_params=_params(("arbitrary",)),
    )(dqa, dka, dva, dga, dgb, dq1, dk1, dv1, dq4, dk4, dv4, dq16, dk16, dv16, tqa, tqb, tkb, tka, qkg, cos4, sin4, bmean)


def _input_grad_reduce(dproj, w, x, gain, dy, blocks_in, blocks_out, small):
    tm = ROW_TILE
    n_steps = SEQ // tm
    stage2_step, stage3_step = 3, 7
    shapes = (blocks_in.shape[1:], blocks_out.shape[1:])

    def body(dp_ref, w_ref, x_ref, g_ref, dy_ref, ga_hbm, gb_hbm, small_ref,
             gx_ref, out_a, out_b, small_out_ref, dgain_out_ref,
             part_a, part_b, sib_a, sib_b, wire_a, wire_b, chips_a, chips_b, small_all, dgain_acc, dgain_all,
             load_sems, sib_send, sib_recv, chip_send, chip_recv, small_send, small_recv, dgain_send, dgain_recv):
        i = pl.program_id(0)
        x, y, c = lax.axis_index("x"), lax.axis_index("y"), lax.axis_index("c")
        sibling = (x, y, 1 - c)
        chips = [(x, y), (1 - x, y), (x, 1 - y), (1 - x, 1 - y)]
        my_id = 4 * x + 2 * y + c
        g_hbm, part, from_sib = (ga_hbm, gb_hbm), (part_a, part_b), (sib_a, sib_b)
        to_wire, from_chips, out = (wire_a, wire_b), (chips_a, chips_b), (out_a, out_b)
        both = (0, 1)

        def blk(a, chip, core):
            return g_hbm[a].at[4 * chip[0] + 2 * chip[1] + core]

        def to_all(src, dst_all, send, recv):
            copies = []
            for rel in range(1, N_DEV):
                dx, dy_, dc = (rel >> 2) & 1, (rel >> 1) & 1, rel & 1
                to = (1 - x if dx else x, 1 - y if dy_ else y, 1 - c if dc else c)
                copies.append(pltpu.make_async_remote_copy(
                    src_ref=src, dst_ref=dst_all.at[my_id], send_sem=send.at[rel - 1], recv_sem=recv.at[rel - 1],
                    device_id=to, device_id_type=MESH))
            return copies

        small_copies = to_all(small_all.at[my_id], small_all, small_send, small_recv)
        dgain_copies = to_all(dgain_acc, dgain_all, dgain_send, dgain_recv)
        loads = [[pltpu.make_async_copy(blk(a, chips[k], c), part[a].at[k], load_sems.at[a, k]) for k in range(4)] for a in both]
        to_sib = [[pltpu.make_async_remote_copy(
            src_ref=blk(a, chips[k], 1 - c), dst_ref=from_sib[a].at[k], send_sem=sib_send.at[a, k], recv_sem=sib_recv.at[a, k],
            device_id=sibling, device_id_type=MESH) for k in range(4)] for a in both]
        first, other, k_first, k_other = _routes()
        to_chips = [[pltpu.make_async_remote_copy(
            src_ref=to_wire[a].at[s], dst_ref=from_chips[a].at[s], send_sem=chip_send.at[a, s], recv_sem=chip_recv.at[a, s],
            device_id=(first, first, other)[s], device_id_type=MESH) for s in range(3)] for a in both]

        def chip_partial(a, k):
            return part[a][k].astype(F32) + from_sib[a][k].astype(F32)

        @pl.when(i == 0)
        def _():
            small_all[my_id] = small_ref[...]
            for cp in small_copies:
                cp.start()
            for k in (1, 2, 3, 0):
                for a in both:
                    loads[a][k].start()
                    to_sib[a][k].start()

        @pl.when(i == stage2_step)
        def _():
            for k in (1, 2, 3):
                for a in both:
                    loads[a][k].wait()
                    to_sib[a][k].wait_recv()
            for s, k in ((0, 3), (1, k_first)):
                for a in both:
                    to_wire[a][s] = chip_partial(a, k).astype(BF16)
                    to_chips[a][s].start()

        @pl.when(i == stage3_step)
        def _():
            for a in both:
                to_chips[a][0].wait_recv()
                to_wire[a][2] = (chip_partial(a, k_other) + from_chips[a][0].astype(F32)).astype(BF16)
                to_chips[a][2].start()

        dh = lax.dot_general(dp_ref[...], w_ref[...], (((1,), (1,)), ((), ())), preferred_element_type=F32)
        xf = x_ref[...]
        r = lax.rsqrt(jnp.mean(xf * xf, axis=-1, keepdims=True) + EPS)
        xhat = xf * r
        dg = jnp.sum(dh * xhat, axis=0, keepdims=True)
        dxh = dh * g_ref[...]
        dx = r * (dxh - xhat * jnp.mean(dxh * xhat, axis=-1, keepdims=True))
        gx_ref[...] = dy_ref[...] + dx

        @pl.when(i == 0)
        def _():
            dgain_acc[...] = dg

        @pl.when(i > 0)
        def _():
            dgain_acc[...] += dg

        @pl.when(i == n_steps - 1)
        def _():
            dgain_all[my_id] = dgain_acc[...]
            for cp in dgain_copies:
                cp.start()
            for a in both:
                loads[a][0].wait()
                to_sib[a][0].wait_recv()
                acc = chip_partial(a, 0)
                for s in (1, 2):
                    to_chips[a][s].wait_recv()
                    acc = acc + from_chips[a][s].astype(F32)
                out[a][...] = acc
            for copies, gathered, dst in ((small_copies, small_all, small_out_ref), (dgain_copies, dgain_all, dgain_out_ref)):
                for cp in copies:
                    cp.wait_recv()
                tot = gathered[0]
                for d in range(1, N_DEV):
                    tot = tot + gathered[d]
                dst[...] = tot
            for cp in to_sib[0] + to_sib[1] + to_chips[0] + to_chips[1] + small_copies + dgain_copies:
                cp.wait_send()

    row = lambda width: pl.BlockSpec((tm, width), lambda i: (i, 0))
    full = lambda a: pl.BlockSpec(a.shape, lambda i: (0,) * a.ndim)
    whole = lambda shape: pl.BlockSpec(shape, lambda i: (0,) * len(shape))
    hbm = pl.BlockSpec(memory_space=pl.ANY)
    dtypes = (blocks_in.dtype, blocks_out.dtype)
    buf = lambda n, dts: [pltpu.VMEM((n,) + s, dt) for s, dt in zip(shapes, dts)]
    return pl.pallas_call(
        body, name="input_grad_rs", grid=(n_steps,),
        in_specs=[row(IN_WIDTH), full(w), row(D_MODEL), full(gain), row(D_MODEL), hbm, hbm, full(small)],
        out_specs=(row(D_MODEL), whole(shapes[0]), whole(shapes[1]), whole((SMALL_ROWS, SMALL_COLS)), whole((1, D_MODEL))),
        out_shape=(jax.ShapeDtypeStruct((SEQ, D_MODEL), F32), jax.ShapeDtypeStruct(shapes[0], F32),
                   jax.ShapeDtypeStruct(shapes[1], F32), jax.ShapeDtypeStruct((SMALL_ROWS, SMALL_COLS), F32),
                   jax.ShapeDtypeStruct((1, D_MODEL), F32)),
        scratch_shapes=[*buf(4, dtypes), *buf(4, dtypes), *buf(3, (BF16, BF16)), *buf(3, (BF16, BF16)),
                        pltpu.VMEM((N_DEV, SMALL_ROWS, SMALL_COLS), F32),
                        pltpu.VMEM((1, D_MODEL), F32), pltpu.VMEM((N_DEV, 1, D_MODEL), F32),
                        pltpu.SemaphoreType.DMA((2, 4)), pltpu.SemaphoreType.DMA((2, 4)), pltpu.SemaphoreType.DMA((2, 4)),
                        pltpu.SemaphoreType.DMA((2, 3)), pltpu.SemaphoreType.DMA((2, 3)),
                        pltpu.SemaphoreType.DMA((7,)), pltpu.SemaphoreType.DMA((7,)),
                        pltpu.SemaphoreType.DMA((7,)), pltpu.SemaphoreType.DMA((7,))],
        compiler_params=_params(("arbitrary",)),
    )(dproj, w, x, gain, dy, blocks_in, blocks_out, small)


def _weight_grad(h_t, dproj):
    tk = 1024
    cb = IN_WIDTH // 2
    n_k = SEQ // tk

    def body(ht_ref, dp_ref, out_ref, acc):
        k = pl.program_id(1)
        upd = jnp.dot(ht_ref[...], dp_ref[...], preferred_element_type=F32)

        @pl.when(k == 0)
        def _():
            acc[...] = upd

        @pl.when(k > 0)
        def _():
            acc[...] += upd

        @pl.when(k == n_k - 1)
        def _():
            for b in range(N_DEV // 2):
                out_ref[b] = acc[:, b * SHARD_IN:(b + 1) * SHARD_IN].astype(BF16)

    return pl.pallas_call(
        body, name="weight_grad", grid=(2, n_k),
        in_specs=[pl.BlockSpec((D_MODEL, tk), lambda j, k: (0, k)), pl.BlockSpec((tk, cb), lambda j, k: (k, j))],
        out_specs=pl.BlockSpec((N_DEV // 2, D_MODEL, SHARD_IN), lambda j, k: (j, 0, 0)),
        out_shape=jax.ShapeDtypeStruct((N_DEV, D_MODEL, SHARD_IN), BF16),
        scratch_shapes=[pltpu.VMEM((D_MODEL, cb), F32)],
        compiler_params=_params(("arbitrary", "arbitrary")),
    )(h_t, dproj)


def _adamw_update(w, g, m, v):
    nm = ADAM_B1 * m + (1.0 - ADAM_B1) * g
    nv = ADAM_B2 * v + (1.0 - ADAM_B2) * jnp.square(g)
    m_hat = nm / (1.0 - ADAM_B1 ** ADAM_STEP)
    v_hat = nv / (1.0 - ADAM_B2 ** ADAM_STEP)
    return -ADAM_LR * (m_hat / (jnp.sqrt(v_hat) + ADAM_EPS) + ADAM_WD * w), nm, nv


def _adamw(transposed, groups):
    def body(*refs):
        ins, outs = refs[:4 * (len(groups) + 1)], refs[4 * (len(groups) + 1):]
        wt_ref, g_ref, mt_ref, vt_ref = ins[:4]
        gt_ref, dt_ref, nmt_ref, nvt_ref = outs[:4]
        cols = g_ref.shape[1]
        for off in [*range(0, cols - LANES, LANES), cols - LANES]:
            band = slice(off, off + LANES)
            gv = g_ref[:, band].T
            gt_ref[band, :] = gv
            dt_ref[band, :], nmt_ref[band, :], nvt_ref[band, :] = _adamw_update(
                wt_ref[band, :], gv, mt_ref[band, :], vt_ref[band, :])
        for i in range(len(groups)):
            w_ref, g_ref, m_ref, v_ref = ins[4 * i + 4:4 * i + 8]
            d_ref, nm_ref, nv_ref = outs[3 * i + 4:3 * i + 7]
            d_ref[...], nm_ref[...], nv_ref[...] = _adamw_update(w_ref[...], g_ref[...], m_ref[...], v_ref[...])

    vmem = pl.BlockSpec(memory_space=pltpu.VMEM)
    out_shape = [jax.ShapeDtypeStruct(transposed[0].shape, F32)] * 4
    out_shape += [jax.ShapeDtypeStruct(w.shape, F32) for w, _, _, _ in groups for _ in range(3)]
    outs = pl.pallas_call(
        body, name="adamw", in_specs=[vmem] * (4 * (len(groups) + 1)), out_specs=(vmem,) * len(out_shape),
        out_shape=tuple(out_shape), compiler_params=pltpu.CompilerParams(vmem_limit_bytes=VMEM_LIMIT),
    )(*transposed, *[a for group in groups for a in group])
    return tuple(outs[:4]), [tuple(outs[3 * i + 4:3 * i + 7]) for i in range(len(groups))]


SMALL_USED = D_MODEL + 4 * HEAD_DIM + 8


def _pack_small(norm_gain, qa, ka, sinks, qb, kb, extra=None):
    parts = [norm_gain.reshape(-1), qa.reshape(-1), ka.reshape(-1), sinks.reshape(-1), qb.reshape(-1), kb.reshape(-1)]
    if extra is not None:
        parts.append(extra.reshape(-1))
    flat = jnp.concatenate(parts)
    flat = jnp.pad(flat, (0, SMALL_ROWS * SMALL_COLS - flat.shape[0]))
    return flat.reshape(SMALL_ROWS, SMALL_COLS)


def _unpack_small(a):
    flat = a.reshape(-1)
    sizes = (D_MODEL, HEAD_DIM, HEAD_DIM, 8, HEAD_DIM, HEAD_DIM)
    out, off = [], 0
    for s in sizes:
        out.append(flat[off:off + s].reshape(1, s))
        off += s
    return out


def _fold_heads(row):
    return row[0, :HEAD_DIM] + row[0, HEAD_DIM:]


def kernel(x, norm_gain, w_in, q_norm_a, k_norm_a, sinks_a, q_norm_b, k_norm_b, w_out, loss_target, m_norm_gain, m_w_in, m_q_norm_a, m_k_norm_a, m_sinks_a, m_q_norm_b, m_k_norm_b, m_w_out, v_norm_gain, v_w_in, v_q_norm_a, v_k_norm_a, v_sinks_a, v_q_norm_b, v_k_norm_b, v_w_out):
    x2, tgt = x[0], loss_target[0]
    w_in_sh, w_out_sh = w_in[0], w_out[0]

    w_full = _all_gather_w_in(w_in_sh)

    inv = np.float32(ROPE_THETA) ** (-np.arange(HEAD_DIM // 2, dtype=np.float32) / np.float32(HEAD_DIM // 2))
    ang = np.arange(SEQ, dtype=np.float32)[:, None] * inv[None, :].astype(np.float32)
    cos, sin = np.cos(ang).astype(np.float32), np.sin(ang).astype(np.float32)
    cos4 = jnp.asarray(np.concatenate([cos, cos, cos, cos], axis=1))
    sin4 = jnp.asarray(np.concatenate([-sin, sin, -sin, sin], axis=1))
    blockdiag = np.kron(np.eye(2, dtype=np.float32), np.ones((HEAD_DIM, HEAD_DIM), np.float32))
    bmean = jnp.asarray(blockdiag / HEAD_DIM, dtype=BF16)
    gather_np = np.kron(np.eye(2 * N_PAIRS, dtype=np.float32), np.ones((HEAD_DIM, STAT_REP), np.float32))
    spread_np = np.kron(np.eye(2 * N_PAIRS, dtype=np.float32), np.ones((STAT_REP, HEAD_DIM), np.float32))
    spread_np[np.arange(STAT_WIDTH) % STAT_REP != 0] = 0.0
    gather, spread = jnp.asarray(gather_np, dtype=BF16), jnp.asarray(spread_np, dtype=BF16)
    two = lambda g: jnp.concatenate([g, g], axis=1)
    qkg = jnp.concatenate([two(q_norm_a), two(k_norm_a), two(q_norm_b), two(k_norm_b),
                           jnp.zeros((SMALL_ROWS - 4, PAIR), F32)], axis=0)
    sinks_paired = jnp.stack([sinks_a[0, :N_PAIRS], sinks_a[0, N_PAIRS:]], axis=1) * LOG2E
    sink_rows = jnp.concatenate([jnp.repeat(sinks_paired, BLOCK, axis=1),
                                 jnp.zeros((SMALL_ROWS - N_PAIRS, 2 * BLOCK), F32)], axis=0)

    (tqa, tka, tqb, tkb, gate_a, gate_b, h_t, qa, ka, va, qb, kb, vb, qb4, qb16, kb4, kb16, vb4, vb16,
     gathered_out) = _proj_fwd(x2, norm_gain, w_full, qkg, cos4, sin4, bmean, w_out_sh)
    wo_full = gathered_out.reshape(D_MODEL, D_MODEL)
    (oa, la), (ob1, lb1), (ob4, lb4), (ob16, lb16) = _attn_fwd("attn_fwd", [
        (qa[None], ka[None], va[None], sink_rows, BLOCK - 1), (qb[None], kb[None], vb[None], None, BLOCK),
        (qb4, kb4, vb4, None, BLOCK), (qb16, kb16, vb16, None, BLOCK)])
    (loss_cols, dy, gwo, doa, dla, dga, dgb, dob, dob4, dob16, dlb, dlb4, dlb16, lse_b, lse4, lse16) = _tail(
        oa[0], ob1[0], lb1[0], ob4, lb4, ob16, lb16, gate_a, gate_b, x2, tgt, wo_full, spread, gather)

    dqa, dka, dva, dsink = _attn_bwd("attn_a_bwd", qa[None], ka[None], va[None], doa[None], la, dla[None], sink_rows, BLOCK - 1)
    dq1, dk1, dv1 = _attn_bwd("attn_b1_bwd", qb[None], kb[None], vb[None], dob[None], lse_b[None], dlb[None], None, BLOCK)
    dq4, dk4, dv4 = _attn_bwd("attn_b4_bwd", qb4, kb4, vb4, dob4, lse4, dlb4, None, BLOCK)
    dq16, dk16, dv16 = _attn_bwd("attn_b16_bwd", qb16, kb16, vb16, dob16, lse16, dlb16, None, BLOCK)
    dproj, dqkg = _dproj_assemble(dqa[0], dka[0], dva[0], dga, dgb, dq1[0], dk1[0], dv1[0], dq4, dk4, dv4,
                                  dq16, dk16, dv16, tqa, tqb, tkb, tka, qkg, cos4, sin4, bmean)
    gw_in = _weight_grad(h_t, dproj)

    blocks_in = gw_in
    blocks_out = gwo.reshape(N_DEV, SHARD_OUT, D_MODEL)
    g_sinks = jnp.concatenate([jnp.sum(dsink[:N_PAIRS, :BLOCK], axis=1), jnp.sum(dsink[:N_PAIRS, BLOCK:], axis=1)])
    small = _pack_small(jnp.zeros((D_MODEL,), F32), _fold_heads(dqkg[0:1]), _fold_heads(dqkg[1:2]), g_sinks,
                        _fold_heads(dqkg[2:3]), _fold_heads(dqkg[3:4]), extra=0.5 * jnp.sum(loss_cols) / D_MODEL)
    grad_x, g_w_in, g_w_out, small_red, dgain_red = _input_grad_reduce(
        dproj, w_full, x2, norm_gain, dy, blocks_in, blocks_out, small)
    n_gain_rows = D_MODEL // SMALL_COLS
    small_red = jnp.concatenate([dgain_red.reshape(n_gain_rows, SMALL_COLS), small_red[n_gain_rows:]], axis=0)
    g_small = _unpack_small(small_red)

    as_held = lambda a: jnp.swapaxes(a, 1, 2)[0]
    w_in_results, [(d_out, nm_out, nv_out), (d_s, nm_s, nv_s)] = _adamw(
        (as_held(w_in), g_w_in, as_held(m_w_in), as_held(v_w_in)), [
        (w_out_sh, g_w_out, m_w_out[0], v_w_out[0]),
        (_pack_small(norm_gain, q_norm_a, k_norm_a, sinks_a, q_norm_b, k_norm_b), small_red,
         _pack_small(m_norm_gain, m_q_norm_a, m_k_norm_a, m_sinks_a, m_q_norm_b, m_k_norm_b),
         _pack_small(v_norm_gain, v_q_norm_a, v_k_norm_a, v_sinks_a, v_q_norm_b, v_k_norm_b))])
    d_small, nm_small, nv_small = _unpack_small(d_s), _unpack_small(nm_s), _unpack_small(nv_s)
    g_in, d_in, nm_in, nv_in = [jnp.swapaxes(a, 0, 1) for a in w_in_results]

    loss = small_red.reshape(-1)[SMALL_USED]

    def assemble(small_list, big_in, big_out):
        ng, qa_, ka_, sk_, qb_, kb_ = small_list
        return [ng, big_in[None], qa_, ka_, sk_, qb_, kb_, big_out[None]]

    return (loss, grad_x[None], *assemble(g_small, g_in, g_w_out), *assemble(d_small, d_in, d_out),
            *assemble(nm_small, nm_in, nm_out), *assemble(nv_small, nv_in, nv_out))
```

```python
import numpy as np
import jax
import jax.numpy as jnp
from jax import lax
from jax.experimental import pallas as pl
from jax.experimental.pallas import tpu as pltpu

F32 = jnp.float32
BF16 = jnp.bfloat16

SEQ = 4096
D_MODEL = 1024
HEAD_DIM = 64
PAIR = 2 * HEAD_DIM
N_PAIRS = 4
HALF_WIDTH = N_PAIRS * PAIR
KV_A_WIDTH = 128
IN_WIDTH = 3328
LANES = 128
BLOCK = 128
STAT_REP = 16
STAT_WIDTH = 128
EPS = 1e-6
NEG = -1e30
ROPE_THETA = 10000.0
LOG2E = 1.4426950408889634
LN2 = 0.6931471805599453
Q_SCALE = HEAD_DIM ** -0.5 * LOG2E
N_DEV = 8
SHARD_IN = IN_WIDTH // N_DEV
SHARD_OUT = D_MODEL // N_DEV
SMALL_ROWS, SMALL_COLS = 8, 256

C_QA, C_KA, C_VA, C_GA, C_QB, C_KB, C_VB, C_GB = 0, 512, 640, 768, 1280, 1792, 2304, 2816

ADAM_LR = 0.001
ADAM_B1 = 0.9
ADAM_B2 = 0.999
ADAM_EPS = 1e-08
ADAM_WD = 0.01
ADAM_STEP = 10

ROW_TILE = 256
PROJ_ROW_TILE = 512
FWD_BLOCKS_PER_STEP = 8
BWD_BLOCKS_PER_STEP = 8
VMEM_LIMIT = 56 * 1024 * 1024

MESH = pl.DeviceIdType.MESH


def _params(sem, vmem=VMEM_LIMIT):
    return pltpu.CompilerParams(dimension_semantics=sem, vmem_limit_bytes=vmem)


def _head_sum(v, bm):
    return jnp.dot(v.astype(BF16), bm, preferred_element_type=F32)


def _swap_halves(y):
    lane = lax.broadcasted_iota(jnp.int32, y.shape, 1)
    first = (lane & 32) == 0
    return jnp.where(first, pltpu.roll(y, 96, 1), pltpu.roll(y, 32, 1))


def _sigmoid(g):
    return 1.0 / (1.0 + jnp.exp(-g))


def _tiles(a):
    return [a[:, j * PAIR:(j + 1) * PAIR] for j in range(N_PAIRS)]


def _pair_tiles(t):
    low = lax.broadcasted_iota(jnp.int32, t[0].shape, 1) < HEAD_DIM
    r = [pltpu.roll(a, HEAD_DIM, 1) for a in t]
    return [jnp.where(low, t[0], r[2]), jnp.where(low, r[0], t[2]), jnp.where(low, t[1], r[3]), jnp.where(low, r[1], t[3])]


def _unpair_tiles(p):
    low = lax.broadcasted_iota(jnp.int32, p[0].shape, 1) < HEAD_DIM
    r = [pltpu.roll(a, HEAD_DIM, 1) for a in p]
    return [jnp.where(low, p[0], r[1]), jnp.where(low, p[2], r[3]), jnp.where(low, r[0], p[1]), jnp.where(low, r[2], p[3])]


def _routes():
    x, y, c = lax.axis_index("x"), lax.axis_index("y"), lax.axis_index("c")
    north = c == 1
    first = (jnp.where(north, 1 - x, x), jnp.where(north, y, 1 - y), c)
    other = (jnp.where(north, x, 1 - x), jnp.where(north, 1 - y, y), c)
    k_first = jnp.where(north, 1, 2)
    return first, other, k_first, 3 - k_first


def _gather_plan(mine_ref, out_ref, send_sems, recv_sems):
    x, y, c = lax.axis_index("x"), lax.axis_index("y"), lax.axis_index("c")
    me, sibling, diag = (x, y, c), (x, y, 1 - c), (1 - x, 1 - y, c)
    first, other, k_first, k_other = _routes()

    def slot(px, py, pc):
        return out_ref.at[4 * px + 2 * py + pc]

    def copy(k, block, to, from_mine=False):
        return pltpu.make_async_remote_copy(
            src_ref=mine_ref if from_mine else slot(*block), dst_ref=slot(*block),
            send_sem=send_sems.at[k], recv_sem=recv_sems.at[k], device_id=to, device_id_type=MESH)

    sends = [copy(0, me, sibling, True), copy(1, me, (1 - x, y, c), True), copy(2, me, (x, 1 - y, c), True)]
    stages = [(copy(k_first, first, me), [copy(3, first, other), copy(3 + k_first, first, sibling)]),
              (copy(k_other, other, me), [copy(3 + k_other, other, sibling)]),
              (copy(3, diag, me), [copy(6, diag, sibling)])]
    from_sibling = [copy(0, sibling, me), copy(4, (1 - x, y, 1 - c), me), copy(5, (x, 1 - y, 1 - c), me),
                    copy(6, (1 - x, 1 - y, 1 - c), me)]
    return slot(*me), sends, stages, from_sibling


GATHER_SCRATCH = [pltpu.SemaphoreType.DMA((7,)), pltpu.SemaphoreType.DMA((7,))]


def _all_gather_w_in(w_in_sh):
    rows, cols = w_in_sh.shape

    def body(w_ref, out_ref, mine_ref, blocks, send_sems, recv_sems):
        mine_ref[...] = w_ref[...].astype(BF16)
        my_slot, sends, stages, from_sibling = _gather_plan(mine_ref, blocks, send_sems, recv_sems)
        for cp in sends:
            cp.start()
        my_slot[...] = mine_ref[...]
        for arrival, forwards in stages:
            arrival.wait_recv()
            for cp in forwards:
                cp.start()
        for arrival in from_sibling:
            arrival.wait_recv()
        for cp in sends + [cp for _, forwards in stages for cp in forwards]:
            cp.wait_send()
        for d in range(N_DEV):
            out_ref[:, d * cols:(d + 1) * cols] = blocks[d]

    vmem = pl.BlockSpec(memory_space=pltpu.VMEM)
    return pl.pallas_call(
        body, name="ag_w_in",
        out_shape=jax.ShapeDtypeStruct((rows, N_DEV * cols), BF16),
        in_specs=[vmem], out_specs=vmem,
        scratch_shapes=[pltpu.VMEM((rows, cols), BF16), pltpu.VMEM((N_DEV, rows, cols), BF16)] + GATHER_SCRATCH,
        compiler_params=pltpu.CompilerParams(vmem_limit_bytes=VMEM_LIMIT),
    )(w_in_sh)


def _fold_scratch(tm):
    return pltpu.VMEM((N_PAIRS, tm, PAIR), F32)


def _fold_store(val, scr, out4, out16, tm):
    groups = range(val.shape[1] // PAIR)
    for j in groups:
        scr[j] = val[:, j * PAIR:(j + 1) * PAIR]
    for dil, out in ((4, out4), (16, out16)):
        for r in range(dil):
            for j in groups:
                out[r, :, j * PAIR:(j + 1) * PAIR] = scr[j, pl.ds(r, tm // dil, stride=dil), :].astype(out.dtype)


def _unfold_load(src, scr, dil, tm):
    groups = range(src.shape[2] // PAIR)
    for r in range(dil):
        for j in groups:
            scr[j, pl.ds(r, tm // dil, stride=dil), :] = src[r, :, j * PAIR:(j + 1) * PAIR].astype(F32)
    return jnp.concatenate([scr[j] for j in groups], axis=1)


def _fold_specs(tm, dtype, width=HALF_WIDTH):
    shapes = (jax.ShapeDtypeStruct((4, SEQ // 4, width), dtype), jax.ShapeDtypeStruct((16, SEQ // 16, width), dtype))
    specs = (pl.BlockSpec((4, tm // 4, width), lambda i: (0, i, 0)),
             pl.BlockSpec((16, tm // 16, width), lambda i: (0, i, 0)))
    return shapes, specs


def _proj_fwd(x, gain, w, qkg, cos4, sin4, bmean, w_out_sh):
    tm = PROJ_ROW_TILE
    n_steps = SEQ // tm

    def norm_rope(t, g, cos, sin, bm, scale):
        rr = lax.rsqrt(_head_sum(t * t, bm) + EPS)
        yv = t * rr * g
        return (yv * cos + _swap_halves(yv) * sin) * scale

    def body(x_ref, g_ref, w_ref, qkg_ref, cos_ref, sin_ref, bm_ref, wo_ref,
             tqa_ref, tka_ref, tqb_ref, tkb_ref, ga_ref, gb_ref, ht_ref, qa_ref, ka_ref, va_ref, qb_ref, kb_ref, vb_ref,
             qb4_ref, qb16_ref, kb4_ref, kb16_ref, vb4_ref, vb16_ref, wo_all_ref,
             proj, scr, wo_mine, wo_all, send_sems, recv_sems):
        i = pl.program_id(0)
        my_slot, sends, stages, from_sibling = _gather_plan(wo_mine, wo_all, send_sems, recv_sems)

        @pl.when(i == 0)
        def _():
            wo_mine[...] = wo_ref[...].astype(BF16)
            for cp in sends:
                cp.start()
            my_slot[...] = wo_mine[...]

        @pl.when(i == n_steps // 2)
        def _():
            for arrival, forwards in stages[:2]:
                arrival.wait_recv()
                for cp in forwards:
                    cp.start()

        xf = x_ref[...]
        r = lax.rsqrt(jnp.mean(xf * xf, axis=-1, keepdims=True) + EPS)
        hf = xf * r * g_ref[...]
        ht_ref[...] = hf.T.astype(BF16)
        cos, sin, bm = cos_ref[...], sin_ref[...], bm_ref[...]
        proj[...] = jnp.dot(hf.astype(BF16), w_ref[...], preferred_element_type=F32)

        def roped(tiles, row, scale):
            g = qkg_ref[row:row + 1, :]
            return jnp.concatenate([norm_rope(t, g, cos, sin, bm, scale) for t in tiles], axis=1)

        tqa = _pair_tiles(_tiles(proj[:, C_QA:C_QA + HALF_WIDTH]))
        tqa_ref[...] = jnp.concatenate(tqa, axis=1).astype(BF16)
        qa_ref[...] = roped(tqa, 0, Q_SCALE).astype(BF16)
        ga_ref[...] = jnp.concatenate(_pair_tiles(_tiles(proj[:, C_GA:C_GA + HALF_WIDTH])), axis=1).astype(BF16)
        gb_ref[...] = proj[:, C_GB:C_GB + HALF_WIDTH].astype(BF16)
        tqb = proj[:, C_QB:C_QB + HALF_WIDTH]
        tqb_ref[...] = tqb.astype(BF16)
        qb = roped(_tiles(tqb), 2, Q_SCALE)
        qb_ref[...] = qb.astype(BF16)
        _fold_store(qb, scr, qb4_ref, qb16_ref, tm)
        tkb = proj[:, C_KB:C_KB + HALF_WIDTH]
        tkb_ref[...] = tkb.astype(BF16)
        kb = roped(_tiles(tkb), 3, 1.0)
        kb_ref[...] = kb.astype(BF16)
        _fold_store(kb, scr, kb4_ref, kb16_ref, tm)
        vb = proj[:, C_VB:C_VB + HALF_WIDTH]
        vb_ref[...] = vb.astype(BF16)
        _fold_store(vb, scr, vb4_ref, vb16_ref, tm)
        tka = proj[:, C_KA:C_KA + KV_A_WIDTH]
        tka_ref[...] = tka.astype(BF16)
        ka_ref[...] = roped([tka], 1, 1.0).astype(BF16)
        va_ref[...] = proj[:, C_VA:C_VA + KV_A_WIDTH].astype(BF16)

        @pl.when(i == n_steps - 1)
        def _():
            arrival, forwards = stages[2]
            arrival.wait_recv()
            for cp in forwards:
                cp.start()
            for arrival in from_sibling:
                arrival.wait_recv()
            for cp in sends + [cp for _, forwards in stages for cp in forwards]:
                cp.wait_send()
            wo_all_ref[...] = wo_all[...]

    row = lambda width: pl.BlockSpec((tm, width), lambda i: (i, 0))
    full = lambda a: pl.BlockSpec(a.shape, lambda i: (0,) * a.ndim)
    nat = lambda width, dtype=BF16: jax.ShapeDtypeStruct((SEQ, width), dtype)
    f_shapes, f_specs = _fold_specs(tm, BF16)
    return pl.pallas_call(
        body, name="proj_fwd", grid=(SEQ // tm,),
        in_specs=[row(D_MODEL), full(gain), full(w), full(qkg), row(PAIR), row(PAIR), full(bmean), full(w_out_sh)],
        out_specs=(row(HALF_WIDTH), row(KV_A_WIDTH), row(HALF_WIDTH), row(HALF_WIDTH), row(HALF_WIDTH), row(HALF_WIDTH),
                   pl.BlockSpec((D_MODEL, tm), lambda i: (0, i)),
                   row(HALF_WIDTH), row(KV_A_WIDTH), row(KV_A_WIDTH), row(HALF_WIDTH), row(HALF_WIDTH), row(HALF_WIDTH),
                   *f_specs, *f_specs, *f_specs,
                   pl.BlockSpec((N_DEV,) + w_out_sh.shape, lambda i: (0, 0, 0))),
        out_shape=(nat(HALF_WIDTH), nat(KV_A_WIDTH), nat(HALF_WIDTH), nat(HALF_WIDTH), nat(HALF_WIDTH), nat(HALF_WIDTH),
                   jax.ShapeDtypeStruct((D_MODEL, SEQ), BF16),
                   nat(HALF_WIDTH), nat(KV_A_WIDTH), nat(KV_A_WIDTH), nat(HALF_WIDTH), nat(HALF_WIDTH), nat(HALF_WIDTH),
                   *f_shapes, *f_shapes, *f_shapes,
                   jax.ShapeDtypeStruct((N_DEV,) + w_out_sh.shape, BF16)),
        scratch_shapes=[pltpu.VMEM((tm, IN_WIDTH), F32), _fold_scratch(tm), pltpu.VMEM(w_out_sh.shape, BF16),
                        pltpu.VMEM((N_DEV,) + w_out_sh.shape, BF16)] + GATHER_SCRATCH,
        compiler_params=_params(("arbitrary",)),
    )(x, gain, w, qkg, cos4, sin4, bmean, w_out_sh)


def _fill_band_bias(bias_ref, max_dist):
    j = lax.broadcasted_iota(jnp.int32, (2 * BLOCK, 2 * BLOCK), 0)
    c = lax.broadcasted_iota(jnp.int32, (2 * BLOCK, 2 * BLOCK), 1)
    dist = (c & (BLOCK - 1)) + BLOCK - j
    band = (dist >= 0) & (dist <= max_dist)
    bias_ref[0] = jnp.where(band, 0.0, NEG)
    bias_ref[1] = jnp.where(band & (j >= BLOCK), 0.0, NEG)


def _band_bias(bias_ref, step, b, qb, nb):
    if nb < qb:
        return bias_ref[1 if b % nb == 0 else 0]
    if b > 0:
        return bias_ref[0]
    return bias_ref[jnp.where(((step * qb) & (nb - 1)) == 0, 1, 0)]


def _stack_heads(t):
    lane = lax.broadcasted_iota(jnp.int32, t.shape, 1)
    low = lane < HEAD_DIM
    zero = jnp.zeros_like(t)
    return jnp.concatenate([jnp.where(low, t, zero), jnp.where(low, zero, t)], axis=0)


def _stack_heads_t(t):
    tt = t.astype(F32).T
    low = lax.broadcasted_iota(jnp.int32, tt.shape, 0) < HEAD_DIM
    zero = jnp.zeros_like(tt)
    return jnp.concatenate([jnp.where(low, tt, zero), jnp.where(low, zero, tt)], axis=1).astype(BF16)


def _unstack_t(t):
    return jnp.concatenate([t[:HEAD_DIM, :BLOCK], t[HEAD_DIM:, BLOCK:]], axis=0).T


def _rows_to_stats(rows):
    parts = []
    for row in rows:
        parts.append(jnp.broadcast_to(row[:, :BLOCK], (STAT_REP, BLOCK)))
        parts.append(jnp.broadcast_to(row[:, BLOCK:], (STAT_REP, BLOCK)))
    return jnp.concatenate(parts, axis=0).T


def _stats_to_rows(t):
    tt = t.T
    return [jnp.concatenate([tt[2 * p * STAT_REP:2 * p * STAT_REP + 1, :],
                             tt[(2 * p + 1) * STAT_REP:(2 * p + 1) * STAT_REP + 1, :]], axis=1) for p in range(N_PAIRS)]


def _attn_fwd(name, patterns):
    qb = FWD_BLOCKS_PER_STEP
    steps = SEQ // (qb * BLOCK)

    def one_pattern(step, nb, shared, max_dist, q_ref, kc_ref, vc_ref, sink_ref, o_ref, lse_ref, kp_ref, vp_ref, bias_ref):
        has_sinks = sink_ref is not None

        @pl.when(step == 0)
        def _():
            kp_ref[...] = jnp.zeros_like(kp_ref)
            vp_ref[...] = jnp.zeros_like(vp_ref)
            _fill_band_bias(bias_ref, max_dist)

        cols = [slice(p * PAIR, (p + 1) * PAIR) for p in range(N_PAIRS)]
        kcols = [slice(0, PAIR) if shared else c for c in cols]
        rows = [slice(b * BLOCK, (b + 1) * BLOCK) for b in range(qb)]
        units = [(b, p) for b in range(qb) for p in range(N_PAIRS)]
        n = range(len(units))

        def window(prev_ref, cur_ref, b, kc):
            before = prev_ref[:, kc] if b == 0 else cur_ref[rows[b - 1], kc]
            return jnp.concatenate([before, cur_ref[rows[b], kc]], axis=0)

        st = [jnp.dot(window(kp_ref, kc_ref, b, kcols[p]), _stack_heads_t(q_ref[rows[b], cols[p]]),
                      preferred_element_type=F32) for b, p in units]
        st = [st[u] + _band_bias(bias_ref, step, units[u][0], qb, nb) for u in n]
        m = [jnp.max(s, axis=0, keepdims=True) for s in st]
        if has_sinks:
            sk = [sink_ref[p:p + 1, :] for _, p in units]
            m = [jnp.maximum(m[u], sk[u]) for u in n]
        pt = [jnp.exp2(st[u] - m[u]) for u in n]
        l = [jnp.sum(t, axis=0, keepdims=True) for t in pt]
        if has_sinks:
            l = [l[u] + jnp.exp2(sk[u] - m[u]) for u in n]
        v2t = [window(vp_ref, vc_ref, b, kcols[p]).astype(F32).T.astype(BF16) for b, p in units]
        ot = [jnp.dot(v2t[u], pt[u].astype(BF16), preferred_element_type=F32) / l[u] for u in n]
        for u, (b, p) in enumerate(units):
            o_ref[rows[b], cols[p]] = _unstack_t(ot[u]).astype(BF16)
        for b in range(qb):
            lse_ref[rows[b], :] = _rows_to_stats([m[u] + jnp.log2(l[u]) for u in n if units[u][0] == b])
        kp_ref[...] = kc_ref[rows[-1], :]
        vp_ref[...] = vc_ref[rows[-1], :]

    n_in = [4 if sinks is not None else 3 for _, _, _, sinks, _ in patterns]

    def body(*refs):
        ins, rest = refs[:sum(n_in)], refs[sum(n_in):]
        outs, scratch = rest[:2 * len(patterns)], rest[2 * len(patterns):]
        bias_ref = scratch[-1]
        step = pl.program_id(0)
        first = 0
        for p, (q, k, _, sinks, max_dist) in enumerate(patterns):
            mine = ins[first:first + n_in[p]]
            first += n_in[p]
            sink_ref = mine[3] if sinks is not None else None
            kp_ref, vp_ref = scratch[2 * p], scratch[2 * p + 1]

            @pl.when((step >= p * steps) & (step < (p + 1) * steps))
            def _():
                one_pattern(step - p * steps, q.shape[1] // BLOCK, k.shape[2] == PAIR, max_dist,
                            mine[0], mine[1], mine[2], sink_ref, outs[2 * p], outs[2 * p + 1], kp_ref, vp_ref, bias_ref)

    def during(p, width):
        return pl.BlockSpec((qb * BLOCK, width), lambda s: (jnp.clip(s - p * steps, 0, steps - 1), 0))

    flat = lambda a: a.reshape(SEQ, a.shape[2])
    in_specs, args, out_specs, out_shape, scratch = [], [], [], [], []
    for p, (q, k, v, sinks, _) in enumerate(patterns):
        ck = k.shape[2]
        in_specs += [during(p, HALF_WIDTH), during(p, ck), during(p, ck)]
        args += [flat(q), flat(k), flat(v)]
        if sinks is not None:
            in_specs.append(pl.BlockSpec(sinks.shape, lambda s: (0, 0)))
            args.append(sinks)
        out_specs += [during(p, HALF_WIDTH), during(p, STAT_WIDTH)]
        out_shape += [jax.ShapeDtypeStruct((SEQ, HALF_WIDTH), BF16), jax.ShapeDtypeStruct((SEQ, STAT_WIDTH), F32)]
        scratch += [pltpu.VMEM((BLOCK, ck), BF16), pltpu.VMEM((BLOCK, ck), BF16)]
    outs = pl.pallas_call(
        body, name=name, grid=(len(patterns) * steps,), in_specs=in_specs,
        out_specs=tuple(out_specs), out_shape=tuple(out_shape),
        scratch_shapes=scratch + [pltpu.VMEM((2, 2 * BLOCK, 2 * BLOCK), F32)],
        compiler_params=_params(("arbitrary",)),
    )(*args)
    return [(outs[2 * p].reshape(q.shape), outs[2 * p + 1].reshape(q.shape[0], q.shape[1], STAT_WIDTH))
            for p, (q, _, _, _, _) in enumerate(patterns)]


def _attn_bwd(name, q, k, v, d_o, lse, delta, sink_rows, max_dist):
    n_seq, length, _ = q.shape
    ck = k.shape[2]
    nb = length // BLOCK
    n_blocks = n_seq * nb
    n_rows = n_seq * length
    shared = ck == PAIR
    has_sinks = sink_rows is not None
    qb = BWD_BLOCKS_PER_STEP
    n_steps = n_blocks // qb

    def body(*refs):
        if has_sinks:
            (q_ref, kc_ref, vc_ref, do_ref, lse_ref, dl_ref, sink_ref,
             dq_ref, dk_ref, dv_ref, dsink_ref, ck_scr, cv_scr, kp_ref, vp_ref, bias_ref) = refs
        else:
            (q_ref, kc_ref, vc_ref, do_ref, lse_ref, dl_ref,
             dq_ref, dk_ref, dv_ref, ck_scr, cv_scr, kp_ref, vp_ref, bias_ref) = refs
        step = pl.program_id(0)

        @pl.when(step == 0)
        def _():
            ck_scr[...] = jnp.zeros_like(ck_scr)
            cv_scr[...] = jnp.zeros_like(cv_scr)
            kp_ref[...] = jnp.zeros_like(kp_ref)
            vp_ref[...] = jnp.zeros_like(vp_ref)
            if has_sinks:
                dsink_ref[...] = jnp.zeros_like(dsink_ref)
            _fill_band_bias(bias_ref, max_dist)

        cols = [slice(p * PAIR, (p + 1) * PAIR) for p in range(N_PAIRS)]
        kcols = [slice(0, PAIR) if shared else c for c in cols]
        rows = [slice(b * BLOCK, (b + 1) * BLOCK) for b in range(qb)]
        units = [(b, p) for b in range(qb) for p in range(N_PAIRS)]
        n = range(len(units))
        nt = (((1,), (1,)), ((), ()))

        def window(prev_ref, cur_ref, b, kc):
            before = prev_ref[:, kc] if b == 0 else cur_ref[rows[b - 1], kc]
            return jnp.concatenate([before, cur_ref[rows[b], kc]], axis=0)

        q_st = [_stack_heads(q_ref[rows[b], cols[p]]) for b, p in units]
        do_st = [_stack_heads(do_ref[rows[b], cols[p]]) for b, p in units]
        k2 = [window(kp_ref, kc_ref, b, kcols[p]) for b, p in units]
        v2 = [window(vp_ref, vc_ref, b, kcols[p]) for b, p in units]
        st = [lax.dot_general(k2[u], q_st[u], nt, preferred_element_type=F32) for u in n]
        dpt = [lax.dot_general(v2[u], do_st[u], nt, preferred_element_type=F32) for u in n]
        lse_rows = [_stats_to_rows(lse_ref[rows[b], :]) for b in range(qb)]
        dl_rows = [_stats_to_rows(dl_ref[rows[b], :]) for b in range(qb)]
        lse_row = [lse_rows[b][p] for b, p in units]
        dl_row = [dl_rows[b][p] for b, p in units]
        pt = [jnp.exp2(st[u] + _band_bias(bias_ref, step, units[u][0], qb, nb) - lse_row[u]) for u in n]
        dst = [(pt[u] * (dpt[u] - dl_row[u])).astype(BF16) for u in n]
        ptb = [t.astype(BF16) for t in pt]
        dv2 = [jnp.dot(ptb[u], do_st[u], preferred_element_type=F32) for u in n]
        dk2 = [jnp.dot(dst[u], q_st[u], preferred_element_type=F32) for u in n]
        k2t = [k2[u].astype(F32).T.astype(BF16) for u in n]
        dqt = [jnp.dot(k2t[u], dst[u], preferred_element_type=F32) for u in n]
        for u, (b, p) in enumerate(units):
            dq_ref[rows[b], cols[p]] = _unstack_t(dqt[u]).astype(BF16)
        if has_sinks:
            for u, (b, p) in enumerate(units):
                p_sink = jnp.exp2(sink_ref[p:p + 1, :] - lse_row[u])
                dsink_ref[p:p + 1, :] = dsink_ref[p:p + 1, :] - p_sink * dl_row[u]

        def total(parts, w, group):
            sel = [u for u, (b, p) in enumerate(units) if (shared or p == group)]
            terms = ([parts[u][:BLOCK] for u in sel if units[u][0] == w]
                     + [parts[u][BLOCK:] for u in sel if units[u][0] == w - 1])
            tot = terms[0]
            for t in terms[1:]:
                tot = tot + t
            return tot

        first_row = step * (qb * BLOCK)
        for acc_ref, out_ref, parts in ((ck_scr, dk_ref, dk2), (cv_scr, dv_ref, dv2)):
            for group in range(1 if shared else N_PAIRS):
                kc = kcols[group]

                @pl.when(step > 0)
                def _():
                    out_ref[pl.ds(pl.multiple_of(first_row - BLOCK, BLOCK), BLOCK), kc] = (
                        acc_ref[:, kc] + total(parts, 0, group)).astype(BF16)

                for w in range(1, qb):
                    out_ref[pl.ds(pl.multiple_of(first_row + (w - 1) * BLOCK, BLOCK), BLOCK), kc] = (
                        total(parts, w, group).astype(BF16))
                acc_ref[:, kc] = total(parts, qb, group)

        @pl.when(step == n_steps - 1)
        def _():
            dk_ref[pl.ds(n_rows - BLOCK, BLOCK), :] = ck_scr[...].astype(BF16)
            dv_ref[pl.ds(n_rows - BLOCK, BLOCK), :] = cv_scr[...].astype(BF16)

        kp_ref[...] = kc_ref[rows[-1], :]
        vp_ref[...] = vc_ref[rows[-1], :]

    cur = lambda width: pl.BlockSpec((qb * BLOCK, width), lambda s: (s, 0))
    whole = lambda width: pl.BlockSpec((n_rows, width), lambda s: (0, 0))
    flat = lambda a: a.reshape(n_rows, a.shape[2])
    in_specs = [cur(HALF_WIDTH), cur(ck), cur(ck), cur(HALF_WIDTH), cur(STAT_WIDTH), cur(STAT_WIDTH)]
    args = [flat(a) for a in (q, k, v, d_o, lse, delta)]
    out_specs = [cur(HALF_WIDTH), whole(ck), whole(ck)]
    out_shape = [jax.ShapeDtypeStruct((n_rows, HALF_WIDTH), BF16),
                 jax.ShapeDtypeStruct((n_rows, ck), BF16), jax.ShapeDtypeStruct((n_rows, ck), BF16)]
    if has_sinks:
        in_specs.append(pl.BlockSpec(sink_rows.shape, lambda s: (0, 0)))
        args.append(sink_rows)
        out_specs.append(pl.BlockSpec(sink_rows.shape, lambda s: (0, 0)))
        out_shape.append(jax.ShapeDtypeStruct(sink_rows.shape, F32))
    outs = pl.pallas_call(
        body, name=name, grid=(n_steps,), in_specs=in_specs,
        out_specs=tuple(out_specs), out_shape=tuple(out_shape),
        scratch_shapes=[pltpu.VMEM((BLOCK, ck), F32), pltpu.VMEM((BLOCK, ck), F32),
                        pltpu.VMEM((BLOCK, ck), BF16), pltpu.VMEM((BLOCK, ck), BF16),
                        pltpu.VMEM((2, 2 * BLOCK, 2 * BLOCK), F32)],
        compiler_params=_params(("arbitrary",)),
    )(*args)
    return tuple(o.reshape(n_seq, length, o.shape[1]) for o in outs[:3]) + tuple(outs[3:])


def _tail(oa, ob1, lb1, ob4, lb4, ob16, lb16, gate_a, gate_b, x, target, w_out, spread, gather):
    tm = ROW_TILE

    def split_dot(v, mat):
        hi = v.astype(BF16)
        lo = (v - hi.astype(F32)).astype(BF16)
        return jnp.dot(hi, mat, preferred_element_type=F32) + jnp.dot(lo, mat, preferred_element_type=F32)

    def body(oa_ref, ob1_ref, lb1_ref, ob4_ref, lb4_ref, ob16_ref, lb16_ref, ga_ref, gb_ref, x_ref, t_ref, w_ref,
             sp_ref, ga_mat_ref,
             loss_ref, dy_ref, gwo_ref, doa_ref, dla_ref, dga_ref, dgb_ref,
             dob_ref, dob4_ref, dob16_ref, dlb_ref, dlb4_ref, dlb16_ref, lse_ref, lse4_ref, lse16_ref,
             s_f, mix_keep, dy_keep):
        i = pl.program_id(0)
        sp, gat = sp_ref[...], ga_mat_ref[...]
        o4, o16 = _unfold_load(ob4_ref, s_f, 4, tm), _unfold_load(ob16_ref, s_f, 16, tm)
        l4, l16 = _unfold_load(lb4_ref, s_f, 4, tm), _unfold_load(lb16_ref, s_f, 16, tm)
        o1, l1 = ob1_ref[...].astype(F32), lb1_ref[...]
        mx = jnp.maximum(jnp.maximum(l1, l4), l16)
        e1, e4, e16 = jnp.exp2(l1 - mx), jnp.exp2(l4 - mx), jnp.exp2(l16 - mx)
        den = e1 + e4 + e16
        inv = 1.0 / den
        ob = split_dot(e1 * inv, sp) * o1 + split_dot(e4 * inv, sp) * o4 + split_dot(e16 * inv, sp) * o16
        lse_b = mx + jnp.log2(den)

        oa, ga, gb = oa_ref[...].astype(F32), ga_ref[...].astype(F32), gb_ref[...].astype(F32)
        sa, sb = _sigmoid(ga), _sigmoid(gb)
        mixed = jnp.concatenate(_unpair_tiles(_tiles(oa * (ga * sa))) + [ob * (gb * sb)], axis=1)
        mixed_bf = mixed.astype(BF16)
        w = w_ref[...]
        yv = x_ref[...] + jnp.dot(mixed_bf, w, preferred_element_type=F32)
        err = yv - t_ref[...]
        sq = jnp.sum(err * err, axis=0, keepdims=True)
        dy = err * (1.0 / D_MODEL)
        dy_ref[...] = dy
        dy_bf = dy.astype(BF16)
        mix_t = mixed.T.astype(BF16)

        @pl.when(i == 0)
        def _():
            loss_ref[...] = sq

        @pl.when(i > 0)
        def _():
            loss_ref[...] += sq

        @pl.when((i & 1) == 0)
        def _():
            mix_keep[...] = mix_t
            dy_keep[...] = dy_bf

        @pl.when((i & 1) == 1)
        def _():
            gw = jnp.dot(jnp.concatenate([mix_keep[...], mix_t], axis=1), jnp.concatenate([dy_keep[...], dy_bf], axis=0),
                         preferred_element_type=F32)

            @pl.when(i == 1)
            def _():
                gwo_ref[...] = gw

            @pl.when(i > 1)
            def _():
                gwo_ref[...] += gw

        dmix = lax.dot_general(dy_bf, w, (((1,), (1,)), ((), ())), preferred_element_type=F32)
        dma = jnp.concatenate(_pair_tiles(_tiles(dmix[:, :HALF_WIDTH])), axis=1)
        dmb = dmix[:, HALF_WIDTH:]

        doa = dma * (ga * sa)
        doa_ref[...] = doa.astype(BF16)
        dla_ref[...] = split_dot(doa * oa, gat)
        dga_ref[...] = (dma * oa * (sa * (1.0 + ga * (1.0 - sa)))).astype(BF16)
        dob = dmb * (gb * sb)
        dgb_ref[...] = (dmb * ob * (sb * (1.0 + gb * (1.0 - sb)))).astype(BF16)
        dlb = split_dot(dob * ob, gat)
        dob_ref[...] = dob.astype(BF16)
        _fold_store(dob, s_f, dob4_ref, dob16_ref, tm)
        dlb_ref[...] = dlb
        _fold_store(dlb, s_f, dlb4_ref, dlb16_ref, tm)
        lse_ref[...] = lse_b
        _fold_store(lse_b, s_f, lse4_ref, lse16_ref, tm)

    row = lambda width: pl.BlockSpec((tm, width), lambda i: (i, 0))
    full = lambda a: pl.BlockSpec(a.shape, lambda i: (0,) * a.ndim)
    fb_shapes, fb_specs = _fold_specs(tm, BF16)
    _, ff_specs = _fold_specs(tm, F32)
    st_shapes, st_specs = _fold_specs(tm, F32, STAT_WIDTH)
    nat = lambda dtype, width=HALF_WIDTH: jax.ShapeDtypeStruct((SEQ, width), dtype)
    return pl.pallas_call(
        body, name="tail", grid=(SEQ // tm,),
        in_specs=[row(HALF_WIDTH), row(HALF_WIDTH), row(STAT_WIDTH), ff_specs[0], st_specs[0], ff_specs[1], st_specs[1],
                  row(HALF_WIDTH), row(HALF_WIDTH), row(D_MODEL), row(D_MODEL), full(w_out), full(spread), full(gather)],
        out_specs=(pl.BlockSpec((1, D_MODEL), lambda i: (0, 0)), row(D_MODEL),
                   pl.BlockSpec((D_MODEL, D_MODEL), lambda i: (0, 0)),
                   row(HALF_WIDTH), row(STAT_WIDTH), row(HALF_WIDTH), row(HALF_WIDTH),
                   row(HALF_WIDTH), *fb_specs, row(STAT_WIDTH), *st_specs, row(STAT_WIDTH), *st_specs),
        out_shape=(jax.ShapeDtypeStruct((1, D_MODEL), F32), jax.ShapeDtypeStruct((SEQ, D_MODEL), F32),
                   jax.ShapeDtypeStruct((D_MODEL, D_MODEL), F32),
                   nat(BF16), nat(F32, STAT_WIDTH), nat(BF16), nat(BF16),
                   nat(BF16), *fb_shapes, nat(F32, STAT_WIDTH), *st_shapes, nat(F32, STAT_WIDTH), *st_shapes),
        scratch_shapes=[_fold_scratch(tm), pltpu.VMEM((D_MODEL, tm), BF16), pltpu.VMEM((tm, D_MODEL), BF16)],
        compiler_params=_params(("arbitrary",)),
    )(oa, ob1, lb1, ob4, lb4, ob16, lb16, gate_a, gate_b, x, target, w_out, spread, gather)


def _dproj_assemble(dqa, dka, dva, dga, dgb, dq1, dk1, dv1, dq4, dk4, dv4, dq16, dk16, dv16, tqa, tqb, tkb, tka,
                    qkg, cos4, sin4, bmean):
    tm = ROW_TILE

    def norm_rope_bwd(d_out, t, g, cos, sin, bm, scale):
        d_r = d_out * scale
        dyv = d_r * cos + _swap_halves(d_r * sin)
        rr = lax.rsqrt(_head_sum(t * t, bm) + EPS)
        that = t * rr
        dgain = jnp.sum(dyv * that, axis=0, keepdims=True)
        gdy = dyv * g
        dt = rr * (gdy - that * _head_sum(that * gdy, bm))
        return dt, dgain

    def body(dqa_ref, dka_ref, dva_ref, dga_ref, dgb_ref, dq1_ref, dk1_ref, dv1_ref, dq4_ref, dk4_ref, dv4_ref,
             dq16_ref, dk16_ref, dv16_ref, tqa_ref, tqb_ref, tkb_ref, tka_ref, qkg_ref, cos_ref, sin_ref, bm_ref,
             dproj_ref, dqkg_ref, s_f):
        i = pl.program_id(0)
        cos, sin, bm = cos_ref[...], sin_ref[...], bm_ref[...]

        def merged(nat_ref, f4_ref, f16_ref):
            return nat_ref[...].astype(F32) + _unfold_load(f4_ref, s_f, 4, tm) + _unfold_load(f16_ref, s_f, 16, tm)

        @pl.when(i == 0)
        def _():
            dqkg_ref[...] = jnp.zeros_like(dqkg_ref)

        def through(d_out, t, row, scale, c0, paired=False):
            g = qkg_ref[row:row + 1, :]
            tot = jnp.zeros((1, PAIR), F32)
            dts = []
            for j in range(d_out.shape[1] // PAIR):
                cols = slice(j * PAIR, (j + 1) * PAIR)
                dt, dg = norm_rope_bwd(d_out[:, cols], t[:, cols], g, cos, sin, bm, scale)
                dts.append(dt)
                tot = tot + dg
            if paired:
                dts = _unpair_tiles(dts)
            for j, dt in enumerate(dts):
                dproj_ref[:, c0 + j * PAIR:c0 + (j + 1) * PAIR] = dt.astype(BF16)
            dqkg_ref[row:row + 1, :] += tot

        through(dqa_ref[...].astype(F32), tqa_ref[...].astype(F32), 0, HEAD_DIM ** -0.5, C_QA, paired=True)
        through(dka_ref[...].astype(F32), tka_ref[...].astype(F32), 1, LN2, C_KA)
        through(merged(dq1_ref, dq4_ref, dq16_ref), tqb_ref[...].astype(F32), 2, HEAD_DIM ** -0.5, C_QB)
        through(merged(dk1_ref, dk4_ref, dk16_ref), tkb_ref[...].astype(F32), 3, LN2, C_KB)
        dproj_ref[:, C_VB:C_VB + HALF_WIDTH] = merged(dv1_ref, dv4_ref, dv16_ref).astype(BF16)
        dproj_ref[:, C_GA:C_GA + HALF_WIDTH] = jnp.concatenate(
            _unpair_tiles(_tiles(dga_ref[...].astype(F32))), axis=1).astype(BF16)
        dproj_ref[:, C_GB:C_GB + HALF_WIDTH] = dgb_ref[...].astype(BF16)
        dproj_ref[:, C_VA:C_VA + KV_A_WIDTH] = dva_ref[...].astype(BF16)

    row = lambda width: pl.BlockSpec((tm, width), lambda i: (i, 0))
    full = lambda a: pl.BlockSpec(a.shape, lambda i: (0,) * a.ndim)
    _, ff_specs = _fold_specs(tm, F32)
    return pl.pallas_call(
        body, name="dproj_assemble", grid=(SEQ // tm,),
        in_specs=[row(HALF_WIDTH), row(KV_A_WIDTH), row(KV_A_WIDTH), row(HALF_WIDTH), row(HALF_WIDTH),
                  row(HALF_WIDTH), row(HALF_WIDTH), row(HALF_WIDTH), ff_specs[0], ff_specs[0], ff_specs[0],
                  ff_specs[1], ff_specs[1], ff_specs[1],
                  row(HALF_WIDTH), row(HALF_WIDTH), row(HALF_WIDTH), row(KV_A_WIDTH),
                  full(qkg), row(PAIR), row(PAIR), full(bmean)],
        out_specs=(row(IN_WIDTH), pl.BlockSpec((SMALL_ROWS, PAIR), lambda i: (0, 0))),
        out_shape=(jax.ShapeDtypeStruct((SEQ, IN_WIDTH), BF16), jax.ShapeDtypeStruct((SMALL_ROWS, PAIR), F32)),
        scratch_shapes=[_fold_scratch(tm)],
        compiler_params=_params(("arbitrary",)),
    )(dqa, dka, dva, dga, dgb, dq1, dk1, dv1, dq4, dk4, dv4, dq16, dk16, dv16, tqa, tqb, tkb, tka, qkg, cos4, sin4, bmean)


def _input_grad_reduce(dproj, w, x, gain, dy, blocks_in, blocks_out, small):
    tm = ROW_TILE
    n_steps = SEQ // tm
    stage2_step, stage3_step = 3, 7
    shapes = (blocks_in.shape[1:], blocks_out.shape[1:])

    def body(dp_ref, w_ref, x_ref, g_ref, dy_ref, ga_hbm, gb_hbm, small_ref,
             gx_ref, out_a, out_b, small_out_ref, dgain_out_ref,
             part_a, part_b, sib_a, sib_b, wire_a, wire_b, chips_a, chips_b, small_all, dgain_acc, dgain_all,
             load_sems, sib_send, sib_recv, chip_send, chip_recv, small_send, small_recv, dgain_send, dgain_recv):
        i = pl.program_id(0)
        x, y, c = lax.axis_index("x"), lax.axis_index("y"), lax.axis_index("c")
        sibling = (x, y, 1 - c)
        chips = [(x, y), (1 - x, y), (x, 1 - y), (1 - x, 1 - y)]
        my_id = 4 * x + 2 * y + c
        g_hbm, part, from_sib = (ga_hbm, gb_hbm), (part_a, part_b), (sib_a, sib_b)
        to_wire, from_chips, out = (wire_a, wire_b), (chips_a, chips_b), (out_a, out_b)
        both = (0, 1)

        def blk(a, chip, core):
            return g_hbm[a].at[4 * chip[0] + 2 * chip[1] + core]

        def to_all(src, dst_all, send, recv):
            copies = []
            for rel in range(1, N_DEV):
                dx, dy_, dc = (rel >> 2) & 1, (rel >> 1) & 1, rel & 1
                to = (1 - x if dx else x, 1 - y if dy_ else y, 1 - c if dc else c)
                copies.append(pltpu.make_async_remote_copy(
                    src_ref=src, dst_ref=dst_all.at[my_id], send_sem=send.at[rel - 1], recv_sem=recv.at[rel - 1],
                    device_id=to, device_id_type=MESH))
            return copies

        small_copies = to_all(small_all.at[my_id], small_all, small_send, small_recv)
        dgain_copies = to_all(dgain_acc, dgain_all, dgain_send, dgain_recv)
        loads = [[pltpu.make_async_copy(blk(a, chips[k], c), part[a].at[k], load_sems.at[a, k]) for k in range(4)] for a in both]
        to_sib = [[pltpu.make_async_remote_copy(
            src_ref=blk(a, chips[k], 1 - c), dst_ref=from_sib[a].at[k], send_sem=sib_send.at[a, k], recv_sem=sib_recv.at[a, k],
            device_id=sibling, device_id_type=MESH) for k in range(4)] for a in both]
        first, other, k_first, k_other = _routes()
        to_chips = [[pltpu.make_async_remote_copy(
            src_ref=to_wire[a].at[s], dst_ref=from_chips[a].at[s], send_sem=chip_send.at[a, s], recv_sem=chip_recv.at[a, s],
            device_id=(first, first, other)[s], device_id_type=MESH) for s in range(3)] for a in both]

        def chip_partial(a, k):
            return part[a][k].astype(F32) + from_sib[a][k].astype(F32)

        @pl.when(i == 0)
        def _():
            small_all[my_id] = small_ref[...]
            for cp in small_copies:
                cp.start()
            for k in (1, 2, 3, 0):
                for a in both:
                    loads[a][k].start()
                    to_sib[a][k].start()

        @pl.when(i == stage2_step)
        def _():
            for k in (1, 2, 3):
                for a in both:
                    loads[a][k].wait()
                    to_sib[a][k].wait_recv()
            for s, k in ((0, 3), (1, k_first)):
                for a in both:
                    to_wire[a][s] = chip_partial(a, k).astype(BF16)
                    to_chips[a][s].start()

        @pl.when(i == stage3_step)
        def _():
            for a in both:
                to_chips[a][0].wait_recv()
                to_wire[a][2] = (chip_partial(a, k_other) + from_chips[a][0].astype(F32)).astype(BF16)
                to_chips[a][2].start()

        dh = lax.dot_general(dp_ref[...], w_ref[...], (((1,), (1,)), ((), ())), preferred_element_type=F32)
        xf = x_ref[...]
        r = lax.rsqrt(jnp.mean(xf * xf, axis=-1, keepdims=True) + EPS)
        xhat = xf * r
        dg = jnp.sum(dh * xhat, axis=0, keepdims=True)
        dxh = dh * g_ref[...]
        dx = r * (dxh - xhat * jnp.mean(dxh * xhat, axis=-1, keepdims=True))
        gx_ref[...] = dy_ref[...] + dx

        @pl.when(i == 0)
        def _():
            dgain_acc[...] = dg

        @pl.when(i > 0)
        def _():
            dgain_acc[...] += dg

        @pl.when(i == n_steps - 1)
        def _():
            dgain_all[my_id] = dgain_acc[...]
            for cp in dgain_copies:
                cp.start()
            for a in both:
                loads[a][0].wait()
                to_sib[a][0].wait_recv()
                acc = chip_partial(a, 0)
                for s in (1, 2):
                    to_chips[a][s].wait_recv()
                    acc = acc + from_chips[a][s].astype(F32)
                out[a][...] = acc
            for copies, gathered, dst in ((small_copies, small_all, small_out_ref), (dgain_copies, dgain_all, dgain_out_ref)):
                for cp in copies:
                    cp.wait_recv()
                tot = gathered[0]
                for d in range(1, N_DEV):
                    tot = tot + gathered[d]
                dst[...] = tot
            for cp in to_sib[0] + to_sib[1] + to_chips[0] + to_chips[1] + small_copies + dgain_copies:
                cp.wait_send()

    row = lambda width: pl.BlockSpec((tm, width), lambda i: (i, 0))
    full = lambda a: pl.BlockSpec(a.shape, lambda i: (0,) * a.ndim)
    whole = lambda shape: pl.BlockSpec(shape, lambda i: (0,) * len(shape))
    hbm = pl.BlockSpec(memory_space=pl.ANY)
    dtypes = (blocks_in.dtype, blocks_out.dtype)
    buf = lambda n, dts: [pltpu.VMEM((n,) + s, dt) for s, dt in zip(shapes, dts)]
    return pl.pallas_call(
        body, name="input_grad_rs", grid=(n_steps,),
        in_specs=[row(IN_WIDTH), full(w), row(D_MODEL), full(gain), row(D_MODEL), hbm, hbm, full(small)],
        out_specs=(row(D_MODEL), whole(shapes[0]), whole(shapes[1]), whole((SMALL_ROWS, SMALL_COLS)), whole((1, D_MODEL))),
        out_shape=(jax.ShapeDtypeStruct((SEQ, D_MODEL), F32), jax.ShapeDtypeStruct(shapes[0], F32),
                   jax.ShapeDtypeStruct(shapes[1], F32), jax.ShapeDtypeStruct((SMALL_ROWS, SMALL_COLS), F32),
                   jax.ShapeDtypeStruct((1, D_MODEL), F32)),
        scratch_shapes=[*buf(4, dtypes), *buf(4, dtypes), *buf(3, (BF16, BF16)), *buf(3, (BF16, BF16)),
                        pltpu.VMEM((N_DEV, SMALL_ROWS, SMALL_COLS), F32),
                        pltpu.VMEM((1, D_MODEL), F32), pltpu.VMEM((N_DEV, 1, D_MODEL), F32),
                        pltpu.SemaphoreType.DMA((2, 4)), pltpu.SemaphoreType.DMA((2, 4)), pltpu.SemaphoreType.DMA((2, 4)),
                        pltpu.SemaphoreType.DMA((2, 3)), pltpu.SemaphoreType.DMA((2, 3)),
                        pltpu.SemaphoreType.DMA((7,)), pltpu.SemaphoreType.DMA((7,)),
                        pltpu.SemaphoreType.DMA((7,)), pltpu.SemaphoreType.DMA((7,))],
        compiler_params=_params(("arbitrary",)),
    )(dproj, w, x, gain, dy, blocks_in, blocks_out, small)


def _weight_grad(h_t, dproj):
    tk = 1024
    cb = IN_WIDTH // 2
    n_k = SEQ // tk

    def body(ht_ref, dp_ref, out_ref, acc):
        k = pl.program_id(1)
        upd = jnp.dot(ht_ref[...], dp_ref[...], preferred_element_type=F32)

        @pl.when(k == 0)
        def _():
            acc[...] = upd

        @pl.when(k > 0)
        def _():
            acc[...] += upd

        @pl.when(k == n_k - 1)
        def _():
            for b in range(N_DEV // 2):
                out_ref[b] = acc[:, b * SHARD_IN:(b + 1) * SHARD_IN].astype(BF16)

    return pl.pallas_call(
        body, name="weight_grad", grid=(2, n_k),
        in_specs=[pl.BlockSpec((D_MODEL, tk), lambda j, k: (0, k)), pl.BlockSpec((tk, cb), lambda j, k: (k, j))],
        out_specs=pl.BlockSpec((N_DEV // 2, D_MODEL, SHARD_IN), lambda j, k: (j, 0, 0)),
        out_shape=jax.ShapeDtypeStruct((N_DEV, D_MODEL, SHARD_IN), BF16),
        scratch_shapes=[pltpu.VMEM((D_MODEL, cb), F32)],
        compiler_params=_params(("arbitrary", "arbitrary")),
    )(h_t, dproj)


def _adamw_update(w, g, m, v):
    nm = ADAM_B1 * m + (1.0 - ADAM_B1) * g
    nv = ADAM_B2 * v + (1.0 - ADAM_B2) * jnp.square(g)
    m_hat = nm / (1.0 - ADAM_B1 ** ADAM_STEP)
    v_hat = nv / (1.0 - ADAM_B2 ** ADAM_STEP)
    return -ADAM_LR * (m_hat / (jnp.sqrt(v_hat) + ADAM_EPS) + ADAM_WD * w), nm, nv


def _adamw(transposed, groups, small):
    params, ms, vs, packed_grads, gain_grad = small
    n_big, n_small = 4 * (len(groups) + 1), len(params)

    def body(*refs):
        n_in = n_big + 3 * n_small + 2
        ins, outs = refs[:n_in], refs[n_in:]
        p_refs, m_refs, v_refs = (ins[n_big + k * n_small:n_big + (k + 1) * n_small] for k in range(3))
        packed_ref, gain_grad_ref = ins[n_big + 3 * n_small:]
        small_outs = outs[4 + 3 * len(groups):]
        for k, window in enumerate(SMALL_WINDOWS):
            gv = gain_grad_ref[...] if window is None else packed_ref[window]
            results = (gv, *_adamw_update(p_refs[k][...], gv, m_refs[k][...], v_refs[k][...]))
            for out_ref, value in zip(small_outs[4 * k:4 * k + 4], results):
                out_ref[...] = value
        wt_ref, g_ref, mt_ref, vt_ref = ins[:4]
        gt_ref, dt_ref, nmt_ref, nvt_ref = outs[:4]
        cols = g_ref.shape[1]
        for off in [*range(0, cols - LANES, LANES), cols - LANES]:
            band = slice(off, off + LANES)
            gv = g_ref[:, band].T
            gt_ref[band, :] = gv
            dt_ref[band, :], nmt_ref[band, :], nvt_ref[band, :] = _adamw_update(
                wt_ref[band, :], gv, mt_ref[band, :], vt_ref[band, :])
        for i in range(len(groups)):
            w_ref, g_ref, m_ref, v_ref = ins[4 * i + 4:4 * i + 8]
            d_ref, nm_ref, nv_ref = outs[3 * i + 4:3 * i + 7]
            d_ref[...], nm_ref[...], nv_ref[...] = _adamw_update(w_ref[...], g_ref[...], m_ref[...], v_ref[...])

    vmem = pl.BlockSpec(memory_space=pltpu.VMEM)
    out_shape = [jax.ShapeDtypeStruct(transposed[0].shape, F32)] * 4
    out_shape += [jax.ShapeDtypeStruct(w.shape, F32) for w, _, _, _ in groups for _ in range(3)]
    out_shape += [jax.ShapeDtypeStruct(p.shape, F32) for p in params for _ in range(4)]
    outs = pl.pallas_call(
        body, name="adamw", in_specs=[vmem] * (n_big + 3 * n_small + 2), out_specs=(vmem,) * len(out_shape),
        out_shape=tuple(out_shape), compiler_params=pltpu.CompilerParams(vmem_limit_bytes=VMEM_LIMIT),
    )(*transposed, *[a for group in groups for a in group], *params, *ms, *vs, packed_grads, gain_grad)
    first_small = 4 + 3 * len(groups)
    return (tuple(outs[:4]), [tuple(outs[3 * i + 4:3 * i + 7]) for i in range(len(groups))],
            [tuple(outs[first_small + 4 * k:first_small + 4 * k + 4]) for k in range(n_small)])


SMALL_USED = D_MODEL + 4 * HEAD_DIM + 8


def _pack_small_grads(qa, ka, qb, kb, sinks, loss):
    flat = jnp.concatenate([jnp.zeros((D_MODEL,), F32), qa, ka, qb, kb, sinks, loss.reshape(1)])
    flat = jnp.pad(flat, (0, SMALL_ROWS * SMALL_COLS - flat.shape[0]))
    return flat.reshape(SMALL_ROWS, SMALL_COLS)


_GAINS_ROW, _SINKS_ROW = D_MODEL // SMALL_COLS, D_MODEL // SMALL_COLS + 1
_head_window = lambda k: (slice(_GAINS_ROW, _GAINS_ROW + 1), slice(k * HEAD_DIM, (k + 1) * HEAD_DIM))
SMALL_WINDOWS = (None, _head_window(0), _head_window(1), (slice(_SINKS_ROW, _SINKS_ROW + 1), slice(0, 8)),
                 _head_window(2), _head_window(3))


def _fold_heads(row):
    return row[0, :HEAD_DIM] + row[0, HEAD_DIM:]


def kernel(x, norm_gain, w_in, q_norm_a, k_norm_a, sinks_a, q_norm_b, k_norm_b, w_out, loss_target, m_norm_gain, m_w_in, m_q_norm_a, m_k_norm_a, m_sinks_a, m_q_norm_b, m_k_norm_b, m_w_out, v_norm_gain, v_w_in, v_q_norm_a, v_k_norm_a, v_sinks_a, v_q_norm_b, v_k_norm_b, v_w_out):
    x2, tgt = x[0], loss_target[0]
    w_in_sh, w_out_sh = w_in[0], w_out[0]

    w_full = _all_gather_w_in(w_in_sh)

    inv = np.float32(ROPE_THETA) ** (-np.arange(HEAD_DIM // 2, dtype=np.float32) / np.float32(HEAD_DIM // 2))
    ang = np.arange(SEQ, dtype=np.float32)[:, None] * inv[None, :].astype(np.float32)
    cos, sin = np.cos(ang).astype(np.float32), np.sin(ang).astype(np.float32)
    cos4 = jnp.asarray(np.concatenate([cos, cos, cos, cos], axis=1))
    sin4 = jnp.asarray(np.concatenate([-sin, sin, -sin, sin], axis=1))
    blockdiag = np.kron(np.eye(2, dtype=np.float32), np.ones((HEAD_DIM, HEAD_DIM), np.float32))
    bmean = jnp.asarray(blockdiag / HEAD_DIM, dtype=BF16)
    gather_np = np.kron(np.eye(2 * N_PAIRS, dtype=np.float32), np.ones((HEAD_DIM, STAT_REP), np.float32))
    spread_np = np.kron(np.eye(2 * N_PAIRS, dtype=np.float32), np.ones((STAT_REP, HEAD_DIM), np.float32))
    spread_np[np.arange(STAT_WIDTH) % STAT_REP != 0] = 0.0
    gather, spread = jnp.asarray(gather_np, dtype=BF16), jnp.asarray(spread_np, dtype=BF16)
    two = lambda g: jnp.concatenate([g, g], axis=1)
    qkg = jnp.concatenate([two(q_norm_a), two(k_norm_a), two(q_norm_b), two(k_norm_b),
                           jnp.zeros((SMALL_ROWS - 4, PAIR), F32)], axis=0)
    sinks_paired = jnp.stack([sinks_a[0, :N_PAIRS], sinks_a[0, N_PAIRS:]], axis=1) * LOG2E
    sink_rows = jnp.concatenate([jnp.repeat(sinks_paired, BLOCK, axis=1),
                                 jnp.zeros((SMALL_ROWS - N_PAIRS, 2 * BLOCK), F32)], axis=0)

    (tqa, tka, tqb, tkb, gate_a, gate_b, h_t, qa, ka, va, qb, kb, vb, qb4, qb16, kb4, kb16, vb4, vb16,
     gathered_out) = _proj_fwd(x2, norm_gain, w_full, qkg, cos4, sin4, bmean, w_out_sh)
    wo_full = gathered_out.reshape(D_MODEL, D_MODEL)
    (oa, la), (ob1, lb1), (ob4, lb4), (ob16, lb16) = _attn_fwd("attn_fwd", [
        (qa[None], ka[None], va[None], sink_rows, BLOCK - 1), (qb[None], kb[None], vb[None], None, BLOCK),
        (qb4, kb4, vb4, None, BLOCK), (qb16, kb16, vb16, None, BLOCK)])
    (loss_cols, dy, gwo, doa, dla, dga, dgb, dob, dob4, dob16, dlb, dlb4, dlb16, lse_b, lse4, lse16) = _tail(
        oa[0], ob1[0], lb1[0], ob4, lb4, ob16, lb16, gate_a, gate_b, x2, tgt, wo_full, spread, gather)

    dqa, dka, dva, dsink = _attn_bwd("attn_a_bwd", qa[None], ka[None], va[None], doa[None], la, dla[None], sink_rows, BLOCK - 1)
    dq1, dk1, dv1 = _attn_bwd("attn_b1_bwd", qb[None], kb[None], vb[None], dob[None], lse_b[None], dlb[None], None, BLOCK)
    dq4, dk4, dv4 = _attn_bwd("attn_b4_bwd", qb4, kb4, vb4, dob4, lse4, dlb4, None, BLOCK)
    dq16, dk16, dv16 = _attn_bwd("attn_b16_bwd", qb16, kb16, vb16, dob16, lse16, dlb16, None, BLOCK)
    dproj, dqkg = _dproj_assemble(dqa[0], dka[0], dva[0], dga, dgb, dq1[0], dk1[0], dv1[0], dq4, dk4, dv4,
                                  dq16, dk16, dv16, tqa, tqb, tkb, tka, qkg, cos4, sin4, bmean)
    gw_in = _weight_grad(h_t, dproj)

    blocks_in = gw_in
    blocks_out = gwo.reshape(N_DEV, SHARD_OUT, D_MODEL)
    g_sinks = jnp.concatenate([jnp.sum(dsink[:N_PAIRS, :BLOCK], axis=1), jnp.sum(dsink[:N_PAIRS, BLOCK:], axis=1)])
    small = _pack_small_grads(_fold_heads(dqkg[0:1]), _fold_heads(dqkg[1:2]), _fold_heads(dqkg[2:3]),
                              _fold_heads(dqkg[3:4]), g_sinks, 0.5 * jnp.sum(loss_cols) / D_MODEL)
    grad_x, g_w_in, g_w_out, small_red, dgain_red = _input_grad_reduce(
        dproj, w_full, x2, norm_gain, dy, blocks_in, blocks_out, small)

    as_held = lambda a: jnp.swapaxes(a, 1, 2)[0]
    w_in_results, [(d_out, nm_out, nv_out)], small_results = _adamw(
        (as_held(w_in), g_w_in, as_held(m_w_in), as_held(v_w_in)),
        [(w_out_sh, g_w_out, m_w_out[0], v_w_out[0])],
        ([norm_gain, q_norm_a, k_norm_a, sinks_a, q_norm_b, k_norm_b],
         [m_norm_gain, m_q_norm_a, m_k_norm_a, m_sinks_a, m_q_norm_b, m_k_norm_b],
         [v_norm_gain, v_q_norm_a, v_k_norm_a, v_sinks_a, v_q_norm_b, v_k_norm_b], small_red, dgain_red))
    g_small, d_small, nm_small, nv_small = ([leaf[j] for leaf in small_results] for j in range(4))
    g_in, d_in, nm_in, nv_in = [jnp.swapaxes(a, 0, 1) for a in w_in_results]

    loss = small_red.reshape(-1)[SMALL_USED]

    def assemble(small_list, big_in, big_out):
        ng, qa_, ka_, sk_, qb_, kb_ = small_list
        return [ng, big_in[None], qa_, ka_, sk_, qb_, kb_, big_out[None]]

    return (loss, grad_x[None], *assemble(g_small, g_in, g_w_out), *assemble(d_small, d_in, d_out),
            *assemble(nm_small, nm_in, nm_out), *assemble(nv_small, nv_in, nv_out))
```

```python
import numpy as np
import jax
import jax.numpy as jnp
from jax import lax
from jax.experimental import pallas as pl
from jax.experimental.pallas import tpu as pltpu

F32 = jnp.float32
BF16 = jnp.bfloat16

SEQ = 4096
D_MODEL = 1024
HEAD_DIM = 64
PAIR = 2 * HEAD_DIM
N_PAIRS = 4
HALF_WIDTH = N_PAIRS * PAIR
KV_A_WIDTH = 128
IN_WIDTH = 3328
LANES = 128
BLOCK = 128
STAT_REP = 16
STAT_WIDTH = 128
EPS = 1e-6
NEG = -1e30
ROPE_THETA = 10000.0
LOG2E = 1.4426950408889634
LN2 = 0.6931471805599453
Q_SCALE = HEAD_DIM ** -0.5 * LOG2E
N_DEV = 8
SHARD_IN = IN_WIDTH // N_DEV
SHARD_OUT = D_MODEL // N_DEV
SMALL_ROWS, SMALL_COLS = 8, 256

C_QA, C_KA, C_VA, C_GA, C_QB, C_KB, C_VB, C_GB = 0, 512, 640, 768, 1280, 1792, 2304, 2816

ADAM_LR = 0.001
ADAM_B1 = 0.9
ADAM_B2 = 0.999
ADAM_EPS = 1e-08
ADAM_WD = 0.01
ADAM_STEP = 10

ROW_TILE = 256
PROJ_ROW_TILE = 512
FWD_BLOCKS_PER_STEP = 8
BWD_BLOCKS_PER_STEP = 8
VMEM_LIMIT = 56 * 1024 * 1024

MESH = pl.DeviceIdType.MESH


def _params(sem, vmem=VMEM_LIMIT):
    return pltpu.CompilerParams(dimension_semantics=sem, vmem_limit_bytes=vmem)


def _head_sum(v, bm):
    return jnp.dot(v.astype(BF16), bm, preferred_element_type=F32)


def _swap_halves(y):
    lane = lax.broadcasted_iota(jnp.int32, y.shape, 1)
    first = (lane & 32) == 0
    return jnp.where(first, pltpu.roll(y, 96, 1), pltpu.roll(y, 32, 1))


def _sigmoid(g):
    return 1.0 / (1.0 + jnp.exp(-g))


def _tiles(a):
    return [a[:, j * PAIR:(j + 1) * PAIR] for j in range(N_PAIRS)]


def _pair_tiles(t):
    low = lax.broadcasted_iota(jnp.int32, t[0].shape, 1) < HEAD_DIM
    r = [pltpu.roll(a, HEAD_DIM, 1) for a in t]
    return [jnp.where(low, t[0], r[2]), jnp.where(low, r[0], t[2]), jnp.where(low, t[1], r[3]), jnp.where(low, r[1], t[3])]


def _unpair_tiles(p):
    low = lax.broadcasted_iota(jnp.int32, p[0].shape, 1) < HEAD_DIM
    r = [pltpu.roll(a, HEAD_DIM, 1) for a in p]
    return [jnp.where(low, p[0], r[1]), jnp.where(low, p[2], r[3]), jnp.where(low, r[0], p[1]), jnp.where(low, r[2], p[3])]


def _routes():
    x, y, c = lax.axis_index("x"), lax.axis_index("y"), lax.axis_index("c")
    north = c == 1
    first = (jnp.where(north, 1 - x, x), jnp.where(north, y, 1 - y), c)
    other = (jnp.where(north, x, 1 - x), jnp.where(north, 1 - y, y), c)
    k_first = jnp.where(north, 1, 2)
    return first, other, k_first, 3 - k_first


def _gather_plan(mine_ref, out_ref, send_sems, recv_sems):
    x, y, c = lax.axis_index("x"), lax.axis_index("y"), lax.axis_index("c")
    me, sibling, diag = (x, y, c), (x, y, 1 - c), (1 - x, 1 - y, c)
    first, other, k_first, k_other = _routes()

    def slot(px, py, pc):
        return out_ref.at[4 * px + 2 * py + pc]

    def copy(k, block, to, from_mine=False):
        return pltpu.make_async_remote_copy(
            src_ref=mine_ref if from_mine else slot(*block), dst_ref=slot(*block),
            send_sem=send_sems.at[k], recv_sem=recv_sems.at[k], device_id=to, device_id_type=MESH)

    sends = [copy(0, me, sibling, True), copy(1, me, (1 - x, y, c), True), copy(2, me, (x, 1 - y, c), True)]
    stages = [(copy(k_first, first, me), [copy(3, first, other), copy(3 + k_first, first, sibling)]),
              (copy(k_other, other, me), [copy(3 + k_other, other, sibling)]),
              (copy(3, diag, me), [copy(6, diag, sibling)])]
    from_sibling = [copy(0, sibling, me), copy(4, (1 - x, y, 1 - c), me), copy(5, (x, 1 - y, 1 - c), me),
                    copy(6, (1 - x, 1 - y, 1 - c), me)]
    return slot(*me), sends, stages, from_sibling


GATHER_SCRATCH = [pltpu.SemaphoreType.DMA((7,)), pltpu.SemaphoreType.DMA((7,))]


def _all_gather_w_in(w_in_sh_t):
    cols, rows = w_in_sh_t.shape

    def body(wt_ref, out_ref, mine_ref, blocks, send_sems, recv_sems):
        for off in [*range(0, cols - LANES, LANES), cols - LANES]:
            mine_ref[:, off:off + LANES] = wt_ref[off:off + LANES, :].T.astype(BF16)
        my_slot, sends, stages, from_sibling = _gather_plan(mine_ref, blocks, send_sems, recv_sems)
        for cp in sends:
            cp.start()
        my_slot[...] = mine_ref[...]
        for arrival, forwards in stages:
            arrival.wait_recv()
            for cp in forwards:
                cp.start()
        for arrival in from_sibling:
            arrival.wait_recv()
        for cp in sends + [cp for _, forwards in stages for cp in forwards]:
            cp.wait_send()
        for d in range(N_DEV):
            out_ref[:, d * cols:(d + 1) * cols] = blocks[d]

    vmem = pl.BlockSpec(memory_space=pltpu.VMEM)
    return pl.pallas_call(
        body, name="ag_w_in",
        out_shape=jax.ShapeDtypeStruct((rows, N_DEV * cols), BF16),
        in_specs=[vmem], out_specs=vmem,
        scratch_shapes=[pltpu.VMEM((rows, cols), BF16), pltpu.VMEM((N_DEV, rows, cols), BF16)] + GATHER_SCRATCH,
        compiler_params=pltpu.CompilerParams(vmem_limit_bytes=VMEM_LIMIT),
    )(w_in_sh_t)


def _fold_scratch(tm):
    return pltpu.VMEM((N_PAIRS, tm, PAIR), F32)


def _fold_store(val, scr, out4, out16, tm):
    groups = range(val.shape[1] // PAIR)
    for j in groups:
        scr[j] = val[:, j * PAIR:(j + 1) * PAIR]
    for dil, out in ((4, out4), (16, out16)):
        for r in range(dil):
            for j in groups:
                out[r, :, j * PAIR:(j + 1) * PAIR] = scr[j, pl.ds(r, tm // dil, stride=dil), :].astype(out.dtype)


def _unfold_load(src, scr, dil, tm):
    groups = range(src.shape[2] // PAIR)
    for r in range(dil):
        for j in groups:
            scr[j, pl.ds(r, tm // dil, stride=dil), :] = src[r, :, j * PAIR:(j + 1) * PAIR].astype(F32)
    return jnp.concatenate([scr[j] for j in groups], axis=1)


def _fold_specs(tm, dtype, width=HALF_WIDTH):
    shapes = (jax.ShapeDtypeStruct((4, SEQ // 4, width), dtype), jax.ShapeDtypeStruct((16, SEQ // 16, width), dtype))
    specs = (pl.BlockSpec((4, tm // 4, width), lambda i: (0, i, 0)),
             pl.BlockSpec((16, tm // 16, width), lambda i: (0, i, 0)))
    return shapes, specs


def _proj_fwd(x, gain, w, qkg, cos4, sin4, bmean, w_out_sh):
    tm = PROJ_ROW_TILE
    n_steps = SEQ // tm

    def norm_rope(t, g, cos, sin, bm, scale):
        rr = lax.rsqrt(_head_sum(t * t, bm) + EPS)
        yv = t * rr * g
        return (yv * cos + _swap_halves(yv) * sin) * scale

    def body(x_ref, g_ref, w_ref, qkg_ref, cos_ref, sin_ref, bm_ref, wo_ref,
             tqa_ref, tka_ref, tqb_ref, tkb_ref, ga_ref, gb_ref, ht_ref, qa_ref, ka_ref, va_ref, qb_ref, kb_ref, vb_ref,
             qb4_ref, qb16_ref, kb4_ref, kb16_ref, vb4_ref, vb16_ref, wo_all_ref,
             proj, scr, wo_mine, wo_all, send_sems, recv_sems):
        i = pl.program_id(0)
        my_slot, sends, stages, from_sibling = _gather_plan(wo_mine, wo_all, send_sems, recv_sems)

        @pl.when(i == 0)
        def _():
            wo_mine[...] = wo_ref[...].astype(BF16)
            for cp in sends:
                cp.start()
            my_slot[...] = wo_mine[...]

        @pl.when(i == n_steps // 2)
        def _():
            for arrival, forwards in stages[:2]:
                arrival.wait_recv()
                for cp in forwards:
                    cp.start()

        xf = x_ref[...]
        r = lax.rsqrt(jnp.mean(xf * xf, axis=-1, keepdims=True) + EPS)
        hf = xf * r * g_ref[...]
        ht_ref[...] = hf.T.astype(BF16)
        cos, sin, bm = cos_ref[...], sin_ref[...], bm_ref[...]
        proj[...] = jnp.dot(hf.astype(BF16), w_ref[...], preferred_element_type=F32)

        def roped(tiles, row, scale):
            g = qkg_ref[row:row + 1, :]
            return jnp.concatenate([norm_rope(t, g, cos, sin, bm, scale) for t in tiles], axis=1)

        tqa = _pair_tiles(_tiles(proj[:, C_QA:C_QA + HALF_WIDTH]))
        tqa_ref[...] = jnp.concatenate(tqa, axis=1).astype(BF16)
        qa_ref[...] = roped(tqa, 0, Q_SCALE).astype(BF16)
        ga_ref[...] = jnp.concatenate(_pair_tiles(_tiles(proj[:, C_GA:C_GA + HALF_WIDTH])), axis=1).astype(BF16)
        gb_ref[...] = proj[:, C_GB:C_GB + HALF_WIDTH].astype(BF16)
        tqb = proj[:, C_QB:C_QB + HALF_WIDTH]
        tqb_ref[...] = tqb.astype(BF16)
        qb = roped(_tiles(tqb), 2, Q_SCALE)
        qb_ref[...] = qb.astype(BF16)
        _fold_store(qb, scr, qb4_ref, qb16_ref, tm)
        tkb = proj[:, C_KB:C_KB + HALF_WIDTH]
        tkb_ref[...] = tkb.astype(BF16)
        kb = roped(_tiles(tkb), 3, 1.0)
        kb_ref[...] = kb.astype(BF16)
        _fold_store(kb, scr, kb4_ref, kb16_ref, tm)
        vb = proj[:, C_VB:C_VB + HALF_WIDTH]
        vb_ref[...] = vb.astype(BF16)
        _fold_store(vb, scr, vb4_ref, vb16_ref, tm)
        tka = proj[:, C_KA:C_KA + KV_A_WIDTH]
        tka_ref[...] = tka.astype(BF16)
        ka_ref[...] = roped([tka], 1, 1.0).astype(BF16)
        va_ref[...] = proj[:, C_VA:C_VA + KV_A_WIDTH].astype(BF16)

        @pl.when(i == n_steps - 1)
        def _():
            arrival, forwards = stages[2]
            arrival.wait_recv()
            for cp in forwards:
                cp.start()
            for arrival in from_sibling:
                arrival.wait_recv()
            for cp in sends + [cp for _, forwards in stages for cp in forwards]:
                cp.wait_send()
            wo_all_ref[...] = wo_all[...]

    row = lambda width: pl.BlockSpec((tm, width), lambda i: (i, 0))
    full = lambda a: pl.BlockSpec(a.shape, lambda i: (0,) * a.ndim)
    nat = lambda width, dtype=BF16: jax.ShapeDtypeStruct((SEQ, width), dtype)
    f_shapes, f_specs = _fold_specs(tm, BF16)
    return pl.pallas_call(
        body, name="proj_fwd", grid=(SEQ // tm,),
        in_specs=[row(D_MODEL), full(gain), full(w), full(qkg), row(PAIR), row(PAIR), full(bmean), full(w_out_sh)],
        out_specs=(row(HALF_WIDTH), row(KV_A_WIDTH), row(HALF_WIDTH), row(HALF_WIDTH), row(HALF_WIDTH), row(HALF_WIDTH),
                   pl.BlockSpec((D_MODEL, tm), lambda i: (0, i)),
                   row(HALF_WIDTH), row(KV_A_WIDTH), row(KV_A_WIDTH), row(HALF_WIDTH), row(HALF_WIDTH), row(HALF_WIDTH),
                   *f_specs, *f_specs, *f_specs,
                   pl.BlockSpec((N_DEV,) + w_out_sh.shape, lambda i: (0, 0, 0))),
        out_shape=(nat(HALF_WIDTH), nat(KV_A_WIDTH), nat(HALF_WIDTH), nat(HALF_WIDTH), nat(HALF_WIDTH), nat(HALF_WIDTH),
                   jax.ShapeDtypeStruct((D_MODEL, SEQ), BF16),
                   nat(HALF_WIDTH), nat(KV_A_WIDTH), nat(KV_A_WIDTH), nat(HALF_WIDTH), nat(HALF_WIDTH), nat(HALF_WIDTH),
                   *f_shapes, *f_shapes, *f_shapes,
                   jax.ShapeDtypeStruct((N_DEV,) + w_out_sh.shape, BF16)),
        scratch_shapes=[pltpu.VMEM((tm, IN_WIDTH), F32), _fold_scratch(tm), pltpu.VMEM(w_out_sh.shape, BF16),
                        pltpu.VMEM((N_DEV,) + w_out_sh.shape, BF16)] + GATHER_SCRATCH,
        compiler_params=_params(("arbitrary",)),
    )(x, gain, w, qkg, cos4, sin4, bmean, w_out_sh)


def _fill_band_bias(bias_ref, max_dist):
    j = lax.broadcasted_iota(jnp.int32, (2 * BLOCK, 2 * BLOCK), 0)
    c = lax.broadcasted_iota(jnp.int32, (2 * BLOCK, 2 * BLOCK), 1)
    dist = (c & (BLOCK - 1)) + BLOCK - j
    band = (dist >= 0) & (dist <= max_dist)
    bias_ref[0] = jnp.where(band, 0.0, NEG)
    bias_ref[1] = jnp.where(band & (j >= BLOCK), 0.0, NEG)


def _band_bias(bias_ref, step, b, qb, nb):
    if nb < qb:
        return bias_ref[1 if b % nb == 0 else 0]
    if b > 0:
        return bias_ref[0]
    return bias_ref[jnp.where(((step * qb) & (nb - 1)) == 0, 1, 0)]


def _stack_heads(t):
    lane = lax.broadcasted_iota(jnp.int32, t.shape, 1)
    low = lane < HEAD_DIM
    zero = jnp.zeros_like(t)
    return jnp.concatenate([jnp.where(low, t, zero), jnp.where(low, zero, t)], axis=0)


def _stack_heads_t(t):
    tt = t.astype(F32).T
    low = lax.broadcasted_iota(jnp.int32, tt.shape, 0) < HEAD_DIM
    zero = jnp.zeros_like(tt)
    return jnp.concatenate([jnp.where(low, tt, zero), jnp.where(low, zero, tt)], axis=1).astype(BF16)


def _unstack_t(t):
    return jnp.concatenate([t[:HEAD_DIM, :BLOCK], t[HEAD_DIM:, BLOCK:]], axis=0).T


def _rows_to_stats(rows):
    parts = []
    for row in rows:
        parts.append(jnp.broadcast_to(row[:, :BLOCK], (STAT_REP, BLOCK)))
        parts.append(jnp.broadcast_to(row[:, BLOCK:], (STAT_REP, BLOCK)))
    return jnp.concatenate(parts, axis=0).T


def _stats_to_rows(t):
    tt = t.T
    return [jnp.concatenate([tt[2 * p * STAT_REP:2 * p * STAT_REP + 1, :],
                             tt[(2 * p + 1) * STAT_REP:(2 * p + 1) * STAT_REP + 1, :]], axis=1) for p in range(N_PAIRS)]


def _attn_fwd(name, patterns):
    qb = FWD_BLOCKS_PER_STEP
    steps = SEQ // (qb * BLOCK)

    def one_pattern(step, nb, shared, max_dist, q_ref, kc_ref, vc_ref, sink_ref, o_ref, lse_ref, kp_ref, vp_ref, bias_ref):
        has_sinks = sink_ref is not None

        @pl.when(step == 0)
        def _():
            kp_ref[...] = jnp.zeros_like(kp_ref)
            vp_ref[...] = jnp.zeros_like(vp_ref)
            _fill_band_bias(bias_ref, max_dist)

        cols = [slice(p * PAIR, (p + 1) * PAIR) for p in range(N_PAIRS)]
        kcols = [slice(0, PAIR) if shared else c for c in cols]
        rows = [slice(b * BLOCK, (b + 1) * BLOCK) for b in range(qb)]
        units = [(b, p) for b in range(qb) for p in range(N_PAIRS)]
        n = range(len(units))

        def window(prev_ref, cur_ref, b, kc):
            before = prev_ref[:, kc] if b == 0 else cur_ref[rows[b - 1], kc]
            return jnp.concatenate([before, cur_ref[rows[b], kc]], axis=0)

        st = [jnp.dot(window(kp_ref, kc_ref, b, kcols[p]), _stack_heads_t(q_ref[rows[b], cols[p]]),
                      preferred_element_type=F32) for b, p in units]
        st = [st[u] + _band_bias(bias_ref, step, units[u][0], qb, nb) for u in n]
        m = [jnp.max(s, axis=0, keepdims=True) for s in st]
        if has_sinks:
            sk = [sink_ref[p:p + 1, :] for _, p in units]
            m = [jnp.maximum(m[u], sk[u]) for u in n]
        pt = [jnp.exp2(st[u] - m[u]) for u in n]
        l = [jnp.sum(t, axis=0, keepdims=True) for t in pt]
        if has_sinks:
            l = [l[u] + jnp.exp2(sk[u] - m[u]) for u in n]
        v2t = [window(vp_ref, vc_ref, b, kcols[p]).astype(F32).T.astype(BF16) for b, p in units]
        ot = [jnp.dot(v2t[u], pt[u].astype(BF16), preferred_element_type=F32) / l[u] for u in n]
        for u, (b, p) in enumerate(units):
            o_ref[rows[b], cols[p]] = _unstack_t(ot[u]).astype(BF16)
        for b in range(qb):
            lse_ref[rows[b], :] = _rows_to_stats([m[u] + jnp.log2(l[u]) for u in n if units[u][0] == b])
        kp_ref[...] = kc_ref[rows[-1], :]
        vp_ref[...] = vc_ref[rows[-1], :]

    n_in = [4 if sinks is not None else 3 for _, _, _, sinks, _ in patterns]

    def body(*refs):
        ins, rest = refs[:sum(n_in)], refs[sum(n_in):]
        outs, scratch = rest[:2 * len(patterns)], rest[2 * len(patterns):]
        bias_ref = scratch[-1]
        step = pl.program_id(0)
        first = 0
        for p, (q, k, _, sinks, max_dist) in enumerate(patterns):
            mine = ins[first:first + n_in[p]]
            first += n_in[p]
            sink_ref = mine[3] if sinks is not None else None
            kp_ref, vp_ref = scratch[2 * p], scratch[2 * p + 1]

            @pl.when((step >= p * steps) & (step < (p + 1) * steps))
            def _():
                one_pattern(step - p * steps, q.shape[1] // BLOCK, k.shape[2] == PAIR, max_dist,
                            mine[0], mine[1], mine[2], sink_ref, outs[2 * p], outs[2 * p + 1], kp_ref, vp_ref, bias_ref)

    def during(p, width):
        return pl.BlockSpec((qb * BLOCK, width), lambda s: (jnp.clip(s - p * steps, 0, steps - 1), 0))

    flat = lambda a: a.reshape(SEQ, a.shape[2])
    in_specs, args, out_specs, out_shape, scratch = [], [], [], [], []
    for p, (q, k, v, sinks, _) in enumerate(patterns):
        ck = k.shape[2]
        in_specs += [during(p, HALF_WIDTH), during(p, ck), during(p, ck)]
        args += [flat(q), flat(k), flat(v)]
        if sinks is not None:
            in_specs.append(pl.BlockSpec(sinks.shape, lambda s: (0, 0)))
            args.append(sinks)
        out_specs += [during(p, HALF_WIDTH), during(p, STAT_WIDTH)]
        out_shape += [jax.ShapeDtypeStruct((SEQ, HALF_WIDTH), BF16), jax.ShapeDtypeStruct((SEQ, STAT_WIDTH), F32)]
        scratch += [pltpu.VMEM((BLOCK, ck), BF16), pltpu.VMEM((BLOCK, ck), BF16)]
    outs = pl.pallas_call(
        body, name=name, grid=(len(patterns) * steps,), in_specs=in_specs,
        out_specs=tuple(out_specs), out_shape=tuple(out_shape),
        scratch_shapes=scratch + [pltpu.VMEM((2, 2 * BLOCK, 2 * BLOCK), F32)],
        compiler_params=_params(("arbitrary",)),
    )(*args)
    return [(outs[2 * p].reshape(q.shape), outs[2 * p + 1].reshape(q.shape[0], q.shape[1], STAT_WIDTH))
            for p, (q, _, _, _, _) in enumerate(patterns)]


def _attn_bwd(name, q, k, v, d_o, lse, delta, sink_rows, max_dist):
    n_seq, length, _ = q.shape
    ck = k.shape[2]
    nb = length // BLOCK
    n_blocks = n_seq * nb
    n_rows = n_seq * length
    shared = ck == PAIR
    has_sinks = sink_rows is not None
    qb = BWD_BLOCKS_PER_STEP
    n_steps = n_blocks // qb

    def body(*refs):
        if has_sinks:
            (q_ref, kc_ref, vc_ref, do_ref, lse_ref, dl_ref, sink_ref,
             dq_ref, dk_ref, dv_ref, dsink_ref, ck_scr, cv_scr, kp_ref, vp_ref, bias_ref) = refs
        else:
            (q_ref, kc_ref, vc_ref, do_ref, lse_ref, dl_ref,
             dq_ref, dk_ref, dv_ref, ck_scr, cv_scr, kp_ref, vp_ref, bias_ref) = refs
        step = pl.program_id(0)

        @pl.when(step == 0)
        def _():
            ck_scr[...] = jnp.zeros_like(ck_scr)
            cv_scr[...] = jnp.zeros_like(cv_scr)
            kp_ref[...] = jnp.zeros_like(kp_ref)
            vp_ref[...] = jnp.zeros_like(vp_ref)
            if has_sinks:
                dsink_ref[...] = jnp.zeros_like(dsink_ref)
            _fill_band_bias(bias_ref, max_dist)

        cols = [slice(p * PAIR, (p + 1) * PAIR) for p in range(N_PAIRS)]
        kcols = [slice(0, PAIR) if shared else c for c in cols]
        rows = [slice(b * BLOCK, (b + 1) * BLOCK) for b in range(qb)]
        units = [(b, p) for b in range(qb) for p in range(N_PAIRS)]
        n = range(len(units))
        nt = (((1,), (1,)), ((), ()))

        def window(prev_ref, cur_ref, b, kc):
            before = prev_ref[:, kc] if b == 0 else cur_ref[rows[b - 1], kc]
            return jnp.concatenate([before, cur_ref[rows[b], kc]], axis=0)

        q_st = [_stack_heads(q_ref[rows[b], cols[p]]) for b, p in units]
        do_st = [_stack_heads(do_ref[rows[b], cols[p]]) for b, p in units]
        k2 = [window(kp_ref, kc_ref, b, kcols[p]) for b, p in units]
        v2 = [window(vp_ref, vc_ref, b, kcols[p]) for b, p in units]
        st = [lax.dot_general(k2[u], q_st[u], nt, preferred_element_type=F32) for u in n]
        dpt = [lax.dot_general(v2[u], do_st[u], nt, preferred_element_type=F32) for u in n]
        lse_rows = [_stats_to_rows(lse_ref[rows[b], :]) for b in range(qb)]
        dl_rows = [_stats_to_rows(dl_ref[rows[b], :]) for b in range(qb)]
        lse_row = [lse_rows[b][p] for b, p in units]
        dl_row = [dl_rows[b][p] for b, p in units]
        pt = [jnp.exp2(st[u] + _band_bias(bias_ref, step, units[u][0], qb, nb) - lse_row[u]) for u in n]
        dst = [(pt[u] * (dpt[u] - dl_row[u])).astype(BF16) for u in n]
        ptb = [t.astype(BF16) for t in pt]
        dv2 = [jnp.dot(ptb[u], do_st[u], preferred_element_type=F32) for u in n]
        dk2 = [jnp.dot(dst[u], q_st[u], preferred_element_type=F32) for u in n]
        k2t = [k2[u].astype(F32).T.astype(BF16) for u in n]
        dqt = [jnp.dot(k2t[u], dst[u], preferred_element_type=F32) for u in n]
        for u, (b, p) in enumerate(units):
            dq_ref[rows[b], cols[p]] = _unstack_t(dqt[u]).astype(BF16)
        if has_sinks:
            for u, (b, p) in enumerate(units):
                p_sink = jnp.exp2(sink_ref[p:p + 1, :] - lse_row[u])
                dsink_ref[p:p + 1, :] = dsink_ref[p:p + 1, :] - p_sink * dl_row[u]

        def total(parts, w, group):
            sel = [u for u, (b, p) in enumerate(units) if (shared or p == group)]
            terms = ([parts[u][:BLOCK] for u in sel if units[u][0] == w]
                     + [parts[u][BLOCK:] for u in sel if units[u][0] == w - 1])
            tot = terms[0]
            for t in terms[1:]:
                tot = tot + t
            return tot

        first_row = step * (qb * BLOCK)
        for acc_ref, out_ref, parts in ((ck_scr, dk_ref, dk2), (cv_scr, dv_ref, dv2)):
            for group in range(1 if shared else N_PAIRS):
                kc = kcols[group]

                @pl.when(step > 0)
                def _():
                    out_ref[pl.ds(pl.multiple_of(first_row - BLOCK, BLOCK), BLOCK), kc] = (
                        acc_ref[:, kc] + total(parts, 0, group)).astype(BF16)

                for w in range(1, qb):
                    out_ref[pl.ds(pl.multiple_of(first_row + (w - 1) * BLOCK, BLOCK), BLOCK), kc] = (
                        total(parts, w, group).astype(BF16))
                acc_ref[:, kc] = total(parts, qb, group)

        @pl.when(step == n_steps - 1)
        def _():
            dk_ref[pl.ds(n_rows - BLOCK, BLOCK), :] = ck_scr[...].astype(BF16)
            dv_ref[pl.ds(n_rows - BLOCK, BLOCK), :] = cv_scr[...].astype(BF16)

        kp_ref[...] = kc_ref[rows[-1], :]
        vp_ref[...] = vc_ref[rows[-1], :]

    cur = lambda width: pl.BlockSpec((qb * BLOCK, width), lambda s: (s, 0))
    whole = lambda width: pl.BlockSpec((n_rows, width), lambda s: (0, 0))
    flat = lambda a: a.reshape(n_rows, a.shape[2])
    in_specs = [cur(HALF_WIDTH), cur(ck), cur(ck), cur(HALF_WIDTH), cur(STAT_WIDTH), cur(STAT_WIDTH)]
    args = [flat(a) for a in (q, k, v, d_o, lse, delta)]
    out_specs = [cur(HALF_WIDTH), whole(ck), whole(ck)]
    out_shape = [jax.ShapeDtypeStruct((n_rows, HALF_WIDTH), BF16),
                 jax.ShapeDtypeStruct((n_rows, ck), BF16), jax.ShapeDtypeStruct((n_rows, ck), BF16)]
    if has_sinks:
        in_specs.append(pl.BlockSpec(sink_rows.shape, lambda s: (0, 0)))
        args.append(sink_rows)
        out_specs.append(pl.BlockSpec(sink_rows.shape, lambda s: (0, 0)))
        out_shape.append(jax.ShapeDtypeStruct(sink_rows.shape, F32))
    outs = pl.pallas_call(
        body, name=name, grid=(n_steps,), in_specs=in_specs,
        out_specs=tuple(out_specs), out_shape=tuple(out_shape),
        scratch_shapes=[pltpu.VMEM((BLOCK, ck), F32), pltpu.VMEM((BLOCK, ck), F32),
                        pltpu.VMEM((BLOCK, ck), BF16), pltpu.VMEM((BLOCK, ck), BF16),
                        pltpu.VMEM((2, 2 * BLOCK, 2 * BLOCK), F32)],
        compiler_params=_params(("arbitrary",)),
    )(*args)
    return tuple(o.reshape(n_seq, length, o.shape[1]) for o in outs[:3]) + tuple(outs[3:])


def _tail(oa, ob1, lb1, ob4, lb4, ob16, lb16, gate_a, gate_b, x, target, w_out, spread, gather):
    tm = ROW_TILE

    def split_dot(v, mat):
        hi = v.astype(BF16)
        lo = (v - hi.astype(F32)).astype(BF16)
        return jnp.dot(hi, mat, preferred_element_type=F32) + jnp.dot(lo, mat, preferred_element_type=F32)

    def body(oa_ref, ob1_ref, lb1_ref, ob4_ref, lb4_ref, ob16_ref, lb16_ref, ga_ref, gb_ref, x_ref, t_ref, w_ref,
             sp_ref, ga_mat_ref,
             loss_ref, dy_ref, gwo_ref, doa_ref, dla_ref, dga_ref, dgb_ref,
             dob_ref, dob4_ref, dob16_ref, dlb_ref, dlb4_ref, dlb16_ref, lse_ref, lse4_ref, lse16_ref,
             s_f, mix_keep, dy_keep):
        i = pl.program_id(0)
        sp, gat = sp_ref[...], ga_mat_ref[...]
        o4, o16 = _unfold_load(ob4_ref, s_f, 4, tm), _unfold_load(ob16_ref, s_f, 16, tm)
        l4, l16 = _unfold_load(lb4_ref, s_f, 4, tm), _unfold_load(lb16_ref, s_f, 16, tm)
        o1, l1 = ob1_ref[...].astype(F32), lb1_ref[...]
        mx = jnp.maximum(jnp.maximum(l1, l4), l16)
        e1, e4, e16 = jnp.exp2(l1 - mx), jnp.exp2(l4 - mx), jnp.exp2(l16 - mx)
        den = e1 + e4 + e16
        inv = 1.0 / den
        ob = split_dot(e1 * inv, sp) * o1 + split_dot(e4 * inv, sp) * o4 + split_dot(e16 * inv, sp) * o16
        lse_b = mx + jnp.log2(den)

        oa, ga, gb = oa_ref[...].astype(F32), ga_ref[...].astype(F32), gb_ref[...].astype(F32)
        sa, sb = _sigmoid(ga), _sigmoid(gb)
        mixed = jnp.concatenate(_unpair_tiles(_tiles(oa * (ga * sa))) + [ob * (gb * sb)], axis=1)
        mixed_bf = mixed.astype(BF16)
        w = w_ref[...]
        yv = x_ref[...] + jnp.dot(mixed_bf, w, preferred_element_type=F32)
        err = yv - t_ref[...]
        sq = jnp.sum(err * err, axis=0, keepdims=True)
        dy = err * (1.0 / D_MODEL)
        dy_ref[...] = dy
        dy_bf = dy.astype(BF16)
        mix_t = mixed.T.astype(BF16)

        @pl.when(i == 0)
        def _():
            loss_ref[...] = sq

        @pl.when(i > 0)
        def _():
            loss_ref[...] += sq

        @pl.when((i & 1) == 0)
        def _():
            mix_keep[...] = mix_t
            dy_keep[...] = dy_bf

        @pl.when((i & 1) == 1)
        def _():
            gw = jnp.dot(jnp.concatenate([mix_keep[...], mix_t], axis=1), jnp.concatenate([dy_keep[...], dy_bf], axis=0),
                         preferred_element_type=F32)

            @pl.when(i == 1)
            def _():
                gwo_ref[...] = gw

            @pl.when(i > 1)
            def _():
                gwo_ref[...] += gw

        dmix = lax.dot_general(dy_bf, w, (((1,), (1,)), ((), ())), preferred_element_type=F32)
        dma = jnp.concatenate(_pair_tiles(_tiles(dmix[:, :HALF_WIDTH])), axis=1)
        dmb = dmix[:, HALF_WIDTH:]

        doa = dma * (ga * sa)
        doa_ref[...] = doa.astype(BF16)
        dla_ref[...] = split_dot(doa * oa, gat)
        dga_ref[...] = (dma * oa * (sa * (1.0 + ga * (1.0 - sa)))).astype(BF16)
        dob = dmb * (gb * sb)
        dgb_ref[...] = (dmb * ob * (sb * (1.0 + gb * (1.0 - sb)))).astype(BF16)
        dlb = split_dot(dob * ob, gat)
        dob_ref[...] = dob.astype(BF16)
        _fold_store(dob, s_f, dob4_ref, dob16_ref, tm)
        dlb_ref[...] = dlb
        _fold_store(dlb, s_f, dlb4_ref, dlb16_ref, tm)
        lse_ref[...] = lse_b
        _fold_store(lse_b, s_f, lse4_ref, lse16_ref, tm)

    row = lambda width: pl.BlockSpec((tm, width), lambda i: (i, 0))
    full = lambda a: pl.BlockSpec(a.shape, lambda i: (0,) * a.ndim)
    fb_shapes, fb_specs = _fold_specs(tm, BF16)
    _, ff_specs = _fold_specs(tm, F32)
    st_shapes, st_specs = _fold_specs(tm, F32, STAT_WIDTH)
    nat = lambda dtype, width=HALF_WIDTH: jax.ShapeDtypeStruct((SEQ, width), dtype)
    return pl.pallas_call(
        body, name="tail", grid=(SEQ // tm,),
        in_specs=[row(HALF_WIDTH), row(HALF_WIDTH), row(STAT_WIDTH), ff_specs[0], st_specs[0], ff_specs[1], st_specs[1],
                  row(HALF_WIDTH), row(HALF_WIDTH), row(D_MODEL), row(D_MODEL), full(w_out), full(spread), full(gather)],
        out_specs=(pl.BlockSpec((1, D_MODEL), lambda i: (0, 0)), row(D_MODEL),
                   pl.BlockSpec((D_MODEL, D_MODEL), lambda i: (0, 0)),
                   row(HALF_WIDTH), row(STAT_WIDTH), row(HALF_WIDTH), row(HALF_WIDTH),
                   row(HALF_WIDTH), *fb_specs, row(STAT_WIDTH), *st_specs, row(STAT_WIDTH), *st_specs),
        out_shape=(jax.ShapeDtypeStruct((1, D_MODEL), F32), jax.ShapeDtypeStruct((SEQ, D_MODEL), F32),
                   jax.ShapeDtypeStruct((D_MODEL, D_MODEL), F32),
                   nat(BF16), nat(F32, STAT_WIDTH), nat(BF16), nat(BF16),
                   nat(BF16), *fb_shapes, nat(F32, STAT_WIDTH), *st_shapes, nat(F32, STAT_WIDTH), *st_shapes),
        scratch_shapes=[_fold_scratch(tm), pltpu.VMEM((D_MODEL, tm), BF16), pltpu.VMEM((tm, D_MODEL), BF16)],
        compiler_params=_params(("arbitrary",)),
    )(oa, ob1, lb1, ob4, lb4, ob16, lb16, gate_a, gate_b, x, target, w_out, spread, gather)


def _dproj_assemble(dqa, dka, dva, dga, dgb, dq1, dk1, dv1, dq4, dk4, dv4, dq16, dk16, dv16, tqa, tqb, tkb, tka,
                    qkg, cos4, sin4, bmean):
    tm = ROW_TILE

    def norm_rope_bwd(d_out, t, g, cos, sin, bm, scale):
        d_r = d_out * scale
        dyv = d_r * cos + _swap_halves(d_r * sin)
        rr = lax.rsqrt(_head_sum(t * t, bm) + EPS)
        that = t * rr
        dgain = jnp.sum(dyv * that, axis=0, keepdims=True)
        gdy = dyv * g
        dt = rr * (gdy - that * _head_sum(that * gdy, bm))
        return dt, dgain

    def body(dqa_ref, dka_ref, dva_ref, dga_ref, dgb_ref, dq1_ref, dk1_ref, dv1_ref, dq4_ref, dk4_ref, dv4_ref,
             dq16_ref, dk16_ref, dv16_ref, tqa_ref, tqb_ref, tkb_ref, tka_ref, qkg_ref, cos_ref, sin_ref, bm_ref,
             dproj_ref, dqkg_ref, s_f):
        i = pl.program_id(0)
        cos, sin, bm = cos_ref[...], sin_ref[...], bm_ref[...]

        def merged(nat_ref, f4_ref, f16_ref):
            return nat_ref[...].astype(F32) + _unfold_load(f4_ref, s_f, 4, tm) + _unfold_load(f16_ref, s_f, 16, tm)

        @pl.when(i == 0)
        def _():
            dqkg_ref[...] = jnp.zeros_like(dqkg_ref)

        def through(d_out, t, row, scale, c0, paired=False):
            g = qkg_ref[row:row + 1, :]
            tot = jnp.zeros((1, PAIR), F32)
            dts = []
            for j in range(d_out.shape[1] // PAIR):
                cols = slice(j * PAIR, (j + 1) * PAIR)
                dt, dg = norm_rope_bwd(d_out[:, cols], t[:, cols], g, cos, sin, bm, scale)
                dts.append(dt)
                tot = tot + dg
            if paired:
                dts = _unpair_tiles(dts)
            for j, dt in enumerate(dts):
                dproj_ref[:, c0 + j * PAIR:c0 + (j + 1) * PAIR] = dt.astype(BF16)
            dqkg_ref[row:row + 1, :] += tot

        through(dqa_ref[...].astype(F32), tqa_ref[...].astype(F32), 0, HEAD_DIM ** -0.5, C_QA, paired=True)
        through(dka_ref[...].astype(F32), tka_ref[...].astype(F32), 1, LN2, C_KA)
        through(merged(dq1_ref, dq4_ref, dq16_ref), tqb_ref[...].astype(F32), 2, HEAD_DIM ** -0.5, C_QB)
        through(merged(dk1_ref, dk4_ref, dk16_ref), tkb_ref[...].astype(F32), 3, LN2, C_KB)
        dproj_ref[:, C_VB:C_VB + HALF_WIDTH] = merged(dv1_ref, dv4_ref, dv16_ref).astype(BF16)
        dproj_ref[:, C_GA:C_GA + HALF_WIDTH] = jnp.concatenate(
            _unpair_tiles(_tiles(dga_ref[...].astype(F32))), axis=1).astype(BF16)
        dproj_ref[:, C_GB:C_GB + HALF_WIDTH] = dgb_ref[...].astype(BF16)
        dproj_ref[:, C_VA:C_VA + KV_A_WIDTH] = dva_ref[...].astype(BF16)

    row = lambda width: pl.BlockSpec((tm, width), lambda i: (i, 0))
    full = lambda a: pl.BlockSpec(a.shape, lambda i: (0,) * a.ndim)
    _, ff_specs = _fold_specs(tm, F32)
    return pl.pallas_call(
        body, name="dproj_assemble", grid=(SEQ // tm,),
        in_specs=[row(HALF_WIDTH), row(KV_A_WIDTH), row(KV_A_WIDTH), row(HALF_WIDTH), row(HALF_WIDTH),
                  row(HALF_WIDTH), row(HALF_WIDTH), row(HALF_WIDTH), ff_specs[0], ff_specs[0], ff_specs[0],
                  ff_specs[1], ff_specs[1], ff_specs[1],
                  row(HALF_WIDTH), row(HALF_WIDTH), row(HALF_WIDTH), row(KV_A_WIDTH),
                  full(qkg), row(PAIR), row(PAIR), full(bmean)],
        out_specs=(row(IN_WIDTH), pl.BlockSpec((SMALL_ROWS, PAIR), lambda i: (0, 0))),
        out_shape=(jax.ShapeDtypeStruct((SEQ, IN_WIDTH), BF16), jax.ShapeDtypeStruct((SMALL_ROWS, PAIR), F32)),
        scratch_shapes=[_fold_scratch(tm)],
        compiler_params=_params(("arbitrary",)),
    )(dqa, dka, dva, dga, dgb, dq1, dk1, dv1, dq4, dk4, dv4, dq16, dk16, dv16, tqa, tqb, tkb, tka, qkg, cos4, sin4, bmean)


def _input_grad_reduce(dproj, w, x, gain, dy, blocks_in, blocks_out, small):
    tm = ROW_TILE
    n_steps = SEQ // tm
    stage2_step, stage3_step = 3, 7
    shapes = (blocks_in.shape[1:], blocks_out.shape[1:])

    def body(dp_ref, w_ref, x_ref, g_ref, dy_ref, ga_hbm, gb_hbm, small_ref,
             gx_ref, out_a, out_b, small_out_ref, dgain_out_ref,
             part_a, part_b, sib_a, sib_b, wire_a, wire_b, chips_a, chips_b, small_all, dgain_acc, dgain_all,
             load_sems, sib_send, sib_recv, chip_send, chip_recv, small_send, small_recv, dgain_send, dgain_recv):
        i = pl.program_id(0)
        x, y, c = lax.axis_index("x"), lax.axis_index("y"), lax.axis_index("c")
        sibling = (x, y, 1 - c)
        chips = [(x, y), (1 - x, y), (x, 1 - y), (1 - x, 1 - y)]
        my_id = 4 * x + 2 * y + c
        g_hbm, part, from_sib = (ga_hbm, gb_hbm), (part_a, part_b), (sib_a, sib_b)
        to_wire, from_chips, out = (wire_a, wire_b), (chips_a, chips_b), (out_a, out_b)
        both = (0, 1)

        def blk(a, chip, core):
            return g_hbm[a].at[4 * chip[0] + 2 * chip[1] + core]

        def to_all(src, dst_all, send, recv):
            copies = []
            for rel in range(1, N_DEV):
                dx, dy_, dc = (rel >> 2) & 1, (rel >> 1) & 1, rel & 1
                to = (1 - x if dx else x, 1 - y if dy_ else y, 1 - c if dc else c)
                copies.append(pltpu.make_async_remote_copy(
                    src_ref=src, dst_ref=dst_all.at[my_id], send_sem=send.at[rel - 1], recv_sem=recv.at[rel - 1],
                    device_id=to, device_id_type=MESH))
            return copies

        small_copies = to_all(small_all.at[my_id], small_all, small_send, small_recv)
        dgain_copies = to_all(dgain_acc, dgain_all, dgain_send, dgain_recv)
        loads = [[pltpu.make_async_copy(blk(a, chips[k], c), part[a].at[k], load_sems.at[a, k]) for k in range(4)] for a in both]
        to_sib = [[pltpu.make_async_remote_copy(
            src_ref=blk(a, chips[k], 1 - c), dst_ref=from_sib[a].at[k], send_sem=sib_send.at[a, k], recv_sem=sib_recv.at[a, k],
            device_id=sibling, device_id_type=MESH) for k in range(4)] for a in both]
        first, other, k_first, k_other = _routes()
        to_chips = [[pltpu.make_async_remote_copy(
            src_ref=to_wire[a].at[s], dst_ref=from_chips[a].at[s], send_sem=chip_send.at[a, s], recv_sem=chip_recv.at[a, s],
            device_id=(first, first, other)[s], device_id_type=MESH) for s in range(3)] for a in both]

        def chip_partial(a, k):
            return part[a][k].astype(F32) + from_sib[a][k].astype(F32)

        @pl.when(i == 0)
        def _():
            small_all[my_id] = small_ref[...]
            for cp in small_copies:
                cp.start()
            for k in (1, 2, 3, 0):
                for a in both:
                    loads[a][k].start()
                    to_sib[a][k].start()

        @pl.when(i == stage2_step)
        def _():
            for k in (1, 2, 3):
                for a in both:
                    loads[a][k].wait()
                    to_sib[a][k].wait_recv()
            for s, k in ((0, 3), (1, k_first)):
                for a in both:
                    to_wire[a][s] = chip_partial(a, k).astype(BF16)
                    to_chips[a][s].start()

        @pl.when(i == stage3_step)
        def _():
            for a in both:
                to_chips[a][0].wait_recv()
                to_wire[a][2] = (chip_partial(a, k_other) + from_chips[a][0].astype(F32)).astype(BF16)
                to_chips[a][2].start()

        dh = lax.dot_general(dp_ref[...], w_ref[...], (((1,), (1,)), ((), ())), preferred_element_type=F32)
        xf = x_ref[...]
        r = lax.rsqrt(jnp.mean(xf * xf, axis=-1, keepdims=True) + EPS)
        xhat = xf * r
        dg = jnp.sum(dh * xhat, axis=0, keepdims=True)
        dxh = dh * g_ref[...]
        dx = r * (dxh - xhat * jnp.mean(dxh * xhat, axis=-1, keepdims=True))
        gx_ref[...] = dy_ref[...] + dx

        @pl.when(i == 0)
        def _():
            dgain_acc[...] = dg

        @pl.when(i > 0)
        def _():
            dgain_acc[...] += dg

        @pl.when(i == n_steps - 1)
        def _():
            dgain_all[my_id] = dgain_acc[...]
            for cp in dgain_copies:
                cp.start()
            for a in both:
                loads[a][0].wait()
                to_sib[a][0].wait_recv()
                acc = chip_partial(a, 0)
                for s in (1, 2):
                    to_chips[a][s].wait_recv()
                    acc = acc + from_chips[a][s].astype(F32)
                out[a][...] = acc
            for copies, gathered, dst in ((small_copies, small_all, small_out_ref), (dgain_copies, dgain_all, dgain_out_ref)):
                for cp in copies:
                    cp.wait_recv()
                tot = gathered[0]
                for d in range(1, N_DEV):
                    tot = tot + gathered[d]
                dst[...] = tot
            for cp in to_sib[0] + to_sib[1] + to_chips[0] + to_chips[1] + small_copies + dgain_copies:
                cp.wait_send()

    row = lambda width: pl.BlockSpec((tm, width), lambda i: (i, 0))
    full = lambda a: pl.BlockSpec(a.shape, lambda i: (0,) * a.ndim)
    whole = lambda shape: pl.BlockSpec(shape, lambda i: (0,) * len(shape))
    hbm = pl.BlockSpec(memory_space=pl.ANY)
    dtypes = (blocks_in.dtype, blocks_out.dtype)
    buf = lambda n, dts: [pltpu.VMEM((n,) + s, dt) for s, dt in zip(shapes, dts)]
    return pl.pallas_call(
        body, name="input_grad_rs", grid=(n_steps,),
        in_specs=[row(IN_WIDTH), full(w), row(D_MODEL), full(gain), row(D_MODEL), hbm, hbm, full(small)],
        out_specs=(row(D_MODEL), whole(shapes[0]), whole(shapes[1]), whole((SMALL_ROWS, SMALL_COLS)), whole((1, D_MODEL))),
        out_shape=(jax.ShapeDtypeStruct((SEQ, D_MODEL), F32), jax.ShapeDtypeStruct(shapes[0], F32),
                   jax.ShapeDtypeStruct(shapes[1], F32), jax.ShapeDtypeStruct((SMALL_ROWS, SMALL_COLS), F32),
                   jax.ShapeDtypeStruct((1, D_MODEL), F32)),
        scratch_shapes=[*buf(4, dtypes), *buf(4, dtypes), *buf(3, (BF16, BF16)), *buf(3, (BF16, BF16)),
                        pltpu.VMEM((N_DEV, SMALL_ROWS, SMALL_COLS), F32),
                        pltpu.VMEM((1, D_MODEL), F32), pltpu.VMEM((N_DEV, 1, D_MODEL), F32),
                        pltpu.SemaphoreType.DMA((2, 4)), pltpu.SemaphoreType.DMA((2, 4)), pltpu.SemaphoreType.DMA((2, 4)),
                        pltpu.SemaphoreType.DMA((2, 3)), pltpu.SemaphoreType.DMA((2, 3)),
                        pltpu.SemaphoreType.DMA((7,)), pltpu.SemaphoreType.DMA((7,)),
                        pltpu.SemaphoreType.DMA((7,)), pltpu.SemaphoreType.DMA((7,))],
        compiler_params=_params(("arbitrary",)),
    )(dproj, w, x, gain, dy, blocks_in, blocks_out, small)


def _weight_grad(h_t, dproj):
    tk = 1024
    cb = IN_WIDTH // 2
    n_k = SEQ // tk

    def body(ht_ref, dp_ref, out_ref, acc):
        k = pl.program_id(1)
        upd = jnp.dot(ht_ref[...], dp_ref[...], preferred_element_type=F32)

        @pl.when(k == 0)
        def _():
            acc[...] = upd

        @pl.when(k > 0)
        def _():
            acc[...] += upd

        @pl.when(k == n_k - 1)
        def _():
            for b in range(N_DEV // 2):
                out_ref[b] = acc[:, b * SHARD_IN:(b + 1) * SHARD_IN].astype(BF16)

    return pl.pallas_call(
        body, name="weight_grad", grid=(2, n_k),
        in_specs=[pl.BlockSpec((D_MODEL, tk), lambda j, k: (0, k)), pl.BlockSpec((tk, cb), lambda j, k: (k, j))],
        out_specs=pl.BlockSpec((N_DEV // 2, D_MODEL, SHARD_IN), lambda j, k: (j, 0, 0)),
        out_shape=jax.ShapeDtypeStruct((N_DEV, D_MODEL, SHARD_IN), BF16),
        scratch_shapes=[pltpu.VMEM((D_MODEL, cb), F32)],
        compiler_params=_params(("arbitrary", "arbitrary")),
    )(h_t, dproj)


def _adamw_update(w, g, m, v):
    nm = ADAM_B1 * m + (1.0 - ADAM_B1) * g
    nv = ADAM_B2 * v + (1.0 - ADAM_B2) * jnp.square(g)
    m_hat = nm / (1.0 - ADAM_B1 ** ADAM_STEP)
    v_hat = nv / (1.0 - ADAM_B2 ** ADAM_STEP)
    return -ADAM_LR * (m_hat / (jnp.sqrt(v_hat) + ADAM_EPS) + ADAM_WD * w), nm, nv


def _adamw(transposed, groups, small):
    params, ms, vs, packed_grads, gain_grad = small
    n_big, n_small = 4 * (len(groups) + 1), len(params)

    def body(*refs):
        n_in = n_big + 3 * n_small + 2
        ins, outs = refs[:n_in], refs[n_in:]
        p_refs, m_refs, v_refs = (ins[n_big + k * n_small:n_big + (k + 1) * n_small] for k in range(3))
        packed_ref, gain_grad_ref = ins[n_big + 3 * n_small:]
        small_outs = outs[4 + 3 * len(groups):]
        for k, window in enumerate(SMALL_WINDOWS):
            gv = gain_grad_ref[...] if window is None else packed_ref[window]
            results = (gv, *_adamw_update(p_refs[k][...], gv, m_refs[k][...], v_refs[k][...]))
            for out_ref, value in zip(small_outs[4 * k:4 * k + 4], results):
                out_ref[...] = value
        wt_ref, g_ref, mt_ref, vt_ref = ins[:4]
        gt_ref, dt_ref, nmt_ref, nvt_ref = outs[:4]
        cols = g_ref.shape[1]
        for off in [*range(0, cols - LANES, LANES), cols - LANES]:
            band = slice(off, off + LANES)
            gv = g_ref[:, band].T
            gt_ref[band, :] = gv
            dt_ref[band, :], nmt_ref[band, :], nvt_ref[band, :] = _adamw_update(
                wt_ref[band, :], gv, mt_ref[band, :], vt_ref[band, :])
        for i in range(len(groups)):
            w_ref, g_ref, m_ref, v_ref = ins[4 * i + 4:4 * i + 8]
            d_ref, nm_ref, nv_ref = outs[3 * i + 4:3 * i + 7]
            d_ref[...], nm_ref[...], nv_ref[...] = _adamw_update(w_ref[...], g_ref[...], m_ref[...], v_ref[...])

    vmem = pl.BlockSpec(memory_space=pltpu.VMEM)
    out_shape = [jax.ShapeDtypeStruct(transposed[0].shape, F32)] * 4
    out_shape += [jax.ShapeDtypeStruct(w.shape, F32) for w, _, _, _ in groups for _ in range(3)]
    out_shape += [jax.ShapeDtypeStruct(p.shape, F32) for p in params for _ in range(4)]
    outs = pl.pallas_call(
        body, name="adamw", in_specs=[vmem] * (n_big + 3 * n_small + 2), out_specs=(vmem,) * len(out_shape),
        out_shape=tuple(out_shape), compiler_params=pltpu.CompilerParams(vmem_limit_bytes=VMEM_LIMIT),
    )(*transposed, *[a for group in groups for a in group], *params, *ms, *vs, packed_grads, gain_grad)
    first_small = 4 + 3 * len(groups)
    return (tuple(outs[:4]), [tuple(outs[3 * i + 4:3 * i + 7]) for i in range(len(groups))],
            [tuple(outs[first_small + 4 * k:first_small + 4 * k + 4]) for k in range(n_small)])


SMALL_USED = D_MODEL + 4 * HEAD_DIM + 8


def _pack_small_grads(qa, ka, qb, kb, sinks, loss):
    flat = jnp.concatenate([jnp.zeros((D_MODEL,), F32), qa, ka, qb, kb, sinks, loss.reshape(1)])
    flat = jnp.pad(flat, (0, SMALL_ROWS * SMALL_COLS - flat.shape[0]))
    return flat.reshape(SMALL_ROWS, SMALL_COLS)


_GAINS_ROW, _SINKS_ROW = D_MODEL // SMALL_COLS, D_MODEL // SMALL_COLS + 1
_head_window = lambda k: (slice(_GAINS_ROW, _GAINS_ROW + 1), slice(k * HEAD_DIM, (k + 1) * HEAD_DIM))
SMALL_WINDOWS = (None, _head_window(0), _head_window(1), (slice(_SINKS_ROW, _SINKS_ROW + 1), slice(0, 8)),
                 _head_window(2), _head_window(3))


def _fold_heads(row):
    return row[0, :HEAD_DIM] + row[0, HEAD_DIM:]


def kernel(x, norm_gain, w_in, q_norm_a, k_norm_a, sinks_a, q_norm_b, k_norm_b, w_out, loss_target, m_norm_gain, m_w_in, m_q_norm_a, m_k_norm_a, m_sinks_a, m_q_norm_b, m_k_norm_b, m_w_out, v_norm_gain, v_w_in, v_q_norm_a, v_k_norm_a, v_sinks_a, v_q_norm_b, v_k_norm_b, v_w_out):
    x2, tgt = x[0], loss_target[0]
    w_out_sh = w_out[0]
    as_held = lambda a: jnp.swapaxes(a, 1, 2)[0]

    w_full = _all_gather_w_in(as_held(w_in))

    inv = np.float32(ROPE_THETA) ** (-np.arange(HEAD_DIM // 2, dtype=np.float32) / np.float32(HEAD_DIM // 2))
    ang = np.arange(SEQ, dtype=np.float32)[:, None] * inv[None, :].astype(np.float32)
    cos, sin = np.cos(ang).astype(np.float32), np.sin(ang).astype(np.float32)
    cos4 = jnp.asarray(np.concatenate([cos, cos, cos, cos], axis=1))
    sin4 = jnp.asarray(np.concatenate([-sin, sin, -sin, sin], axis=1))
    blockdiag = np.kron(np.eye(2, dtype=np.float32), np.ones((HEAD_DIM, HEAD_DIM), np.float32))
    bmean = jnp.asarray(blockdiag / HEAD_DIM, dtype=BF16)
    gather_np = np.kron(np.eye(2 * N_PAIRS, dtype=np.float32), np.ones((HEAD_DIM, STAT_REP), np.float32))
    spread_np = np.kron(np.eye(2 * N_PAIRS, dtype=np.float32), np.ones((STAT_REP, HEAD_DIM), np.float32))
    spread_np[np.arange(STAT_WIDTH) % STAT_REP != 0] = 0.0
    gather, spread = jnp.asarray(gather_np, dtype=BF16), jnp.asarray(spread_np, dtype=BF16)
    two = lambda g: jnp.concatenate([g, g], axis=1)
    qkg = jnp.concatenate([two(q_norm_a), two(k_norm_a), two(q_norm_b), two(k_norm_b),
                           jnp.zeros((SMALL_ROWS - 4, PAIR), F32)], axis=0)
    sinks_paired = jnp.stack([sinks_a[0, :N_PAIRS], sinks_a[0, N_PAIRS:]], axis=1) * LOG2E
    sink_rows = jnp.concatenate([jnp.repeat(sinks_paired, BLOCK, axis=1),
                                 jnp.zeros((SMALL_ROWS - N_PAIRS, 2 * BLOCK), F32)], axis=0)

    (tqa, tka, tqb, tkb, gate_a, gate_b, h_t, qa, ka, va, qb, kb, vb, qb4, qb16, kb4, kb16, vb4, vb16,
     gathered_out) = _proj_fwd(x2, norm_gain, w_full, qkg, cos4, sin4, bmean, w_out_sh)
    wo_full = gathered_out.reshape(D_MODEL, D_MODEL)
    (oa, la), (ob1, lb1), (ob4, lb4), (ob16, lb16) = _attn_fwd("attn_fwd", [
        (qa[None], ka[None], va[None], sink_rows, BLOCK - 1), (qb[None], kb[None], vb[None], None, BLOCK),
        (qb4, kb4, vb4, None, BLOCK), (qb16, kb16, vb16, None, BLOCK)])
    (loss_cols, dy, gwo, doa, dla, dga, dgb, dob, dob4, dob16, dlb, dlb4, dlb16, lse_b, lse4, lse16) = _tail(
        oa[0], ob1[0], lb1[0], ob4, lb4, ob16, lb16, gate_a, gate_b, x2, tgt, wo_full, spread, gather)

    dqa, dka, dva, dsink = _attn_bwd("attn_a_bwd", qa[None], ka[None], va[None], doa[None], la, dla[None], sink_rows, BLOCK - 1)
    dq1, dk1, dv1 = _attn_bwd("attn_b1_bwd", qb[None], kb[None], vb[None], dob[None], lse_b[None], dlb[None], None, BLOCK)
    dq4, dk4, dv4 = _attn_bwd("attn_b4_bwd", qb4, kb4, vb4, dob4, lse4, dlb4, None, BLOCK)
    dq16, dk16, dv16 = _attn_bwd("attn_b16_bwd", qb16, kb16, vb16, dob16, lse16, dlb16, None, BLOCK)
    dproj, dqkg = _dproj_assemble(dqa[0], dka[0], dva[0], dga, dgb, dq1[0], dk1[0], dv1[0], dq4, dk4, dv4,
                                  dq16, dk16, dv16, tqa, tqb, tkb, tka, qkg, cos4, sin4, bmean)
    gw_in = _weight_grad(h_t, dproj)

    blocks_in = gw_in
    blocks_out = gwo.reshape(N_DEV, SHARD_OUT, D_MODEL)
    g_sinks = jnp.concatenate([jnp.sum(dsink[:N_PAIRS, :BLOCK], axis=1), jnp.sum(dsink[:N_PAIRS, BLOCK:], axis=1)])
    small = _pack_small_grads(_fold_heads(dqkg[0:1]), _fold_heads(dqkg[1:2]), _fold_heads(dqkg[2:3]),
                              _fold_heads(dqkg[3:4]), g_sinks, 0.5 * jnp.sum(loss_cols) / D_MODEL)
    grad_x, g_w_in, g_w_out, small_red, dgain_red = _input_grad_reduce(
        dproj, w_full, x2, norm_gain, dy, blocks_in, blocks_out, small)

    w_in_results, [(d_out, nm_out, nv_out)], small_results = _adamw(
        (as_held(w_in), g_w_in, as_held(m_w_in), as_held(v_w_in)),
        [(w_out_sh, g_w_out, m_w_out[0], v_w_out[0])],
        ([norm_gain, q_norm_a, k_norm_a, sinks_a, q_norm_b, k_norm_b],
         [m_norm_gain, m_q_norm_a, m_k_norm_a, m_sinks_a, m_q_norm_b, m_k_norm_b],
         [v_norm_gain, v_q_norm_a, v_k_norm_a, v_sinks_a, v_q_norm_b, v_k_norm_b], small_red, dgain_red))
    g_small, d_small, nm_small, nv_small = ([leaf[j] for leaf in small_results] for j in range(4))
    g_in, d_in, nm_in, nv_in = [jnp.swapaxes(a, 0, 1) for a in w_in_results]

    loss = small_red.reshape(-1)[SMALL_USED]

    def assemble(small_list, big_in, big_out):
        ng, qa_, ka_, sk_, qb_, kb_ = small_list
        return [ng, big_in[None], qa_, ka_, sk_, qb_, kb_, big_out[None]]

    return (loss, grad_x[None], *assemble(g_small, g_in, g_w_out), *assemble(d_small, d_in, d_out),
            *assemble(nm_small, nm_in, nm_out), *assemble(nv_small, nv_in, nv_out))
```

```python
import numpy as np
import jax
import jax.numpy as jnp
from jax import lax
from jax.experimental import pallas as pl
from jax.experimental.pallas import tpu as pltpu

F32 = jnp.float32
BF16 = jnp.bfloat16

SEQ = 4096
D_MODEL = 1024
HEAD_DIM = 64
PAIR = 2 * HEAD_DIM
N_PAIRS = 4
HALF_WIDTH = N_PAIRS * PAIR
KV_A_WIDTH = 128
IN_WIDTH = 3328
LANES = 128
BLOCK = 128
STAT_REP = 16
STAT_WIDTH = 128
EPS = 1e-6
NEG = -1e30
ROPE_THETA = 10000.0
LOG2E = 1.4426950408889634
LN2 = 0.6931471805599453
Q_SCALE = HEAD_DIM ** -0.5 * LOG2E
N_DEV = 8
SHARD_IN = IN_WIDTH // N_DEV
SHARD_OUT = D_MODEL // N_DEV
SMALL_ROWS, SMALL_COLS = 8, 256

C_QA, C_KA, C_VA, C_GA, C_QB, C_KB, C_VB, C_GB = 0, 512, 640, 768, 1280, 1792, 2304, 2816

ADAM_LR = 0.001
ADAM_B1 = 0.9
ADAM_B2 = 0.999
ADAM_EPS = 1e-08
ADAM_WD = 0.01
ADAM_STEP = 10

ROW_TILE = 256
PROJ_ROW_TILE = 512
FWD_BLOCKS_PER_STEP = 8
BWD_BLOCKS_PER_STEP = 8
ADAMW_STEPS = 4
VMEM_LIMIT = 56 * 1024 * 1024

MESH = pl.DeviceIdType.MESH


def _params(sem, vmem=VMEM_LIMIT):
    return pltpu.CompilerParams(dimension_semantics=sem, vmem_limit_bytes=vmem)


def _head_sum(v, bm):
    return jnp.dot(v.astype(BF16), bm, preferred_element_type=F32)


def _swap_halves(y):
    lane = lax.broadcasted_iota(jnp.int32, y.shape, 1)
    first = (lane & 32) == 0
    return jnp.where(first, pltpu.roll(y, 96, 1), pltpu.roll(y, 32, 1))


def _sigmoid(g):
    return 1.0 / (1.0 + jnp.exp(-g))


def _tiles(a):
    return [a[:, j * PAIR:(j + 1) * PAIR] for j in range(N_PAIRS)]


def _pair_tiles(t):
    low = lax.broadcasted_iota(jnp.int32, t[0].shape, 1) < HEAD_DIM
    r = [pltpu.roll(a, HEAD_DIM, 1) for a in t]
    return [jnp.where(low, t[0], r[2]), jnp.where(low, r[0], t[2]), jnp.where(low, t[1], r[3]), jnp.where(low, r[1], t[3])]


def _unpair_tiles(p):
    low = lax.broadcasted_iota(jnp.int32, p[0].shape, 1) < HEAD_DIM
    r = [pltpu.roll(a, HEAD_DIM, 1) for a in p]
    return [jnp.where(low, p[0], r[1]), jnp.where(low, p[2], r[3]), jnp.where(low, r[0], p[1]), jnp.where(low, r[2], p[3])]


def _routes():
    x, y, c = lax.axis_index("x"), lax.axis_index("y"), lax.axis_index("c")
    north = c == 1
    first = (jnp.where(north, 1 - x, x), jnp.where(north, y, 1 - y), c)
    other = (jnp.where(north, x, 1 - x), jnp.where(north, 1 - y, y), c)
    k_first = jnp.where(north, 1, 2)
    return first, other, k_first, 3 - k_first


def _gather_plan(mine_ref, out_ref, send_sems, recv_sems):
    x, y, c = lax.axis_index("x"), lax.axis_index("y"), lax.axis_index("c")
    me, sibling, diag = (x, y, c), (x, y, 1 - c), (1 - x, 1 - y, c)
    first, other, k_first, k_other = _routes()

    def slot(px, py, pc):
        return out_ref.at[4 * px + 2 * py + pc]

    def copy(k, block, to, from_mine=False):
        return pltpu.make_async_remote_copy(
            src_ref=mine_ref if from_mine else slot(*block), dst_ref=slot(*block),
            send_sem=send_sems.at[k], recv_sem=recv_sems.at[k], device_id=to, device_id_type=MESH)

    sends = [copy(0, me, sibling, True), copy(1, me, (1 - x, y, c), True), copy(2, me, (x, 1 - y, c), True)]
    stages = [(copy(k_first, first, me), [copy(3, first, other), copy(3 + k_first, first, sibling)]),
              (copy(k_other, other, me), [copy(3 + k_other, other, sibling)]),
              (copy(3, diag, me), [copy(6, diag, sibling)])]
    from_sibling = [copy(0, sibling, me), copy(4, (1 - x, y, 1 - c), me), copy(5, (x, 1 - y, 1 - c), me),
                    copy(6, (1 - x, 1 - y, 1 - c), me)]
    return slot(*me), sends, stages, from_sibling


GATHER_SCRATCH = [pltpu.SemaphoreType.DMA((7,)), pltpu.SemaphoreType.DMA((7,))]


def _all_gather_w_in(w_in_sh_t):
    cols, rows = w_in_sh_t.shape

    def body(wt_ref, out_ref, mine_ref, blocks, send_sems, recv_sems):
        for off in [*range(0, cols - LANES, LANES), cols - LANES]:
            mine_ref[:, off:off + LANES] = wt_ref[off:off + LANES, :].T.astype(BF16)
        my_slot, sends, stages, from_sibling = _gather_plan(mine_ref, blocks, send_sems, recv_sems)
        for cp in sends:
            cp.start()
        my_slot[...] = mine_ref[...]
        for arrival, forwards in stages:
            arrival.wait_recv()
            for cp in forwards:
                cp.start()
        for arrival in from_sibling:
            arrival.wait_recv()
        for cp in sends + [cp for _, forwards in stages for cp in forwards]:
            cp.wait_send()
        for d in range(N_DEV):
            out_ref[:, d * cols:(d + 1) * cols] = blocks[d]

    vmem = pl.BlockSpec(memory_space=pltpu.VMEM)
    return pl.pallas_call(
        body, name="ag_w_in",
        out_shape=jax.ShapeDtypeStruct((rows, N_DEV * cols), BF16),
        in_specs=[vmem], out_specs=vmem,
        scratch_shapes=[pltpu.VMEM((rows, cols), BF16), pltpu.VMEM((N_DEV, rows, cols), BF16)] + GATHER_SCRATCH,
        compiler_params=pltpu.CompilerParams(vmem_limit_bytes=VMEM_LIMIT),
    )(w_in_sh_t)


def _fold_scratch(tm):
    return pltpu.VMEM((N_PAIRS, tm, PAIR), F32)


def _fold_store(val, scr, out4, out16, tm):
    groups = range(val.shape[1] // PAIR)
    for j in groups:
        scr[j] = val[:, j * PAIR:(j + 1) * PAIR]
    for dil, out in ((4, out4), (16, out16)):
        for r in range(dil):
            for j in groups:
                out[r, :, j * PAIR:(j + 1) * PAIR] = scr[j, pl.ds(r, tm // dil, stride=dil), :].astype(out.dtype)


def _unfold_load(src, scr, dil, tm):
    groups = range(src.shape[2] // PAIR)
    for r in range(dil):
        for j in groups:
            scr[j, pl.ds(r, tm // dil, stride=dil), :] = src[r, :, j * PAIR:(j + 1) * PAIR].astype(F32)
    return jnp.concatenate([scr[j] for j in groups], axis=1)


def _fold_specs(tm, dtype, width=HALF_WIDTH):
    shapes = (jax.ShapeDtypeStruct((4, SEQ // 4, width), dtype), jax.ShapeDtypeStruct((16, SEQ // 16, width), dtype))
    specs = (pl.BlockSpec((4, tm // 4, width), lambda i: (0, i, 0)),
             pl.BlockSpec((16, tm // 16, width), lambda i: (0, i, 0)))
    return shapes, specs


def _proj_fwd(x, gain, w, qkg, cos4, sin4, bmean, w_out_sh):
    tm = PROJ_ROW_TILE
    n_steps = SEQ // tm

    def norm_rope(t, g, cos, sin, bm, scale):
        rr = lax.rsqrt(_head_sum(t * t, bm) + EPS)
        yv = t * rr * g
        return (yv * cos + _swap_halves(yv) * sin) * scale

    def body(x_ref, g_ref, w_ref, qkg_ref, cos_ref, sin_ref, bm_ref, wo_ref,
             tqa_ref, tka_ref, tqb_ref, tkb_ref, ga_ref, gb_ref, ht_ref, qa_ref, ka_ref, va_ref, qb_ref, kb_ref, vb_ref,
             qb4_ref, qb16_ref, kb4_ref, kb16_ref, vb4_ref, vb16_ref, wo_all_ref,
             proj, scr, wo_mine, wo_all, send_sems, recv_sems):
        i = pl.program_id(0)
        my_slot, sends, stages, from_sibling = _gather_plan(wo_mine, wo_all, send_sems, recv_sems)

        @pl.when(i == 0)
        def _():
            wo_mine[...] = wo_ref[...].astype(BF16)
            for cp in sends:
                cp.start()
            my_slot[...] = wo_mine[...]

        @pl.when(i == n_steps // 2)
        def _():
            for arrival, forwards in stages[:2]:
                arrival.wait_recv()
                for cp in forwards:
                    cp.start()

        xf = x_ref[...]
        r = lax.rsqrt(jnp.mean(xf * xf, axis=-1, keepdims=True) + EPS)
        hf = xf * r * g_ref[...]
        ht_ref[...] = hf.T.astype(BF16)
        cos, sin, bm = cos_ref[...], sin_ref[...], bm_ref[...]
        proj[...] = jnp.dot(hf.astype(BF16), w_ref[...], preferred_element_type=F32)

        def roped(tiles, row, scale):
            g = qkg_ref[row:row + 1, :]
            return jnp.concatenate([norm_rope(t, g, cos, sin, bm, scale) for t in tiles], axis=1)

        tqa = _pair_tiles(_tiles(proj[:, C_QA:C_QA + HALF_WIDTH]))
        tqa_ref[...] = jnp.concatenate(tqa, axis=1).astype(BF16)
        qa_ref[...] = roped(tqa, 0, Q_SCALE).astype(BF16)
        ga_ref[...] = jnp.concatenate(_pair_tiles(_tiles(proj[:, C_GA:C_GA + HALF_WIDTH])), axis=1).astype(BF16)
        gb_ref[...] = proj[:, C_GB:C_GB + HALF_WIDTH].astype(BF16)
        tqb = proj[:, C_QB:C_QB + HALF_WIDTH]
        tqb_ref[...] = tqb.astype(BF16)
        qb = roped(_tiles(tqb), 2, Q_SCALE)
        qb_ref[...] = qb.astype(BF16)
        _fold_store(qb, scr, qb4_ref, qb16_ref, tm)
        tkb = proj[:, C_KB:C_KB + HALF_WIDTH]
        tkb_ref[...] = tkb.astype(BF16)
        kb = roped(_tiles(tkb), 3, 1.0)
        kb_ref[...] = kb.astype(BF16)
        _fold_store(kb, scr, kb4_ref, kb16_ref, tm)
        vb = proj[:, C_VB:C_VB + HALF_WIDTH]
        vb_ref[...] = vb.astype(BF16)
        _fold_store(vb, scr, vb4_ref, vb16_ref, tm)
        tka = proj[:, C_KA:C_KA + KV_A_WIDTH]
        tka_ref[...] = tka.astype(BF16)
        ka_ref[...] = roped([tka], 1, 1.0).astype(BF16)
        va_ref[...] = proj[:, C_VA:C_VA + KV_A_WIDTH].astype(BF16)

        @pl.when(i == n_steps - 1)
        def _():
            arrival, forwards = stages[2]
            arrival.wait_recv()
            for cp in forwards:
                cp.start()
            for arrival in from_sibling:
                arrival.wait_recv()
            for cp in sends + [cp for _, forwards in stages for cp in forwards]:
                cp.wait_send()
            wo_all_ref[...] = wo_all[...]

    row = lambda width: pl.BlockSpec((tm, width), lambda i: (i, 0))
    full = lambda a: pl.BlockSpec(a.shape, lambda i: (0,) * a.ndim)
    nat = lambda width, dtype=BF16: jax.ShapeDtypeStruct((SEQ, width), dtype)
    f_shapes, f_specs = _fold_specs(tm, BF16)
    return pl.pallas_call(
        body, name="proj_fwd", grid=(SEQ // tm,),
        in_specs=[row(D_MODEL), full(gain), full(w), full(qkg), row(PAIR), row(PAIR), full(bmean), full(w_out_sh)],
        out_specs=(row(HALF_WIDTH), row(KV_A_WIDTH), row(HALF_WIDTH), row(HALF_WIDTH), row(HALF_WIDTH), row(HALF_WIDTH),
                   pl.BlockSpec((D_MODEL, tm), lambda i: (0, i)),
                   row(HALF_WIDTH), row(KV_A_WIDTH), row(KV_A_WIDTH), row(HALF_WIDTH), row(HALF_WIDTH), row(HALF_WIDTH),
                   *f_specs, *f_specs, *f_specs,
                   pl.BlockSpec((N_DEV,) + w_out_sh.shape, lambda i: (0, 0, 0))),
        out_shape=(nat(HALF_WIDTH), nat(KV_A_WIDTH), nat(HALF_WIDTH), nat(HALF_WIDTH), nat(HALF_WIDTH), nat(HALF_WIDTH),
                   jax.ShapeDtypeStruct((D_MODEL, SEQ), BF16),
                   nat(HALF_WIDTH), nat(KV_A_WIDTH), nat(KV_A_WIDTH), nat(HALF_WIDTH), nat(HALF_WIDTH), nat(HALF_WIDTH),
                   *f_shapes, *f_shapes, *f_shapes,
                   jax.ShapeDtypeStruct((N_DEV,) + w_out_sh.shape, BF16)),
        scratch_shapes=[pltpu.VMEM((tm, IN_WIDTH), F32), _fold_scratch(tm), pltpu.VMEM(w_out_sh.shape, BF16),
                        pltpu.VMEM((N_DEV,) + w_out_sh.shape, BF16)] + GATHER_SCRATCH,
        compiler_params=_params(("arbitrary",)),
    )(x, gain, w, qkg, cos4, sin4, bmean, w_out_sh)


def _fill_band_bias(bias_ref, max_dist):
    j = lax.broadcasted_iota(jnp.int32, (2 * BLOCK, 2 * BLOCK), 0)
    c = lax.broadcasted_iota(jnp.int32, (2 * BLOCK, 2 * BLOCK), 1)
    dist = (c & (BLOCK - 1)) + BLOCK - j
    band = (dist >= 0) & (dist <= max_dist)
    bias_ref[0] = jnp.where(band, 0.0, NEG)
    bias_ref[1] = jnp.where(band & (j >= BLOCK), 0.0, NEG)


def _band_bias(bias_ref, step, b, qb, nb):
    if nb < qb:
        return bias_ref[1 if b % nb == 0 else 0]
    if b > 0:
        return bias_ref[0]
    return bias_ref[jnp.where(((step * qb) & (nb - 1)) == 0, 1, 0)]


def _stack_heads(t):
    lane = lax.broadcasted_iota(jnp.int32, t.shape, 1)
    low = lane < HEAD_DIM
    zero = jnp.zeros_like(t)
    return jnp.concatenate([jnp.where(low, t, zero), jnp.where(low, zero, t)], axis=0)


def _stack_heads_t(t):
    tt = t.astype(F32).T
    low = lax.broadcasted_iota(jnp.int32, tt.shape, 0) < HEAD_DIM
    zero = jnp.zeros_like(tt)
    return jnp.concatenate([jnp.where(low, tt, zero), jnp.where(low, zero, tt)], axis=1).astype(BF16)


def _unstack_t(t):
    return jnp.concatenate([t[:HEAD_DIM, :BLOCK], t[HEAD_DIM:, BLOCK:]], axis=0).T


def _rows_to_stats(rows):
    parts = []
    for row in rows:
        parts.append(jnp.broadcast_to(row[:, :BLOCK], (STAT_REP, BLOCK)))
        parts.append(jnp.broadcast_to(row[:, BLOCK:], (STAT_REP, BLOCK)))
    return jnp.concatenate(parts, axis=0).T


def _stats_to_rows(t):
    tt = t.T
    return [jnp.concatenate([tt[2 * p * STAT_REP:2 * p * STAT_REP + 1, :],
                             tt[(2 * p + 1) * STAT_REP:(2 * p + 1) * STAT_REP + 1, :]], axis=1) for p in range(N_PAIRS)]


def _attn_fwd(name, patterns):
    qb = FWD_BLOCKS_PER_STEP
    steps = SEQ // (qb * BLOCK)

    def one_pattern(step, nb, shared, max_dist, q_ref, kc_ref, vc_ref, sink_ref, o_ref, lse_ref, kp_ref, vp_ref, bias_ref):
        has_sinks = sink_ref is not None

        @pl.when(step == 0)
        def _():
            kp_ref[...] = jnp.zeros_like(kp_ref)
            vp_ref[...] = jnp.zeros_like(vp_ref)
            _fill_band_bias(bias_ref, max_dist)

        cols = [slice(p * PAIR, (p + 1) * PAIR) for p in range(N_PAIRS)]
        kcols = [slice(0, PAIR) if shared else c for c in cols]
        rows = [slice(b * BLOCK, (b + 1) * BLOCK) for b in range(qb)]
        units = [(b, p) for b in range(qb) for p in range(N_PAIRS)]
        n = range(len(units))

        def window(prev_ref, cur_ref, b, kc):
            before = prev_ref[:, kc] if b == 0 else cur_ref[rows[b - 1], kc]
            return jnp.concatenate([before, cur_ref[rows[b], kc]], axis=0)

        st = [jnp.dot(window(kp_ref, kc_ref, b, kcols[p]), _stack_heads_t(q_ref[rows[b], cols[p]]),
                      preferred_element_type=F32) for b, p in units]
        st = [st[u] + _band_bias(bias_ref, step, units[u][0], qb, nb) for u in n]
        m = [jnp.max(s, axis=0, keepdims=True) for s in st]
        if has_sinks:
            sk = [sink_ref[p:p + 1, :] for _, p in units]
            m = [jnp.maximum(m[u], sk[u]) for u in n]
        pt = [jnp.exp2(st[u] - m[u]) for u in n]
        l = [jnp.sum(t, axis=0, keepdims=True) for t in pt]
        if has_sinks:
            l = [l[u] + jnp.exp2(sk[u] - m[u]) for u in n]
        v2t = [window(vp_ref, vc_ref, b, kcols[p]).astype(F32).T.astype(BF16) for b, p in units]
        ot = [jnp.dot(v2t[u], pt[u].astype(BF16), preferred_element_type=F32) / l[u] for u in n]
        for u, (b, p) in enumerate(units):
            o_ref[rows[b], cols[p]] = _unstack_t(ot[u]).astype(BF16)
        for b in range(qb):
            lse_ref[rows[b], :] = _rows_to_stats([m[u] + jnp.log2(l[u]) for u in n if units[u][0] == b])
        kp_ref[...] = kc_ref[rows[-1], :]
        vp_ref[...] = vc_ref[rows[-1], :]

    n_in = [4 if sinks is not None else 3 for _, _, _, sinks, _ in patterns]

    def body(*refs):
        ins, rest = refs[:sum(n_in)], refs[sum(n_in):]
        outs, scratch = rest[:2 * len(patterns)], rest[2 * len(patterns):]
        bias_ref = scratch[-1]
        step = pl.program_id(0)
        first = 0
        for p, (q, k, _, sinks, max_dist) in enumerate(patterns):
            mine = ins[first:first + n_in[p]]
            first += n_in[p]
            sink_ref = mine[3] if sinks is not None else None
            kp_ref, vp_ref = scratch[2 * p], scratch[2 * p + 1]

            @pl.when((step >= p * steps) & (step < (p + 1) * steps))
            def _():
                one_pattern(step - p * steps, q.shape[1] // BLOCK, k.shape[2] == PAIR, max_dist,
                            mine[0], mine[1], mine[2], sink_ref, outs[2 * p], outs[2 * p + 1], kp_ref, vp_ref, bias_ref)

    def during(p, width):
        return pl.BlockSpec((qb * BLOCK, width), lambda s: (jnp.clip(s - p * steps, 0, steps - 1), 0))

    flat = lambda a: a.reshape(SEQ, a.shape[2])
    in_specs, args, out_specs, out_shape, scratch = [], [], [], [], []
    for p, (q, k, v, sinks, _) in enumerate(patterns):
        ck = k.shape[2]
        in_specs += [during(p, HALF_WIDTH), during(p, ck), during(p, ck)]
        args += [flat(q), flat(k), flat(v)]
        if sinks is not None:
            in_specs.append(pl.BlockSpec(sinks.shape, lambda s: (0, 0)))
            args.append(sinks)
        out_specs += [during(p, HALF_WIDTH), during(p, STAT_WIDTH)]
        out_shape += [jax.ShapeDtypeStruct((SEQ, HALF_WIDTH), BF16), jax.ShapeDtypeStruct((SEQ, STAT_WIDTH), F32)]
        scratch += [pltpu.VMEM((BLOCK, ck), BF16), pltpu.VMEM((BLOCK, ck), BF16)]
    outs = pl.pallas_call(
        body, name=name, grid=(len(patterns) * steps,), in_specs=in_specs,
        out_specs=tuple(out_specs), out_shape=tuple(out_shape),
        scratch_shapes=scratch + [pltpu.VMEM((2, 2 * BLOCK, 2 * BLOCK), F32)],
        compiler_params=_params(("arbitrary",)),
    )(*args)
    return [(outs[2 * p].reshape(q.shape), outs[2 * p + 1].reshape(q.shape[0], q.shape[1], STAT_WIDTH))
            for p, (q, _, _, _, _) in enumerate(patterns)]


def _attn_bwd(name, q, k, v, d_o, lse, delta, sink_rows, max_dist):
    n_seq, length, _ = q.shape
    ck = k.shape[2]
    nb = length // BLOCK
    n_blocks = n_seq * nb
    n_rows = n_seq * length
    shared = ck == PAIR
    has_sinks = sink_rows is not None
    qb = BWD_BLOCKS_PER_STEP
    n_steps = n_blocks // qb

    def body(*refs):
        if has_sinks:
            (q_ref, kc_ref, vc_ref, do_ref, lse_ref, dl_ref, sink_ref,
             dq_ref, dk_ref, dv_ref, dsink_ref, ck_scr, cv_scr, kp_ref, vp_ref, bias_ref) = refs
        else:
            (q_ref, kc_ref, vc_ref, do_ref, lse_ref, dl_ref,
             dq_ref, dk_ref, dv_ref, ck_scr, cv_scr, kp_ref, vp_ref, bias_ref) = refs
        step = pl.program_id(0)

        @pl.when(step == 0)
        def _():
            ck_scr[...] = jnp.zeros_like(ck_scr)
            cv_scr[...] = jnp.zeros_like(cv_scr)
            kp_ref[...] = jnp.zeros_like(kp_ref)
            vp_ref[...] = jnp.zeros_like(vp_ref)
            if has_sinks:
                dsink_ref[...] = jnp.zeros_like(dsink_ref)
            _fill_band_bias(bias_ref, max_dist)

        cols = [slice(p * PAIR, (p + 1) * PAIR) for p in range(N_PAIRS)]
        kcols = [slice(0, PAIR) if shared else c for c in cols]
        rows = [slice(b * BLOCK, (b + 1) * BLOCK) for b in range(qb)]
        units = [(b, p) for b in range(qb) for p in range(N_PAIRS)]
        n = range(len(units))
        nt = (((1,), (1,)), ((), ()))

        def window(prev_ref, cur_ref, b, kc):
            before = prev_ref[:, kc] if b == 0 else cur_ref[rows[b - 1], kc]
            return jnp.concatenate([before, cur_ref[rows[b], kc]], axis=0)

        q_st = [_stack_heads(q_ref[rows[b], cols[p]]) for b, p in units]
        do_st = [_stack_heads(do_ref[rows[b], cols[p]]) for b, p in units]
        k2 = [window(kp_ref, kc_ref, b, kcols[p]) for b, p in units]
        v2 = [window(vp_ref, vc_ref, b, kcols[p]) for b, p in units]
        st = [lax.dot_general(k2[u], q_st[u], nt, preferred_element_type=F32) for u in n]
        dpt = [lax.dot_general(v2[u], do_st[u], nt, preferred_element_type=F32) for u in n]
        lse_rows = [_stats_to_rows(lse_ref[rows[b], :]) for b in range(qb)]
        dl_rows = [_stats_to_rows(dl_ref[rows[b], :]) for b in range(qb)]
        lse_row = [lse_rows[b][p] for b, p in units]
        dl_row = [dl_rows[b][p] for b, p in units]
        pt = [jnp.exp2(st[u] + _band_bias(bias_ref, step, units[u][0], qb, nb) - lse_row[u]) for u in n]
        dst = [(pt[u] * (dpt[u] - dl_row[u])).astype(BF16) for u in n]
        ptb = [t.astype(BF16) for t in pt]
        dv2 = [jnp.dot(ptb[u], do_st[u], preferred_element_type=F32) for u in n]
        dk2 = [jnp.dot(dst[u], q_st[u], preferred_element_type=F32) for u in n]
        k2t = [k2[u].astype(F32).T.astype(BF16) for u in n]
        dqt = [jnp.dot(k2t[u], dst[u], preferred_element_type=F32) for u in n]
        for u, (b, p) in enumerate(units):
            dq_ref[rows[b], cols[p]] = _unstack_t(dqt[u]).astype(BF16)
        if has_sinks:
            for u, (b, p) in enumerate(units):
                p_sink = jnp.exp2(sink_ref[p:p + 1, :] - lse_row[u])
                dsink_ref[p:p + 1, :] = dsink_ref[p:p + 1, :] - p_sink * dl_row[u]

        def total(parts, w, group):
            sel = [u for u, (b, p) in enumerate(units) if (shared or p == group)]
            terms = ([parts[u][:BLOCK] for u in sel if units[u][0] == w]
                     + [parts[u][BLOCK:] for u in sel if units[u][0] == w - 1])
            tot = terms[0]
            for t in terms[1:]:
                tot = tot + t
            return tot

        first_row = step * (qb * BLOCK)
        for acc_ref, out_ref, parts in ((ck_scr, dk_ref, dk2), (cv_scr, dv_ref, dv2)):
            for group in range(1 if shared else N_PAIRS):
                kc = kcols[group]

                @pl.when(step > 0)
                def _():
                    out_ref[pl.ds(pl.multiple_of(first_row - BLOCK, BLOCK), BLOCK), kc] = (
                        acc_ref[:, kc] + total(parts, 0, group)).astype(BF16)

                for w in range(1, qb):
                    out_ref[pl.ds(pl.multiple_of(first_row + (w - 1) * BLOCK, BLOCK), BLOCK), kc] = (
                        total(parts, w, group).astype(BF16))
                acc_ref[:, kc] = total(parts, qb, group)

        @pl.when(step == n_steps - 1)
        def _():
            dk_ref[pl.ds(n_rows - BLOCK, BLOCK), :] = ck_scr[...].astype(BF16)
            dv_ref[pl.ds(n_rows - BLOCK, BLOCK), :] = cv_scr[...].astype(BF16)

        kp_ref[...] = kc_ref[rows[-1], :]
        vp_ref[...] = vc_ref[rows[-1], :]

    cur = lambda width: pl.BlockSpec((qb * BLOCK, width), lambda s: (s, 0))
    whole = lambda width: pl.BlockSpec((n_rows, width), lambda s: (0, 0))
    flat = lambda a: a.reshape(n_rows, a.shape[2])
    in_specs = [cur(HALF_WIDTH), cur(ck), cur(ck), cur(HALF_WIDTH), cur(STAT_WIDTH), cur(STAT_WIDTH)]
    args = [flat(a) for a in (q, k, v, d_o, lse, delta)]
    out_specs = [cur(HALF_WIDTH), whole(ck), whole(ck)]
    out_shape = [jax.ShapeDtypeStruct((n_rows, HALF_WIDTH), BF16),
                 jax.ShapeDtypeStruct((n_rows, ck), BF16), jax.ShapeDtypeStruct((n_rows, ck), BF16)]
    if has_sinks:
        in_specs.append(pl.BlockSpec(sink_rows.shape, lambda s: (0, 0)))
        args.append(sink_rows)
        out_specs.append(pl.BlockSpec(sink_rows.shape, lambda s: (0, 0)))
        out_shape.append(jax.ShapeDtypeStruct(sink_rows.shape, F32))
    outs = pl.pallas_call(
        body, name=name, grid=(n_steps,), in_specs=in_specs,
        out_specs=tuple(out_specs), out_shape=tuple(out_shape),
        scratch_shapes=[pltpu.VMEM((BLOCK, ck), F32), pltpu.VMEM((BLOCK, ck), F32),
                        pltpu.VMEM((BLOCK, ck), BF16), pltpu.VMEM((BLOCK, ck), BF16),
                        pltpu.VMEM((2, 2 * BLOCK, 2 * BLOCK), F32)],
        compiler_params=_params(("arbitrary",)),
    )(*args)
    return tuple(o.reshape(n_seq, length, o.shape[1]) for o in outs[:3]) + tuple(outs[3:])


def _tail(oa, ob1, lb1, ob4, lb4, ob16, lb16, gate_a, gate_b, x, target, w_out, spread, gather):
    tm = ROW_TILE

    def split_dot(v, mat):
        hi = v.astype(BF16)
        lo = (v - hi.astype(F32)).astype(BF16)
        return jnp.dot(hi, mat, preferred_element_type=F32) + jnp.dot(lo, mat, preferred_element_type=F32)

    def body(oa_ref, ob1_ref, lb1_ref, ob4_ref, lb4_ref, ob16_ref, lb16_ref, ga_ref, gb_ref, x_ref, t_ref, w_ref,
             sp_ref, ga_mat_ref,
             loss_ref, dy_ref, gwo_ref, doa_ref, dla_ref, dga_ref, dgb_ref,
             dob_ref, dob4_ref, dob16_ref, dlb_ref, dlb4_ref, dlb16_ref, lse_ref, lse4_ref, lse16_ref,
             s_f, mix_keep, dy_keep):
        i = pl.program_id(0)
        sp, gat = sp_ref[...], ga_mat_ref[...]
        o4, o16 = _unfold_load(ob4_ref, s_f, 4, tm), _unfold_load(ob16_ref, s_f, 16, tm)
        l4, l16 = _unfold_load(lb4_ref, s_f, 4, tm), _unfold_load(lb16_ref, s_f, 16, tm)
        o1, l1 = ob1_ref[...].astype(F32), lb1_ref[...]
        mx = jnp.maximum(jnp.maximum(l1, l4), l16)
        e1, e4, e16 = jnp.exp2(l1 - mx), jnp.exp2(l4 - mx), jnp.exp2(l16 - mx)
        den = e1 + e4 + e16
        inv = 1.0 / den
        ob = split_dot(e1 * inv, sp) * o1 + split_dot(e4 * inv, sp) * o4 + split_dot(e16 * inv, sp) * o16
        lse_b = mx + jnp.log2(den)

        oa, ga, gb = oa_ref[...].astype(F32), ga_ref[...].astype(F32), gb_ref[...].astype(F32)
        sa, sb = _sigmoid(ga), _sigmoid(gb)
        mixed = jnp.concatenate(_unpair_tiles(_tiles(oa * (ga * sa))) + [ob * (gb * sb)], axis=1)
        mixed_bf = mixed.astype(BF16)
        w = w_ref[...]
        yv = x_ref[...] + jnp.dot(mixed_bf, w, preferred_element_type=F32)
        err = yv - t_ref[...]
        sq = jnp.sum(err * err, axis=0, keepdims=True)
        dy = err * (1.0 / D_MODEL)
        dy_ref[...] = dy
        dy_bf = dy.astype(BF16)
        mix_t = mixed.T.astype(BF16)

        @pl.when(i == 0)
        def _():
            loss_ref[...] = sq

        @pl.when(i > 0)
        def _():
            loss_ref[...] += sq

        @pl.when((i & 1) == 0)
        def _():
            mix_keep[...] = mix_t
            dy_keep[...] = dy_bf

        @pl.when((i & 1) == 1)
        def _():
            gw = jnp.dot(jnp.concatenate([mix_keep[...], mix_t], axis=1), jnp.concatenate([dy_keep[...], dy_bf], axis=0),
                         preferred_element_type=F32)

            @pl.when(i == 1)
            def _():
                gwo_ref[...] = gw

            @pl.when(i > 1)
            def _():
                gwo_ref[...] += gw

        dmix = lax.dot_general(dy_bf, w, (((1,), (1,)), ((), ())), preferred_element_type=F32)
        dma = jnp.concatenate(_pair_tiles(_tiles(dmix[:, :HALF_WIDTH])), axis=1)
        dmb = dmix[:, HALF_WIDTH:]

        doa = dma * (ga * sa)
        doa_ref[...] = doa.astype(BF16)
        dla_ref[...] = split_dot(doa * oa, gat)
        dga_ref[...] = (dma * oa * (sa * (1.0 + ga * (1.0 - sa)))).astype(BF16)
        dob = dmb * (gb * sb)
        dgb_ref[...] = (dmb * ob * (sb * (1.0 + gb * (1.0 - sb)))).astype(BF16)
        dlb = split_dot(dob * ob, gat)
        dob_ref[...] = dob.astype(BF16)
        _fold_store(dob, s_f, dob4_ref, dob16_ref, tm)
        dlb_ref[...] = dlb
        _fold_store(dlb, s_f, dlb4_ref, dlb16_ref, tm)
        lse_ref[...] = lse_b
        _fold_store(lse_b, s_f, lse4_ref, lse16_ref, tm)

    row = lambda width: pl.BlockSpec((tm, width), lambda i: (i, 0))
    full = lambda a: pl.BlockSpec(a.shape, lambda i: (0,) * a.ndim)
    fb_shapes, fb_specs = _fold_specs(tm, BF16)
    _, ff_specs = _fold_specs(tm, F32)
    st_shapes, st_specs = _fold_specs(tm, F32, STAT_WIDTH)
    nat = lambda dtype, width=HALF_WIDTH: jax.ShapeDtypeStruct((SEQ, width), dtype)
    return pl.pallas_call(
        body, name="tail", grid=(SEQ // tm,),
        in_specs=[row(HALF_WIDTH), row(HALF_WIDTH), row(STAT_WIDTH), ff_specs[0], st_specs[0], ff_specs[1], st_specs[1],
                  row(HALF_WIDTH), row(HALF_WIDTH), row(D_MODEL), row(D_MODEL), full(w_out), full(spread), full(gather)],
        out_specs=(pl.BlockSpec((1, D_MODEL), lambda i: (0, 0)), row(D_MODEL),
                   pl.BlockSpec((D_MODEL, D_MODEL), lambda i: (0, 0)),
                   row(HALF_WIDTH), row(STAT_WIDTH), row(HALF_WIDTH), row(HALF_WIDTH),
                   row(HALF_WIDTH), *fb_specs, row(STAT_WIDTH), *st_specs, row(STAT_WIDTH), *st_specs),
        out_shape=(jax.ShapeDtypeStruct((1, D_MODEL), F32), jax.ShapeDtypeStruct((SEQ, D_MODEL), F32),
                   jax.ShapeDtypeStruct((D_MODEL, D_MODEL), F32),
                   nat(BF16), nat(F32, STAT_WIDTH), nat(BF16), nat(BF16),
                   nat(BF16), *fb_shapes, nat(F32, STAT_WIDTH), *st_shapes, nat(F32, STAT_WIDTH), *st_shapes),
        scratch_shapes=[_fold_scratch(tm), pltpu.VMEM((D_MODEL, tm), BF16), pltpu.VMEM((tm, D_MODEL), BF16)],
        compiler_params=_params(("arbitrary",)),
    )(oa, ob1, lb1, ob4, lb4, ob16, lb16, gate_a, gate_b, x, target, w_out, spread, gather)


def _dproj_assemble(dqa, dka, dva, dga, dgb, dq1, dk1, dv1, dq4, dk4, dv4, dq16, dk16, dv16, tqa, tqb, tkb, tka,
                    qkg, cos4, sin4, bmean):
    tm = ROW_TILE

    def norm_rope_bwd(d_out, t, g, cos, sin, bm, scale):
        d_r = d_out * scale
        dyv = d_r * cos + _swap_halves(d_r * sin)
        rr = lax.rsqrt(_head_sum(t * t, bm) + EPS)
        that = t * rr
        dgain = jnp.sum(dyv * that, axis=0, keepdims=True)
        gdy = dyv * g
        dt = rr * (gdy - that * _head_sum(that * gdy, bm))
        return dt, dgain

    def body(dqa_ref, dka_ref, dva_ref, dga_ref, dgb_ref, dq1_ref, dk1_ref, dv1_ref, dq4_ref, dk4_ref, dv4_ref,
             dq16_ref, dk16_ref, dv16_ref, tqa_ref, tqb_ref, tkb_ref, tka_ref, qkg_ref, cos_ref, sin_ref, bm_ref,
             dproj_ref, dqkg_ref, s_f):
        i = pl.program_id(0)
        cos, sin, bm = cos_ref[...], sin_ref[...], bm_ref[...]

        def merged(nat_ref, f4_ref, f16_ref):
            return nat_ref[...].astype(F32) + _unfold_load(f4_ref, s_f, 4, tm) + _unfold_load(f16_ref, s_f, 16, tm)

        @pl.when(i == 0)
        def _():
            dqkg_ref[...] = jnp.zeros_like(dqkg_ref)

        def through(d_out, t, row, scale, c0, paired=False):
            g = qkg_ref[row:row + 1, :]
            tot = jnp.zeros((1, PAIR), F32)
            dts = []
            for j in range(d_out.shape[1] // PAIR):
                cols = slice(j * PAIR, (j + 1) * PAIR)
                dt, dg = norm_rope_bwd(d_out[:, cols], t[:, cols], g, cos, sin, bm, scale)
                dts.append(dt)
                tot = tot + dg
            if paired:
                dts = _unpair_tiles(dts)
            for j, dt in enumerate(dts):
                dproj_ref[:, c0 + j * PAIR:c0 + (j + 1) * PAIR] = dt.astype(BF16)
            dqkg_ref[row:row + 1, :] += tot

        through(dqa_ref[...].astype(F32), tqa_ref[...].astype(F32), 0, HEAD_DIM ** -0.5, C_QA, paired=True)
        through(dka_ref[...].astype(F32), tka_ref[...].astype(F32), 1, LN2, C_KA)
        through(merged(dq1_ref, dq4_ref, dq16_ref), tqb_ref[...].astype(F32), 2, HEAD_DIM ** -0.5, C_QB)
        through(merged(dk1_ref, dk4_ref, dk16_ref), tkb_ref[...].astype(F32), 3, LN2, C_KB)
        dproj_ref[:, C_VB:C_VB + HALF_WIDTH] = merged(dv1_ref, dv4_ref, dv16_ref).astype(BF16)
        dproj_ref[:, C_GA:C_GA + HALF_WIDTH] = jnp.concatenate(
            _unpair_tiles(_tiles(dga_ref[...].astype(F32))), axis=1).astype(BF16)
        dproj_ref[:, C_GB:C_GB + HALF_WIDTH] = dgb_ref[...].astype(BF16)
        dproj_ref[:, C_VA:C_VA + KV_A_WIDTH] = dva_ref[...].astype(BF16)

    row = lambda width: pl.BlockSpec((tm, width), lambda i: (i, 0))
    full = lambda a: pl.BlockSpec(a.shape, lambda i: (0,) * a.ndim)
    _, ff_specs = _fold_specs(tm, F32)
    return pl.pallas_call(
        body, name="dproj_assemble", grid=(SEQ // tm,),
        in_specs=[row(HALF_WIDTH), row(KV_A_WIDTH), row(KV_A_WIDTH), row(HALF_WIDTH), row(HALF_WIDTH),
                  row(HALF_WIDTH), row(HALF_WIDTH), row(HALF_WIDTH), ff_specs[0], ff_specs[0], ff_specs[0],
                  ff_specs[1], ff_specs[1], ff_specs[1],
                  row(HALF_WIDTH), row(HALF_WIDTH), row(HALF_WIDTH), row(KV_A_WIDTH),
                  full(qkg), row(PAIR), row(PAIR), full(bmean)],
        out_specs=(row(IN_WIDTH), pl.BlockSpec((SMALL_ROWS, PAIR), lambda i: (0, 0))),
        out_shape=(jax.ShapeDtypeStruct((SEQ, IN_WIDTH), BF16), jax.ShapeDtypeStruct((SMALL_ROWS, PAIR), F32)),
        scratch_shapes=[_fold_scratch(tm)],
        compiler_params=_params(("arbitrary",)),
    )(dqa, dka, dva, dga, dgb, dq1, dk1, dv1, dq4, dk4, dv4, dq16, dk16, dv16, tqa, tqb, tkb, tka, qkg, cos4, sin4, bmean)


def _input_grad_reduce(dproj, w, x, gain, dy, blocks_in, blocks_out, small):
    tm = ROW_TILE
    n_steps = SEQ // tm
    stage2_step, stage3_step = 3, 7
    shapes = (blocks_in.shape[1:], blocks_out.shape[1:])

    def body(dp_ref, w_ref, x_ref, g_ref, dy_ref, ga_hbm, gb_hbm, small_ref,
             gx_ref, out_a, out_b, small_out_ref, dgain_out_ref,
             part_a, part_b, sib_a, sib_b, wire_a, wire_b, chips_a, chips_b, small_all, dgain_acc, dgain_all,
             load_sems, sib_send, sib_recv, chip_send, chip_recv, small_send, small_recv, dgain_send, dgain_recv):
        i = pl.program_id(0)
        x, y, c = lax.axis_index("x"), lax.axis_index("y"), lax.axis_index("c")
        sibling = (x, y, 1 - c)
        chips = [(x, y), (1 - x, y), (x, 1 - y), (1 - x, 1 - y)]
        my_id = 4 * x + 2 * y + c
        g_hbm, part, from_sib = (ga_hbm, gb_hbm), (part_a, part_b), (sib_a, sib_b)
        to_wire, from_chips, out = (wire_a, wire_b), (chips_a, chips_b), (out_a, out_b)
        both = (0, 1)

        def blk(a, chip, core):
            return g_hbm[a].at[4 * chip[0] + 2 * chip[1] + core]

        def to_all(src, dst_all, send, recv):
            copies = []
            for rel in range(1, N_DEV):
                dx, dy_, dc = (rel >> 2) & 1, (rel >> 1) & 1, rel & 1
                to = (1 - x if dx else x, 1 - y if dy_ else y, 1 - c if dc else c)
                copies.append(pltpu.make_async_remote_copy(
                    src_ref=src, dst_ref=dst_all.at[my_id], send_sem=send.at[rel - 1], recv_sem=recv.at[rel - 1],
                    device_id=to, device_id_type=MESH))
            return copies

        small_copies = to_all(small_all.at[my_id], small_all, small_send, small_recv)
        dgain_copies = to_all(dgain_acc, dgain_all, dgain_send, dgain_recv)
        loads = [[pltpu.make_async_copy(blk(a, chips[k], c), part[a].at[k], load_sems.at[a, k]) for k in range(4)] for a in both]
        to_sib = [[pltpu.make_async_remote_copy(
            src_ref=blk(a, chips[k], 1 - c), dst_ref=from_sib[a].at[k], send_sem=sib_send.at[a, k], recv_sem=sib_recv.at[a, k],
            device_id=sibling, device_id_type=MESH) for k in range(4)] for a in both]
        first, other, k_first, k_other = _routes()
        to_chips = [[pltpu.make_async_remote_copy(
            src_ref=to_wire[a].at[s], dst_ref=from_chips[a].at[s], send_sem=chip_send.at[a, s], recv_sem=chip_recv.at[a, s],
            device_id=(first, first, other)[s], device_id_type=MESH) for s in range(3)] for a in both]

        def chip_partial(a, k):
            return part[a][k].astype(F32) + from_sib[a][k].astype(F32)

        @pl.when(i == 0)
        def _():
            small_all[my_id] = small_ref[...]
            for cp in small_copies:
                cp.start()
            for k in (1, 2, 3, 0):
                for a in both:
                    loads[a][k].start()
                    to_sib[a][k].start()

        @pl.when(i == stage2_step)
        def _():
            for k in (1, 2, 3):
                for a in both:
                    loads[a][k].wait()
                    to_sib[a][k].wait_recv()
            for s, k in ((0, 3), (1, k_first)):
                for a in both:
                    to_wire[a][s] = chip_partial(a, k).astype(BF16)
                    to_chips[a][s].start()

        @pl.when(i == stage3_step)
        def _():
            for a in both:
                to_chips[a][0].wait_recv()
                to_wire[a][2] = (chip_partial(a, k_other) + from_chips[a][0].astype(F32)).astype(BF16)
                to_chips[a][2].start()

        dh = lax.dot_general(dp_ref[...], w_ref[...], (((1,), (1,)), ((), ())), preferred_element_type=F32)
        xf = x_ref[...]
        r = lax.rsqrt(jnp.mean(xf * xf, axis=-1, keepdims=True) + EPS)
        xhat = xf * r
        dg = jnp.sum(dh * xhat, axis=0, keepdims=True)
        dxh = dh * g_ref[...]
        dx = r * (dxh - xhat * jnp.mean(dxh * xhat, axis=-1, keepdims=True))
        gx_ref[...] = dy_ref[...] + dx

        @pl.when(i == 0)
        def _():
            dgain_acc[...] = dg

        @pl.when(i > 0)
        def _():
            dgain_acc[...] += dg

        @pl.when(i == n_steps - 1)
        def _():
            dgain_all[my_id] = dgain_acc[...]
            for cp in dgain_copies:
                cp.start()
            for a in both:
                loads[a][0].wait()
                to_sib[a][0].wait_recv()
                acc = chip_partial(a, 0)
                for s in (1, 2):
                    to_chips[a][s].wait_recv()
                    acc = acc + from_chips[a][s].astype(F32)
                out[a][...] = acc
            for copies, gathered, dst in ((small_copies, small_all, small_out_ref), (dgain_copies, dgain_all, dgain_out_ref)):
                for cp in copies:
                    cp.wait_recv()
                tot = gathered[0]
                for d in range(1, N_DEV):
                    tot = tot + gathered[d]
                dst[...] = tot
            for cp in to_sib[0] + to_sib[1] + to_chips[0] + to_chips[1] + small_copies + dgain_copies:
                cp.wait_send()

    row = lambda width: pl.BlockSpec((tm, width), lambda i: (i, 0))
    full = lambda a: pl.BlockSpec(a.shape, lambda i: (0,) * a.ndim)
    whole = lambda shape: pl.BlockSpec(shape, lambda i: (0,) * len(shape))
    hbm = pl.BlockSpec(memory_space=pl.ANY)
    dtypes = (blocks_in.dtype, blocks_out.dtype)
    buf = lambda n, dts: [pltpu.VMEM((n,) + s, dt) for s, dt in zip(shapes, dts)]
    return pl.pallas_call(
        body, name="input_grad_rs", grid=(n_steps,),
        in_specs=[row(IN_WIDTH), full(w), row(D_MODEL), full(gain), row(D_MODEL), hbm, hbm, full(small)],
        out_specs=(row(D_MODEL), whole(shapes[0]), whole(shapes[1]), whole((SMALL_ROWS, SMALL_COLS)), whole((1, D_MODEL))),
        out_shape=(jax.ShapeDtypeStruct((SEQ, D_MODEL), F32), jax.ShapeDtypeStruct(shapes[0], F32),
                   jax.ShapeDtypeStruct(shapes[1], F32), jax.ShapeDtypeStruct((SMALL_ROWS, SMALL_COLS), F32),
                   jax.ShapeDtypeStruct((1, D_MODEL), F32)),
        scratch_shapes=[*buf(4, dtypes), *buf(4, dtypes), *buf(3, (BF16, BF16)), *buf(3, (BF16, BF16)),
                        pltpu.VMEM((N_DEV, SMALL_ROWS, SMALL_COLS), F32),
                        pltpu.VMEM((1, D_MODEL), F32), pltpu.VMEM((N_DEV, 1, D_MODEL), F32),
                        pltpu.SemaphoreType.DMA((2, 4)), pltpu.SemaphoreType.DMA((2, 4)), pltpu.SemaphoreType.DMA((2, 4)),
                        pltpu.SemaphoreType.DMA((2, 3)), pltpu.SemaphoreType.DMA((2, 3)),
                        pltpu.SemaphoreType.DMA((7,)), pltpu.SemaphoreType.DMA((7,)),
                        pltpu.SemaphoreType.DMA((7,)), pltpu.SemaphoreType.DMA((7,))],
        compiler_params=_params(("arbitrary",)),
    )(dproj, w, x, gain, dy, blocks_in, blocks_out, small)


def _weight_grad(h_t, dproj):
    tk = 1024
    cb = IN_WIDTH // 2
    n_k = SEQ // tk

    def body(ht_ref, dp_ref, out_ref, acc):
        k = pl.program_id(1)
        upd = jnp.dot(ht_ref[...], dp_ref[...], preferred_element_type=F32)

        @pl.when(k == 0)
        def _():
            acc[...] = upd

        @pl.when(k > 0)
        def _():
            acc[...] += upd

        @pl.when(k == n_k - 1)
        def _():
            for b in range(N_DEV // 2):
                out_ref[b] = acc[:, b * SHARD_IN:(b + 1) * SHARD_IN].astype(BF16)

    return pl.pallas_call(
        body, name="weight_grad", grid=(2, n_k),
        in_specs=[pl.BlockSpec((D_MODEL, tk), lambda j, k: (0, k)), pl.BlockSpec((tk, cb), lambda j, k: (k, j))],
        out_specs=pl.BlockSpec((N_DEV // 2, D_MODEL, SHARD_IN), lambda j, k: (j, 0, 0)),
        out_shape=jax.ShapeDtypeStruct((N_DEV, D_MODEL, SHARD_IN), BF16),
        scratch_shapes=[pltpu.VMEM((D_MODEL, cb), F32)],
        compiler_params=_params(("arbitrary", "arbitrary")),
    )(h_t, dproj)


def _adamw_update(w, g, m, v):
    nm = ADAM_B1 * m + (1.0 - ADAM_B1) * g
    nv = ADAM_B2 * v + (1.0 - ADAM_B2) * jnp.square(g)
    m_hat = nm / (1.0 - ADAM_B1 ** ADAM_STEP)
    v_hat = nv / (1.0 - ADAM_B2 ** ADAM_STEP)
    return -ADAM_LR * (m_hat / (jnp.sqrt(v_hat) + ADAM_EPS) + ADAM_WD * w), nm, nv


def _adamw(transposed, groups, small):
    params, ms, vs, packed_grads, gain_grad = small
    n_big, n_small = 4 * (len(groups) + 1), len(params)

    def body(*refs):
        n_in = n_big + 3 * n_small + 2
        ins, outs = refs[:n_in], refs[n_in:]
        p_refs, m_refs, v_refs = (ins[n_big + k * n_small:n_big + (k + 1) * n_small] for k in range(3))
        packed_ref, gain_grad_ref = ins[n_big + 3 * n_small:]
        small_outs = outs[4 + 3 * len(groups):]

        @pl.when(pl.program_id(0) == 0)
        def _():
            for k, window in enumerate(SMALL_WINDOWS):
                gv = gain_grad_ref[...] if window is None else packed_ref[window]
                results = (gv, *_adamw_update(p_refs[k][...], gv, m_refs[k][...], v_refs[k][...]))
                for out_ref, value in zip(small_outs[4 * k:4 * k + 4], results):
                    out_ref[...] = value

        wt_ref, g_ref, mt_ref, vt_ref = ins[:4]
        gt_ref, dt_ref, nmt_ref, nvt_ref = outs[:4]
        cols = g_ref.shape[1]
        for off in [*range(0, cols - LANES, LANES), cols - LANES]:
            band = slice(off, off + LANES)
            gv = g_ref[:, band].T
            gt_ref[band, :] = gv
            dt_ref[band, :], nmt_ref[band, :], nvt_ref[band, :] = _adamw_update(
                wt_ref[band, :], gv, mt_ref[band, :], vt_ref[band, :])
        for i in range(len(groups)):
            w_ref, g_ref, m_ref, v_ref = ins[4 * i + 4:4 * i + 8]
            d_ref, nm_ref, nv_ref = outs[3 * i + 4:3 * i + 7]
            d_ref[...], nm_ref[...], nv_ref[...] = _adamw_update(w_ref[...], g_ref[...], m_ref[...], v_ref[...])

    col_band = lambda a: pl.BlockSpec((a.shape[0], a.shape[1] // ADAMW_STEPS), lambda i: (0, i))
    row_band = lambda a: pl.BlockSpec((a.shape[0] // ADAMW_STEPS, a.shape[1]), lambda i: (i, 0))
    whole = lambda a: pl.BlockSpec(a.shape, lambda i: (0, 0))
    w_t, g, m_t, v_t = transposed
    in_specs = [col_band(w_t), row_band(g), col_band(m_t), col_band(v_t)]
    in_specs += [col_band(a) for group in groups for a in group]
    in_specs += [whole(a) for a in (*params, *ms, *vs, packed_grads, gain_grad)]
    out_shape = [jax.ShapeDtypeStruct(w_t.shape, F32)] * 4
    out_shape += [jax.ShapeDtypeStruct(w.shape, F32) for w, _, _, _ in groups for _ in range(3)]
    out_specs = [col_band(a) for a in out_shape]
    out_shape += [jax.ShapeDtypeStruct(p.shape, F32) for p in params for _ in range(4)]
    out_specs += [whole(a) for a in out_shape[len(out_specs):]]
    outs = pl.pallas_call(
        body, name="adamw", grid=(ADAMW_STEPS,), in_specs=in_specs, out_specs=tuple(out_specs),
        out_shape=tuple(out_shape), compiler_params=_params(("arbitrary",)),
    )(*transposed, *[a for group in groups for a in group], *params, *ms, *vs, packed_grads, gain_grad)
    first_small = 4 + 3 * len(groups)
    return (tuple(outs[:4]), [tuple(outs[3 * i + 4:3 * i + 7]) for i in range(len(groups))],
            [tuple(outs[first_small + 4 * k:first_small + 4 * k + 4]) for k in range(n_small)])


SMALL_USED = D_MODEL + 4 * HEAD_DIM + 8


def _pack_small_grads(qa, ka, qb, kb, sinks, loss):
    flat = jnp.concatenate([jnp.zeros((D_MODEL,), F32), qa, ka, qb, kb, sinks, loss.reshape(1)])
    flat = jnp.pad(flat, (0, SMALL_ROWS * SMALL_COLS - flat.shape[0]))
    return flat.reshape(SMALL_ROWS, SMALL_COLS)


_GAINS_ROW, _SINKS_ROW = D_MODEL // SMALL_COLS, D_MODEL // SMALL_COLS + 1
_head_window = lambda k: (slice(_GAINS_ROW, _GAINS_ROW + 1), slice(k * HEAD_DIM, (k + 1) * HEAD_DIM))
SMALL_WINDOWS = (None, _head_window(0), _head_window(1), (slice(_SINKS_ROW, _SINKS_ROW + 1), slice(0, 8)),
                 _head_window(2), _head_window(3))


def _fold_heads(row):
    return row[0, :HEAD_DIM] + row[0, HEAD_DIM:]


def kernel(x, norm_gain, w_in, q_norm_a, k_norm_a, sinks_a, q_norm_b, k_norm_b, w_out, loss_target, m_norm_gain, m_w_in, m_q_norm_a, m_k_norm_a, m_sinks_a, m_q_norm_b, m_k_norm_b, m_w_out, v_norm_gain, v_w_in, v_q_norm_a, v_k_norm_a, v_sinks_a, v_q_norm_b, v_k_norm_b, v_w_out):
    x2, tgt = x[0], loss_target[0]
    w_out_sh = w_out[0]
    as_held = lambda a: jnp.swapaxes(a, 1, 2)[0]

    w_full = _all_gather_w_in(as_held(w_in))

    inv = np.float32(ROPE_THETA) ** (-np.arange(HEAD_DIM // 2, dtype=np.float32) / np.float32(HEAD_DIM // 2))
    ang = np.arange(SEQ, dtype=np.float32)[:, None] * inv[None, :].astype(np.float32)
    cos, sin = np.cos(ang).astype(np.float32), np.sin(ang).astype(np.float32)
    cos4 = jnp.asarray(np.concatenate([cos, cos, cos, cos], axis=1))
    sin4 = jnp.asarray(np.concatenate([-sin, sin, -sin, sin], axis=1))
    blockdiag = np.kron(np.eye(2, dtype=np.float32), np.ones((HEAD_DIM, HEAD_DIM), np.float32))
    bmean = jnp.asarray(blockdiag / HEAD_DIM, dtype=BF16)
    gather_np = np.kron(np.eye(2 * N_PAIRS, dtype=np.float32), np.ones((HEAD_DIM, STAT_REP), np.float32))
    spread_np = np.kron(np.eye(2 * N_PAIRS, dtype=np.float32), np.ones((STAT_REP, HEAD_DIM), np.float32))
    spread_np[np.arange(STAT_WIDTH) % STAT_REP != 0] = 0.0
    gather, spread = jnp.asarray(gather_np, dtype=BF16), jnp.asarray(spread_np, dtype=BF16)
    two = lambda g: jnp.concatenate([g, g], axis=1)
    qkg = jnp.concatenate([two(q_norm_a), two(k_norm_a), two(q_norm_b), two(k_norm_b),
                           jnp.zeros((SMALL_ROWS - 4, PAIR), F32)], axis=0)
    sinks_paired = jnp.stack([sinks_a[0, :N_PAIRS], sinks_a[0, N_PAIRS:]], axis=1) * LOG2E
    sink_rows = jnp.concatenate([jnp.repeat(sinks_paired, BLOCK, axis=1),
                                 jnp.zeros((SMALL_ROWS - N_PAIRS, 2 * BLOCK), F32)], axis=0)

    (tqa, tka, tqb, tkb, gate_a, gate_b, h_t, qa, ka, va, qb, kb, vb, qb4, qb16, kb4, kb16, vb4, vb16,
     gathered_out) = _proj_fwd(x2, norm_gain, w_full, qkg, cos4, sin4, bmean, w_out_sh)
    wo_full = gathered_out.reshape(D_MODEL, D_MODEL)
    (oa, la), (ob1, lb1), (ob4, lb4), (ob16, lb16) = _attn_fwd("attn_fwd", [
        (qa[None], ka[None], va[None], sink_rows, BLOCK - 1), (qb[None], kb[None], vb[None], None, BLOCK),
        (qb4, kb4, vb4, None, BLOCK), (qb16, kb16, vb16, None, BLOCK)])
    (loss_cols, dy, gwo, doa, dla, dga, dgb, dob, dob4, dob16, dlb, dlb4, dlb16, lse_b, lse4, lse16) = _tail(
        oa[0], ob1[0], lb1[0], ob4, lb4, ob16, lb16, gate_a, gate_b, x2, tgt, wo_full, spread, gather)

    dqa, dka, dva, dsink = _attn_bwd("attn_a_bwd", qa[None], ka[None], va[None], doa[None], la, dla[None], sink_rows, BLOCK - 1)
    dq1, dk1, dv1 = _attn_bwd("attn_b1_bwd", qb[None], kb[None], vb[None], dob[None], lse_b[None], dlb[None], None, BLOCK)
    dq4, dk4, dv4 = _attn_bwd("attn_b4_bwd", qb4, kb4, vb4, dob4, lse4, dlb4, None, BLOCK)
    dq16, dk16, dv16 = _attn_bwd("attn_b16_bwd", qb16, kb16, vb16, dob16, lse16, dlb16, None, BLOCK)
    dproj, dqkg = _dproj_assemble(dqa[0], dka[0], dva[0], dga, dgb, dq1[0], dk1[0], dv1[0], dq4, dk4, dv4,
                                  dq16, dk16, dv16, tqa, tqb, tkb, tka, qkg, cos4, sin4, bmean)
    gw_in = _weight_grad(h_t, dproj)

    blocks_in = gw_in
    blocks_out = gwo.reshape(N_DEV, SHARD_OUT, D_MODEL)
    g_sinks = jnp.concatenate([jnp.sum(dsink[:N_PAIRS, :BLOCK], axis=1), jnp.sum(dsink[:N_PAIRS, BLOCK:], axis=1)])
    small = _pack_small_grads(_fold_heads(dqkg[0:1]), _fold_heads(dqkg[1:2]), _fold_heads(dqkg[2:3]),
                              _fold_heads(dqkg[3:4]), g_sinks, 0.5 * jnp.sum(loss_cols) / D_MODEL)
    grad_x, g_w_in, g_w_out, small_red, dgain_red = _input_grad_reduce(
        dproj, w_full, x2, norm_gain, dy, blocks_in, blocks_out, small)

    w_in_results, [(d_out, nm_out, nv_out)], small_results = _adamw(
        (as_held(w_in), g_w_in, as_held(m_w_in), as_held(v_w_in)),
        [(w_out_sh, g_w_out, m_w_out[0], v_w_out[0])],
        ([norm_gain, q_norm_a, k_norm_a, sinks_a, q_norm_b, k_norm_b],
         [m_norm_gain, m_q_norm_a, m_k_norm_a, m_sinks_a, m_q_norm_b, m_k_norm_b],
         [v_norm_gain, v_q_norm_a, v_k_norm_a, v_sinks_a, v_q_norm_b, v_k_norm_b], small_red, dgain_red))
    g_small, d_small, nm_small, nv_small = ([leaf[j] for leaf in small_results] for j in range(4))
    g_in, d_in, nm_in, nv_in = [jnp.swapaxes(a, 0, 1) for a in w_in_results]

    loss = small_red.reshape(-1)[SMALL_USED]

    def assemble(small_list, big_in, big_out):
        ng, qa_, ka_, sk_, qb_, kb_ = small_list
        return [ng, big_in[None], qa_, ka_, sk_, qb_, kb_, big_out[None]]

    return (loss, grad_x[None], *assemble(g_small, g_in, g_w_out), *assemble(d_small, d_in, d_out),
            *assemble(nm_small, nm_in, nm_out), *assemble(nv_small, nv_in, nv_out))
```
